```python
import math
import jax, jax.numpy as jnp
from jax import lax
import numpy as np

D_MODEL = 2048
BATCH = 8
SEQ = 4096
DEPTH = 1

SSM_WIDTH = 1024
SSM_GROUP = 16
SSM_GROUPS = SSM_WIDTH // SSM_GROUP
SSM_STATE = 64
CONV_WIDTH = 1024
CONV_GROUP = 64
CONV_K = 3
D_FF = 5632
EPS = 1e-6
DT_MIN = 1e-3
DT_MAX = 1e-1
IN_COLS = SSM_WIDTH + 3 * CONV_WIDTH + 2 * D_MODEL

kernel_name = "hybrid_s5_shortconv_gated_block"


def rmsnorm(x, g):
    xf = x.astype(jnp.float32)
    y = xf * lax.rsqrt(jnp.mean(xf * xf, axis=-1, keepdims=True) + EPS)
    return (y * g.astype(jnp.float32)).astype(x.dtype)


def causal_dwconv(x, w, b):
    c = x.shape[-1]
    y = lax.conv_general_dilated(
        x, w[:, None, :].astype(x.dtype), window_strides=(1,),
        padding=((CONV_K - 1, 0),), dimension_numbers=("NWC", "WIO", "NWC"),
        feature_group_count=c)
    return y + b.astype(x.dtype)


def s5_scan(u, a_re, a_im, log_dt, b_re, b_im, c_re, c_im, d_skip):
    bsz, seq_len, _ = u.shape
    f32 = jnp.float32
    uf = u.astype(f32).reshape(bsz, seq_len, SSM_GROUPS, SSM_GROUP)
    ar = a_re.astype(f32)
    ai = a_im.astype(f32)
    dt = jnp.exp(log_dt.astype(f32))[:, None]
    mag = jnp.exp(dt * ar)
    abar_re = mag * jnp.cos(dt * ai)
    abar_im = mag * jnp.sin(dt * ai)
    nr = abar_re - 1.0
    ni = abar_im
    den = ar * ar + ai * ai
    fr = (nr * ar + ni * ai) / den
    fi = (ni * ar - nr * ai) / den
    br = b_re.astype(f32)
    bi = b_im.astype(f32)
    bbar_re = fr[..., None] * br - fi[..., None] * bi
    bbar_im = fr[..., None] * bi + fi[..., None] * br
    bu_re = jnp.einsum("blgh,gph->blgp", uf, bbar_re)
    bu_im = jnp.einsum("blgh,gph->blgp", uf, bbar_im)
    a_seq_re = jnp.broadcast_to(abar_re[None, None], (1, seq_len, SSM_GROUPS, SSM_STATE))
    a_seq_im = jnp.broadcast_to(abar_im[None, None], (1, seq_len, SSM_GROUPS, SSM_STATE))

    def combine(e1, e2):
        a1r, a1i, b1r, b1i = e1
        a2r, a2i, b2r, b2i = e2
        return (a2r * a1r - a2i * a1i,
                a2r * a1i + a2i * a1r,
                a2r * b1r - a2i * b1i + b2r,
                a2r * b1i + a2i * b1r + b2i)

    _, _, xr, xi = lax.associative_scan(combine, (a_seq_re, a_seq_im, bu_re, bu_im), axis=1)
    y = (jnp.einsum("blgp,ghp->blgh", xr, c_re.astype(f32))
         - jnp.einsum("blgp,ghp->blgh", xi, c_im.astype(f32))
         + d_skip.astype(f32) * uf)
    return y.reshape(bsz, seq_len, SSM_WIDTH).astype(u.dtype)


def token_mixer(xn, w_in, a_re, a_im, log_dt, b_re, b_im, c_re, c_im, d_skip,
                w_glu, w_ssm_out, conv_w, conv_b, w_conv_out, w_o):
    proj = xn @ w_in
    s0 = SSM_WIDTH
    s1 = s0 + CONV_WIDTH
    s2 = s1 + CONV_WIDTH
    s3 = s2 + CONV_WIDTH
    s4 = s3 + D_MODEL
    u = proj[..., :s0]
    v = proj[..., s0:s1]
    gate_b = proj[..., s1:s2]
    gate_c = proj[..., s2:s3]
    merge_a = proj[..., s3:s4]
    merge_b = proj[..., s4:]
    ya = jax.nn.gelu(s5_scan(u, a_re, a_im, log_dt, b_re, b_im, c_re, c_im, d_skip))
    ya = ya * jax.nn.sigmoid(ya @ w_glu)
    ya = ya @ w_ssm_out
    yb = (gate_b * causal_dwconv(gate_c * v, conv_w, conv_b)) @ w_conv_out
    merged = jax.nn.sigmoid(merge_a) * ya + jax.nn.sigmoid(merge_b) * yb
    return merged @ w_o


def conv_ffn(xn, w_up, ffn_conv_w, ffn_conv_b, w_down):
    h = xn @ w_up
    a = causal_dwconv(h[..., :D_FF], ffn_conv_w, ffn_conv_b)
    return (jax.nn.gelu(a) * h[..., D_FF:]) @ w_down


def _fwd_setup_inputs(seed: int = 0) -> dict:
    key = jax.random.key(seed)
    ks = jax.random.split(key, 24)
    f32 = jnp.float32
    nrm = lambda k, s, sc: jax.random.normal(k, s, f32) * sc
    n_idx = jnp.arange(SSM_STATE, dtype=f32)
    a_re = -0.5 * jnp.exp(nrm(ks[3], (DEPTH, SSM_GROUPS, SSM_STATE), 0.05))
    a_im = math.pi * n_idx[None, None, :] + nrm(ks[4], (DEPTH, SSM_GROUPS, SSM_STATE), 0.05)
    log_dt = jax.random.uniform(ks[5], (DEPTH, SSM_GROUPS), f32, math.log(DT_MIN), math.log(DT_MAX))
    return {
        "x": nrm(ks[0], (BATCH, SEQ, D_MODEL), 1.0),
        "norm_tok": 1.0 + nrm(ks[1], (DEPTH, D_MODEL), 0.01),
        "w_in": nrm(ks[2], (DEPTH, D_MODEL, IN_COLS), D_MODEL ** -0.5),
        "a_re": a_re,
        "a_im": a_im,
        "log_dt": log_dt,
        "b_re": nrm(ks[6], (DEPTH, SSM_GROUPS, SSM_STATE, SSM_GROUP), (2 * SSM_GROUP) ** -0.5),
        "b_im": nrm(ks[7], (DEPTH, SSM_GROUPS, SSM_STATE, SSM_GROUP), (2 * SSM_GROUP) ** -0.5),
        "c_re": nrm(ks[8], (DEPTH, SSM_GROUPS, SSM_GROUP, SSM_STATE), SSM_STATE ** -0.5),
        "c_im": nrm(ks[9], (DEPTH, SSM_GROUPS, SSM_GROUP, SSM_STATE), SSM_STATE ** -0.5),
        "d_skip": nrm(ks[10], (DEPTH, SSM_GROUPS, SSM_GROUP), 1.0),
        "w_glu": nrm(ks[11], (DEPTH, SSM_WIDTH, SSM_WIDTH), SSM_WIDTH ** -0.5),
        "w_ssm_out": nrm(ks[12], (DEPTH, SSM_WIDTH, D_MODEL), SSM_WIDTH ** -0.5),
        "conv_w": nrm(ks[13], (DEPTH, CONV_K, CONV_WIDTH), CONV_K ** -0.5),
        "conv_b": nrm(ks[14], (DEPTH, CONV_WIDTH), 0.01),
        "w_conv_out": nrm(ks[15], (DEPTH, CONV_WIDTH, D_MODEL), CONV_WIDTH ** -0.5),
        "w_o": nrm(ks[16], (DEPTH, D_MODEL, D_MODEL), D_MODEL ** -0.5),
        "norm_ffn": 1.0 + nrm(ks[17], (DEPTH, D_MODEL), 0.01),
        "w_up": nrm(ks[18], (DEPTH, D_MODEL, 2 * D_FF), D_MODEL ** -0.5),
        "ffn_conv_w": nrm(ks[19], (DEPTH, CONV_K, D_FF), CONV_K ** -0.5),
        "ffn_conv_b": nrm(ks[20], (DEPTH, D_FF), 0.01),
        "w_down": nrm(ks[21], (DEPTH, D_FF, D_MODEL), D_FF ** -0.5),
        "norm_final": 1.0 + nrm(ks[22], (D_MODEL,), 0.01),
    }


def _fwd_reference(x, norm_tok, w_in, a_re, a_im, log_dt, b_re, b_im, c_re, c_im, d_skip,
              w_glu, w_ssm_out, conv_w, conv_b, w_conv_out, w_o,
              norm_ffn, w_up, ffn_conv_w, ffn_conv_b, w_down, norm_final):
    h = x
    for l in range(DEPTH):
        h = h + token_mixer(rmsnorm(h, norm_tok[l]), w_in[l], a_re[l], a_im[l], log_dt[l],
                            b_re[l], b_im[l], c_re[l], c_im[l], d_skip[l],
                            w_glu[l], w_ssm_out[l], conv_w[l], conv_b[l], w_conv_out[l], w_o[l])
        h = h + conv_ffn(rmsnorm(h, norm_ffn[l]), w_up[l], ffn_conv_w[l], ffn_conv_b[l], w_down[l])
    return rmsnorm(h, norm_final)


import jax as _jax
import jax.numpy as _jnp

TWIN_FORMAT = 'train_step'
FWD_PARAMS = ['x', 'norm_tok', 'w_in', 'a_re', 'a_im', 'log_dt', 'b_re', 'b_im', 'c_re', 'c_im', 'd_skip', 'w_glu', 'w_ssm_out', 'conv_w', 'conv_b', 'w_conv_out', 'w_o', 'norm_ffn', 'w_up', 'ffn_conv_w', 'ffn_conv_b', 'w_down', 'norm_final']
TWIN_WEIGHTS = ['norm_tok', 'w_in', 'a_re', 'a_im', 'log_dt', 'b_re', 'b_im', 'c_re', 'c_im', 'd_skip', 'w_glu', 'w_ssm_out', 'conv_w', 'conv_b', 'w_conv_out', 'w_o', 'norm_ffn', 'w_up', 'ffn_conv_w', 'ffn_conv_b', 'w_down', 'norm_final']
TWIN_DIFF_INPUT = 'x'
TWIN_INPUTS = ['x', 'norm_tok', 'w_in', 'a_re', 'a_im', 'log_dt', 'b_re', 'b_im', 'c_re', 'c_im', 'd_skip', 'w_glu', 'w_ssm_out', 'conv_w', 'conv_b', 'w_conv_out', 'w_o', 'norm_ffn', 'w_up', 'ffn_conv_w', 'ffn_conv_b', 'w_down', 'norm_final', 'loss_target', 'm_norm_tok', 'm_w_in', 'm_a_re', 'm_a_im', 'm_log_dt', 'm_b_re', 'm_b_im', 'm_c_re', 'm_c_im', 'm_d_skip', 'm_w_glu', 'm_w_ssm_out', 'm_conv_w', 'm_conv_b', 'm_w_conv_out', 'm_w_o', 'm_norm_ffn', 'm_w_up', 'm_ffn_conv_w', 'm_ffn_conv_b', 'm_w_down', 'm_norm_final', 'v_norm_tok', 'v_w_in', 'v_a_re', 'v_a_im', 'v_log_dt', 'v_b_re', 'v_b_im', 'v_c_re', 'v_c_im', 'v_d_skip', 'v_w_glu', 'v_w_ssm_out', 'v_conv_w', 'v_conv_b', 'v_w_conv_out', 'v_w_o', 'v_norm_ffn', 'v_w_up', 'v_ffn_conv_w', 'v_ffn_conv_b', 'v_w_down', 'v_norm_final']
TWIN_OUTPUTS = ['loss', 'grad_x', 'grad_norm_tok', 'grad_w_in', 'grad_a_re', 'grad_a_im', 'grad_log_dt', 'grad_b_re', 'grad_b_im', 'grad_c_re', 'grad_c_im', 'grad_d_skip', 'grad_w_glu', 'grad_w_ssm_out', 'grad_conv_w', 'grad_conv_b', 'grad_w_conv_out', 'grad_w_o', 'grad_norm_ffn', 'grad_w_up', 'grad_ffn_conv_w', 'grad_ffn_conv_b', 'grad_w_down', 'grad_norm_final', 'delta_norm_tok', 'delta_w_in', 'delta_a_re', 'delta_a_im', 'delta_log_dt', 'delta_b_re', 'delta_b_im', 'delta_c_re', 'delta_c_im', 'delta_d_skip', 'delta_w_glu', 'delta_w_ssm_out', 'delta_conv_w', 'delta_conv_b', 'delta_w_conv_out', 'delta_w_o', 'delta_norm_ffn', 'delta_w_up', 'delta_ffn_conv_w', 'delta_ffn_conv_b', 'delta_w_down', 'delta_norm_final', 'new_m_norm_tok', 'new_m_w_in', 'new_m_a_re', 'new_m_a_im', 'new_m_log_dt', 'new_m_b_re', 'new_m_b_im', 'new_m_c_re', 'new_m_c_im', 'new_m_d_skip', 'new_m_w_glu', 'new_m_w_ssm_out', 'new_m_conv_w', 'new_m_conv_b', 'new_m_w_conv_out', 'new_m_w_o', 'new_m_norm_ffn', 'new_m_w_up', 'new_m_ffn_conv_w', 'new_m_ffn_conv_b', 'new_m_w_down', 'new_m_norm_final', 'new_v_norm_tok', 'new_v_w_in', 'new_v_a_re', 'new_v_a_im', 'new_v_log_dt', 'new_v_b_re', 'new_v_b_im', 'new_v_c_re', 'new_v_c_im', 'new_v_d_skip', 'new_v_w_glu', 'new_v_w_ssm_out', 'new_v_conv_w', 'new_v_conv_b', 'new_v_w_conv_out', 'new_v_w_o', 'new_v_norm_ffn', 'new_v_w_up', 'new_v_ffn_conv_w', 'new_v_ffn_conv_b', 'new_v_w_down', 'new_v_norm_final']
TWIN_LEAF_KINDS = {'loss': 'loss', 'grad_x': 'grad_x', 'grad_norm_tok': 'grad_w', 'grad_w_in': 'grad_w', 'grad_a_re': 'grad_w', 'grad_a_im': 'grad_w', 'grad_log_dt': 'grad_w', 'grad_b_re': 'grad_w', 'grad_b_im': 'grad_w', 'grad_c_re': 'grad_w', 'grad_c_im': 'grad_w', 'grad_d_skip': 'grad_w', 'grad_w_glu': 'grad_w', 'grad_w_ssm_out': 'grad_w', 'grad_conv_w': 'grad_w', 'grad_conv_b': 'grad_w', 'grad_w_conv_out': 'grad_w', 'grad_w_o': 'grad_w', 'grad_norm_ffn': 'grad_w', 'grad_w_up': 'grad_w', 'grad_ffn_conv_w': 'grad_w', 'grad_ffn_conv_b': 'grad_w', 'grad_w_down': 'grad_w', 'grad_norm_final': 'grad_w', 'delta_norm_tok': 'delta_w', 'delta_w_in': 'delta_w', 'delta_a_re': 'delta_w', 'delta_a_im': 'delta_w', 'delta_log_dt': 'delta_w', 'delta_b_re': 'delta_w', 'delta_b_im': 'delta_w', 'delta_c_re': 'delta_w', 'delta_c_im': 'delta_w', 'delta_d_skip': 'delta_w', 'delta_w_glu': 'delta_w', 'delta_w_ssm_out': 'delta_w', 'delta_conv_w': 'delta_w', 'delta_conv_b': 'delta_w', 'delta_w_conv_out': 'delta_w', 'delta_w_o': 'delta_w', 'delta_norm_ffn': 'delta_w', 'delta_w_up': 'delta_w', 'delta_ffn_conv_w': 'delta_w', 'delta_ffn_conv_b': 'delta_w', 'delta_w_down': 'delta_w', 'delta_norm_final': 'delta_w', 'new_m_norm_tok': 'new_m', 'new_m_w_in': 'new_m', 'new_m_a_re': 'new_m', 'new_m_a_im': 'new_m', 'new_m_log_dt': 'new_m', 'new_m_b_re': 'new_m', 'new_m_b_im': 'new_m', 'new_m_c_re': 'new_m', 'new_m_c_im': 'new_m', 'new_m_d_skip': 'new_m', 'new_m_w_glu': 'new_m', 'new_m_w_ssm_out': 'new_m', 'new_m_conv_w': 'new_m', 'new_m_conv_b': 'new_m', 'new_m_w_conv_out': 'new_m', 'new_m_w_o': 'new_m', 'new_m_norm_ffn': 'new_m', 'new_m_w_up': 'new_m', 'new_m_ffn_conv_w': 'new_m', 'new_m_ffn_conv_b': 'new_m', 'new_m_w_down': 'new_m', 'new_m_norm_final': 'new_m', 'new_v_norm_tok': 'new_v', 'new_v_w_in': 'new_v', 'new_v_a_re': 'new_v', 'new_v_a_im': 'new_v', 'new_v_log_dt': 'new_v', 'new_v_b_re': 'new_v', 'new_v_b_im': 'new_v', 'new_v_c_re': 'new_v', 'new_v_c_im': 'new_v', 'new_v_d_skip': 'new_v', 'new_v_w_glu': 'new_v', 'new_v_w_ssm_out': 'new_v', 'new_v_conv_w': 'new_v', 'new_v_conv_b': 'new_v', 'new_v_w_conv_out': 'new_v', 'new_v_w_o': 'new_v', 'new_v_norm_ffn': 'new_v', 'new_v_w_up': 'new_v', 'new_v_ffn_conv_w': 'new_v', 'new_v_ffn_conv_b': 'new_v', 'new_v_w_down': 'new_v', 'new_v_norm_final': 'new_v'}


def _forward(args):
    return _fwd_reference(*[args[k] for k in FWD_PARAMS])


def _output_shape():
    def fwd():
        inp = _fwd_setup_inputs(0)
        return _fwd_reference(*[inp[k] for k in FWD_PARAMS])
    out = _jax.eval_shape(fwd)
    return out.shape, out.dtype

N_MICROBATCH = 1
ADAM_LR = 0.001
ADAM_B1 = 0.9
ADAM_B2 = 0.999
ADAM_EPS = 1e-08
ADAM_WD = 0.01
ADAM_STEP = 10
PER_EXAMPLE_BATCH_AXIS = {'x': 0, 'loss_target': 0}
SHARED_INPUTS = []
_WEIGHT_DTYPES = {'norm_tok': _jnp.float32, 'w_in': _jnp.float32, 'a_re': _jnp.float32, 'a_im': _jnp.float32, 'log_dt': _jnp.float32, 'b_re': _jnp.float32, 'b_im': _jnp.float32, 'c_re': _jnp.float32, 'c_im': _jnp.float32, 'd_skip': _jnp.float32, 'w_glu': _jnp.float32, 'w_ssm_out': _jnp.float32, 'conv_w': _jnp.float32, 'conv_b': _jnp.float32, 'w_conv_out': _jnp.float32, 'w_o': _jnp.float32, 'norm_ffn': _jnp.float32, 'w_up': _jnp.float32, 'ffn_conv_w': _jnp.float32, 'ffn_conv_b': _jnp.float32, 'w_down': _jnp.float32, 'norm_final': _jnp.float32}
MOMENT_SCALE = {'norm_tok': 8.547773e-02, 'w_in': 4.270736e-02, 'a_re': 1.788360e-03, 'a_im': 1.770263e-03, 'log_dt': 1.344824e+00, 'b_re': 1.177977e-03, 'b_im': 1.164208e-03, 'c_re': 1.606639e-03, 'c_im': 1.641888e-03, 'd_skip': 2.438311e-02, 'w_glu': 6.872421e-03, 'w_ssm_out': 1.612890e-02, 'conv_w': 6.673808e-02, 'conv_b': 6.880750e-02, 'w_conv_out': 4.704282e-02, 'w_o': 4.979440e-02, 'norm_ffn': 6.283887e-02, 'w_up': 2.679784e-02, 'ffn_conv_w': 2.742024e-02, 'ffn_conv_b': 2.643462e-02, 'w_down': 4.369446e-02, 'norm_final': 1.599488e+01}


def _to_microbatches(a, axis):
    t = _jnp.moveaxis(a, axis, 0)
    t = t.reshape((N_MICROBATCH, t.shape[0] // N_MICROBATCH) + t.shape[1:])
    return _jnp.moveaxis(t, 1, axis + 1)


def setup_inputs(seed: int = 0) -> dict:
    inp = _fwd_setup_inputs(seed)
    key = _jax.random.fold_in(_jax.random.key(seed), 7919)
    shape, _ = _output_shape()
    out = dict(inp)
    out["loss_target"] = _jax.random.normal(_jax.random.fold_in(key, 0), shape, _jnp.float32)
    for i, name in enumerate(TWIN_WEIGHTS):
        w = inp[name].astype(_jnp.float32)
        if MOMENT_SCALE is None:
            s = _jnp.sqrt(_jnp.mean(_jnp.square(w)) + 1e-30)
        else:
            s = MOMENT_SCALE[name]
        km, kv = _jax.random.split(_jax.random.fold_in(key, i + 1))
        out[name] = w
        out["m_" + name] = s * _jax.random.normal(km, w.shape, _jnp.float32)
        out["v_" + name] = (s * s) * _jax.random.uniform(kv, w.shape, _jnp.float32, 0.5, 1.5)
    if N_MICROBATCH > 1:
        for name, axis in PER_EXAMPLE_BATCH_AXIS.items():
            out[name] = _to_microbatches(out[name], axis)
    return {'x': out['x'], 'norm_tok': out['norm_tok'], 'w_in': out['w_in'], 'a_re': out['a_re'], 'a_im': out['a_im'], 'log_dt': out['log_dt'], 'b_re': out['b_re'], 'b_im': out['b_im'], 'c_re': out['c_re'], 'c_im': out['c_im'], 'd_skip': out['d_skip'], 'w_glu': out['w_glu'], 'w_ssm_out': out['w_ssm_out'], 'conv_w': out['conv_w'], 'conv_b': out['conv_b'], 'w_conv_out': out['w_conv_out'], 'w_o': out['w_o'], 'norm_ffn': out['norm_ffn'], 'w_up': out['w_up'], 'ffn_conv_w': out['ffn_conv_w'], 'ffn_conv_b': out['ffn_conv_b'], 'w_down': out['w_down'], 'norm_final': out['norm_final'], 'loss_target': out['loss_target'], 'm_norm_tok': out['m_norm_tok'], 'm_w_in': out['m_w_in'], 'm_a_re': out['m_a_re'], 'm_a_im': out['m_a_im'], 'm_log_dt': out['m_log_dt'], 'm_b_re': out['m_b_re'], 'm_b_im': out['m_b_im'], 'm_c_re': out['m_c_re'], 'm_c_im': out['m_c_im'], 'm_d_skip': out['m_d_skip'], 'm_w_glu': out['m_w_glu'], 'm_w_ssm_out': out['m_w_ssm_out'], 'm_conv_w': out['m_conv_w'], 'm_conv_b': out['m_conv_b'], 'm_w_conv_out': out['m_w_conv_out'], 'm_w_o': out['m_w_o'], 'm_norm_ffn': out['m_norm_ffn'], 'm_w_up': out['m_w_up'], 'm_ffn_conv_w': out['m_ffn_conv_w'], 'm_ffn_conv_b': out['m_ffn_conv_b'], 'm_w_down': out['m_w_down'], 'm_norm_final': out['m_norm_final'], 'v_norm_tok': out['v_norm_tok'], 'v_w_in': out['v_w_in'], 'v_a_re': out['v_a_re'], 'v_a_im': out['v_a_im'], 'v_log_dt': out['v_log_dt'], 'v_b_re': out['v_b_re'], 'v_b_im': out['v_b_im'], 'v_c_re': out['v_c_re'], 'v_c_im': out['v_c_im'], 'v_d_skip': out['v_d_skip'], 'v_w_glu': out['v_w_glu'], 'v_w_ssm_out': out['v_w_ssm_out'], 'v_conv_w': out['v_conv_w'], 'v_conv_b': out['v_conv_b'], 'v_w_conv_out': out['v_w_conv_out'], 'v_w_o': out['v_w_o'], 'v_norm_ffn': out['v_norm_ffn'], 'v_w_up': out['v_w_up'], 'v_ffn_conv_w': out['v_ffn_conv_w'], 'v_ffn_conv_b': out['v_ffn_conv_b'], 'v_w_down': out['v_w_down'], 'v_norm_final': out['v_norm_final']}


def _loss(weights, diff, rest, loss_target):
    with _jax.named_scope("forward"):
        args = {**rest, TWIN_DIFF_INPUT: diff, **{k: w.astype(_WEIGHT_DTYPES[k]) for k, w in weights.items()}}
        y = _forward(args)
    with _jax.named_scope("loss_head"):
        err = _jnp.square(y.astype(_jnp.float32) - loss_target)
        return 0.5 * _jnp.sum(_jnp.mean(err, axis=-1)) if err.ndim else 0.5 * err


def _adamw(w, g, m, v):
    m = ADAM_B1 * m + (1.0 - ADAM_B1) * g
    v = ADAM_B2 * v + (1.0 - ADAM_B2) * _jnp.square(g)
    m_hat = m / (1.0 - ADAM_B1 ** ADAM_STEP)
    v_hat = v / (1.0 - ADAM_B2 ** ADAM_STEP)
    delta = -ADAM_LR * (m_hat / (_jnp.sqrt(v_hat) + ADAM_EPS) + ADAM_WD * w)
    return delta, m, v


def reference(x, norm_tok, w_in, a_re, a_im, log_dt, b_re, b_im, c_re, c_im, d_skip, w_glu, w_ssm_out, conv_w, conv_b, w_conv_out, w_o, norm_ffn, w_up, ffn_conv_w, ffn_conv_b, w_down, norm_final, loss_target, m_norm_tok, m_w_in, m_a_re, m_a_im, m_log_dt, m_b_re, m_b_im, m_c_re, m_c_im, m_d_skip, m_w_glu, m_w_ssm_out, m_conv_w, m_conv_b, m_w_conv_out, m_w_o, m_norm_ffn, m_w_up, m_ffn_conv_w, m_ffn_conv_b, m_w_down, m_norm_final, v_norm_tok, v_w_in, v_a_re, v_a_im, v_log_dt, v_b_re, v_b_im, v_c_re, v_c_im, v_d_skip, v_w_glu, v_w_ssm_out, v_conv_w, v_conv_b, v_w_conv_out, v_w_o, v_norm_ffn, v_w_up, v_ffn_conv_w, v_ffn_conv_b, v_w_down, v_norm_final):
    given = dict(x=x, norm_tok=norm_tok, w_in=w_in, a_re=a_re, a_im=a_im, log_dt=log_dt, b_re=b_re, b_im=b_im, c_re=c_re, c_im=c_im, d_skip=d_skip, w_glu=w_glu, w_ssm_out=w_ssm_out, conv_w=conv_w, conv_b=conv_b, w_conv_out=w_conv_out, w_o=w_o, norm_ffn=norm_ffn, w_up=w_up, ffn_conv_w=ffn_conv_w, ffn_conv_b=ffn_conv_b, w_down=w_down, norm_final=norm_final, loss_target=loss_target, m_norm_tok=m_norm_tok, m_w_in=m_w_in, m_a_re=m_a_re, m_a_im=m_a_im, m_log_dt=m_log_dt, m_b_re=m_b_re, m_b_im=m_b_im, m_c_re=m_c_re, m_c_im=m_c_im, m_d_skip=m_d_skip, m_w_glu=m_w_glu, m_w_ssm_out=m_w_ssm_out, m_conv_w=m_conv_w, m_conv_b=m_conv_b, m_w_conv_out=m_w_conv_out, m_w_o=m_w_o, m_norm_ffn=m_norm_ffn, m_w_up=m_w_up, m_ffn_conv_w=m_ffn_conv_w, m_ffn_conv_b=m_ffn_conv_b, m_w_down=m_w_down, m_norm_final=m_norm_final, v_norm_tok=v_norm_tok, v_w_in=v_w_in, v_a_re=v_a_re, v_a_im=v_a_im, v_log_dt=v_log_dt, v_b_re=v_b_re, v_b_im=v_b_im, v_c_re=v_c_re, v_c_im=v_c_im, v_d_skip=v_d_skip, v_w_glu=v_w_glu, v_w_ssm_out=v_w_ssm_out, v_conv_w=v_conv_w, v_conv_b=v_conv_b, v_w_conv_out=v_w_conv_out, v_w_o=v_w_o, v_norm_ffn=v_norm_ffn, v_w_up=v_w_up, v_ffn_conv_w=v_ffn_conv_w, v_ffn_conv_b=v_ffn_conv_b, v_w_down=v_w_down, v_norm_final=v_norm_final)
    weights = {n: given[n] for n in TWIN_WEIGHTS}
    shared = {n: given[n] for n in SHARED_INPUTS}
    per_example = {n: given[n] for n in ['x']}
    grad_fn = _jax.value_and_grad(_loss, argnums=(0, 1))

    def one_microbatch(ex, loss_target):
        ex = dict(ex)
        diff = ex.pop(TWIN_DIFF_INPUT)
        return grad_fn(weights, diff, {**shared, **ex}, loss_target)

    if N_MICROBATCH == 1:
        loss, (grad_w, grad_x) = one_microbatch(per_example, given["loss_target"])
    else:
        def body(carry, xs):
            loss_sum, grad_sum = carry
            l_k, (gw_k, gx_k) = one_microbatch(xs[0], xs[1])
            with _jax.named_scope("update"):
                return (loss_sum + l_k, _jax.tree.map(_jnp.add, grad_sum, gw_k)), gx_k

        init = (_jnp.zeros((), _jnp.float32), _jax.tree.map(_jnp.zeros_like, weights))
        (loss, grad_w), grad_x = _jax.lax.scan(body, init, (per_example, given["loss_target"]))
    with _jax.named_scope("update"):
        delta_w, new_m, new_v = {}, {}, {}
        for n in TWIN_WEIGHTS:
            delta_w[n], new_m[n], new_v[n] = _adamw(weights[n], grad_w[n], given["m_" + n], given["v_" + n])
    return (loss, grad_x, *[grad_w[n] for n in TWIN_WEIGHTS], *[delta_w[n] for n in TWIN_WEIGHTS],
            *[new_m[n] for n in TWIN_WEIGHTS], *[new_v[n] for n in TWIN_WEIGHTS])
```

```python
import functools

import jax
import jax.numpy as jnp
from jax import lax
from jax.experimental import pallas as pl
from jax.experimental.pallas import tpu as pltpu

F32 = jnp.float32
BF16 = jnp.bfloat16
N_DEV = 8
LANES = 128
SLAB = 4
EPS = 1e-6
ADAM_LR = 0.001
ADAM_B1 = 0.9
ADAM_B2 = 0.999
ADAM_EPS = 1e-08
ADAM_WD = 0.01
ADAM_STEP = 10
VMEM_LIMIT = 56 * 1024 * 1024
MESH = pl.DeviceIdType.MESH


def _tile(n, pref, mult=LANES):
    best = None
    t = mult
    while t <= min(n, pref):
        if n % t == 0:
            best = t
        t += mult
    return best if best is not None else n


def _params(ndim):
    return pltpu.CompilerParams(dimension_semantics=("arbitrary",) * ndim, vmem_limit_bytes=VMEM_LIMIT)


def _sds(shape, dtype):
    return jax.ShapeDtypeStruct(tuple(shape), dtype)


def _mm(name, a, b, mode, *, out_dtype=F32, tm=1024, tn=1024, tk=2048, dims=None, a_spec=None, b_spec=None):
    if dims is None:
        if mode == "nn":
            (M, K), N = a.shape, b.shape[1]
        elif mode == "nt":
            (M, K), N = a.shape, b.shape[0]
        else:
            (K, M), N = a.shape, b.shape[1]
    else:
        M, N, K = dims
    tm, tn, tk = _tile(M, tm), _tile(N, tn), _tile(K, tk)
    nk = K // tk
    if mode == "nn":
        dn = (((1,), (0,)), ((), ()))
        sa = pl.BlockSpec((tm, tk), lambda i, j, k: (i, k))
        sb = pl.BlockSpec((tk, tn), lambda i, j, k: (k, j))
    elif mode == "nt":
        dn = (((1,), (1,)), ((), ()))
        sa = pl.BlockSpec((tm, tk), lambda i, j, k: (i, k))
        sb = pl.BlockSpec((tn, tk), lambda i, j, k: (j, k))
    else:
        dn = (((0,), (0,)), ((), ()))
        sa = pl.BlockSpec((tk, tm), lambda i, j, k: (k, i))
        sb = pl.BlockSpec((tk, tn), lambda i, j, k: (k, j))
    sa = a_spec(tm, tn, tk) if a_spec is not None else sa
    sb = b_spec(tm, tn, tk) if b_spec is not None else sb
    use_acc = nk > 1 and out_dtype != F32

    def body(a_ref, b_ref, o_ref, *acc):
        k = pl.program_id(2)
        p = lax.dot_general(a_ref[...], b_ref[...], dn, preferred_element_type=F32)
        if nk == 1:
            o_ref[...] = p.astype(out_dtype)
        else:
            tgt = acc[0] if use_acc else o_ref

            @pl.when(k == 0)
            def _():
                tgt[...] = p

            @pl.when(k > 0)
            def _():
                tgt[...] += p

            if use_acc:
                @pl.when(k == nk - 1)
                def _():
                    o_ref[...] = acc[0][...].astype(out_dtype)

    return pl.pallas_call(
        body, name=name, grid=(M // tm, N // tn, nk),
        in_specs=[sa, sb], out_specs=pl.BlockSpec((tm, tn), lambda i, j, k: (i, j)),
        out_shape=_sds((M, N), out_dtype),
        scratch_shapes=[pltpu.VMEM((tm, tn), F32)] if use_acc else [],
        compiler_params=_params(3),
    )(a, b)


def _rows(name, body, L, tm, ins, outs):
    return pl.pallas_call(
        body, name=name, grid=(L // tm,),
        in_specs=[s for _, s in ins], out_specs=[s for _, s in outs],
        out_shape=[o for o, _ in outs], compiler_params=_params(1),
    )(*[a for a, _ in ins])


def _rs(tm, w, cb=0):
    return pl.BlockSpec((tm, w), lambda i: (i, cb))


def _fs(shape):
    return pl.BlockSpec(tuple(shape), lambda i: (0,) * len(shape))


def _acc_rows(i, ref, part):
    @pl.when(i == 0)
    def _():
        ref[...] = part

    @pl.when(i > 0)
    def _():
        ref[...] += part


def _cast_bf16(name, w):
    R, C = w.shape
    tr = _tile(R, max(16, (1 << 20) // C), 16)

    def body(w_ref, o_ref):
        o_ref[...] = w_ref[...].astype(BF16)

    return _rows(name, body, R, tr, [(w, _rs(tr, C))], [(_sds((R, C), BF16), _rs(tr, C))])[0]


def _rms_fwd(name, x, g, tm):
    L, D = x.shape

    def body(x_ref, g_ref, o_ref):
        xv = x_ref[...]
        r = lax.rsqrt(jnp.mean(xv * xv, axis=-1, keepdims=True) + EPS)
        o_ref[...] = (xv * r * g_ref[...]).astype(BF16)

    return _rows(name, body, L, tm, [(x, _rs(tm, D)), (g, _fs((1, D)))], [(_sds((L, D), BF16), _rs(tm, D))])[0]


def _res_rms_fwd(name, x, o, g, tm):
    L, D = x.shape

    def body(x_ref, o_ref, g_ref, h_ref, hn_ref):
        h = x_ref[...] + o_ref[...]
        r = lax.rsqrt(jnp.mean(h * h, axis=-1, keepdims=True) + EPS)
        h_ref[...] = h
        hn_ref[...] = (h * r * g_ref[...]).astype(BF16)

    return _rows(name, body, L, tm, [(x, _rs(tm, D)), (o, _rs(tm, D)), (g, _fs((1, D)))],
                 [(_sds((L, D), F32), _rs(tm, D)), (_sds((L, D), BF16), _rs(tm, D))])


def _rms_bwd(name, dn, h, g, dres, tm, with_bf16):
    L, D = h.shape

    def body(dn_ref, h_ref, g_ref, dres_ref, dh_ref, *rest):
        i = pl.program_id(0)
        h = h_ref[...]
        r = lax.rsqrt(jnp.mean(h * h, axis=-1, keepdims=True) + EPS)
        xh = h * r
        d = dn_ref[...]
        dxh = d * g_ref[...]
        dh = dres_ref[...] + r * (dxh - xh * jnp.mean(dxh * xh, axis=-1, keepdims=True))
        dh_ref[...] = dh
        if with_bf16:
            rest[0][...] = dh.astype(BF16)
        _acc_rows(i, rest[-1], jnp.sum(d * xh, axis=0, keepdims=True))

    outs = [(_sds((L, D), F32), _rs(tm, D))]
    if with_bf16:
        outs.append((_sds((L, D), BF16), _rs(tm, D)))
    outs.append((_sds((1, D), F32), _fs((1, D))))
    return _rows(name, body, L, tm, [(dn, _rs(tm, D)), (h, _rs(tm, D)), (g, _fs((1, D))), (dres, _rs(tm, D))], outs)


def _final(name, h1, o2, g, tgt, tm):
    L, D = h1.shape

    def body(h1_ref, o2_ref, g_ref, t_ref, dh_ref, dhb_ref, dg_ref, loss_ref):
        i = pl.program_id(0)
        h = h1_ref[...] + o2_ref[...]
        r = lax.rsqrt(jnp.mean(h * h, axis=-1, keepdims=True) + EPS)
        xh = h * r
        gv = g_ref[...]
        e = xh * gv - t_ref[...]
        part = 0.5 * jnp.sum(jnp.mean(e * e, axis=-1, keepdims=True), axis=0, keepdims=True)
        dy = e / D
        dxh = dy * gv
        dh = r * (dxh - xh * jnp.mean(dxh * xh, axis=-1, keepdims=True))
        dh_ref[...] = dh
        dhb_ref[...] = dh.astype(BF16)
        _acc_rows(i, dg_ref, jnp.sum(dy * xh, axis=0, keepdims=True))
        _acc_rows(i, loss_ref, jnp.broadcast_to(part, (8, LANES)))

    return _rows(name, body, L, tm,
                 [(h1, _rs(tm, D)), (o2, _rs(tm, D)), (g, _fs((1, D))), (tgt, _rs(tm, D))],
                 [(_sds((L, D), F32), _rs(tm, D)), (_sds((L, D), BF16), _rs(tm, D)),
                  (_sds((1, D), F32), _fs((1, D))), (_sds((8, LANES), F32), _fs((8, LANES)))])


def _glu_fn(y, g1):
    ya = jax.nn.gelu(y)
    return ya * jax.nn.sigmoid(g1)


def _glu_fwd(name, y, g1, tm):
    L, W = y.shape

    def body(y_ref, g_ref, o_ref):
        o_ref[...] = _glu_fn(y_ref[...], g_ref[...]).astype(BF16)

    return _rows(name, body, L, tm, [(y, _rs(tm, W)), (g1, _rs(tm, W))], [(_sds((L, W), BF16), _rs(tm, W))])[0]


def _glu_bwd(name, y, g1, dya2, tm):
    L, W = y.shape

    def body(y_ref, g_ref, d_ref, dy_ref, dg_ref):
        _, vjp = jax.vjp(_glu_fn, y_ref[...], g_ref[...])
        dy, dg = vjp(d_ref[...])
        dy_ref[...] = dy
        dg_ref[...] = dg.astype(BF16)

    return _rows(name, body, L, tm, [(y, _rs(tm, W)), (g1, _rs(tm, W)), (dya2, _rs(tm, W))],
                 [(_sds((L, W), F32), _rs(tm, W)), (_sds((L, W), BF16), _rs(tm, W))])


def _gelu_bwd(name, y, dy_direct, dya_g, proj, dskip, tm):
    L, W = y.shape

    def body(y_ref, dd_ref, dg_ref, u_ref, dyb_ref, dsk_ref):
        i = pl.program_id(0)
        _, vjp = jax.vjp(jax.nn.gelu, y_ref[...])
        dy = dd_ref[...] + vjp(dg_ref[...])[0]
        dyb_ref[...] = dy.astype(BF16)
        _acc_rows(i, dsk_ref, jnp.sum(dy * u_ref[...], axis=0, keepdims=True))

    del dskip
    return _rows(name, body, L, tm,
                 [(y, _rs(tm, W)), (dy_direct, _rs(tm, W)), (dya_g, _rs(tm, W)), (proj, _rs(tm, W, 0))],
                 [(_sds((L, W), BF16), _rs(tm, W)), (_sds((1, W), F32), _fs((1, W)))])


def _merge_fn(ma, mb, za, zb):
    return jax.nn.sigmoid(ma) * za + jax.nn.sigmoid(mb) * zb


def _merge_fwd(name, proj, cb_a, cb_b, za, zb, tm):
    L, D = za.shape

    def body(ma_ref, mb_ref, za_ref, zb_ref, o_ref):
        o_ref[...] = _merge_fn(ma_ref[...], mb_ref[...], za_ref[...], zb_ref[...]).astype(BF16)

    return _rows(name, body, L, tm,
                 [(proj, _rs(tm, D, cb_a)), (proj, _rs(tm, D, cb_b)), (za, _rs(tm, D)), (zb, _rs(tm, D))],
                 [(_sds((L, D), BF16), _rs(tm, D))])[0]


def _merge_bwd(name, proj, cb_a, cb_b, za, zb, dmerged, tm):
    L, D = za.shape

    def body(ma_ref, mb_ref, za_ref, zb_ref, d_ref, dza_ref, dzb_ref, dma_ref, dmb_ref):
        _, vjp = jax.vjp(_merge_fn, ma_ref[...], mb_ref[...], za_ref[...], zb_ref[...])
        dma, dmb, dza, dzb = vjp(d_ref[...])
        dza_ref[...] = dza.astype(BF16)
        dzb_ref[...] = dzb.astype(BF16)
        dma_ref[...] = dma.astype(BF16)
        dmb_ref[...] = dmb.astype(BF16)

    return _rows(name, body, L, tm,
                 [(proj, _rs(tm, D, cb_a)), (proj, _rs(tm, D, cb_b)), (za, _rs(tm, D)), (zb, _rs(tm, D)),
                  (dmerged, _rs(tm, D))],
                 [(_sds((L, D), BF16), _rs(tm, D)), (_sds((L, D), BF16), _rs(tm, D)),
                  (_sds((L, D), BF16), _rs(tm, D)), (_sds((L, D), BF16), _rs(tm, D))])


def _shift_down(x, k):
    row = lax.broadcasted_iota(jnp.int32, x.shape, 0)
    return jnp.where(row >= k, pltpu.roll(x, k, axis=0), 0.0)


def _shift_up(x, k):
    n = x.shape[0]
    row = lax.broadcasted_iota(jnp.int32, x.shape, 0)
    return jnp.where(row < n - k, pltpu.roll(x, n - k, axis=0), 0.0)


def _conv3(cv, w_ref, b_ref):
    return (w_ref[2:3, :] * cv + w_ref[1:2, :] * _shift_down(cv, 1) + w_ref[0:1, :] * _shift_down(cv, 2)
            + b_ref[...])


def _conv3_bwd(dcc, cv, w_ref):
    dcv = w_ref[2:3, :] * dcc + w_ref[1:2, :] * _shift_up(dcc, 1) + w_ref[0:1, :] * _shift_up(dcc, 2)
    dw = [jnp.sum(dcc * _shift_down(cv, 2), axis=0, keepdims=True),
          jnp.sum(dcc * _shift_down(cv, 1), axis=0, keepdims=True),
          jnp.sum(dcc * cv, axis=0, keepdims=True)]
    db = jnp.sum(dcc, axis=0, keepdims=True)
    return dcv, dw, db


def _store_rows(ref, rows):
    for r, val in enumerate(rows):
        ref[r:r + 1, :] = val


def _cols(name, body, ncb, ins, outs):
    return pl.pallas_call(
        body, name=name, grid=(ncb,),
        in_specs=[s for _, s in ins], out_specs=[s for _, s in outs],
        out_shape=[o for o, _ in outs], compiler_params=_params(1),
    )(*[a for a, _ in ins])


def _cb(L, w, off=0):
    return pl.BlockSpec((L, w), lambda j: (0, j + off))


def _convb_fwd(name, proj, cb_v, cb_gb, cb_gc, w, b):
    L = proj.shape[0]
    W = w.shape[1]
    c = LANES

    def body(v_ref, gb_ref, gc_ref, w_ref, b_ref, q_ref):
        cc = _conv3(gc_ref[...] * v_ref[...], w_ref, b_ref)
        q_ref[...] = (gb_ref[...] * cc).astype(BF16)

    return _cols(name, body, W // c,
                 [(proj, _cb(L, c, cb_v)), (proj, _cb(L, c, cb_gb)), (proj, _cb(L, c, cb_gc)),
                  (w, _cb(3, c)), (b, _cb(1, c))],
                 [(_sds((L, W), BF16), _cb(L, c))])[0]


def _convb_bwd(name, proj, cb_v, cb_gb, cb_gc, w, b, dq):
    L = proj.shape[0]
    W = w.shape[1]
    c = LANES

    def body(v_ref, gb_ref, gc_ref, w_ref, b_ref, dq_ref, dv_ref, dgb_ref, dgc_ref, dw_ref, db_ref):
        v, gc = v_ref[...], gc_ref[...]
        cv = gc * v
        cc = _conv3(cv, w_ref, b_ref)
        dq = dq_ref[...]
        dgb_ref[...] = (dq * cc).astype(BF16)
        dcv, dw, db = _conv3_bwd(dq * gb_ref[...], cv, w_ref)
        dv_ref[...] = (dcv * gc).astype(BF16)
        dgc_ref[...] = (dcv * v).astype(BF16)
        _store_rows(dw_ref, dw)
        db_ref[...] = db

    return _cols(name, body, W // c,
                 [(proj, _cb(L, c, cb_v)), (proj, _cb(L, c, cb_gb)), (proj, _cb(L, c, cb_gc)),
                  (w, _cb(3, c)), (b, _cb(1, c)), (dq, _cb(L, c))],
                 [(_sds((L, W), BF16), _cb(L, c)), (_sds((L, W), BF16), _cb(L, c)), (_sds((L, W), BF16), _cb(L, c)),
                  (_sds((3, W), F32), _cb(3, c)), (_sds((1, W), F32), _cb(1, c))])


def _ffn_fwd(name, hh, w, b):
    L = hh.shape[0]
    Fw = w.shape[1]
    c = LANES
    nf = Fw // c

    def body(a_ref, h2_ref, w_ref, b_ref, f_ref):
        a = _conv3(a_ref[...], w_ref, b_ref)
        f_ref[...] = (jax.nn.gelu(a) * h2_ref[...]).astype(BF16)

    return _cols(name, body, nf, [(hh, _cb(L, c)), (hh, _cb(L, c, nf)), (w, _cb(3, c)), (b, _cb(1, c))],
                 [(_sds((L, Fw), BF16), _cb(L, c))])[0]


def _ffn_bwd(name, hh, w, b, df):
    L = hh.shape[0]
    Fw = w.shape[1]
    c = LANES
    nf = Fw // c

    def body(a_ref, h2_ref, w_ref, b_ref, df_ref, dhh_ref, dw_ref, db_ref):
        h1 = a_ref[...]
        a = _conv3(h1, w_ref, b_ref)
        ga, vjp = jax.vjp(jax.nn.gelu, a)
        d = df_ref[...]
        dhh_ref[1] = (d * ga).astype(BF16)
        da = vjp(d * h2_ref[...])[0]
        dh1, dw, db = _conv3_bwd(da, h1, w_ref)
        dhh_ref[0] = dh1.astype(BF16)
        _store_rows(dw_ref, dw)
        db_ref[...] = db

    return _cols(name, body, nf,
                 [(hh, _cb(L, c)), (hh, _cb(L, c, nf)), (w, _cb(3, c)), (b, _cb(1, c)), (df, _cb(L, c))],
                 [(_sds((2, L, Fw), BF16), pl.BlockSpec((2, L, c), lambda j: (0, 0, j))),
                  (_sds((3, Fw), F32), _cb(3, c)), (_sds((1, Fw), F32), _cb(1, c))])


def _prep_fn(ar, ai, ldt, brt, bit):
    dt = jnp.exp(ldt)
    mag = jnp.exp(dt * ar)
    are = mag * jnp.cos(dt * ai)
    aim = mag * jnp.sin(dt * ai)
    nr = are - 1.0
    ni = aim
    den = ar * ar + ai * ai
    fr = (nr * ar + ni * ai) / den
    fi = (ni * ar - nr * ai) / den
    return are, aim, fr * brt - fi * bit, fr * bit + fi * brt


def _prep_fwd(name, ar, ai, ldt, brt, bit):
    def body(ar_ref, ai_ref, l_ref, br_ref, bi_ref, o1, o2, o3, o4):
        o1[...], o2[...], o3[...], o4[...] = _prep_fn(ar_ref[...], ai_ref[...], l_ref[...], br_ref[...], bi_ref[...])

    return pl.pallas_call(body, name=name,
                          out_shape=[_sds(ar.shape, F32), _sds(ar.shape, F32), _sds(brt.shape, F32), _sds(brt.shape, F32)],
                          )(ar, ai, ldt, brt, bit)


def _prep_bwd(name, ar, ai, ldt, brt, bit, g1, g2, g3, g4):
    def body(ar_ref, ai_ref, l_ref, br_ref, bi_ref, g1_ref, g2_ref, g3_ref, g4_ref, o1, o2, o3, o4, o5):
        _, vjp = jax.vjp(_prep_fn, ar_ref[...], ai_ref[...], l_ref[...], br_ref[...], bi_ref[...])
        o1[...], o2[...], o3[...], o4[...], o5[...] = vjp((g1_ref[...], g2_ref[...], g3_ref[...], g4_ref[...]))

    return pl.pallas_call(body, name=name,
                          out_shape=[_sds(ar.shape, F32)] * 3 + [_sds(brt.shape, F32)] * 2,
                          )(ar, ai, ldt, brt, bit, g1, g2, g3, g4)


def _ssm_in(name, src, m1, m2, tm):
    L = src.shape[0]
    nb = m1.shape[0]

    def body(s_ref, m1_ref, m2_ref, o1_ref, o2_ref):
        u = s_ref[...].astype(BF16)
        r1 = jnp.dot(u, m1_ref[...], preferred_element_type=F32)
        r2 = jnp.dot(u, m2_ref[...], preferred_element_type=F32)
        for q in range(SLAB):
            o1_ref[q] = r1[:, q * LANES:(q + 1) * LANES]
            o2_ref[q] = r2[:, q * LANES:(q + 1) * LANES]

    ms = pl.BlockSpec((None, LANES, SLAB * LANES), lambda i, j: (j, 0, 0))
    os_ = pl.BlockSpec((SLAB, tm, LANES), lambda i, j: (j, i, 0))
    return pl.pallas_call(
        body, name=name, grid=(L // tm, nb),
        in_specs=[pl.BlockSpec((tm, LANES), lambda i, j: (i, j)), ms, ms], out_specs=[os_, os_],
        out_shape=[_sds((SLAB * nb, L, LANES), F32)] * 2, compiler_params=_params(2),
    )(src, m1, m2)


def _ssm_out(name, x1, x2, m1, m2, aux, dvec, tm, post=None):
    L = x1.shape[1]
    nb = m1.shape[0]

    def body(x1_ref, x2_ref, m1_ref, m2_ref, a_ref, d_ref, o_ref, *rest):
        a1 = jnp.concatenate([x1_ref[q] for q in range(SLAB)], axis=1).astype(BF16)
        a2 = jnp.concatenate([x2_ref[q] for q in range(SLAB)], axis=1).astype(BF16)
        y = (jnp.dot(a1, m1_ref[...], preferred_element_type=F32) + jnp.dot(a2, m2_ref[...], preferred_element_type=F32)
             + d_ref[...] * a_ref[...].astype(F32))
        o_ref[...] = y
        if post is not None:
            rest[0][...] = post(y).astype(BF16)

    xs = pl.BlockSpec((SLAB, tm, LANES), lambda i, j: (j, i, 0))
    ms = pl.BlockSpec((None, SLAB * LANES, LANES), lambda i, j: (j, 0, 0))
    cs = pl.BlockSpec((tm, LANES), lambda i, j: (i, j))
    W = nb * LANES
    outs, ospecs = [_sds((L, W), F32)], [cs]
    if post is not None:
        outs.append(_sds((L, W), BF16))
        ospecs.append(cs)
    return pl.pallas_call(
        body, name=name, grid=(L // tm, nb),
        in_specs=[xs, xs, ms, ms, cs, pl.BlockSpec((1, LANES), lambda i, j: (0, j))], out_specs=ospecs,
        out_shape=outs, compiler_params=_params(2),
    )(x1, x2, m1, m2, aux, dvec)


def _ssm_dw(name, src, x1, x2, tk):
    L = src.shape[0]
    nb = x1.shape[0] // SLAB
    dn = (((0,), (0,)), ((), ()))

    def body(s_ref, x1_ref, x2_ref, o1_ref, o2_ref):
        k = pl.program_id(1)
        s = s_ref[...].astype(BF16)
        a1 = jnp.concatenate([x1_ref[q] for q in range(SLAB)], axis=1).astype(BF16)
        a2 = jnp.concatenate([x2_ref[q] for q in range(SLAB)], axis=1).astype(BF16)
        _acc_rows(k, o1_ref, lax.dot_general(s, a1, dn, preferred_element_type=F32))
        _acc_rows(k, o2_ref, lax.dot_general(s, a2, dn, preferred_element_type=F32))

    xs = pl.BlockSpec((SLAB, tk, LANES), lambda j, k: (j, k, 0))
    os_ = pl.BlockSpec((None, LANES, SLAB * LANES), lambda j, k: (j, 0, 0))
    return pl.pallas_call(
        body, name=name, grid=(nb, L // tk),
        in_specs=[pl.BlockSpec((tk, LANES), lambda j, k: (k, j)), xs, xs], out_specs=[os_, os_],
        out_shape=[_sds((nb, LANES, SLAB * LANES), F32)] * 2, compiler_params=_params(2),
    )(src, x1, x2)


def _scan(name, b_re, b_im, a_re, a_im, xs=None):
    reverse = xs is not None
    ns, L, _ = b_re.shape
    ng = ns // 8
    tc = min(LANES, L)
    pitch = tc + 8
    nt = L // tc
    n_in = 4 if reverse else 2

    def body(*refs):
        ins = refs[:n_in]
        ar_ref, ai_ref = refs[n_in], refs[n_in + 1]
        o_re, o_im = refs[n_in + 2], refs[n_in + 3]
        k = n_in + 4
        if reverse:
            da_re, da_im = refs[k], refs[k + 1]
            k += 2
        stage = refs[k:k + n_in]
        out_re, out_im, st_re, st_im = refs[k + n_in:k + n_in + 4]
        acc = refs[k + n_in + 4:]
        i = pl.program_id(0)

        @pl.when(i == 0)
        def _():
            st_re[...] = jnp.zeros(st_re.shape, F32)
            st_im[...] = jnp.zeros(st_im.shape, F32)
            for r in acc:
                r[...] = jnp.zeros(r.shape, F32)

        for s in range(ns):
            for src, dst in zip(ins, stage):
                dst[pl.ds(s * pitch, tc), :] = src[s]

        a_r = [ar_ref[g] for g in range(ng)]
        a_i = [ai_ref[g] for g in range(ng)]

        def step(tt, carry):
            t = (tc - 1 - tt) if reverse else tt
            new = []
            for g in range(ng):
                rows = pl.ds(g * 8 * pitch + t, 8, stride=pitch)
                cr, ci = carry[2 * g], carry[2 * g + 1]
                br, bi = stage[0][rows, :], stage[1][rows, :]
                if reverse:
                    xr, xi = stage[2][rows, :], stage[3][rows, :]
                    acc[0][g] += xr * cr + xi * ci
                    acc[1][g] += xr * ci - xi * cr
                    nr = a_r[g] * cr + a_i[g] * ci + br
                    ni = a_r[g] * ci - a_i[g] * cr + bi
                else:
                    nr = a_r[g] * cr - a_i[g] * ci + br
                    ni = a_r[g] * ci + a_i[g] * cr + bi
                out_re[rows, :] = nr
                out_im[rows, :] = ni
                new += [nr, ni]
            return tuple(new)

        init = []
        for g in range(ng):
            init += [st_re[g], st_im[g]]
        fin = lax.fori_loop(0, tc, step, tuple(init), unroll=2)
        for g in range(ng):
            st_re[g] = fin[2 * g]
            st_im[g] = fin[2 * g + 1]
        for s in range(ns):
            o_re[s] = out_re[pl.ds(s * pitch, tc), :]
            o_im[s] = out_im[pl.ds(s * pitch, tc), :]
        if reverse:
            da_re[...] = acc[0][...]
            da_im[...] = acc[1][...]

    tmap = (lambda i: (0, nt - 1 - i, 0)) if reverse else (lambda i: (0, i, 0))
    bs = pl.BlockSpec((ns, tc, LANES), tmap)
    as_ = pl.BlockSpec((ng, 8, LANES), lambda i: (0, 0, 0))
    ins = [b_re, b_im] + (list(xs) if reverse else [])
    out_shape = [_sds((ns, L, LANES), F32)] * 2 + ([_sds((ng, 8, LANES), F32)] * 2 if reverse else [])
    out_specs = [bs, bs] + ([as_, as_] if reverse else [])
    scratch = [pltpu.VMEM((ns * pitch, LANES), F32)] * (n_in + 2) + [pltpu.VMEM((ng, 8, LANES), F32)] * (4 if reverse else 2)
    return pl.pallas_call(
        body, name=name, grid=(nt,), in_specs=[bs] * n_in + [as_, as_], out_specs=out_specs,
        out_shape=out_shape, scratch_shapes=scratch, compiler_params=_params(1),
    )(*ins, a_re, a_im)


def _peer(k):
    x, y, c = lax.axis_index("x"), lax.axis_index("y"), lax.axis_index("c")
    px = 1 - x if (k >> 2) & 1 else x
    py = 1 - y if (k >> 1) & 1 else y
    pc = 1 - c if k & 1 else c
    return (px, py, pc), 4 * px + 2 * py + pc


def _window(ref, kind, idx, n):
    if kind == "col":
        w = ref.shape[1] // n
        return ref.at[:, pl.ds(pl.multiple_of(idx * w, LANES), w)]
    r = ref.shape[0] // n
    return ref.at[pl.ds(pl.multiple_of(idx * r, 8), r), :]


def _all_gather(name, shards, kinds):
    n = len(shards)
    fulls = []
    for s, kind in zip(shards, kinds):
        fulls.append(_sds((s.shape[0], s.shape[1] * N_DEV) if kind == "col" else (s.shape[0] * N_DEV, s.shape[1]), s.dtype))

    def body(*refs):
        src, dst = refs[:n], refs[n:2 * n]
        send, recv, loc = refs[2 * n:]
        me = 4 * lax.axis_index("x") + 2 * lax.axis_index("y") + lax.axis_index("c")
        copies = []
        for a in range(n):
            own = pltpu.make_async_copy(src[a], _window(dst[a], kinds[a], me, N_DEV), loc.at[a])
            own.start()
            copies.append(own)
        sends = []
        for k in range(1, N_DEV):
            dev, _ = _peer(k)
            for a in range(n):
                cp = pltpu.make_async_remote_copy(
                    src_ref=src[a], dst_ref=_window(dst[a], kinds[a], me, N_DEV),
                    send_sem=send.at[a * N_DEV + k], recv_sem=recv.at[a * N_DEV + k],
                    device_id=dev, device_id_type=MESH)
                cp.start()
                sends.append(cp)
        for k in range(1, N_DEV):
            dev, pidx = _peer(k)
            for a in range(n):
                pltpu.make_async_remote_copy(
                    src_ref=src[a], dst_ref=_window(dst[a], kinds[a], pidx, N_DEV),
                    send_sem=send.at[a * N_DEV + k], recv_sem=recv.at[a * N_DEV + k],
                    device_id=dev, device_id_type=MESH).wait_recv()
        for cp in sends:
            cp.wait_send()
        for cp in copies:
            cp.wait()

    any_ = pl.BlockSpec(memory_space=pl.ANY)
    return pl.pallas_call(
        body, name=name, in_specs=[any_] * n, out_specs=[any_] * n, out_shape=fulls,
        scratch_shapes=[pltpu.SemaphoreType.DMA((n * N_DEV,)), pltpu.SemaphoreType.DMA((n * N_DEV,)),
                        pltpu.SemaphoreType.DMA((n,))],
        compiler_params=pltpu.CompilerParams(has_side_effects=True),
    )(*shards)


def _all_to_all(name, fulls, kinds):
    n = len(fulls)
    outs = []
    for f, kind in zip(fulls, kinds):
        shp = (f.shape[0], f.shape[1] // N_DEV) if kind == "col" else (f.shape[0] // N_DEV, f.shape[1])
        outs.append(_sds((N_DEV,) + shp, f.dtype))

    def body(*refs):
        src, dst = refs[:n], refs[n:2 * n]
        send, recv, loc = refs[2 * n:]
        me = 4 * lax.axis_index("x") + 2 * lax.axis_index("y") + lax.axis_index("c")
        copies = []
        for a in range(n):
            own = pltpu.make_async_copy(_window(src[a], kinds[a], me, N_DEV), dst[a].at[me], loc.at[a])
            own.start()
            copies.append(own)
        sends = []
        for k in range(1, N_DEV):
            dev, pidx = _peer(k)
            for a in range(n):
                cp = pltpu.make_async_remote_copy(
                    src_ref=_window(src[a], kinds[a], pidx, N_DEV), dst_ref=dst[a].at[me],
                    send_sem=send.at[a * N_DEV + k], recv_sem=recv.at[a * N_DEV + k],
                    device_id=dev, device_id_type=MESH)
                cp.start()
                sends.append(cp)
        for k in range(1, N_DEV):
            dev, pidx = _peer(k)
            for a in range(n):
                pltpu.make_async_remote_copy(
                    src_ref=_window(src[a], kinds[a], pidx, N_DEV), dst_ref=dst[a].at[pidx],
                    send_sem=send.at[a * N_DEV + k], recv_sem=recv.at[a * N_DEV + k],
                    device_id=dev, device_id_type=MESH).wait_recv()
        for cp in sends:
            cp.wait_send()
        for cp in copies:
            cp.wait()

    any_ = pl.BlockSpec(memory_space=pl.ANY)
    return pl.pallas_call(
        body, name=name, in_specs=[any_] * n, out_specs=[any_] * n, out_shape=outs,
        scratch_shapes=[pltpu.SemaphoreType.DMA((n * N_DEV,)), pltpu.SemaphoreType.DMA((n * N_DEV,)),
                        pltpu.SemaphoreType.DMA((n,))],
        compiler_params=pltpu.CompilerParams(has_side_effects=True),
    )(*fulls)


def _adamw(name, parts, w, m, v):
    P, R, C = parts.shape
    sub = 16 if parts.dtype == BF16 else 8
    tr = R if R * C <= (1 << 18) else _tile(R, max(sub, (1 << 18) // C), sub)

    def body(p_ref, w_ref, m_ref, v_ref, g_ref, d_ref, nm_ref, nv_ref):
        g = p_ref[0].astype(F32)
        for s in range(1, P):
            g = g + p_ref[s].astype(F32)
        m2 = ADAM_B1 * m_ref[...] + (1.0 - ADAM_B1) * g
        v2 = ADAM_B2 * v_ref[...] + (1.0 - ADAM_B2) * (g * g)
        m_hat = m2 / (1.0 - ADAM_B1 ** ADAM_STEP)
        v_hat = v2 / (1.0 - ADAM_B2 ** ADAM_STEP)
        g_ref[...] = g
        d_ref[...] = -ADAM_LR * (m_hat / (jnp.sqrt(v_hat) + ADAM_EPS) + ADAM_WD * w_ref[...])
        nm_ref[...] = m2
        nv_ref[...] = v2

    sp = pl.BlockSpec((tr, C), lambda i: (i, 0))
    return pl.pallas_call(
        body, name=name, grid=(R // tr,),
        in_specs=[pl.BlockSpec((P, tr, C), lambda i: (0, i, 0)), sp, sp, sp], out_specs=[sp] * 4,
        out_shape=[_sds((R, C), F32)] * 4, compiler_params=_params(1),
    )(parts, w, m, v)


def _pack(arrs):
    pieces = []
    for a in arrs:
        f = a.reshape(-1).astype(F32)
        pad = (-f.shape[0]) % (8 * LANES)
        pieces.append(jnp.pad(f, (0, pad)) if pad else f)
    return jnp.concatenate(pieces).reshape(-1, LANES)


def _unpack(buf, shapes, lead=()):
    out, row = [], 0
    for shp in shapes:
        size = 1
        for d in shp:
            size *= d
        rows = -(-size // (8 * LANES)) * 8
        piece = buf[..., row:row + rows, :].reshape(lead + (rows * LANES,))[..., :size]
        out.append(piece.reshape(lead + tuple(shp)))
        row += rows
    return out


def kernel(x, norm_tok, w_in, a_re, a_im, log_dt, b_re, b_im, c_re, c_im, d_skip, w_glu, w_ssm_out, conv_w, conv_b, w_conv_out, w_o, norm_ffn, w_up, ffn_conv_w, ffn_conv_b, w_down, norm_final, loss_target, m_norm_tok, m_w_in, m_a_re, m_a_im, m_log_dt, m_b_re, m_b_im, m_c_re, m_c_im, m_d_skip, m_w_glu, m_w_ssm_out, m_conv_w, m_conv_b, m_w_conv_out, m_w_o, m_norm_ffn, m_w_up, m_ffn_conv_w, m_ffn_conv_b, m_w_down, m_norm_final, v_norm_tok, v_w_in, v_a_re, v_a_im, v_log_dt, v_b_re, v_b_im, v_c_re, v_c_im, v_d_skip, v_w_glu, v_w_ssm_out, v_conv_w, v_conv_b, v_w_conv_out, v_w_o, v_norm_ffn, v_w_up, v_ffn_conv_w, v_ffn_conv_b, v_w_down, v_norm_final):
    args = dict(locals())
    L, D = x.shape[1], x.shape[2]
    G, P, H = b_re.shape[1], b_re.shape[2], b_re.shape[3]
    SW = G * H
    CW = conv_b.shape[1]
    FF = ffn_conv_b.shape[1]
    GP = G * P
    nb = SW // LANES
    gpb = LANES // H
    me = 4 * lax.axis_index("x") + 2 * lax.axis_index("y") + lax.axis_index("c")
    tm = _tile(L, 256, 16)
    x2 = x[0]
    tgt = loss_target[0]

    big = [("w_in", "col"), ("w_glu", "row"), ("w_ssm_out", "col"), ("w_conv_out", "col"), ("w_o", "row"),
           ("w_up", "col"), ("w_down", "row")]
    shards = [_cast_bf16("cast_" + n, args[n][0]) for n, _ in big]
    small_in = _pack([conv_w[0], ffn_conv_w[0]])
    gathered = _all_gather("gather_weights", shards + [small_in], [k for _, k in big] + ["row"])
    W = dict(zip([n for n, _ in big], gathered[:-1]))
    cw_parts, fcw_parts = _unpack(gathered[-1].reshape(N_DEV, -1, LANES), [conv_w.shape[1:], ffn_conv_w.shape[1:]], (N_DEV,))
    conv_w_full = jnp.moveaxis(cw_parts, 0, 1).reshape(3, CW)
    ffn_conv_w_full = jnp.moveaxis(fcw_parts, 0, 1).reshape(3, FF)

    ar_row, ai_row = a_re.reshape(1, GP), a_im.reshape(1, GP)
    ldt_row = jnp.broadcast_to(log_dt.reshape(G, 1), (G, P)).reshape(1, GP)
    brt = jnp.transpose(b_re[0], (2, 0, 1)).reshape(H, GP)
    bit = jnp.transpose(b_im[0], (2, 0, 1)).reshape(H, GP)
    abar_re, abar_im, bbar_re, bbar_im = _prep_fwd("s5_prep", ar_row, ai_row, ldt_row, brt, bit)
    eye = jnp.eye(gpb, dtype=F32)

    def b_blocks(bt):
        return jnp.einsum("ab,hjbp->jahbp", eye, bt.reshape(H, nb, gpb, P)).reshape(nb, LANES, gpb * P)

    def c_blocks(c):
        return jnp.einsum("ab,jahp->jbpah", eye, c.reshape(nb, gpb, H, P)).reshape(nb, gpb * P, LANES)

    def diag_blocks(mat):
        return jnp.einsum("jahap->hjap", mat.reshape(nb, gpb, H, gpb, P))

    bm_re, bm_im = b_blocks(bbar_re), b_blocks(bbar_im)
    cm_re, cm_im = c_blocks(c_re[0]), -c_blocks(c_im[0])
    a3_re, a3_im = abar_re.reshape(-1, 8, LANES), abar_im.reshape(-1, 8, LANES)
    dskip_row = d_skip.reshape(1, SW)

    cbs = SW // LANES
    cb_v, cb_gb, cb_gc = cbs, cbs + CW // LANES, cbs + 2 * CW // LANES
    cb_ma = (SW + 3 * CW) // D
    xn = _rms_fwd("rms_tok", x2, norm_tok, tm)
    proj = _mm("proj", xn, W["w_in"], "nn")
    bu_re, bu_im = _ssm_in("s5_bu", proj, bm_re.astype(BF16), bm_im.astype(BF16), tm)
    xs_re, xs_im = _scan("s5_scan", bu_re, bu_im, a3_re, a3_im)
    y, ya = _ssm_out("s5_y", xs_re, xs_im, cm_re.astype(BF16), cm_im.astype(BF16), proj, dskip_row, tm, post=jax.nn.gelu)
    g1 = _mm("glu_gate", ya, W["w_glu"], "nn")
    ya2 = _glu_fwd("glu", y, g1, tm)
    za = _mm("ssm_out", ya2, W["w_ssm_out"], "nn")
    q = _convb_fwd("convb", proj, cb_v, cb_gb, cb_gc, conv_w_full, conv_b)
    zb = _mm("conv_out", q, W["w_conv_out"], "nn")
    merged = _merge_fwd("merge", proj, cb_ma, cb_ma + 1, za, zb, tm)
    o1 = _mm("mix_out", merged, W["w_o"], "nn")
    h1, hn = _res_rms_fwd("rms_ffn", x2, o1, norm_ffn, tm)
    hh = _mm("ffn_up", hn, W["w_up"], "nn")
    f = _ffn_fwd("ffn_act", hh, ffn_conv_w_full, ffn_conv_b)
    o2 = _mm("ffn_down", f, W["w_down"], "nn", tk=1408)
    dh2, dh2b, g_norm_final, loss_part = _final("final", h1, o2, norm_final.reshape(1, D), tgt, tm)

    df = _mm("d_ffn_act", dh2b, W["w_down"], "nt", tn=1408)
    gw_down = _mm("gw_down", f, dh2b, "tn", out_dtype=BF16, tm=1408, tk=512)
    dhh, g_ffn_conv_w, g_ffn_conv_b = _ffn_bwd("ffn_act_bwd", hh, ffn_conv_w_full, ffn_conv_b, df)
    nhalf = lambda t: FF // t
    dhn = _mm("d_ffn_in", dhh, W["w_up"], "nt", tk=_tile(FF, 1024), dims=(L, D, 2 * FF),
              a_spec=lambda a, b, c: pl.BlockSpec((None, a, c), lambda i, j, k: (k // nhalf(c), i, k % nhalf(c))))
    gw_up = _mm("gw_up", hn, dhh, "tn", out_dtype=BF16, tn=_tile(FF, 1024), tk=512, dims=(D, 2 * FF, L),
                b_spec=lambda a, b, c: pl.BlockSpec((None, c, b), lambda i, j, k: (j // nhalf(b), k, j % nhalf(b))))
    dh1, dh1b, g_norm_ffn = _rms_bwd("rms_ffn_bwd", dhn, h1, norm_ffn, dh2, tm, True)

    dmerged = _mm("d_merged", dh1b, W["w_o"], "nt")
    gw_o = _mm("gw_o", merged, dh1b, "tn", out_dtype=BF16, tk=512)
    dza, dzb, dma, dmb = _merge_bwd("merge_bwd", proj, cb_ma, cb_ma + 1, za, zb, dmerged, tm)
    dq = _mm("d_q", dzb, W["w_conv_out"], "nt")
    gw_conv_out = _mm("gw_conv_out", q, dzb, "tn", out_dtype=BF16, tk=512)
    dv, dgb, dgc, g_conv_w, g_conv_b = _convb_bwd("convb_bwd", proj, cb_v, cb_gb, cb_gc, conv_w_full, conv_b, dq)
    dya2 = _mm("d_ya2", dza, W["w_ssm_out"], "nt")
    gw_ssm_out = _mm("gw_ssm_out", ya2, dza, "tn", out_dtype=BF16, tk=512)
    dy_direct, dg1 = _glu_bwd("glu_bwd", y, g1, dya2, tm)
    dya_g = _mm("d_ya_gate", dg1, W["w_glu"], "nt")
    gw_glu = _mm("gw_glu", ya, dg1, "tn", out_dtype=BF16, tk=512)
    dyb, g_dskip = _gelu_bwd("gelu_bwd", y, dy_direct, dya_g, proj, dskip_row, tm)
    dxs_re, dxs_im = _ssm_in("s5_dx", dyb, jnp.swapaxes(cm_re, 1, 2).astype(BF16), jnp.swapaxes(cm_im, 1, 2).astype(BF16), tm)
    gc_re, gc_im = _ssm_dw("s5_gc", dyb, xs_re, xs_im, tm)
    lam_re, lam_im, dab_re, dab_im = _scan("s5_scan_bwd", dxs_re, dxs_im, a3_re, a3_im, xs=(xs_re, xs_im))
    du = _ssm_out("s5_du", lam_re, lam_im, jnp.swapaxes(bm_re, 1, 2).astype(BF16), jnp.swapaxes(bm_im, 1, 2).astype(BF16),
                  dyb, dskip_row, tm, post=lambda t: t)[1]
    gb_re, gb_im = _ssm_dw("s5_gb", proj, lam_re, lam_im, tm)
    g_ar, g_ai, g_ldt, g_brt, g_bit = _prep_bwd(
        "s5_prep_bwd", ar_row, ai_row, ldt_row, brt, bit, dab_re.reshape(1, GP), dab_im.reshape(1, GP),
        diag_blocks(gb_re).reshape(H, GP), diag_blocks(gb_im).reshape(H, GP))
    dproj = jnp.concatenate([du, dv, dgb, dgc, dma, dmb], axis=1)
    dxn = _mm("d_xn", dproj, W["w_in"], "nt")
    gw_in = _mm("gw_in", xn, dproj, "tn", out_dtype=BF16, tk=512)
    grad_x, g_norm_tok = _rms_bwd("rms_tok_bwd", dxn, x2, norm_tok, dh1, tm, False)

    gfull = dict(w_in=gw_in, w_glu=gw_glu, w_ssm_out=gw_ssm_out, w_conv_out=gw_conv_out, w_o=gw_o, w_up=gw_up, w_down=gw_down)
    parts = _all_to_all("exchange_grads", [gfull[n] for n, _ in big], [k for _, k in big])
    small = dict(
        norm_tok=g_norm_tok, a_re=g_ar.reshape(1, G, P), a_im=g_ai.reshape(1, G, P),
        log_dt=g_ldt.reshape(G, P).sum(axis=1).reshape(1, G),
        b_re=jnp.transpose(g_brt.reshape(H, G, P), (1, 2, 0))[None], b_im=jnp.transpose(g_bit.reshape(H, G, P), (1, 2, 0))[None],
        c_re=jnp.transpose(diag_blocks(gc_re), (1, 2, 0, 3)).reshape(1, G, H, P),
        c_im=-jnp.transpose(diag_blocks(gc_im), (1, 2, 0, 3)).reshape(1, G, H, P),
        d_skip=g_dskip.reshape(1, G, H), conv_b=g_conv_b, norm_ffn=g_norm_ffn, ffn_conv_b=g_ffn_conv_b,
        norm_final=g_norm_final.reshape(D), conv_w=g_conv_w[None], ffn_conv_w=g_ffn_conv_w[None])
    rep = ["norm_tok", "a_re", "a_im", "log_dt", "b_re", "b_im", "c_re", "c_im", "d_skip", "conv_b", "norm_ffn",
           "ffn_conv_b", "norm_final"]
    order = rep + ["conv_w", "ffn_conv_w"]
    full_shapes = {n: args[n].shape for n in rep}
    full_shapes["conv_w"], full_shapes["ffn_conv_w"] = (1, 3, CW), (1, 3, FF)
    gpack = _pack([small[n] for n in order])
    rows = gpack.shape[0]
    gall = _all_gather("gather_small_grads", [gpack], ["row"])[0].reshape(N_DEV, rows, LANES)
    rep_rows = _pack([small[n] for n in rep]).shape[0]
    gcw, gfcw = _unpack(gall[:, rep_rows:], [full_shapes["conv_w"], full_shapes["ffn_conv_w"]], (N_DEV,))
    cws, fcws = CW // N_DEV, FF // N_DEV
    gcw = lax.dynamic_slice_in_dim(gcw[:, 0], me * cws, cws, axis=2)
    gfcw = lax.dynamic_slice_in_dim(gfcw[:, 0], me * fcws, fcws, axis=2)

    res = {}
    for (n, _), p in zip(big, parts):
        res[n] = [r[None] for r in _adamw("adamw_" + n, p, args[n][0], args["m_" + n][0], args["v_" + n][0])]
    res["conv_w"] = [r[None] for r in _adamw("adamw_conv_w", gcw, conv_w[0], m_conv_w[0], v_conv_w[0])]
    res["ffn_conv_w"] = [r[None] for r in _adamw("adamw_ffn_conv_w", gfcw, ffn_conv_w[0], m_ffn_conv_w[0], v_ffn_conv_w[0])]
    rep_out = _adamw("adamw_small", gall[:, :rep_rows], _pack([args[n] for n in rep]),
                     _pack([args["m_" + n] for n in rep]), _pack([args["v_" + n] for n in rep]))
    rep_out = [_unpack(r, [full_shapes[n] for n in rep]) for r in rep_out]
    for i, n in enumerate(rep):
        res[n] = [r[i] for r in rep_out]

    loss = lax.psum(loss_part[0, 0], ("x", "y", "c"))
    names = ["norm_tok", "w_in", "a_re", "a_im", "log_dt", "b_re", "b_im", "c_re", "c_im", "d_skip", "w_glu", "w_ssm_out",
             "conv_w", "conv_b", "w_conv_out", "w_o", "norm_ffn", "w_up", "ffn_conv_w", "ffn_conv_b", "w_down", "norm_final"]
    out = [loss, grad_x[None]]
    for slot in range(4):
        out += [res[n][slot] for n in names]
    return tuple(out)
```

```python
import functools

import jax
import jax.numpy as jnp
from jax import lax
from jax.experimental import pallas as pl
from jax.experimental.pallas import tpu as pltpu

F32 = jnp.float32
BF16 = jnp.bfloat16
N_DEV = 8
LANES = 128
SLAB = 4
EPS = 1e-6
ADAM_LR = 0.001
ADAM_B1 = 0.9
ADAM_B2 = 0.999
ADAM_EPS = 1e-08
ADAM_WD = 0.01
ADAM_STEP = 10
VMEM_LIMIT = 56 * 1024 * 1024
MESH = pl.DeviceIdType.MESH


def _tile(n, pref, mult=LANES):
    best = None
    t = mult
    while t <= min(n, pref):
        if n % t == 0:
            best = t
        t += mult
    return best if best is not None else n


def _params(ndim):
    return pltpu.CompilerParams(dimension_semantics=("arbitrary",) * ndim, vmem_limit_bytes=VMEM_LIMIT)


def _sds(shape, dtype):
    return jax.ShapeDtypeStruct(tuple(shape), dtype)


def _mm(name, a, b, mode, *, out_dtype=F32, tm=1024, tn=1024, tk=2048, dims=None, a_spec=None, b_spec=None, dep=None):
    if dims is None:
        if mode == "nn":
            (M, K), N = a.shape, b.shape[1]
        elif mode == "nt":
            (M, K), N = a.shape, b.shape[0]
        else:
            (K, M), N = a.shape, b.shape[1]
    else:
        M, N, K = dims
    tm, tn, tk = _tile(M, tm), _tile(N, tn), _tile(K, tk)
    nk = K // tk
    if mode == "nn":
        dn = (((1,), (0,)), ((), ()))
        sa = pl.BlockSpec((tm, tk), lambda i, j, k: (i, k))
        sb = pl.BlockSpec((tk, tn), lambda i, j, k: (k, j))
    elif mode == "nt":
        dn = (((1,), (1,)), ((), ()))
        sa = pl.BlockSpec((tm, tk), lambda i, j, k: (i, k))
        sb = pl.BlockSpec((tn, tk), lambda i, j, k: (j, k))
    else:
        dn = (((0,), (0,)), ((), ()))
        sa = pl.BlockSpec((tk, tm), lambda i, j, k: (k, i))
        sb = pl.BlockSpec((tk, tn), lambda i, j, k: (k, j))
    sa = a_spec(tm, tn, tk) if a_spec is not None else sa
    sb = b_spec(tm, tn, tk) if b_spec is not None else sb
    use_acc = nk > 1 and out_dtype != F32

    deps = [] if dep is None else [dep]

    def body(a_ref, b_ref, *rest):
        o_ref, acc = rest[len(deps)], rest[len(deps) + 1:]
        k = pl.program_id(2)
        p = lax.dot_general(a_ref[...], b_ref[...], dn, preferred_element_type=F32)
        if nk == 1:
            o_ref[...] = p.astype(out_dtype)
        else:
            tgt = acc[0] if use_acc else o_ref

            @pl.when(k == 0)
            def _():
                tgt[...] = p

            @pl.when(k > 0)
            def _():
                tgt[...] += p

            if use_acc:
                @pl.when(k == nk - 1)
                def _():
                    o_ref[...] = acc[0][...].astype(out_dtype)

    return pl.pallas_call(
        body, name=name, grid=(M // tm, N // tn, nk),
        in_specs=[sa, sb] + [pl.BlockSpec(memory_space=pl.ANY)] * len(deps),
        out_specs=pl.BlockSpec((tm, tn), lambda i, j, k: (i, j)),
        out_shape=_sds((M, N), out_dtype),
        scratch_shapes=[pltpu.VMEM((tm, tn), F32)] if use_acc else [],
        compiler_params=_params(3),
    )(a, b, *deps)


def _rows(name, body, L, tm, ins, outs):
    return pl.pallas_call(
        body, name=name, grid=(L // tm,),
        in_specs=[s for _, s in ins], out_specs=[s for _, s in outs],
        out_shape=[o for o, _ in outs], compiler_params=_params(1),
    )(*[a for a, _ in ins])


def _rs(tm, w, cb=0):
    return pl.BlockSpec((tm, w), lambda i: (i, cb))


def _fs(shape):
    return pl.BlockSpec(tuple(shape), lambda i: (0,) * len(shape))


def _acc_rows(i, ref, part):
    @pl.when(i == 0)
    def _():
        ref[...] = part

    @pl.when(i > 0)
    def _():
        ref[...] += part


def _cast_bf16(name, w):
    R, C = w.shape
    tr = _tile(R, max(16, (1 << 20) // C), 16)

    def body(w_ref, o_ref):
        o_ref[...] = w_ref[...].astype(BF16)

    return _rows(name, body, R, tr, [(w, _rs(tr, C))], [(_sds((R, C), BF16), _rs(tr, C))])[0]


def _rms_fwd(name, x, g, tm):
    L, D = x.shape

    def body(x_ref, g_ref, o_ref):
        xv = x_ref[...]
        r = lax.rsqrt(jnp.mean(xv * xv, axis=-1, keepdims=True) + EPS)
        o_ref[...] = (xv * r * g_ref[...]).astype(BF16)

    return _rows(name, body, L, tm, [(x, _rs(tm, D)), (g, _fs((1, D)))], [(_sds((L, D), BF16), _rs(tm, D))])[0]


def _res_rms_fwd(name, x, o, g, tm):
    L, D = x.shape

    def body(x_ref, o_ref, g_ref, h_ref, hn_ref):
        h = x_ref[...] + o_ref[...]
        r = lax.rsqrt(jnp.mean(h * h, axis=-1, keepdims=True) + EPS)
        h_ref[...] = h
        hn_ref[...] = (h * r * g_ref[...]).astype(BF16)

    return _rows(name, body, L, tm, [(x, _rs(tm, D)), (o, _rs(tm, D)), (g, _fs((1, D)))],
                 [(_sds((L, D), F32), _rs(tm, D)), (_sds((L, D), BF16), _rs(tm, D))])


def _rms_bwd(name, dn, h, g, dres, tm, with_bf16):
    L, D = h.shape

    def body(dn_ref, h_ref, g_ref, dres_ref, dh_ref, *rest):
        i = pl.program_id(0)
        h = h_ref[...]
        r = lax.rsqrt(jnp.mean(h * h, axis=-1, keepdims=True) + EPS)
        xh = h * r
        d = dn_ref[...]
        dxh = d * g_ref[...]
        dh = dres_ref[...] + r * (dxh - xh * jnp.mean(dxh * xh, axis=-1, keepdims=True))
        dh_ref[...] = dh
        if with_bf16:
            rest[0][...] = dh.astype(BF16)
        _acc_rows(i, rest[-1], jnp.sum(d * xh, axis=0, keepdims=True))

    outs = [(_sds((L, D), F32), _rs(tm, D))]
    if with_bf16:
        outs.append((_sds((L, D), BF16), _rs(tm, D)))
    outs.append((_sds((1, D), F32), _fs((1, D))))
    return _rows(name, body, L, tm, [(dn, _rs(tm, D)), (h, _rs(tm, D)), (g, _fs((1, D))), (dres, _rs(tm, D))], outs)


def _final(name, h1, o2, g, tgt, tm):
    L, D = h1.shape

    def body(h1_ref, o2_ref, g_ref, t_ref, dh_ref, dhb_ref, dg_ref, loss_ref):
        i = pl.program_id(0)
        h = h1_ref[...] + o2_ref[...]
        r = lax.rsqrt(jnp.mean(h * h, axis=-1, keepdims=True) + EPS)
        xh = h * r
        gv = g_ref[...]
        e = xh * gv - t_ref[...]
        part = 0.5 * jnp.sum(jnp.mean(e * e, axis=-1, keepdims=True), axis=0, keepdims=True)
        dy = e / D
        dxh = dy * gv
        dh = r * (dxh - xh * jnp.mean(dxh * xh, axis=-1, keepdims=True))
        dh_ref[...] = dh
        dhb_ref[...] = dh.astype(BF16)
        _acc_rows(i, dg_ref, jnp.sum(dy * xh, axis=0, keepdims=True))
        _acc_rows(i, loss_ref, jnp.broadcast_to(part, (8, LANES)))

    return _rows(name, body, L, tm,
                 [(h1, _rs(tm, D)), (o2, _rs(tm, D)), (g, _fs((1, D))), (tgt, _rs(tm, D))],
                 [(_sds((L, D), F32), _rs(tm, D)), (_sds((L, D), BF16), _rs(tm, D)),
                  (_sds((1, D), F32), _fs((1, D))), (_sds((8, LANES), F32), _fs((8, LANES)))])


def _glu_fn(y, g1):
    ya = jax.nn.gelu(y)
    return ya * jax.nn.sigmoid(g1)


def _glu_fwd(name, y, g1, tm):
    L, W = y.shape

    def body(y_ref, g_ref, o_ref):
        o_ref[...] = _glu_fn(y_ref[...], g_ref[...]).astype(BF16)

    return _rows(name, body, L, tm, [(y, _rs(tm, W)), (g1, _rs(tm, W))], [(_sds((L, W), BF16), _rs(tm, W))])[0]


def _glu_bwd(name, y, g1, dya2, tm):
    L, W = y.shape

    def body(y_ref, g_ref, d_ref, dy_ref, dg_ref):
        _, vjp = jax.vjp(_glu_fn, y_ref[...], g_ref[...])
        dy, dg = vjp(d_ref[...])
        dy_ref[...] = dy
        dg_ref[...] = dg.astype(BF16)

    return _rows(name, body, L, tm, [(y, _rs(tm, W)), (g1, _rs(tm, W)), (dya2, _rs(tm, W))],
                 [(_sds((L, W), F32), _rs(tm, W)), (_sds((L, W), BF16), _rs(tm, W))])


def _gelu_bwd(name, y, dy_direct, dya_g, proj, dskip, tm):
    L, W = y.shape

    def body(y_ref, dd_ref, dg_ref, u_ref, dyb_ref, dsk_ref):
        i = pl.program_id(0)
        _, vjp = jax.vjp(jax.nn.gelu, y_ref[...])
        dy = dd_ref[...] + vjp(dg_ref[...])[0]
        dyb_ref[...] = dy.astype(BF16)
        _acc_rows(i, dsk_ref, jnp.sum(dy * u_ref[...], axis=0, keepdims=True))

    del dskip
    return _rows(name, body, L, tm,
                 [(y, _rs(tm, W)), (dy_direct, _rs(tm, W)), (dya_g, _rs(tm, W)), (proj, _rs(tm, W, 0))],
                 [(_sds((L, W), BF16), _rs(tm, W)), (_sds((1, W), F32), _fs((1, W)))])


def _merge_fn(ma, mb, za, zb):
    return jax.nn.sigmoid(ma) * za + jax.nn.sigmoid(mb) * zb


def _merge_fwd(name, proj, cb_a, cb_b, za, zb, tm):
    L, D = za.shape

    def body(ma_ref, mb_ref, za_ref, zb_ref, o_ref):
        o_ref[...] = _merge_fn(ma_ref[...], mb_ref[...], za_ref[...], zb_ref[...]).astype(BF16)

    return _rows(name, body, L, tm,
                 [(proj, _rs(tm, D, cb_a)), (proj, _rs(tm, D, cb_b)), (za, _rs(tm, D)), (zb, _rs(tm, D))],
                 [(_sds((L, D), BF16), _rs(tm, D))])[0]


def _merge_bwd(name, proj, cb_a, cb_b, za, zb, dmerged, tm):
    L, D = za.shape

    def body(ma_ref, mb_ref, za_ref, zb_ref, d_ref, dza_ref, dzb_ref, dma_ref, dmb_ref):
        _, vjp = jax.vjp(_merge_fn, ma_ref[...], mb_ref[...], za_ref[...], zb_ref[...])
        dma, dmb, dza, dzb = vjp(d_ref[...])
        dza_ref[...] = dza.astype(BF16)
        dzb_ref[...] = dzb.astype(BF16)
        dma_ref[...] = dma.astype(BF16)
        dmb_ref[...] = dmb.astype(BF16)

    return _rows(name, body, L, tm,
                 [(proj, _rs(tm, D, cb_a)), (proj, _rs(tm, D, cb_b)), (za, _rs(tm, D)), (zb, _rs(tm, D)),
                  (dmerged, _rs(tm, D))],
                 [(_sds((L, D), BF16), _rs(tm, D)), (_sds((L, D), BF16), _rs(tm, D)),
                  (_sds((L, D), BF16), _rs(tm, D)), (_sds((L, D), BF16), _rs(tm, D))])


def _shift_down(x, k):
    row = lax.broadcasted_iota(jnp.int32, x.shape, 0)
    return jnp.where(row >= k, pltpu.roll(x, k, axis=0), 0.0)


def _shift_up(x, k):
    n = x.shape[0]
    row = lax.broadcasted_iota(jnp.int32, x.shape, 0)
    return jnp.where(row < n - k, pltpu.roll(x, n - k, axis=0), 0.0)


def _conv3(cv, w_ref, b_ref):
    return (w_ref[2:3, :] * cv + w_ref[1:2, :] * _shift_down(cv, 1) + w_ref[0:1, :] * _shift_down(cv, 2)
            + b_ref[...])


def _conv3_bwd(dcc, cv, w_ref):
    dcv = w_ref[2:3, :] * dcc + w_ref[1:2, :] * _shift_up(dcc, 1) + w_ref[0:1, :] * _shift_up(dcc, 2)
    dw = [jnp.sum(dcc * _shift_down(cv, 2), axis=0, keepdims=True),
          jnp.sum(dcc * _shift_down(cv, 1), axis=0, keepdims=True),
          jnp.sum(dcc * cv, axis=0, keepdims=True)]
    db = jnp.sum(dcc, axis=0, keepdims=True)
    return dcv, dw, db


def _store_rows(ref, rows):
    for r, val in enumerate(rows):
        ref[r:r + 1, :] = val


def _cols(name, body, ncb, ins, outs):
    return pl.pallas_call(
        body, name=name, grid=(ncb,),
        in_specs=[s for _, s in ins], out_specs=[s for _, s in outs],
        out_shape=[o for o, _ in outs], compiler_params=_params(1),
    )(*[a for a, _ in ins])


def _cb(L, w, off=0):
    return pl.BlockSpec((L, w), lambda j: (0, j + off))


def _convb_fwd(name, proj, cb_v, cb_gb, cb_gc, w, b):
    L = proj.shape[0]
    W = w.shape[1]
    c = LANES

    def body(v_ref, gb_ref, gc_ref, w_ref, b_ref, q_ref):
        cc = _conv3(gc_ref[...] * v_ref[...], w_ref, b_ref)
        q_ref[...] = (gb_ref[...] * cc).astype(BF16)

    return _cols(name, body, W // c,
                 [(proj, _cb(L, c, cb_v)), (proj, _cb(L, c, cb_gb)), (proj, _cb(L, c, cb_gc)),
                  (w, _cb(3, c)), (b, _cb(1, c))],
                 [(_sds((L, W), BF16), _cb(L, c))])[0]


def _convb_bwd(name, proj, cb_v, cb_gb, cb_gc, w, b, dq):
    L = proj.shape[0]
    W = w.shape[1]
    c = LANES

    def body(v_ref, gb_ref, gc_ref, w_ref, b_ref, dq_ref, dv_ref, dgb_ref, dgc_ref, dw_ref, db_ref):
        v, gc = v_ref[...], gc_ref[...]
        cv = gc * v
        cc = _conv3(cv, w_ref, b_ref)
        dq = dq_ref[...]
        dgb_ref[...] = (dq * cc).astype(BF16)
        dcv, dw, db = _conv3_bwd(dq * gb_ref[...], cv, w_ref)
        dv_ref[...] = (dcv * gc).astype(BF16)
        dgc_ref[...] = (dcv * v).astype(BF16)
        _store_rows(dw_ref, dw)
        db_ref[...] = db

    return _cols(name, body, W // c,
                 [(proj, _cb(L, c, cb_v)), (proj, _cb(L, c, cb_gb)), (proj, _cb(L, c, cb_gc)),
                  (w, _cb(3, c)), (b, _cb(1, c)), (dq, _cb(L, c))],
                 [(_sds((L, W), BF16), _cb(L, c)), (_sds((L, W), BF16), _cb(L, c)), (_sds((L, W), BF16), _cb(L, c)),
                  (_sds((3, W), F32), _cb(3, c)), (_sds((1, W), F32), _cb(1, c))])


def _ffn_fwd(name, hh, w, b):
    L = hh.shape[0]
    Fw = w.shape[1]
    c = LANES
    nf = Fw // c

    def body(a_ref, h2_ref, w_ref, b_ref, f_ref):
        a = _conv3(a_ref[...], w_ref, b_ref)
        f_ref[...] = (jax.nn.gelu(a) * h2_ref[...]).astype(BF16)

    return _cols(name, body, nf, [(hh, _cb(L, c)), (hh, _cb(L, c, nf)), (w, _cb(3, c)), (b, _cb(1, c))],
                 [(_sds((L, Fw), BF16), _cb(L, c))])[0]


def _ffn_bwd(name, hh, w, b, df):
    L = hh.shape[0]
    Fw = w.shape[1]
    c = LANES
    nf = Fw // c

    def body(a_ref, h2_ref, w_ref, b_ref, df_ref, dhh_ref, dw_ref, db_ref):
        h1 = a_ref[...]
        a = _conv3(h1, w_ref, b_ref)
        ga, vjp = jax.vjp(jax.nn.gelu, a)
        d = df_ref[...]
        dhh_ref[1] = (d * ga).astype(BF16)
        da = vjp(d * h2_ref[...])[0]
        dh1, dw, db = _conv3_bwd(da, h1, w_ref)
        dhh_ref[0] = dh1.astype(BF16)
        _store_rows(dw_ref, dw)
        db_ref[...] = db

    return _cols(name, body, nf,
                 [(hh, _cb(L, c)), (hh, _cb(L, c, nf)), (w, _cb(3, c)), (b, _cb(1, c)), (df, _cb(L, c))],
                 [(_sds((2, L, Fw), BF16), pl.BlockSpec((2, L, c), lambda j: (0, 0, j))),
                  (_sds((3, Fw), F32), _cb(3, c)), (_sds((1, Fw), F32), _cb(1, c))])


def _prep_fn(ar, ai, ldt, brt, bit):
    dt = jnp.exp(ldt)
    mag = jnp.exp(dt * ar)
    are = mag * jnp.cos(dt * ai)
    aim = mag * jnp.sin(dt * ai)
    nr = are - 1.0
    ni = aim
    den = ar * ar + ai * ai
    fr = (nr * ar + ni * ai) / den
    fi = (ni * ar - nr * ai) / den
    return are, aim, fr * brt - fi * bit, fr * bit + fi * brt


def _prep_fwd(name, ar, ai, ldt, brt, bit):
    def body(ar_ref, ai_ref, l_ref, br_ref, bi_ref, o1, o2, o3, o4):
        o1[...], o2[...], o3[...], o4[...] = _prep_fn(ar_ref[...], ai_ref[...], l_ref[...], br_ref[...], bi_ref[...])

    return pl.pallas_call(body, name=name,
                          out_shape=[_sds(ar.shape, F32), _sds(ar.shape, F32), _sds(brt.shape, F32), _sds(brt.shape, F32)],
                          )(ar, ai, ldt, brt, bit)


def _prep_bwd(name, ar, ai, ldt, brt, bit, g1, g2, g3, g4):
    def body(ar_ref, ai_ref, l_ref, br_ref, bi_ref, g1_ref, g2_ref, g3_ref, g4_ref, o1, o2, o3, o4, o5):
        _, vjp = jax.vjp(_prep_fn, ar_ref[...], ai_ref[...], l_ref[...], br_ref[...], bi_ref[...])
        o1[...], o2[...], o3[...], o4[...], o5[...] = vjp((g1_ref[...], g2_ref[...], g3_ref[...], g4_ref[...]))

    return pl.pallas_call(body, name=name,
                          out_shape=[_sds(ar.shape, F32)] * 3 + [_sds(brt.shape, F32)] * 2,
                          )(ar, ai, ldt, brt, bit, g1, g2, g3, g4)


def _ssm_in(name, src, m1, m2, tm, dep=None):
    L = src.shape[0]
    nb = m1.shape[0]

    deps = [] if dep is None else [dep]

    def body(s_ref, m1_ref, m2_ref, *rest):
        o1_ref, o2_ref = rest[len(deps):]
        u = s_ref[...].astype(BF16)
        r1 = jnp.dot(u, m1_ref[...], preferred_element_type=F32)
        r2 = jnp.dot(u, m2_ref[...], preferred_element_type=F32)
        for q in range(SLAB):
            o1_ref[q] = r1[:, q * LANES:(q + 1) * LANES]
            o2_ref[q] = r2[:, q * LANES:(q + 1) * LANES]

    ms = pl.BlockSpec((None, LANES, SLAB * LANES), lambda i, j: (j, 0, 0))
    os_ = pl.BlockSpec((SLAB, tm, LANES), lambda i, j: (j, i, 0))
    return pl.pallas_call(
        body, name=name, grid=(L // tm, nb),
        in_specs=[pl.BlockSpec((tm, LANES), lambda i, j: (i, j)), ms, ms] + [pl.BlockSpec(memory_space=pl.ANY)] * len(deps),
        out_specs=[os_, os_],
        out_shape=[_sds((SLAB * nb, L, LANES), F32)] * 2, compiler_params=_params(2),
    )(src, m1, m2, *deps)


def _ssm_out(name, x1, x2, m1, m2, aux, dvec, tm, post=None):
    L = x1.shape[1]
    nb = m1.shape[0]

    def body(x1_ref, x2_ref, m1_ref, m2_ref, a_ref, d_ref, o_ref, *rest):
        a1 = jnp.concatenate([x1_ref[q] for q in range(SLAB)], axis=1).astype(BF16)
        a2 = jnp.concatenate([x2_ref[q] for q in range(SLAB)], axis=1).astype(BF16)
        y = (jnp.dot(a1, m1_ref[...], preferred_element_type=F32) + jnp.dot(a2, m2_ref[...], preferred_element_type=F32)
             + d_ref[...] * a_ref[...].astype(F32))
        o_ref[...] = y
        if post is not None:
            rest[0][...] = post(y).astype(BF16)

    xs = pl.BlockSpec((SLAB, tm, LANES), lambda i, j: (j, i, 0))
    ms = pl.BlockSpec((None, SLAB * LANES, LANES), lambda i, j: (j, 0, 0))
    cs = pl.BlockSpec((tm, LANES), lambda i, j: (i, j))
    W = nb * LANES
    outs, ospecs = [_sds((L, W), F32)], [cs]
    if post is not None:
        outs.append(_sds((L, W), BF16))
        ospecs.append(cs)
    return pl.pallas_call(
        body, name=name, grid=(L // tm, nb),
        in_specs=[xs, xs, ms, ms, cs, pl.BlockSpec((1, LANES), lambda i, j: (0, j))], out_specs=ospecs,
        out_shape=outs, compiler_params=_params(2),
    )(x1, x2, m1, m2, aux, dvec)


def _ssm_dw(name, src, x1, x2, tk):
    L = src.shape[0]
    nb = x1.shape[0] // SLAB
    dn = (((0,), (0,)), ((), ()))

    def body(s_ref, x1_ref, x2_ref, o1_ref, o2_ref):
        k = pl.program_id(1)
        s = s_ref[...].astype(BF16)
        a1 = jnp.concatenate([x1_ref[q] for q in range(SLAB)], axis=1).astype(BF16)
        a2 = jnp.concatenate([x2_ref[q] for q in range(SLAB)], axis=1).astype(BF16)
        _acc_rows(k, o1_ref, lax.dot_general(s, a1, dn, preferred_element_type=F32))
        _acc_rows(k, o2_ref, lax.dot_general(s, a2, dn, preferred_element_type=F32))

    xs = pl.BlockSpec((SLAB, tk, LANES), lambda j, k: (j, k, 0))
    os_ = pl.BlockSpec((None, LANES, SLAB * LANES), lambda j, k: (j, 0, 0))
    return pl.pallas_call(
        body, name=name, grid=(nb, L // tk),
        in_specs=[pl.BlockSpec((tk, LANES), lambda j, k: (k, j)), xs, xs], out_specs=[os_, os_],
        out_shape=[_sds((nb, LANES, SLAB * LANES), F32)] * 2, compiler_params=_params(2),
    )(src, x1, x2)


def _scan(name, b_re, b_im, a_re, a_im, xs=None):
    reverse = xs is not None
    ns, L, _ = b_re.shape
    ng = ns // 8
    tc = min(LANES, L)
    pitch = tc + 8
    nt = L // tc
    n_in = 4 if reverse else 2

    def body(*refs):
        ins = refs[:n_in]
        ar_ref, ai_ref = refs[n_in], refs[n_in + 1]
        o_re, o_im = refs[n_in + 2], refs[n_in + 3]
        k = n_in + 4
        if reverse:
            da_re, da_im = refs[k], refs[k + 1]
            k += 2
        stage = refs[k:k + n_in]
        out_re, out_im, st_re, st_im = refs[k + n_in:k + n_in + 4]
        acc = refs[k + n_in + 4:]
        i = pl.program_id(0)

        @pl.when(i == 0)
        def _():
            st_re[...] = jnp.zeros(st_re.shape, F32)
            st_im[...] = jnp.zeros(st_im.shape, F32)
            for r in acc:
                r[...] = jnp.zeros(r.shape, F32)

        for s in range(ns):
            for src, dst in zip(ins, stage):
                dst[pl.ds(s * pitch, tc), :] = src[s]

        a_r = [ar_ref[g] for g in range(ng)]
        a_i = [ai_ref[g] for g in range(ng)]

        def step(tt, carry):
            t = (tc - 1 - tt) if reverse else tt
            new = []
            for g in range(ng):
                rows = pl.ds(g * 8 * pitch + t, 8, stride=pitch)
                cr, ci = carry[2 * g], carry[2 * g + 1]
                br, bi = stage[0][rows, :], stage[1][rows, :]
                if reverse:
                    xr, xi = stage[2][rows, :], stage[3][rows, :]
                    acc[0][g] += xr * cr + xi * ci
                    acc[1][g] += xr * ci - xi * cr
                    nr = a_r[g] * cr + a_i[g] * ci + br
                    ni = a_r[g] * ci - a_i[g] * cr + bi
                else:
                    nr = a_r[g] * cr - a_i[g] * ci + br
                    ni = a_r[g] * ci + a_i[g] * cr + bi
                out_re[rows, :] = nr
                out_im[rows, :] = ni
                new += [nr, ni]
            return tuple(new)

        init = []
        for g in range(ng):
            init += [st_re[g], st_im[g]]
        fin = lax.fori_loop(0, tc, step, tuple(init), unroll=2)
        for g in range(ng):
            st_re[g] = fin[2 * g]
            st_im[g] = fin[2 * g + 1]
        for s in range(ns):
            o_re[s] = out_re[pl.ds(s * pitch, tc), :]
            o_im[s] = out_im[pl.ds(s * pitch, tc), :]
        if reverse:
            da_re[...] = acc[0][...]
            da_im[...] = acc[1][...]

    tmap = (lambda i: (0, nt - 1 - i, 0)) if reverse else (lambda i: (0, i, 0))
    bs = pl.BlockSpec((ns, tc, LANES), tmap)
    as_ = pl.BlockSpec((ng, 8, LANES), lambda i: (0, 0, 0))
    ins = [b_re, b_im] + (list(xs) if reverse else [])
    out_shape = [_sds((ns, L, LANES), F32)] * 2 + ([_sds((ng, 8, LANES), F32)] * 2 if reverse else [])
    out_specs = [bs, bs] + ([as_, as_] if reverse else [])
    scratch = [pltpu.VMEM((ns * pitch, LANES), F32)] * (n_in + 2) + [pltpu.VMEM((ng, 8, LANES), F32)] * (4 if reverse else 2)
    return pl.pallas_call(
        body, name=name, grid=(nt,), in_specs=[bs] * n_in + [as_, as_], out_specs=out_specs,
        out_shape=out_shape, scratch_shapes=scratch, compiler_params=_params(1),
    )(*ins, a_re, a_im)


def _peer(k):
    x, y, c = lax.axis_index("x"), lax.axis_index("y"), lax.axis_index("c")
    px = 1 - x if (k >> 2) & 1 else x
    py = 1 - y if (k >> 1) & 1 else y
    pc = 1 - c if k & 1 else c
    return (px, py, pc), 4 * px + 2 * py + pc


def _window(ref, kind, idx, n):
    if kind == "col":
        w = ref.shape[1] // n
        return ref.at[:, pl.ds(pl.multiple_of(idx * w, LANES), w)]
    r = ref.shape[0] // n
    return ref.at[pl.ds(pl.multiple_of(idx * r, 8), r), :]


def _all_gather(name, shards, kinds):
    n = len(shards)
    fulls = []
    for s, kind in zip(shards, kinds):
        fulls.append(_sds((s.shape[0], s.shape[1] * N_DEV) if kind == "col" else (s.shape[0] * N_DEV, s.shape[1]), s.dtype))

    def body(*refs):
        src, dst = refs[:n], refs[n:2 * n]
        send, recv, loc = refs[2 * n:]
        me = 4 * lax.axis_index("x") + 2 * lax.axis_index("y") + lax.axis_index("c")
        copies = []
        for a in range(n):
            own = pltpu.make_async_copy(src[a], _window(dst[a], kinds[a], me, N_DEV), loc.at[a])
            own.start()
            copies.append(own)
        sends = []
        for k in range(1, N_DEV):
            dev, _ = _peer(k)
            for a in range(n):
                cp = pltpu.make_async_remote_copy(
                    src_ref=src[a], dst_ref=_window(dst[a], kinds[a], me, N_DEV),
                    send_sem=send.at[a * N_DEV + k], recv_sem=recv.at[a * N_DEV + k],
                    device_id=dev, device_id_type=MESH)
                cp.start()
                sends.append(cp)
        for k in range(1, N_DEV):
            dev, pidx = _peer(k)
            for a in range(n):
                pltpu.make_async_remote_copy(
                    src_ref=src[a], dst_ref=_window(dst[a], kinds[a], pidx, N_DEV),
                    send_sem=send.at[a * N_DEV + k], recv_sem=recv.at[a * N_DEV + k],
                    device_id=dev, device_id_type=MESH).wait_recv()
        for cp in sends:
            cp.wait_send()
        for cp in copies:
            cp.wait()

    any_ = pl.BlockSpec(memory_space=pl.ANY)
    return pl.pallas_call(
        body, name=name, in_specs=[any_] * n, out_specs=[any_] * n, out_shape=fulls,
        scratch_shapes=[pltpu.SemaphoreType.DMA((n * N_DEV,)), pltpu.SemaphoreType.DMA((n * N_DEV,)),
                        pltpu.SemaphoreType.DMA((n,))],
        compiler_params=pltpu.CompilerParams(has_side_effects=True),
    )(*shards)


def _all_to_all(name, fulls, kinds):
    n = len(fulls)
    outs = []
    for f, kind in zip(fulls, kinds):
        shp = (f.shape[0], f.shape[1] // N_DEV) if kind == "col" else (f.shape[0] // N_DEV, f.shape[1])
        outs.append(_sds((N_DEV,) + shp, f.dtype))

    def body(*refs):
        src, dst = refs[:n], refs[n:2 * n]
        send, recv, loc = refs[2 * n:]
        me = 4 * lax.axis_index("x") + 2 * lax.axis_index("y") + lax.axis_index("c")
        copies = []
        for a in range(n):
            own = pltpu.make_async_copy(_window(src[a], kinds[a], me, N_DEV), dst[a].at[me], loc.at[a])
            own.start()
            copies.append(own)
        sends = []
        for k in range(1, N_DEV):
            dev, pidx = _peer(k)
            for a in range(n):
                cp = pltpu.make_async_remote_copy(
                    src_ref=_window(src[a], kinds[a], pidx, N_DEV), dst_ref=dst[a].at[me],
                    send_sem=send.at[a * N_DEV + k], recv_sem=recv.at[a * N_DEV + k],
                    device_id=dev, device_id_type=MESH)
                cp.start()
                sends.append(cp)
        for k in range(1, N_DEV):
            dev, pidx = _peer(k)
            for a in range(n):
                pltpu.make_async_remote_copy(
                    src_ref=_window(src[a], kinds[a], pidx, N_DEV), dst_ref=dst[a].at[pidx],
                    send_sem=send.at[a * N_DEV + k], recv_sem=recv.at[a * N_DEV + k],
                    device_id=dev, device_id_type=MESH).wait_recv()
        for cp in sends:
            cp.wait_send()
        for cp in copies:
            cp.wait()

    any_ = pl.BlockSpec(memory_space=pl.ANY)
    return pl.pallas_call(
        body, name=name, in_specs=[any_] * n, out_specs=[any_] * n, out_shape=outs,
        scratch_shapes=[pltpu.SemaphoreType.DMA((n * N_DEV,)), pltpu.SemaphoreType.DMA((n * N_DEV,)),
                        pltpu.SemaphoreType.DMA((n,))],
        compiler_params=pltpu.CompilerParams(has_side_effects=True),
    )(*fulls)


_HBM = pl.BlockSpec(memory_space=pltpu.HBM)
_SEM = pl.BlockSpec(memory_space=pltpu.SEMAPHORE)
_ANY = pl.BlockSpec(memory_space=pl.ANY)
_EFFECT = pltpu.SideEffectType.DATAFLOW_SIDE_EFFECTING


def _xfer_refs(mode, kinds, a, src, dst, me, pidx):
    if mode == "gather":
        return src[a], _window(dst[a], kinds[a], me, N_DEV), _window(dst[a], kinds[a], pidx, N_DEV)
    return _window(src[a], kinds[a], pidx, N_DEV), dst[a].at[me], dst[a].at[pidx]


def _xfer_out_shapes(mode, arrs, kinds):
    outs = []
    for s, kind in zip(arrs, kinds):
        if mode == "gather":
            outs.append((s.shape[0], s.shape[1] * N_DEV) if kind == "col" else (s.shape[0] * N_DEV, s.shape[1]))
        else:
            outs.append((N_DEV,) + ((s.shape[0], s.shape[1] // N_DEV) if kind == "col" else (s.shape[0] // N_DEV, s.shape[1])))
    return outs


def _xfer_start(name, mode, arrs, kinds, after):
    n = len(arrs)
    shapes = _xfer_out_shapes(mode, arrs, kinds)

    def body(*refs):
        src, dst = refs[:n], refs[n:2 * n]
        send, recv = refs[2 * n + 1], refs[2 * n + 2]
        token, loc = refs[2 * n + 3 + 2 * n], refs[2 * n + 4 + 2 * n]
        me = 4 * lax.axis_index("x") + 2 * lax.axis_index("y") + lax.axis_index("c")
        for k in range(1, N_DEV):
            dev, pidx = _peer(k)
            for a in range(n):
                s, d, _ = _xfer_refs(mode, kinds, a, src, dst, me, pidx)
                pltpu.make_async_remote_copy(src_ref=s, dst_ref=d, send_sem=send.at[a * N_DEV + k],
                                             recv_sem=recv.at[a * N_DEV + k], device_id=dev, device_id_type=MESH).start()
        own = []
        for a in range(n):
            s, _, d = _xfer_refs(mode, kinds, a, src, dst, me, me)
            own.append(pltpu.make_async_copy(s, d, loc.at[a]))
            own[-1].start()
        for cp in own:
            cp.wait()
        token[...] = jnp.zeros(token.shape, F32)

    lands = [pltpu.with_memory_space_constraint(lax.empty(shp, s.dtype), pltpu.HBM) for shp, s in zip(shapes, arrs)]
    srcs = [pltpu.with_memory_space_constraint(s, pltpu.HBM) for s in arrs]
    res = pl.pallas_call(
        body, name=name,
        in_specs=[_HBM] * (2 * n) + [_ANY],
        out_specs=[_SEM, _SEM] + [_HBM] * (2 * n) + [pl.BlockSpec(memory_space=pltpu.VMEM)],
        out_shape=[pltpu.SemaphoreType.DMA((n * N_DEV,)), pltpu.SemaphoreType.DMA((n * N_DEV,))]
        + [pltpu.HBM(s.shape, s.dtype) for s in arrs] + [pltpu.HBM(shp, s.dtype) for shp, s in zip(shapes, arrs)]
        + [_sds((8, LANES), F32)],
        input_output_aliases={i: 2 + i for i in range(2 * n)},
        scratch_shapes=[pltpu.SemaphoreType.DMA((n,))],
        compiler_params=pltpu.CompilerParams(has_side_effects=_EFFECT),
    )(*srcs, *lands, after)
    return dict(mode=mode, kinds=kinds, n=n, send=res[0], recv=res[1], srcs=res[2:2 + n], lands=res[2 + n:2 + 2 * n]), res[-1]


def _xfer_wait(name, st, after):
    n, mode, kinds = st["n"], st["mode"], st["kinds"]

    def body(*refs):
        src, dst = refs[:n], refs[n:2 * n]
        send, recv = refs[2 * n], refs[2 * n + 1]
        me = 4 * lax.axis_index("x") + 2 * lax.axis_index("y") + lax.axis_index("c")
        for k in range(1, N_DEV):
            dev, pidx = _peer(k)
            for a in range(n):
                s, d, land = _xfer_refs(mode, kinds, a, src, dst, me, pidx)
                cp = pltpu.make_async_remote_copy(src_ref=s, dst_ref=land, send_sem=send.at[a * N_DEV + k],
                                                  recv_sem=recv.at[a * N_DEV + k], device_id=dev, device_id_type=MESH)
                cp.wait_send()
                cp.wait_recv()

    res = pl.pallas_call(
        body, name=name,
        in_specs=[_HBM] * (2 * n) + [_SEM, _SEM, _ANY],
        out_specs=[_HBM] * (2 * n),
        out_shape=[pltpu.HBM(s.shape, s.dtype) for s in st["srcs"]] + [pltpu.HBM(s.shape, s.dtype) for s in st["lands"]],
        input_output_aliases={i: i for i in range(2 * n)},
        compiler_params=pltpu.CompilerParams(has_side_effects=_EFFECT),
    )(*st["srcs"], *st["lands"], st["send"], st["recv"], after)
    return list(res[n:])


def _adamw(name, parts, w, m, v):
    P, R, C = parts.shape
    sub = 16 if parts.dtype == BF16 else 8
    tr = R if R * C <= (1 << 18) else _tile(R, max(sub, (1 << 18) // C), sub)

    def body(p_ref, w_ref, m_ref, v_ref, g_ref, d_ref, nm_ref, nv_ref):
        g = p_ref[0].astype(F32)
        for s in range(1, P):
            g = g + p_ref[s].astype(F32)
        m2 = ADAM_B1 * m_ref[...] + (1.0 - ADAM_B1) * g
        v2 = ADAM_B2 * v_ref[...] + (1.0 - ADAM_B2) * (g * g)
        m_hat = m2 / (1.0 - ADAM_B1 ** ADAM_STEP)
        v_hat = v2 / (1.0 - ADAM_B2 ** ADAM_STEP)
        g_ref[...] = g
        d_ref[...] = -ADAM_LR * (m_hat / (jnp.sqrt(v_hat) + ADAM_EPS) + ADAM_WD * w_ref[...])
        nm_ref[...] = m2
        nv_ref[...] = v2

    sp = pl.BlockSpec((tr, C), lambda i: (i, 0))
    return pl.pallas_call(
        body, name=name, grid=(R // tr,),
        in_specs=[pl.BlockSpec((P, tr, C), lambda i: (0, i, 0)), sp, sp, sp], out_specs=[sp] * 4,
        out_shape=[_sds((R, C), F32)] * 4, compiler_params=_params(1),
    )(parts, w, m, v)


def _pack(arrs):
    pieces = []
    for a in arrs:
        f = a.reshape(-1).astype(F32)
        pad = (-f.shape[0]) % (8 * LANES)
        pieces.append(jnp.pad(f, (0, pad)) if pad else f)
    return jnp.concatenate(pieces).reshape(-1, LANES)


def _unpack(buf, shapes, lead=()):
    out, row = [], 0
    for shp in shapes:
        size = 1
        for d in shp:
            size *= d
        rows = -(-size // (8 * LANES)) * 8
        piece = buf[..., row:row + rows, :].reshape(lead + (rows * LANES,))[..., :size]
        out.append(piece.reshape(lead + tuple(shp)))
        row += rows
    return out


def kernel(x, norm_tok, w_in, a_re, a_im, log_dt, b_re, b_im, c_re, c_im, d_skip, w_glu, w_ssm_out, conv_w, conv_b, w_conv_out, w_o, norm_ffn, w_up, ffn_conv_w, ffn_conv_b, w_down, norm_final, loss_target, m_norm_tok, m_w_in, m_a_re, m_a_im, m_log_dt, m_b_re, m_b_im, m_c_re, m_c_im, m_d_skip, m_w_glu, m_w_ssm_out, m_conv_w, m_conv_b, m_w_conv_out, m_w_o, m_norm_ffn, m_w_up, m_ffn_conv_w, m_ffn_conv_b, m_w_down, m_norm_final, v_norm_tok, v_w_in, v_a_re, v_a_im, v_log_dt, v_b_re, v_b_im, v_c_re, v_c_im, v_d_skip, v_w_glu, v_w_ssm_out, v_conv_w, v_conv_b, v_w_conv_out, v_w_o, v_norm_ffn, v_w_up, v_ffn_conv_w, v_ffn_conv_b, v_w_down, v_norm_final):
    args = dict(locals())
    L, D = x.shape[1], x.shape[2]
    G, P, H = b_re.shape[1], b_re.shape[2], b_re.shape[3]
    SW = G * H
    CW = conv_b.shape[1]
    FF = ffn_conv_b.shape[1]
    GP = G * P
    nb = SW // LANES
    gpb = LANES // H
    me = 4 * lax.axis_index("x") + 2 * lax.axis_index("y") + lax.axis_index("c")
    tm = _tile(L, 256, 16)
    x2 = x[0]
    tgt = loss_target[0]

    big = [("w_in", "col"), ("w_glu", "row"), ("w_ssm_out", "col"), ("w_conv_out", "col"), ("w_o", "row"),
           ("w_up", "col"), ("w_down", "row")]
    shards = [_cast_bf16("cast_" + n, args[n][0]) for n, _ in big]
    small_in = _pack([conv_w[0], ffn_conv_w[0]])
    kind = dict(big)
    mixw, ffnw = ["w_glu", "w_ssm_out", "w_conv_out", "w_o"], ["w_up", "w_down"]
    shard = dict(zip([n for n, _ in big], shards))
    gathered = _all_gather("gather_w_in", [shard["w_in"], small_in], ["col", "row"])
    W = {"w_in": gathered[0]}
    st_mix, tok_mix = _xfer_start("gather_mix_start", "gather", [shard[n] for n in mixw], [kind[n] for n in mixw], gathered[0])
    st_ffn, tok_ffn = _xfer_start("gather_ffn_start", "gather", [shard[n] for n in ffnw], [kind[n] for n in ffnw], tok_mix)
    cw_parts, fcw_parts = _unpack(gathered[-1].reshape(N_DEV, -1, LANES), [conv_w.shape[1:], ffn_conv_w.shape[1:]], (N_DEV,))
    conv_w_full = jnp.moveaxis(cw_parts, 0, 1).reshape(3, CW)
    ffn_conv_w_full = jnp.moveaxis(fcw_parts, 0, 1).reshape(3, FF)

    ar_row, ai_row = a_re.reshape(1, GP), a_im.reshape(1, GP)
    ldt_row = jnp.broadcast_to(log_dt.reshape(G, 1), (G, P)).reshape(1, GP)
    brt = jnp.transpose(b_re[0], (2, 0, 1)).reshape(H, GP)
    bit = jnp.transpose(b_im[0], (2, 0, 1)).reshape(H, GP)
    abar_re, abar_im, bbar_re, bbar_im = _prep_fwd("s5_prep", ar_row, ai_row, ldt_row, brt, bit)
    eye = jnp.eye(gpb, dtype=F32)

    def b_blocks(bt):
        return jnp.einsum("ab,hjbp->jahbp", eye, bt.reshape(H, nb, gpb, P)).reshape(nb, LANES, gpb * P)

    def c_blocks(c):
        return jnp.einsum("ab,jahp->jbpah", eye, c.reshape(nb, gpb, H, P)).reshape(nb, gpb * P, LANES)

    def diag_blocks(mat):
        return jnp.einsum("jahap->hjap", mat.reshape(nb, gpb, H, gpb, P))

    bm_re, bm_im = b_blocks(bbar_re), b_blocks(bbar_im)
    cm_re, cm_im = c_blocks(c_re[0]), -c_blocks(c_im[0])
    a3_re, a3_im = abar_re.reshape(-1, 8, LANES), abar_im.reshape(-1, 8, LANES)
    dskip_row = d_skip.reshape(1, SW)

    cbs = SW // LANES
    cb_v, cb_gb, cb_gc = cbs, cbs + CW // LANES, cbs + 2 * CW // LANES
    cb_ma = (SW + 3 * CW) // D
    xn = _rms_fwd("rms_tok", x2, norm_tok, tm)
    proj = _mm("proj", xn, W["w_in"], "nn", dep=tok_ffn)
    bu_re, bu_im = _ssm_in("s5_bu", proj, bm_re.astype(BF16), bm_im.astype(BF16), tm)
    xs_re, xs_im = _scan("s5_scan", bu_re, bu_im, a3_re, a3_im)
    y, ya = _ssm_out("s5_y", xs_re, xs_im, cm_re.astype(BF16), cm_im.astype(BF16), proj, dskip_row, tm, post=jax.nn.gelu)
    W.update(zip(mixw, _xfer_wait("gather_mix_wait", st_mix, ya)))
    g1 = _mm("glu_gate", ya, W["w_glu"], "nn")
    ya2 = _glu_fwd("glu", y, g1, tm)
    za = _mm("ssm_out", ya2, W["w_ssm_out"], "nn")
    q = _convb_fwd("convb", proj, cb_v, cb_gb, cb_gc, conv_w_full, conv_b)
    zb = _mm("conv_out", q, W["w_conv_out"], "nn")
    merged = _merge_fwd("merge", proj, cb_ma, cb_ma + 1, za, zb, tm)
    o1 = _mm("mix_out", merged, W["w_o"], "nn")
    h1, hn = _res_rms_fwd("rms_ffn", x2, o1, norm_ffn, tm)
    W.update(zip(ffnw, _xfer_wait("gather_ffn_wait", st_ffn, hn)))
    hh = _mm("ffn_up", hn, W["w_up"], "nn")
    f = _ffn_fwd("ffn_act", hh, ffn_conv_w_full, ffn_conv_b)
    o2 = _mm("ffn_down", f, W["w_down"], "nn", tk=1408)
    dh2, dh2b, g_norm_final, loss_part = _final("final", h1, o2, norm_final.reshape(1, D), tgt, tm)

    df = _mm("d_ffn_act", dh2b, W["w_down"], "nt", tn=1408)
    gw_down = _mm("gw_down", f, dh2b, "tn", out_dtype=BF16, tm=1408, tk=512)
    dhh, g_ffn_conv_w, g_ffn_conv_b = _ffn_bwd("ffn_act_bwd", hh, ffn_conv_w_full, ffn_conv_b, df)
    nhalf = lambda t: FF // t
    dhn = _mm("d_ffn_in", dhh, W["w_up"], "nt", tk=_tile(FF, 1024), dims=(L, D, 2 * FF),
              a_spec=lambda a, b, c: pl.BlockSpec((None, a, c), lambda i, j, k: (k // nhalf(c), i, k % nhalf(c))))
    gw_up = _mm("gw_up", hn, dhh, "tn", out_dtype=BF16, tn=_tile(FF, 1024), tk=512, dims=(D, 2 * FF, L),
                b_spec=lambda a, b, c: pl.BlockSpec((None, c, b), lambda i, j, k: (j // nhalf(b), k, j % nhalf(b))))
    st_gffn, tok_gffn = _xfer_start("grads_ffn_start", "exchange", [gw_up, gw_down], [kind[n] for n in ffnw], g_ffn_conv_b)
    dh1, dh1b, g_norm_ffn = _rms_bwd("rms_ffn_bwd", dhn, h1, norm_ffn, dh2, tm, True)

    dmerged = _mm("d_merged", dh1b, W["w_o"], "nt", dep=tok_gffn)
    gw_o = _mm("gw_o", merged, dh1b, "tn", out_dtype=BF16, tk=512)
    dza, dzb, dma, dmb = _merge_bwd("merge_bwd", proj, cb_ma, cb_ma + 1, za, zb, dmerged, tm)
    dq = _mm("d_q", dzb, W["w_conv_out"], "nt")
    gw_conv_out = _mm("gw_conv_out", q, dzb, "tn", out_dtype=BF16, tk=512)
    dv, dgb, dgc, g_conv_w, g_conv_b = _convb_bwd("convb_bwd", proj, cb_v, cb_gb, cb_gc, conv_w_full, conv_b, dq)
    dya2 = _mm("d_ya2", dza, W["w_ssm_out"], "nt")
    gw_ssm_out = _mm("gw_ssm_out", ya2, dza, "tn", out_dtype=BF16, tk=512)
    dy_direct, dg1 = _glu_bwd("glu_bwd", y, g1, dya2, tm)
    dya_g = _mm("d_ya_gate", dg1, W["w_glu"], "nt")
    gw_glu = _mm("gw_glu", ya, dg1, "tn", out_dtype=BF16, tk=512)
    st_gmix, tok_gmix = _xfer_start("grads_mix_start", "exchange", [gw_glu, gw_ssm_out, gw_conv_out, gw_o],
                                     [kind[n] for n in mixw], dya_g)
    dyb, g_dskip = _gelu_bwd("gelu_bwd", y, dy_direct, dya_g, proj, dskip_row, tm)
    dxs_re, dxs_im = _ssm_in("s5_dx", dyb, jnp.swapaxes(cm_re, 1, 2).astype(BF16), jnp.swapaxes(cm_im, 1, 2).astype(BF16), tm,
                             dep=tok_gmix)
    gc_re, gc_im = _ssm_dw("s5_gc", dyb, xs_re, xs_im, tm)
    lam_re, lam_im, dab_re, dab_im = _scan("s5_scan_bwd", dxs_re, dxs_im, a3_re, a3_im, xs=(xs_re, xs_im))
    parts = dict(zip(ffnw, _xfer_wait("grads_ffn_wait", st_gffn, dab_re)))
    du = _ssm_out("s5_du", lam_re, lam_im, jnp.swapaxes(bm_re, 1, 2).astype(BF16), jnp.swapaxes(bm_im, 1, 2).astype(BF16),
                  dyb, dskip_row, tm, post=lambda t: t)[1]
    gb_re, gb_im = _ssm_dw("s5_gb", proj, lam_re, lam_im, tm)
    g_ar, g_ai, g_ldt, g_brt, g_bit = _prep_bwd(
        "s5_prep_bwd", ar_row, ai_row, ldt_row, brt, bit, dab_re.reshape(1, GP), dab_im.reshape(1, GP),
        diag_blocks(gb_re).reshape(H, GP), diag_blocks(gb_im).reshape(H, GP))
    dproj = jnp.concatenate([du, dv, dgb, dgc, dma, dmb], axis=1)
    gw_in = _mm("gw_in", xn, dproj, "tn", out_dtype=BF16, tk=512)
    st_gin, tok_gin = _xfer_start("grads_in_start", "exchange", [gw_in], ["col"], g_ar)
    dxn = _mm("d_xn", dproj, W["w_in"], "nt", dep=tok_gin)
    grad_x, g_norm_tok = _rms_bwd("rms_tok_bwd", dxn, x2, norm_tok, dh1, tm, False)

    small = dict(
        norm_tok=g_norm_tok, a_re=g_ar.reshape(1, G, P), a_im=g_ai.reshape(1, G, P),
        log_dt=g_ldt.reshape(G, P).sum(axis=1).reshape(1, G),
        b_re=jnp.transpose(g_brt.reshape(H, G, P), (1, 2, 0))[None], b_im=jnp.transpose(g_bit.reshape(H, G, P), (1, 2, 0))[None],
        c_re=jnp.transpose(diag_blocks(gc_re), (1, 2, 0, 3)).reshape(1, G, H, P),
        c_im=-jnp.transpose(diag_blocks(gc_im), (1, 2, 0, 3)).reshape(1, G, H, P),
        d_skip=g_dskip.reshape(1, G, H), conv_b=g_conv_b, norm_ffn=g_norm_ffn, ffn_conv_b=g_ffn_conv_b,
        norm_final=g_norm_final.reshape(D), conv_w=g_conv_w[None], ffn_conv_w=g_ffn_conv_w[None])
    rep = ["norm_tok", "a_re", "a_im", "log_dt", "b_re", "b_im", "c_re", "c_im", "d_skip", "conv_b", "norm_ffn",
           "ffn_conv_b", "norm_final"]
    order = rep + ["conv_w", "ffn_conv_w"]
    full_shapes = {n: args[n].shape for n in rep}
    full_shapes["conv_w"], full_shapes["ffn_conv_w"] = (1, 3, CW), (1, 3, FF)
    gpack = _pack([small[n] for n in order])
    rows = gpack.shape[0]
    gall = _all_gather("gather_small_grads", [gpack], ["row"])[0].reshape(N_DEV, rows, LANES)
    rep_rows = _pack([small[n] for n in rep]).shape[0]
    gcw, gfcw = _unpack(gall[:, rep_rows:], [full_shapes["conv_w"], full_shapes["ffn_conv_w"]], (N_DEV,))
    cws, fcws = CW // N_DEV, FF // N_DEV
    gcw = lax.dynamic_slice_in_dim(gcw[:, 0], me * cws, cws, axis=2)
    gfcw = lax.dynamic_slice_in_dim(gfcw[:, 0], me * fcws, fcws, axis=2)

    res = {}

    def big_update(n):
        res[n] = [r[None] for r in _adamw("adamw_" + n, parts[n], args[n][0], args["m_" + n][0], args["v_" + n][0])]

    for n in ffnw:
        big_update(n)
    parts.update(zip(mixw, _xfer_wait("grads_mix_wait", st_gmix, grad_x)))
    for n in mixw:
        big_update(n)
    parts["w_in"] = _xfer_wait("grads_in_wait", st_gin, res["w_o"][1])[0]
    big_update("w_in")
    res["conv_w"] = [r[None] for r in _adamw("adamw_conv_w", gcw, conv_w[0], m_conv_w[0], v_conv_w[0])]
    res["ffn_conv_w"] = [r[None] for r in _adamw("adamw_ffn_conv_w", gfcw, ffn_conv_w[0], m_ffn_conv_w[0], v_ffn_conv_w[0])]
    rep_out = _adamw("adamw_small", gall[:, :rep_rows], _pack([args[n] for n in rep]),
                     _pack([args["m_" + n] for n in rep]), _pack([args["v_" + n] for n in rep]))
    rep_out = [_unpack(r, [full_shapes[n] for n in rep]) for r in rep_out]
    for i, n in enumerate(rep):
        res[n] = [r[i] for r in rep_out]

    loss = lax.psum(loss_part[0, 0], ("x", "y", "c"))
    names = ["norm_tok", "w_in", "a_re", "a_im", "log_dt", "b_re", "b_im", "c_re", "c_im", "d_skip", "w_glu", "w_ssm_out",
             "conv_w", "conv_b", "w_conv_out", "w_o", "norm_ffn", "w_up", "ffn_conv_w", "ffn_conv_b", "w_down", "norm_final"]
    out = [loss, grad_x[None]]
    for slot in range(4):
        out += [res[n][slot] for n in names]
    return tuple(out)
```

```python
import functools

import jax
import jax.numpy as jnp
from jax import lax
from jax.experimental import pallas as pl
from jax.experimental.pallas import tpu as pltpu

F32 = jnp.float32
BF16 = jnp.bfloat16
N_DEV = 8
LANES = 128
SLAB = 4
EPS = 1e-6
ADAM_LR = 0.001
ADAM_B1 = 0.9
ADAM_B2 = 0.999
ADAM_EPS = 1e-08
ADAM_WD = 0.01
ADAM_STEP = 10
VMEM_LIMIT = 56 * 1024 * 1024
MESH = pl.DeviceIdType.MESH


def _tile(n, pref, mult=LANES):
    best = None
    t = mult
    while t <= min(n, pref):
        if n % t == 0:
            best = t
        t += mult
    return best if best is not None else n


def _params(ndim):
    return pltpu.CompilerParams(dimension_semantics=("arbitrary",) * ndim, vmem_limit_bytes=VMEM_LIMIT)


def _sds(shape, dtype):
    return jax.ShapeDtypeStruct(tuple(shape), dtype)


def _mm(name, a, b, mode, *, out_dtype=F32, tm=1024, tn=1024, tk=2048, dims=None, a_spec=None, b_spec=None, dep=None):
    if dims is None:
        if mode == "nn":
            (M, K), N = a.shape, b.shape[1]
        elif mode == "nt":
            (M, K), N = a.shape, b.shape[0]
        else:
            (K, M), N = a.shape, b.shape[1]
    else:
        M, N, K = dims
    tm, tn, tk = _tile(M, tm), _tile(N, tn), _tile(K, tk)
    nk = K // tk
    if mode == "nn":
        dn = (((1,), (0,)), ((), ()))
        sa = pl.BlockSpec((tm, tk), lambda i, j, k: (i, k))
        sb = pl.BlockSpec((tk, tn), lambda i, j, k: (k, j))
    elif mode == "nt":
        dn = (((1,), (1,)), ((), ()))
        sa = pl.BlockSpec((tm, tk), lambda i, j, k: (i, k))
        sb = pl.BlockSpec((tn, tk), lambda i, j, k: (j, k))
    else:
        dn = (((0,), (0,)), ((), ()))
        sa = pl.BlockSpec((tk, tm), lambda i, j, k: (k, i))
        sb = pl.BlockSpec((tk, tn), lambda i, j, k: (k, j))
    sa = a_spec(tm, tn, tk) if a_spec is not None else sa
    sb = b_spec(tm, tn, tk) if b_spec is not None else sb
    use_acc = nk > 1 and out_dtype != F32

    deps = [] if dep is None else [dep]

    def body(a_ref, b_ref, *rest):
        o_ref, acc = rest[len(deps)], rest[len(deps) + 1:]
        k = pl.program_id(2)
        p = lax.dot_general(a_ref[...], b_ref[...], dn, preferred_element_type=F32)
        if nk == 1:
            o_ref[...] = p.astype(out_dtype)
        else:
            tgt = acc[0] if use_acc else o_ref

            @pl.when(k == 0)
            def _():
                tgt[...] = p

            @pl.when(k > 0)
            def _():
                tgt[...] += p

            if use_acc:
                @pl.when(k == nk - 1)
                def _():
                    o_ref[...] = acc[0][...].astype(out_dtype)

    return pl.pallas_call(
        body, name=name, grid=(M // tm, N // tn, nk),
        in_specs=[sa, sb] + [pl.BlockSpec(memory_space=pl.ANY)] * len(deps),
        out_specs=pl.BlockSpec((tm, tn), lambda i, j, k: (i, j)),
        out_shape=_sds((M, N), out_dtype),
        scratch_shapes=[pltpu.VMEM((tm, tn), F32)] if use_acc else [],
        compiler_params=_params(3),
    )(a, b, *deps)


def _rows(name, body, L, tm, ins, outs):
    return pl.pallas_call(
        body, name=name, grid=(L // tm,),
        in_specs=[s for _, s in ins], out_specs=[s for _, s in outs],
        out_shape=[o for o, _ in outs], compiler_params=_params(1),
    )(*[a for a, _ in ins])


def _rs(tm, w, cb=0):
    return pl.BlockSpec((tm, w), lambda i: (i, cb))


def _fs(shape):
    return pl.BlockSpec(tuple(shape), lambda i: (0,) * len(shape))


def _acc_rows(i, ref, part):
    @pl.when(i == 0)
    def _():
        ref[...] = part

    @pl.when(i > 0)
    def _():
        ref[...] += part


def _cast_bf16(name, w):
    R, C = w.shape
    tr = _tile(R, max(16, (1 << 20) // C), 16)

    def body(w_ref, o_ref):
        o_ref[...] = w_ref[...].astype(BF16)

    return _rows(name, body, R, tr, [(w, _rs(tr, C))], [(_sds((R, C), BF16), _rs(tr, C))])[0]


def _rms_fwd(name, x, g, tm):
    L, D = x.shape

    def body(x_ref, g_ref, o_ref):
        xv = x_ref[...]
        r = lax.rsqrt(jnp.mean(xv * xv, axis=-1, keepdims=True) + EPS)
        o_ref[...] = (xv * r * g_ref[...]).astype(BF16)

    return _rows(name, body, L, tm, [(x, _rs(tm, D)), (g, _fs((1, D)))], [(_sds((L, D), BF16), _rs(tm, D))])[0]


def _res_rms_fwd(name, x, o, g, tm):
    L, D = x.shape

    def body(x_ref, o_ref, g_ref, h_ref, hn_ref):
        h = x_ref[...] + o_ref[...]
        r = lax.rsqrt(jnp.mean(h * h, axis=-1, keepdims=True) + EPS)
        h_ref[...] = h
        hn_ref[...] = (h * r * g_ref[...]).astype(BF16)

    return _rows(name, body, L, tm, [(x, _rs(tm, D)), (o, _rs(tm, D)), (g, _fs((1, D)))],
                 [(_sds((L, D), F32), _rs(tm, D)), (_sds((L, D), BF16), _rs(tm, D))])


def _rms_bwd(name, dn, h, g, dres, tm, with_bf16):
    L, D = h.shape

    def body(dn_ref, h_ref, g_ref, dres_ref, dh_ref, *rest):
        i = pl.program_id(0)
        h = h_ref[...]
        r = lax.rsqrt(jnp.mean(h * h, axis=-1, keepdims=True) + EPS)
        xh = h * r
        d = dn_ref[...]
        dxh = d * g_ref[...]
        dh = dres_ref[...] + r * (dxh - xh * jnp.mean(dxh * xh, axis=-1, keepdims=True))
        dh_ref[...] = dh
        if with_bf16:
            rest[0][...] = dh.astype(BF16)
        _acc_rows(i, rest[-1], jnp.sum(d * xh, axis=0, keepdims=True))

    outs = [(_sds((L, D), F32), _rs(tm, D))]
    if with_bf16:
        outs.append((_sds((L, D), BF16), _rs(tm, D)))
    outs.append((_sds((1, D), F32), _fs((1, D))))
    return _rows(name, body, L, tm, [(dn, _rs(tm, D)), (h, _rs(tm, D)), (g, _fs((1, D))), (dres, _rs(tm, D))], outs)


def _final(name, h1, o2, g, tgt, tm):
    L, D = h1.shape

    def body(h1_ref, o2_ref, g_ref, t_ref, dh_ref, dhb_ref, dg_ref, loss_ref):
        i = pl.program_id(0)
        h = h1_ref[...] + o2_ref[...]
        r = lax.rsqrt(jnp.mean(h * h, axis=-1, keepdims=True) + EPS)
        xh = h * r
        gv = g_ref[...]
        e = xh * gv - t_ref[...]
        part = 0.5 * jnp.sum(jnp.mean(e * e, axis=-1, keepdims=True), axis=0, keepdims=True)
        dy = e / D
        dxh = dy * gv
        dh = r * (dxh - xh * jnp.mean(dxh * xh, axis=-1, keepdims=True))
        dh_ref[...] = dh
        dhb_ref[...] = dh.astype(BF16)
        _acc_rows(i, dg_ref, jnp.sum(dy * xh, axis=0, keepdims=True))
        _acc_rows(i, loss_ref, jnp.broadcast_to(part, (8, LANES)))

    return _rows(name, body, L, tm,
                 [(h1, _rs(tm, D)), (o2, _rs(tm, D)), (g, _fs((1, D))), (tgt, _rs(tm, D))],
                 [(_sds((L, D), F32), _rs(tm, D)), (_sds((L, D), BF16), _rs(tm, D)),
                  (_sds((1, D), F32), _fs((1, D))), (_sds((8, LANES), F32), _fs((8, LANES)))])


def _glu_fn(y, g1):
    ya = jax.nn.gelu(y)
    return ya * jax.nn.sigmoid(g1)


def _glu_fwd(name, y, g1, tm):
    L, W = y.shape

    def body(y_ref, g_ref, o_ref):
        o_ref[...] = _glu_fn(y_ref[...], g_ref[...]).astype(BF16)

    return _rows(name, body, L, tm, [(y, _rs(tm, W)), (g1, _rs(tm, W))], [(_sds((L, W), BF16), _rs(tm, W))])[0]


def _glu_bwd(name, y, g1, dya2, tm):
    L, W = y.shape

    def body(y_ref, g_ref, d_ref, dy_ref, dg_ref):
        _, vjp = jax.vjp(_glu_fn, y_ref[...], g_ref[...])
        dy, dg = vjp(d_ref[...])
        dy_ref[...] = dy
        dg_ref[...] = dg.astype(BF16)

    return _rows(name, body, L, tm, [(y, _rs(tm, W)), (g1, _rs(tm, W)), (dya2, _rs(tm, W))],
                 [(_sds((L, W), F32), _rs(tm, W)), (_sds((L, W), BF16), _rs(tm, W))])


def _gelu_bwd(name, y, dy_direct, dya_g, proj, dskip, tm):
    L, W = y.shape

    def body(y_ref, dd_ref, dg_ref, u_ref, dyb_ref, dsk_ref):
        i = pl.program_id(0)
        _, vjp = jax.vjp(jax.nn.gelu, y_ref[...])
        dy = dd_ref[...] + vjp(dg_ref[...])[0]
        dyb_ref[...] = dy.astype(BF16)
        _acc_rows(i, dsk_ref, jnp.sum(dy * u_ref[...], axis=0, keepdims=True))

    del dskip
    return _rows(name, body, L, tm,
                 [(y, _rs(tm, W)), (dy_direct, _rs(tm, W)), (dya_g, _rs(tm, W)), (proj, _rs(tm, W, 0))],
                 [(_sds((L, W), BF16), _rs(tm, W)), (_sds((1, W), F32), _fs((1, W)))])


def _merge_fn(ma, mb, za, zb):
    return jax.nn.sigmoid(ma) * za + jax.nn.sigmoid(mb) * zb


def _merge_fwd(name, proj, cb_a, cb_b, za, zb, tm):
    L, D = za.shape

    def body(ma_ref, mb_ref, za_ref, zb_ref, o_ref):
        o_ref[...] = _merge_fn(ma_ref[...], mb_ref[...], za_ref[...], zb_ref[...]).astype(BF16)

    return _rows(name, body, L, tm,
                 [(proj, _rs(tm, D, cb_a)), (proj, _rs(tm, D, cb_b)), (za, _rs(tm, D)), (zb, _rs(tm, D))],
                 [(_sds((L, D), BF16), _rs(tm, D))])[0]


def _merge_bwd(name, proj, cb_a, cb_b, za, zb, dmerged, tm):
    L, D = za.shape

    def body(ma_ref, mb_ref, za_ref, zb_ref, d_ref, dza_ref, dzb_ref, dma_ref, dmb_ref):
        _, vjp = jax.vjp(_merge_fn, ma_ref[...], mb_ref[...], za_ref[...], zb_ref[...])
        dma, dmb, dza, dzb = vjp(d_ref[...])
        dza_ref[...] = dza.astype(BF16)
        dzb_ref[...] = dzb.astype(BF16)
        dma_ref[...] = dma.astype(BF16)
        dmb_ref[...] = dmb.astype(BF16)

    return _rows(name, body, L, tm,
                 [(proj, _rs(tm, D, cb_a)), (proj, _rs(tm, D, cb_b)), (za, _rs(tm, D)), (zb, _rs(tm, D)),
                  (dmerged, _rs(tm, D))],
                 [(_sds((L, D), BF16), _rs(tm, D)), (_sds((L, D), BF16), _rs(tm, D)),
                  (_sds((L, D), BF16), _rs(tm, D)), (_sds((L, D), BF16), _rs(tm, D))])


def _shift_down(x, k):
    row = lax.broadcasted_iota(jnp.int32, x.shape, 0)
    return jnp.where(row >= k, pltpu.roll(x, k, axis=0), 0.0)


def _shift_up(x, k):
    n = x.shape[0]
    row = lax.broadcasted_iota(jnp.int32, x.shape, 0)
    return jnp.where(row < n - k, pltpu.roll(x, n - k, axis=0), 0.0)


def _conv3(cv, w_ref, b_ref):
    return (w_ref[2:3, :] * cv + w_ref[1:2, :] * _shift_down(cv, 1) + w_ref[0:1, :] * _shift_down(cv, 2)
            + b_ref[...])


def _conv3_bwd(dcc, cv, w_ref):
    dcv = w_ref[2:3, :] * dcc + w_ref[1:2, :] * _shift_up(dcc, 1) + w_ref[0:1, :] * _shift_up(dcc, 2)
    dw = [jnp.sum(dcc * _shift_down(cv, 2), axis=0, keepdims=True),
          jnp.sum(dcc * _shift_down(cv, 1), axis=0, keepdims=True),
          jnp.sum(dcc * cv, axis=0, keepdims=True)]
    db = jnp.sum(dcc, axis=0, keepdims=True)
    return dcv, dw, db


def _store_rows(ref, rows):
    for r, val in enumerate(rows):
        ref[r:r + 1, :] = val


def _cols(name, body, ncb, ins, outs):
    return pl.pallas_call(
        body, name=name, grid=(ncb,),
        in_specs=[s for _, s in ins], out_specs=[s for _, s in outs],
        out_shape=[o for o, _ in outs], compiler_params=_params(1),
    )(*[a for a, _ in ins])


def _cb(L, w, off=0):
    return pl.BlockSpec((L, w), lambda j: (0, j + off))


def _convb_fwd(name, proj, cb_v, cb_gb, cb_gc, w, b):
    L = proj.shape[0]
    W = w.shape[1]
    c = LANES

    def body(v_ref, gb_ref, gc_ref, w_ref, b_ref, q_ref):
        cc = _conv3(gc_ref[...] * v_ref[...], w_ref, b_ref)
        q_ref[...] = (gb_ref[...] * cc).astype(BF16)

    return _cols(name, body, W // c,
                 [(proj, _cb(L, c, cb_v)), (proj, _cb(L, c, cb_gb)), (proj, _cb(L, c, cb_gc)),
                  (w, _cb(3, c)), (b, _cb(1, c))],
                 [(_sds((L, W), BF16), _cb(L, c))])[0]


def _convb_bwd(name, proj, cb_v, cb_gb, cb_gc, w, b, dq):
    L = proj.shape[0]
    W = w.shape[1]
    c = LANES

    def body(v_ref, gb_ref, gc_ref, w_ref, b_ref, dq_ref, dv_ref, dgb_ref, dgc_ref, dw_ref, db_ref):
        v, gc = v_ref[...], gc_ref[...]
        cv = gc * v
        cc = _conv3(cv, w_ref, b_ref)
        dq = dq_ref[...]
        dgb_ref[...] = (dq * cc).astype(BF16)
        dcv, dw, db = _conv3_bwd(dq * gb_ref[...], cv, w_ref)
        dv_ref[...] = (dcv * gc).astype(BF16)
        dgc_ref[...] = (dcv * v).astype(BF16)
        _store_rows(dw_ref, dw)
        db_ref[...] = db

    return _cols(name, body, W // c,
                 [(proj, _cb(L, c, cb_v)), (proj, _cb(L, c, cb_gb)), (proj, _cb(L, c, cb_gc)),
                  (w, _cb(3, c)), (b, _cb(1, c)), (dq, _cb(L, c))],
                 [(_sds((L, W), BF16), _cb(L, c)), (_sds((L, W), BF16), _cb(L, c)), (_sds((L, W), BF16), _cb(L, c)),
                  (_sds((3, W), F32), _cb(3, c)), (_sds((1, W), F32), _cb(1, c))])


def _ffn_fwd(name, hh, w, b):
    L = hh.shape[0]
    Fw = w.shape[1]
    c = LANES
    nf = Fw // c

    def body(a_ref, h2_ref, w_ref, b_ref, f_ref):
        a = _conv3(a_ref[...], w_ref, b_ref)
        f_ref[...] = (jax.nn.gelu(a) * h2_ref[...]).astype(BF16)

    return _cols(name, body, nf, [(hh, _cb(L, c)), (hh, _cb(L, c, nf)), (w, _cb(3, c)), (b, _cb(1, c))],
                 [(_sds((L, Fw), BF16), _cb(L, c))])[0]


def _ffn_bwd(name, hh, w, b, df):
    L = hh.shape[0]
    Fw = w.shape[1]
    c = LANES
    nf = Fw // c

    def body(a_ref, h2_ref, w_ref, b_ref, df_ref, dhh_ref, dw_ref, db_ref):
        h1 = a_ref[...]
        a = _conv3(h1, w_ref, b_ref)
        ga, vjp = jax.vjp(jax.nn.gelu, a)
        d = df_ref[...]
        dhh_ref[1] = (d * ga).astype(BF16)
        da = vjp(d * h2_ref[...])[0]
        dh1, dw, db = _conv3_bwd(da, h1, w_ref)
        dhh_ref[0] = dh1.astype(BF16)
        _store_rows(dw_ref, dw)
        db_ref[...] = db

    return _cols(name, body, nf,
                 [(hh, _cb(L, c)), (hh, _cb(L, c, nf)), (w, _cb(3, c)), (b, _cb(1, c)), (df, _cb(L, c))],
                 [(_sds((2, L, Fw), BF16), pl.BlockSpec((2, L, c), lambda j: (0, 0, j))),
                  (_sds((3, Fw), F32), _cb(3, c)), (_sds((1, Fw), F32), _cb(1, c))])


def _prep_fn(ar, ai, ldt, brt, bit):
    dt = jnp.exp(ldt)
    mag = jnp.exp(dt * ar)
    are = mag * jnp.cos(dt * ai)
    aim = mag * jnp.sin(dt * ai)
    nr = are - 1.0
    ni = aim
    den = ar * ar + ai * ai
    fr = (nr * ar + ni * ai) / den
    fi = (ni * ar - nr * ai) / den
    return are, aim, fr * brt - fi * bit, fr * bit + fi * brt


def _prep_fwd(name, ar, ai, ldt, brt, bit):
    def body(ar_ref, ai_ref, l_ref, br_ref, bi_ref, o1, o2, o3, o4):
        o1[...], o2[...], o3[...], o4[...] = _prep_fn(ar_ref[...], ai_ref[...], l_ref[...], br_ref[...], bi_ref[...])

    return pl.pallas_call(body, name=name,
                          out_shape=[_sds(ar.shape, F32), _sds(ar.shape, F32), _sds(brt.shape, F32), _sds(brt.shape, F32)],
                          )(ar, ai, ldt, brt, bit)


def _prep_bwd(name, ar, ai, ldt, brt, bit, g1, g2, g3, g4):
    def body(ar_ref, ai_ref, l_ref, br_ref, bi_ref, g1_ref, g2_ref, g3_ref, g4_ref, o1, o2, o3, o4, o5):
        _, vjp = jax.vjp(_prep_fn, ar_ref[...], ai_ref[...], l_ref[...], br_ref[...], bi_ref[...])
        o1[...], o2[...], o3[...], o4[...], o5[...] = vjp((g1_ref[...], g2_ref[...], g3_ref[...], g4_ref[...]))

    return pl.pallas_call(body, name=name,
                          out_shape=[_sds(ar.shape, F32)] * 3 + [_sds(brt.shape, F32)] * 2,
                          )(ar, ai, ldt, brt, bit, g1, g2, g3, g4)


def _ssm_in(name, src, m1, m2, tm, dep=None):
    L = src.shape[0]
    nb = m1.shape[0]

    deps = [] if dep is None else [dep]

    def body(s_ref, m1_ref, m2_ref, *rest):
        o1_ref, o2_ref = rest[len(deps):]
        u = s_ref[...].astype(BF16)
        r1 = jnp.dot(u, m1_ref[...], preferred_element_type=F32)
        r2 = jnp.dot(u, m2_ref[...], preferred_element_type=F32)
        for q in range(SLAB):
            o1_ref[q] = r1[:, q * LANES:(q + 1) * LANES]
            o2_ref[q] = r2[:, q * LANES:(q + 1) * LANES]

    ms = pl.BlockSpec((None, LANES, SLAB * LANES), lambda i, j: (j, 0, 0))
    os_ = pl.BlockSpec((SLAB, tm, LANES), lambda i, j: (j, i, 0))
    return pl.pallas_call(
        body, name=name, grid=(L // tm, nb),
        in_specs=[pl.BlockSpec((tm, LANES), lambda i, j: (i, j)), ms, ms] + [pl.BlockSpec(memory_space=pl.ANY)] * len(deps),
        out_specs=[os_, os_],
        out_shape=[_sds((SLAB * nb, L, LANES), F32)] * 2, compiler_params=_params(2),
    )(src, m1, m2, *deps)


def _ssm_out(name, x1, x2, m1, m2, aux, dvec, tm, post=None):
    L = x1.shape[1]
    nb = m1.shape[0]

    def body(x1_ref, x2_ref, m1_ref, m2_ref, a_ref, d_ref, o_ref, *rest):
        a1 = jnp.concatenate([x1_ref[q] for q in range(SLAB)], axis=1).astype(BF16)
        a2 = jnp.concatenate([x2_ref[q] for q in range(SLAB)], axis=1).astype(BF16)
        y = (jnp.dot(a1, m1_ref[...], preferred_element_type=F32) + jnp.dot(a2, m2_ref[...], preferred_element_type=F32)
             + d_ref[...] * a_ref[...].astype(F32))
        o_ref[...] = y
        if post is not None:
            rest[0][...] = post(y).astype(BF16)

    xs = pl.BlockSpec((SLAB, tm, LANES), lambda i, j: (j, i, 0))
    ms = pl.BlockSpec((None, SLAB * LANES, LANES), lambda i, j: (j, 0, 0))
    cs = pl.BlockSpec((tm, LANES), lambda i, j: (i, j))
    W = nb * LANES
    outs, ospecs = [_sds((L, W), F32)], [cs]
    if post is not None:
        outs.append(_sds((L, W), BF16))
        ospecs.append(cs)
    return pl.pallas_call(
        body, name=name, grid=(L // tm, nb),
        in_specs=[xs, xs, ms, ms, cs, pl.BlockSpec((1, LANES), lambda i, j: (0, j))], out_specs=ospecs,
        out_shape=outs, compiler_params=_params(2),
    )(x1, x2, m1, m2, aux, dvec)


def _ssm_dw(name, src, x1, x2, tk):
    L = src.shape[0]
    nb = x1.shape[0] // SLAB
    dn = (((0,), (0,)), ((), ()))

    def body(s_ref, x1_ref, x2_ref, o1_ref, o2_ref):
        k = pl.program_id(1)
        s = s_ref[...].astype(BF16)
        a1 = jnp.concatenate([x1_ref[q] for q in range(SLAB)], axis=1).astype(BF16)
        a2 = jnp.concatenate([x2_ref[q] for q in range(SLAB)], axis=1).astype(BF16)
        _acc_rows(k, o1_ref, lax.dot_general(s, a1, dn, preferred_element_type=F32))
        _acc_rows(k, o2_ref, lax.dot_general(s, a2, dn, preferred_element_type=F32))

    xs = pl.BlockSpec((SLAB, tk, LANES), lambda j, k: (j, k, 0))
    os_ = pl.BlockSpec((None, LANES, SLAB * LANES), lambda j, k: (j, 0, 0))
    return pl.pallas_call(
        body, name=name, grid=(nb, L // tk),
        in_specs=[pl.BlockSpec((tk, LANES), lambda j, k: (k, j)), xs, xs], out_specs=[os_, os_],
        out_shape=[_sds((nb, LANES, SLAB * LANES), F32)] * 2, compiler_params=_params(2),
    )(src, x1, x2)


def _scan(name, b_re, b_im, a_re, a_im, xs=None):
    reverse = xs is not None
    ns, L, _ = b_re.shape
    ng = ns // 8
    tc = min(LANES, L)
    pitch = tc + 8
    nt = L // tc
    n_in = 4 if reverse else 2

    def body(*refs):
        ins = refs[:n_in]
        ar_ref, ai_ref = refs[n_in], refs[n_in + 1]
        o_re, o_im = refs[n_in + 2], refs[n_in + 3]
        k = n_in + 4
        if reverse:
            da_re, da_im = refs[k], refs[k + 1]
            k += 2
        stage = refs[k:k + n_in]
        out_re, out_im, st_re, st_im = refs[k + n_in:k + n_in + 4]
        acc = refs[k + n_in + 4:]
        i = pl.program_id(0)

        @pl.when(i == 0)
        def _():
            st_re[...] = jnp.zeros(st_re.shape, F32)
            st_im[...] = jnp.zeros(st_im.shape, F32)
            for r in acc:
                r[...] = jnp.zeros(r.shape, F32)

        for s in range(ns):
            for src, dst in zip(ins, stage):
                dst[pl.ds(s * pitch, tc), :] = src[s]

        a_r = [ar_ref[g] for g in range(ng)]
        a_i = [ai_ref[g] for g in range(ng)]

        def step(tt, carry):
            t = (tc - 1 - tt) if reverse else tt
            new = []
            for g in range(ng):
                rows = pl.ds(g * 8 * pitch + t, 8, stride=pitch)
                cr, ci = carry[2 * g], carry[2 * g + 1]
                br, bi = stage[0][rows, :], stage[1][rows, :]
                if reverse:
                    xr, xi = stage[2][rows, :], stage[3][rows, :]
                    acc[0][g] += xr * cr + xi * ci
                    acc[1][g] += xr * ci - xi * cr
                    nr = a_r[g] * cr + a_i[g] * ci + br
                    ni = a_r[g] * ci - a_i[g] * cr + bi
                else:
                    nr = a_r[g] * cr - a_i[g] * ci + br
                    ni = a_r[g] * ci + a_i[g] * cr + bi
                out_re[rows, :] = nr
                out_im[rows, :] = ni
                new += [nr, ni]
            return tuple(new)

        init = []
        for g in range(ng):
            init += [st_re[g], st_im[g]]
        fin = lax.fori_loop(0, tc, step, tuple(init), unroll=2)
        for g in range(ng):
            st_re[g] = fin[2 * g]
            st_im[g] = fin[2 * g + 1]
        for s in range(ns):
            o_re[s] = out_re[pl.ds(s * pitch, tc), :]
            o_im[s] = out_im[pl.ds(s * pitch, tc), :]
        if reverse:
            da_re[...] = acc[0][...]
            da_im[...] = acc[1][...]

    tmap = (lambda i: (0, nt - 1 - i, 0)) if reverse else (lambda i: (0, i, 0))
    bs = pl.BlockSpec((ns, tc, LANES), tmap)
    as_ = pl.BlockSpec((ng, 8, LANES), lambda i: (0, 0, 0))
    ins = [b_re, b_im] + (list(xs) if reverse else [])
    out_shape = [_sds((ns, L, LANES), F32)] * 2 + ([_sds((ng, 8, LANES), F32)] * 2 if reverse else [])
    out_specs = [bs, bs] + ([as_, as_] if reverse else [])
    scratch = [pltpu.VMEM((ns * pitch, LANES), F32)] * (n_in + 2) + [pltpu.VMEM((ng, 8, LANES), F32)] * (4 if reverse else 2)
    return pl.pallas_call(
        body, name=name, grid=(nt,), in_specs=[bs] * n_in + [as_, as_], out_specs=out_specs,
        out_shape=out_shape, scratch_shapes=scratch, compiler_params=_params(1),
    )(*ins, a_re, a_im)


def _peer(k):
    x, y, c = lax.axis_index("x"), lax.axis_index("y"), lax.axis_index("c")
    px = 1 - x if (k >> 2) & 1 else x
    py = 1 - y if (k >> 1) & 1 else y
    pc = 1 - c if k & 1 else c
    return (px, py, pc), 4 * px + 2 * py + pc


def _window(ref, kind, idx, n):
    if kind == "col":
        w = ref.shape[1] // n
        return ref.at[:, pl.ds(pl.multiple_of(idx * w, LANES), w)]
    r = ref.shape[0] // n
    return ref.at[pl.ds(pl.multiple_of(idx * r, 8), r), :]


def _all_gather(name, shards, kinds):
    n = len(shards)
    fulls = []
    for s, kind in zip(shards, kinds):
        fulls.append(_sds((s.shape[0], s.shape[1] * N_DEV) if kind == "col" else (s.shape[0] * N_DEV, s.shape[1]), s.dtype))

    def body(*refs):
        src, dst = refs[:n], refs[n:2 * n]
        send, recv, loc = refs[2 * n:]
        me = 4 * lax.axis_index("x") + 2 * lax.axis_index("y") + lax.axis_index("c")
        copies = []
        for a in range(n):
            own = pltpu.make_async_copy(src[a], _window(dst[a], kinds[a], me, N_DEV), loc.at[a])
            own.start()
            copies.append(own)
        sends = []
        for k in range(1, N_DEV):
            dev, _ = _peer(k)
            for a in range(n):
                cp = pltpu.make_async_remote_copy(
                    src_ref=src[a], dst_ref=_window(dst[a], kinds[a], me, N_DEV),
                    send_sem=send.at[a * N_DEV + k], recv_sem=recv.at[a * N_DEV + k],
                    device_id=dev, device_id_type=MESH)
                cp.start()
                sends.append(cp)
        for k in range(1, N_DEV):
            dev, pidx = _peer(k)
            for a in range(n):
                pltpu.make_async_remote_copy(
                    src_ref=src[a], dst_ref=_window(dst[a], kinds[a], pidx, N_DEV),
                    send_sem=send.at[a * N_DEV + k], recv_sem=recv.at[a * N_DEV + k],
                    device_id=dev, device_id_type=MESH).wait_recv()
        for cp in sends:
            cp.wait_send()
        for cp in copies:
            cp.wait()

    any_ = pl.BlockSpec(memory_space=pl.ANY)
    return pl.pallas_call(
        body, name=name, in_specs=[any_] * n, out_specs=[any_] * n, out_shape=fulls,
        scratch_shapes=[pltpu.SemaphoreType.DMA((n * N_DEV,)), pltpu.SemaphoreType.DMA((n * N_DEV,)),
                        pltpu.SemaphoreType.DMA((n,))],
        compiler_params=pltpu.CompilerParams(has_side_effects=True),
    )(*shards)


_HBM = pl.BlockSpec(memory_space=pltpu.HBM)
_SEM = pl.BlockSpec(memory_space=pltpu.SEMAPHORE)
_ANY = pl.BlockSpec(memory_space=pl.ANY)
_EFFECT = pltpu.SideEffectType.DATAFLOW_SIDE_EFFECTING


def _xfer_refs(mode, kinds, a, src, dst, me, pidx):
    if mode == "gather":
        return src[a], _window(dst[a], kinds[a], me, N_DEV), _window(dst[a], kinds[a], pidx, N_DEV)
    return _window(src[a], kinds[a], pidx, N_DEV), dst[a].at[me], dst[a].at[pidx]


def _xfer_out_shapes(mode, arrs, kinds):
    outs = []
    for s, kind in zip(arrs, kinds):
        if mode == "gather":
            outs.append((s.shape[0], s.shape[1] * N_DEV) if kind == "col" else (s.shape[0] * N_DEV, s.shape[1]))
        else:
            outs.append((N_DEV,) + ((s.shape[0], s.shape[1] // N_DEV) if kind == "col" else (s.shape[0] // N_DEV, s.shape[1])))
    return outs


def _xfer_start(name, mode, arrs, kinds, after):
    n = len(arrs)
    shapes = _xfer_out_shapes(mode, arrs, kinds)

    def body(*refs):
        src, dst = refs[:n], refs[n:2 * n]
        send, recv = refs[2 * n + 1], refs[2 * n + 2]
        token, loc = refs[2 * n + 3 + 2 * n], refs[2 * n + 4 + 2 * n]
        me = 4 * lax.axis_index("x") + 2 * lax.axis_index("y") + lax.axis_index("c")
        own = []
        for a in range(n):
            s, _, d = _xfer_refs(mode, kinds, a, src, dst, me, me)
            own.append(pltpu.make_async_copy(s, d, loc.at[a]))
            own[-1].start()
        for cp in own:
            cp.wait()
        for k in range(1, N_DEV):
            dev, pidx = _peer(k)
            for a in range(n):
                s, d, _ = _xfer_refs(mode, kinds, a, src, dst, me, pidx)
                pltpu.make_async_remote_copy(src_ref=s, dst_ref=d, send_sem=send.at[a * N_DEV + k],
                                             recv_sem=recv.at[a * N_DEV + k], device_id=dev, device_id_type=MESH).start()
        token[...] = jnp.zeros(token.shape, F32)

    lands = [pltpu.with_memory_space_constraint(lax.empty(shp, s.dtype), pltpu.HBM) for shp, s in zip(shapes, arrs)]
    srcs = [pltpu.with_memory_space_constraint(s, pltpu.HBM) for s in arrs]
    res = pl.pallas_call(
        body, name=name,
        in_specs=[_HBM] * (2 * n) + [_ANY],
        out_specs=[_SEM, _SEM] + [_HBM] * (2 * n) + [pl.BlockSpec(memory_space=pltpu.VMEM)],
        out_shape=[pltpu.SemaphoreType.DMA((n * N_DEV,)), pltpu.SemaphoreType.DMA((n * N_DEV,))]
        + [pltpu.HBM(s.shape, s.dtype) for s in arrs] + [pltpu.HBM(shp, s.dtype) for shp, s in zip(shapes, arrs)]
        + [_sds((8, LANES), F32)],
        input_output_aliases={i: 2 + i for i in range(2 * n)},
        scratch_shapes=[pltpu.SemaphoreType.DMA((n,))],
        compiler_params=pltpu.CompilerParams(has_side_effects=_EFFECT),
    )(*srcs, *lands, after)
    return dict(mode=mode, kinds=kinds, n=n, send=res[0], recv=res[1], srcs=res[2:2 + n], lands=res[2 + n:2 + 2 * n]), res[-1]


def _xfer_wait(name, st, after):
    n, mode, kinds = st["n"], st["mode"], st["kinds"]

    def body(*refs):
        src, dst = refs[:n], refs[n:2 * n]
        send, recv = refs[2 * n], refs[2 * n + 1]
        me = 4 * lax.axis_index("x") + 2 * lax.axis_index("y") + lax.axis_index("c")
        for k in range(1, N_DEV):
            dev, pidx = _peer(k)
            for a in range(n):
                s, d, land = _xfer_refs(mode, kinds, a, src, dst, me, pidx)
                cp = pltpu.make_async_remote_copy(src_ref=s, dst_ref=land, send_sem=send.at[a * N_DEV + k],
                                                  recv_sem=recv.at[a * N_DEV + k], device_id=dev, device_id_type=MESH)
                cp.wait_send()
                cp.wait_recv()

    res = pl.pallas_call(
        body, name=name,
        in_specs=[_HBM] * (2 * n) + [_SEM, _SEM, _ANY],
        out_specs=[_HBM] * (2 * n),
        out_shape=[pltpu.HBM(s.shape, s.dtype) for s in st["srcs"]] + [pltpu.HBM(s.shape, s.dtype) for s in st["lands"]],
        input_output_aliases={i: i for i in range(2 * n)},
        compiler_params=pltpu.CompilerParams(has_side_effects=_EFFECT),
    )(*st["srcs"], *st["lands"], st["send"], st["recv"], after)
    return list(res[n:])


def _adamw(name, parts, w, m, v):
    P, R, C = parts.shape
    sub = 16 if parts.dtype == BF16 else 8
    tr = R if R * C <= (1 << 18) else _tile(R, max(sub, (1 << 18) // C), sub)

    def body(p_ref, w_ref, m_ref, v_ref, g_ref, d_ref, nm_ref, nv_ref):
        g = p_ref[0].astype(F32)
        for s in range(1, P):
            g = g + p_ref[s].astype(F32)
        m2 = ADAM_B1 * m_ref[...] + (1.0 - ADAM_B1) * g
        v2 = ADAM_B2 * v_ref[...] + (1.0 - ADAM_B2) * (g * g)
        m_hat = m2 / (1.0 - ADAM_B1 ** ADAM_STEP)
        v_hat = v2 / (1.0 - ADAM_B2 ** ADAM_STEP)
        g_ref[...] = g
        d_ref[...] = -ADAM_LR * (m_hat / (jnp.sqrt(v_hat) + ADAM_EPS) + ADAM_WD * w_ref[...])
        nm_ref[...] = m2
        nv_ref[...] = v2

    sp = pl.BlockSpec((tr, C), lambda i: (i, 0))
    return pl.pallas_call(
        body, name=name, grid=(R // tr,),
        in_specs=[pl.BlockSpec((P, tr, C), lambda i: (0, i, 0)), sp, sp, sp], out_specs=[sp] * 4,
        out_shape=[_sds((R, C), F32)] * 4, compiler_params=_params(1),
    )(parts, w, m, v)


def _pack(arrs):
    pieces = []
    for a in arrs:
        f = a.reshape(-1).astype(F32)
        pad = (-f.shape[0]) % (8 * LANES)
        pieces.append(jnp.pad(f, (0, pad)) if pad else f)
    return jnp.concatenate(pieces).reshape(-1, LANES)


def _unpack(buf, shapes, lead=()):
    out, row = [], 0
    for shp in shapes:
        size = 1
        for d in shp:
            size *= d
        rows = -(-size // (8 * LANES)) * 8
        piece = buf[..., row:row + rows, :].reshape(lead + (rows * LANES,))[..., :size]
        out.append(piece.reshape(lead + tuple(shp)))
        row += rows
    return out


def kernel(x, norm_tok, w_in, a_re, a_im, log_dt, b_re, b_im, c_re, c_im, d_skip, w_glu, w_ssm_out, conv_w, conv_b, w_conv_out, w_o, norm_ffn, w_up, ffn_conv_w, ffn_conv_b, w_down, norm_final, loss_target, m_norm_tok, m_w_in, m_a_re, m_a_im, m_log_dt, m_b_re, m_b_im, m_c_re, m_c_im, m_d_skip, m_w_glu, m_w_ssm_out, m_conv_w, m_conv_b, m_w_conv_out, m_w_o, m_norm_ffn, m_w_up, m_ffn_conv_w, m_ffn_conv_b, m_w_down, m_norm_final, v_norm_tok, v_w_in, v_a_re, v_a_im, v_log_dt, v_b_re, v_b_im, v_c_re, v_c_im, v_d_skip, v_w_glu, v_w_ssm_out, v_conv_w, v_conv_b, v_w_conv_out, v_w_o, v_norm_ffn, v_w_up, v_ffn_conv_w, v_ffn_conv_b, v_w_down, v_norm_final):
    args = dict(locals())
    L, D = x.shape[1], x.shape[2]
    G, P, H = b_re.shape[1], b_re.shape[2], b_re.shape[3]
    SW = G * H
    CW = conv_b.shape[1]
    FF = ffn_conv_b.shape[1]
    GP = G * P
    nb = SW // LANES
    gpb = LANES // H
    me = 4 * lax.axis_index("x") + 2 * lax.axis_index("y") + lax.axis_index("c")
    tm = _tile(L, 256, 16)
    x2 = x[0]
    tgt = loss_target[0]

    big = [("w_in", "col"), ("w_glu", "row"), ("w_ssm_out", "col"), ("w_conv_out", "col"), ("w_o", "row"),
           ("w_up", "col"), ("w_down", "row")]
    shards = [_cast_bf16("cast_" + n, args[n][0]) for n, _ in big]
    small_in = _pack([conv_w[0], ffn_conv_w[0]])
    kind = dict(big)
    mixw, ffnw = ["w_glu", "w_ssm_out", "w_conv_out", "w_o"], ["w_up", "w_down"]
    shard = dict(zip([n for n, _ in big], shards))
    gathered = _all_gather("gather_w_in", [shard["w_in"], small_in], ["col", "row"])
    W = {"w_in": gathered[0]}
    st_mix, tok_mix = _xfer_start("gather_mix_start", "gather", [shard[n] for n in mixw], [kind[n] for n in mixw], gathered[0])
    st_ffn, tok_ffn = _xfer_start("gather_ffn_start", "gather", [shard[n] for n in ffnw], [kind[n] for n in ffnw], tok_mix)
    cw_parts, fcw_parts = _unpack(gathered[-1].reshape(N_DEV, -1, LANES), [conv_w.shape[1:], ffn_conv_w.shape[1:]], (N_DEV,))
    conv_w_full = jnp.moveaxis(cw_parts, 0, 1).reshape(3, CW)
    ffn_conv_w_full = jnp.moveaxis(fcw_parts, 0, 1).reshape(3, FF)

    ar_row, ai_row = a_re.reshape(1, GP), a_im.reshape(1, GP)
    ldt_row = jnp.broadcast_to(log_dt.reshape(G, 1), (G, P)).reshape(1, GP)
    brt = jnp.transpose(b_re[0], (2, 0, 1)).reshape(H, GP)
    bit = jnp.transpose(b_im[0], (2, 0, 1)).reshape(H, GP)
    abar_re, abar_im, bbar_re, bbar_im = _prep_fwd("s5_prep", ar_row, ai_row, ldt_row, brt, bit)
    eye = jnp.eye(gpb, dtype=F32)

    def b_blocks(bt):
        return jnp.einsum("ab,hjbp->jahbp", eye, bt.reshape(H, nb, gpb, P)).reshape(nb, LANES, gpb * P)

    def c_blocks(c):
        return jnp.einsum("ab,jahp->jbpah", eye, c.reshape(nb, gpb, H, P)).reshape(nb, gpb * P, LANES)

    def diag_blocks(mat):
        return jnp.einsum("jahap->hjap", mat.reshape(nb, gpb, H, gpb, P))

    bm_re, bm_im = b_blocks(bbar_re), b_blocks(bbar_im)
    cm_re, cm_im = c_blocks(c_re[0]), -c_blocks(c_im[0])
    a3_re, a3_im = abar_re.reshape(-1, 8, LANES), abar_im.reshape(-1, 8, LANES)
    dskip_row = d_skip.reshape(1, SW)

    cbs = SW // LANES
    cb_v, cb_gb, cb_gc = cbs, cbs + CW // LANES, cbs + 2 * CW // LANES
    cb_ma = (SW + 3 * CW) // D
    xn = _rms_fwd("rms_tok", x2, norm_tok, tm)
    proj = _mm("proj", xn, W["w_in"], "nn", dep=tok_ffn)
    ts, tw = _tile(L, 512, 16), _tile(L, 1024, 16)
    bu_re, bu_im = _ssm_in("s5_bu", proj, bm_re.astype(BF16), bm_im.astype(BF16), ts)
    xs_re, xs_im = _scan("s5_scan", bu_re, bu_im, a3_re, a3_im)
    y, ya = _ssm_out("s5_y", xs_re, xs_im, cm_re.astype(BF16), cm_im.astype(BF16), proj, dskip_row, ts, post=jax.nn.gelu)
    W.update(zip(mixw, _xfer_wait("gather_mix_wait", st_mix, ya)))
    g1 = _mm("glu_gate", ya, W["w_glu"], "nn")
    ya2 = _glu_fwd("glu", y, g1, tm)
    za = _mm("ssm_out", ya2, W["w_ssm_out"], "nn")
    q = _convb_fwd("convb", proj, cb_v, cb_gb, cb_gc, conv_w_full, conv_b)
    zb = _mm("conv_out", q, W["w_conv_out"], "nn")
    merged = _merge_fwd("merge", proj, cb_ma, cb_ma + 1, za, zb, tm)
    o1 = _mm("mix_out", merged, W["w_o"], "nn")
    h1, hn = _res_rms_fwd("rms_ffn", x2, o1, norm_ffn, tm)
    W.update(zip(ffnw, _xfer_wait("gather_ffn_wait", st_ffn, hn)))
    hh = _mm("ffn_up", hn, W["w_up"], "nn")
    f = _ffn_fwd("ffn_act", hh, ffn_conv_w_full, ffn_conv_b)
    o2 = _mm("ffn_down", f, W["w_down"], "nn", tk=2816)
    dh2, dh2b, g_norm_final, loss_part = _final("final", h1, o2, norm_final.reshape(1, D), tgt, tm)

    df = _mm("d_ffn_act", dh2b, W["w_down"], "nt", tn=1408)
    gw_down = _mm("gw_down", f, dh2b, "tn", out_dtype=BF16, tm=1408, tn=512, tk=L)
    dhh, g_ffn_conv_w, g_ffn_conv_b = _ffn_bwd("ffn_act_bwd", hh, ffn_conv_w_full, ffn_conv_b, df)
    nhalf = lambda t: FF // t
    dhn = _mm("d_ffn_in", dhh, W["w_up"], "nt", tk=_tile(FF, 2816), dims=(L, D, 2 * FF),
              a_spec=lambda a, b, c: pl.BlockSpec((None, a, c), lambda i, j, k: (k // nhalf(c), i, k % nhalf(c))))
    gw_up = _mm("gw_up", hn, dhh, "tn", out_dtype=BF16, tn=_tile(FF, 1024), tk=L, dims=(D, 2 * FF, L),
                b_spec=lambda a, b, c: pl.BlockSpec((None, c, b), lambda i, j, k: (j // nhalf(b), k, j % nhalf(b))))
    st_gffn, tok_gffn = _xfer_start("grads_ffn_start", "exchange", [gw_up, gw_down], [kind[n] for n in ffnw], g_ffn_conv_b)
    dh1, dh1b, g_norm_ffn = _rms_bwd("rms_ffn_bwd", dhn, h1, norm_ffn, dh2, tm, True)

    dmerged = _mm("d_merged", dh1b, W["w_o"], "nt", dep=tok_gffn)
    gw_o = _mm("gw_o", merged, dh1b, "tn", out_dtype=BF16, tk=L)
    dza, dzb, dma, dmb = _merge_bwd("merge_bwd", proj, cb_ma, cb_ma + 1, za, zb, dmerged, tm)
    dq = _mm("d_q", dzb, W["w_conv_out"], "nt")
    gw_conv_out = _mm("gw_conv_out", q, dzb, "tn", out_dtype=BF16, tk=L)
    dv, dgb, dgc, g_conv_w, g_conv_b = _convb_bwd("convb_bwd", proj, cb_v, cb_gb, cb_gc, conv_w_full, conv_b, dq)
    dya2 = _mm("d_ya2", dza, W["w_ssm_out"], "nt")
    gw_ssm_out = _mm("gw_ssm_out", ya2, dza, "tn", out_dtype=BF16, tk=L)
    dy_direct, dg1 = _glu_bwd("glu_bwd", y, g1, dya2, tm)
    dya_g = _mm("d_ya_gate", dg1, W["w_glu"], "nt")
    gw_glu = _mm("gw_glu", ya, dg1, "tn", out_dtype=BF16, tk=L)
    st_gmix, tok_gmix = _xfer_start("grads_mix_start", "exchange", [gw_glu, gw_ssm_out, gw_conv_out, gw_o],
                                     [kind[n] for n in mixw], dya_g)
    dyb, g_dskip = _gelu_bwd("gelu_bwd", y, dy_direct, dya_g, proj, dskip_row, tm)
    dxs_re, dxs_im = _ssm_in("s5_dx", dyb, jnp.swapaxes(cm_re, 1, 2).astype(BF16), jnp.swapaxes(cm_im, 1, 2).astype(BF16), ts,
                             dep=tok_gmix)
    gc_re, gc_im = _ssm_dw("s5_gc", dyb, xs_re, xs_im, tw)
    lam_re, lam_im, dab_re, dab_im = _scan("s5_scan_bwd", dxs_re, dxs_im, a3_re, a3_im, xs=(xs_re, xs_im))
    parts = dict(zip(ffnw, _xfer_wait("grads_ffn_wait", st_gffn, dab_re)))
    du = _ssm_out("s5_du", lam_re, lam_im, jnp.swapaxes(bm_re, 1, 2).astype(BF16), jnp.swapaxes(bm_im, 1, 2).astype(BF16),
                  dyb, dskip_row, ts, post=lambda t: t)[1]
    gb_re, gb_im = _ssm_dw("s5_gb", proj, lam_re, lam_im, tw)
    g_ar, g_ai, g_ldt, g_brt, g_bit = _prep_bwd(
        "s5_prep_bwd", ar_row, ai_row, ldt_row, brt, bit, dab_re.reshape(1, GP), dab_im.reshape(1, GP),
        diag_blocks(gb_re).reshape(H, GP), diag_blocks(gb_im).reshape(H, GP))
    small = dict(
        a_re=g_ar.reshape(1, G, P), a_im=g_ai.reshape(1, G, P),
        log_dt=g_ldt.reshape(G, P).sum(axis=1).reshape(1, G),
        b_re=jnp.transpose(g_brt.reshape(H, G, P), (1, 2, 0))[None], b_im=jnp.transpose(g_bit.reshape(H, G, P), (1, 2, 0))[None],
        c_re=jnp.transpose(diag_blocks(gc_re), (1, 2, 0, 3)).reshape(1, G, H, P),
        c_im=-jnp.transpose(diag_blocks(gc_im), (1, 2, 0, 3)).reshape(1, G, H, P),
        d_skip=g_dskip.reshape(1, G, H), conv_b=g_conv_b, norm_ffn=g_norm_ffn, ffn_conv_b=g_ffn_conv_b,
        norm_final=g_norm_final.reshape(D), conv_w=g_conv_w[None], ffn_conv_w=g_ffn_conv_w[None])
    rep = ["a_re", "a_im", "log_dt", "b_re", "b_im", "c_re", "c_im", "d_skip", "conv_b", "norm_ffn", "ffn_conv_b", "norm_final"]
    order = rep + ["conv_w", "ffn_conv_w"]
    full_shapes = {n: args[n].shape for n in rep}
    full_shapes["conv_w"], full_shapes["ffn_conv_w"] = (1, 3, CW), (1, 3, FF)
    gpack = _pack([small[n] for n in order])
    rows = gpack.shape[0]
    st_small, tok_small = _xfer_start("grads_small_start", "gather", [gpack], ["row"], g_ar)

    dproj = jnp.concatenate([du, dv, dgb, dgc, dma, dmb], axis=1)
    gw_in = _mm("gw_in", xn, dproj, "tn", out_dtype=BF16, tk=L, dep=tok_small)
    st_gin, tok_gin = _xfer_start("grads_in_start", "exchange", [gw_in], ["col"], tok_small)
    dxn = _mm("d_xn", dproj, W["w_in"], "nt", dep=tok_gin)
    grad_x, g_norm_tok = _rms_bwd("rms_tok_bwd", dxn, x2, norm_tok, dh1, tm, False)
    nt_all = _all_gather("gather_norm_tok_grad", [_pack([g_norm_tok])], ["row"])[0].reshape(N_DEV, -1, LANES)
    gall = _xfer_wait("grads_small_wait", st_small, nt_all)[0].reshape(N_DEV, rows, LANES)
    rep_rows = _pack([small[n] for n in rep]).shape[0]
    gcw, gfcw = _unpack(gall[:, rep_rows:], [full_shapes["conv_w"], full_shapes["ffn_conv_w"]], (N_DEV,))
    cws, fcws = CW // N_DEV, FF // N_DEV
    gcw = lax.dynamic_slice_in_dim(gcw[:, 0], me * cws, cws, axis=2)
    gfcw = lax.dynamic_slice_in_dim(gfcw[:, 0], me * fcws, fcws, axis=2)

    res = {}

    def big_update(n):
        res[n] = [r[None] for r in _adamw("adamw_" + n, parts[n], args[n][0], args["m_" + n][0], args["v_" + n][0])]

    for n in ffnw:
        big_update(n)
    parts.update(zip(mixw, _xfer_wait("grads_mix_wait", st_gmix, grad_x)))
    for n in mixw:
        big_update(n)
    parts["w_in"] = _xfer_wait("grads_in_wait", st_gin, res["w_o"][1])[0]
    big_update("w_in")
    res["conv_w"] = [r[None] for r in _adamw("adamw_conv_w", gcw, conv_w[0], m_conv_w[0], v_conv_w[0])]
    res["ffn_conv_w"] = [r[None] for r in _adamw("adamw_ffn_conv_w", gfcw, ffn_conv_w[0], m_ffn_conv_w[0], v_ffn_conv_w[0])]
    rep_out = _adamw("adamw_small", gall[:, :rep_rows], _pack([args[n] for n in rep]),
                     _pack([args["m_" + n] for n in rep]), _pack([args["v_" + n] for n in rep]))
    rep_out = [_unpack(r, [full_shapes[n] for n in rep]) for r in rep_out]
    for i, n in enumerate(rep):
        res[n] = [r[i] for r in rep_out]
    nt_out = _adamw("adamw_norm_tok", nt_all, _pack([norm_tok]), _pack([m_norm_tok]), _pack([v_norm_tok]))
    res["norm_tok"] = [_unpack(r, [norm_tok.shape])[0] for r in nt_out]

    loss = lax.psum(loss_part[0, 0], ("x", "y", "c"))
    names = ["norm_tok", "w_in", "a_re", "a_im", "log_dt", "b_re", "b_im", "c_re", "c_im", "d_skip", "w_glu", "w_ssm_out",
             "conv_w", "conv_b", "w_conv_out", "w_o", "norm_ffn", "w_up", "ffn_conv_w", "ffn_conv_b", "w_down", "norm_final"]
    out = [loss, grad_x[None]]
    for slot in range(4):
        out += [res[n][slot] for n in names]
    return tuple(out)
```

```python
import functools

import jax
import jax.numpy as jnp
from jax import lax
from jax.experimental import pallas as pl
from jax.experimental.pallas import tpu as pltpu
from jax.experimental.pallas import tpu_sc as plsc

F32 = jnp.float32
BF16 = jnp.bfloat16
N_DEV = 8
LANES = 128
SLAB = 4
EPS = 1e-6
ADAM_LR = 0.001
ADAM_B1 = 0.9
ADAM_B2 = 0.999
ADAM_EPS = 1e-08
ADAM_WD = 0.01
ADAM_STEP = 10
VMEM_LIMIT = 56 * 1024 * 1024
MESH = pl.DeviceIdType.MESH


def _tile(n, pref, mult=LANES):
    best = None
    t = mult
    while t <= min(n, pref):
        if n % t == 0:
            best = t
        t += mult
    return best if best is not None else n


def _params(ndim):
    return pltpu.CompilerParams(dimension_semantics=("arbitrary",) * ndim, vmem_limit_bytes=VMEM_LIMIT)


def _sds(shape, dtype):
    return jax.ShapeDtypeStruct(tuple(shape), dtype)


def _mm(name, a, b, mode, *, out_dtype=F32, tm=1024, tn=1024, tk=2048, dims=None, a_spec=None, b_spec=None, dep=None):
    if dims is None:
        if mode == "nn":
            (M, K), N = a.shape, b.shape[1]
        elif mode == "nt":
            (M, K), N = a.shape, b.shape[0]
        else:
            (K, M), N = a.shape, b.shape[1]
    else:
        M, N, K = dims
    tm, tn, tk = _tile(M, tm), _tile(N, tn), _tile(K, tk)
    nk = K // tk
    if mode == "nn":
        dn = (((1,), (0,)), ((), ()))
        sa = pl.BlockSpec((tm, tk), lambda i, j, k: (i, k))
        sb = pl.BlockSpec((tk, tn), lambda i, j, k: (k, j))
    elif mode == "nt":
        dn = (((1,), (1,)), ((), ()))
        sa = pl.BlockSpec((tm, tk), lambda i, j, k: (i, k))
        sb = pl.BlockSpec((tn, tk), lambda i, j, k: (j, k))
    else:
        dn = (((0,), (0,)), ((), ()))
        sa = pl.BlockSpec((tk, tm), lambda i, j, k: (k, i))
        sb = pl.BlockSpec((tk, tn), lambda i, j, k: (k, j))
    sa = a_spec(tm, tn, tk) if a_spec is not None else sa
    sb = b_spec(tm, tn, tk) if b_spec is not None else sb
    use_acc = nk > 1 and out_dtype != F32

    deps = [] if dep is None else [dep]

    def body(a_ref, b_ref, *rest):
        o_ref, acc = rest[len(deps)], rest[len(deps) + 1:]
        k = pl.program_id(2)
        p = lax.dot_general(a_ref[...], b_ref[...], dn, preferred_element_type=F32)
        if nk == 1:
            o_ref[...] = p.astype(out_dtype)
        else:
            tgt = acc[0] if use_acc else o_ref

            @pl.when(k == 0)
            def _():
                tgt[...] = p

            @pl.when(k > 0)
            def _():
                tgt[...] += p

            if use_acc:
                @pl.when(k == nk - 1)
                def _():
                    o_ref[...] = acc[0][...].astype(out_dtype)

    return pl.pallas_call(
        body, name=name, grid=(M // tm, N // tn, nk),
        in_specs=[sa, sb] + [pl.BlockSpec(memory_space=pl.ANY)] * len(deps),
        out_specs=pl.BlockSpec((tm, tn), lambda i, j, k: (i, j)),
        out_shape=_sds((M, N), out_dtype),
        scratch_shapes=[pltpu.VMEM((tm, tn), F32)] if use_acc else [],
        compiler_params=_params(3),
    )(a, b, *deps)


def _rows(name, body, L, tm, ins, outs):
    return pl.pallas_call(
        body, name=name, grid=(L // tm,),
        in_specs=[s for _, s in ins], out_specs=[s for _, s in outs],
        out_shape=[o for o, _ in outs], compiler_params=_params(1),
    )(*[a for a, _ in ins])


def _rs(tm, w, cb=0):
    return pl.BlockSpec((tm, w), lambda i: (i, cb))


def _fs(shape):
    return pl.BlockSpec(tuple(shape), lambda i: (0,) * len(shape))


def _acc_rows(i, ref, part):
    @pl.when(i == 0)
    def _():
        ref[...] = part

    @pl.when(i > 0)
    def _():
        ref[...] += part


def _cast_bf16(name, w):
    R, C = w.shape
    tr = _tile(R, max(16, (1 << 20) // C), 16)

    def body(w_ref, o_ref):
        o_ref[...] = w_ref[...].astype(BF16)

    return _rows(name, body, R, tr, [(w, _rs(tr, C))], [(_sds((R, C), BF16), _rs(tr, C))])[0]


def _rms_fwd(name, x, g, tm):
    L, D = x.shape

    def body(x_ref, g_ref, o_ref):
        xv = x_ref[...]
        r = lax.rsqrt(jnp.mean(xv * xv, axis=-1, keepdims=True) + EPS)
        o_ref[...] = (xv * r * g_ref[...]).astype(BF16)

    return _rows(name, body, L, tm, [(x, _rs(tm, D)), (g, _fs((1, D)))], [(_sds((L, D), BF16), _rs(tm, D))])[0]


def _res_rms_fwd(name, x, o, g, tm):
    L, D = x.shape

    def body(x_ref, o_ref, g_ref, h_ref, hn_ref):
        h = x_ref[...] + o_ref[...]
        r = lax.rsqrt(jnp.mean(h * h, axis=-1, keepdims=True) + EPS)
        h_ref[...] = h
        hn_ref[...] = (h * r * g_ref[...]).astype(BF16)

    return _rows(name, body, L, tm, [(x, _rs(tm, D)), (o, _rs(tm, D)), (g, _fs((1, D)))],
                 [(_sds((L, D), F32), _rs(tm, D)), (_sds((L, D), BF16), _rs(tm, D))])


def _rms_bwd(name, dn, h, g, dres, tm, with_bf16):
    L, D = h.shape

    def body(dn_ref, h_ref, g_ref, dres_ref, dh_ref, *rest):
        i = pl.program_id(0)
        h = h_ref[...]
        r = lax.rsqrt(jnp.mean(h * h, axis=-1, keepdims=True) + EPS)
        xh = h * r
        d = dn_ref[...]
        dxh = d * g_ref[...]
        dh = dres_ref[...] + r * (dxh - xh * jnp.mean(dxh * xh, axis=-1, keepdims=True))
        dh_ref[...] = dh
        if with_bf16:
            rest[0][...] = dh.astype(BF16)
        _acc_rows(i, rest[-1], jnp.sum(d * xh, axis=0, keepdims=True))

    outs = [(_sds((L, D), F32), _rs(tm, D))]
    if with_bf16:
        outs.append((_sds((L, D), BF16), _rs(tm, D)))
    outs.append((_sds((1, D), F32), _fs((1, D))))
    return _rows(name, body, L, tm, [(dn, _rs(tm, D)), (h, _rs(tm, D)), (g, _fs((1, D))), (dres, _rs(tm, D))], outs)


def _final(name, h1, o2, g, tgt, tm):
    L, D = h1.shape

    def body(h1_ref, o2_ref, g_ref, t_ref, dh_ref, dhb_ref, dg_ref, loss_ref):
        i = pl.program_id(0)
        h = h1_ref[...] + o2_ref[...]
        r = lax.rsqrt(jnp.mean(h * h, axis=-1, keepdims=True) + EPS)
        xh = h * r
        gv = g_ref[...]
        e = xh * gv - t_ref[...]
        part = 0.5 * jnp.sum(jnp.mean(e * e, axis=-1, keepdims=True), axis=0, keepdims=True)
        dy = e / D
        dxh = dy * gv
        dh = r * (dxh - xh * jnp.mean(dxh * xh, axis=-1, keepdims=True))
        dh_ref[...] = dh
        dhb_ref[...] = dh.astype(BF16)
        _acc_rows(i, dg_ref, jnp.sum(dy * xh, axis=0, keepdims=True))
        _acc_rows(i, loss_ref, jnp.broadcast_to(part, (8, LANES)))

    return _rows(name, body, L, tm,
                 [(h1, _rs(tm, D)), (o2, _rs(tm, D)), (g, _fs((1, D))), (tgt, _rs(tm, D))],
                 [(_sds((L, D), F32), _rs(tm, D)), (_sds((L, D), BF16), _rs(tm, D)),
                  (_sds((1, D), F32), _fs((1, D))), (_sds((8, LANES), F32), _fs((8, LANES)))])


def _glu_fn(y, g1):
    ya = jax.nn.gelu(y)
    return ya * jax.nn.sigmoid(g1)


def _glu_fwd(name, y, g1, tm):
    L, W = y.shape

    def body(y_ref, g_ref, o_ref):
        o_ref[...] = _glu_fn(y_ref[...], g_ref[...]).astype(BF16)

    return _rows(name, body, L, tm, [(y, _rs(tm, W)), (g1, _rs(tm, W))], [(_sds((L, W), BF16), _rs(tm, W))])[0]


def _glu_bwd(name, y, g1, dya2, tm):
    L, W = y.shape

    def body(y_ref, g_ref, d_ref, dy_ref, dg_ref):
        _, vjp = jax.vjp(_glu_fn, y_ref[...], g_ref[...])
        dy, dg = vjp(d_ref[...])
        dy_ref[...] = dy
        dg_ref[...] = dg.astype(BF16)

    return _rows(name, body, L, tm, [(y, _rs(tm, W)), (g1, _rs(tm, W)), (dya2, _rs(tm, W))],
                 [(_sds((L, W), F32), _rs(tm, W)), (_sds((L, W), BF16), _rs(tm, W))])


def _gelu_bwd(name, y, dy_direct, dya_g, proj, dskip, tm):
    L, W = y.shape

    def body(y_ref, dd_ref, dg_ref, u_ref, dyb_ref, dsk_ref):
        i = pl.program_id(0)
        _, vjp = jax.vjp(jax.nn.gelu, y_ref[...])
        dy = dd_ref[...] + vjp(dg_ref[...])[0]
        dyb_ref[...] = dy.astype(BF16)
        _acc_rows(i, dsk_ref, jnp.sum(dy * u_ref[...], axis=0, keepdims=True))

    del dskip
    return _rows(name, body, L, tm,
                 [(y, _rs(tm, W)), (dy_direct, _rs(tm, W)), (dya_g, _rs(tm, W)), (proj, _rs(tm, W, 0))],
                 [(_sds((L, W), BF16), _rs(tm, W)), (_sds((1, W), F32), _fs((1, W)))])


def _merge_fn(ma, mb, za, zb):
    return jax.nn.sigmoid(ma) * za + jax.nn.sigmoid(mb) * zb


def _merge_fwd(name, proj, cb_a, cb_b, za, zb, tm):
    L, D = za.shape

    def body(ma_ref, mb_ref, za_ref, zb_ref, o_ref):
        o_ref[...] = _merge_fn(ma_ref[...], mb_ref[...], za_ref[...], zb_ref[...]).astype(BF16)

    return _rows(name, body, L, tm,
                 [(proj, _rs(tm, D, cb_a)), (proj, _rs(tm, D, cb_b)), (za, _rs(tm, D)), (zb, _rs(tm, D))],
                 [(_sds((L, D), BF16), _rs(tm, D))])[0]


def _merge_bwd(name, proj, cb_a, cb_b, za, zb, dmerged, tm):
    L, D = za.shape

    def body(ma_ref, mb_ref, za_ref, zb_ref, d_ref, dza_ref, dzb_ref, dma_ref, dmb_ref):
        _, vjp = jax.vjp(_merge_fn, ma_ref[...], mb_ref[...], za_ref[...], zb_ref[...])
        dma, dmb, dza, dzb = vjp(d_ref[...])
        dza_ref[...] = dza.astype(BF16)
        dzb_ref[...] = dzb.astype(BF16)
        dma_ref[...] = dma.astype(BF16)
        dmb_ref[...] = dmb.astype(BF16)

    return _rows(name, body, L, tm,
                 [(proj, _rs(tm, D, cb_a)), (proj, _rs(tm, D, cb_b)), (za, _rs(tm, D)), (zb, _rs(tm, D)),
                  (dmerged, _rs(tm, D))],
                 [(_sds((L, D), BF16), _rs(tm, D)), (_sds((L, D), BF16), _rs(tm, D)),
                  (_sds((L, D), BF16), _rs(tm, D)), (_sds((L, D), BF16), _rs(tm, D))])


def _shift_down(x, k):
    row = lax.broadcasted_iota(jnp.int32, x.shape, 0)
    return jnp.where(row >= k, pltpu.roll(x, k, axis=0), 0.0)


def _shift_up(x, k):
    n = x.shape[0]
    row = lax.broadcasted_iota(jnp.int32, x.shape, 0)
    return jnp.where(row < n - k, pltpu.roll(x, n - k, axis=0), 0.0)


def _conv3(cv, w_ref, b_ref):
    return (w_ref[2:3, :] * cv + w_ref[1:2, :] * _shift_down(cv, 1) + w_ref[0:1, :] * _shift_down(cv, 2)
            + b_ref[...])


def _conv3_bwd(dcc, cv, w_ref):
    dcv = w_ref[2:3, :] * dcc + w_ref[1:2, :] * _shift_up(dcc, 1) + w_ref[0:1, :] * _shift_up(dcc, 2)
    dw = [jnp.sum(dcc * _shift_down(cv, 2), axis=0, keepdims=True),
          jnp.sum(dcc * _shift_down(cv, 1), axis=0, keepdims=True),
          jnp.sum(dcc * cv, axis=0, keepdims=True)]
    db = jnp.sum(dcc, axis=0, keepdims=True)
    return dcv, dw, db


def _store_rows(ref, rows):
    for r, val in enumerate(rows):
        ref[r:r + 1, :] = val


def _cols(name, body, ncb, ins, outs):
    return pl.pallas_call(
        body, name=name, grid=(ncb,),
        in_specs=[s for _, s in ins], out_specs=[s for _, s in outs],
        out_shape=[o for o, _ in outs], compiler_params=_params(1),
    )(*[a for a, _ in ins])


def _cb(L, w, off=0):
    return pl.BlockSpec((L, w), lambda j: (0, j + off))


def _convb_fwd(name, proj, cb_v, cb_gb, cb_gc, w, b):
    L = proj.shape[0]
    W = w.shape[1]
    c = LANES

    def body(v_ref, gb_ref, gc_ref, w_ref, b_ref, q_ref):
        cc = _conv3(gc_ref[...] * v_ref[...], w_ref, b_ref)
        q_ref[...] = (gb_ref[...] * cc).astype(BF16)

    return _cols(name, body, W // c,
                 [(proj, _cb(L, c, cb_v)), (proj, _cb(L, c, cb_gb)), (proj, _cb(L, c, cb_gc)),
                  (w, _cb(3, c)), (b, _cb(1, c))],
                 [(_sds((L, W), BF16), _cb(L, c))])[0]


def _convb_bwd(name, proj, cb_v, cb_gb, cb_gc, w, b, dq):
    L = proj.shape[0]
    W = w.shape[1]
    c = LANES

    def body(v_ref, gb_ref, gc_ref, w_ref, b_ref, dq_ref, dv_ref, dgb_ref, dgc_ref, dw_ref, db_ref):
        v, gc = v_ref[...], gc_ref[...]
        cv = gc * v
        cc = _conv3(cv, w_ref, b_ref)
        dq = dq_ref[...]
        dgb_ref[...] = (dq * cc).astype(BF16)
        dcv, dw, db = _conv3_bwd(dq * gb_ref[...], cv, w_ref)
        dv_ref[...] = (dcv * gc).astype(BF16)
        dgc_ref[...] = (dcv * v).astype(BF16)
        _store_rows(dw_ref, dw)
        db_ref[...] = db

    return _cols(name, body, W // c,
                 [(proj, _cb(L, c, cb_v)), (proj, _cb(L, c, cb_gb)), (proj, _cb(L, c, cb_gc)),
                  (w, _cb(3, c)), (b, _cb(1, c)), (dq, _cb(L, c))],
                 [(_sds((L, W), BF16), _cb(L, c)), (_sds((L, W), BF16), _cb(L, c)), (_sds((L, W), BF16), _cb(L, c)),
                  (_sds((3, W), F32), _cb(3, c)), (_sds((1, W), F32), _cb(1, c))])


def _ffn_fwd(name, hh, w, b):
    L = hh.shape[0]
    Fw = w.shape[1]
    c = LANES
    nf = Fw // c

    def body(a_ref, h2_ref, w_ref, b_ref, f_ref):
        a = _conv3(a_ref[...], w_ref, b_ref)
        f_ref[...] = (jax.nn.gelu(a) * h2_ref[...]).astype(BF16)

    return _cols(name, body, nf, [(hh, _cb(L, c)), (hh, _cb(L, c, nf)), (w, _cb(3, c)), (b, _cb(1, c))],
                 [(_sds((L, Fw), BF16), _cb(L, c))])[0]


def _ffn_bwd(name, hh, w, b, df):
    L = hh.shape[0]
    Fw = w.shape[1]
    c = LANES
    nf = Fw // c

    def body(a_ref, h2_ref, w_ref, b_ref, df_ref, dhh_ref, dw_ref, db_ref):
        h1 = a_ref[...]
        a = _conv3(h1, w_ref, b_ref)
        ga, vjp = jax.vjp(jax.nn.gelu, a)
        d = df_ref[...]
        dhh_ref[1] = (d * ga).astype(BF16)
        da = vjp(d * h2_ref[...])[0]
        dh1, dw, db = _conv3_bwd(da, h1, w_ref)
        dhh_ref[0] = dh1.astype(BF16)
        _store_rows(dw_ref, dw)
        db_ref[...] = db

    return _cols(name, body, nf,
                 [(hh, _cb(L, c)), (hh, _cb(L, c, nf)), (w, _cb(3, c)), (b, _cb(1, c)), (df, _cb(L, c))],
                 [(_sds((2, L, Fw), BF16), pl.BlockSpec((2, L, c), lambda j: (0, 0, j))),
                  (_sds((3, Fw), F32), _cb(3, c)), (_sds((1, Fw), F32), _cb(1, c))])


def _prep_fn(ar, ai, ldt, brt, bit):
    dt = jnp.exp(ldt)
    mag = jnp.exp(dt * ar)
    are = mag * jnp.cos(dt * ai)
    aim = mag * jnp.sin(dt * ai)
    nr = are - 1.0
    ni = aim
    den = ar * ar + ai * ai
    fr = (nr * ar + ni * ai) / den
    fi = (ni * ar - nr * ai) / den
    return are, aim, fr * brt - fi * bit, fr * bit + fi * brt


def _prep_fwd(name, ar, ai, ldt, brt, bit):
    def body(ar_ref, ai_ref, l_ref, br_ref, bi_ref, o1, o2, o3, o4):
        o1[...], o2[...], o3[...], o4[...] = _prep_fn(ar_ref[...], ai_ref[...], l_ref[...], br_ref[...], bi_ref[...])

    return pl.pallas_call(body, name=name,
                          out_shape=[_sds(ar.shape, F32), _sds(ar.shape, F32), _sds(brt.shape, F32), _sds(brt.shape, F32)],
                          )(ar, ai, ldt, brt, bit)


def _prep_bwd(name, ar, ai, ldt, brt, bit, g1, g2, g3, g4):
    def body(ar_ref, ai_ref, l_ref, br_ref, bi_ref, g1_ref, g2_ref, g3_ref, g4_ref, o1, o2, o3, o4, o5):
        _, vjp = jax.vjp(_prep_fn, ar_ref[...], ai_ref[...], l_ref[...], br_ref[...], bi_ref[...])
        o1[...], o2[...], o3[...], o4[...], o5[...] = vjp((g1_ref[...], g2_ref[...], g3_ref[...], g4_ref[...]))

    return pl.pallas_call(body, name=name,
                          out_shape=[_sds(ar.shape, F32)] * 3 + [_sds(brt.shape, F32)] * 2,
                          )(ar, ai, ldt, brt, bit, g1, g2, g3, g4)


def _ssm_in(name, src, m1, m2, tm, dep=None):
    L = src.shape[0]
    nb = m1.shape[0]

    deps = [] if dep is None else [dep]

    def body(s_ref, m1_ref, m2_ref, *rest):
        o1_ref, o2_ref = rest[len(deps):]
        u = s_ref[...].astype(BF16)
        r1 = jnp.dot(u, m1_ref[...], preferred_element_type=F32)
        r2 = jnp.dot(u, m2_ref[...], preferred_element_type=F32)
        for q in range(SLAB):
            o1_ref[q] = r1[:, q * LANES:(q + 1) * LANES]
            o2_ref[q] = r2[:, q * LANES:(q + 1) * LANES]

    ms = pl.BlockSpec((None, LANES, SLAB * LANES), lambda i, j: (j, 0, 0))
    os_ = pl.BlockSpec((SLAB, tm, LANES), lambda i, j: (j, i, 0))
    return pl.pallas_call(
        body, name=name, grid=(L // tm, nb),
        in_specs=[pl.BlockSpec((tm, LANES), lambda i, j: (i, j)), ms, ms] + [pl.BlockSpec(memory_space=pl.ANY)] * len(deps),
        out_specs=[os_, os_],
        out_shape=[_sds((SLAB * nb, L, LANES), F32)] * 2, compiler_params=_params(2),
    )(src, m1, m2, *deps)


def _ssm_out(name, x1, x2, m1, m2, aux, dvec, tm, post=None):
    L = x1.shape[1]
    nb = m1.shape[0]

    def body(x1_ref, x2_ref, m1_ref, m2_ref, a_ref, d_ref, o_ref, *rest):
        a1 = jnp.concatenate([x1_ref[q] for q in range(SLAB)], axis=1).astype(BF16)
        a2 = jnp.concatenate([x2_ref[q] for q in range(SLAB)], axis=1).astype(BF16)
        y = (jnp.dot(a1, m1_ref[...], preferred_element_type=F32) + jnp.dot(a2, m2_ref[...], preferred_element_type=F32)
             + d_ref[...] * a_ref[...].astype(F32))
        o_ref[...] = y
        if post is not None:
            rest[0][...] = post(y).astype(BF16)

    xs = pl.BlockSpec((SLAB, tm, LANES), lambda i, j: (j, i, 0))
    ms = pl.BlockSpec((None, SLAB * LANES, LANES), lambda i, j: (j, 0, 0))
    cs = pl.BlockSpec((tm, LANES), lambda i, j: (i, j))
    W = nb * LANES
    outs, ospecs = [_sds((L, W), F32)], [cs]
    if post is not None:
        outs.append(_sds((L, W), BF16))
        ospecs.append(cs)
    return pl.pallas_call(
        body, name=name, grid=(L // tm, nb),
        in_specs=[xs, xs, ms, ms, cs, pl.BlockSpec((1, LANES), lambda i, j: (0, j))], out_specs=ospecs,
        out_shape=outs, compiler_params=_params(2),
    )(x1, x2, m1, m2, aux, dvec)


def _ssm_dw(name, src, x1, x2, tk):
    L = src.shape[0]
    nb = x1.shape[0] // SLAB
    dn = (((0,), (0,)), ((), ()))

    def body(s_ref, x1_ref, x2_ref, o1_ref, o2_ref):
        k = pl.program_id(1)
        s = s_ref[...].astype(BF16)
        a1 = jnp.concatenate([x1_ref[q] for q in range(SLAB)], axis=1).astype(BF16)
        a2 = jnp.concatenate([x2_ref[q] for q in range(SLAB)], axis=1).astype(BF16)
        _acc_rows(k, o1_ref, lax.dot_general(s, a1, dn, preferred_element_type=F32))
        _acc_rows(k, o2_ref, lax.dot_general(s, a2, dn, preferred_element_type=F32))

    xs = pl.BlockSpec((SLAB, tk, LANES), lambda j, k: (j, k, 0))
    os_ = pl.BlockSpec((None, LANES, SLAB * LANES), lambda j, k: (j, 0, 0))
    return pl.pallas_call(
        body, name=name, grid=(nb, L // tk),
        in_specs=[pl.BlockSpec((tk, LANES), lambda j, k: (k, j)), xs, xs], out_specs=[os_, os_],
        out_shape=[_sds((nb, LANES, SLAB * LANES), F32)] * 2, compiler_params=_params(2),
    )(src, x1, x2)


def _scan(name, b_re, b_im, a_re, a_im, xs=None):
    reverse = xs is not None
    ns, L, _ = b_re.shape
    ng = ns // 8
    tc = min(LANES, L)
    pitch = tc + 8
    nt = L // tc
    n_in = 4 if reverse else 2

    def body(*refs):
        ins = refs[:n_in]
        ar_ref, ai_ref = refs[n_in], refs[n_in + 1]
        o_re, o_im = refs[n_in + 2], refs[n_in + 3]
        k = n_in + 4
        if reverse:
            da_re, da_im = refs[k], refs[k + 1]
            k += 2
        stage = refs[k:k + n_in]
        out_re, out_im, st_re, st_im = refs[k + n_in:k + n_in + 4]
        acc = refs[k + n_in + 4:]
        i = pl.program_id(0)

        @pl.when(i == 0)
        def _():
            st_re[...] = jnp.zeros(st_re.shape, F32)
            st_im[...] = jnp.zeros(st_im.shape, F32)
            for r in acc:
                r[...] = jnp.zeros(r.shape, F32)

        for s in range(ns):
            for src, dst in zip(ins, stage):
                dst[pl.ds(s * pitch, tc), :] = src[s]

        a_r = [ar_ref[g] for g in range(ng)]
        a_i = [ai_ref[g] for g in range(ng)]

        def step(tt, carry):
            t = (tc - 1 - tt) if reverse else tt
            new = []
            for g in range(ng):
                rows = pl.ds(g * 8 * pitch + t, 8, stride=pitch)
                cr, ci = carry[2 * g], carry[2 * g + 1]
                br, bi = stage[0][rows, :], stage[1][rows, :]
                if reverse:
                    xr, xi = stage[2][rows, :], stage[3][rows, :]
                    acc[0][g] += xr * cr + xi * ci
                    acc[1][g] += xr * ci - xi * cr
                    nr = a_r[g] * cr + a_i[g] * ci + br
                    ni = a_r[g] * ci - a_i[g] * cr + bi
                else:
                    nr = a_r[g] * cr - a_i[g] * ci + br
                    ni = a_r[g] * ci + a_i[g] * cr + bi
                out_re[rows, :] = nr
                out_im[rows, :] = ni
                new += [nr, ni]
            return tuple(new)

        init = []
        for g in range(ng):
            init += [st_re[g], st_im[g]]
        fin = lax.fori_loop(0, tc, step, tuple(init), unroll=2)
        for g in range(ng):
            st_re[g] = fin[2 * g]
            st_im[g] = fin[2 * g + 1]
        for s in range(ns):
            o_re[s] = out_re[pl.ds(s * pitch, tc), :]
            o_im[s] = out_im[pl.ds(s * pitch, tc), :]
        if reverse:
            da_re[...] = acc[0][...]
            da_im[...] = acc[1][...]

    tmap = (lambda i: (0, nt - 1 - i, 0)) if reverse else (lambda i: (0, i, 0))
    bs = pl.BlockSpec((ns, tc, LANES), tmap)
    as_ = pl.BlockSpec((ng, 8, LANES), lambda i: (0, 0, 0))
    ins = [b_re, b_im] + (list(xs) if reverse else [])
    out_shape = [_sds((ns, L, LANES), F32)] * 2 + ([_sds((ng, 8, LANES), F32)] * 2 if reverse else [])
    out_specs = [bs, bs] + ([as_, as_] if reverse else [])
    scratch = [pltpu.VMEM((ns * pitch, LANES), F32)] * (n_in + 2) + [pltpu.VMEM((ng, 8, LANES), F32)] * (4 if reverse else 2)
    return pl.pallas_call(
        body, name=name, grid=(nt,), in_specs=[bs] * n_in + [as_, as_], out_specs=out_specs,
        out_shape=out_shape, scratch_shapes=scratch, compiler_params=_params(1),
    )(*ins, a_re, a_im)


def _peer(k):
    x, y, c = lax.axis_index("x"), lax.axis_index("y"), lax.axis_index("c")
    px = 1 - x if (k >> 2) & 1 else x
    py = 1 - y if (k >> 1) & 1 else y
    pc = 1 - c if k & 1 else c
    return (px, py, pc), 4 * px + 2 * py + pc


def _window(ref, kind, idx, n):
    if kind == "col":
        w = ref.shape[1] // n
        return ref.at[:, pl.ds(pl.multiple_of(idx * w, LANES), w)]
    r = ref.shape[0] // n
    return ref.at[pl.ds(pl.multiple_of(idx * r, 8), r), :]


def _all_gather(name, shards, kinds):
    n = len(shards)
    fulls = []
    for s, kind in zip(shards, kinds):
        fulls.append(_sds((s.shape[0], s.shape[1] * N_DEV) if kind == "col" else (s.shape[0] * N_DEV, s.shape[1]), s.dtype))

    def body(*refs):
        src, dst = refs[:n], refs[n:2 * n]
        send, recv, loc = refs[2 * n:]
        me = 4 * lax.axis_index("x") + 2 * lax.axis_index("y") + lax.axis_index("c")
        copies = []
        for a in range(n):
            own = pltpu.make_async_copy(src[a], _window(dst[a], kinds[a], me, N_DEV), loc.at[a])
            own.start()
            copies.append(own)
        sends = []
        for k in range(1, N_DEV):
            dev, _ = _peer(k)
            for a in range(n):
                cp = pltpu.make_async_remote_copy(
                    src_ref=src[a], dst_ref=_window(dst[a], kinds[a], me, N_DEV),
                    send_sem=send.at[a * N_DEV + k], recv_sem=recv.at[a * N_DEV + k],
                    device_id=dev, device_id_type=MESH)
                cp.start()
                sends.append(cp)
        for k in range(1, N_DEV):
            dev, pidx = _peer(k)
            for a in range(n):
                pltpu.make_async_remote_copy(
                    src_ref=src[a], dst_ref=_window(dst[a], kinds[a], pidx, N_DEV),
                    send_sem=send.at[a * N_DEV + k], recv_sem=recv.at[a * N_DEV + k],
                    device_id=dev, device_id_type=MESH).wait_recv()
        for cp in sends:
            cp.wait_send()
        for cp in copies:
            cp.wait()

    any_ = pl.BlockSpec(memory_space=pl.ANY)
    return pl.pallas_call(
        body, name=name, in_specs=[any_] * n, out_specs=[any_] * n, out_shape=fulls,
        scratch_shapes=[pltpu.SemaphoreType.DMA((n * N_DEV,)), pltpu.SemaphoreType.DMA((n * N_DEV,)),
                        pltpu.SemaphoreType.DMA((n,))],
        compiler_params=pltpu.CompilerParams(has_side_effects=True),
    )(*shards)


_HBM = pl.BlockSpec(memory_space=pltpu.HBM)
_SEM = pl.BlockSpec(memory_space=pltpu.SEMAPHORE)
_ANY = pl.BlockSpec(memory_space=pl.ANY)
_EFFECT = pltpu.SideEffectType.DATAFLOW_SIDE_EFFECTING


def _xfer_refs(mode, kinds, a, src, dst, me, pidx):
    if mode == "gather":
        return src[a], _window(dst[a], kinds[a], me, N_DEV), _window(dst[a], kinds[a], pidx, N_DEV)
    return _window(src[a], kinds[a], pidx, N_DEV), dst[a].at[me], dst[a].at[pidx]


def _xfer_out_shapes(mode, arrs, kinds):
    outs = []
    for s, kind in zip(arrs, kinds):
        if mode == "gather":
            outs.append((s.shape[0], s.shape[1] * N_DEV) if kind == "col" else (s.shape[0] * N_DEV, s.shape[1]))
        else:
            outs.append((N_DEV,) + ((s.shape[0], s.shape[1] // N_DEV) if kind == "col" else (s.shape[0] // N_DEV, s.shape[1])))
    return outs


def _xfer_start(name, mode, arrs, kinds, after):
    n = len(arrs)
    shapes = _xfer_out_shapes(mode, arrs, kinds)

    def body(*refs):
        src, dst = refs[:n], refs[n:2 * n]
        send, recv = refs[2 * n + 1], refs[2 * n + 2]
        token, loc = refs[2 * n + 3 + 2 * n], refs[2 * n + 4 + 2 * n]
        me = 4 * lax.axis_index("x") + 2 * lax.axis_index("y") + lax.axis_index("c")
        own = []
        for a in range(n):
            s, _, d = _xfer_refs(mode, kinds, a, src, dst, me, me)
            own.append(pltpu.make_async_copy(s, d, loc.at[a]))
            own[-1].start()
        for cp in own:
            cp.wait()
        for k in range(1, N_DEV):
            dev, pidx = _peer(k)
            for a in range(n):
                s, d, _ = _xfer_refs(mode, kinds, a, src, dst, me, pidx)
                pltpu.make_async_remote_copy(src_ref=s, dst_ref=d, send_sem=send.at[a * N_DEV + k],
                                             recv_sem=recv.at[a * N_DEV + k], device_id=dev, device_id_type=MESH).start()
        token[...] = jnp.zeros(token.shape, F32)

    lands = [pltpu.with_memory_space_constraint(lax.empty(shp, s.dtype), pltpu.HBM) for shp, s in zip(shapes, arrs)]
    srcs = [pltpu.with_memory_space_constraint(s, pltpu.HBM) for s in arrs]
    res = pl.pallas_call(
        body, name=name,
        in_specs=[_HBM] * (2 * n) + [_ANY],
        out_specs=[_SEM, _SEM] + [_HBM] * (2 * n) + [pl.BlockSpec(memory_space=pltpu.VMEM)],
        out_shape=[pltpu.SemaphoreType.DMA((n * N_DEV,)), pltpu.SemaphoreType.DMA((n * N_DEV,))]
        + [pltpu.HBM(s.shape, s.dtype) for s in arrs] + [pltpu.HBM(shp, s.dtype) for shp, s in zip(shapes, arrs)]
        + [_sds((8, LANES), F32)],
        input_output_aliases={i: 2 + i for i in range(2 * n)},
        scratch_shapes=[pltpu.SemaphoreType.DMA((n,))],
        compiler_params=pltpu.CompilerParams(has_side_effects=_EFFECT),
    )(*srcs, *lands, after)
    return dict(mode=mode, kinds=kinds, n=n, send=res[0], recv=res[1], srcs=res[2:2 + n], lands=res[2 + n:2 + 2 * n]), res[-1]


def _xfer_wait(name, st, after):
    n, mode, kinds = st["n"], st["mode"], st["kinds"]

    def body(*refs):
        src, dst = refs[:n], refs[n:2 * n]
        send, recv = refs[2 * n], refs[2 * n + 1]
        me = 4 * lax.axis_index("x") + 2 * lax.axis_index("y") + lax.axis_index("c")
        for k in range(1, N_DEV):
            dev, pidx = _peer(k)
            for a in range(n):
                s, d, land = _xfer_refs(mode, kinds, a, src, dst, me, pidx)
                cp = pltpu.make_async_remote_copy(src_ref=s, dst_ref=land, send_sem=send.at[a * N_DEV + k],
                                                  recv_sem=recv.at[a * N_DEV + k], device_id=dev, device_id_type=MESH)
                cp.wait_send()
                cp.wait_recv()

    res = pl.pallas_call(
        body, name=name,
        in_specs=[_HBM] * (2 * n) + [_SEM, _SEM, _ANY],
        out_specs=[_HBM] * (2 * n),
        out_shape=[pltpu.HBM(s.shape, s.dtype) for s in st["srcs"]] + [pltpu.HBM(s.shape, s.dtype) for s in st["lands"]],
        input_output_aliases={i: i for i in range(2 * n)},
        compiler_params=pltpu.CompilerParams(has_side_effects=_EFFECT),
    )(*st["srcs"], *st["lands"], st["send"], st["recv"], after)
    return list(res[n:])


def _sc_xfer(name, mode, arrs, kinds, collective_id):
    n = len(arrs)
    shapes = _xfer_out_shapes(mode, arrs, kinds)
    hbm = pltpu.MemorySpace.HBM
    src = [jax.new_ref(a, memory_space=hbm) for a in arrs]
    dst = [jax.empty_ref(_sds(shp, a.dtype), memory_space=hbm) for shp, a in zip(shapes, arrs)]

    @pl.kernel(mesh=plsc.ScalarSubcoreMesh(axis_name="seq", num_cores=1), name=name,
               scratch_types=(pltpu.SemaphoreType.DMA((n * N_DEV,)), pltpu.SemaphoreType.DMA((n * N_DEV,)),
                              pltpu.SemaphoreType.DMA((n,))),
               compiler_params=pltpu.CompilerParams(collective_id=collective_id))
    def launch(send, recv, loc):
        barrier = pltpu.get_barrier_semaphore()
        for k in range(1, N_DEV):
            pl.semaphore_signal(barrier, inc=1, device_id=_peer(k)[0], device_id_type=MESH)
        pl.semaphore_wait(barrier, N_DEV - 1)
        me = 4 * lax.axis_index("x") + 2 * lax.axis_index("y") + lax.axis_index("c")
        own, sends = [], []
        for a in range(n):
            s, _, d = _xfer_refs(mode, kinds, a, src, dst, me, me)
            own.append(pltpu.make_async_copy(s, d, loc.at[a]))
            own[-1].start()
        for k in range(1, N_DEV):
            dev, pidx = _peer(k)
            for a in range(n):
                s, d, _ = _xfer_refs(mode, kinds, a, src, dst, me, pidx)
                sends.append(pltpu.make_async_remote_copy(src_ref=s, dst_ref=d, send_sem=send.at[a * N_DEV + k],
                                                          recv_sem=recv.at[a * N_DEV + k], device_id=dev, device_id_type=MESH))
                sends[-1].start()
        for cp in own:
            cp.wait()
        for k in range(1, N_DEV):
            dev, pidx = _peer(k)
            for a in range(n):
                s, _, land = _xfer_refs(mode, kinds, a, src, dst, me, pidx)
                pltpu.make_async_remote_copy(src_ref=s, dst_ref=land, send_sem=send.at[a * N_DEV + k],
                                             recv_sem=recv.at[a * N_DEV + k], device_id=dev, device_id_type=MESH).wait_recv()
        for cp in sends:
            cp.wait_send()

    launch()
    return [d[...] for d in dst]


_SEQ_IDS = {"gather_mix": 1, "gather_ffn": 2, "grads_ffn": 3, "grads_mix": 4, "grads_small": 5, "grads_in": 6}


def _seq_start(name, mode, arrs, kinds, after):
    arrs = list(arrs)
    if after is not None:
        after, *arrs = lax.optimization_barrier((after, *arrs))
    return _sc_xfer(name, mode, arrs, kinds, _SEQ_IDS[name]), None


def _seq_wait(name, res, after):
    del name, after
    return list(res)


def _adamw(name, parts, w, m, v):
    P, R, C = parts.shape
    sub = 16 if parts.dtype == BF16 else 8
    tr = R if R * C <= (1 << 18) else _tile(R, max(sub, (1 << 18) // C), sub)

    def body(p_ref, w_ref, m_ref, v_ref, g_ref, d_ref, nm_ref, nv_ref):
        g = p_ref[0].astype(F32)
        for s in range(1, P):
            g = g + p_ref[s].astype(F32)
        m2 = ADAM_B1 * m_ref[...] + (1.0 - ADAM_B1) * g
        v2 = ADAM_B2 * v_ref[...] + (1.0 - ADAM_B2) * (g * g)
        m_hat = m2 / (1.0 - ADAM_B1 ** ADAM_STEP)
        v_hat = v2 / (1.0 - ADAM_B2 ** ADAM_STEP)
        g_ref[...] = g
        d_ref[...] = -ADAM_LR * (m_hat / (jnp.sqrt(v_hat) + ADAM_EPS) + ADAM_WD * w_ref[...])
        nm_ref[...] = m2
        nv_ref[...] = v2

    sp = pl.BlockSpec((tr, C), lambda i: (i, 0))
    return pl.pallas_call(
        body, name=name, grid=(R // tr,),
        in_specs=[pl.BlockSpec((P, tr, C), lambda i: (0, i, 0)), sp, sp, sp], out_specs=[sp] * 4,
        out_shape=[_sds((R, C), F32)] * 4, compiler_params=_params(1),
    )(parts, w, m, v)


def _pack(arrs, row_mult=8):
    pieces, total = [], 0
    for a in arrs:
        f = a.reshape(-1).astype(F32)
        pad = (-f.shape[0]) % (8 * LANES)
        pieces.append(jnp.pad(f, (0, pad)) if pad else f)
        total += f.shape[0] + pad
    tail = (-total) % (row_mult * LANES)
    if tail:
        pieces.append(jnp.zeros((tail,), F32))
    return jnp.concatenate(pieces).reshape(-1, LANES)


def _unpack(buf, shapes, lead=()):
    out, row = [], 0
    for shp in shapes:
        size = 1
        for d in shp:
            size *= d
        rows = -(-size // (8 * LANES)) * 8
        piece = buf[..., row:row + rows, :].reshape(lead + (rows * LANES,))[..., :size]
        out.append(piece.reshape(lead + tuple(shp)))
        row += rows
    return out


def kernel(x, norm_tok, w_in, a_re, a_im, log_dt, b_re, b_im, c_re, c_im, d_skip, w_glu, w_ssm_out, conv_w, conv_b, w_conv_out, w_o, norm_ffn, w_up, ffn_conv_w, ffn_conv_b, w_down, norm_final, loss_target, m_norm_tok, m_w_in, m_a_re, m_a_im, m_log_dt, m_b_re, m_b_im, m_c_re, m_c_im, m_d_skip, m_w_glu, m_w_ssm_out, m_conv_w, m_conv_b, m_w_conv_out, m_w_o, m_norm_ffn, m_w_up, m_ffn_conv_w, m_ffn_conv_b, m_w_down, m_norm_final, v_norm_tok, v_w_in, v_a_re, v_a_im, v_log_dt, v_b_re, v_b_im, v_c_re, v_c_im, v_d_skip, v_w_glu, v_w_ssm_out, v_conv_w, v_conv_b, v_w_conv_out, v_w_o, v_norm_ffn, v_w_up, v_ffn_conv_w, v_ffn_conv_b, v_w_down, v_norm_final):
    args = dict(locals())
    L, D = x.shape[1], x.shape[2]
    G, P, H = b_re.shape[1], b_re.shape[2], b_re.shape[3]
    SW = G * H
    CW = conv_b.shape[1]
    FF = ffn_conv_b.shape[1]
    GP = G * P
    nb = SW // LANES
    gpb = LANES // H
    me = 4 * lax.axis_index("x") + 2 * lax.axis_index("y") + lax.axis_index("c")
    tm = _tile(L, 256, 16)
    x2 = x[0]
    tgt = loss_target[0]

    big = [("w_in", "col"), ("w_glu", "row"), ("w_ssm_out", "col"), ("w_conv_out", "col"), ("w_o", "row"),
           ("w_up", "col"), ("w_down", "row")]
    shards = [_cast_bf16("cast_" + n, args[n][0]) for n, _ in big]
    small_in = _pack([conv_w[0], ffn_conv_w[0]])
    kind = dict(big)
    mixw, ffnw = ["w_glu", "w_ssm_out", "w_conv_out", "w_o"], ["w_up", "w_down"]
    shard = dict(zip([n for n, _ in big], shards))
    gathered = _all_gather("gather_w_in", [shard["w_in"], small_in], ["col", "row"])
    W = {"w_in": gathered[0]}
    st_mix, tok_mix = _seq_start("gather_mix", "gather", [shard[n] for n in mixw], [kind[n] for n in mixw], gathered[0])
    st_ffn, tok_ffn = _seq_start("gather_ffn", "gather", [shard[n] for n in ffnw], [kind[n] for n in ffnw], gathered[0])
    cw_parts, fcw_parts = _unpack(gathered[-1].reshape(N_DEV, -1, LANES), [conv_w.shape[1:], ffn_conv_w.shape[1:]], (N_DEV,))
    conv_w_full = jnp.moveaxis(cw_parts, 0, 1).reshape(3, CW)
    ffn_conv_w_full = jnp.moveaxis(fcw_parts, 0, 1).reshape(3, FF)

    ar_row, ai_row = a_re.reshape(1, GP), a_im.reshape(1, GP)
    ldt_row = jnp.broadcast_to(log_dt.reshape(G, 1), (G, P)).reshape(1, GP)
    brt = jnp.transpose(b_re[0], (2, 0, 1)).reshape(H, GP)
    bit = jnp.transpose(b_im[0], (2, 0, 1)).reshape(H, GP)
    abar_re, abar_im, bbar_re, bbar_im = _prep_fwd("s5_prep", ar_row, ai_row, ldt_row, brt, bit)
    eye = jnp.eye(gpb, dtype=F32)

    def b_blocks(bt):
        return jnp.einsum("ab,hjbp->jahbp", eye, bt.reshape(H, nb, gpb, P)).reshape(nb, LANES, gpb * P)

    def c_blocks(c):
        return jnp.einsum("ab,jahp->jbpah", eye, c.reshape(nb, gpb, H, P)).reshape(nb, gpb * P, LANES)

    def diag_blocks(mat):
        return jnp.einsum("jahap->hjap", mat.reshape(nb, gpb, H, gpb, P))

    bm_re, bm_im = b_blocks(bbar_re), b_blocks(bbar_im)
    cm_re, cm_im = c_blocks(c_re[0]), -c_blocks(c_im[0])
    a3_re, a3_im = abar_re.reshape(-1, 8, LANES), abar_im.reshape(-1, 8, LANES)
    dskip_row = d_skip.reshape(1, SW)

    cbs = SW // LANES
    cb_v, cb_gb, cb_gc = cbs, cbs + CW // LANES, cbs + 2 * CW // LANES
    cb_ma = (SW + 3 * CW) // D
    xn = _rms_fwd("rms_tok", x2, norm_tok, tm)
    proj = _mm("proj", xn, W["w_in"], "nn", dep=tok_ffn)
    ts, tw = _tile(L, 512, 16), _tile(L, 1024, 16)
    bu_re, bu_im = _ssm_in("s5_bu", proj, bm_re.astype(BF16), bm_im.astype(BF16), ts)
    xs_re, xs_im = _scan("s5_scan", bu_re, bu_im, a3_re, a3_im)
    y, ya = _ssm_out("s5_y", xs_re, xs_im, cm_re.astype(BF16), cm_im.astype(BF16), proj, dskip_row, ts, post=jax.nn.gelu)
    W.update(zip(mixw, _seq_wait("gather_mix", st_mix, ya)))
    g1 = _mm("glu_gate", ya, W["w_glu"], "nn")
    ya2 = _glu_fwd("glu", y, g1, tm)
    za = _mm("ssm_out", ya2, W["w_ssm_out"], "nn")
    q = _convb_fwd("convb", proj, cb_v, cb_gb, cb_gc, conv_w_full, conv_b)
    zb = _mm("conv_out", q, W["w_conv_out"], "nn")
    merged = _merge_fwd("merge", proj, cb_ma, cb_ma + 1, za, zb, tm)
    o1 = _mm("mix_out", merged, W["w_o"], "nn")
    h1, hn = _res_rms_fwd("rms_ffn", x2, o1, norm_ffn, tm)
    W.update(zip(ffnw, _seq_wait("gather_ffn", st_ffn, hn)))
    hh = _mm("ffn_up", hn, W["w_up"], "nn")
    f = _ffn_fwd("ffn_act", hh, ffn_conv_w_full, ffn_conv_b)
    o2 = _mm("ffn_down", f, W["w_down"], "nn", tk=2816)
    dh2, dh2b, g_norm_final, loss_part = _final("final", h1, o2, norm_final.reshape(1, D), tgt, tm)

    df = _mm("d_ffn_act", dh2b, W["w_down"], "nt", tn=1408)
    gw_down = _mm("gw_down", f, dh2b, "tn", out_dtype=BF16, tm=1408, tn=512, tk=L)
    dhh, g_ffn_conv_w, g_ffn_conv_b = _ffn_bwd("ffn_act_bwd", hh, ffn_conv_w_full, ffn_conv_b, df)
    nhalf = lambda t: FF // t
    dhn = _mm("d_ffn_in", dhh, W["w_up"], "nt", tk=_tile(FF, 2816), dims=(L, D, 2 * FF),
              a_spec=lambda a, b, c: pl.BlockSpec((None, a, c), lambda i, j, k: (k // nhalf(c), i, k % nhalf(c))))
    gw_up = _mm("gw_up", hn, dhh, "tn", out_dtype=BF16, tn=_tile(FF, 1024), tk=L, dims=(D, 2 * FF, L),
                b_spec=lambda a, b, c: pl.BlockSpec((None, c, b), lambda i, j, k: (j // nhalf(b), k, j % nhalf(b))))
    dhn, gw_up, gw_down = lax.optimization_barrier((dhn, gw_up, gw_down))
    st_gffn, tok_gffn = _seq_start("grads_ffn", "exchange", [gw_up, gw_down], [kind[n] for n in ffnw], None)
    dh1, dh1b, g_norm_ffn = _rms_bwd("rms_ffn_bwd", dhn, h1, norm_ffn, dh2, tm, True)

    dmerged = _mm("d_merged", dh1b, W["w_o"], "nt", dep=tok_gffn)
    gw_o = _mm("gw_o", merged, dh1b, "tn", out_dtype=BF16, tk=L)
    dmerged, gw_o = lax.optimization_barrier((dmerged, gw_o))
    dza, dzb, dma, dmb = _merge_bwd("merge_bwd", proj, cb_ma, cb_ma + 1, za, zb, dmerged, tm)
    dq = _mm("d_q", dzb, W["w_conv_out"], "nt")
    gw_conv_out = _mm("gw_conv_out", q, dzb, "tn", out_dtype=BF16, tk=L)
    dq, gw_conv_out = lax.optimization_barrier((dq, gw_conv_out))
    dv, dgb, dgc, g_conv_w, g_conv_b = _convb_bwd("convb_bwd", proj, cb_v, cb_gb, cb_gc, conv_w_full, conv_b, dq)
    dya2 = _mm("d_ya2", dza, W["w_ssm_out"], "nt")
    gw_ssm_out = _mm("gw_ssm_out", ya2, dza, "tn", out_dtype=BF16, tk=L)
    dya2, gw_ssm_out = lax.optimization_barrier((dya2, gw_ssm_out))
    dy_direct, dg1 = _glu_bwd("glu_bwd", y, g1, dya2, tm)
    dya_g = _mm("d_ya_gate", dg1, W["w_glu"], "nt")
    gw_glu = _mm("gw_glu", ya, dg1, "tn", out_dtype=BF16, tk=L)
    dya_g, gw_glu = lax.optimization_barrier((dya_g, gw_glu))
    st_gmix, tok_gmix = _seq_start("grads_mix", "exchange", [gw_glu, gw_ssm_out, gw_conv_out, gw_o],
                                   [kind[n] for n in mixw], None)
    dyb, g_dskip = _gelu_bwd("gelu_bwd", y, dy_direct, dya_g, proj, dskip_row, tm)
    dxs_re, dxs_im = _ssm_in("s5_dx", dyb, jnp.swapaxes(cm_re, 1, 2).astype(BF16), jnp.swapaxes(cm_im, 1, 2).astype(BF16), ts,
                             dep=tok_gmix)
    gc_re, gc_im = _ssm_dw("s5_gc", dyb, xs_re, xs_im, tw)
    lam_re, lam_im, dab_re, dab_im = _scan("s5_scan_bwd", dxs_re, dxs_im, a3_re, a3_im, xs=(xs_re, xs_im))
    parts = dict(zip(ffnw, _seq_wait("grads_ffn", st_gffn, dab_re)))
    du = _ssm_out("s5_du", lam_re, lam_im, jnp.swapaxes(bm_re, 1, 2).astype(BF16), jnp.swapaxes(bm_im, 1, 2).astype(BF16),
                  dyb, dskip_row, ts, post=lambda t: t)[1]
    gb_re, gb_im = _ssm_dw("s5_gb", proj, lam_re, lam_im, tw)
    g_ar, g_ai, g_ldt, g_brt, g_bit = _prep_bwd(
        "s5_prep_bwd", ar_row, ai_row, ldt_row, brt, bit, dab_re.reshape(1, GP), dab_im.reshape(1, GP),
        diag_blocks(gb_re).reshape(H, GP), diag_blocks(gb_im).reshape(H, GP))
    small = dict(
        a_re=g_ar.reshape(1, G, P), a_im=g_ai.reshape(1, G, P),
        log_dt=g_ldt.reshape(G, P).sum(axis=1).reshape(1, G),
        b_re=jnp.transpose(g_brt.reshape(H, G, P), (1, 2, 0))[None], b_im=jnp.transpose(g_bit.reshape(H, G, P), (1, 2, 0))[None],
        c_re=jnp.transpose(diag_blocks(gc_re), (1, 2, 0, 3)).reshape(1, G, H, P),
        c_im=-jnp.transpose(diag_blocks(gc_im), (1, 2, 0, 3)).reshape(1, G, H, P),
        d_skip=g_dskip.reshape(1, G, H), conv_b=g_conv_b, norm_ffn=g_norm_ffn, ffn_conv_b=g_ffn_conv_b,
        norm_final=g_norm_final.reshape(D), conv_w=g_conv_w[None], ffn_conv_w=g_ffn_conv_w[None])
    rep = ["a_re", "a_im", "log_dt", "b_re", "b_im", "c_re", "c_im", "d_skip", "conv_b", "norm_ffn", "ffn_conv_b", "norm_final"]
    order = rep + ["conv_w", "ffn_conv_w"]
    full_shapes = {n: args[n].shape for n in rep}
    full_shapes["conv_w"], full_shapes["ffn_conv_w"] = (1, 3, CW), (1, 3, FF)
    rep_pack = _pack([small[n] for n in rep], LANES)
    rep_rows = rep_pack.shape[0]
    gpack = jnp.concatenate([rep_pack, _pack([small["conv_w"], small["ffn_conv_w"]])], axis=0)
    rows = gpack.shape[0]
    du, gpack = lax.optimization_barrier((du, gpack))
    st_small, tok_small = _seq_start("grads_small", "gather", [gpack], ["row"], None)

    dproj = jnp.concatenate([du, dv, dgb, dgc, dma, dmb], axis=1)
    gw_in = _mm("gw_in", xn, dproj, "tn", out_dtype=BF16, tk=L, dep=tok_small)
    dproj, gw_in = lax.optimization_barrier((dproj, gw_in))
    st_gin, tok_gin = _seq_start("grads_in", "exchange", [gw_in], ["col"], None)
    dxn = _mm("d_xn", dproj, W["w_in"], "nt", dep=tok_gin)
    grad_x, g_norm_tok = _rms_bwd("rms_tok_bwd", dxn, x2, norm_tok, dh1, tm, False)
    nt_all = _all_gather("gather_norm_tok_grad", [_pack([g_norm_tok])], ["row"])[0].reshape(N_DEV, -1, LANES)

    res = {}

    def big_update(n):
        res[n] = [r[None] for r in _adamw("adamw_" + n, parts[n], args[n][0], args["m_" + n][0], args["v_" + n][0])]

    def after(xs, dep):
        return lax.optimization_barrier((list(xs), dep))[0]

    nt_out = _adamw("adamw_norm_tok", nt_all, _pack([norm_tok]), _pack([m_norm_tok]), _pack([v_norm_tok]))
    res["norm_tok"] = [_unpack(r, [norm_tok.shape])[0] for r in nt_out]
    parts = dict(zip(ffnw, after([parts[n] for n in ffnw], nt_out[0])))
    for n in ffnw:
        big_update(n)
    parts.update(zip(mixw, after(_seq_wait("grads_mix", st_gmix, grad_x), res["w_down"][1])))
    for n in mixw:
        big_update(n)
    gall = after(_seq_wait("grads_small", st_small, nt_all), res["w_o"][1])[0].reshape(N_DEV, rows, LANES)
    gcw, gfcw = _unpack(gall[:, rep_rows:], [full_shapes["conv_w"], full_shapes["ffn_conv_w"]], (N_DEV,))
    cws, fcws = CW // N_DEV, FF // N_DEV
    gcw = lax.dynamic_slice_in_dim(gcw[:, 0], me * cws, cws, axis=2)
    gfcw = lax.dynamic_slice_in_dim(gfcw[:, 0], me * fcws, fcws, axis=2)
    res["conv_w"] = [r[None] for r in _adamw("adamw_conv_w", gcw, conv_w[0], m_conv_w[0], v_conv_w[0])]
    res["ffn_conv_w"] = [r[None] for r in _adamw("adamw_ffn_conv_w", gfcw, ffn_conv_w[0], m_ffn_conv_w[0], v_ffn_conv_w[0])]
    rep_out = _adamw("adamw_small", gall[:, :rep_rows], _pack([args[n] for n in rep], LANES),
                     _pack([args["m_" + n] for n in rep], LANES), _pack([args["v_" + n] for n in rep], LANES))
    rep_out = [_unpack(r, [full_shapes[n] for n in rep]) for r in rep_out]
    for i, n in enumerate(rep):
        res[n] = [r[i] for r in rep_out]
    parts["w_in"] = after(_seq_wait("grads_in", st_gin, None), res["a_re"][1])[0]
    big_update("w_in")

    loss = lax.psum(loss_part[0, 0], ("x", "y", "c"))
    names = ["norm_tok", "w_in", "a_re", "a_im", "log_dt", "b_re", "b_im", "c_re", "c_im", "d_skip", "w_glu", "w_ssm_out",
             "conv_w", "conv_b", "w_conv_out", "w_o", "norm_ffn", "w_up", "ffn_conv_w", "ffn_conv_b", "w_down", "norm_final"]
    out = [loss, grad_x[None]]
    for slot in range(4):
        out += [res[n][slot] for n in names]
    return tuple(out)
```

```python
import functools

import jax
import jax.numpy as jnp
from jax import lax
from jax.experimental import pallas as pl
from jax.experimental.pallas import tpu as pltpu
from jax.experimental.pallas import tpu_sc as plsc

F32 = jnp.float32
BF16 = jnp.bfloat16
N_DEV = 8
LANES = 128
SLAB = 4
EPS = 1e-6
ADAM_LR = 0.001
ADAM_B1 = 0.9
ADAM_B2 = 0.999
ADAM_EPS = 1e-08
ADAM_WD = 0.01
ADAM_STEP = 10
VMEM_LIMIT = 56 * 1024 * 1024
MESH = pl.DeviceIdType.MESH


def _tile(n, pref, mult=LANES):
    best = None
    t = mult
    while t <= min(n, pref):
        if n % t == 0:
            best = t
        t += mult
    return best if best is not None else n


def _params(ndim):
    return pltpu.CompilerParams(dimension_semantics=("arbitrary",) * ndim, vmem_limit_bytes=VMEM_LIMIT)


def _sds(shape, dtype):
    return jax.ShapeDtypeStruct(tuple(shape), dtype)


def _mm(name, a, b, mode, *, out_dtype=F32, tm=1024, tn=1024, tk=2048, dims=None, a_spec=None, b_spec=None, dep=None):
    if dims is None:
        if mode == "nn":
            (M, K), N = a.shape, b.shape[1]
        elif mode == "nt":
            (M, K), N = a.shape, b.shape[0]
        else:
            (K, M), N = a.shape, b.shape[1]
    else:
        M, N, K = dims
    tm, tn, tk = _tile(M, tm), _tile(N, tn), _tile(K, tk)
    nk = K // tk
    if mode == "nn":
        dn = (((1,), (0,)), ((), ()))
        sa = pl.BlockSpec((tm, tk), lambda i, j, k: (i, k))
        sb = pl.BlockSpec((tk, tn), lambda i, j, k: (k, j))
    elif mode == "nt":
        dn = (((1,), (1,)), ((), ()))
        sa = pl.BlockSpec((tm, tk), lambda i, j, k: (i, k))
        sb = pl.BlockSpec((tn, tk), lambda i, j, k: (j, k))
    else:
        dn = (((0,), (0,)), ((), ()))
        sa = pl.BlockSpec((tk, tm), lambda i, j, k: (k, i))
        sb = pl.BlockSpec((tk, tn), lambda i, j, k: (k, j))
    sa = a_spec(tm, tn, tk) if a_spec is not None else sa
    sb = b_spec(tm, tn, tk) if b_spec is not None else sb
    use_acc = nk > 1 and out_dtype != F32

    deps = [] if dep is None else [dep]

    def body(a_ref, b_ref, *rest):
        o_ref, acc = rest[len(deps)], rest[len(deps) + 1:]
        k = pl.program_id(2)
        p = lax.dot_general(a_ref[...], b_ref[...], dn, preferred_element_type=F32)
        if nk == 1:
            o_ref[...] = p.astype(out_dtype)
        else:
            tgt = acc[0] if use_acc else o_ref

            @pl.when(k == 0)
            def _():
                tgt[...] = p

            @pl.when(k > 0)
            def _():
                tgt[...] += p

            if use_acc:
                @pl.when(k == nk - 1)
                def _():
                    o_ref[...] = acc[0][...].astype(out_dtype)

    return pl.pallas_call(
        body, name=name, grid=(M // tm, N // tn, nk),
        in_specs=[sa, sb] + [pl.BlockSpec(memory_space=pl.ANY)] * len(deps),
        out_specs=pl.BlockSpec((tm, tn), lambda i, j, k: (i, j)),
        out_shape=_sds((M, N), out_dtype),
        scratch_shapes=[pltpu.VMEM((tm, tn), F32)] if use_acc else [],
        compiler_params=_params(3),
    )(a, b, *deps)


def _rows(name, body, L, tm, ins, outs):
    return pl.pallas_call(
        body, name=name, grid=(L // tm,),
        in_specs=[s for _, s in ins], out_specs=[s for _, s in outs],
        out_shape=[o for o, _ in outs], compiler_params=_params(1),
    )(*[a for a, _ in ins])


def _rs(tm, w, cb=0):
    return pl.BlockSpec((tm, w), lambda i: (i, cb))


def _fs(shape):
    return pl.BlockSpec(tuple(shape), lambda i: (0,) * len(shape))


def _acc_rows(i, ref, part):
    @pl.when(i == 0)
    def _():
        ref[...] = part

    @pl.when(i > 0)
    def _():
        ref[...] += part


def _cast_bf16(name, w):
    R, C = w.shape
    tr = _tile(R, max(16, (1 << 20) // C), 16)

    def body(w_ref, o_ref):
        o_ref[...] = w_ref[...].astype(BF16)

    return _rows(name, body, R, tr, [(w, _rs(tr, C))], [(_sds((R, C), BF16), _rs(tr, C))])[0]


def _rms_fwd(name, x, g, tm):
    L, D = x.shape

    def body(x_ref, g_ref, o_ref):
        xv = x_ref[...]
        r = lax.rsqrt(jnp.mean(xv * xv, axis=-1, keepdims=True) + EPS)
        o_ref[...] = (xv * r * g_ref[...]).astype(BF16)

    return _rows(name, body, L, tm, [(x, _rs(tm, D)), (g, _fs((1, D)))], [(_sds((L, D), BF16), _rs(tm, D))])[0]


def _res_rms_fwd(name, x, o, g, tm):
    L, D = x.shape

    def body(x_ref, o_ref, g_ref, h_ref, hn_ref):
        h = x_ref[...] + o_ref[...]
        r = lax.rsqrt(jnp.mean(h * h, axis=-1, keepdims=True) + EPS)
        h_ref[...] = h
        hn_ref[...] = (h * r * g_ref[...]).astype(BF16)

    return _rows(name, body, L, tm, [(x, _rs(tm, D)), (o, _rs(tm, D)), (g, _fs((1, D)))],
                 [(_sds((L, D), F32), _rs(tm, D)), (_sds((L, D), BF16), _rs(tm, D))])


def _rms_bwd(name, dn, h, g, dres, tm, with_bf16):
    L, D = h.shape

    def body(dn_ref, h_ref, g_ref, dres_ref, dh_ref, *rest):
        i = pl.program_id(0)
        h = h_ref[...]
        r = lax.rsqrt(jnp.mean(h * h, axis=-1, keepdims=True) + EPS)
        xh = h * r
        d = dn_ref[...]
        dxh = d * g_ref[...]
        dh = dres_ref[...] + r * (dxh - xh * jnp.mean(dxh * xh, axis=-1, keepdims=True))
        dh_ref[...] = dh
        if with_bf16:
            rest[0][...] = dh.astype(BF16)
        _acc_rows(i, rest[-1], jnp.sum(d * xh, axis=0, keepdims=True))

    outs = [(_sds((L, D), F32), _rs(tm, D))]
    if with_bf16:
        outs.append((_sds((L, D), BF16), _rs(tm, D)))
    outs.append((_sds((1, D), F32), _fs((1, D))))
    return _rows(name, body, L, tm, [(dn, _rs(tm, D)), (h, _rs(tm, D)), (g, _fs((1, D))), (dres, _rs(tm, D))], outs)


def _final(name, h1, o2, g, tgt, tm):
    L, D = h1.shape

    def body(h1_ref, o2_ref, g_ref, t_ref, dh_ref, dhb_ref, dg_ref, loss_ref):
        i = pl.program_id(0)
        h = h1_ref[...] + o2_ref[...]
        r = lax.rsqrt(jnp.mean(h * h, axis=-1, keepdims=True) + EPS)
        xh = h * r
        gv = g_ref[...]
        e = xh * gv - t_ref[...]
        part = 0.5 * jnp.sum(jnp.mean(e * e, axis=-1, keepdims=True), axis=0, keepdims=True)
        dy = e / D
        dxh = dy * gv
        dh = r * (dxh - xh * jnp.mean(dxh * xh, axis=-1, keepdims=True))
        dh_ref[...] = dh
        dhb_ref[...] = dh.astype(BF16)
        _acc_rows(i, dg_ref, jnp.sum(dy * xh, axis=0, keepdims=True))
        _acc_rows(i, loss_ref, jnp.broadcast_to(part, (8, LANES)))

    return _rows(name, body, L, tm,
                 [(h1, _rs(tm, D)), (o2, _rs(tm, D)), (g, _fs((1, D))), (tgt, _rs(tm, D))],
                 [(_sds((L, D), F32), _rs(tm, D)), (_sds((L, D), BF16), _rs(tm, D)),
                  (_sds((1, D), F32), _fs((1, D))), (_sds((8, LANES), F32), _fs((8, LANES)))])


def _glu_fn(y, g1):
    ya = jax.nn.gelu(y)
    return ya * jax.nn.sigmoid(g1)


def _glu_fwd(name, y, g1, tm):
    L, W = y.shape

    def body(y_ref, g_ref, o_ref):
        o_ref[...] = _glu_fn(y_ref[...], g_ref[...]).astype(BF16)

    return _rows(name, body, L, tm, [(y, _rs(tm, W)), (g1, _rs(tm, W))], [(_sds((L, W), BF16), _rs(tm, W))])[0]


def _glu_bwd(name, y, g1, dya2, tm):
    L, W = y.shape

    def body(y_ref, g_ref, d_ref, dy_ref, dg_ref):
        _, vjp = jax.vjp(_glu_fn, y_ref[...], g_ref[...])
        dy, dg = vjp(d_ref[...])
        dy_ref[...] = dy
        dg_ref[...] = dg.astype(BF16)

    return _rows(name, body, L, tm, [(y, _rs(tm, W)), (g1, _rs(tm, W)), (dya2, _rs(tm, W))],
                 [(_sds((L, W), F32), _rs(tm, W)), (_sds((L, W), BF16), _rs(tm, W))])


def _gelu_bwd(name, y, dy_direct, dya_g, proj, dskip, tm):
    L, W = y.shape

    def body(y_ref, dd_ref, dg_ref, u_ref, dyb_ref, dsk_ref):
        i = pl.program_id(0)
        _, vjp = jax.vjp(jax.nn.gelu, y_ref[...])
        dy = dd_ref[...] + vjp(dg_ref[...])[0]
        dyb_ref[...] = dy.astype(BF16)
        _acc_rows(i, dsk_ref, jnp.sum(dy * u_ref[...], axis=0, keepdims=True))

    del dskip
    return _rows(name, body, L, tm,
                 [(y, _rs(tm, W)), (dy_direct, _rs(tm, W)), (dya_g, _rs(tm, W)), (proj, _rs(tm, W, 0))],
                 [(_sds((L, W), BF16), _rs(tm, W)), (_sds((1, W), F32), _fs((1, W)))])


def _merge_fn(ma, mb, za, zb):
    return jax.nn.sigmoid(ma) * za + jax.nn.sigmoid(mb) * zb


def _merge_fwd(name, proj, cb_a, cb_b, za, zb, tm):
    L, D = za.shape

    def body(ma_ref, mb_ref, za_ref, zb_ref, o_ref):
        o_ref[...] = _merge_fn(ma_ref[...], mb_ref[...], za_ref[...], zb_ref[...]).astype(BF16)

    return _rows(name, body, L, tm,
                 [(proj, _rs(tm, D, cb_a)), (proj, _rs(tm, D, cb_b)), (za, _rs(tm, D)), (zb, _rs(tm, D))],
                 [(_sds((L, D), BF16), _rs(tm, D))])[0]


def _merge_bwd(name, proj, cb_a, cb_b, za, zb, dmerged, tm):
    L, D = za.shape

    def body(ma_ref, mb_ref, za_ref, zb_ref, d_ref, dza_ref, dzb_ref, dma_ref, dmb_ref):
        _, vjp = jax.vjp(_merge_fn, ma_ref[...], mb_ref[...], za_ref[...], zb_ref[...])
        dma, dmb, dza, dzb = vjp(d_ref[...])
        dza_ref[...] = dza.astype(BF16)
        dzb_ref[...] = dzb.astype(BF16)
        dma_ref[...] = dma.astype(BF16)
        dmb_ref[...] = dmb.astype(BF16)

    return _rows(name, body, L, tm,
                 [(proj, _rs(tm, D, cb_a)), (proj, _rs(tm, D, cb_b)), (za, _rs(tm, D)), (zb, _rs(tm, D)),
                  (dmerged, _rs(tm, D))],
                 [(_sds((L, D), BF16), _rs(tm, D)), (_sds((L, D), BF16), _rs(tm, D)),
                  (_sds((L, D), BF16), _rs(tm, D)), (_sds((L, D), BF16), _rs(tm, D))])


def _shift_down(x, k):
    row = lax.broadcasted_iota(jnp.int32, x.shape, 0)
    return jnp.where(row >= k, pltpu.roll(x, k, axis=0), 0.0)


def _shift_up(x, k):
    n = x.shape[0]
    row = lax.broadcasted_iota(jnp.int32, x.shape, 0)
    return jnp.where(row < n - k, pltpu.roll(x, n - k, axis=0), 0.0)


def _conv3(cv, w_ref, b_ref):
    return (w_ref[2:3, :] * cv + w_ref[1:2, :] * _shift_down(cv, 1) + w_ref[0:1, :] * _shift_down(cv, 2)
            + b_ref[...])


def _conv3_bwd(dcc, cv, w_ref):
    dcv = w_ref[2:3, :] * dcc + w_ref[1:2, :] * _shift_up(dcc, 1) + w_ref[0:1, :] * _shift_up(dcc, 2)
    dw = [jnp.sum(dcc * _shift_down(cv, 2), axis=0, keepdims=True),
          jnp.sum(dcc * _shift_down(cv, 1), axis=0, keepdims=True),
          jnp.sum(dcc * cv, axis=0, keepdims=True)]
    db = jnp.sum(dcc, axis=0, keepdims=True)
    return dcv, dw, db


def _store_rows(ref, rows):
    for r, val in enumerate(rows):
        ref[r:r + 1, :] = val


def _cols(name, body, ncb, ins, outs):
    return pl.pallas_call(
        body, name=name, grid=(ncb,),
        in_specs=[s for _, s in ins], out_specs=[s for _, s in outs],
        out_shape=[o for o, _ in outs], compiler_params=_params(1),
    )(*[a for a, _ in ins])


def _cb(L, w, off=0):
    return pl.BlockSpec((L, w), lambda j: (0, j + off))


def _convb_fwd(name, proj, cb_v, cb_gb, cb_gc, w, b):
    L = proj.shape[0]
    W = w.shape[1]
    c = LANES

    def body(v_ref, gb_ref, gc_ref, w_ref, b_ref, q_ref):
        cc = _conv3(gc_ref[...] * v_ref[...], w_ref, b_ref)
        q_ref[...] = (gb_ref[...] * cc).astype(BF16)

    return _cols(name, body, W // c,
                 [(proj, _cb(L, c, cb_v)), (proj, _cb(L, c, cb_gb)), (proj, _cb(L, c, cb_gc)),
                  (w, _cb(3, c)), (b, _cb(1, c))],
                 [(_sds((L, W), BF16), _cb(L, c))])[0]


def _convb_bwd(name, proj, cb_v, cb_gb, cb_gc, w, b, dq):
    L = proj.shape[0]
    W = w.shape[1]
    c = LANES

    def body(v_ref, gb_ref, gc_ref, w_ref, b_ref, dq_ref, dv_ref, dgb_ref, dgc_ref, dw_ref, db_ref):
        v, gc = v_ref[...], gc_ref[...]
        cv = gc * v
        cc = _conv3(cv, w_ref, b_ref)
        dq = dq_ref[...]
        dgb_ref[...] = (dq * cc).astype(BF16)
        dcv, dw, db = _conv3_bwd(dq * gb_ref[...], cv, w_ref)
        dv_ref[...] = (dcv * gc).astype(BF16)
        dgc_ref[...] = (dcv * v).astype(BF16)
        _store_rows(dw_ref, dw)
        db_ref[...] = db

    return _cols(name, body, W // c,
                 [(proj, _cb(L, c, cb_v)), (proj, _cb(L, c, cb_gb)), (proj, _cb(L, c, cb_gc)),
                  (w, _cb(3, c)), (b, _cb(1, c)), (dq, _cb(L, c))],
                 [(_sds((L, W), BF16), _cb(L, c)), (_sds((L, W), BF16), _cb(L, c)), (_sds((L, W), BF16), _cb(L, c)),
                  (_sds((3, W), F32), _cb(3, c)), (_sds((1, W), F32), _cb(1, c))])


def _ffn_fwd(name, hh, w, b):
    L = hh.shape[0]
    Fw = w.shape[1]
    c = LANES
    nf = Fw // c

    def body(a_ref, h2_ref, w_ref, b_ref, f_ref):
        a = _conv3(a_ref[...], w_ref, b_ref)
        f_ref[...] = (jax.nn.gelu(a) * h2_ref[...]).astype(BF16)

    return _cols(name, body, nf, [(hh, _cb(L, c)), (hh, _cb(L, c, nf)), (w, _cb(3, c)), (b, _cb(1, c))],
                 [(_sds((L, Fw), BF16), _cb(L, c))])[0]


def _ffn_bwd(name, hh, w, b, df):
    L = hh.shape[0]
    Fw = w.shape[1]
    c = LANES
    nf = Fw // c

    def body(a_ref, h2_ref, w_ref, b_ref, df_ref, dhh_ref, dw_ref, db_ref):
        h1 = a_ref[...]
        a = _conv3(h1, w_ref, b_ref)
        ga, vjp = jax.vjp(jax.nn.gelu, a)
        d = df_ref[...]
        dhh_ref[1] = (d * ga).astype(BF16)
        da = vjp(d * h2_ref[...])[0]
        dh1, dw, db = _conv3_bwd(da, h1, w_ref)
        dhh_ref[0] = dh1.astype(BF16)
        _store_rows(dw_ref, dw)
        db_ref[...] = db

    return _cols(name, body, nf,
                 [(hh, _cb(L, c)), (hh, _cb(L, c, nf)), (w, _cb(3, c)), (b, _cb(1, c)), (df, _cb(L, c))],
                 [(_sds((2, L, Fw), BF16), pl.BlockSpec((2, L, c), lambda j: (0, 0, j))),
                  (_sds((3, Fw), F32), _cb(3, c)), (_sds((1, Fw), F32), _cb(1, c))])


def _prep_fn(ar, ai, ldt, brt, bit):
    dt = jnp.exp(ldt)
    mag = jnp.exp(dt * ar)
    are = mag * jnp.cos(dt * ai)
    aim = mag * jnp.sin(dt * ai)
    nr = are - 1.0
    ni = aim
    den = ar * ar + ai * ai
    fr = (nr * ar + ni * ai) / den
    fi = (ni * ar - nr * ai) / den
    return are, aim, fr * brt - fi * bit, fr * bit + fi * brt


def _prep_fwd(name, ar, ai, ldt, brt, bit):
    def body(ar_ref, ai_ref, l_ref, br_ref, bi_ref, o1, o2, o3, o4):
        o1[...], o2[...], o3[...], o4[...] = _prep_fn(ar_ref[...], ai_ref[...], l_ref[...], br_ref[...], bi_ref[...])

    return pl.pallas_call(body, name=name,
                          out_shape=[_sds(ar.shape, F32), _sds(ar.shape, F32), _sds(brt.shape, F32), _sds(brt.shape, F32)],
                          )(ar, ai, ldt, brt, bit)


def _prep_bwd(name, ar, ai, ldt, brt, bit, g1, g2, g3, g4):
    def body(ar_ref, ai_ref, l_ref, br_ref, bi_ref, g1_ref, g2_ref, g3_ref, g4_ref, o1, o2, o3, o4, o5):
        _, vjp = jax.vjp(_prep_fn, ar_ref[...], ai_ref[...], l_ref[...], br_ref[...], bi_ref[...])
        o1[...], o2[...], o3[...], o4[...], o5[...] = vjp((g1_ref[...], g2_ref[...], g3_ref[...], g4_ref[...]))

    return pl.pallas_call(body, name=name,
                          out_shape=[_sds(ar.shape, F32)] * 3 + [_sds(brt.shape, F32)] * 2,
                          )(ar, ai, ldt, brt, bit, g1, g2, g3, g4)


def _ssm_in(name, src, m1, m2, tm, dep=None):
    L = src.shape[0]
    nb = m1.shape[0]

    deps = [] if dep is None else [dep]

    def body(s_ref, m1_ref, m2_ref, *rest):
        o1_ref, o2_ref = rest[len(deps):]
        u = s_ref[...].astype(BF16)
        r1 = jnp.dot(u, m1_ref[...], preferred_element_type=F32)
        r2 = jnp.dot(u, m2_ref[...], preferred_element_type=F32)
        for q in range(SLAB):
            o1_ref[q] = r1[:, q * LANES:(q + 1) * LANES].astype(BF16)
            o2_ref[q] = r2[:, q * LANES:(q + 1) * LANES].astype(BF16)

    ms = pl.BlockSpec((None, LANES, SLAB * LANES), lambda i, j: (j, 0, 0))
    os_ = pl.BlockSpec((SLAB, tm, LANES), lambda i, j: (j, i, 0))
    return pl.pallas_call(
        body, name=name, grid=(L // tm, nb),
        in_specs=[pl.BlockSpec((tm, LANES), lambda i, j: (i, j)), ms, ms] + [pl.BlockSpec(memory_space=pl.ANY)] * len(deps),
        out_specs=[os_, os_],
        out_shape=[_sds((SLAB * nb, L, LANES), BF16)] * 2, compiler_params=_params(2),
    )(src, m1, m2, *deps)


def _ssm_out(name, x1, x2, m1, m2, aux, dvec, tm, post=None):
    L = x1.shape[1]
    nb = m1.shape[0]

    def body(x1_ref, x2_ref, m1_ref, m2_ref, a_ref, d_ref, o_ref, *rest):
        a1 = jnp.concatenate([x1_ref[q] for q in range(SLAB)], axis=1).astype(BF16)
        a2 = jnp.concatenate([x2_ref[q] for q in range(SLAB)], axis=1).astype(BF16)
        y = (jnp.dot(a1, m1_ref[...], preferred_element_type=F32) + jnp.dot(a2, m2_ref[...], preferred_element_type=F32)
             + d_ref[...] * a_ref[...].astype(F32))
        o_ref[...] = y
        if post is not None:
            rest[0][...] = post(y).astype(BF16)

    xs = pl.BlockSpec((SLAB, tm, LANES), lambda i, j: (j, i, 0))
    ms = pl.BlockSpec((None, SLAB * LANES, LANES), lambda i, j: (j, 0, 0))
    cs = pl.BlockSpec((tm, LANES), lambda i, j: (i, j))
    W = nb * LANES
    outs, ospecs = [_sds((L, W), F32)], [cs]
    if post is not None:
        outs.append(_sds((L, W), BF16))
        ospecs.append(cs)
    return pl.pallas_call(
        body, name=name, grid=(L // tm, nb),
        in_specs=[xs, xs, ms, ms, cs, pl.BlockSpec((1, LANES), lambda i, j: (0, j))], out_specs=ospecs,
        out_shape=outs, compiler_params=_params(2),
    )(x1, x2, m1, m2, aux, dvec)


def _ssm_dw(name, src, x1, x2, tk):
    L = src.shape[0]
    nb = x1.shape[0] // SLAB
    dn = (((0,), (0,)), ((), ()))

    def body(s_ref, x1_ref, x2_ref, o1_ref, o2_ref):
        k = pl.program_id(1)
        s = s_ref[...].astype(BF16)
        a1 = jnp.concatenate([x1_ref[q] for q in range(SLAB)], axis=1).astype(BF16)
        a2 = jnp.concatenate([x2_ref[q] for q in range(SLAB)], axis=1).astype(BF16)
        _acc_rows(k, o1_ref, lax.dot_general(s, a1, dn, preferred_element_type=F32))
        _acc_rows(k, o2_ref, lax.dot_general(s, a2, dn, preferred_element_type=F32))

    xs = pl.BlockSpec((SLAB, tk, LANES), lambda j, k: (j, k, 0))
    os_ = pl.BlockSpec((None, LANES, SLAB * LANES), lambda j, k: (j, 0, 0))
    return pl.pallas_call(
        body, name=name, grid=(nb, L // tk),
        in_specs=[pl.BlockSpec((tk, LANES), lambda j, k: (k, j)), xs, xs], out_specs=[os_, os_],
        out_shape=[_sds((nb, LANES, SLAB * LANES), F32)] * 2, compiler_params=_params(2),
    )(src, x1, x2)


def _scan(name, b_re, b_im, a_re, a_im, xs=None):
    reverse = xs is not None
    ns, L, _ = b_re.shape
    ng = ns // 8
    tc = min(LANES, L)
    pitch = tc + 8
    nt = L // tc
    n_in = 4 if reverse else 2

    def body(*refs):
        ins = refs[:n_in]
        ar_ref, ai_ref = refs[n_in], refs[n_in + 1]
        o_re, o_im = refs[n_in + 2], refs[n_in + 3]
        k = n_in + 4
        if reverse:
            da_re, da_im = refs[k], refs[k + 1]
            k += 2
        stage = refs[k:k + n_in]
        out_re, out_im, st_re, st_im = refs[k + n_in:k + n_in + 4]
        acc = refs[k + n_in + 4:]
        i = pl.program_id(0)

        @pl.when(i == 0)
        def _():
            st_re[...] = jnp.zeros(st_re.shape, F32)
            st_im[...] = jnp.zeros(st_im.shape, F32)
            for r in acc:
                r[...] = jnp.zeros(r.shape, F32)

        for s in range(ns):
            for src, dst in zip(ins, stage):
                dst[pl.ds(s * pitch, tc), :] = src[s].astype(F32)

        a_r = [ar_ref[g] for g in range(ng)]
        a_i = [ai_ref[g] for g in range(ng)]

        def step(tt, carry):
            t = (tc - 1 - tt) if reverse else tt
            new = []
            for g in range(ng):
                rows = pl.ds(g * 8 * pitch + t, 8, stride=pitch)
                cr, ci = carry[2 * g], carry[2 * g + 1]
                br, bi = stage[0][rows, :], stage[1][rows, :]
                if reverse:
                    xr, xi = stage[2][rows, :], stage[3][rows, :]
                    acc[0][g] += xr * cr + xi * ci
                    acc[1][g] += xr * ci - xi * cr
                    nr = a_r[g] * cr + a_i[g] * ci + br
                    ni = a_r[g] * ci - a_i[g] * cr + bi
                else:
                    nr = a_r[g] * cr - a_i[g] * ci + br
                    ni = a_r[g] * ci + a_i[g] * cr + bi
                out_re[rows, :] = nr
                out_im[rows, :] = ni
                new += [nr, ni]
            return tuple(new)

        init = []
        for g in range(ng):
            init += [st_re[g], st_im[g]]
        fin = lax.fori_loop(0, tc, step, tuple(init), unroll=2)
        for g in range(ng):
            st_re[g] = fin[2 * g]
            st_im[g] = fin[2 * g + 1]
        for s in range(ns):
            o_re[s] = out_re[pl.ds(s * pitch, tc), :].astype(BF16)
            o_im[s] = out_im[pl.ds(s * pitch, tc), :].astype(BF16)
        if reverse:
            da_re[...] = acc[0][...]
            da_im[...] = acc[1][...]

    tmap = (lambda i: (0, nt - 1 - i, 0)) if reverse else (lambda i: (0, i, 0))
    bs = pl.BlockSpec((ns, tc, LANES), tmap)
    as_ = pl.BlockSpec((ng, 8, LANES), lambda i: (0, 0, 0))
    ins = [b_re, b_im] + (list(xs) if reverse else [])
    out_shape = [_sds((ns, L, LANES), BF16)] * 2 + ([_sds((ng, 8, LANES), F32)] * 2 if reverse else [])
    out_specs = [bs, bs] + ([as_, as_] if reverse else [])
    scratch = [pltpu.VMEM((ns * pitch, LANES), F32)] * (n_in + 2) + [pltpu.VMEM((ng, 8, LANES), F32)] * (4 if reverse else 2)
    return pl.pallas_call(
        body, name=name, grid=(nt,), in_specs=[bs] * n_in + [as_, as_], out_specs=out_specs,
        out_shape=out_shape, scratch_shapes=scratch, compiler_params=_params(1),
    )(*ins, a_re, a_im)


def _peer(k):
    x, y, c = lax.axis_index("x"), lax.axis_index("y"), lax.axis_index("c")
    px = 1 - x if (k >> 2) & 1 else x
    py = 1 - y if (k >> 1) & 1 else y
    pc = 1 - c if k & 1 else c
    return (px, py, pc), 4 * px + 2 * py + pc


def _window(ref, kind, idx, n):
    if kind == "col":
        w = ref.shape[1] // n
        return ref.at[:, pl.ds(pl.multiple_of(idx * w, LANES), w)]
    r = ref.shape[0] // n
    return ref.at[pl.ds(pl.multiple_of(idx * r, 8), r), :]


def _all_gather(name, shards, kinds):
    n = len(shards)
    fulls = []
    for s, kind in zip(shards, kinds):
        fulls.append(_sds((s.shape[0], s.shape[1] * N_DEV) if kind == "col" else (s.shape[0] * N_DEV, s.shape[1]), s.dtype))

    def body(*refs):
        src, dst = refs[:n], refs[n:2 * n]
        send, recv, loc = refs[2 * n:]
        me = 4 * lax.axis_index("x") + 2 * lax.axis_index("y") + lax.axis_index("c")
        copies = []
        for a in range(n):
            own = pltpu.make_async_copy(src[a], _window(dst[a], kinds[a], me, N_DEV), loc.at[a])
            own.start()
            copies.append(own)
        sends = []
        for k in range(1, N_DEV):
            dev, _ = _peer(k)
            for a in range(n):
                cp = pltpu.make_async_remote_copy(
                    src_ref=src[a], dst_ref=_window(dst[a], kinds[a], me, N_DEV),
                    send_sem=send.at[a * N_DEV + k], recv_sem=recv.at[a * N_DEV + k],
                    device_id=dev, device_id_type=MESH)
                cp.start()
                sends.append(cp)
        for k in range(1, N_DEV):
            dev, pidx = _peer(k)
            for a in range(n):
                pltpu.make_async_remote_copy(
                    src_ref=src[a], dst_ref=_window(dst[a], kinds[a], pidx, N_DEV),
                    send_sem=send.at[a * N_DEV + k], recv_sem=recv.at[a * N_DEV + k],
                    device_id=dev, device_id_type=MESH).wait_recv()
        for cp in sends:
            cp.wait_send()
        for cp in copies:
            cp.wait()

    any_ = pl.BlockSpec(memory_space=pl.ANY)
    return pl.pallas_call(
        body, name=name, in_specs=[any_] * n, out_specs=[any_] * n, out_shape=fulls,
        scratch_shapes=[pltpu.SemaphoreType.DMA((n * N_DEV,)), pltpu.SemaphoreType.DMA((n * N_DEV,)),
                        pltpu.SemaphoreType.DMA((n,))],
        compiler_params=pltpu.CompilerParams(has_side_effects=True),
    )(*shards)


_HBM = pl.BlockSpec(memory_space=pltpu.HBM)
_SEM = pl.BlockSpec(memory_space=pltpu.SEMAPHORE)
_ANY = pl.BlockSpec(memory_space=pl.ANY)
_EFFECT = pltpu.SideEffectType.DATAFLOW_SIDE_EFFECTING


def _xfer_refs(mode, kinds, a, src, dst, me, pidx):
    if mode == "gather":
        return src[a], _window(dst[a], kinds[a], me, N_DEV), _window(dst[a], kinds[a], pidx, N_DEV)
    return _window(src[a], kinds[a], pidx, N_DEV), dst[a].at[me], dst[a].at[pidx]


def _xfer_out_shapes(mode, arrs, kinds):
    outs = []
    for s, kind in zip(arrs, kinds):
        if mode == "gather":
            outs.append((s.shape[0], s.shape[1] * N_DEV) if kind == "col" else (s.shape[0] * N_DEV, s.shape[1]))
        else:
            outs.append((N_DEV,) + ((s.shape[0], s.shape[1] // N_DEV) if kind == "col" else (s.shape[0] // N_DEV, s.shape[1])))
    return outs


def _xfer_start(name, mode, arrs, kinds, after):
    n = len(arrs)
    shapes = _xfer_out_shapes(mode, arrs, kinds)

    def body(*refs):
        src, dst = refs[:n], refs[n:2 * n]
        send, recv = refs[2 * n + 1], refs[2 * n + 2]
        token, loc = refs[2 * n + 3 + 2 * n], refs[2 * n + 4 + 2 * n]
        me = 4 * lax.axis_index("x") + 2 * lax.axis_index("y") + lax.axis_index("c")
        own = []
        for a in range(n):
            s, _, d = _xfer_refs(mode, kinds, a, src, dst, me, me)
            own.append(pltpu.make_async_copy(s, d, loc.at[a]))
            own[-1].start()
        for cp in own:
            cp.wait()
        for k in range(1, N_DEV):
            dev, pidx = _peer(k)
            for a in range(n):
                s, d, _ = _xfer_refs(mode, kinds, a, src, dst, me, pidx)
                pltpu.make_async_remote_copy(src_ref=s, dst_ref=d, send_sem=send.at[a * N_DEV + k],
                                             recv_sem=recv.at[a * N_DEV + k], device_id=dev, device_id_type=MESH).start()
        token[...] = jnp.zeros(token.shape, F32)

    lands = [pltpu.with_memory_space_constraint(lax.empty(shp, s.dtype), pltpu.HBM) for shp, s in zip(shapes, arrs)]
    srcs = [pltpu.with_memory_space_constraint(s, pltpu.HBM) for s in arrs]
    res = pl.pallas_call(
        body, name=name,
        in_specs=[_HBM] * (2 * n) + [_ANY],
        out_specs=[_SEM, _SEM] + [_HBM] * (2 * n) + [pl.BlockSpec(memory_space=pltpu.VMEM)],
        out_shape=[pltpu.SemaphoreType.DMA((n * N_DEV,)), pltpu.SemaphoreType.DMA((n * N_DEV,))]
        + [pltpu.HBM(s.shape, s.dtype) for s in arrs] + [pltpu.HBM(shp, s.dtype) for shp, s in zip(shapes, arrs)]
        + [_sds((8, LANES), F32)],
        input_output_aliases={i: 2 + i for i in range(2 * n)},
        scratch_shapes=[pltpu.SemaphoreType.DMA((n,))],
        compiler_params=pltpu.CompilerParams(has_side_effects=_EFFECT),
    )(*srcs, *lands, after)
    return dict(mode=mode, kinds=kinds, n=n, send=res[0], recv=res[1], srcs=res[2:2 + n], lands=res[2 + n:2 + 2 * n]), res[-1]


def _xfer_wait(name, st, after):
    n, mode, kinds = st["n"], st["mode"], st["kinds"]

    def body(*refs):
        src, dst = refs[:n], refs[n:2 * n]
        send, recv = refs[2 * n], refs[2 * n + 1]
        me = 4 * lax.axis_index("x") + 2 * lax.axis_index("y") + lax.axis_index("c")
        for k in range(1, N_DEV):
            dev, pidx = _peer(k)
            for a in range(n):
                s, d, land = _xfer_refs(mode, kinds, a, src, dst, me, pidx)
                cp = pltpu.make_async_remote_copy(src_ref=s, dst_ref=land, send_sem=send.at[a * N_DEV + k],
                                                  recv_sem=recv.at[a * N_DEV + k], device_id=dev, device_id_type=MESH)
                cp.wait_send()
                cp.wait_recv()

    res = pl.pallas_call(
        body, name=name,
        in_specs=[_HBM] * (2 * n) + [_SEM, _SEM, _ANY],
        out_specs=[_HBM] * (2 * n),
        out_shape=[pltpu.HBM(s.shape, s.dtype) for s in st["srcs"]] + [pltpu.HBM(s.shape, s.dtype) for s in st["lands"]],
        input_output_aliases={i: i for i in range(2 * n)},
        compiler_params=pltpu.CompilerParams(has_side_effects=_EFFECT),
    )(*st["srcs"], *st["lands"], st["send"], st["recv"], after)
    return list(res[n:])


def _sc_xfer(name, mode, arrs, kinds, collective_id):
    n = len(arrs)
    shapes = _xfer_out_shapes(mode, arrs, kinds)
    hbm = pltpu.MemorySpace.HBM
    src = [jax.new_ref(a, memory_space=hbm) for a in arrs]
    dst = [jax.empty_ref(_sds(shp, a.dtype), memory_space=hbm) for shp, a in zip(shapes, arrs)]

    @pl.kernel(mesh=plsc.ScalarSubcoreMesh(axis_name="seq", num_cores=1), name=name,
               scratch_types=(pltpu.SemaphoreType.DMA((n * N_DEV,)), pltpu.SemaphoreType.DMA((n * N_DEV,)),
                              pltpu.SemaphoreType.DMA((n,))),
               compiler_params=pltpu.CompilerParams(collective_id=collective_id))
    def launch(send, recv, loc):
        barrier = pltpu.get_barrier_semaphore()
        for k in range(1, N_DEV):
            pl.semaphore_signal(barrier, inc=1, device_id=_peer(k)[0], device_id_type=MESH)
        pl.semaphore_wait(barrier, N_DEV - 1)
        me = 4 * lax.axis_index("x") + 2 * lax.axis_index("y") + lax.axis_index("c")
        own, sends = [], []
        for a in range(n):
            s, _, d = _xfer_refs(mode, kinds, a, src, dst, me, me)
            own.append(pltpu.make_async_copy(s, d, loc.at[a]))
            own[-1].start()
        for k in range(1, N_DEV):
            dev, pidx = _peer(k)
            for a in range(n):
                s, d, _ = _xfer_refs(mode, kinds, a, src, dst, me, pidx)
                sends.append(pltpu.make_async_remote_copy(src_ref=s, dst_ref=d, send_sem=send.at[a * N_DEV + k],
                                                          recv_sem=recv.at[a * N_DEV + k], device_id=dev, device_id_type=MESH))
                sends[-1].start()
        for cp in own:
            cp.wait()
        for k in range(1, N_DEV):
            dev, pidx = _peer(k)
            for a in range(n):
                s, _, land = _xfer_refs(mode, kinds, a, src, dst, me, pidx)
                pltpu.make_async_remote_copy(src_ref=s, dst_ref=land, send_sem=send.at[a * N_DEV + k],
                                             recv_sem=recv.at[a * N_DEV + k], device_id=dev, device_id_type=MESH).wait_recv()
        for cp in sends:
            cp.wait_send()

    launch()
    return [d[...] for d in dst]


def _sc_gather(name, arrs, kinds, collective_id):
    n = len(arrs)
    pairs = 7
    shapes = _xfer_out_shapes("gather", arrs, kinds)
    hbm = pltpu.MemorySpace.HBM
    src = [jax.new_ref(a, memory_space=hbm) for a in arrs]
    dst = [jax.empty_ref(_sds(shp, a.dtype), memory_space=hbm) for shp, a in zip(shapes, arrs)]

    @pl.kernel(mesh=plsc.ScalarSubcoreMesh(axis_name="seq", num_cores=1), name=name,
               scratch_types=(pltpu.SemaphoreType.DMA((n * pairs,)), pltpu.SemaphoreType.DMA((n * pairs,)),
                              pltpu.SemaphoreType.DMA((n,))),
               compiler_params=pltpu.CompilerParams(collective_id=collective_id))
    def launch(send, recv, loc):
        x, y, c = lax.axis_index("x"), lax.axis_index("y"), lax.axis_index("c")
        me = 4 * x + 2 * y + c
        sib = (x, y, 1 - c)
        chips = []
        for fx, fy in ((1, 0), (0, 1), (1, 1)):
            px, py = (1 - x if fx else x), (1 - y if fy else y)
            chips.append(((px, py, c), 4 * px + 2 * py + c, 4 * px + 2 * py + (1 - c)))
        barrier = pltpu.get_barrier_semaphore()
        for dev in [sib] + [ch[0] for ch in chips]:
            pl.semaphore_signal(barrier, inc=1, device_id=dev, device_id_type=MESH)
        pl.semaphore_wait(barrier, 4)

        def win(a, idx):
            return _window(dst[a], kinds[a], idx, N_DEV)

        def rcopy(a, p, s, d, dev):
            return pltpu.make_async_remote_copy(src_ref=s, dst_ref=d, send_sem=send.at[a * pairs + p],
                                                recv_sem=recv.at[a * pairs + p], device_id=dev, device_id_type=MESH)

        own, sends = [], []
        for a in range(n):
            own.append(pltpu.make_async_copy(src[a], win(a, me), loc.at[a]))
            own[-1].start()
        for j, (dev, _, _) in enumerate(chips):
            for a in range(n):
                sends.append(rcopy(a, 1 + j, src[a], win(a, me), dev))
                sends[-1].start()
        for a in range(n):
            sends.append(rcopy(a, 0, src[a], win(a, me), sib))
            sends[-1].start()
        for j, (dev, idx, _) in enumerate(chips):
            for a in range(n):
                rcopy(a, 1 + j, src[a], win(a, idx), dev).wait_recv()
                sends.append(rcopy(a, 4 + j, win(a, idx), win(a, idx), sib))
                sends[-1].start()
        for cp in own:
            cp.wait()
        for a in range(n):
            rcopy(a, 0, src[a], win(a, 4 * x + 2 * y + (1 - c)), sib).wait_recv()
        for j, (_, _, sidx) in enumerate(chips):
            for a in range(n):
                rcopy(a, 4 + j, src[a], win(a, sidx), sib).wait_recv()
        for cp in sends:
            cp.wait_send()

    launch()
    return [d[...] for d in dst]


_SEQ_IDS = {"gather_in": 7, "gather_mix": 1, "gather_ffn": 2, "grads_ffn": 3, "grads_mix": 4, "grads_small": 5, "grads_in": 6}


def _seq_start(name, mode, arrs, kinds, after):
    arrs = list(arrs)
    if after is not None:
        after, *arrs = lax.optimization_barrier((after, *arrs))
    if mode == "gather":
        return _sc_gather(name, arrs, kinds, _SEQ_IDS[name]), None
    return _sc_xfer(name, mode, arrs, kinds, _SEQ_IDS[name]), None


def _seq_wait(name, res, after):
    del name, after
    return list(res)


def _adamw(name, parts, w, m, v):
    P, R, C = parts.shape
    sub = 16 if parts.dtype == BF16 else 8
    tr = R if R * C <= (1 << 18) else _tile(R, max(sub, (1 << 18) // C), sub)

    def body(p_ref, w_ref, m_ref, v_ref, g_ref, d_ref, nm_ref, nv_ref):
        g = p_ref[0].astype(F32)
        for s in range(1, P):
            g = g + p_ref[s].astype(F32)
        m2 = ADAM_B1 * m_ref[...] + (1.0 - ADAM_B1) * g
        v2 = ADAM_B2 * v_ref[...] + (1.0 - ADAM_B2) * (g * g)
        m_hat = m2 / (1.0 - ADAM_B1 ** ADAM_STEP)
        v_hat = v2 / (1.0 - ADAM_B2 ** ADAM_STEP)
        g_ref[...] = g
        d_ref[...] = -ADAM_LR * (m_hat / (jnp.sqrt(v_hat) + ADAM_EPS) + ADAM_WD * w_ref[...])
        nm_ref[...] = m2
        nv_ref[...] = v2

    sp = pl.BlockSpec((tr, C), lambda i: (i, 0))
    return pl.pallas_call(
        body, name=name, grid=(R // tr,),
        in_specs=[pl.BlockSpec((P, tr, C), lambda i: (0, i, 0)), sp, sp, sp], out_specs=[sp] * 4,
        out_shape=[_sds((R, C), F32)] * 4, compiler_params=_params(1),
    )(parts, w, m, v)


def _pack(arrs, row_mult=8):
    pieces, total = [], 0
    for a in arrs:
        f = a.reshape(-1).astype(F32)
        pad = (-f.shape[0]) % (8 * LANES)
        pieces.append(jnp.pad(f, (0, pad)) if pad else f)
        total += f.shape[0] + pad
    tail = (-total) % (row_mult * LANES)
    if tail:
        pieces.append(jnp.zeros((tail,), F32))
    return jnp.concatenate(pieces).reshape(-1, LANES)


def _unpack(buf, shapes, lead=()):
    out, row = [], 0
    for shp in shapes:
        size = 1
        for d in shp:
            size *= d
        rows = -(-size // (8 * LANES)) * 8
        piece = buf[..., row:row + rows, :].reshape(lead + (rows * LANES,))[..., :size]
        out.append(piece.reshape(lead + tuple(shp)))
        row += rows
    return out


def kernel(x, norm_tok, w_in, a_re, a_im, log_dt, b_re, b_im, c_re, c_im, d_skip, w_glu, w_ssm_out, conv_w, conv_b, w_conv_out, w_o, norm_ffn, w_up, ffn_conv_w, ffn_conv_b, w_down, norm_final, loss_target, m_norm_tok, m_w_in, m_a_re, m_a_im, m_log_dt, m_b_re, m_b_im, m_c_re, m_c_im, m_d_skip, m_w_glu, m_w_ssm_out, m_conv_w, m_conv_b, m_w_conv_out, m_w_o, m_norm_ffn, m_w_up, m_ffn_conv_w, m_ffn_conv_b, m_w_down, m_norm_final, v_norm_tok, v_w_in, v_a_re, v_a_im, v_log_dt, v_b_re, v_b_im, v_c_re, v_c_im, v_d_skip, v_w_glu, v_w_ssm_out, v_conv_w, v_conv_b, v_w_conv_out, v_w_o, v_norm_ffn, v_w_up, v_ffn_conv_w, v_ffn_conv_b, v_w_down, v_norm_final):
    args = dict(locals())
    L, D = x.shape[1], x.shape[2]
    G, P, H = b_re.shape[1], b_re.shape[2], b_re.shape[3]
    SW = G * H
    CW = conv_b.shape[1]
    FF = ffn_conv_b.shape[1]
    GP = G * P
    nb = SW // LANES
    gpb = LANES // H
    me = 4 * lax.axis_index("x") + 2 * lax.axis_index("y") + lax.axis_index("c")
    tm = _tile(L, 256, 16)
    x2 = x[0]
    tgt = loss_target[0]

    big = [("w_in", "col"), ("w_glu", "row"), ("w_ssm_out", "col"), ("w_conv_out", "col"), ("w_o", "row"),
           ("w_up", "col"), ("w_down", "row")]
    shards = [_cast_bf16("cast_" + n, args[n][0]) for n, _ in big]
    small_in = _pack([conv_w[0], ffn_conv_w[0]])
    kind = dict(big)
    mixw, ffnw = ["w_glu", "w_ssm_out", "w_conv_out", "w_o"], ["w_up", "w_down"]
    shard = dict(zip([n for n, _ in big], shards))
    gathered, _ = _seq_start("gather_in", "gather", [shard["w_in"], small_in], ["col", "row"], None)
    W = {"w_in": gathered[0]}
    st_mix, tok_mix = _seq_start("gather_mix", "gather", [shard[n] for n in mixw], [kind[n] for n in mixw], None)
    st_ffn, tok_ffn = _seq_start("gather_ffn", "gather", [shard[n] for n in ffnw], [kind[n] for n in ffnw], None)
    cw_parts, fcw_parts = _unpack(gathered[-1].reshape(N_DEV, -1, LANES), [conv_w.shape[1:], ffn_conv_w.shape[1:]], (N_DEV,))
    conv_w_full = jnp.moveaxis(cw_parts, 0, 1).reshape(3, CW)
    ffn_conv_w_full = jnp.moveaxis(fcw_parts, 0, 1).reshape(3, FF)

    ar_row, ai_row = a_re.reshape(1, GP), a_im.reshape(1, GP)
    ldt_row = jnp.broadcast_to(log_dt.reshape(G, 1), (G, P)).reshape(1, GP)
    brt = jnp.transpose(b_re[0], (2, 0, 1)).reshape(H, GP)
    bit = jnp.transpose(b_im[0], (2, 0, 1)).reshape(H, GP)
    abar_re, abar_im, bbar_re, bbar_im = _prep_fwd("s5_prep", ar_row, ai_row, ldt_row, brt, bit)
    eye = jnp.eye(gpb, dtype=F32)

    def b_blocks(bt):
        return jnp.einsum("ab,hjbp->jahbp", eye, bt.reshape(H, nb, gpb, P)).reshape(nb, LANES, gpb * P)

    def c_blocks(c):
        return jnp.einsum("ab,jahp->jbpah", eye, c.reshape(nb, gpb, H, P)).reshape(nb, gpb * P, LANES)

    def diag_blocks(mat):
        return jnp.einsum("jahap->hjap", mat.reshape(nb, gpb, H, gpb, P))

    bm_re, bm_im = b_blocks(bbar_re), b_blocks(bbar_im)
    cm_re, cm_im = c_blocks(c_re[0]), -c_blocks(c_im[0])
    a3_re, a3_im = abar_re.reshape(-1, 8, LANES), abar_im.reshape(-1, 8, LANES)
    dskip_row = d_skip.reshape(1, SW)

    cbs = SW // LANES
    cb_v, cb_gb, cb_gc = cbs, cbs + CW // LANES, cbs + 2 * CW // LANES
    cb_ma = (SW + 3 * CW) // D
    xn = _rms_fwd("rms_tok", x2, norm_tok, tm)
    proj = _mm("proj", xn, W["w_in"], "nn", dep=tok_ffn)
    ts, tw = _tile(L, 512, 16), _tile(L, 1024, 16)
    bu_re, bu_im = _ssm_in("s5_bu", proj, bm_re.astype(BF16), bm_im.astype(BF16), ts)
    xs_re, xs_im = _scan("s5_scan", bu_re, bu_im, a3_re, a3_im)
    y, ya = _ssm_out("s5_y", xs_re, xs_im, cm_re.astype(BF16), cm_im.astype(BF16), proj, dskip_row, ts, post=jax.nn.gelu)
    W.update(zip(mixw, _seq_wait("gather_mix", st_mix, ya)))
    g1 = _mm("glu_gate", ya, W["w_glu"], "nn")
    ya2 = _glu_fwd("glu", y, g1, tm)
    za = _mm("ssm_out", ya2, W["w_ssm_out"], "nn")
    q = _convb_fwd("convb", proj, cb_v, cb_gb, cb_gc, conv_w_full, conv_b)
    zb = _mm("conv_out", q, W["w_conv_out"], "nn")
    merged = _merge_fwd("merge", proj, cb_ma, cb_ma + 1, za, zb, tm)
    o1 = _mm("mix_out", merged, W["w_o"], "nn")
    h1, hn = _res_rms_fwd("rms_ffn", x2, o1, norm_ffn, tm)
    W.update(zip(ffnw, _seq_wait("gather_ffn", st_ffn, hn)))
    hh = _mm("ffn_up", hn, W["w_up"], "nn")
    f = _ffn_fwd("ffn_act", hh, ffn_conv_w_full, ffn_conv_b)
    o2 = _mm("ffn_down", f, W["w_down"], "nn", tk=2816)
    dh2, dh2b, g_norm_final, loss_part = _final("final", h1, o2, norm_final.reshape(1, D), tgt, tm)

    df = _mm("d_ffn_act", dh2b, W["w_down"], "nt", tn=1408)
    gw_down = _mm("gw_down", f, dh2b, "tn", out_dtype=BF16, tm=1408, tn=512, tk=L)
    dhh, g_ffn_conv_w, g_ffn_conv_b = _ffn_bwd("ffn_act_bwd", hh, ffn_conv_w_full, ffn_conv_b, df)
    nhalf = lambda t: FF // t
    dhn = _mm("d_ffn_in", dhh, W["w_up"], "nt", tk=_tile(FF, 2816), dims=(L, D, 2 * FF),
              a_spec=lambda a, b, c: pl.BlockSpec((None, a, c), lambda i, j, k: (k // nhalf(c), i, k % nhalf(c))))
    gw_up = _mm("gw_up", hn, dhh, "tn", out_dtype=BF16, tn=_tile(FF, 1024), tk=L, dims=(D, 2 * FF, L),
                b_spec=lambda a, b, c: pl.BlockSpec((None, c, b), lambda i, j, k: (j // nhalf(b), k, j % nhalf(b))))
    dhn, gw_up, gw_down = lax.optimization_barrier((dhn, gw_up, gw_down))
    st_gffn, tok_gffn = _seq_start("grads_ffn", "exchange", [gw_up, gw_down], [kind[n] for n in ffnw], None)
    dh1, dh1b, g_norm_ffn = _rms_bwd("rms_ffn_bwd", dhn, h1, norm_ffn, dh2, tm, True)

    dmerged = _mm("d_merged", dh1b, W["w_o"], "nt", dep=tok_gffn)
    gw_o = _mm("gw_o", merged, dh1b, "tn", out_dtype=BF16, tk=L)
    dmerged, gw_o = lax.optimization_barrier((dmerged, gw_o))
    dza, dzb, dma, dmb = _merge_bwd("merge_bwd", proj, cb_ma, cb_ma + 1, za, zb, dmerged, tm)
    dq = _mm("d_q", dzb, W["w_conv_out"], "nt")
    gw_conv_out = _mm("gw_conv_out", q, dzb, "tn", out_dtype=BF16, tk=L)
    dq, gw_conv_out = lax.optimization_barrier((dq, gw_conv_out))
    dv, dgb, dgc, g_conv_w, g_conv_b = _convb_bwd("convb_bwd", proj, cb_v, cb_gb, cb_gc, conv_w_full, conv_b, dq)
    dya2 = _mm("d_ya2", dza, W["w_ssm_out"], "nt")
    gw_ssm_out = _mm("gw_ssm_out", ya2, dza, "tn", out_dtype=BF16, tk=L)
    dya2, gw_ssm_out = lax.optimization_barrier((dya2, gw_ssm_out))
    dy_direct, dg1 = _glu_bwd("glu_bwd", y, g1, dya2, tm)
    dya_g = _mm("d_ya_gate", dg1, W["w_glu"], "nt")
    gw_glu = _mm("gw_glu", ya, dg1, "tn", out_dtype=BF16, tk=L)
    dya_g, gw_glu = lax.optimization_barrier((dya_g, gw_glu))
    st_gmix, tok_gmix = _seq_start("grads_mix", "exchange", [gw_glu, gw_ssm_out, gw_conv_out, gw_o],
                                   [kind[n] for n in mixw], None)
    dyb, g_dskip = _gelu_bwd("gelu_bwd", y, dy_direct, dya_g, proj, dskip_row, tm)
    dxs_re, dxs_im = _ssm_in("s5_dx", dyb, jnp.swapaxes(cm_re, 1, 2).astype(BF16), jnp.swapaxes(cm_im, 1, 2).astype(BF16), ts,
                             dep=tok_gmix)
    gc_re, gc_im = _ssm_dw("s5_gc", dyb, xs_re, xs_im, tw)
    lam_re, lam_im, dab_re, dab_im = _scan("s5_scan_bwd", dxs_re, dxs_im, a3_re, a3_im, xs=(xs_re, xs_im))
    parts = dict(zip(ffnw, _seq_wait("grads_ffn", st_gffn, dab_re)))
    du = _ssm_out("s5_du", lam_re, lam_im, jnp.swapaxes(bm_re, 1, 2).astype(BF16), jnp.swapaxes(bm_im, 1, 2).astype(BF16),
                  dyb, dskip_row, ts, post=lambda t: t)[1]
    gb_re, gb_im = _ssm_dw("s5_gb", proj, lam_re, lam_im, tw)
    g_ar, g_ai, g_ldt, g_brt, g_bit = _prep_bwd(
        "s5_prep_bwd", ar_row, ai_row, ldt_row, brt, bit, dab_re.reshape(1, GP), dab_im.reshape(1, GP),
        diag_blocks(gb_re).reshape(H, GP), diag_blocks(gb_im).reshape(H, GP))
    small = dict(
        a_re=g_ar.reshape(1, G, P), a_im=g_ai.reshape(1, G, P),
        log_dt=g_ldt.reshape(G, P).sum(axis=1).reshape(1, G),
        b_re=jnp.transpose(g_brt.reshape(H, G, P), (1, 2, 0))[None], b_im=jnp.transpose(g_bit.reshape(H, G, P), (1, 2, 0))[None],
        c_re=jnp.transpose(diag_blocks(gc_re), (1, 2, 0, 3)).reshape(1, G, H, P),
        c_im=-jnp.transpose(diag_blocks(gc_im), (1, 2, 0, 3)).reshape(1, G, H, P),
        d_skip=g_dskip.reshape(1, G, H), conv_b=g_conv_b, norm_ffn=g_norm_ffn, ffn_conv_b=g_ffn_conv_b,
        norm_final=g_norm_final.reshape(D), conv_w=g_conv_w[None], ffn_conv_w=g_ffn_conv_w[None])
    rep = ["a_re", "a_im", "log_dt", "b_re", "b_im", "c_re", "c_im", "d_skip", "conv_b", "norm_ffn", "ffn_conv_b", "norm_final"]
    order = rep + ["conv_w", "ffn_conv_w"]
    full_shapes = {n: args[n].shape for n in rep}
    full_shapes["conv_w"], full_shapes["ffn_conv_w"] = (1, 3, CW), (1, 3, FF)
    rep_pack = _pack([small[n] for n in rep], LANES)
    rep_rows = rep_pack.shape[0]
    gpack = jnp.concatenate([rep_pack, _pack([small["conv_w"], small["ffn_conv_w"]])], axis=0)
    rows = gpack.shape[0]
    du, gpack = lax.optimization_barrier((du, gpack))
    st_small, tok_small = _seq_start("grads_small", "gather", [gpack], ["row"], None)

    dproj = jnp.concatenate([du, dv, dgb, dgc, dma, dmb], axis=1)
    gw_in = _mm("gw_in", xn, dproj, "tn", out_dtype=BF16, tk=L, dep=tok_small)
    dproj, gw_in = lax.optimization_barrier((dproj, gw_in))
    st_gin, tok_gin = _seq_start("grads_in", "exchange", [gw_in], ["col"], None)
    dxn = _mm("d_xn", dproj, W["w_in"], "nt", dep=tok_gin)
    grad_x, g_norm_tok = _rms_bwd("rms_tok_bwd", dxn, x2, norm_tok, dh1, tm, False)

    res = {}

    def big_update(n):
        res[n] = [r[None] for r in _adamw("adamw_" + n, parts[n], args[n][0], args["m_" + n][0], args["v_" + n][0])]

    def after(xs, dep):
        return lax.optimization_barrier((list(xs), dep))[0]

    parts = dict(zip(ffnw, after([parts[n] for n in ffnw], grad_x)))
    for n in ffnw:
        big_update(n)
    parts.update(zip(mixw, after(_seq_wait("grads_mix", st_gmix, grad_x), [res[n][1] for n in ffnw])))
    for n in mixw:
        big_update(n)
    gall = after(_seq_wait("grads_small", st_small, None), [res[n][1] for n in mixw])[0].reshape(N_DEV, rows, LANES)
    gcw, gfcw = _unpack(gall[:, rep_rows:], [full_shapes["conv_w"], full_shapes["ffn_conv_w"]], (N_DEV,))
    cws, fcws = CW // N_DEV, FF // N_DEV
    gcw = lax.dynamic_slice_in_dim(gcw[:, 0], me * cws, cws, axis=2)
    gfcw = lax.dynamic_slice_in_dim(gfcw[:, 0], me * fcws, fcws, axis=2)
    res["conv_w"] = [r[None] for r in _adamw("adamw_conv_w", gcw, conv_w[0], m_conv_w[0], v_conv_w[0])]
    res["ffn_conv_w"] = [r[None] for r in _adamw("adamw_ffn_conv_w", gfcw, ffn_conv_w[0], m_ffn_conv_w[0], v_ffn_conv_w[0])]
    rep_out = _adamw("adamw_small", gall[:, :rep_rows], _pack([args[n] for n in rep], LANES),
                     _pack([args["m_" + n] for n in rep], LANES), _pack([args["v_" + n] for n in rep], LANES))
    rep_out = [_unpack(r, [full_shapes[n] for n in rep]) for r in rep_out]
    for i, n in enumerate(rep):
        res[n] = [r[i] for r in rep_out]
    nt_pack = after([_pack([g_norm_tok])], [res[n][1] for n in ("a_re", "conv_w", "ffn_conv_w")])
    nt_all = _all_gather("gather_norm_tok_grad", nt_pack, ["row"])[0].reshape(N_DEV, -1, LANES)
    nt_out = _adamw("adamw_norm_tok", nt_all, _pack([norm_tok]), _pack([m_norm_tok]), _pack([v_norm_tok]))
    res["norm_tok"] = [_unpack(r, [norm_tok.shape])[0] for r in nt_out]
    parts["w_in"] = after(_seq_wait("grads_in", st_gin, None), nt_out[0])[0]
    big_update("w_in")

    loss = lax.psum(loss_part[0, 0], ("x", "y", "c"))
    names = ["norm_tok", "w_in", "a_re", "a_im", "log_dt", "b_re", "b_im", "c_re", "c_im", "d_skip", "w_glu", "w_ssm_out",
             "conv_w", "conv_b", "w_conv_out", "w_o", "norm_ffn", "w_up", "ffn_conv_w", "ffn_conv_b", "w_down", "norm_final"]
    out = [loss, grad_x[None]]
    for slot in range(4):
        out += [res[n][slot] for n in names]
    return tuple(out)
```

```python
import functools

import jax
import jax.numpy as jnp
from jax import lax
from jax.experimental import pallas as pl
from jax.experimental.pallas import tpu as pltpu
from jax.experimental.pallas import tpu_sc as plsc

F32 = jnp.float32
BF16 = jnp.bfloat16
N_DEV = 8
LANES = 128
SLAB = 4
EPS = 1e-6
ADAM_LR = 0.001
ADAM_B1 = 0.9
ADAM_B2 = 0.999
ADAM_EPS = 1e-08
ADAM_WD = 0.01
ADAM_STEP = 10
VMEM_LIMIT = 56 * 1024 * 1024
MESH = pl.DeviceIdType.MESH


def _tile(n, pref, mult=LANES):
    best = None
    t = mult
    while t <= min(n, pref):
        if n % t == 0:
            best = t
        t += mult
    return best if best is not None else n


def _params(ndim):
    return pltpu.CompilerParams(dimension_semantics=("arbitrary",) * ndim, vmem_limit_bytes=VMEM_LIMIT)


def _sds(shape, dtype):
    return jax.ShapeDtypeStruct(tuple(shape), dtype)


def _mm(name, a, b, mode, *, out_dtype=F32, tm=1024, tn=1024, tk=2048, dims=None, a_spec=None, b_spec=None, dep=None):
    if dims is None:
        if mode == "nn":
            (M, K), N = a.shape, b.shape[1]
        elif mode == "nt":
            (M, K), N = a.shape, b.shape[0]
        else:
            (K, M), N = a.shape, b.shape[1]
    else:
        M, N, K = dims
    tm, tn, tk = _tile(M, tm), _tile(N, tn), _tile(K, tk)
    nk = K // tk
    if mode == "nn":
        dn = (((1,), (0,)), ((), ()))
        sa = pl.BlockSpec((tm, tk), lambda i, j, k: (i, k))
        sb = pl.BlockSpec((tk, tn), lambda i, j, k: (k, j))
    elif mode == "nt":
        dn = (((1,), (1,)), ((), ()))
        sa = pl.BlockSpec((tm, tk), lambda i, j, k: (i, k))
        sb = pl.BlockSpec((tn, tk), lambda i, j, k: (j, k))
    else:
        dn = (((0,), (0,)), ((), ()))
        sa = pl.BlockSpec((tk, tm), lambda i, j, k: (k, i))
        sb = pl.BlockSpec((tk, tn), lambda i, j, k: (k, j))
    sa = a_spec(tm, tn, tk) if a_spec is not None else sa
    sb = b_spec(tm, tn, tk) if b_spec is not None else sb
    use_acc = nk > 1 and out_dtype != F32

    deps = [] if dep is None else [dep]

    def body(a_ref, b_ref, *rest):
        o_ref, acc = rest[len(deps)], rest[len(deps) + 1:]
        k = pl.program_id(2)
        p = lax.dot_general(a_ref[...], b_ref[...], dn, preferred_element_type=F32)
        if nk == 1:
            o_ref[...] = p.astype(out_dtype)
        else:
            tgt = acc[0] if use_acc else o_ref

            @pl.when(k == 0)
            def _():
                tgt[...] = p

            @pl.when(k > 0)
            def _():
                tgt[...] += p

            if use_acc:
                @pl.when(k == nk - 1)
                def _():
                    o_ref[...] = acc[0][...].astype(out_dtype)

    return pl.pallas_call(
        body, name=name, grid=(M // tm, N // tn, nk),
        in_specs=[sa, sb] + [pl.BlockSpec(memory_space=pl.ANY)] * len(deps),
        out_specs=pl.BlockSpec((tm, tn), lambda i, j, k: (i, j)),
        out_shape=_sds((M, N), out_dtype),
        scratch_shapes=[pltpu.VMEM((tm, tn), F32)] if use_acc else [],
        compiler_params=_params(3),
    )(a, b, *deps)


def _rows(name, body, L, tm, ins, outs):
    return pl.pallas_call(
        body, name=name, grid=(L // tm,),
        in_specs=[s for _, s in ins], out_specs=[s for _, s in outs],
        out_shape=[o for o, _ in outs], compiler_params=_params(1),
    )(*[a for a, _ in ins])


def _rs(tm, w, cb=0):
    return pl.BlockSpec((tm, w), lambda i: (i, cb))


def _fs(shape):
    return pl.BlockSpec(tuple(shape), lambda i: (0,) * len(shape))


def _acc_rows(i, ref, part):
    @pl.when(i == 0)
    def _():
        ref[...] = part

    @pl.when(i > 0)
    def _():
        ref[...] += part


def _cast_bf16(name, w):
    R, C = w.shape
    tr = _tile(R, max(16, (1 << 20) // C), 16)

    def body(w_ref, o_ref):
        o_ref[...] = w_ref[...].astype(BF16)

    return _rows(name, body, R, tr, [(w, _rs(tr, C))], [(_sds((R, C), BF16), _rs(tr, C))])[0]


def _rms_fwd(name, x, g, tm):
    L, D = x.shape

    def body(x_ref, g_ref, o_ref):
        xv = x_ref[...]
        r = lax.rsqrt(jnp.mean(xv * xv, axis=-1, keepdims=True) + EPS)
        o_ref[...] = (xv * r * g_ref[...]).astype(BF16)

    return _rows(name, body, L, tm, [(x, _rs(tm, D)), (g, _fs((1, D)))], [(_sds((L, D), BF16), _rs(tm, D))])[0]


def _res_rms_fwd(name, x, o, g, tm):
    L, D = x.shape

    def body(x_ref, o_ref, g_ref, h_ref, hn_ref):
        h = x_ref[...] + o_ref[...]
        r = lax.rsqrt(jnp.mean(h * h, axis=-1, keepdims=True) + EPS)
        h_ref[...] = h
        hn_ref[...] = (h * r * g_ref[...]).astype(BF16)

    return _rows(name, body, L, tm, [(x, _rs(tm, D)), (o, _rs(tm, D)), (g, _fs((1, D)))],
                 [(_sds((L, D), F32), _rs(tm, D)), (_sds((L, D), BF16), _rs(tm, D))])


def _rms_bwd(name, dn, h, g, dres, tm, with_bf16):
    L, D = h.shape

    def body(dn_ref, h_ref, g_ref, dres_ref, dh_ref, *rest):
        i = pl.program_id(0)
        h = h_ref[...]
        r = lax.rsqrt(jnp.mean(h * h, axis=-1, keepdims=True) + EPS)
        xh = h * r
        d = dn_ref[...]
        dxh = d * g_ref[...]
        dh = dres_ref[...] + r * (dxh - xh * jnp.mean(dxh * xh, axis=-1, keepdims=True))
        dh_ref[...] = dh
        if with_bf16:
            rest[0][...] = dh.astype(BF16)
        _acc_rows(i, rest[-1], jnp.sum(d * xh, axis=0, keepdims=True))

    outs = [(_sds((L, D), F32), _rs(tm, D))]
    if with_bf16:
        outs.append((_sds((L, D), BF16), _rs(tm, D)))
    outs.append((_sds((1, D), F32), _fs((1, D))))
    return _rows(name, body, L, tm, [(dn, _rs(tm, D)), (h, _rs(tm, D)), (g, _fs((1, D))), (dres, _rs(tm, D))], outs)


def _final(name, h1, o2, g, tgt, tm):
    L, D = h1.shape

    def body(h1_ref, o2_ref, g_ref, t_ref, dh_ref, dhb_ref, dg_ref, loss_ref):
        i = pl.program_id(0)
        h = h1_ref[...] + o2_ref[...]
        r = lax.rsqrt(jnp.mean(h * h, axis=-1, keepdims=True) + EPS)
        xh = h * r
        gv = g_ref[...]
        e = xh * gv - t_ref[...]
        part = 0.5 * jnp.sum(jnp.mean(e * e, axis=-1, keepdims=True), axis=0, keepdims=True)
        dy = e / D
        dxh = dy * gv
        dh = r * (dxh - xh * jnp.mean(dxh * xh, axis=-1, keepdims=True))
        dh_ref[...] = dh
        dhb_ref[...] = dh.astype(BF16)
        _acc_rows(i, dg_ref, jnp.sum(dy * xh, axis=0, keepdims=True))
        _acc_rows(i, loss_ref, jnp.broadcast_to(part, (8, LANES)))

    return _rows(name, body, L, tm,
                 [(h1, _rs(tm, D)), (o2, _rs(tm, D)), (g, _fs((1, D))), (tgt, _rs(tm, D))],
                 [(_sds((L, D), F32), _rs(tm, D)), (_sds((L, D), BF16), _rs(tm, D)),
                  (_sds((1, D), F32), _fs((1, D))), (_sds((8, LANES), F32), _fs((8, LANES)))])


def _glu_fn(y, g1):
    ya = jax.nn.gelu(y)
    return ya * jax.nn.sigmoid(g1)


def _glu_fwd(name, y, g1, tm):
    L, W = y.shape

    def body(y_ref, g_ref, o_ref):
        o_ref[...] = _glu_fn(y_ref[...], g_ref[...]).astype(BF16)

    return _rows(name, body, L, tm, [(y, _rs(tm, W)), (g1, _rs(tm, W))], [(_sds((L, W), BF16), _rs(tm, W))])[0]


def _glu_bwd(name, y, g1, dya2, tm):
    L, W = y.shape

    def body(y_ref, g_ref, d_ref, dy_ref, dg_ref):
        _, vjp = jax.vjp(_glu_fn, y_ref[...], g_ref[...])
        dy, dg = vjp(d_ref[...])
        dy_ref[...] = dy
        dg_ref[...] = dg.astype(BF16)

    return _rows(name, body, L, tm, [(y, _rs(tm, W)), (g1, _rs(tm, W)), (dya2, _rs(tm, W))],
                 [(_sds((L, W), F32), _rs(tm, W)), (_sds((L, W), BF16), _rs(tm, W))])


def _gelu_bwd(name, y, dy_direct, dya_g, proj, dskip, tm):
    L, W = y.shape

    def body(y_ref, dd_ref, dg_ref, u_ref, dyb_ref, dsk_ref):
        i = pl.program_id(0)
        _, vjp = jax.vjp(jax.nn.gelu, y_ref[...])
        dy = dd_ref[...] + vjp(dg_ref[...])[0]
        dyb_ref[...] = dy.astype(BF16)
        _acc_rows(i, dsk_ref, jnp.sum(dy * u_ref[...].astype(F32), axis=0, keepdims=True))

    del dskip
    return _rows(name, body, L, tm,
                 [(y, _rs(tm, W)), (dy_direct, _rs(tm, W)), (dya_g, _rs(tm, W)), (proj, _rs(tm, W, 0))],
                 [(_sds((L, W), BF16), _rs(tm, W)), (_sds((1, W), F32), _fs((1, W)))])


def _merge_fn(ma, mb, za, zb):
    return jax.nn.sigmoid(ma) * za + jax.nn.sigmoid(mb) * zb


def _merge_fwd(name, proj, cb_a, cb_b, za, zb, tm):
    L, D = za.shape

    def body(ma_ref, mb_ref, za_ref, zb_ref, o_ref):
        o_ref[...] = _merge_fn(ma_ref[...].astype(F32), mb_ref[...].astype(F32), za_ref[...], zb_ref[...]).astype(BF16)

    return _rows(name, body, L, tm,
                 [(proj, _rs(tm, D, cb_a)), (proj, _rs(tm, D, cb_b)), (za, _rs(tm, D)), (zb, _rs(tm, D))],
                 [(_sds((L, D), BF16), _rs(tm, D))])[0]


def _merge_bwd(name, proj, cb_a, cb_b, za, zb, dmerged, tm):
    L, D = za.shape

    def body(ma_ref, mb_ref, za_ref, zb_ref, d_ref, dza_ref, dzb_ref, dma_ref, dmb_ref):
        _, vjp = jax.vjp(_merge_fn, ma_ref[...].astype(F32), mb_ref[...].astype(F32), za_ref[...], zb_ref[...])
        dma, dmb, dza, dzb = vjp(d_ref[...])
        dza_ref[...] = dza.astype(BF16)
        dzb_ref[...] = dzb.astype(BF16)
        dma_ref[...] = dma.astype(BF16)
        dmb_ref[...] = dmb.astype(BF16)

    return _rows(name, body, L, tm,
                 [(proj, _rs(tm, D, cb_a)), (proj, _rs(tm, D, cb_b)), (za, _rs(tm, D)), (zb, _rs(tm, D)),
                  (dmerged, _rs(tm, D))],
                 [(_sds((L, D), BF16), _rs(tm, D)), (_sds((L, D), BF16), _rs(tm, D)),
                  (_sds((L, D), BF16), _rs(tm, D)), (_sds((L, D), BF16), _rs(tm, D))])


def _shift_down(x, k):
    row = lax.broadcasted_iota(jnp.int32, x.shape, 0)
    return jnp.where(row >= k, pltpu.roll(x, k, axis=0), 0.0)


def _shift_up(x, k):
    n = x.shape[0]
    row = lax.broadcasted_iota(jnp.int32, x.shape, 0)
    return jnp.where(row < n - k, pltpu.roll(x, n - k, axis=0), 0.0)


def _conv3(cv, w_ref, b_ref):
    return (w_ref[2:3, :] * cv + w_ref[1:2, :] * _shift_down(cv, 1) + w_ref[0:1, :] * _shift_down(cv, 2)
            + b_ref[...])


def _conv3_bwd(dcc, cv, w_ref):
    dcv = w_ref[2:3, :] * dcc + w_ref[1:2, :] * _shift_up(dcc, 1) + w_ref[0:1, :] * _shift_up(dcc, 2)
    dw = [jnp.sum(dcc * _shift_down(cv, 2), axis=0, keepdims=True),
          jnp.sum(dcc * _shift_down(cv, 1), axis=0, keepdims=True),
          jnp.sum(dcc * cv, axis=0, keepdims=True)]
    db = jnp.sum(dcc, axis=0, keepdims=True)
    return dcv, dw, db


def _store_rows(ref, rows):
    for r, val in enumerate(rows):
        ref[r:r + 1, :] = val


def _cols(name, body, ncb, ins, outs):
    return pl.pallas_call(
        body, name=name, grid=(ncb,),
        in_specs=[s for _, s in ins], out_specs=[s for _, s in outs],
        out_shape=[o for o, _ in outs], compiler_params=_params(1),
    )(*[a for a, _ in ins])


def _cb(L, w, off=0):
    return pl.BlockSpec((L, w), lambda j: (0, j + off))


def _convb_fwd(name, proj, cb_v, cb_gb, cb_gc, w, b):
    L = proj.shape[0]
    W = w.shape[1]
    c = LANES

    def body(v_ref, gb_ref, gc_ref, w_ref, b_ref, q_ref):
        cc = _conv3(gc_ref[...].astype(F32) * v_ref[...].astype(F32), w_ref, b_ref)
        q_ref[...] = (gb_ref[...].astype(F32) * cc).astype(BF16)

    return _cols(name, body, W // c,
                 [(proj, _cb(L, c, cb_v)), (proj, _cb(L, c, cb_gb)), (proj, _cb(L, c, cb_gc)),
                  (w, _cb(3, c)), (b, _cb(1, c))],
                 [(_sds((L, W), BF16), _cb(L, c))])[0]


def _convb_bwd(name, proj, cb_v, cb_gb, cb_gc, w, b, dq):
    L = proj.shape[0]
    W = w.shape[1]
    c = LANES

    def body(v_ref, gb_ref, gc_ref, w_ref, b_ref, dq_ref, dv_ref, dgb_ref, dgc_ref, dw_ref, db_ref):
        v, gc = v_ref[...].astype(F32), gc_ref[...].astype(F32)
        cv = gc * v
        cc = _conv3(cv, w_ref, b_ref)
        dq = dq_ref[...]
        dgb_ref[...] = (dq * cc).astype(BF16)
        dcv, dw, db = _conv3_bwd(dq * gb_ref[...].astype(F32), cv, w_ref)
        dv_ref[...] = (dcv * gc).astype(BF16)
        dgc_ref[...] = (dcv * v).astype(BF16)
        _store_rows(dw_ref, dw)
        db_ref[...] = db

    return _cols(name, body, W // c,
                 [(proj, _cb(L, c, cb_v)), (proj, _cb(L, c, cb_gb)), (proj, _cb(L, c, cb_gc)),
                  (w, _cb(3, c)), (b, _cb(1, c)), (dq, _cb(L, c))],
                 [(_sds((L, W), BF16), _cb(L, c)), (_sds((L, W), BF16), _cb(L, c)), (_sds((L, W), BF16), _cb(L, c)),
                  (_sds((3, W), F32), _cb(3, c)), (_sds((1, W), F32), _cb(1, c))])


HALO = 16


def _ffn_tiles(L, Fw):
    tr = _tile(L, 256, HALO)
    tc = _tile(Fw, 1408)
    return tr, tc, Fw // tc, L // tr, tr // HALO


def _ffn_fwd(name, hh, w, b):
    L = hh.shape[0]
    Fw = w.shape[1]
    tr, tc, ncb, nrt, rpt = _ffn_tiles(L, Fw)

    def body(a_ref, p_ref, h2_ref, w_ref, b_ref, f_ref):
        first = pl.program_id(1) == 0
        for c0 in range(0, tc, LANES):
            cs = slice(c0, c0 + LANES)
            prev = jnp.where(first, 0.0, p_ref[:, cs].astype(F32))
            x = jnp.concatenate([prev, a_ref[:, cs].astype(F32)], axis=0)
            n = x.shape[0]
            a = (w_ref[2:3, cs] * x + w_ref[1:2, cs] * pltpu.roll(x, 1, axis=0) + w_ref[0:1, cs] * pltpu.roll(x, 2, axis=0)
                 + b_ref[:, cs])[HALO:n]
            f_ref[:, cs] = (jax.nn.gelu(a) * h2_ref[:, cs].astype(F32)).astype(BF16)

    main = pl.BlockSpec((tr, tc), lambda j, i: (i, j))
    return pl.pallas_call(
        body, name=name, grid=(ncb, nrt),
        in_specs=[main, pl.BlockSpec((HALO, tc), lambda j, i: (jnp.maximum(i * rpt - 1, 0), j)),
                  pl.BlockSpec((tr, tc), lambda j, i: (i, j + ncb)),
                  pl.BlockSpec((3, tc), lambda j, i: (0, j)), pl.BlockSpec((1, tc), lambda j, i: (0, j))],
        out_specs=main, out_shape=_sds((L, Fw), BF16), compiler_params=_params(2),
    )(hh, hh, hh, w, b)


def _ffn_bwd(name, hh, w, b, df):
    L = hh.shape[0]
    Fw = w.shape[1]
    tr, tc, ncb, nrt, rpt = _ffn_tiles(L, Fw)

    def body(a_ref, ap_ref, an_ref, h2_ref, h2n_ref, df_ref, dfn_ref, w_ref, b_ref, dhh_ref, dw_ref, db_ref):
        i = pl.program_id(1)
        first, last = i == 0, i == nrt - 1
        for c0 in range(0, tc, LANES):
            cs = slice(c0, c0 + LANES)
            zero = jnp.zeros((HALO, LANES), F32)
            h1 = jnp.concatenate([jnp.where(first, 0.0, ap_ref[:, cs].astype(F32)), a_ref[:, cs].astype(F32),
                                  an_ref[:, cs].astype(F32)], axis=0)
            h2 = jnp.concatenate([zero, h2_ref[:, cs].astype(F32), h2n_ref[:, cs].astype(F32)], axis=0)
            d = jnp.concatenate([zero, df_ref[:, cs].astype(F32), jnp.where(last, 0.0, dfn_ref[:, cs].astype(F32))], axis=0)
            n = h1.shape[0]
            s1, s2 = pltpu.roll(h1, 1, axis=0), pltpu.roll(h1, 2, axis=0)
            a = w_ref[2:3, cs] * h1 + w_ref[1:2, cs] * s1 + w_ref[0:1, cs] * s2 + b_ref[:, cs]
            ga, vjp = jax.vjp(jax.nn.gelu, a)
            da = vjp(d * h2)[0]
            dh1 = w_ref[2:3, cs] * da + w_ref[1:2, cs] * pltpu.roll(da, n - 1, axis=0) + w_ref[0:1, cs] * pltpu.roll(da, n - 2, axis=0)
            dhh_ref[0, :, cs] = dh1[HALO:HALO + tr].astype(BF16)
            dhh_ref[1, :, cs] = (d * ga)[HALO:HALO + tr].astype(BF16)
            dam = da[HALO:HALO + tr]
            rows = [jnp.sum(dam * s2[HALO:HALO + tr], axis=0, keepdims=True),
                    jnp.sum(dam * s1[HALO:HALO + tr], axis=0, keepdims=True),
                    jnp.sum(dam * h1[HALO:HALO + tr], axis=0, keepdims=True),
                    jnp.sum(dam, axis=0, keepdims=True)]

            @pl.when(first)
            def _():
                for r in range(3):
                    dw_ref[r:r + 1, cs] = rows[r]
                db_ref[:, cs] = rows[3]

            @pl.when(i > 0)
            def _():
                for r in range(3):
                    dw_ref[r:r + 1, cs] += rows[r]
                db_ref[:, cs] += rows[3]

    def spec(col_off, kind):
        if kind == "main":
            return pl.BlockSpec((tr, tc), lambda j, i: (i, j + col_off))
        if kind == "prev":
            return pl.BlockSpec((HALO, tc), lambda j, i: (jnp.maximum(i * rpt - 1, 0), j + col_off))
        return pl.BlockSpec((HALO, tc), lambda j, i: (jnp.minimum((i + 1) * rpt, nrt * rpt - 1), j + col_off))

    return pl.pallas_call(
        body, name=name, grid=(ncb, nrt),
        in_specs=[spec(0, "main"), spec(0, "prev"), spec(0, "next"), spec(ncb, "main"), spec(ncb, "next"),
                  spec(0, "main"), spec(0, "next"),
                  pl.BlockSpec((3, tc), lambda j, i: (0, j)), pl.BlockSpec((1, tc), lambda j, i: (0, j))],
        out_specs=[pl.BlockSpec((2, tr, tc), lambda j, i: (0, i, j)),
                   pl.BlockSpec((3, tc), lambda j, i: (0, j)), pl.BlockSpec((1, tc), lambda j, i: (0, j))],
        out_shape=[_sds((2, L, Fw), BF16), _sds((3, Fw), F32), _sds((1, Fw), F32)], compiler_params=_params(2),
    )(hh, hh, hh, hh, hh, df, df, w, b)


def _prep_fn(ar, ai, ldt, brt, bit):
    dt = jnp.exp(ldt)
    mag = jnp.exp(dt * ar)
    are = mag * jnp.cos(dt * ai)
    aim = mag * jnp.sin(dt * ai)
    nr = are - 1.0
    ni = aim
    den = ar * ar + ai * ai
    fr = (nr * ar + ni * ai) / den
    fi = (ni * ar - nr * ai) / den
    return are, aim, fr * brt - fi * bit, fr * bit + fi * brt


def _prep_fwd(name, ar, ai, ldt, brt, bit):
    def body(ar_ref, ai_ref, l_ref, br_ref, bi_ref, o1, o2, o3, o4):
        o1[...], o2[...], o3[...], o4[...] = _prep_fn(ar_ref[...], ai_ref[...], l_ref[...], br_ref[...], bi_ref[...])

    return pl.pallas_call(body, name=name,
                          out_shape=[_sds(ar.shape, F32), _sds(ar.shape, F32), _sds(brt.shape, F32), _sds(brt.shape, F32)],
                          )(ar, ai, ldt, brt, bit)


def _prep_bwd(name, ar, ai, ldt, brt, bit, g1, g2, g3, g4):
    def body(ar_ref, ai_ref, l_ref, br_ref, bi_ref, g1_ref, g2_ref, g3_ref, g4_ref, o1, o2, o3, o4, o5):
        _, vjp = jax.vjp(_prep_fn, ar_ref[...], ai_ref[...], l_ref[...], br_ref[...], bi_ref[...])
        o1[...], o2[...], o3[...], o4[...], o5[...] = vjp((g1_ref[...], g2_ref[...], g3_ref[...], g4_ref[...]))

    return pl.pallas_call(body, name=name,
                          out_shape=[_sds(ar.shape, F32)] * 3 + [_sds(brt.shape, F32)] * 2,
                          )(ar, ai, ldt, brt, bit, g1, g2, g3, g4)


def _ssm_in(name, src, m1, m2, tm, dep=None):
    L = src.shape[0]
    nb = m1.shape[0]

    deps = [] if dep is None else [dep]

    def body(s_ref, m1_ref, m2_ref, *rest):
        o1_ref, o2_ref = rest[len(deps):]
        u = s_ref[...].astype(BF16)
        r1 = jnp.dot(u, m1_ref[...], preferred_element_type=F32)
        r2 = jnp.dot(u, m2_ref[...], preferred_element_type=F32)
        for q in range(SLAB):
            o1_ref[q] = r1[:, q * LANES:(q + 1) * LANES].astype(BF16)
            o2_ref[q] = r2[:, q * LANES:(q + 1) * LANES].astype(BF16)

    ms = pl.BlockSpec((None, LANES, SLAB * LANES), lambda i, j: (j, 0, 0))
    os_ = pl.BlockSpec((SLAB, tm, LANES), lambda i, j: (j, i, 0))
    return pl.pallas_call(
        body, name=name, grid=(L // tm, nb),
        in_specs=[pl.BlockSpec((tm, LANES), lambda i, j: (i, j)), ms, ms] + [pl.BlockSpec(memory_space=pl.ANY)] * len(deps),
        out_specs=[os_, os_],
        out_shape=[_sds((SLAB * nb, L, LANES), BF16)] * 2, compiler_params=_params(2),
    )(src, m1, m2, *deps)


def _ssm_out(name, x1, x2, m1, m2, aux, dvec, tm, post=None):
    L = x1.shape[1]
    nb = m1.shape[0]

    def body(x1_ref, x2_ref, m1_ref, m2_ref, a_ref, d_ref, o_ref, *rest):
        a1 = jnp.concatenate([x1_ref[q] for q in range(SLAB)], axis=1).astype(BF16)
        a2 = jnp.concatenate([x2_ref[q] for q in range(SLAB)], axis=1).astype(BF16)
        y = (jnp.dot(a1, m1_ref[...], preferred_element_type=F32) + jnp.dot(a2, m2_ref[...], preferred_element_type=F32)
             + d_ref[...] * a_ref[...].astype(F32))
        o_ref[...] = y
        if post is not None:
            rest[0][...] = post(y).astype(BF16)

    xs = pl.BlockSpec((SLAB, tm, LANES), lambda i, j: (j, i, 0))
    ms = pl.BlockSpec((None, SLAB * LANES, LANES), lambda i, j: (j, 0, 0))
    cs = pl.BlockSpec((tm, LANES), lambda i, j: (i, j))
    W = nb * LANES
    outs, ospecs = [_sds((L, W), F32)], [cs]
    if post is not None:
        outs.append(_sds((L, W), BF16))
        ospecs.append(cs)
    return pl.pallas_call(
        body, name=name, grid=(L // tm, nb),
        in_specs=[xs, xs, ms, ms, cs, pl.BlockSpec((1, LANES), lambda i, j: (0, j))], out_specs=ospecs,
        out_shape=outs, compiler_params=_params(2),
    )(x1, x2, m1, m2, aux, dvec)


def _ssm_dw(name, src, x1, x2, tk):
    L = src.shape[0]
    nb = x1.shape[0] // SLAB
    dn = (((0,), (0,)), ((), ()))

    def body(s_ref, x1_ref, x2_ref, o1_ref, o2_ref):
        k = pl.program_id(1)
        s = s_ref[...].astype(BF16)
        a1 = jnp.concatenate([x1_ref[q] for q in range(SLAB)], axis=1).astype(BF16)
        a2 = jnp.concatenate([x2_ref[q] for q in range(SLAB)], axis=1).astype(BF16)
        _acc_rows(k, o1_ref, lax.dot_general(s, a1, dn, preferred_element_type=F32))
        _acc_rows(k, o2_ref, lax.dot_general(s, a2, dn, preferred_element_type=F32))

    xs = pl.BlockSpec((SLAB, tk, LANES), lambda j, k: (j, k, 0))
    os_ = pl.BlockSpec((None, LANES, SLAB * LANES), lambda j, k: (j, 0, 0))
    return pl.pallas_call(
        body, name=name, grid=(nb, L // tk),
        in_specs=[pl.BlockSpec((tk, LANES), lambda j, k: (k, j)), xs, xs], out_specs=[os_, os_],
        out_shape=[_sds((nb, LANES, SLAB * LANES), F32)] * 2, compiler_params=_params(2),
    )(src, x1, x2)


def _scan(name, b_re, b_im, a_re, a_im, xs=None):
    reverse = xs is not None
    ns, L, _ = b_re.shape
    ng = ns // 8
    tc = min(LANES, L)
    pitch = tc + 8
    nt = L // tc
    n_in = 4 if reverse else 2

    def body(*refs):
        ins = refs[:n_in]
        ar_ref, ai_ref = refs[n_in], refs[n_in + 1]
        o_re, o_im = refs[n_in + 2], refs[n_in + 3]
        k = n_in + 4
        if reverse:
            da_re, da_im = refs[k], refs[k + 1]
            k += 2
        stage = refs[k:k + n_in]
        out_re, out_im, st_re, st_im = refs[k + n_in:k + n_in + 4]
        acc = refs[k + n_in + 4:]
        i = pl.program_id(0)

        @pl.when(i == 0)
        def _():
            st_re[...] = jnp.zeros(st_re.shape, F32)
            st_im[...] = jnp.zeros(st_im.shape, F32)
            for r in acc:
                r[...] = jnp.zeros(r.shape, F32)

        for s in range(ns):
            for src, dst in zip(ins, stage):
                dst[pl.ds(s * pitch, tc), :] = src[s].astype(F32)

        a_r = [ar_ref[g] for g in range(ng)]
        a_i = [ai_ref[g] for g in range(ng)]

        def step(tt, carry):
            t = (tc - 1 - tt) if reverse else tt
            new = []
            for g in range(ng):
                rows = pl.ds(g * 8 * pitch + t, 8, stride=pitch)
                cr, ci = carry[2 * g], carry[2 * g + 1]
                br, bi = stage[0][rows, :], stage[1][rows, :]
                if reverse:
                    xr, xi = stage[2][rows, :], stage[3][rows, :]
                    acc[0][g] += xr * cr + xi * ci
                    acc[1][g] += xr * ci - xi * cr
                    nr = a_r[g] * cr + a_i[g] * ci + br
                    ni = a_r[g] * ci - a_i[g] * cr + bi
                else:
                    nr = a_r[g] * cr - a_i[g] * ci + br
                    ni = a_r[g] * ci + a_i[g] * cr + bi
                out_re[rows, :] = nr
                out_im[rows, :] = ni
                new += [nr, ni]
            return tuple(new)

        init = []
        for g in range(ng):
            init += [st_re[g], st_im[g]]
        fin = lax.fori_loop(0, tc, step, tuple(init), unroll=2)
        for g in range(ng):
            st_re[g] = fin[2 * g]
            st_im[g] = fin[2 * g + 1]
        for s in range(ns):
            o_re[s] = out_re[pl.ds(s * pitch, tc), :].astype(BF16)
            o_im[s] = out_im[pl.ds(s * pitch, tc), :].astype(BF16)
        if reverse:
            da_re[...] = acc[0][...]
            da_im[...] = acc[1][...]

    tmap = (lambda i: (0, nt - 1 - i, 0)) if reverse else (lambda i: (0, i, 0))
    bs = pl.BlockSpec((ns, tc, LANES), tmap)
    as_ = pl.BlockSpec((ng, 8, LANES), lambda i: (0, 0, 0))
    ins = [b_re, b_im] + (list(xs) if reverse else [])
    out_shape = [_sds((ns, L, LANES), BF16)] * 2 + ([_sds((ng, 8, LANES), F32)] * 2 if reverse else [])
    out_specs = [bs, bs] + ([as_, as_] if reverse else [])
    scratch = [pltpu.VMEM((ns * pitch, LANES), F32)] * (n_in + 2) + [pltpu.VMEM((ng, 8, LANES), F32)] * (4 if reverse else 2)
    return pl.pallas_call(
        body, name=name, grid=(nt,), in_specs=[bs] * n_in + [as_, as_], out_specs=out_specs,
        out_shape=out_shape, scratch_shapes=scratch, compiler_params=_params(1),
    )(*ins, a_re, a_im)


def _peer(k):
    x, y, c = lax.axis_index("x"), lax.axis_index("y"), lax.axis_index("c")
    px = 1 - x if (k >> 2) & 1 else x
    py = 1 - y if (k >> 1) & 1 else y
    pc = 1 - c if k & 1 else c
    return (px, py, pc), 4 * px + 2 * py + pc


def _window(ref, kind, idx, n):
    if kind == "col":
        w = ref.shape[1] // n
        return ref.at[:, pl.ds(pl.multiple_of(idx * w, LANES), w)]
    r = ref.shape[0] // n
    return ref.at[pl.ds(pl.multiple_of(idx * r, 8), r), :]


def _all_gather(name, shards, kinds):
    n = len(shards)
    fulls = []
    for s, kind in zip(shards, kinds):
        fulls.append(_sds((s.shape[0], s.shape[1] * N_DEV) if kind == "col" else (s.shape[0] * N_DEV, s.shape[1]), s.dtype))

    def body(*refs):
        src, dst = refs[:n], refs[n:2 * n]
        send, recv, loc = refs[2 * n:]
        me = 4 * lax.axis_index("x") + 2 * lax.axis_index("y") + lax.axis_index("c")
        copies = []
        for a in range(n):
            own = pltpu.make_async_copy(src[a], _window(dst[a], kinds[a], me, N_DEV), loc.at[a])
            own.start()
            copies.append(own)
        sends = []
        for k in range(1, N_DEV):
            dev, _ = _peer(k)
            for a in range(n):
                cp = pltpu.make_async_remote_copy(
                    src_ref=src[a], dst_ref=_window(dst[a], kinds[a], me, N_DEV),
                    send_sem=send.at[a * N_DEV + k], recv_sem=recv.at[a * N_DEV + k],
                    device_id=dev, device_id_type=MESH)
                cp.start()
                sends.append(cp)
        for k in range(1, N_DEV):
            dev, pidx = _peer(k)
            for a in range(n):
                pltpu.make_async_remote_copy(
                    src_ref=src[a], dst_ref=_window(dst[a], kinds[a], pidx, N_DEV),
                    send_sem=send.at[a * N_DEV + k], recv_sem=recv.at[a * N_DEV + k],
                    device_id=dev, device_id_type=MESH).wait_recv()
        for cp in sends:
            cp.wait_send()
        for cp in copies:
            cp.wait()

    any_ = pl.BlockSpec(memory_space=pl.ANY)
    return pl.pallas_call(
        body, name=name, in_specs=[any_] * n, out_specs=[any_] * n, out_shape=fulls,
        scratch_shapes=[pltpu.SemaphoreType.DMA((n * N_DEV,)), pltpu.SemaphoreType.DMA((n * N_DEV,)),
                        pltpu.SemaphoreType.DMA((n,))],
        compiler_params=pltpu.CompilerParams(has_side_effects=True),
    )(*shards)


_HBM = pl.BlockSpec(memory_space=pltpu.HBM)
_SEM = pl.BlockSpec(memory_space=pltpu.SEMAPHORE)
_ANY = pl.BlockSpec(memory_space=pl.ANY)
_EFFECT = pltpu.SideEffectType.DATAFLOW_SIDE_EFFECTING


def _xfer_refs(mode, kinds, a, src, dst, me, pidx):
    if mode == "gather":
        return src[a], _window(dst[a], kinds[a], me, N_DEV), _window(dst[a], kinds[a], pidx, N_DEV)
    return _window(src[a], kinds[a], pidx, N_DEV), dst[a].at[me], dst[a].at[pidx]


def _xfer_out_shapes(mode, arrs, kinds):
    outs = []
    for s, kind in zip(arrs, kinds):
        if mode == "gather":
            outs.append((s.shape[0], s.shape[1] * N_DEV) if kind == "col" else (s.shape[0] * N_DEV, s.shape[1]))
        else:
            outs.append((N_DEV,) + ((s.shape[0], s.shape[1] // N_DEV) if kind == "col" else (s.shape[0] // N_DEV, s.shape[1])))
    return outs


def _xfer_start(name, mode, arrs, kinds, after):
    n = len(arrs)
    shapes = _xfer_out_shapes(mode, arrs, kinds)

    def body(*refs):
        src, dst = refs[:n], refs[n:2 * n]
        send, recv = refs[2 * n + 1], refs[2 * n + 2]
        token, loc = refs[2 * n + 3 + 2 * n], refs[2 * n + 4 + 2 * n]
        me = 4 * lax.axis_index("x") + 2 * lax.axis_index("y") + lax.axis_index("c")
        own = []
        for a in range(n):
            s, _, d = _xfer_refs(mode, kinds, a, src, dst, me, me)
            own.append(pltpu.make_async_copy(s, d, loc.at[a]))
            own[-1].start()
        for cp in own:
            cp.wait()
        for k in range(1, N_DEV):
            dev, pidx = _peer(k)
            for a in range(n):
                s, d, _ = _xfer_refs(mode, kinds, a, src, dst, me, pidx)
                pltpu.make_async_remote_copy(src_ref=s, dst_ref=d, send_sem=send.at[a * N_DEV + k],
                                             recv_sem=recv.at[a * N_DEV + k], device_id=dev, device_id_type=MESH).start()
        token[...] = jnp.zeros(token.shape, F32)

    lands = [pltpu.with_memory_space_constraint(lax.empty(shp, s.dtype), pltpu.HBM) for shp, s in zip(shapes, arrs)]
    srcs = [pltpu.with_memory_space_constraint(s, pltpu.HBM) for s in arrs]
    res = pl.pallas_call(
        body, name=name,
        in_specs=[_HBM] * (2 * n) + [_ANY],
        out_specs=[_SEM, _SEM] + [_HBM] * (2 * n) + [pl.BlockSpec(memory_space=pltpu.VMEM)],
        out_shape=[pltpu.SemaphoreType.DMA((n * N_DEV,)), pltpu.SemaphoreType.DMA((n * N_DEV,))]
        + [pltpu.HBM(s.shape, s.dtype) for s in arrs] + [pltpu.HBM(shp, s.dtype) for shp, s in zip(shapes, arrs)]
        + [_sds((8, LANES), F32)],
        input_output_aliases={i: 2 + i for i in range(2 * n)},
        scratch_shapes=[pltpu.SemaphoreType.DMA((n,))],
        compiler_params=pltpu.CompilerParams(has_side_effects=_EFFECT),
    )(*srcs, *lands, after)
    return dict(mode=mode, kinds=kinds, n=n, send=res[0], recv=res[1], srcs=res[2:2 + n], lands=res[2 + n:2 + 2 * n]), res[-1]


def _xfer_wait(name, st, after):
    n, mode, kinds = st["n"], st["mode"], st["kinds"]

    def body(*refs):
        src, dst = refs[:n], refs[n:2 * n]
        send, recv = refs[2 * n], refs[2 * n + 1]
        me = 4 * lax.axis_index("x") + 2 * lax.axis_index("y") + lax.axis_index("c")
        for k in range(1, N_DEV):
            dev, pidx = _peer(k)
            for a in range(n):
                s, d, land = _xfer_refs(mode, kinds, a, src, dst, me, pidx)
                cp = pltpu.make_async_remote_copy(src_ref=s, dst_ref=land, send_sem=send.at[a * N_DEV + k],
                                                  recv_sem=recv.at[a * N_DEV + k], device_id=dev, device_id_type=MESH)
                cp.wait_send()
                cp.wait_recv()

    res = pl.pallas_call(
        body, name=name,
        in_specs=[_HBM] * (2 * n) + [_SEM, _SEM, _ANY],
        out_specs=[_HBM] * (2 * n),
        out_shape=[pltpu.HBM(s.shape, s.dtype) for s in st["srcs"]] + [pltpu.HBM(s.shape, s.dtype) for s in st["lands"]],
        input_output_aliases={i: i for i in range(2 * n)},
        compiler_params=pltpu.CompilerParams(has_side_effects=_EFFECT),
    )(*st["srcs"], *st["lands"], st["send"], st["recv"], after)
    return list(res[n:])


def _sc_xfer(name, mode, arrs, kinds, collective_id):
    n = len(arrs)
    shapes = _xfer_out_shapes(mode, arrs, kinds)
    hbm = pltpu.MemorySpace.HBM
    src = [jax.new_ref(a, memory_space=hbm) for a in arrs]
    dst = [jax.empty_ref(_sds(shp, a.dtype), memory_space=hbm) for shp, a in zip(shapes, arrs)]

    @pl.kernel(mesh=plsc.ScalarSubcoreMesh(axis_name="seq", num_cores=1), name=name,
               scratch_types=(pltpu.SemaphoreType.DMA((n * N_DEV,)), pltpu.SemaphoreType.DMA((n * N_DEV,)),
                              pltpu.SemaphoreType.DMA((n,))),
               compiler_params=pltpu.CompilerParams(collective_id=collective_id))
    def launch(send, recv, loc):
        barrier = pltpu.get_barrier_semaphore()
        for k in range(1, N_DEV):
            pl.semaphore_signal(barrier, inc=1, device_id=_peer(k)[0], device_id_type=MESH)
        pl.semaphore_wait(barrier, N_DEV - 1)
        me = 4 * lax.axis_index("x") + 2 * lax.axis_index("y") + lax.axis_index("c")
        own, sends = [], []
        for a in range(n):
            s, _, d = _xfer_refs(mode, kinds, a, src, dst, me, me)
            own.append(pltpu.make_async_copy(s, d, loc.at[a]))
            own[-1].start()
        for k in range(1, N_DEV):
            dev, pidx = _peer(k)
            for a in range(n):
                s, d, _ = _xfer_refs(mode, kinds, a, src, dst, me, pidx)
                sends.append(pltpu.make_async_remote_copy(src_ref=s, dst_ref=d, send_sem=send.at[a * N_DEV + k],
                                                          recv_sem=recv.at[a * N_DEV + k], device_id=dev, device_id_type=MESH))
                sends[-1].start()
        for cp in own:
            cp.wait()
        for k in range(1, N_DEV):
            dev, pidx = _peer(k)
            for a in range(n):
                s, _, land = _xfer_refs(mode, kinds, a, src, dst, me, pidx)
                pltpu.make_async_remote_copy(src_ref=s, dst_ref=land, send_sem=send.at[a * N_DEV + k],
                                             recv_sem=recv.at[a * N_DEV + k], device_id=dev, device_id_type=MESH).wait_recv()
        for cp in sends:
            cp.wait_send()

    launch()
    return [d[...] for d in dst]


def _sc_gather(name, arrs, kinds, collective_id):
    n = len(arrs)
    pairs = 7
    shapes = _xfer_out_shapes("gather", arrs, kinds)
    hbm = pltpu.MemorySpace.HBM
    src = [jax.new_ref(a, memory_space=hbm) for a in arrs]
    dst = [jax.empty_ref(_sds(shp, a.dtype), memory_space=hbm) for shp, a in zip(shapes, arrs)]

    @pl.kernel(mesh=plsc.ScalarSubcoreMesh(axis_name="seq", num_cores=1), name=name,
               scratch_types=(pltpu.SemaphoreType.DMA((n * pairs,)), pltpu.SemaphoreType.DMA((n * pairs,)),
                              pltpu.SemaphoreType.DMA((n,))),
               compiler_params=pltpu.CompilerParams(collective_id=collective_id))
    def launch(send, recv, loc):
        x, y, c = lax.axis_index("x"), lax.axis_index("y"), lax.axis_index("c")
        me = 4 * x + 2 * y + c
        sib = (x, y, 1 - c)
        chips = []
        for fx, fy in ((1, 0), (0, 1), (1, 1)):
            px, py = (1 - x if fx else x), (1 - y if fy else y)
            chips.append(((px, py, c), 4 * px + 2 * py + c, 4 * px + 2 * py + (1 - c)))
        barrier = pltpu.get_barrier_semaphore()
        for dev in [sib] + [ch[0] for ch in chips]:
            pl.semaphore_signal(barrier, inc=1, device_id=dev, device_id_type=MESH)
        pl.semaphore_wait(barrier, 4)

        def win(a, idx):
            return _window(dst[a], kinds[a], idx, N_DEV)

        def rcopy(a, p, s, d, dev):
            return pltpu.make_async_remote_copy(src_ref=s, dst_ref=d, send_sem=send.at[a * pairs + p],
                                                recv_sem=recv.at[a * pairs + p], device_id=dev, device_id_type=MESH)

        own, sends = [], []
        for a in range(n):
            own.append(pltpu.make_async_copy(src[a], win(a, me), loc.at[a]))
            own[-1].start()
        for j, (dev, _, _) in enumerate(chips):
            for a in range(n):
                sends.append(rcopy(a, 1 + j, src[a], win(a, me), dev))
                sends[-1].start()
        for a in range(n):
            sends.append(rcopy(a, 0, src[a], win(a, me), sib))
            sends[-1].start()
        for j, (dev, idx, _) in enumerate(chips):
            for a in range(n):
                rcopy(a, 1 + j, src[a], win(a, idx), dev).wait_recv()
                sends.append(rcopy(a, 4 + j, win(a, idx), win(a, idx), sib))
                sends[-1].start()
        for cp in own:
            cp.wait()
        for a in range(n):
            rcopy(a, 0, src[a], win(a, 4 * x + 2 * y + (1 - c)), sib).wait_recv()
        for j, (_, _, sidx) in enumerate(chips):
            for a in range(n):
                rcopy(a, 4 + j, src[a], win(a, sidx), sib).wait_recv()
        for cp in sends:
            cp.wait_send()

    launch()
    return [d[...] for d in dst]


_SEQ_IDS = {"gather_in": 7, "gather_mix": 1, "gather_ffn": 2, "grads_ffn": 3, "grads_mix": 4, "grads_small": 5, "grads_in": 6}


def _seq_start(name, mode, arrs, kinds, after):
    arrs = list(arrs)
    if after is not None:
        after, *arrs = lax.optimization_barrier((after, *arrs))
    if mode == "gather":
        return _sc_gather(name, arrs, kinds, _SEQ_IDS[name]), None
    return _sc_xfer(name, mode, arrs, kinds, _SEQ_IDS[name]), None


def _seq_wait(name, res, after):
    del name, after
    return list(res)


def _adamw(name, parts, w, m, v):
    P, R, C = parts.shape
    sub = 16 if parts.dtype == BF16 else 8
    tr = R if R * C <= (1 << 18) else _tile(R, max(sub, (1 << 18) // C), sub)

    def body(p_ref, w_ref, m_ref, v_ref, g_ref, d_ref, nm_ref, nv_ref):
        g = p_ref[0].astype(F32)
        for s in range(1, P):
            g = g + p_ref[s].astype(F32)
        m2 = ADAM_B1 * m_ref[...] + (1.0 - ADAM_B1) * g
        v2 = ADAM_B2 * v_ref[...] + (1.0 - ADAM_B2) * (g * g)
        m_hat = m2 / (1.0 - ADAM_B1 ** ADAM_STEP)
        v_hat = v2 / (1.0 - ADAM_B2 ** ADAM_STEP)
        g_ref[...] = g
        d_ref[...] = -ADAM_LR * (m_hat / (jnp.sqrt(v_hat) + ADAM_EPS) + ADAM_WD * w_ref[...])
        nm_ref[...] = m2
        nv_ref[...] = v2

    sp = pl.BlockSpec((tr, C), lambda i: (i, 0))
    return pl.pallas_call(
        body, name=name, grid=(R // tr,),
        in_specs=[pl.BlockSpec((P, tr, C), lambda i: (0, i, 0)), sp, sp, sp], out_specs=[sp] * 4,
        out_shape=[_sds((R, C), F32)] * 4, compiler_params=_params(1),
    )(parts, w, m, v)


def _pack(arrs, row_mult=8):
    pieces, total = [], 0
    for a in arrs:
        f = a.reshape(-1).astype(F32)
        pad = (-f.shape[0]) % (8 * LANES)
        pieces.append(jnp.pad(f, (0, pad)) if pad else f)
        total += f.shape[0] + pad
    tail = (-total) % (row_mult * LANES)
    if tail:
        pieces.append(jnp.zeros((tail,), F32))
    return jnp.concatenate(pieces).reshape(-1, LANES)


def _unpack(buf, shapes, lead=()):
    out, row = [], 0
    for shp in shapes:
        size = 1
        for d in shp:
            size *= d
        rows = -(-size // (8 * LANES)) * 8
        piece = buf[..., row:row + rows, :].reshape(lead + (rows * LANES,))[..., :size]
        out.append(piece.reshape(lead + tuple(shp)))
        row += rows
    return out


def kernel(x, norm_tok, w_in, a_re, a_im, log_dt, b_re, b_im, c_re, c_im, d_skip, w_glu, w_ssm_out, conv_w, conv_b, w_conv_out, w_o, norm_ffn, w_up, ffn_conv_w, ffn_conv_b, w_down, norm_final, loss_target, m_norm_tok, m_w_in, m_a_re, m_a_im, m_log_dt, m_b_re, m_b_im, m_c_re, m_c_im, m_d_skip, m_w_glu, m_w_ssm_out, m_conv_w, m_conv_b, m_w_conv_out, m_w_o, m_norm_ffn, m_w_up, m_ffn_conv_w, m_ffn_conv_b, m_w_down, m_norm_final, v_norm_tok, v_w_in, v_a_re, v_a_im, v_log_dt, v_b_re, v_b_im, v_c_re, v_c_im, v_d_skip, v_w_glu, v_w_ssm_out, v_conv_w, v_conv_b, v_w_conv_out, v_w_o, v_norm_ffn, v_w_up, v_ffn_conv_w, v_ffn_conv_b, v_w_down, v_norm_final):
    args = dict(locals())
    L, D = x.shape[1], x.shape[2]
    G, P, H = b_re.shape[1], b_re.shape[2], b_re.shape[3]
    SW = G * H
    CW = conv_b.shape[1]
    FF = ffn_conv_b.shape[1]
    GP = G * P
    nb = SW // LANES
    gpb = LANES // H
    me = 4 * lax.axis_index("x") + 2 * lax.axis_index("y") + lax.axis_index("c")
    tm = _tile(L, 256, 16)
    x2 = x[0]
    tgt = loss_target[0]

    big = [("w_in", "col"), ("w_glu", "row"), ("w_ssm_out", "col"), ("w_conv_out", "col"), ("w_o", "row"),
           ("w_up", "col"), ("w_down", "row")]
    shards = [_cast_bf16("cast_" + n, args[n][0]) for n, _ in big]
    small_in = _pack([conv_w[0], ffn_conv_w[0]])
    kind = dict(big)
    mixw, ffnw = ["w_glu", "w_ssm_out", "w_conv_out", "w_o"], ["w_up", "w_down"]
    shard = dict(zip([n for n, _ in big], shards))
    gathered, _ = _seq_start("gather_in", "gather", [shard["w_in"], small_in], ["col", "row"], None)
    W = {"w_in": gathered[0]}
    st_mix, tok_mix = _seq_start("gather_mix", "gather", [shard[n] for n in mixw], [kind[n] for n in mixw], None)
    st_ffn, tok_ffn = _seq_start("gather_ffn", "gather", [shard[n] for n in ffnw], [kind[n] for n in ffnw], None)
    cw_parts, fcw_parts = _unpack(gathered[-1].reshape(N_DEV, -1, LANES), [conv_w.shape[1:], ffn_conv_w.shape[1:]], (N_DEV,))
    conv_w_full = jnp.moveaxis(cw_parts, 0, 1).reshape(3, CW)
    ffn_conv_w_full = jnp.moveaxis(fcw_parts, 0, 1).reshape(3, FF)

    ar_row, ai_row = a_re.reshape(1, GP), a_im.reshape(1, GP)
    ldt_row = jnp.broadcast_to(log_dt.reshape(G, 1), (G, P)).reshape(1, GP)
    brt = jnp.transpose(b_re[0], (2, 0, 1)).reshape(H, GP)
    bit = jnp.transpose(b_im[0], (2, 0, 1)).reshape(H, GP)
    abar_re, abar_im, bbar_re, bbar_im = _prep_fwd("s5_prep", ar_row, ai_row, ldt_row, brt, bit)
    eye = jnp.eye(gpb, dtype=F32)

    def b_blocks(bt):
        return jnp.einsum("ab,hjbp->jahbp", eye, bt.reshape(H, nb, gpb, P)).reshape(nb, LANES, gpb * P)

    def c_blocks(c):
        return jnp.einsum("ab,jahp->jbpah", eye, c.reshape(nb, gpb, H, P)).reshape(nb, gpb * P, LANES)

    def diag_blocks(mat):
        return jnp.einsum("jahap->hjap", mat.reshape(nb, gpb, H, gpb, P))

    bm_re, bm_im = b_blocks(bbar_re), b_blocks(bbar_im)
    cm_re, cm_im = c_blocks(c_re[0]), -c_blocks(c_im[0])
    a3_re, a3_im = abar_re.reshape(-1, 8, LANES), abar_im.reshape(-1, 8, LANES)
    dskip_row = d_skip.reshape(1, SW)

    cbs = SW // LANES
    cb_v, cb_gb, cb_gc = cbs, cbs + CW // LANES, cbs + 2 * CW // LANES
    cb_ma = (SW + 3 * CW) // D
    xn = _rms_fwd("rms_tok", x2, norm_tok, tm)
    proj = _mm("proj", xn, W["w_in"], "nn", out_dtype=BF16)
    ts, tw = _tile(L, 512, 16), _tile(L, 1024, 16)
    bu_re, bu_im = _ssm_in("s5_bu", proj, bm_re.astype(BF16), bm_im.astype(BF16), ts)
    xs_re, xs_im = _scan("s5_scan", bu_re, bu_im, a3_re, a3_im)
    y, ya = _ssm_out("s5_y", xs_re, xs_im, cm_re.astype(BF16), cm_im.astype(BF16), proj, dskip_row, ts, post=jax.nn.gelu)
    W.update(zip(mixw, _seq_wait("gather_mix", st_mix, ya)))
    g1 = _mm("glu_gate", ya, W["w_glu"], "nn")
    ya2 = _glu_fwd("glu", y, g1, tm)
    za = _mm("ssm_out", ya2, W["w_ssm_out"], "nn")
    q = _convb_fwd("convb", proj, cb_v, cb_gb, cb_gc, conv_w_full, conv_b)
    zb = _mm("conv_out", q, W["w_conv_out"], "nn")
    merged = _merge_fwd("merge", proj, cb_ma, cb_ma + 1, za, zb, tm)
    o1 = _mm("mix_out", merged, W["w_o"], "nn")
    h1, hn = _res_rms_fwd("rms_ffn", x2, o1, norm_ffn, tm)
    W.update(zip(ffnw, _seq_wait("gather_ffn", st_ffn, hn)))
    hh = _mm("ffn_up", hn, W["w_up"], "nn", out_dtype=BF16)
    f = _ffn_fwd("ffn_act", hh, ffn_conv_w_full, ffn_conv_b)
    o2 = _mm("ffn_down", f, W["w_down"], "nn", tk=2816)
    dh2, dh2b, g_norm_final, loss_part = _final("final", h1, o2, norm_final.reshape(1, D), tgt, tm)

    df = _mm("d_ffn_act", dh2b, W["w_down"], "nt", tn=1408, out_dtype=BF16)
    gw_down = _mm("gw_down", f, dh2b, "tn", out_dtype=BF16, tm=1408, tn=512, tk=L)
    dhh, g_ffn_conv_w, g_ffn_conv_b = _ffn_bwd("ffn_act_bwd", hh, ffn_conv_w_full, ffn_conv_b, df)
    nhalf = lambda t: FF // t
    dhn = _mm("d_ffn_in", dhh, W["w_up"], "nt", tk=_tile(FF, 2816), dims=(L, D, 2 * FF),
              a_spec=lambda a, b, c: pl.BlockSpec((None, a, c), lambda i, j, k: (k // nhalf(c), i, k % nhalf(c))))
    gw_up = _mm("gw_up", hn, dhh, "tn", out_dtype=BF16, tn=_tile(FF, 1024), tk=L, dims=(D, 2 * FF, L),
                b_spec=lambda a, b, c: pl.BlockSpec((None, c, b), lambda i, j, k: (j // nhalf(b), k, j % nhalf(b))))
    dhn, gw_up, gw_down = lax.optimization_barrier((dhn, gw_up, gw_down))
    st_gffn, tok_gffn = _seq_start("grads_ffn", "exchange", [gw_up, gw_down], [kind[n] for n in ffnw], None)
    dh1, dh1b, g_norm_ffn = _rms_bwd("rms_ffn_bwd", dhn, h1, norm_ffn, dh2, tm, True)

    dmerged = _mm("d_merged", dh1b, W["w_o"], "nt", dep=tok_gffn)
    gw_o = _mm("gw_o", merged, dh1b, "tn", out_dtype=BF16, tk=L)
    dmerged, gw_o = lax.optimization_barrier((dmerged, gw_o))
    dza, dzb, dma, dmb = _merge_bwd("merge_bwd", proj, cb_ma, cb_ma + 1, za, zb, dmerged, tm)
    dq = _mm("d_q", dzb, W["w_conv_out"], "nt")
    gw_conv_out = _mm("gw_conv_out", q, dzb, "tn", out_dtype=BF16, tk=L)
    dq, gw_conv_out = lax.optimization_barrier((dq, gw_conv_out))
    dv, dgb, dgc, g_conv_w, g_conv_b = _convb_bwd("convb_bwd", proj, cb_v, cb_gb, cb_gc, conv_w_full, conv_b, dq)
    dya2 = _mm("d_ya2", dza, W["w_ssm_out"], "nt")
    gw_ssm_out = _mm("gw_ssm_out", ya2, dza, "tn", out_dtype=BF16, tk=L)
    dya2, gw_ssm_out = lax.optimization_barrier((dya2, gw_ssm_out))
    dy_direct, dg1 = _glu_bwd("glu_bwd", y, g1, dya2, tm)
    dya_g = _mm("d_ya_gate", dg1, W["w_glu"], "nt")
    gw_glu = _mm("gw_glu", ya, dg1, "tn", out_dtype=BF16, tk=L)
    dya_g, gw_glu = lax.optimization_barrier((dya_g, gw_glu))
    st_gmix, tok_gmix = _seq_start("grads_mix", "exchange", [gw_glu, gw_ssm_out, gw_conv_out, gw_o],
                                   [kind[n] for n in mixw], None)
    dyb, g_dskip = _gelu_bwd("gelu_bwd", y, dy_direct, dya_g, proj, dskip_row, tm)
    dxs_re, dxs_im = _ssm_in("s5_dx", dyb, jnp.swapaxes(cm_re, 1, 2).astype(BF16), jnp.swapaxes(cm_im, 1, 2).astype(BF16), ts,
                             dep=tok_gmix)
    gc_re, gc_im = _ssm_dw("s5_gc", dyb, xs_re, xs_im, tw)
    lam_re, lam_im, dab_re, dab_im = _scan("s5_scan_bwd", dxs_re, dxs_im, a3_re, a3_im, xs=(xs_re, xs_im))
    parts = dict(zip(ffnw, _seq_wait("grads_ffn", st_gffn, dab_re)))
    du = _ssm_out("s5_du", lam_re, lam_im, jnp.swapaxes(bm_re, 1, 2).astype(BF16), jnp.swapaxes(bm_im, 1, 2).astype(BF16),
                  dyb, dskip_row, ts, post=lambda t: t)[1]
    gb_re, gb_im = _ssm_dw("s5_gb", proj, lam_re, lam_im, tw)
    g_ar, g_ai, g_ldt, g_brt, g_bit = _prep_bwd(
        "s5_prep_bwd", ar_row, ai_row, ldt_row, brt, bit, dab_re.reshape(1, GP), dab_im.reshape(1, GP),
        diag_blocks(gb_re).reshape(H, GP), diag_blocks(gb_im).reshape(H, GP))
    small = dict(
        a_re=g_ar.reshape(1, G, P), a_im=g_ai.reshape(1, G, P),
        log_dt=g_ldt.reshape(G, P).sum(axis=1).reshape(1, G),
        b_re=jnp.transpose(g_brt.reshape(H, G, P), (1, 2, 0))[None], b_im=jnp.transpose(g_bit.reshape(H, G, P), (1, 2, 0))[None],
        c_re=jnp.transpose(diag_blocks(gc_re), (1, 2, 0, 3)).reshape(1, G, H, P),
        c_im=-jnp.transpose(diag_blocks(gc_im), (1, 2, 0, 3)).reshape(1, G, H, P),
        d_skip=g_dskip.reshape(1, G, H), conv_b=g_conv_b, norm_ffn=g_norm_ffn, ffn_conv_b=g_ffn_conv_b,
        norm_final=g_norm_final.reshape(D), conv_w=g_conv_w[None], ffn_conv_w=g_ffn_conv_w[None])
    rep = ["a_re", "a_im", "log_dt", "b_re", "b_im", "c_re", "c_im", "d_skip", "conv_b", "norm_ffn", "ffn_conv_b", "norm_final"]
    order = rep + ["conv_w", "ffn_conv_w"]
    full_shapes = {n: args[n].shape for n in rep}
    full_shapes["conv_w"], full_shapes["ffn_conv_w"] = (1, 3, CW), (1, 3, FF)
    rep_pack = _pack([small[n] for n in rep], LANES)
    rep_rows = rep_pack.shape[0]
    gpack = jnp.concatenate([rep_pack, _pack([small["conv_w"], small["ffn_conv_w"]])], axis=0)
    rows = gpack.shape[0]
    du, gpack = lax.optimization_barrier((du, gpack))
    st_small, tok_small = _seq_start("grads_small", "gather", [gpack], ["row"], None)

    dproj = jnp.concatenate([du, dv, dgb, dgc, dma, dmb], axis=1)
    gw_in = _mm("gw_in", xn, dproj, "tn", out_dtype=BF16, tk=L, dep=tok_small)
    dproj, gw_in = lax.optimization_barrier((dproj, gw_in))
    st_gin, tok_gin = _seq_start("grads_in", "exchange", [gw_in], ["col"], None)
    dxn = _mm("d_xn", dproj, W["w_in"], "nt", dep=tok_gin)
    grad_x, g_norm_tok = _rms_bwd("rms_tok_bwd", dxn, x2, norm_tok, dh1, tm, False)

    res = {}

    def big_update(n):
        res[n] = [r[None] for r in _adamw("adamw_" + n, parts[n], args[n][0], args["m_" + n][0], args["v_" + n][0])]

    def after(xs, dep):
        return lax.optimization_barrier((list(xs), dep))[0]

    parts = dict(zip(ffnw, after([parts[n] for n in ffnw], grad_x)))
    for n in ffnw:
        big_update(n)
    parts.update(zip(mixw, after(_seq_wait("grads_mix", st_gmix, grad_x), [res[n][1] for n in ffnw])))
    for n in mixw:
        big_update(n)
    gall = after(_seq_wait("grads_small", st_small, None), [res[n][1] for n in mixw])[0].reshape(N_DEV, rows, LANES)
    gcw, gfcw = _unpack(gall[:, rep_rows:], [full_shapes["conv_w"], full_shapes["ffn_conv_w"]], (N_DEV,))
    cws, fcws = CW // N_DEV, FF // N_DEV
    gcw = lax.dynamic_slice_in_dim(gcw[:, 0], me * cws, cws, axis=2)
    gfcw = lax.dynamic_slice_in_dim(gfcw[:, 0], me * fcws, fcws, axis=2)
    res["conv_w"] = [r[None] for r in _adamw("adamw_conv_w", gcw, conv_w[0], m_conv_w[0], v_conv_w[0])]
    res["ffn_conv_w"] = [r[None] for r in _adamw("adamw_ffn_conv_w", gfcw, ffn_conv_w[0], m_ffn_conv_w[0], v_ffn_conv_w[0])]
    rep_out = _adamw("adamw_small", gall[:, :rep_rows], _pack([args[n] for n in rep], LANES),
                     _pack([args["m_" + n] for n in rep], LANES), _pack([args["v_" + n] for n in rep], LANES))
    rep_out = [_unpack(r, [full_shapes[n] for n in rep]) for r in rep_out]
    for i, n in enumerate(rep):
        res[n] = [r[i] for r in rep_out]
    nt_pack = after([_pack([g_norm_tok])], [res[n][1] for n in ("a_re", "conv_w", "ffn_conv_w")])
    nt_all = _all_gather("gather_norm_tok_grad", nt_pack, ["row"])[0].reshape(N_DEV, -1, LANES)
    nt_out = _adamw("adamw_norm_tok", nt_all, _pack([norm_tok]), _pack([m_norm_tok]), _pack([v_norm_tok]))
    res["norm_tok"] = [_unpack(r, [norm_tok.shape])[0] for r in nt_out]
    parts["w_in"] = after(_seq_wait("grads_in", st_gin, None), nt_out[0])[0]
    big_update("w_in")

    loss = lax.psum(loss_part[0, 0], ("x", "y", "c"))
    names = ["norm_tok", "w_in", "a_re", "a_im", "log_dt", "b_re", "b_im", "c_re", "c_im", "d_skip", "w_glu", "w_ssm_out",
             "conv_w", "conv_b", "w_conv_out", "w_o", "norm_ffn", "w_up", "ffn_conv_w", "ffn_conv_b", "w_down", "norm_final"]
    out = [loss, grad_x[None]]
    for slot in range(4):
        out += [res[n][slot] for n in names]
    return tuple(out)
```

```python
import functools

import jax
import jax.numpy as jnp
from jax import lax
from jax.experimental import pallas as pl
from jax.experimental.pallas import tpu as pltpu
from jax.experimental.pallas import tpu_sc as plsc

F32 = jnp.float32
BF16 = jnp.bfloat16
N_DEV = 8
LANES = 128
SLAB = 4
EPS = 1e-6
ADAM_LR = 0.001
ADAM_B1 = 0.9
ADAM_B2 = 0.999
ADAM_EPS = 1e-08
ADAM_WD = 0.01
ADAM_STEP = 10
VMEM_LIMIT = 56 * 1024 * 1024
MESH = pl.DeviceIdType.MESH


def _tile(n, pref, mult=LANES):
    best = None
    t = mult
    while t <= min(n, pref):
        if n % t == 0:
            best = t
        t += mult
    return best if best is not None else n


def _params(ndim):
    return pltpu.CompilerParams(dimension_semantics=("arbitrary",) * ndim, vmem_limit_bytes=VMEM_LIMIT)


def _sds(shape, dtype):
    return jax.ShapeDtypeStruct(tuple(shape), dtype)


def _mm(name, a, b, mode, *, out_dtype=F32, tm=1024, tn=1024, tk=2048, dims=None, a_spec=None, b_spec=None, dep=None):
    if dims is None:
        if mode == "nn":
            (M, K), N = a.shape, b.shape[1]
        elif mode == "nt":
            (M, K), N = a.shape, b.shape[0]
        else:
            (K, M), N = a.shape, b.shape[1]
    else:
        M, N, K = dims
    tm, tn, tk = _tile(M, tm), _tile(N, tn), _tile(K, tk)
    nk = K // tk
    if mode == "nn":
        dn = (((1,), (0,)), ((), ()))
        sa = pl.BlockSpec((tm, tk), lambda i, j, k: (i, k))
        sb = pl.BlockSpec((tk, tn), lambda i, j, k: (k, j))
    elif mode == "nt":
        dn = (((1,), (1,)), ((), ()))
        sa = pl.BlockSpec((tm, tk), lambda i, j, k: (i, k))
        sb = pl.BlockSpec((tn, tk), lambda i, j, k: (j, k))
    else:
        dn = (((0,), (0,)), ((), ()))
        sa = pl.BlockSpec((tk, tm), lambda i, j, k: (k, i))
        sb = pl.BlockSpec((tk, tn), lambda i, j, k: (k, j))
    sa = a_spec(tm, tn, tk) if a_spec is not None else sa
    sb = b_spec(tm, tn, tk) if b_spec is not None else sb
    use_acc = nk > 1 and out_dtype != F32

    deps = [] if dep is None else [dep]

    def body(a_ref, b_ref, *rest):
        o_ref, acc = rest[len(deps)], rest[len(deps) + 1:]
        k = pl.program_id(2)
        p = lax.dot_general(a_ref[...], b_ref[...], dn, preferred_element_type=F32)
        if nk == 1:
            o_ref[...] = p.astype(out_dtype)
        else:
            tgt = acc[0] if use_acc else o_ref

            @pl.when(k == 0)
            def _():
                tgt[...] = p

            @pl.when(k > 0)
            def _():
                tgt[...] += p

            if use_acc:
                @pl.when(k == nk - 1)
                def _():
                    o_ref[...] = acc[0][...].astype(out_dtype)

    return pl.pallas_call(
        body, name=name, grid=(M // tm, N // tn, nk),
        in_specs=[sa, sb] + [pl.BlockSpec(memory_space=pl.ANY)] * len(deps),
        out_specs=pl.BlockSpec((tm, tn), lambda i, j, k: (i, j)),
        out_shape=_sds((M, N), out_dtype),
        scratch_shapes=[pltpu.VMEM((tm, tn), F32)] if use_acc else [],
        compiler_params=_params(3),
    )(a, b, *deps)


def _rows(name, body, L, tm, ins, outs):
    return pl.pallas_call(
        body, name=name, grid=(L // tm,),
        in_specs=[s for _, s in ins], out_specs=[s for _, s in outs],
        out_shape=[o for o, _ in outs], compiler_params=_params(1),
    )(*[a for a, _ in ins])


def _rs(tm, w, cb=0):
    return pl.BlockSpec((tm, w), lambda i: (i, cb))


def _fs(shape):
    return pl.BlockSpec(tuple(shape), lambda i: (0,) * len(shape))


def _acc_rows(i, ref, part):
    @pl.when(i == 0)
    def _():
        ref[...] = part

    @pl.when(i > 0)
    def _():
        ref[...] += part


def _cast_bf16(name, w):
    R, C = w.shape
    tr = _tile(R, max(16, (1 << 20) // C), 16)

    def body(w_ref, o_ref):
        o_ref[...] = w_ref[...].astype(BF16)

    return _rows(name, body, R, tr, [(w, _rs(tr, C))], [(_sds((R, C), BF16), _rs(tr, C))])[0]


def _concat_cols(name, pieces, tm):
    L = pieces[0].shape[0]
    widths = [p.shape[1] for p in pieces]

    def body(*refs):
        o_ref, off = refs[-1], 0
        for p_ref, w in zip(refs[:-1], widths):
            o_ref[:, off:off + w] = p_ref[...]
            off += w

    return _rows(name, body, L, tm, [(p, _rs(tm, w)) for p, w in zip(pieces, widths)],
                 [(_sds((L, sum(widths)), pieces[0].dtype), _rs(tm, sum(widths)))])[0]


def _rms_fwd(name, x, g, tm):
    L, D = x.shape

    def body(x_ref, g_ref, o_ref):
        xv = x_ref[...]
        r = lax.rsqrt(jnp.mean(xv * xv, axis=-1, keepdims=True) + EPS)
        o_ref[...] = (xv * r * g_ref[...]).astype(BF16)

    return _rows(name, body, L, tm, [(x, _rs(tm, D)), (g, _fs((1, D)))], [(_sds((L, D), BF16), _rs(tm, D))])[0]


def _res_rms_fwd(name, x, o, g, tm):
    L, D = x.shape

    def body(x_ref, o_ref, g_ref, h_ref, hn_ref):
        h = x_ref[...] + o_ref[...]
        r = lax.rsqrt(jnp.mean(h * h, axis=-1, keepdims=True) + EPS)
        h_ref[...] = h
        hn_ref[...] = (h * r * g_ref[...]).astype(BF16)

    return _rows(name, body, L, tm, [(x, _rs(tm, D)), (o, _rs(tm, D)), (g, _fs((1, D)))],
                 [(_sds((L, D), F32), _rs(tm, D)), (_sds((L, D), BF16), _rs(tm, D))])


def _rms_bwd(name, dn, h, g, dres, tm, with_bf16):
    L, D = h.shape

    def body(dn_ref, h_ref, g_ref, dres_ref, dh_ref, *rest):
        i = pl.program_id(0)
        h = h_ref[...]
        r = lax.rsqrt(jnp.mean(h * h, axis=-1, keepdims=True) + EPS)
        xh = h * r
        d = dn_ref[...]
        dxh = d * g_ref[...]
        dh = dres_ref[...] + r * (dxh - xh * jnp.mean(dxh * xh, axis=-1, keepdims=True))
        dh_ref[...] = dh
        if with_bf16:
            rest[0][...] = dh.astype(BF16)
        _acc_rows(i, rest[-1], jnp.sum(d * xh, axis=0, keepdims=True))

    outs = [(_sds((L, D), F32), _rs(tm, D))]
    if with_bf16:
        outs.append((_sds((L, D), BF16), _rs(tm, D)))
    outs.append((_sds((1, D), F32), _fs((1, D))))
    return _rows(name, body, L, tm, [(dn, _rs(tm, D)), (h, _rs(tm, D)), (g, _fs((1, D))), (dres, _rs(tm, D))], outs)


def _final(name, h1, o2, g, tgt, tm):
    L, D = h1.shape

    def body(h1_ref, o2_ref, g_ref, t_ref, dh_ref, dhb_ref, dg_ref, loss_ref):
        i = pl.program_id(0)
        h = h1_ref[...] + o2_ref[...]
        r = lax.rsqrt(jnp.mean(h * h, axis=-1, keepdims=True) + EPS)
        xh = h * r
        gv = g_ref[...]
        e = xh * gv - t_ref[...]
        part = 0.5 * jnp.sum(jnp.mean(e * e, axis=-1, keepdims=True), axis=0, keepdims=True)
        dy = e / D
        dxh = dy * gv
        dh = r * (dxh - xh * jnp.mean(dxh * xh, axis=-1, keepdims=True))
        dh_ref[...] = dh
        dhb_ref[...] = dh.astype(BF16)
        _acc_rows(i, dg_ref, jnp.sum(dy * xh, axis=0, keepdims=True))
        _acc_rows(i, loss_ref, jnp.broadcast_to(part, (8, LANES)))

    return _rows(name, body, L, tm,
                 [(h1, _rs(tm, D)), (o2, _rs(tm, D)), (g, _fs((1, D))), (tgt, _rs(tm, D))],
                 [(_sds((L, D), F32), _rs(tm, D)), (_sds((L, D), BF16), _rs(tm, D)),
                  (_sds((1, D), F32), _fs((1, D))), (_sds((8, LANES), F32), _fs((8, LANES)))])


def _glu_fn(y, g1):
    ya = jax.nn.gelu(y)
    return ya * jax.nn.sigmoid(g1)


def _glu_fwd(name, y, g1, tm):
    L, W = y.shape

    def body(y_ref, g_ref, o_ref):
        o_ref[...] = _glu_fn(y_ref[...], g_ref[...]).astype(BF16)

    return _rows(name, body, L, tm, [(y, _rs(tm, W)), (g1, _rs(tm, W))], [(_sds((L, W), BF16), _rs(tm, W))])[0]


def _glu_bwd(name, y, g1, dya2, tm):
    L, W = y.shape

    def body(y_ref, g_ref, d_ref, dy_ref, dg_ref):
        _, vjp = jax.vjp(_glu_fn, y_ref[...], g_ref[...])
        dy, dg = vjp(d_ref[...])
        dy_ref[...] = dy
        dg_ref[...] = dg.astype(BF16)

    return _rows(name, body, L, tm, [(y, _rs(tm, W)), (g1, _rs(tm, W)), (dya2, _rs(tm, W))],
                 [(_sds((L, W), F32), _rs(tm, W)), (_sds((L, W), BF16), _rs(tm, W))])


def _gelu_bwd(name, y, dy_direct, dya_g, proj, dskip, tm):
    L, W = y.shape

    def body(y_ref, dd_ref, dg_ref, u_ref, dyb_ref, dsk_ref):
        i = pl.program_id(0)
        _, vjp = jax.vjp(jax.nn.gelu, y_ref[...])
        dy = dd_ref[...] + vjp(dg_ref[...])[0]
        dyb_ref[...] = dy.astype(BF16)
        _acc_rows(i, dsk_ref, jnp.sum(dy * u_ref[...].astype(F32), axis=0, keepdims=True))

    del dskip
    return _rows(name, body, L, tm,
                 [(y, _rs(tm, W)), (dy_direct, _rs(tm, W)), (dya_g, _rs(tm, W)), (proj, _rs(tm, W, 0))],
                 [(_sds((L, W), BF16), _rs(tm, W)), (_sds((1, W), F32), _fs((1, W)))])


def _merge_fn(ma, mb, za, zb):
    return jax.nn.sigmoid(ma) * za + jax.nn.sigmoid(mb) * zb


def _merge_fwd(name, proj, cb_a, cb_b, za, zb, tm):
    L, D = za.shape

    def body(ma_ref, mb_ref, za_ref, zb_ref, o_ref):
        o_ref[...] = _merge_fn(ma_ref[...].astype(F32), mb_ref[...].astype(F32), za_ref[...], zb_ref[...]).astype(BF16)

    return _rows(name, body, L, tm,
                 [(proj, _rs(tm, D, cb_a)), (proj, _rs(tm, D, cb_b)), (za, _rs(tm, D)), (zb, _rs(tm, D))],
                 [(_sds((L, D), BF16), _rs(tm, D))])[0]


def _merge_bwd(name, proj, cb_a, cb_b, za, zb, dmerged, tm):
    L, D = za.shape

    def body(ma_ref, mb_ref, za_ref, zb_ref, d_ref, dza_ref, dzb_ref, dma_ref, dmb_ref):
        _, vjp = jax.vjp(_merge_fn, ma_ref[...].astype(F32), mb_ref[...].astype(F32), za_ref[...], zb_ref[...])
        dma, dmb, dza, dzb = vjp(d_ref[...])
        dza_ref[...] = dza.astype(BF16)
        dzb_ref[...] = dzb.astype(BF16)
        dma_ref[...] = dma.astype(BF16)
        dmb_ref[...] = dmb.astype(BF16)

    return _rows(name, body, L, tm,
                 [(proj, _rs(tm, D, cb_a)), (proj, _rs(tm, D, cb_b)), (za, _rs(tm, D)), (zb, _rs(tm, D)),
                  (dmerged, _rs(tm, D))],
                 [(_sds((L, D), BF16), _rs(tm, D)), (_sds((L, D), BF16), _rs(tm, D)),
                  (_sds((L, D), BF16), _rs(tm, D)), (_sds((L, D), BF16), _rs(tm, D))])


def _shift_down(x, k):
    row = lax.broadcasted_iota(jnp.int32, x.shape, 0)
    return jnp.where(row >= k, pltpu.roll(x, k, axis=0), 0.0)


def _shift_up(x, k):
    n = x.shape[0]
    row = lax.broadcasted_iota(jnp.int32, x.shape, 0)
    return jnp.where(row < n - k, pltpu.roll(x, n - k, axis=0), 0.0)


def _conv3(cv, w_ref, b_ref):
    return (w_ref[2:3, :] * cv + w_ref[1:2, :] * _shift_down(cv, 1) + w_ref[0:1, :] * _shift_down(cv, 2)
            + b_ref[...])


def _conv3_bwd(dcc, cv, w_ref):
    dcv = w_ref[2:3, :] * dcc + w_ref[1:2, :] * _shift_up(dcc, 1) + w_ref[0:1, :] * _shift_up(dcc, 2)
    dw = [jnp.sum(dcc * _shift_down(cv, 2), axis=0, keepdims=True),
          jnp.sum(dcc * _shift_down(cv, 1), axis=0, keepdims=True),
          jnp.sum(dcc * cv, axis=0, keepdims=True)]
    db = jnp.sum(dcc, axis=0, keepdims=True)
    return dcv, dw, db


def _store_rows(ref, rows):
    for r, val in enumerate(rows):
        ref[r:r + 1, :] = val


def _cols(name, body, ncb, ins, outs):
    return pl.pallas_call(
        body, name=name, grid=(ncb,),
        in_specs=[s for _, s in ins], out_specs=[s for _, s in outs],
        out_shape=[o for o, _ in outs], compiler_params=_params(1),
    )(*[a for a, _ in ins])


def _cb(L, w, off=0):
    return pl.BlockSpec((L, w), lambda j: (0, j + off))


def _convb_fwd(name, proj, cb_v, cb_gb, cb_gc, w, b):
    L = proj.shape[0]
    W = w.shape[1]
    c = LANES

    def body(v_ref, gb_ref, gc_ref, w_ref, b_ref, q_ref):
        cc = _conv3(gc_ref[...].astype(F32) * v_ref[...].astype(F32), w_ref, b_ref)
        q_ref[...] = (gb_ref[...].astype(F32) * cc).astype(BF16)

    return _cols(name, body, W // c,
                 [(proj, _cb(L, c, cb_v)), (proj, _cb(L, c, cb_gb)), (proj, _cb(L, c, cb_gc)),
                  (w, _cb(3, c)), (b, _cb(1, c))],
                 [(_sds((L, W), BF16), _cb(L, c))])[0]


def _convb_bwd(name, proj, cb_v, cb_gb, cb_gc, w, b, dq):
    L = proj.shape[0]
    W = w.shape[1]
    c = LANES

    def body(v_ref, gb_ref, gc_ref, w_ref, b_ref, dq_ref, dv_ref, dgb_ref, dgc_ref, dw_ref, db_ref):
        v, gc = v_ref[...].astype(F32), gc_ref[...].astype(F32)
        cv = gc * v
        cc = _conv3(cv, w_ref, b_ref)
        dq = dq_ref[...]
        dgb_ref[...] = (dq * cc).astype(BF16)
        dcv, dw, db = _conv3_bwd(dq * gb_ref[...].astype(F32), cv, w_ref)
        dv_ref[...] = (dcv * gc).astype(BF16)
        dgc_ref[...] = (dcv * v).astype(BF16)
        _store_rows(dw_ref, dw)
        db_ref[...] = db

    return _cols(name, body, W // c,
                 [(proj, _cb(L, c, cb_v)), (proj, _cb(L, c, cb_gb)), (proj, _cb(L, c, cb_gc)),
                  (w, _cb(3, c)), (b, _cb(1, c)), (dq, _cb(L, c))],
                 [(_sds((L, W), BF16), _cb(L, c)), (_sds((L, W), BF16), _cb(L, c)), (_sds((L, W), BF16), _cb(L, c)),
                  (_sds((3, W), F32), _cb(3, c)), (_sds((1, W), F32), _cb(1, c))])


HALO = 16


def _ffn_tiles(L, Fw):
    tr = _tile(L, 256, HALO)
    tc = _tile(Fw, 1408)
    return tr, tc, Fw // tc, L // tr, tr // HALO


def _ffn_fwd(name, hh, w, b):
    L = hh.shape[0]
    Fw = w.shape[1]
    tr, tc, ncb, nrt, rpt = _ffn_tiles(L, Fw)

    def body(a_ref, p_ref, h2_ref, w_ref, b_ref, f_ref):
        first = pl.program_id(1) == 0
        for c0 in range(0, tc, LANES):
            cs = slice(c0, c0 + LANES)
            prev = jnp.where(first, 0.0, p_ref[:, cs].astype(F32))
            x = jnp.concatenate([prev, a_ref[:, cs].astype(F32)], axis=0)
            n = x.shape[0]
            a = (w_ref[2:3, cs] * x + w_ref[1:2, cs] * pltpu.roll(x, 1, axis=0) + w_ref[0:1, cs] * pltpu.roll(x, 2, axis=0)
                 + b_ref[:, cs])[HALO:n]
            f_ref[:, cs] = (jax.nn.gelu(a) * h2_ref[:, cs].astype(F32)).astype(BF16)

    main = pl.BlockSpec((tr, tc), lambda j, i: (i, j))
    return pl.pallas_call(
        body, name=name, grid=(ncb, nrt),
        in_specs=[main, pl.BlockSpec((HALO, tc), lambda j, i: (jnp.maximum(i * rpt - 1, 0), j)),
                  pl.BlockSpec((tr, tc), lambda j, i: (i, j + ncb)),
                  pl.BlockSpec((3, tc), lambda j, i: (0, j)), pl.BlockSpec((1, tc), lambda j, i: (0, j))],
        out_specs=main, out_shape=_sds((L, Fw), BF16), compiler_params=_params(2),
    )(hh, hh, hh, w, b)


def _ffn_bwd(name, hh, w, b, df):
    L = hh.shape[0]
    Fw = w.shape[1]
    tr, tc, ncb, nrt, rpt = _ffn_tiles(L, Fw)

    def body(a_ref, ap_ref, an_ref, h2_ref, h2n_ref, df_ref, dfn_ref, w_ref, b_ref, dhh_ref, dw_ref, db_ref):
        i = pl.program_id(1)
        first, last = i == 0, i == nrt - 1
        for c0 in range(0, tc, LANES):
            cs = slice(c0, c0 + LANES)
            zero = jnp.zeros((HALO, LANES), F32)
            h1 = jnp.concatenate([jnp.where(first, 0.0, ap_ref[:, cs].astype(F32)), a_ref[:, cs].astype(F32),
                                  an_ref[:, cs].astype(F32)], axis=0)
            h2 = jnp.concatenate([zero, h2_ref[:, cs].astype(F32), h2n_ref[:, cs].astype(F32)], axis=0)
            d = jnp.concatenate([zero, df_ref[:, cs].astype(F32), jnp.where(last, 0.0, dfn_ref[:, cs].astype(F32))], axis=0)
            n = h1.shape[0]
            s1, s2 = pltpu.roll(h1, 1, axis=0), pltpu.roll(h1, 2, axis=0)
            a = w_ref[2:3, cs] * h1 + w_ref[1:2, cs] * s1 + w_ref[0:1, cs] * s2 + b_ref[:, cs]
            ga, vjp = jax.vjp(jax.nn.gelu, a)
            da = vjp(d * h2)[0]
            dh1 = w_ref[2:3, cs] * da + w_ref[1:2, cs] * pltpu.roll(da, n - 1, axis=0) + w_ref[0:1, cs] * pltpu.roll(da, n - 2, axis=0)
            dhh_ref[0, :, cs] = dh1[HALO:HALO + tr].astype(BF16)
            dhh_ref[1, :, cs] = (d * ga)[HALO:HALO + tr].astype(BF16)
            dam = da[HALO:HALO + tr]
            rows = [jnp.sum(dam * s2[HALO:HALO + tr], axis=0, keepdims=True),
                    jnp.sum(dam * s1[HALO:HALO + tr], axis=0, keepdims=True),
                    jnp.sum(dam * h1[HALO:HALO + tr], axis=0, keepdims=True),
                    jnp.sum(dam, axis=0, keepdims=True)]

            @pl.when(first)
            def _():
                for r in range(3):
                    dw_ref[r:r + 1, cs] = rows[r]
                db_ref[:, cs] = rows[3]

            @pl.when(i > 0)
            def _():
                for r in range(3):
                    dw_ref[r:r + 1, cs] += rows[r]
                db_ref[:, cs] += rows[3]

    def spec(col_off, kind):
        if kind == "main":
            return pl.BlockSpec((tr, tc), lambda j, i: (i, j + col_off))
        if kind == "prev":
            return pl.BlockSpec((HALO, tc), lambda j, i: (jnp.maximum(i * rpt - 1, 0), j + col_off))
        return pl.BlockSpec((HALO, tc), lambda j, i: (jnp.minimum((i + 1) * rpt, nrt * rpt - 1), j + col_off))

    return pl.pallas_call(
        body, name=name, grid=(ncb, nrt),
        in_specs=[spec(0, "main"), spec(0, "prev"), spec(0, "next"), spec(ncb, "main"), spec(ncb, "next"),
                  spec(0, "main"), spec(0, "next"),
                  pl.BlockSpec((3, tc), lambda j, i: (0, j)), pl.BlockSpec((1, tc), lambda j, i: (0, j))],
        out_specs=[pl.BlockSpec((2, tr, tc), lambda j, i: (0, i, j)),
                   pl.BlockSpec((3, tc), lambda j, i: (0, j)), pl.BlockSpec((1, tc), lambda j, i: (0, j))],
        out_shape=[_sds((2, L, Fw), BF16), _sds((3, Fw), F32), _sds((1, Fw), F32)], compiler_params=_params(2),
    )(hh, hh, hh, hh, hh, df, df, w, b)


def _prep_fn(ar, ai, ldt, brt, bit):
    dt = jnp.exp(ldt)
    mag = jnp.exp(dt * ar)
    are = mag * jnp.cos(dt * ai)
    aim = mag * jnp.sin(dt * ai)
    nr = are - 1.0
    ni = aim
    den = ar * ar + ai * ai
    fr = (nr * ar + ni * ai) / den
    fi = (ni * ar - nr * ai) / den
    return are, aim, fr * brt - fi * bit, fr * bit + fi * brt


def _prep_fwd(name, ar, ai, ldt, brt, bit):
    def body(ar_ref, ai_ref, l_ref, br_ref, bi_ref, o1, o2, o3, o4):
        o1[...], o2[...], o3[...], o4[...] = _prep_fn(ar_ref[...], ai_ref[...], l_ref[...], br_ref[...], bi_ref[...])

    return pl.pallas_call(body, name=name,
                          out_shape=[_sds(ar.shape, F32), _sds(ar.shape, F32), _sds(brt.shape, F32), _sds(brt.shape, F32)],
                          )(ar, ai, ldt, brt, bit)


def _prep_bwd(name, ar, ai, ldt, brt, bit, g1, g2, g3, g4):
    def body(ar_ref, ai_ref, l_ref, br_ref, bi_ref, g1_ref, g2_ref, g3_ref, g4_ref, o1, o2, o3, o4, o5):
        _, vjp = jax.vjp(_prep_fn, ar_ref[...], ai_ref[...], l_ref[...], br_ref[...], bi_ref[...])
        o1[...], o2[...], o3[...], o4[...], o5[...] = vjp((g1_ref[...], g2_ref[...], g3_ref[...], g4_ref[...]))

    return pl.pallas_call(body, name=name,
                          out_shape=[_sds(ar.shape, F32)] * 3 + [_sds(brt.shape, F32)] * 2,
                          )(ar, ai, ldt, brt, bit, g1, g2, g3, g4)


def _ssm_in(name, src, m1, m2, tm, dep=None):
    L = src.shape[0]
    nb = m1.shape[0]

    deps = [] if dep is None else [dep]

    def body(s_ref, m1_ref, m2_ref, *rest):
        o1_ref, o2_ref = rest[len(deps):]
        u = s_ref[...].astype(BF16)
        r1 = jnp.dot(u, m1_ref[...], preferred_element_type=F32)
        r2 = jnp.dot(u, m2_ref[...], preferred_element_type=F32)
        for q in range(SLAB):
            o1_ref[q] = r1[:, q * LANES:(q + 1) * LANES].astype(BF16)
            o2_ref[q] = r2[:, q * LANES:(q + 1) * LANES].astype(BF16)

    ms = pl.BlockSpec((None, LANES, SLAB * LANES), lambda i, j: (j, 0, 0))
    os_ = pl.BlockSpec((SLAB, tm, LANES), lambda i, j: (j, i, 0))
    return pl.pallas_call(
        body, name=name, grid=(L // tm, nb),
        in_specs=[pl.BlockSpec((tm, LANES), lambda i, j: (i, j)), ms, ms] + [pl.BlockSpec(memory_space=pl.ANY)] * len(deps),
        out_specs=[os_, os_],
        out_shape=[_sds((SLAB * nb, L, LANES), BF16)] * 2, compiler_params=_params(2),
    )(src, m1, m2, *deps)


def _ssm_out(name, x1, x2, m1, m2, aux, dvec, tm, post=None):
    L = x1.shape[1]
    nb = m1.shape[0]

    def body(x1_ref, x2_ref, m1_ref, m2_ref, a_ref, d_ref, o_ref, *rest):
        a1 = jnp.concatenate([x1_ref[q] for q in range(SLAB)], axis=1).astype(BF16)
        a2 = jnp.concatenate([x2_ref[q] for q in range(SLAB)], axis=1).astype(BF16)
        y = (jnp.dot(a1, m1_ref[...], preferred_element_type=F32) + jnp.dot(a2, m2_ref[...], preferred_element_type=F32)
             + d_ref[...] * a_ref[...].astype(F32))
        o_ref[...] = y
        if post is not None:
            rest[0][...] = post(y).astype(BF16)

    xs = pl.BlockSpec((SLAB, tm, LANES), lambda i, j: (j, i, 0))
    ms = pl.BlockSpec((None, SLAB * LANES, LANES), lambda i, j: (j, 0, 0))
    cs = pl.BlockSpec((tm, LANES), lambda i, j: (i, j))
    W = nb * LANES
    outs, ospecs = [_sds((L, W), F32)], [cs]
    if post is not None:
        outs.append(_sds((L, W), BF16))
        ospecs.append(cs)
    return pl.pallas_call(
        body, name=name, grid=(L // tm, nb),
        in_specs=[xs, xs, ms, ms, cs, pl.BlockSpec((1, LANES), lambda i, j: (0, j))], out_specs=ospecs,
        out_shape=outs, compiler_params=_params(2),
    )(x1, x2, m1, m2, aux, dvec)


def _ssm_dw(name, src, x1, x2, tk):
    L = src.shape[0]
    nb = x1.shape[0] // SLAB
    dn = (((0,), (0,)), ((), ()))

    def body(s_ref, x1_ref, x2_ref, o1_ref, o2_ref):
        k = pl.program_id(1)
        s = s_ref[...].astype(BF16)
        a1 = jnp.concatenate([x1_ref[q] for q in range(SLAB)], axis=1).astype(BF16)
        a2 = jnp.concatenate([x2_ref[q] for q in range(SLAB)], axis=1).astype(BF16)
        _acc_rows(k, o1_ref, lax.dot_general(s, a1, dn, preferred_element_type=F32))
        _acc_rows(k, o2_ref, lax.dot_general(s, a2, dn, preferred_element_type=F32))

    xs = pl.BlockSpec((SLAB, tk, LANES), lambda j, k: (j, k, 0))
    os_ = pl.BlockSpec((None, LANES, SLAB * LANES), lambda j, k: (j, 0, 0))
    return pl.pallas_call(
        body, name=name, grid=(nb, L // tk),
        in_specs=[pl.BlockSpec((tk, LANES), lambda j, k: (k, j)), xs, xs], out_specs=[os_, os_],
        out_shape=[_sds((nb, LANES, SLAB * LANES), F32)] * 2, compiler_params=_params(2),
    )(src, x1, x2)


def _scan(name, b_re, b_im, a_re, a_im, xs=None):
    reverse = xs is not None
    ns, L, _ = b_re.shape
    ng = ns // 8
    tc = min(LANES, L)
    pitch = tc + 8
    nt = L // tc
    n_in = 4 if reverse else 2

    def body(*refs):
        ins = refs[:n_in]
        ar_ref, ai_ref = refs[n_in], refs[n_in + 1]
        o_re, o_im = refs[n_in + 2], refs[n_in + 3]
        k = n_in + 4
        if reverse:
            da_re, da_im = refs[k], refs[k + 1]
            k += 2
        stage = refs[k:k + n_in]
        out_re, out_im, st_re, st_im = refs[k + n_in:k + n_in + 4]
        acc = refs[k + n_in + 4:]
        i = pl.program_id(0)

        @pl.when(i == 0)
        def _():
            st_re[...] = jnp.zeros(st_re.shape, F32)
            st_im[...] = jnp.zeros(st_im.shape, F32)
            for r in acc:
                r[...] = jnp.zeros(r.shape, F32)

        for s in range(ns):
            for src, dst in zip(ins, stage):
                dst[pl.ds(s * pitch, tc), :] = src[s].astype(F32)

        a_r = [ar_ref[g] for g in range(ng)]
        a_i = [ai_ref[g] for g in range(ng)]

        def step(tt, carry):
            t = (tc - 1 - tt) if reverse else tt
            new = []
            for g in range(ng):
                rows = pl.ds(g * 8 * pitch + t, 8, stride=pitch)
                cr, ci = carry[2 * g], carry[2 * g + 1]
                br, bi = stage[0][rows, :], stage[1][rows, :]
                if reverse:
                    xr, xi = stage[2][rows, :], stage[3][rows, :]
                    acc[0][g] += xr * cr + xi * ci
                    acc[1][g] += xr * ci - xi * cr
                    nr = a_r[g] * cr + a_i[g] * ci + br
                    ni = a_r[g] * ci - a_i[g] * cr + bi
                else:
                    nr = a_r[g] * cr - a_i[g] * ci + br
                    ni = a_r[g] * ci + a_i[g] * cr + bi
                out_re[rows, :] = nr
                out_im[rows, :] = ni
                new += [nr, ni]
            return tuple(new)

        init = []
        for g in range(ng):
            init += [st_re[g], st_im[g]]
        fin = lax.fori_loop(0, tc, step, tuple(init), unroll=2)
        for g in range(ng):
            st_re[g] = fin[2 * g]
            st_im[g] = fin[2 * g + 1]
        for s in range(ns):
            o_re[s] = out_re[pl.ds(s * pitch, tc), :].astype(BF16)
            o_im[s] = out_im[pl.ds(s * pitch, tc), :].astype(BF16)
        if reverse:
            da_re[...] = acc[0][...]
            da_im[...] = acc[1][...]

    tmap = (lambda i: (0, nt - 1 - i, 0)) if reverse else (lambda i: (0, i, 0))
    bs = pl.BlockSpec((ns, tc, LANES), tmap)
    as_ = pl.BlockSpec((ng, 8, LANES), lambda i: (0, 0, 0))
    ins = [b_re, b_im] + (list(xs) if reverse else [])
    out_shape = [_sds((ns, L, LANES), BF16)] * 2 + ([_sds((ng, 8, LANES), F32)] * 2 if reverse else [])
    out_specs = [bs, bs] + ([as_, as_] if reverse else [])
    scratch = [pltpu.VMEM((ns * pitch, LANES), F32)] * (n_in + 2) + [pltpu.VMEM((ng, 8, LANES), F32)] * (4 if reverse else 2)
    return pl.pallas_call(
        body, name=name, grid=(nt,), in_specs=[bs] * n_in + [as_, as_], out_specs=out_specs,
        out_shape=out_shape, scratch_shapes=scratch, compiler_params=_params(1),
    )(*ins, a_re, a_im)


def _scan_steps(tc, pitch, ng, reverse, a_r, a_i, stage_b, stage_x, out_re, out_im, st_re, st_im, acc):
    def step(tt, carry):
        t = (tc - 1 - tt) if reverse else tt
        new = []
        for g in range(ng):
            rows = pl.ds(g * 8 * pitch + t, 8, stride=pitch)
            cr, ci = carry[2 * g], carry[2 * g + 1]
            br, bi = stage_b[0][rows, :], stage_b[1][rows, :]
            if reverse:
                xr, xi = stage_x[0][rows, :], stage_x[1][rows, :]
                acc[0][g] += xr * cr + xi * ci
                acc[1][g] += xr * ci - xi * cr
                nr = a_r[g] * cr + a_i[g] * ci + br
                ni = a_r[g] * ci - a_i[g] * cr + bi
            else:
                nr = a_r[g] * cr - a_i[g] * ci + br
                ni = a_r[g] * ci + a_i[g] * cr + bi
            out_re[rows, :] = nr
            out_im[rows, :] = ni
            new += [nr, ni]
        return tuple(new)

    init = []
    for g in range(ng):
        init += [st_re[g], st_im[g]]
    fin = lax.fori_loop(0, tc, step, tuple(init), unroll=2)
    for g in range(ng):
        st_re[g] = fin[2 * g]
        st_im[g] = fin[2 * g + 1]


def _s5_fwd(name, proj, bm_re, bm_im, cm_re, cm_im, dskip, a_re, a_im):
    L = proj.shape[0]
    nb = bm_re.shape[0]
    ns, W = SLAB * nb, nb * LANES
    ng = ns // 8
    tc = min(LANES, L)
    pitch = tc + 8
    wide = SLAB * LANES

    def body(u_ref, bre_ref, bim_ref, cre_ref, cim_ref, d_ref, ar_ref, ai_ref, xr_ref, xi_ref, y_ref, ya_ref,
             sb_re, sb_im, out_re, out_im, st_re, st_im):
        @pl.when(pl.program_id(0) == 0)
        def _():
            st_re[...] = jnp.zeros(st_re.shape, F32)
            st_im[...] = jnp.zeros(st_im.shape, F32)

        for j in range(nb):
            ub = u_ref[:, j * LANES:(j + 1) * LANES]
            r1 = jnp.dot(ub, bre_ref[j], preferred_element_type=F32)
            r2 = jnp.dot(ub, bim_ref[j], preferred_element_type=F32)
            for q in range(SLAB):
                sb_re[pl.ds((SLAB * j + q) * pitch, tc), :] = r1[:, q * LANES:(q + 1) * LANES]
                sb_im[pl.ds((SLAB * j + q) * pitch, tc), :] = r2[:, q * LANES:(q + 1) * LANES]
        a_r = [ar_ref[g] for g in range(ng)]
        a_i = [ai_ref[g] for g in range(ng)]
        _scan_steps(tc, pitch, ng, False, a_r, a_i, (sb_re, sb_im), None, out_re, out_im, st_re, st_im, None)
        for j in range(nb):
            x1 = [out_re[pl.ds((SLAB * j + q) * pitch, tc), :].astype(BF16) for q in range(SLAB)]
            x2 = [out_im[pl.ds((SLAB * j + q) * pitch, tc), :].astype(BF16) for q in range(SLAB)]
            for q in range(SLAB):
                xr_ref[SLAB * j + q] = x1[q]
                xi_ref[SLAB * j + q] = x2[q]
            cols = slice(j * LANES, (j + 1) * LANES)
            y = (jnp.dot(jnp.concatenate(x1, axis=1), cre_ref[j], preferred_element_type=F32)
                 + jnp.dot(jnp.concatenate(x2, axis=1), cim_ref[j], preferred_element_type=F32)
                 + d_ref[:, cols] * u_ref[:, cols].astype(F32))
            y_ref[:, cols] = y
            ya_ref[:, cols] = jax.nn.gelu(y).astype(BF16)

    full3 = lambda s: pl.BlockSpec(s, lambda i: (0, 0, 0))
    xs = pl.BlockSpec((ns, tc, LANES), lambda i: (0, i, 0))
    rows = pl.BlockSpec((tc, W), lambda i: (i, 0))
    return pl.pallas_call(
        body, name=name, grid=(L // tc,),
        in_specs=[rows, full3((nb, LANES, wide)), full3((nb, LANES, wide)), full3((nb, wide, LANES)),
                  full3((nb, wide, LANES)), pl.BlockSpec((1, W), lambda i: (0, 0)), full3((ng, 8, LANES)), full3((ng, 8, LANES))],
        out_specs=[xs, xs, rows, rows],
        out_shape=[_sds((ns, L, LANES), BF16)] * 2 + [_sds((L, W), F32), _sds((L, W), BF16)],
        scratch_shapes=[pltpu.VMEM((ns * pitch, LANES), F32)] * 4 + [pltpu.VMEM((ng, 8, LANES), F32)] * 2,
        compiler_params=_params(1),
    )(proj, bm_re, bm_im, cm_re, cm_im, dskip, a_re, a_im)


def _s5_bwd(name, dyb, proj, xs_re, xs_im, cmt_re, cmt_im, bmt_re, bmt_im, dskip, a_re, a_im):
    L = dyb.shape[0]
    nb = cmt_re.shape[0]
    ns, W = SLAB * nb, nb * LANES
    ng = ns // 8
    tc = min(LANES, L)
    pitch = tc + 8
    nt = L // tc
    wide = SLAB * LANES
    dn = (((0,), (0,)), ((), ()))

    def body(dy_ref, u_ref, xr_ref, xi_ref, cre_ref, cim_ref, bre_ref, bim_ref, d_ref, ar_ref, ai_ref,
             du_ref, gbr_ref, gbi_ref, gcr_ref, gci_ref, dar_ref, dai_ref,
             sd_re, sd_im, sx_re, sx_im, out_re, out_im, st_re, st_im, acc_re, acc_im):
        first = pl.program_id(0) == 0

        @pl.when(first)
        def _():
            for r in (st_re, st_im, acc_re, acc_im):
                r[...] = jnp.zeros(r.shape, F32)
            for r in (gbr_ref, gbi_ref, gcr_ref, gci_ref):
                r[...] = jnp.zeros(r.shape, F32)

        for j in range(nb):
            dyj = dy_ref[:, j * LANES:(j + 1) * LANES]
            r1 = jnp.dot(dyj, cre_ref[j], preferred_element_type=F32)
            r2 = jnp.dot(dyj, cim_ref[j], preferred_element_type=F32)
            for q in range(SLAB):
                s = SLAB * j + q
                sd_re[pl.ds(s * pitch, tc), :] = r1[:, q * LANES:(q + 1) * LANES]
                sd_im[pl.ds(s * pitch, tc), :] = r2[:, q * LANES:(q + 1) * LANES]
                sx_re[pl.ds(s * pitch, tc), :] = xr_ref[s].astype(F32)
                sx_im[pl.ds(s * pitch, tc), :] = xi_ref[s].astype(F32)
        a_r = [ar_ref[g] for g in range(ng)]
        a_i = [ai_ref[g] for g in range(ng)]
        _scan_steps(tc, pitch, ng, True, a_r, a_i, (sd_re, sd_im), (sx_re, sx_im), out_re, out_im, st_re, st_im,
                    (acc_re, acc_im))
        for j in range(nb):
            cols = slice(j * LANES, (j + 1) * LANES)
            l1 = jnp.concatenate([out_re[pl.ds((SLAB * j + q) * pitch, tc), :] for q in range(SLAB)], axis=1).astype(BF16)
            l2 = jnp.concatenate([out_im[pl.ds((SLAB * j + q) * pitch, tc), :] for q in range(SLAB)], axis=1).astype(BF16)
            dyj = dy_ref[:, cols]
            du = (jnp.dot(l1, bre_ref[j], preferred_element_type=F32) + jnp.dot(l2, bim_ref[j], preferred_element_type=F32)
                  + d_ref[:, cols] * dyj.astype(F32))
            du_ref[:, cols] = du.astype(BF16)
            uj = u_ref[:, cols]
            gbr_ref[j] += lax.dot_general(uj, l1, dn, preferred_element_type=F32)
            gbi_ref[j] += lax.dot_general(uj, l2, dn, preferred_element_type=F32)
            x1 = jnp.concatenate([xr_ref[SLAB * j + q] for q in range(SLAB)], axis=1)
            x2 = jnp.concatenate([xi_ref[SLAB * j + q] for q in range(SLAB)], axis=1)
            gcr_ref[j] += lax.dot_general(dyj, x1, dn, preferred_element_type=F32)
            gci_ref[j] += lax.dot_general(dyj, x2, dn, preferred_element_type=F32)
        dar_ref[...] = acc_re[...]
        dai_ref[...] = acc_im[...]

    full3 = lambda s: pl.BlockSpec(s, lambda i: (0, 0, 0))
    xs = pl.BlockSpec((ns, tc, LANES), lambda i: (0, nt - 1 - i, 0))
    rows = pl.BlockSpec((tc, W), lambda i: (nt - 1 - i, 0))
    mat_a, mat_b = full3((nb, LANES, wide)), full3((nb, wide, LANES))
    vec = full3((ng, 8, LANES))
    return pl.pallas_call(
        body, name=name, grid=(nt,),
        in_specs=[rows, rows, xs, xs, mat_a, mat_a, mat_b, mat_b, pl.BlockSpec((1, W), lambda i: (0, 0)), vec, vec],
        out_specs=[rows, mat_a, mat_a, mat_a, mat_a, vec, vec],
        out_shape=[_sds((L, W), BF16)] + [_sds((nb, LANES, wide), F32)] * 4 + [_sds((ng, 8, LANES), F32)] * 2,
        scratch_shapes=[pltpu.VMEM((ns * pitch, LANES), F32)] * 6 + [pltpu.VMEM((ng, 8, LANES), F32)] * 4,
        compiler_params=_params(1),
    )(dyb, proj, xs_re, xs_im, cmt_re, cmt_im, bmt_re, bmt_im, dskip, a_re, a_im)


def _peer(k):
    x, y, c = lax.axis_index("x"), lax.axis_index("y"), lax.axis_index("c")
    px = 1 - x if (k >> 2) & 1 else x
    py = 1 - y if (k >> 1) & 1 else y
    pc = 1 - c if k & 1 else c
    return (px, py, pc), 4 * px + 2 * py + pc


def _window(ref, kind, idx, n):
    if kind == "col":
        w = ref.shape[1] // n
        return ref.at[:, pl.ds(pl.multiple_of(idx * w, LANES), w)]
    r = ref.shape[0] // n
    return ref.at[pl.ds(pl.multiple_of(idx * r, 8), r), :]


def _all_gather(name, shards, kinds):
    n = len(shards)
    fulls = []
    for s, kind in zip(shards, kinds):
        fulls.append(_sds((s.shape[0], s.shape[1] * N_DEV) if kind == "col" else (s.shape[0] * N_DEV, s.shape[1]), s.dtype))

    def body(*refs):
        src, dst = refs[:n], refs[n:2 * n]
        send, recv, loc = refs[2 * n:]
        me = 4 * lax.axis_index("x") + 2 * lax.axis_index("y") + lax.axis_index("c")
        copies = []
        for a in range(n):
            own = pltpu.make_async_copy(src[a], _window(dst[a], kinds[a], me, N_DEV), loc.at[a])
            own.start()
            copies.append(own)
        sends = []
        for k in range(1, N_DEV):
            dev, _ = _peer(k)
            for a in range(n):
                cp = pltpu.make_async_remote_copy(
                    src_ref=src[a], dst_ref=_window(dst[a], kinds[a], me, N_DEV),
                    send_sem=send.at[a * N_DEV + k], recv_sem=recv.at[a * N_DEV + k],
                    device_id=dev, device_id_type=MESH)
                cp.start()
                sends.append(cp)
        for k in range(1, N_DEV):
            dev, pidx = _peer(k)
            for a in range(n):
                pltpu.make_async_remote_copy(
                    src_ref=src[a], dst_ref=_window(dst[a], kinds[a], pidx, N_DEV),
                    send_sem=send.at[a * N_DEV + k], recv_sem=recv.at[a * N_DEV + k],
                    device_id=dev, device_id_type=MESH).wait_recv()
        for cp in sends:
            cp.wait_send()
        for cp in copies:
            cp.wait()

    any_ = pl.BlockSpec(memory_space=pl.ANY)
    return pl.pallas_call(
        body, name=name, in_specs=[any_] * n, out_specs=[any_] * n, out_shape=fulls,
        scratch_shapes=[pltpu.SemaphoreType.DMA((n * N_DEV,)), pltpu.SemaphoreType.DMA((n * N_DEV,)),
                        pltpu.SemaphoreType.DMA((n,))],
        compiler_params=pltpu.CompilerParams(has_side_effects=True),
    )(*shards)


_HBM = pl.BlockSpec(memory_space=pltpu.HBM)
_SEM = pl.BlockSpec(memory_space=pltpu.SEMAPHORE)
_ANY = pl.BlockSpec(memory_space=pl.ANY)
_EFFECT = pltpu.SideEffectType.DATAFLOW_SIDE_EFFECTING


def _xfer_refs(mode, kinds, a, src, dst, me, pidx):
    if mode == "gather":
        return src[a], _window(dst[a], kinds[a], me, N_DEV), _window(dst[a], kinds[a], pidx, N_DEV)
    return _window(src[a], kinds[a], pidx, N_DEV), dst[a].at[me], dst[a].at[pidx]


def _xfer_out_shapes(mode, arrs, kinds):
    outs = []
    for s, kind in zip(arrs, kinds):
        if mode == "gather":
            outs.append((s.shape[0], s.shape[1] * N_DEV) if kind == "col" else (s.shape[0] * N_DEV, s.shape[1]))
        else:
            outs.append((N_DEV,) + ((s.shape[0], s.shape[1] // N_DEV) if kind == "col" else (s.shape[0] // N_DEV, s.shape[1])))
    return outs


def _xfer_start(name, mode, arrs, kinds, after):
    n = len(arrs)
    shapes = _xfer_out_shapes(mode, arrs, kinds)

    def body(*refs):
        src, dst = refs[:n], refs[n:2 * n]
        send, recv = refs[2 * n + 1], refs[2 * n + 2]
        token, loc = refs[2 * n + 3 + 2 * n], refs[2 * n + 4 + 2 * n]
        me = 4 * lax.axis_index("x") + 2 * lax.axis_index("y") + lax.axis_index("c")
        own = []
        for a in range(n):
            s, _, d = _xfer_refs(mode, kinds, a, src, dst, me, me)
            own.append(pltpu.make_async_copy(s, d, loc.at[a]))
            own[-1].start()
        for cp in own:
            cp.wait()
        for k in range(1, N_DEV):
            dev, pidx = _peer(k)
            for a in range(n):
                s, d, _ = _xfer_refs(mode, kinds, a, src, dst, me, pidx)
                pltpu.make_async_remote_copy(src_ref=s, dst_ref=d, send_sem=send.at[a * N_DEV + k],
                                             recv_sem=recv.at[a * N_DEV + k], device_id=dev, device_id_type=MESH).start()
        token[...] = jnp.zeros(token.shape, F32)

    lands = [pltpu.with_memory_space_constraint(lax.empty(shp, s.dtype), pltpu.HBM) for shp, s in zip(shapes, arrs)]
    srcs = [pltpu.with_memory_space_constraint(s, pltpu.HBM) for s in arrs]
    res = pl.pallas_call(
        body, name=name,
        in_specs=[_HBM] * (2 * n) + [_ANY],
        out_specs=[_SEM, _SEM] + [_HBM] * (2 * n) + [pl.BlockSpec(memory_space=pltpu.VMEM)],
        out_shape=[pltpu.SemaphoreType.DMA((n * N_DEV,)), pltpu.SemaphoreType.DMA((n * N_DEV,))]
        + [pltpu.HBM(s.shape, s.dtype) for s in arrs] + [pltpu.HBM(shp, s.dtype) for shp, s in zip(shapes, arrs)]
        + [_sds((8, LANES), F32)],
        input_output_aliases={i: 2 + i for i in range(2 * n)},
        scratch_shapes=[pltpu.SemaphoreType.DMA((n,))],
        compiler_params=pltpu.CompilerParams(has_side_effects=_EFFECT),
    )(*srcs, *lands, after)
    return dict(mode=mode, kinds=kinds, n=n, send=res[0], recv=res[1], srcs=res[2:2 + n], lands=res[2 + n:2 + 2 * n]), res[-1]


def _xfer_wait(name, st, after):
    n, mode, kinds = st["n"], st["mode"], st["kinds"]

    def body(*refs):
        src, dst = refs[:n], refs[n:2 * n]
        send, recv = refs[2 * n], refs[2 * n + 1]
        me = 4 * lax.axis_index("x") + 2 * lax.axis_index("y") + lax.axis_index("c")
        for k in range(1, N_DEV):
            dev, pidx = _peer(k)
            for a in range(n):
                s, d, land = _xfer_refs(mode, kinds, a, src, dst, me, pidx)
                cp = pltpu.make_async_remote_copy(src_ref=s, dst_ref=land, send_sem=send.at[a * N_DEV + k],
                                                  recv_sem=recv.at[a * N_DEV + k], device_id=dev, device_id_type=MESH)
                cp.wait_send()
                cp.wait_recv()

    res = pl.pallas_call(
        body, name=name,
        in_specs=[_HBM] * (2 * n) + [_SEM, _SEM, _ANY],
        out_specs=[_HBM] * (2 * n),
        out_shape=[pltpu.HBM(s.shape, s.dtype) for s in st["srcs"]] + [pltpu.HBM(s.shape, s.dtype) for s in st["lands"]],
        input_output_aliases={i: i for i in range(2 * n)},
        compiler_params=pltpu.CompilerParams(has_side_effects=_EFFECT),
    )(*st["srcs"], *st["lands"], st["send"], st["recv"], after)
    return list(res[n:])


def _sc_xfer(name, mode, arrs, kinds, collective_id):
    n = len(arrs)
    shapes = _xfer_out_shapes(mode, arrs, kinds)
    hbm = pltpu.MemorySpace.HBM
    src = [jax.new_ref(a, memory_space=hbm) for a in arrs]
    dst = [jax.empty_ref(_sds(shp, a.dtype), memory_space=hbm) for shp, a in zip(shapes, arrs)]

    @pl.kernel(mesh=plsc.ScalarSubcoreMesh(axis_name="seq", num_cores=1), name=name,
               scratch_types=(pltpu.SemaphoreType.DMA((n * N_DEV,)), pltpu.SemaphoreType.DMA((n * N_DEV,)),
                              pltpu.SemaphoreType.DMA((n,))),
               compiler_params=pltpu.CompilerParams(collective_id=collective_id))
    def launch(send, recv, loc):
        barrier = pltpu.get_barrier_semaphore()
        for k in range(1, N_DEV):
            pl.semaphore_signal(barrier, inc=1, device_id=_peer(k)[0], device_id_type=MESH)
        pl.semaphore_wait(barrier, N_DEV - 1)
        me = 4 * lax.axis_index("x") + 2 * lax.axis_index("y") + lax.axis_index("c")
        own, sends = [], []
        for a in range(n):
            s, _, d = _xfer_refs(mode, kinds, a, src, dst, me, me)
            own.append(pltpu.make_async_copy(s, d, loc.at[a]))
            own[-1].start()
        for k in range(1, N_DEV):
            dev, pidx = _peer(k)
            for a in range(n):
                s, d, _ = _xfer_refs(mode, kinds, a, src, dst, me, pidx)
                sends.append(pltpu.make_async_remote_copy(src_ref=s, dst_ref=d, send_sem=send.at[a * N_DEV + k],
                                                          recv_sem=recv.at[a * N_DEV + k], device_id=dev, device_id_type=MESH))
                sends[-1].start()
        for cp in own:
            cp.wait()
        for k in range(1, N_DEV):
            dev, pidx = _peer(k)
            for a in range(n):
                s, _, land = _xfer_refs(mode, kinds, a, src, dst, me, pidx)
                pltpu.make_async_remote_copy(src_ref=s, dst_ref=land, send_sem=send.at[a * N_DEV + k],
                                             recv_sem=recv.at[a * N_DEV + k], device_id=dev, device_id_type=MESH).wait_recv()
        for cp in sends:
            cp.wait_send()

    launch()
    return [d[...] for d in dst]


def _sc_gather(name, arrs, kinds, collective_id):
    n = len(arrs)
    pairs = 7
    shapes = _xfer_out_shapes("gather", arrs, kinds)
    hbm = pltpu.MemorySpace.HBM
    src = [jax.new_ref(a, memory_space=hbm) for a in arrs]
    dst = [jax.empty_ref(_sds(shp, a.dtype), memory_space=hbm) for shp, a in zip(shapes, arrs)]

    @pl.kernel(mesh=plsc.ScalarSubcoreMesh(axis_name="seq", num_cores=1), name=name,
               scratch_types=(pltpu.SemaphoreType.DMA((n * pairs,)), pltpu.SemaphoreType.DMA((n * pairs,)),
                              pltpu.SemaphoreType.DMA((n,))),
               compiler_params=pltpu.CompilerParams(collective_id=collective_id))
    def launch(send, recv, loc):
        x, y, c = lax.axis_index("x"), lax.axis_index("y"), lax.axis_index("c")
        me = 4 * x + 2 * y + c
        sib = (x, y, 1 - c)
        chips = []
        for fx, fy in ((1, 0), (0, 1), (1, 1)):
            px, py = (1 - x if fx else x), (1 - y if fy else y)
            chips.append(((px, py, c), 4 * px + 2 * py + c, 4 * px + 2 * py + (1 - c)))
        barrier = pltpu.get_barrier_semaphore()
        for dev in [sib] + [ch[0] for ch in chips]:
            pl.semaphore_signal(barrier, inc=1, device_id=dev, device_id_type=MESH)
        pl.semaphore_wait(barrier, 4)

        def win(a, idx):
            return _window(dst[a], kinds[a], idx, N_DEV)

        def rcopy(a, p, s, d, dev):
            return pltpu.make_async_remote_copy(src_ref=s, dst_ref=d, send_sem=send.at[a * pairs + p],
                                                recv_sem=recv.at[a * pairs + p], device_id=dev, device_id_type=MESH)

        own, sends = [], []
        for a in range(n):
            own.append(pltpu.make_async_copy(src[a], win(a, me), loc.at[a]))
            own[-1].start()
        for j, (dev, _, _) in enumerate(chips):
            for a in range(n):
                sends.append(rcopy(a, 1 + j, src[a], win(a, me), dev))
                sends[-1].start()
        for a in range(n):
            sends.append(rcopy(a, 0, src[a], win(a, me), sib))
            sends[-1].start()
        for j, (dev, idx, _) in enumerate(chips):
            for a in range(n):
                rcopy(a, 1 + j, src[a], win(a, idx), dev).wait_recv()
                sends.append(rcopy(a, 4 + j, win(a, idx), win(a, idx), sib))
                sends[-1].start()
        for cp in own:
            cp.wait()
        for a in range(n):
            rcopy(a, 0, src[a], win(a, 4 * x + 2 * y + (1 - c)), sib).wait_recv()
        for j, (_, _, sidx) in enumerate(chips):
            for a in range(n):
                rcopy(a, 4 + j, src[a], win(a, sidx), sib).wait_recv()
        for cp in sends:
            cp.wait_send()

    launch()
    return [d[...] for d in dst]


_SEQ_IDS = {"gather_in": 7, "gather_mix": 1, "gather_ffn": 2, "grads_ffn": 3, "grads_mix": 4, "grads_small": 5, "grads_in": 6}


def _seq_start(name, mode, arrs, kinds, after):
    arrs = list(arrs)
    if after is not None:
        after, *arrs = lax.optimization_barrier((after, *arrs))
    if mode == "gather":
        return _sc_gather(name, arrs, kinds, _SEQ_IDS[name]), None
    return _sc_xfer(name, mode, arrs, kinds, _SEQ_IDS[name]), None


def _seq_wait(name, res, after):
    del name, after
    return list(res)


def _adamw(name, parts, w, m, v):
    P, R, C = parts.shape
    sub = 16 if parts.dtype == BF16 else 8
    tr = R if R * C <= (1 << 18) else _tile(R, max(sub, (1 << 18) // C), sub)

    def body(p_ref, w_ref, m_ref, v_ref, g_ref, d_ref, nm_ref, nv_ref):
        g = p_ref[0].astype(F32)
        for s in range(1, P):
            g = g + p_ref[s].astype(F32)
        m2 = ADAM_B1 * m_ref[...] + (1.0 - ADAM_B1) * g
        v2 = ADAM_B2 * v_ref[...] + (1.0 - ADAM_B2) * (g * g)
        m_hat = m2 / (1.0 - ADAM_B1 ** ADAM_STEP)
        v_hat = v2 / (1.0 - ADAM_B2 ** ADAM_STEP)
        g_ref[...] = g
        d_ref[...] = -ADAM_LR * (m_hat / (jnp.sqrt(v_hat) + ADAM_EPS) + ADAM_WD * w_ref[...])
        nm_ref[...] = m2
        nv_ref[...] = v2

    sp = pl.BlockSpec((tr, C), lambda i: (i, 0))
    return pl.pallas_call(
        body, name=name, grid=(R // tr,),
        in_specs=[pl.BlockSpec((P, tr, C), lambda i: (0, i, 0)), sp, sp, sp], out_specs=[sp] * 4,
        out_shape=[_sds((R, C), F32)] * 4, compiler_params=_params(1),
    )(parts, w, m, v)


def _pack(arrs, row_mult=8):
    pieces, total = [], 0
    for a in arrs:
        f = a.reshape(-1).astype(F32)
        pad = (-f.shape[0]) % (8 * LANES)
        pieces.append(jnp.pad(f, (0, pad)) if pad else f)
        total += f.shape[0] + pad
    tail = (-total) % (row_mult * LANES)
    if tail:
        pieces.append(jnp.zeros((tail,), F32))
    return jnp.concatenate(pieces).reshape(-1, LANES)


def _unpack(buf, shapes, lead=()):
    out, row = [], 0
    for shp in shapes:
        size = 1
        for d in shp:
            size *= d
        rows = -(-size // (8 * LANES)) * 8
        piece = buf[..., row:row + rows, :].reshape(lead + (rows * LANES,))[..., :size]
        out.append(piece.reshape(lead + tuple(shp)))
        row += rows
    return out


def kernel(x, norm_tok, w_in, a_re, a_im, log_dt, b_re, b_im, c_re, c_im, d_skip, w_glu, w_ssm_out, conv_w, conv_b, w_conv_out, w_o, norm_ffn, w_up, ffn_conv_w, ffn_conv_b, w_down, norm_final, loss_target, m_norm_tok, m_w_in, m_a_re, m_a_im, m_log_dt, m_b_re, m_b_im, m_c_re, m_c_im, m_d_skip, m_w_glu, m_w_ssm_out, m_conv_w, m_conv_b, m_w_conv_out, m_w_o, m_norm_ffn, m_w_up, m_ffn_conv_w, m_ffn_conv_b, m_w_down, m_norm_final, v_norm_tok, v_w_in, v_a_re, v_a_im, v_log_dt, v_b_re, v_b_im, v_c_re, v_c_im, v_d_skip, v_w_glu, v_w_ssm_out, v_conv_w, v_conv_b, v_w_conv_out, v_w_o, v_norm_ffn, v_w_up, v_ffn_conv_w, v_ffn_conv_b, v_w_down, v_norm_final):
    args = dict(locals())
    L, D = x.shape[1], x.shape[2]
    G, P, H = b_re.shape[1], b_re.shape[2], b_re.shape[3]
    SW = G * H
    CW = conv_b.shape[1]
    FF = ffn_conv_b.shape[1]
    GP = G * P
    nb = SW // LANES
    gpb = LANES // H
    me = 4 * lax.axis_index("x") + 2 * lax.axis_index("y") + lax.axis_index("c")
    tm = _tile(L, 256, 16)
    x2 = x[0]
    tgt = loss_target[0]

    big = [("w_in", "col"), ("w_glu", "row"), ("w_ssm_out", "col"), ("w_conv_out", "col"), ("w_o", "row"),
           ("w_up", "col"), ("w_down", "row")]
    shards = [_cast_bf16("cast_" + n, args[n][0]) for n, _ in big]
    small_in = _pack([conv_w[0], ffn_conv_w[0]])
    kind = dict(big)
    mixw, ffnw = ["w_glu", "w_ssm_out", "w_conv_out", "w_o"], ["w_up", "w_down"]
    shard = dict(zip([n for n, _ in big], shards))
    gathered, _ = _seq_start("gather_in", "gather", [shard["w_in"], small_in], ["col", "row"], None)
    W = {"w_in": gathered[0]}
    st_mix, tok_mix = _seq_start("gather_mix", "gather", [shard[n] for n in mixw], [kind[n] for n in mixw], None)
    st_ffn, tok_ffn = _seq_start("gather_ffn", "gather", [shard[n] for n in ffnw], [kind[n] for n in ffnw], None)
    cw_parts, fcw_parts = _unpack(gathered[-1].reshape(N_DEV, -1, LANES), [conv_w.shape[1:], ffn_conv_w.shape[1:]], (N_DEV,))
    conv_w_full = jnp.moveaxis(cw_parts, 0, 1).reshape(3, CW)
    ffn_conv_w_full = jnp.moveaxis(fcw_parts, 0, 1).reshape(3, FF)

    ar_row, ai_row = a_re.reshape(1, GP), a_im.reshape(1, GP)
    ldt_row = jnp.broadcast_to(log_dt.reshape(G, 1), (G, P)).reshape(1, GP)
    brt = jnp.transpose(b_re[0], (2, 0, 1)).reshape(H, GP)
    bit = jnp.transpose(b_im[0], (2, 0, 1)).reshape(H, GP)
    abar_re, abar_im, bbar_re, bbar_im = _prep_fwd("s5_prep", ar_row, ai_row, ldt_row, brt, bit)
    eye = jnp.eye(gpb, dtype=F32)

    def b_blocks(bt):
        return jnp.einsum("ab,hjbp->jahbp", eye, bt.reshape(H, nb, gpb, P)).reshape(nb, LANES, gpb * P)

    def c_blocks(c):
        return jnp.einsum("ab,jahp->jbpah", eye, c.reshape(nb, gpb, H, P)).reshape(nb, gpb * P, LANES)

    def diag_blocks(mat):
        return jnp.einsum("jahap->hjap", mat.reshape(nb, gpb, H, gpb, P))

    bm_re, bm_im = b_blocks(bbar_re), b_blocks(bbar_im)
    cm_re, cm_im = c_blocks(c_re[0]), -c_blocks(c_im[0])
    a3_re, a3_im = abar_re.reshape(-1, 8, LANES), abar_im.reshape(-1, 8, LANES)
    dskip_row = d_skip.reshape(1, SW)

    cbs = SW // LANES
    cb_v, cb_gb, cb_gc = cbs, cbs + CW // LANES, cbs + 2 * CW // LANES
    cb_ma = (SW + 3 * CW) // D
    xn = _rms_fwd("rms_tok", x2, norm_tok, tm)
    proj = _mm("proj", xn, W["w_in"], "nn", out_dtype=BF16)
    xs_re, xs_im, y, ya = _s5_fwd("s5_fwd", proj, bm_re.astype(BF16), bm_im.astype(BF16), cm_re.astype(BF16),
                                  cm_im.astype(BF16), dskip_row, a3_re, a3_im)
    W.update(zip(mixw, _seq_wait("gather_mix", st_mix, ya)))
    g1 = _mm("glu_gate", ya, W["w_glu"], "nn")
    ya2 = _glu_fwd("glu", y, g1, tm)
    za = _mm("ssm_out", ya2, W["w_ssm_out"], "nn")
    q = _convb_fwd("convb", proj, cb_v, cb_gb, cb_gc, conv_w_full, conv_b)
    zb = _mm("conv_out", q, W["w_conv_out"], "nn")
    merged = _merge_fwd("merge", proj, cb_ma, cb_ma + 1, za, zb, tm)
    o1 = _mm("mix_out", merged, W["w_o"], "nn")
    h1, hn = _res_rms_fwd("rms_ffn", x2, o1, norm_ffn, tm)
    W.update(zip(ffnw, _seq_wait("gather_ffn", st_ffn, hn)))
    hh = _mm("ffn_up", hn, W["w_up"], "nn", out_dtype=BF16)
    f = _ffn_fwd("ffn_act", hh, ffn_conv_w_full, ffn_conv_b)
    o2 = _mm("ffn_down", f, W["w_down"], "nn", tk=2816)
    dh2, dh2b, g_norm_final, loss_part = _final("final", h1, o2, norm_final.reshape(1, D), tgt, tm)

    df = _mm("d_ffn_act", dh2b, W["w_down"], "nt", tn=1408, out_dtype=BF16)
    gw_down = _mm("gw_down", f, dh2b, "tn", out_dtype=BF16, tm=1408, tn=512, tk=L)
    dhh, g_ffn_conv_w, g_ffn_conv_b = _ffn_bwd("ffn_act_bwd", hh, ffn_conv_w_full, ffn_conv_b, df)
    nhalf = lambda t: FF // t
    dhn = _mm("d_ffn_in", dhh, W["w_up"], "nt", tk=_tile(FF, 2816), dims=(L, D, 2 * FF),
              a_spec=lambda a, b, c: pl.BlockSpec((None, a, c), lambda i, j, k: (k // nhalf(c), i, k % nhalf(c))))
    gw_up = _mm("gw_up", hn, dhh, "tn", out_dtype=BF16, tn=_tile(FF, 1024), tk=L, dims=(D, 2 * FF, L),
                b_spec=lambda a, b, c: pl.BlockSpec((None, c, b), lambda i, j, k: (j // nhalf(b), k, j % nhalf(b))))
    dhn, gw_up, gw_down = lax.optimization_barrier((dhn, gw_up, gw_down))
    st_gffn, tok_gffn = _seq_start("grads_ffn", "exchange", [gw_up, gw_down], [kind[n] for n in ffnw], None)
    dh1, dh1b, g_norm_ffn = _rms_bwd("rms_ffn_bwd", dhn, h1, norm_ffn, dh2, tm, True)

    dmerged = _mm("d_merged", dh1b, W["w_o"], "nt", dep=tok_gffn)
    gw_o = _mm("gw_o", merged, dh1b, "tn", out_dtype=BF16, tk=L)
    dmerged, gw_o = lax.optimization_barrier((dmerged, gw_o))
    dza, dzb, dma, dmb = _merge_bwd("merge_bwd", proj, cb_ma, cb_ma + 1, za, zb, dmerged, tm)
    dq = _mm("d_q", dzb, W["w_conv_out"], "nt")
    gw_conv_out = _mm("gw_conv_out", q, dzb, "tn", out_dtype=BF16, tk=L)
    dq, gw_conv_out = lax.optimization_barrier((dq, gw_conv_out))
    dv, dgb, dgc, g_conv_w, g_conv_b = _convb_bwd("convb_bwd", proj, cb_v, cb_gb, cb_gc, conv_w_full, conv_b, dq)
    dya2 = _mm("d_ya2", dza, W["w_ssm_out"], "nt")
    gw_ssm_out = _mm("gw_ssm_out", ya2, dza, "tn", out_dtype=BF16, tk=L)
    dya2, gw_ssm_out = lax.optimization_barrier((dya2, gw_ssm_out))
    dy_direct, dg1 = _glu_bwd("glu_bwd", y, g1, dya2, tm)
    dya_g = _mm("d_ya_gate", dg1, W["w_glu"], "nt")
    gw_glu = _mm("gw_glu", ya, dg1, "tn", out_dtype=BF16, tk=L)
    dya_g, gw_glu = lax.optimization_barrier((dya_g, gw_glu))
    st_gmix, tok_gmix = _seq_start("grads_mix", "exchange", [gw_glu, gw_ssm_out, gw_conv_out, gw_o],
                                   [kind[n] for n in mixw], None)
    dyb, g_dskip = _gelu_bwd("gelu_bwd", y, dy_direct, dya_g, proj, dskip_row, tm)
    swap = lambda m: jnp.swapaxes(m, 1, 2).astype(BF16)
    du, gb_re, gb_im, gc_re, gc_im, dab_re, dab_im = _s5_bwd(
        "s5_bwd", dyb, proj, xs_re, xs_im, swap(cm_re), swap(cm_im), swap(bm_re), swap(bm_im), dskip_row, a3_re, a3_im)
    parts = dict(zip(ffnw, _seq_wait("grads_ffn", st_gffn, dab_re)))
    g_ar, g_ai, g_ldt, g_brt, g_bit = _prep_bwd(
        "s5_prep_bwd", ar_row, ai_row, ldt_row, brt, bit, dab_re.reshape(1, GP), dab_im.reshape(1, GP),
        diag_blocks(gb_re).reshape(H, GP), diag_blocks(gb_im).reshape(H, GP))
    small = dict(
        a_re=g_ar.reshape(1, G, P), a_im=g_ai.reshape(1, G, P),
        log_dt=g_ldt.reshape(G, P).sum(axis=1).reshape(1, G),
        b_re=jnp.transpose(g_brt.reshape(H, G, P), (1, 2, 0))[None], b_im=jnp.transpose(g_bit.reshape(H, G, P), (1, 2, 0))[None],
        c_re=jnp.transpose(diag_blocks(gc_re), (1, 2, 0, 3)).reshape(1, G, H, P),
        c_im=-jnp.transpose(diag_blocks(gc_im), (1, 2, 0, 3)).reshape(1, G, H, P),
        d_skip=g_dskip.reshape(1, G, H), conv_b=g_conv_b, norm_ffn=g_norm_ffn, ffn_conv_b=g_ffn_conv_b,
        norm_final=g_norm_final.reshape(D), conv_w=g_conv_w[None], ffn_conv_w=g_ffn_conv_w[None])
    rep = ["a_re", "a_im", "log_dt", "b_re", "b_im", "c_re", "c_im", "d_skip", "conv_b", "norm_ffn", "ffn_conv_b", "norm_final"]
    order = rep + ["conv_w", "ffn_conv_w"]
    full_shapes = {n: args[n].shape for n in rep}
    full_shapes["conv_w"], full_shapes["ffn_conv_w"] = (1, 3, CW), (1, 3, FF)
    rep_pack = _pack([small[n] for n in rep], LANES)
    rep_rows = rep_pack.shape[0]
    gpack = jnp.concatenate([rep_pack, _pack([small["conv_w"], small["ffn_conv_w"]])], axis=0)
    rows = gpack.shape[0]
    du, gpack = lax.optimization_barrier((du, gpack))
    st_small, tok_small = _seq_start("grads_small", "gather", [gpack], ["row"], None)

    dproj = _concat_cols("dproj", [du, dv, dgb, dgc, dma, dmb], tm)
    gw_in = _mm("gw_in", xn, dproj, "tn", out_dtype=BF16, tk=L, dep=tok_small)
    dproj, gw_in = lax.optimization_barrier((dproj, gw_in))
    st_gin, tok_gin = _seq_start("grads_in", "exchange", [gw_in], ["col"], None)
    dxn = _mm("d_xn", dproj, W["w_in"], "nt", dep=tok_gin)
    grad_x, g_norm_tok = _rms_bwd("rms_tok_bwd", dxn, x2, norm_tok, dh1, tm, False)

    res = {}

    def big_update(n):
        res[n] = [r[None] for r in _adamw("adamw_" + n, parts[n], args[n][0], args["m_" + n][0], args["v_" + n][0])]

    def after(xs, dep):
        return lax.optimization_barrier((list(xs), dep))[0]

    parts = dict(zip(ffnw, after([parts[n] for n in ffnw], grad_x)))
    for n in ffnw:
        big_update(n)
    parts.update(zip(mixw, after(_seq_wait("grads_mix", st_gmix, grad_x), [res[n][1] for n in ffnw])))
    for n in mixw:
        big_update(n)
    gall = after(_seq_wait("grads_small", st_small, None), [res[n][1] for n in mixw])[0].reshape(N_DEV, rows, LANES)
    gcw, gfcw = _unpack(gall[:, rep_rows:], [full_shapes["conv_w"], full_shapes["ffn_conv_w"]], (N_DEV,))
    cws, fcws = CW // N_DEV, FF // N_DEV
    gcw = lax.dynamic_slice_in_dim(gcw[:, 0], me * cws, cws, axis=2)
    gfcw = lax.dynamic_slice_in_dim(gfcw[:, 0], me * fcws, fcws, axis=2)
    res["conv_w"] = [r[None] for r in _adamw("adamw_conv_w", gcw, conv_w[0], m_conv_w[0], v_conv_w[0])]
    res["ffn_conv_w"] = [r[None] for r in _adamw("adamw_ffn_conv_w", gfcw, ffn_conv_w[0], m_ffn_conv_w[0], v_ffn_conv_w[0])]
    rep_out = _adamw("adamw_small", gall[:, :rep_rows], _pack([args[n] for n in rep], LANES),
                     _pack([args["m_" + n] for n in rep], LANES), _pack([args["v_" + n] for n in rep], LANES))
    rep_out = [_unpack(r, [full_shapes[n] for n in rep]) for r in rep_out]
    for i, n in enumerate(rep):
        res[n] = [r[i] for r in rep_out]
    nt_pack = after([_pack([g_norm_tok])], [res[n][1] for n in ("a_re", "conv_w", "ffn_conv_w")])
    nt_all = _all_gather("gather_norm_tok_grad", nt_pack, ["row"])[0].reshape(N_DEV, -1, LANES)
    nt_out = _adamw("adamw_norm_tok", nt_all, _pack([norm_tok]), _pack([m_norm_tok]), _pack([v_norm_tok]))
    res["norm_tok"] = [_unpack(r, [norm_tok.shape])[0] for r in nt_out]
    parts["w_in"] = after(_seq_wait("grads_in", st_gin, None), nt_out[0])[0]
    big_update("w_in")

    loss = lax.psum(loss_part[0, 0], ("x", "y", "c"))
    names = ["norm_tok", "w_in", "a_re", "a_im", "log_dt", "b_re", "b_im", "c_re", "c_im", "d_skip", "w_glu", "w_ssm_out",
             "conv_w", "conv_b", "w_conv_out", "w_o", "norm_ffn", "w_up", "ffn_conv_w", "ffn_conv_b", "w_down", "norm_final"]
    out = [loss, grad_x[None]]
    for slot in range(4):
        out += [res[n][slot] for n in names]
    return tuple(out)
```

```python
import functools

import jax
import jax.numpy as jnp
from jax import lax
from jax.experimental import pallas as pl
from jax.experimental.pallas import tpu as pltpu
from jax.experimental.pallas import tpu_sc as plsc

F32 = jnp.float32
BF16 = jnp.bfloat16
N_DEV = 8
LANES = 128
SLAB = 4
EPS = 1e-6
ADAM_LR = 0.001
ADAM_B1 = 0.9
ADAM_B2 = 0.999
ADAM_EPS = 1e-08
ADAM_WD = 0.01
ADAM_STEP = 10
VMEM_LIMIT = 56 * 1024 * 1024
MESH = pl.DeviceIdType.MESH


def _tile(n, pref, mult=LANES):
    best = None
    t = mult
    while t <= min(n, pref):
        if n % t == 0:
            best = t
        t += mult
    return best if best is not None else n


def _params(ndim):
    return pltpu.CompilerParams(dimension_semantics=("arbitrary",) * ndim, vmem_limit_bytes=VMEM_LIMIT)


def _sds(shape, dtype):
    return jax.ShapeDtypeStruct(tuple(shape), dtype)


def _mm(name, a, b, mode, *, out_dtype=F32, tm=1024, tn=1024, tk=2048, dims=None, a_spec=None, b_spec=None, dep=None):
    if dims is None:
        if mode == "nn":
            (M, K), N = a.shape, b.shape[1]
        elif mode == "nt":
            (M, K), N = a.shape, b.shape[0]
        else:
            (K, M), N = a.shape, b.shape[1]
    else:
        M, N, K = dims
    tm, tn, tk = _tile(M, tm), _tile(N, tn), _tile(K, tk)
    nk = K // tk
    if mode == "nn":
        dn = (((1,), (0,)), ((), ()))
        sa = pl.BlockSpec((tm, tk), lambda i, j, k: (i, k))
        sb = pl.BlockSpec((tk, tn), lambda i, j, k: (k, j))
    elif mode == "nt":
        dn = (((1,), (1,)), ((), ()))
        sa = pl.BlockSpec((tm, tk), lambda i, j, k: (i, k))
        sb = pl.BlockSpec((tn, tk), lambda i, j, k: (j, k))
    else:
        dn = (((0,), (0,)), ((), ()))
        sa = pl.BlockSpec((tk, tm), lambda i, j, k: (k, i))
        sb = pl.BlockSpec((tk, tn), lambda i, j, k: (k, j))
    sa = a_spec(tm, tn, tk) if a_spec is not None else sa
    sb = b_spec(tm, tn, tk) if b_spec is not None else sb
    use_acc = nk > 1 and out_dtype != F32

    deps = [] if dep is None else [dep]

    def body(a_ref, b_ref, *rest):
        o_ref, acc = rest[len(deps)], rest[len(deps) + 1:]
        k = pl.program_id(2)
        p = lax.dot_general(a_ref[...], b_ref[...], dn, preferred_element_type=F32)
        if nk == 1:
            o_ref[...] = p.astype(out_dtype)
        else:
            tgt = acc[0] if use_acc else o_ref

            @pl.when(k == 0)
            def _():
                tgt[...] = p

            @pl.when(k > 0)
            def _():
                tgt[...] += p

            if use_acc:
                @pl.when(k == nk - 1)
                def _():
                    o_ref[...] = acc[0][...].astype(out_dtype)

    return pl.pallas_call(
        body, name=name, grid=(M // tm, N // tn, nk),
        in_specs=[sa, sb] + [pl.BlockSpec(memory_space=pl.ANY)] * len(deps),
        out_specs=pl.BlockSpec((tm, tn), lambda i, j, k: (i, j)),
        out_shape=_sds((M, N), out_dtype),
        scratch_shapes=[pltpu.VMEM((tm, tn), F32)] if use_acc else [],
        compiler_params=_params(3),
    )(a, b, *deps)


def _rows(name, body, L, tm, ins, outs):
    return pl.pallas_call(
        body, name=name, grid=(L // tm,),
        in_specs=[s for _, s in ins], out_specs=[s for _, s in outs],
        out_shape=[o for o, _ in outs], compiler_params=_params(1),
    )(*[a for a, _ in ins])


def _rs(tm, w, cb=0):
    return pl.BlockSpec((tm, w), lambda i: (i, cb))


def _fs(shape):
    return pl.BlockSpec(tuple(shape), lambda i: (0,) * len(shape))


def _acc_rows(i, ref, part):
    @pl.when(i == 0)
    def _():
        ref[...] = part

    @pl.when(i > 0)
    def _():
        ref[...] += part


def _cast_bf16(name, w):
    R, C = w.shape
    tr = _tile(R, max(16, (1 << 20) // C), 16)

    def body(w_ref, o_ref):
        o_ref[...] = w_ref[...].astype(BF16)

    return _rows(name, body, R, tr, [(w, _rs(tr, C))], [(_sds((R, C), BF16), _rs(tr, C))])[0]


def _concat_cols(name, pieces, tm):
    L = pieces[0].shape[0]
    widths = [p.shape[1] for p in pieces]

    def body(*refs):
        o_ref, off = refs[-1], 0
        for p_ref, w in zip(refs[:-1], widths):
            o_ref[:, off:off + w] = p_ref[...]
            off += w

    return _rows(name, body, L, tm, [(p, _rs(tm, w)) for p, w in zip(pieces, widths)],
                 [(_sds((L, sum(widths)), pieces[0].dtype), _rs(tm, sum(widths)))])[0]


def _rms_fwd(name, x, g, tm):
    L, D = x.shape

    def body(x_ref, g_ref, o_ref):
        xv = x_ref[...]
        r = lax.rsqrt(jnp.mean(xv * xv, axis=-1, keepdims=True) + EPS)
        o_ref[...] = (xv * r * g_ref[...]).astype(BF16)

    return _rows(name, body, L, tm, [(x, _rs(tm, D)), (g, _fs((1, D)))], [(_sds((L, D), BF16), _rs(tm, D))])[0]


def _res_rms_fwd(name, x, o, g, tm):
    L, D = x.shape

    def body(x_ref, o_ref, g_ref, h_ref, hn_ref):
        h = x_ref[...] + o_ref[...]
        r = lax.rsqrt(jnp.mean(h * h, axis=-1, keepdims=True) + EPS)
        h_ref[...] = h
        hn_ref[...] = (h * r * g_ref[...]).astype(BF16)

    return _rows(name, body, L, tm, [(x, _rs(tm, D)), (o, _rs(tm, D)), (g, _fs((1, D)))],
                 [(_sds((L, D), F32), _rs(tm, D)), (_sds((L, D), BF16), _rs(tm, D))])


def _rms_bwd(name, dn, h, g, dres, tm, with_bf16):
    L, D = h.shape

    def body(dn_ref, h_ref, g_ref, dres_ref, dh_ref, *rest):
        i = pl.program_id(0)
        h = h_ref[...]
        r = lax.rsqrt(jnp.mean(h * h, axis=-1, keepdims=True) + EPS)
        xh = h * r
        d = dn_ref[...].astype(F32)
        dxh = d * g_ref[...]
        dh = dres_ref[...] + r * (dxh - xh * jnp.mean(dxh * xh, axis=-1, keepdims=True))
        dh_ref[...] = dh
        if with_bf16:
            rest[0][...] = dh.astype(BF16)
        _acc_rows(i, rest[-1], jnp.sum(d * xh, axis=0, keepdims=True))

    outs = [(_sds((L, D), F32), _rs(tm, D))]
    if with_bf16:
        outs.append((_sds((L, D), BF16), _rs(tm, D)))
    outs.append((_sds((1, D), F32), _fs((1, D))))
    return _rows(name, body, L, tm, [(dn, _rs(tm, D)), (h, _rs(tm, D)), (g, _fs((1, D))), (dres, _rs(tm, D))], outs)


def _final(name, h1, o2, g, tgt, tm):
    L, D = h1.shape

    def body(h1_ref, o2_ref, g_ref, t_ref, dh_ref, dhb_ref, dg_ref, loss_ref):
        i = pl.program_id(0)
        h = h1_ref[...] + o2_ref[...]
        r = lax.rsqrt(jnp.mean(h * h, axis=-1, keepdims=True) + EPS)
        xh = h * r
        gv = g_ref[...]
        e = xh * gv - t_ref[...]
        part = 0.5 * jnp.sum(jnp.mean(e * e, axis=-1, keepdims=True), axis=0, keepdims=True)
        dy = e / D
        dxh = dy * gv
        dh = r * (dxh - xh * jnp.mean(dxh * xh, axis=-1, keepdims=True))
        dh_ref[...] = dh
        dhb_ref[...] = dh.astype(BF16)
        _acc_rows(i, dg_ref, jnp.sum(dy * xh, axis=0, keepdims=True))
        _acc_rows(i, loss_ref, jnp.broadcast_to(part, (8, LANES)))

    return _rows(name, body, L, tm,
                 [(h1, _rs(tm, D)), (o2, _rs(tm, D)), (g, _fs((1, D))), (tgt, _rs(tm, D))],
                 [(_sds((L, D), F32), _rs(tm, D)), (_sds((L, D), BF16), _rs(tm, D)),
                  (_sds((1, D), F32), _fs((1, D))), (_sds((8, LANES), F32), _fs((8, LANES)))])


def _gelu_and_grad(x):
    c, k = 0.7978845608028654, 0.044715
    x2 = x * x
    t = jnp.tanh(c * x * (1.0 + k * x2))
    half = 0.5 * x
    return half * (1.0 + t), 0.5 * (1.0 + t) + half * (1.0 - t * t) * (c * (1.0 + 3.0 * k * x2))


def _glu_fn(y, g1):
    ya = jax.nn.gelu(y)
    return ya * jax.nn.sigmoid(g1)


def _glu_fwd(name, y, g1, tm):
    L, W = y.shape

    def body(y_ref, g_ref, o_ref):
        o_ref[...] = _glu_fn(y_ref[...], g_ref[...].astype(F32)).astype(BF16)

    return _rows(name, body, L, tm, [(y, _rs(tm, W)), (g1, _rs(tm, W))], [(_sds((L, W), BF16), _rs(tm, W))])[0]


def _glu_bwd(name, y, g1, dya2, tm):
    L, W = y.shape

    def body(y_ref, g_ref, d_ref, dy_ref, dg_ref):
        _, vjp = jax.vjp(_glu_fn, y_ref[...], g_ref[...].astype(F32))
        dy, dg = vjp(d_ref[...].astype(F32))
        dy_ref[...] = dy
        dg_ref[...] = dg.astype(BF16)

    return _rows(name, body, L, tm, [(y, _rs(tm, W)), (g1, _rs(tm, W)), (dya2, _rs(tm, W))],
                 [(_sds((L, W), F32), _rs(tm, W)), (_sds((L, W), BF16), _rs(tm, W))])


def _gelu_bwd(name, y, dy_direct, dya_g, proj, dskip, tm):
    L, W = y.shape

    def body(y_ref, dd_ref, dg_ref, u_ref, dyb_ref, dsk_ref):
        i = pl.program_id(0)
        dy = dd_ref[...] + dg_ref[...].astype(F32) * _gelu_and_grad(y_ref[...])[1]
        dyb_ref[...] = dy.astype(BF16)
        _acc_rows(i, dsk_ref, jnp.sum(dy * u_ref[...].astype(F32), axis=0, keepdims=True))

    del dskip
    return _rows(name, body, L, tm,
                 [(y, _rs(tm, W)), (dy_direct, _rs(tm, W)), (dya_g, _rs(tm, W)), (proj, _rs(tm, W, 0))],
                 [(_sds((L, W), BF16), _rs(tm, W)), (_sds((1, W), F32), _fs((1, W)))])


def _merge_fn(ma, mb, za, zb):
    return jax.nn.sigmoid(ma) * za + jax.nn.sigmoid(mb) * zb


def _merge_fwd(name, proj, cb_a, cb_b, za, zb, tm):
    L, D = za.shape

    def body(ma_ref, mb_ref, za_ref, zb_ref, o_ref):
        o_ref[...] = _merge_fn(ma_ref[...].astype(F32), mb_ref[...].astype(F32), za_ref[...].astype(F32),
                               zb_ref[...].astype(F32)).astype(BF16)

    return _rows(name, body, L, tm,
                 [(proj, _rs(tm, D, cb_a)), (proj, _rs(tm, D, cb_b)), (za, _rs(tm, D)), (zb, _rs(tm, D))],
                 [(_sds((L, D), BF16), _rs(tm, D))])[0]


def _merge_bwd(name, proj, cb_a, cb_b, za, zb, dmerged, tm):
    L, D = za.shape

    def body(ma_ref, mb_ref, za_ref, zb_ref, d_ref, dza_ref, dzb_ref, dma_ref, dmb_ref):
        _, vjp = jax.vjp(_merge_fn, ma_ref[...].astype(F32), mb_ref[...].astype(F32), za_ref[...].astype(F32),
                         zb_ref[...].astype(F32))
        dma, dmb, dza, dzb = vjp(d_ref[...].astype(F32))
        dza_ref[...] = dza.astype(BF16)
        dzb_ref[...] = dzb.astype(BF16)
        dma_ref[...] = dma.astype(BF16)
        dmb_ref[...] = dmb.astype(BF16)

    return _rows(name, body, L, tm,
                 [(proj, _rs(tm, D, cb_a)), (proj, _rs(tm, D, cb_b)), (za, _rs(tm, D)), (zb, _rs(tm, D)),
                  (dmerged, _rs(tm, D))],
                 [(_sds((L, D), BF16), _rs(tm, D)), (_sds((L, D), BF16), _rs(tm, D)),
                  (_sds((L, D), BF16), _rs(tm, D)), (_sds((L, D), BF16), _rs(tm, D))])


def _shift_down(x, k):
    row = lax.broadcasted_iota(jnp.int32, x.shape, 0)
    return jnp.where(row >= k, pltpu.roll(x, k, axis=0), 0.0)


def _shift_up(x, k):
    n = x.shape[0]
    row = lax.broadcasted_iota(jnp.int32, x.shape, 0)
    return jnp.where(row < n - k, pltpu.roll(x, n - k, axis=0), 0.0)


def _conv3(cv, w_ref, b_ref):
    return (w_ref[2:3, :] * cv + w_ref[1:2, :] * _shift_down(cv, 1) + w_ref[0:1, :] * _shift_down(cv, 2)
            + b_ref[...])


def _conv3_bwd(dcc, cv, w_ref):
    dcv = w_ref[2:3, :] * dcc + w_ref[1:2, :] * _shift_up(dcc, 1) + w_ref[0:1, :] * _shift_up(dcc, 2)
    dw = [jnp.sum(dcc * _shift_down(cv, 2), axis=0, keepdims=True),
          jnp.sum(dcc * _shift_down(cv, 1), axis=0, keepdims=True),
          jnp.sum(dcc * cv, axis=0, keepdims=True)]
    db = jnp.sum(dcc, axis=0, keepdims=True)
    return dcv, dw, db


def _store_rows(ref, rows):
    for r, val in enumerate(rows):
        ref[r:r + 1, :] = val


def _cols(name, body, ncb, ins, outs):
    return pl.pallas_call(
        body, name=name, grid=(ncb,),
        in_specs=[s for _, s in ins], out_specs=[s for _, s in outs],
        out_shape=[o for o, _ in outs], compiler_params=_params(1),
    )(*[a for a, _ in ins])


def _cb(L, w, off=0):
    return pl.BlockSpec((L, w), lambda j: (0, j + off))


def _convb_fwd(name, proj, cb_v, cb_gb, cb_gc, w, b):
    L = proj.shape[0]
    W = w.shape[1]
    c = LANES

    def body(v_ref, gb_ref, gc_ref, w_ref, b_ref, q_ref):
        cc = _conv3(gc_ref[...].astype(F32) * v_ref[...].astype(F32), w_ref, b_ref)
        q_ref[...] = (gb_ref[...].astype(F32) * cc).astype(BF16)

    return _cols(name, body, W // c,
                 [(proj, _cb(L, c, cb_v)), (proj, _cb(L, c, cb_gb)), (proj, _cb(L, c, cb_gc)),
                  (w, _cb(3, c)), (b, _cb(1, c))],
                 [(_sds((L, W), BF16), _cb(L, c))])[0]


def _convb_bwd(name, proj, cb_v, cb_gb, cb_gc, w, b, dq):
    L = proj.shape[0]
    W = w.shape[1]
    c = LANES

    def body(v_ref, gb_ref, gc_ref, w_ref, b_ref, dq_ref, dv_ref, dgb_ref, dgc_ref, dw_ref, db_ref):
        v, gc = v_ref[...].astype(F32), gc_ref[...].astype(F32)
        cv = gc * v
        cc = _conv3(cv, w_ref, b_ref)
        dq = dq_ref[...].astype(F32)
        dgb_ref[...] = (dq * cc).astype(BF16)
        dcv, dw, db = _conv3_bwd(dq * gb_ref[...].astype(F32), cv, w_ref)
        dv_ref[...] = (dcv * gc).astype(BF16)
        dgc_ref[...] = (dcv * v).astype(BF16)
        _store_rows(dw_ref, dw)
        db_ref[...] = db

    return _cols(name, body, W // c,
                 [(proj, _cb(L, c, cb_v)), (proj, _cb(L, c, cb_gb)), (proj, _cb(L, c, cb_gc)),
                  (w, _cb(3, c)), (b, _cb(1, c)), (dq, _cb(L, c))],
                 [(_sds((L, W), BF16), _cb(L, c)), (_sds((L, W), BF16), _cb(L, c)), (_sds((L, W), BF16), _cb(L, c)),
                  (_sds((3, W), F32), _cb(3, c)), (_sds((1, W), F32), _cb(1, c))])


HALO = 16


def _ffn_tiles(L, Fw):
    tr = _tile(L, 256, HALO)
    tc = _tile(Fw, 1408)
    return tr, tc, Fw // tc, L // tr, tr // HALO


def _ffn_fwd(name, hh, w, b):
    L = hh.shape[0]
    Fw = w.shape[1]
    tr, tc, ncb, nrt, rpt = _ffn_tiles(L, Fw)

    def body(a_ref, p_ref, h2_ref, w_ref, b_ref, f_ref):
        first = pl.program_id(1) == 0
        for c0 in range(0, tc, LANES):
            cs = slice(c0, c0 + LANES)
            prev = jnp.where(first, 0.0, p_ref[:, cs].astype(F32))
            x = jnp.concatenate([prev, a_ref[:, cs].astype(F32)], axis=0)
            n = x.shape[0]
            a = (w_ref[2:3, cs] * x + w_ref[1:2, cs] * pltpu.roll(x, 1, axis=0) + w_ref[0:1, cs] * pltpu.roll(x, 2, axis=0)
                 + b_ref[:, cs])[HALO:n]
            f_ref[:, cs] = (_gelu_and_grad(a)[0] * h2_ref[:, cs].astype(F32)).astype(BF16)

    main = pl.BlockSpec((tr, tc), lambda j, i: (i, j))
    return pl.pallas_call(
        body, name=name, grid=(ncb, nrt),
        in_specs=[main, pl.BlockSpec((HALO, tc), lambda j, i: (jnp.maximum(i * rpt - 1, 0), j)),
                  pl.BlockSpec((tr, tc), lambda j, i: (i, j + ncb)),
                  pl.BlockSpec((3, tc), lambda j, i: (0, j)), pl.BlockSpec((1, tc), lambda j, i: (0, j))],
        out_specs=main, out_shape=_sds((L, Fw), BF16), compiler_params=_params(2),
    )(hh, hh, hh, w, b)


def _ffn_bwd(name, hh, w, b, df):
    L = hh.shape[0]
    Fw = w.shape[1]
    tr, tc, ncb, nrt, rpt = _ffn_tiles(L, Fw)

    def body(a_ref, ap_ref, an_ref, h2_ref, h2n_ref, df_ref, dfn_ref, w_ref, b_ref, dhh_ref, dw_ref, db_ref):
        i = pl.program_id(1)
        first, last = i == 0, i == nrt - 1
        for c0 in range(0, tc, LANES):
            cs = slice(c0, c0 + LANES)
            zero = jnp.zeros((HALO, LANES), F32)
            h1 = jnp.concatenate([jnp.where(first, 0.0, ap_ref[:, cs].astype(F32)), a_ref[:, cs].astype(F32),
                                  an_ref[:, cs].astype(F32)], axis=0)
            h2 = jnp.concatenate([zero, h2_ref[:, cs].astype(F32), h2n_ref[:, cs].astype(F32)], axis=0)
            d = jnp.concatenate([zero, df_ref[:, cs].astype(F32), jnp.where(last, 0.0, dfn_ref[:, cs].astype(F32))], axis=0)
            n = h1.shape[0]
            s1, s2 = pltpu.roll(h1, 1, axis=0), pltpu.roll(h1, 2, axis=0)
            a = w_ref[2:3, cs] * h1 + w_ref[1:2, cs] * s1 + w_ref[0:1, cs] * s2 + b_ref[:, cs]
            ga, dga = _gelu_and_grad(a)
            da = d * h2 * dga
            dh1 = w_ref[2:3, cs] * da + w_ref[1:2, cs] * pltpu.roll(da, n - 1, axis=0) + w_ref[0:1, cs] * pltpu.roll(da, n - 2, axis=0)
            dhh_ref[0, :, cs] = dh1[HALO:HALO + tr].astype(BF16)
            dhh_ref[1, :, cs] = (d * ga)[HALO:HALO + tr].astype(BF16)
            dam = da[HALO:HALO + tr]
            rows = [jnp.sum(dam * s2[HALO:HALO + tr], axis=0, keepdims=True),
                    jnp.sum(dam * s1[HALO:HALO + tr], axis=0, keepdims=True),
                    jnp.sum(dam * h1[HALO:HALO + tr], axis=0, keepdims=True),
                    jnp.sum(dam, axis=0, keepdims=True)]

            @pl.when(first)
            def _():
                for r in range(3):
                    dw_ref[r:r + 1, cs] = rows[r]
                db_ref[:, cs] = rows[3]

            @pl.when(i > 0)
            def _():
                for r in range(3):
                    dw_ref[r:r + 1, cs] += rows[r]
                db_ref[:, cs] += rows[3]

    def spec(col_off, kind):
        if kind == "main":
            return pl.BlockSpec((tr, tc), lambda j, i: (i, j + col_off))
        if kind == "prev":
            return pl.BlockSpec((HALO, tc), lambda j, i: (jnp.maximum(i * rpt - 1, 0), j + col_off))
        return pl.BlockSpec((HALO, tc), lambda j, i: (jnp.minimum((i + 1) * rpt, nrt * rpt - 1), j + col_off))

    return pl.pallas_call(
        body, name=name, grid=(ncb, nrt),
        in_specs=[spec(0, "main"), spec(0, "prev"), spec(0, "next"), spec(ncb, "main"), spec(ncb, "next"),
                  spec(0, "main"), spec(0, "next"),
                  pl.BlockSpec((3, tc), lambda j, i: (0, j)), pl.BlockSpec((1, tc), lambda j, i: (0, j))],
        out_specs=[pl.BlockSpec((2, tr, tc), lambda j, i: (0, i, j)),
                   pl.BlockSpec((3, tc), lambda j, i: (0, j)), pl.BlockSpec((1, tc), lambda j, i: (0, j))],
        out_shape=[_sds((2, L, Fw), BF16), _sds((3, Fw), F32), _sds((1, Fw), F32)], compiler_params=_params(2),
    )(hh, hh, hh, hh, hh, df, df, w, b)


def _prep_fn(ar, ai, ldt, brt, bit):
    dt = jnp.exp(ldt)
    mag = jnp.exp(dt * ar)
    are = mag * jnp.cos(dt * ai)
    aim = mag * jnp.sin(dt * ai)
    nr = are - 1.0
    ni = aim
    den = ar * ar + ai * ai
    fr = (nr * ar + ni * ai) / den
    fi = (ni * ar - nr * ai) / den
    return are, aim, fr * brt - fi * bit, fr * bit + fi * brt


def _prep_fwd(name, ar, ai, ldt, brt, bit):
    def body(ar_ref, ai_ref, l_ref, br_ref, bi_ref, o1, o2, o3, o4):
        o1[...], o2[...], o3[...], o4[...] = _prep_fn(ar_ref[...], ai_ref[...], l_ref[...], br_ref[...], bi_ref[...])

    return pl.pallas_call(body, name=name,
                          out_shape=[_sds(ar.shape, F32), _sds(ar.shape, F32), _sds(brt.shape, F32), _sds(brt.shape, F32)],
                          )(ar, ai, ldt, brt, bit)


def _prep_bwd(name, ar, ai, ldt, brt, bit, g1, g2, g3, g4):
    def body(ar_ref, ai_ref, l_ref, br_ref, bi_ref, g1_ref, g2_ref, g3_ref, g4_ref, o1, o2, o3, o4, o5):
        _, vjp = jax.vjp(_prep_fn, ar_ref[...], ai_ref[...], l_ref[...], br_ref[...], bi_ref[...])
        o1[...], o2[...], o3[...], o4[...], o5[...] = vjp((g1_ref[...], g2_ref[...], g3_ref[...], g4_ref[...]))

    return pl.pallas_call(body, name=name,
                          out_shape=[_sds(ar.shape, F32)] * 3 + [_sds(brt.shape, F32)] * 2,
                          )(ar, ai, ldt, brt, bit, g1, g2, g3, g4)


def _ssm_in(name, src, m1, m2, tm, dep=None):
    L = src.shape[0]
    nb = m1.shape[0]

    deps = [] if dep is None else [dep]

    def body(s_ref, m1_ref, m2_ref, *rest):
        o1_ref, o2_ref = rest[len(deps):]
        u = s_ref[...].astype(BF16)
        r1 = jnp.dot(u, m1_ref[...], preferred_element_type=F32)
        r2 = jnp.dot(u, m2_ref[...], preferred_element_type=F32)
        for q in range(SLAB):
            o1_ref[q] = r1[:, q * LANES:(q + 1) * LANES].astype(BF16)
            o2_ref[q] = r2[:, q * LANES:(q + 1) * LANES].astype(BF16)

    ms = pl.BlockSpec((None, LANES, SLAB * LANES), lambda i, j: (j, 0, 0))
    os_ = pl.BlockSpec((SLAB, tm, LANES), lambda i, j: (j, i, 0))
    return pl.pallas_call(
        body, name=name, grid=(L // tm, nb),
        in_specs=[pl.BlockSpec((tm, LANES), lambda i, j: (i, j)), ms, ms] + [pl.BlockSpec(memory_space=pl.ANY)] * len(deps),
        out_specs=[os_, os_],
        out_shape=[_sds((SLAB * nb, L, LANES), BF16)] * 2, compiler_params=_params(2),
    )(src, m1, m2, *deps)


def _ssm_out(name, x1, x2, m1, m2, aux, dvec, tm, post=None):
    L = x1.shape[1]
    nb = m1.shape[0]

    def body(x1_ref, x2_ref, m1_ref, m2_ref, a_ref, d_ref, o_ref, *rest):
        a1 = jnp.concatenate([x1_ref[q] for q in range(SLAB)], axis=1).astype(BF16)
        a2 = jnp.concatenate([x2_ref[q] for q in range(SLAB)], axis=1).astype(BF16)
        y = (jnp.dot(a1, m1_ref[...], preferred_element_type=F32) + jnp.dot(a2, m2_ref[...], preferred_element_type=F32)
             + d_ref[...] * a_ref[...].astype(F32))
        o_ref[...] = y
        if post is not None:
            rest[0][...] = post(y).astype(BF16)

    xs = pl.BlockSpec((SLAB, tm, LANES), lambda i, j: (j, i, 0))
    ms = pl.BlockSpec((None, SLAB * LANES, LANES), lambda i, j: (j, 0, 0))
    cs = pl.BlockSpec((tm, LANES), lambda i, j: (i, j))
    W = nb * LANES
    outs, ospecs = [_sds((L, W), F32)], [cs]
    if post is not None:
        outs.append(_sds((L, W), BF16))
        ospecs.append(cs)
    return pl.pallas_call(
        body, name=name, grid=(L // tm, nb),
        in_specs=[xs, xs, ms, ms, cs, pl.BlockSpec((1, LANES), lambda i, j: (0, j))], out_specs=ospecs,
        out_shape=outs, compiler_params=_params(2),
    )(x1, x2, m1, m2, aux, dvec)


def _ssm_dw(name, src, x1, x2, tk):
    L = src.shape[0]
    nb = x1.shape[0] // SLAB
    dn = (((0,), (0,)), ((), ()))

    def body(s_ref, x1_ref, x2_ref, o1_ref, o2_ref):
        k = pl.program_id(1)
        s = s_ref[...].astype(BF16)
        a1 = jnp.concatenate([x1_ref[q] for q in range(SLAB)], axis=1).astype(BF16)
        a2 = jnp.concatenate([x2_ref[q] for q in range(SLAB)], axis=1).astype(BF16)
        _acc_rows(k, o1_ref, lax.dot_general(s, a1, dn, preferred_element_type=F32))
        _acc_rows(k, o2_ref, lax.dot_general(s, a2, dn, preferred_element_type=F32))

    xs = pl.BlockSpec((SLAB, tk, LANES), lambda j, k: (j, k, 0))
    os_ = pl.BlockSpec((None, LANES, SLAB * LANES), lambda j, k: (j, 0, 0))
    return pl.pallas_call(
        body, name=name, grid=(nb, L // tk),
        in_specs=[pl.BlockSpec((tk, LANES), lambda j, k: (k, j)), xs, xs], out_specs=[os_, os_],
        out_shape=[_sds((nb, LANES, SLAB * LANES), F32)] * 2, compiler_params=_params(2),
    )(src, x1, x2)


def _scan(name, b_re, b_im, a_re, a_im, xs=None):
    reverse = xs is not None
    ns, L, _ = b_re.shape
    ng = ns // 8
    tc = min(LANES, L)
    pitch = tc + 8
    nt = L // tc
    n_in = 4 if reverse else 2

    def body(*refs):
        ins = refs[:n_in]
        ar_ref, ai_ref = refs[n_in], refs[n_in + 1]
        o_re, o_im = refs[n_in + 2], refs[n_in + 3]
        k = n_in + 4
        if reverse:
            da_re, da_im = refs[k], refs[k + 1]
            k += 2
        stage = refs[k:k + n_in]
        out_re, out_im, st_re, st_im = refs[k + n_in:k + n_in + 4]
        acc = refs[k + n_in + 4:]
        i = pl.program_id(0)

        @pl.when(i == 0)
        def _():
            st_re[...] = jnp.zeros(st_re.shape, F32)
            st_im[...] = jnp.zeros(st_im.shape, F32)
            for r in acc:
                r[...] = jnp.zeros(r.shape, F32)

        for s in range(ns):
            for src, dst in zip(ins, stage):
                dst[pl.ds(s * pitch, tc), :] = src[s].astype(F32)

        a_r = [ar_ref[g] for g in range(ng)]
        a_i = [ai_ref[g] for g in range(ng)]

        def step(tt, carry):
            t = (tc - 1 - tt) if reverse else tt
            new = []
            for g in range(ng):
                rows = pl.ds(g * 8 * pitch + t, 8, stride=pitch)
                cr, ci = carry[2 * g], carry[2 * g + 1]
                br, bi = stage[0][rows, :], stage[1][rows, :]
                if reverse:
                    xr, xi = stage[2][rows, :], stage[3][rows, :]
                    acc[0][g] += xr * cr + xi * ci
                    acc[1][g] += xr * ci - xi * cr
                    nr = a_r[g] * cr + a_i[g] * ci + br
                    ni = a_r[g] * ci - a_i[g] * cr + bi
                else:
                    nr = a_r[g] * cr - a_i[g] * ci + br
                    ni = a_r[g] * ci + a_i[g] * cr + bi
                out_re[rows, :] = nr
                out_im[rows, :] = ni
                new += [nr, ni]
            return tuple(new)

        init = []
        for g in range(ng):
            init += [st_re[g], st_im[g]]
        fin = lax.fori_loop(0, tc, step, tuple(init), unroll=2)
        for g in range(ng):
            st_re[g] = fin[2 * g]
            st_im[g] = fin[2 * g + 1]
        for s in range(ns):
            o_re[s] = out_re[pl.ds(s * pitch, tc), :].astype(BF16)
            o_im[s] = out_im[pl.ds(s * pitch, tc), :].astype(BF16)
        if reverse:
            da_re[...] = acc[0][...]
            da_im[...] = acc[1][...]

    tmap = (lambda i: (0, nt - 1 - i, 0)) if reverse else (lambda i: (0, i, 0))
    bs = pl.BlockSpec((ns, tc, LANES), tmap)
    as_ = pl.BlockSpec((ng, 8, LANES), lambda i: (0, 0, 0))
    ins = [b_re, b_im] + (list(xs) if reverse else [])
    out_shape = [_sds((ns, L, LANES), BF16)] * 2 + ([_sds((ng, 8, LANES), F32)] * 2 if reverse else [])
    out_specs = [bs, bs] + ([as_, as_] if reverse else [])
    scratch = [pltpu.VMEM((ns * pitch, LANES), F32)] * (n_in + 2) + [pltpu.VMEM((ng, 8, LANES), F32)] * (4 if reverse else 2)
    return pl.pallas_call(
        body, name=name, grid=(nt,), in_specs=[bs] * n_in + [as_, as_], out_specs=out_specs,
        out_shape=out_shape, scratch_shapes=scratch, compiler_params=_params(1),
    )(*ins, a_re, a_im)


def _scan_steps(tc, pitch, ng, reverse, a_r, a_i, stage_b, stage_x, out_re, out_im, st_re, st_im, acc):
    def step(tt, carry):
        t = (tc - 1 - tt) if reverse else tt
        new = []
        for g in range(ng):
            rows = pl.ds(g * 8 * pitch + t, 8, stride=pitch)
            cr, ci = carry[2 * g], carry[2 * g + 1]
            br, bi = stage_b[0][rows, :], stage_b[1][rows, :]
            if reverse:
                xr, xi = stage_x[0][rows, :], stage_x[1][rows, :]
                acc[0][g] += xr * cr + xi * ci
                acc[1][g] += xr * ci - xi * cr
                nr = a_r[g] * cr + a_i[g] * ci + br
                ni = a_r[g] * ci - a_i[g] * cr + bi
            else:
                nr = a_r[g] * cr - a_i[g] * ci + br
                ni = a_r[g] * ci + a_i[g] * cr + bi
            out_re[rows, :] = nr
            out_im[rows, :] = ni
            new += [nr, ni]
        return tuple(new)

    init = []
    for g in range(ng):
        init += [st_re[g], st_im[g]]
    fin = lax.fori_loop(0, tc, step, tuple(init), unroll=2)
    for g in range(ng):
        st_re[g] = fin[2 * g]
        st_im[g] = fin[2 * g + 1]


def _s5_fwd(name, proj, bm_re, bm_im, cm_re, cm_im, dskip, a_re, a_im):
    L = proj.shape[0]
    nb = bm_re.shape[0]
    ns, W = SLAB * nb, nb * LANES
    ng = ns // 8
    tc = min(LANES, L)
    pitch = tc + 8
    wide = SLAB * LANES

    def body(u_ref, bre_ref, bim_ref, cre_ref, cim_ref, d_ref, ar_ref, ai_ref, xr_ref, xi_ref, y_ref, ya_ref,
             sb_re, sb_im, out_re, out_im, st_re, st_im):
        @pl.when(pl.program_id(0) == 0)
        def _():
            st_re[...] = jnp.zeros(st_re.shape, F32)
            st_im[...] = jnp.zeros(st_im.shape, F32)

        for j in range(nb):
            ub = u_ref[:, j * LANES:(j + 1) * LANES]
            r1 = jnp.dot(ub, bre_ref[j], preferred_element_type=F32)
            r2 = jnp.dot(ub, bim_ref[j], preferred_element_type=F32)
            for q in range(SLAB):
                sb_re[pl.ds((SLAB * j + q) * pitch, tc), :] = r1[:, q * LANES:(q + 1) * LANES]
                sb_im[pl.ds((SLAB * j + q) * pitch, tc), :] = r2[:, q * LANES:(q + 1) * LANES]
        a_r = [ar_ref[g] for g in range(ng)]
        a_i = [ai_ref[g] for g in range(ng)]
        _scan_steps(tc, pitch, ng, False, a_r, a_i, (sb_re, sb_im), None, out_re, out_im, st_re, st_im, None)
        for j in range(nb):
            x1 = [out_re[pl.ds((SLAB * j + q) * pitch, tc), :].astype(BF16) for q in range(SLAB)]
            x2 = [out_im[pl.ds((SLAB * j + q) * pitch, tc), :].astype(BF16) for q in range(SLAB)]
            for q in range(SLAB):
                xr_ref[SLAB * j + q] = x1[q]
                xi_ref[SLAB * j + q] = x2[q]
            cols = slice(j * LANES, (j + 1) * LANES)
            y = (jnp.dot(jnp.concatenate(x1, axis=1), cre_ref[j], preferred_element_type=F32)
                 + jnp.dot(jnp.concatenate(x2, axis=1), cim_ref[j], preferred_element_type=F32)
                 + d_ref[:, cols] * u_ref[:, cols].astype(F32))
            y_ref[:, cols] = y
            ya_ref[:, cols] = jax.nn.gelu(y).astype(BF16)

    full3 = lambda s: pl.BlockSpec(s, lambda i: (0, 0, 0))
    xs = pl.BlockSpec((ns, tc, LANES), lambda i: (0, i, 0))
    rows = pl.BlockSpec((tc, W), lambda i: (i, 0))
    return pl.pallas_call(
        body, name=name, grid=(L // tc,),
        in_specs=[rows, full3((nb, LANES, wide)), full3((nb, LANES, wide)), full3((nb, wide, LANES)),
                  full3((nb, wide, LANES)), pl.BlockSpec((1, W), lambda i: (0, 0)), full3((ng, 8, LANES)), full3((ng, 8, LANES))],
        out_specs=[xs, xs, rows, rows],
        out_shape=[_sds((ns, L, LANES), BF16)] * 2 + [_sds((L, W), F32), _sds((L, W), BF16)],
        scratch_shapes=[pltpu.VMEM((ns * pitch, LANES), F32)] * 4 + [pltpu.VMEM((ng, 8, LANES), F32)] * 2,
        compiler_params=_params(1),
    )(proj, bm_re, bm_im, cm_re, cm_im, dskip, a_re, a_im)


def _s5_bwd(name, dyb, proj, xs_re, xs_im, cmt_re, cmt_im, bmt_re, bmt_im, dskip, a_re, a_im):
    L = dyb.shape[0]
    nb = cmt_re.shape[0]
    ns, W = SLAB * nb, nb * LANES
    ng = ns // 8
    tc = min(LANES, L)
    pitch = tc + 8
    nt = L // tc
    wide = SLAB * LANES
    dn = (((0,), (0,)), ((), ()))

    def body(dy_ref, u_ref, xr_ref, xi_ref, cre_ref, cim_ref, bre_ref, bim_ref, d_ref, ar_ref, ai_ref,
             du_ref, gbr_ref, gbi_ref, gcr_ref, gci_ref, dar_ref, dai_ref,
             sd_re, sd_im, sx_re, sx_im, out_re, out_im, st_re, st_im, acc_re, acc_im):
        first = pl.program_id(0) == 0

        @pl.when(first)
        def _():
            for r in (st_re, st_im, acc_re, acc_im):
                r[...] = jnp.zeros(r.shape, F32)
            for r in (gbr_ref, gbi_ref, gcr_ref, gci_ref):
                r[...] = jnp.zeros(r.shape, F32)

        for j in range(nb):
            dyj = dy_ref[:, j * LANES:(j + 1) * LANES]
            r1 = jnp.dot(dyj, cre_ref[j], preferred_element_type=F32)
            r2 = jnp.dot(dyj, cim_ref[j], preferred_element_type=F32)
            for q in range(SLAB):
                s = SLAB * j + q
                sd_re[pl.ds(s * pitch, tc), :] = r1[:, q * LANES:(q + 1) * LANES]
                sd_im[pl.ds(s * pitch, tc), :] = r2[:, q * LANES:(q + 1) * LANES]
                sx_re[pl.ds(s * pitch, tc), :] = xr_ref[s].astype(F32)
                sx_im[pl.ds(s * pitch, tc), :] = xi_ref[s].astype(F32)
        a_r = [ar_ref[g] for g in range(ng)]
        a_i = [ai_ref[g] for g in range(ng)]
        _scan_steps(tc, pitch, ng, True, a_r, a_i, (sd_re, sd_im), (sx_re, sx_im), out_re, out_im, st_re, st_im,
                    (acc_re, acc_im))
        for j in range(nb):
            cols = slice(j * LANES, (j + 1) * LANES)
            l1 = jnp.concatenate([out_re[pl.ds((SLAB * j + q) * pitch, tc), :] for q in range(SLAB)], axis=1).astype(BF16)
            l2 = jnp.concatenate([out_im[pl.ds((SLAB * j + q) * pitch, tc), :] for q in range(SLAB)], axis=1).astype(BF16)
            dyj = dy_ref[:, cols]
            du = (jnp.dot(l1, bre_ref[j], preferred_element_type=F32) + jnp.dot(l2, bim_ref[j], preferred_element_type=F32)
                  + d_ref[:, cols] * dyj.astype(F32))
            du_ref[:, cols] = du.astype(BF16)
            uj = u_ref[:, cols]
            gbr_ref[j] += lax.dot_general(uj, l1, dn, preferred_element_type=F32)
            gbi_ref[j] += lax.dot_general(uj, l2, dn, preferred_element_type=F32)
            x1 = jnp.concatenate([xr_ref[SLAB * j + q] for q in range(SLAB)], axis=1)
            x2 = jnp.concatenate([xi_ref[SLAB * j + q] for q in range(SLAB)], axis=1)
            gcr_ref[j] += lax.dot_general(dyj, x1, dn, preferred_element_type=F32)
            gci_ref[j] += lax.dot_general(dyj, x2, dn, preferred_element_type=F32)
        dar_ref[...] = acc_re[...]
        dai_ref[...] = acc_im[...]

    full3 = lambda s: pl.BlockSpec(s, lambda i: (0, 0, 0))
    xs = pl.BlockSpec((ns, tc, LANES), lambda i: (0, nt - 1 - i, 0))
    rows = pl.BlockSpec((tc, W), lambda i: (nt - 1 - i, 0))
    mat_a, mat_b = full3((nb, LANES, wide)), full3((nb, wide, LANES))
    vec = full3((ng, 8, LANES))
    return pl.pallas_call(
        body, name=name, grid=(nt,),
        in_specs=[rows, rows, xs, xs, mat_a, mat_a, mat_b, mat_b, pl.BlockSpec((1, W), lambda i: (0, 0)), vec, vec],
        out_specs=[rows, mat_a, mat_a, mat_a, mat_a, vec, vec],
        out_shape=[_sds((L, W), BF16)] + [_sds((nb, LANES, wide), F32)] * 4 + [_sds((ng, 8, LANES), F32)] * 2,
        scratch_shapes=[pltpu.VMEM((ns * pitch, LANES), F32)] * 6 + [pltpu.VMEM((ng, 8, LANES), F32)] * 4,
        compiler_params=_params(1),
    )(dyb, proj, xs_re, xs_im, cmt_re, cmt_im, bmt_re, bmt_im, dskip, a_re, a_im)


def _peer(k):
    x, y, c = lax.axis_index("x"), lax.axis_index("y"), lax.axis_index("c")
    px = 1 - x if (k >> 2) & 1 else x
    py = 1 - y if (k >> 1) & 1 else y
    pc = 1 - c if k & 1 else c
    return (px, py, pc), 4 * px + 2 * py + pc


def _window(ref, kind, idx, n):
    if kind == "col":
        w = ref.shape[1] // n
        return ref.at[:, pl.ds(pl.multiple_of(idx * w, LANES), w)]
    r = ref.shape[0] // n
    return ref.at[pl.ds(pl.multiple_of(idx * r, 8), r), :]


def _all_gather(name, shards, kinds):
    n = len(shards)
    fulls = []
    for s, kind in zip(shards, kinds):
        fulls.append(_sds((s.shape[0], s.shape[1] * N_DEV) if kind == "col" else (s.shape[0] * N_DEV, s.shape[1]), s.dtype))

    def body(*refs):
        src, dst = refs[:n], refs[n:2 * n]
        send, recv, loc = refs[2 * n:]
        me = 4 * lax.axis_index("x") + 2 * lax.axis_index("y") + lax.axis_index("c")
        copies = []
        for a in range(n):
            own = pltpu.make_async_copy(src[a], _window(dst[a], kinds[a], me, N_DEV), loc.at[a])
            own.start()
            copies.append(own)
        sends = []
        for k in range(1, N_DEV):
            dev, _ = _peer(k)
            for a in range(n):
                cp = pltpu.make_async_remote_copy(
                    src_ref=src[a], dst_ref=_window(dst[a], kinds[a], me, N_DEV),
                    send_sem=send.at[a * N_DEV + k], recv_sem=recv.at[a * N_DEV + k],
                    device_id=dev, device_id_type=MESH)
                cp.start()
                sends.append(cp)
        for k in range(1, N_DEV):
            dev, pidx = _peer(k)
            for a in range(n):
                pltpu.make_async_remote_copy(
                    src_ref=src[a], dst_ref=_window(dst[a], kinds[a], pidx, N_DEV),
                    send_sem=send.at[a * N_DEV + k], recv_sem=recv.at[a * N_DEV + k],
                    device_id=dev, device_id_type=MESH).wait_recv()
        for cp in sends:
            cp.wait_send()
        for cp in copies:
            cp.wait()

    any_ = pl.BlockSpec(memory_space=pl.ANY)
    return pl.pallas_call(
        body, name=name, in_specs=[any_] * n, out_specs=[any_] * n, out_shape=fulls,
        scratch_shapes=[pltpu.SemaphoreType.DMA((n * N_DEV,)), pltpu.SemaphoreType.DMA((n * N_DEV,)),
                        pltpu.SemaphoreType.DMA((n,))],
        compiler_params=pltpu.CompilerParams(has_side_effects=True),
    )(*shards)


_HBM = pl.BlockSpec(memory_space=pltpu.HBM)
_SEM = pl.BlockSpec(memory_space=pltpu.SEMAPHORE)
_ANY = pl.BlockSpec(memory_space=pl.ANY)
_EFFECT = pltpu.SideEffectType.DATAFLOW_SIDE_EFFECTING


def _xfer_refs(mode, kinds, a, src, dst, me, pidx):
    if mode == "gather":
        return src[a], _window(dst[a], kinds[a], me, N_DEV), _window(dst[a], kinds[a], pidx, N_DEV)
    return _window(src[a], kinds[a], pidx, N_DEV), dst[a].at[me], dst[a].at[pidx]


def _xfer_out_shapes(mode, arrs, kinds):
    outs = []
    for s, kind in zip(arrs, kinds):
        if mode == "gather":
            outs.append((s.shape[0], s.shape[1] * N_DEV) if kind == "col" else (s.shape[0] * N_DEV, s.shape[1]))
        else:
            outs.append((N_DEV,) + ((s.shape[0], s.shape[1] // N_DEV) if kind == "col" else (s.shape[0] // N_DEV, s.shape[1])))
    return outs


def _xfer_start(name, mode, arrs, kinds, after):
    n = len(arrs)
    shapes = _xfer_out_shapes(mode, arrs, kinds)

    def body(*refs):
        src, dst = refs[:n], refs[n:2 * n]
        send, recv = refs[2 * n + 1], refs[2 * n + 2]
        token, loc = refs[2 * n + 3 + 2 * n], refs[2 * n + 4 + 2 * n]
        me = 4 * lax.axis_index("x") + 2 * lax.axis_index("y") + lax.axis_index("c")
        own = []
        for a in range(n):
            s, _, d = _xfer_refs(mode, kinds, a, src, dst, me, me)
            own.append(pltpu.make_async_copy(s, d, loc.at[a]))
            own[-1].start()
        for cp in own:
            cp.wait()
        for k in range(1, N_DEV):
            dev, pidx = _peer(k)
            for a in range(n):
                s, d, _ = _xfer_refs(mode, kinds, a, src, dst, me, pidx)
                pltpu.make_async_remote_copy(src_ref=s, dst_ref=d, send_sem=send.at[a * N_DEV + k],
                                             recv_sem=recv.at[a * N_DEV + k], device_id=dev, device_id_type=MESH).start()
        token[...] = jnp.zeros(token.shape, F32)

    lands = [pltpu.with_memory_space_constraint(lax.empty(shp, s.dtype), pltpu.HBM) for shp, s in zip(shapes, arrs)]
    srcs = [pltpu.with_memory_space_constraint(s, pltpu.HBM) for s in arrs]
    res = pl.pallas_call(
        body, name=name,
        in_specs=[_HBM] * (2 * n) + [_ANY],
        out_specs=[_SEM, _SEM] + [_HBM] * (2 * n) + [pl.BlockSpec(memory_space=pltpu.VMEM)],
        out_shape=[pltpu.SemaphoreType.DMA((n * N_DEV,)), pltpu.SemaphoreType.DMA((n * N_DEV,))]
        + [pltpu.HBM(s.shape, s.dtype) for s in arrs] + [pltpu.HBM(shp, s.dtype) for shp, s in zip(shapes, arrs)]
        + [_sds((8, LANES), F32)],
        input_output_aliases={i: 2 + i for i in range(2 * n)},
        scratch_shapes=[pltpu.SemaphoreType.DMA((n,))],
        compiler_params=pltpu.CompilerParams(has_side_effects=_EFFECT),
    )(*srcs, *lands, after)
    return dict(mode=mode, kinds=kinds, n=n, send=res[0], recv=res[1], srcs=res[2:2 + n], lands=res[2 + n:2 + 2 * n]), res[-1]


def _xfer_wait(name, st, after):
    n, mode, kinds = st["n"], st["mode"], st["kinds"]

    def body(*refs):
        src, dst = refs[:n], refs[n:2 * n]
        send, recv = refs[2 * n], refs[2 * n + 1]
        me = 4 * lax.axis_index("x") + 2 * lax.axis_index("y") + lax.axis_index("c")
        for k in range(1, N_DEV):
            dev, pidx = _peer(k)
            for a in range(n):
                s, d, land = _xfer_refs(mode, kinds, a, src, dst, me, pidx)
                cp = pltpu.make_async_remote_copy(src_ref=s, dst_ref=land, send_sem=send.at[a * N_DEV + k],
                                                  recv_sem=recv.at[a * N_DEV + k], device_id=dev, device_id_type=MESH)
                cp.wait_send()
                cp.wait_recv()

    res = pl.pallas_call(
        body, name=name,
        in_specs=[_HBM] * (2 * n) + [_SEM, _SEM, _ANY],
        out_specs=[_HBM] * (2 * n),
        out_shape=[pltpu.HBM(s.shape, s.dtype) for s in st["srcs"]] + [pltpu.HBM(s.shape, s.dtype) for s in st["lands"]],
        input_output_aliases={i: i for i in range(2 * n)},
        compiler_params=pltpu.CompilerParams(has_side_effects=_EFFECT),
    )(*st["srcs"], *st["lands"], st["send"], st["recv"], after)
    return list(res[n:])


def _sc_xfer(name, mode, arrs, kinds, collective_id):
    n = len(arrs)
    shapes = _xfer_out_shapes(mode, arrs, kinds)
    hbm = pltpu.MemorySpace.HBM
    src = [jax.new_ref(a, memory_space=hbm) for a in arrs]
    dst = [jax.empty_ref(_sds(shp, a.dtype), memory_space=hbm) for shp, a in zip(shapes, arrs)]

    @pl.kernel(mesh=plsc.ScalarSubcoreMesh(axis_name="seq", num_cores=1), name=name,
               scratch_types=(pltpu.SemaphoreType.DMA((n * N_DEV,)), pltpu.SemaphoreType.DMA((n * N_DEV,)),
                              pltpu.SemaphoreType.DMA((n,))),
               compiler_params=pltpu.CompilerParams(collective_id=collective_id))
    def launch(send, recv, loc):
        barrier = pltpu.get_barrier_semaphore()
        for k in range(1, N_DEV):
            pl.semaphore_signal(barrier, inc=1, device_id=_peer(k)[0], device_id_type=MESH)
        pl.semaphore_wait(barrier, N_DEV - 1)
        me = 4 * lax.axis_index("x") + 2 * lax.axis_index("y") + lax.axis_index("c")
        own, sends = [], []
        for a in range(n):
            s, _, d = _xfer_refs(mode, kinds, a, src, dst, me, me)
            own.append(pltpu.make_async_copy(s, d, loc.at[a]))
            own[-1].start()
        for k in range(1, N_DEV):
            dev, pidx = _peer(k)
            for a in range(n):
                s, d, _ = _xfer_refs(mode, kinds, a, src, dst, me, pidx)
                sends.append(pltpu.make_async_remote_copy(src_ref=s, dst_ref=d, send_sem=send.at[a * N_DEV + k],
                                                          recv_sem=recv.at[a * N_DEV + k], device_id=dev, device_id_type=MESH))
                sends[-1].start()
        for cp in own:
            cp.wait()
        for k in range(1, N_DEV):
            dev, pidx = _peer(k)
            for a in range(n):
                s, _, land = _xfer_refs(mode, kinds, a, src, dst, me, pidx)
                pltpu.make_async_remote_copy(src_ref=s, dst_ref=land, send_sem=send.at[a * N_DEV + k],
                                             recv_sem=recv.at[a * N_DEV + k], device_id=dev, device_id_type=MESH).wait_recv()
        for cp in sends:
            cp.wait_send()

    launch()
    return [d[...] for d in dst]


def _sc_gather(name, arrs, kinds, collective_id):
    n = len(arrs)
    pairs = 7
    shapes = _xfer_out_shapes("gather", arrs, kinds)
    hbm = pltpu.MemorySpace.HBM
    src = [jax.new_ref(a, memory_space=hbm) for a in arrs]
    dst = [jax.empty_ref(_sds(shp, a.dtype), memory_space=hbm) for shp, a in zip(shapes, arrs)]

    @pl.kernel(mesh=plsc.ScalarSubcoreMesh(axis_name="seq", num_cores=1), name=name,
               scratch_types=(pltpu.SemaphoreType.DMA((n * pairs,)), pltpu.SemaphoreType.DMA((n * pairs,)),
                              pltpu.SemaphoreType.DMA((n,))),
               compiler_params=pltpu.CompilerParams(collective_id=collective_id))
    def launch(send, recv, loc):
        x, y, c = lax.axis_index("x"), lax.axis_index("y"), lax.axis_index("c")
        me = 4 * x + 2 * y + c
        sib = (x, y, 1 - c)
        chips = []
        for fx, fy in ((1, 0), (0, 1), (1, 1)):
            px, py = (1 - x if fx else x), (1 - y if fy else y)
            chips.append(((px, py, c), 4 * px + 2 * py + c, 4 * px + 2 * py + (1 - c)))
        barrier = pltpu.get_barrier_semaphore()
        for dev in [sib] + [ch[0] for ch in chips]:
            pl.semaphore_signal(barrier, inc=1, device_id=dev, device_id_type=MESH)
        pl.semaphore_wait(barrier, 4)

        def win(a, idx):
            return _window(dst[a], kinds[a], idx, N_DEV)

        def rcopy(a, p, s, d, dev):
            return pltpu.make_async_remote_copy(src_ref=s, dst_ref=d, send_sem=send.at[a * pairs + p],
                                                recv_sem=recv.at[a * pairs + p], device_id=dev, device_id_type=MESH)

        own, sends = [], []
        for a in range(n):
            own.append(pltpu.make_async_copy(src[a], win(a, me), loc.at[a]))
            own[-1].start()
        for j, (dev, _, _) in enumerate(chips):
            for a in range(n):
                sends.append(rcopy(a, 1 + j, src[a], win(a, me), dev))
                sends[-1].start()
        for a in range(n):
            sends.append(rcopy(a, 0, src[a], win(a, me), sib))
            sends[-1].start()
        for j, (dev, idx, _) in enumerate(chips):
            for a in range(n):
                rcopy(a, 1 + j, src[a], win(a, idx), dev).wait_recv()
                sends.append(rcopy(a, 4 + j, win(a, idx), win(a, idx), sib))
                sends[-1].start()
        for cp in own:
            cp.wait()
        for a in range(n):
            rcopy(a, 0, src[a], win(a, 4 * x + 2 * y + (1 - c)), sib).wait_recv()
        for j, (_, _, sidx) in enumerate(chips):
            for a in range(n):
                rcopy(a, 4 + j, src[a], win(a, sidx), sib).wait_recv()
        for cp in sends:
            cp.wait_send()

    launch()
    return [d[...] for d in dst]


_SEQ_IDS = {"gather_in": 7, "gather_mix": 1, "gather_ffn": 2, "grads_ffn": 3, "grads_mix": 4, "grads_small": 5, "grads_in": 6}


def _seq_start(name, mode, arrs, kinds, after):
    arrs = list(arrs)
    if after is not None:
        after, *arrs = lax.optimization_barrier((after, *arrs))
    if mode == "gather":
        return _sc_gather(name, arrs, kinds, _SEQ_IDS[name]), None
    return _sc_xfer(name, mode, arrs, kinds, _SEQ_IDS[name]), None


def _seq_wait(name, res, after):
    del name, after
    return list(res)


def _adamw(name, parts, w, m, v):
    P, R, C = parts.shape
    sub = 16 if parts.dtype == BF16 else 8
    tr = R if R * C <= (1 << 18) else _tile(R, max(sub, (1 << 18) // C), sub)

    def body(p_ref, w_ref, m_ref, v_ref, g_ref, d_ref, nm_ref, nv_ref):
        g = p_ref[0].astype(F32)
        for s in range(1, P):
            g = g + p_ref[s].astype(F32)
        m2 = ADAM_B1 * m_ref[...] + (1.0 - ADAM_B1) * g
        v2 = ADAM_B2 * v_ref[...] + (1.0 - ADAM_B2) * (g * g)
        m_hat = m2 / (1.0 - ADAM_B1 ** ADAM_STEP)
        v_hat = v2 / (1.0 - ADAM_B2 ** ADAM_STEP)
        g_ref[...] = g
        d_ref[...] = -ADAM_LR * (m_hat / (jnp.sqrt(v_hat) + ADAM_EPS) + ADAM_WD * w_ref[...])
        nm_ref[...] = m2
        nv_ref[...] = v2

    sp = pl.BlockSpec((tr, C), lambda i: (i, 0))
    return pl.pallas_call(
        body, name=name, grid=(R // tr,),
        in_specs=[pl.BlockSpec((P, tr, C), lambda i: (0, i, 0)), sp, sp, sp], out_specs=[sp] * 4,
        out_shape=[_sds((R, C), F32)] * 4, compiler_params=_params(1),
    )(parts, w, m, v)


def _pack(arrs, row_mult=8):
    pieces, total = [], 0
    for a in arrs:
        f = a.reshape(-1).astype(F32)
        pad = (-f.shape[0]) % (8 * LANES)
        pieces.append(jnp.pad(f, (0, pad)) if pad else f)
        total += f.shape[0] + pad
    tail = (-total) % (row_mult * LANES)
    if tail:
        pieces.append(jnp.zeros((tail,), F32))
    return jnp.concatenate(pieces).reshape(-1, LANES)


def _unpack(buf, shapes, lead=()):
    out, row = [], 0
    for shp in shapes:
        size = 1
        for d in shp:
            size *= d
        rows = -(-size // (8 * LANES)) * 8
        piece = buf[..., row:row + rows, :].reshape(lead + (rows * LANES,))[..., :size]
        out.append(piece.reshape(lead + tuple(shp)))
        row += rows
    return out


def kernel(x, norm_tok, w_in, a_re, a_im, log_dt, b_re, b_im, c_re, c_im, d_skip, w_glu, w_ssm_out, conv_w, conv_b, w_conv_out, w_o, norm_ffn, w_up, ffn_conv_w, ffn_conv_b, w_down, norm_final, loss_target, m_norm_tok, m_w_in, m_a_re, m_a_im, m_log_dt, m_b_re, m_b_im, m_c_re, m_c_im, m_d_skip, m_w_glu, m_w_ssm_out, m_conv_w, m_conv_b, m_w_conv_out, m_w_o, m_norm_ffn, m_w_up, m_ffn_conv_w, m_ffn_conv_b, m_w_down, m_norm_final, v_norm_tok, v_w_in, v_a_re, v_a_im, v_log_dt, v_b_re, v_b_im, v_c_re, v_c_im, v_d_skip, v_w_glu, v_w_ssm_out, v_conv_w, v_conv_b, v_w_conv_out, v_w_o, v_norm_ffn, v_w_up, v_ffn_conv_w, v_ffn_conv_b, v_w_down, v_norm_final):
    args = dict(locals())
    L, D = x.shape[1], x.shape[2]
    G, P, H = b_re.shape[1], b_re.shape[2], b_re.shape[3]
    SW = G * H
    CW = conv_b.shape[1]
    FF = ffn_conv_b.shape[1]
    GP = G * P
    nb = SW // LANES
    gpb = LANES // H
    me = 4 * lax.axis_index("x") + 2 * lax.axis_index("y") + lax.axis_index("c")
    tm = _tile(L, 256, 16)
    x2 = x[0]
    tgt = loss_target[0]

    big = [("w_in", "col"), ("w_glu", "row"), ("w_ssm_out", "col"), ("w_conv_out", "col"), ("w_o", "row"),
           ("w_up", "col"), ("w_down", "row")]
    shards = [_cast_bf16("cast_" + n, args[n][0]) for n, _ in big]
    small_in = _pack([conv_w[0], ffn_conv_w[0]])
    kind = dict(big)
    mixw, ffnw = ["w_glu", "w_ssm_out", "w_conv_out", "w_o"], ["w_up", "w_down"]
    shard = dict(zip([n for n, _ in big], shards))
    gathered, _ = _seq_start("gather_in", "gather", [shard["w_in"], small_in], ["col", "row"], None)
    W = {"w_in": gathered[0]}
    st_mix, tok_mix = _seq_start("gather_mix", "gather", [shard[n] for n in mixw], [kind[n] for n in mixw], None)
    st_ffn, tok_ffn = _seq_start("gather_ffn", "gather", [shard[n] for n in ffnw], [kind[n] for n in ffnw], None)
    cw_parts, fcw_parts = _unpack(gathered[-1].reshape(N_DEV, -1, LANES), [conv_w.shape[1:], ffn_conv_w.shape[1:]], (N_DEV,))
    conv_w_full = jnp.moveaxis(cw_parts, 0, 1).reshape(3, CW)
    ffn_conv_w_full = jnp.moveaxis(fcw_parts, 0, 1).reshape(3, FF)

    ar_row, ai_row = a_re.reshape(1, GP), a_im.reshape(1, GP)
    ldt_row = jnp.broadcast_to(log_dt.reshape(G, 1), (G, P)).reshape(1, GP)
    brt = jnp.transpose(b_re[0], (2, 0, 1)).reshape(H, GP)
    bit = jnp.transpose(b_im[0], (2, 0, 1)).reshape(H, GP)
    abar_re, abar_im, bbar_re, bbar_im = _prep_fwd("s5_prep", ar_row, ai_row, ldt_row, brt, bit)
    eye = jnp.eye(gpb, dtype=F32)

    def b_blocks(bt):
        return jnp.einsum("ab,hjbp->jahbp", eye, bt.reshape(H, nb, gpb, P)).reshape(nb, LANES, gpb * P)

    def c_blocks(c):
        return jnp.einsum("ab,jahp->jbpah", eye, c.reshape(nb, gpb, H, P)).reshape(nb, gpb * P, LANES)

    def diag_blocks(mat):
        return jnp.einsum("jahap->hjap", mat.reshape(nb, gpb, H, gpb, P))

    bm_re, bm_im = b_blocks(bbar_re), b_blocks(bbar_im)
    cm_re, cm_im = c_blocks(c_re[0]), -c_blocks(c_im[0])
    a3_re, a3_im = abar_re.reshape(-1, 8, LANES), abar_im.reshape(-1, 8, LANES)
    dskip_row = d_skip.reshape(1, SW)

    cbs = SW // LANES
    cb_v, cb_gb, cb_gc = cbs, cbs + CW // LANES, cbs + 2 * CW // LANES
    cb_ma = (SW + 3 * CW) // D
    xn = _rms_fwd("rms_tok", x2, norm_tok, tm)
    proj = _mm("proj", xn, W["w_in"], "nn", out_dtype=BF16)
    xs_re, xs_im, y, ya = _s5_fwd("s5_fwd", proj, bm_re.astype(BF16), bm_im.astype(BF16), cm_re.astype(BF16),
                                  cm_im.astype(BF16), dskip_row, a3_re, a3_im)
    W.update(zip(mixw, _seq_wait("gather_mix", st_mix, ya)))
    g1 = _mm("glu_gate", ya, W["w_glu"], "nn", out_dtype=BF16)
    ya2 = _glu_fwd("glu", y, g1, tm)
    za = _mm("ssm_out", ya2, W["w_ssm_out"], "nn", out_dtype=BF16)
    q = _convb_fwd("convb", proj, cb_v, cb_gb, cb_gc, conv_w_full, conv_b)
    zb = _mm("conv_out", q, W["w_conv_out"], "nn", out_dtype=BF16)
    merged = _merge_fwd("merge", proj, cb_ma, cb_ma + 1, za, zb, tm)
    o1 = _mm("mix_out", merged, W["w_o"], "nn")
    h1, hn = _res_rms_fwd("rms_ffn", x2, o1, norm_ffn, tm)
    W.update(zip(ffnw, _seq_wait("gather_ffn", st_ffn, hn)))
    hh = _mm("ffn_up", hn, W["w_up"], "nn", out_dtype=BF16)
    f = _ffn_fwd("ffn_act", hh, ffn_conv_w_full, ffn_conv_b)
    o2 = _mm("ffn_down", f, W["w_down"], "nn", tk=2816)
    dh2, dh2b, g_norm_final, loss_part = _final("final", h1, o2, norm_final.reshape(1, D), tgt, tm)

    df = _mm("d_ffn_act", dh2b, W["w_down"], "nt", tn=1408, out_dtype=BF16)
    gw_down = _mm("gw_down", f, dh2b, "tn", out_dtype=BF16, tm=1408, tn=512, tk=L)
    dhh, g_ffn_conv_w, g_ffn_conv_b = _ffn_bwd("ffn_act_bwd", hh, ffn_conv_w_full, ffn_conv_b, df)
    nhalf = lambda t: FF // t
    dhn = _mm("d_ffn_in", dhh, W["w_up"], "nt", out_dtype=BF16, tk=_tile(FF, 2816), dims=(L, D, 2 * FF),
              a_spec=lambda a, b, c: pl.BlockSpec((None, a, c), lambda i, j, k: (k // nhalf(c), i, k % nhalf(c))))
    gw_up = _mm("gw_up", hn, dhh, "tn", out_dtype=BF16, tn=_tile(FF, 1024), tk=L, dims=(D, 2 * FF, L),
                b_spec=lambda a, b, c: pl.BlockSpec((None, c, b), lambda i, j, k: (j // nhalf(b), k, j % nhalf(b))))
    dhn, gw_up, gw_down = lax.optimization_barrier((dhn, gw_up, gw_down))
    st_gffn, tok_gffn = _seq_start("grads_ffn", "exchange", [gw_up, gw_down], [kind[n] for n in ffnw], None)
    dh1, dh1b, g_norm_ffn = _rms_bwd("rms_ffn_bwd", dhn, h1, norm_ffn, dh2, tm, True)

    dmerged = _mm("d_merged", dh1b, W["w_o"], "nt", out_dtype=BF16)
    gw_o = _mm("gw_o", merged, dh1b, "tn", out_dtype=BF16, tk=L)
    dmerged, gw_o = lax.optimization_barrier((dmerged, gw_o))
    dza, dzb, dma, dmb = _merge_bwd("merge_bwd", proj, cb_ma, cb_ma + 1, za, zb, dmerged, tm)
    dq = _mm("d_q", dzb, W["w_conv_out"], "nt", out_dtype=BF16)
    gw_conv_out = _mm("gw_conv_out", q, dzb, "tn", out_dtype=BF16, tk=L)
    dq, gw_conv_out = lax.optimization_barrier((dq, gw_conv_out))
    dv, dgb, dgc, g_conv_w, g_conv_b = _convb_bwd("convb_bwd", proj, cb_v, cb_gb, cb_gc, conv_w_full, conv_b, dq)
    dya2 = _mm("d_ya2", dza, W["w_ssm_out"], "nt", out_dtype=BF16)
    gw_ssm_out = _mm("gw_ssm_out", ya2, dza, "tn", out_dtype=BF16, tk=L)
    dya2, gw_ssm_out = lax.optimization_barrier((dya2, gw_ssm_out))
    dy_direct, dg1 = _glu_bwd("glu_bwd", y, g1, dya2, tm)
    dya_g = _mm("d_ya_gate", dg1, W["w_glu"], "nt", out_dtype=BF16)
    gw_glu = _mm("gw_glu", ya, dg1, "tn", out_dtype=BF16, tk=L)
    dya_g, gw_glu = lax.optimization_barrier((dya_g, gw_glu))
    st_gmix, tok_gmix = _seq_start("grads_mix", "exchange", [gw_glu, gw_ssm_out, gw_conv_out, gw_o],
                                   [kind[n] for n in mixw], None)
    dyb, g_dskip = _gelu_bwd("gelu_bwd", y, dy_direct, dya_g, proj, dskip_row, tm)
    swap = lambda m: jnp.swapaxes(m, 1, 2).astype(BF16)
    du, gb_re, gb_im, gc_re, gc_im, dab_re, dab_im = _s5_bwd(
        "s5_bwd", dyb, proj, xs_re, xs_im, swap(cm_re), swap(cm_im), swap(bm_re), swap(bm_im), dskip_row, a3_re, a3_im)
    parts = dict(zip(ffnw, _seq_wait("grads_ffn", st_gffn, dab_re)))
    g_ar, g_ai, g_ldt, g_brt, g_bit = _prep_bwd(
        "s5_prep_bwd", ar_row, ai_row, ldt_row, brt, bit, dab_re.reshape(1, GP), dab_im.reshape(1, GP),
        diag_blocks(gb_re).reshape(H, GP), diag_blocks(gb_im).reshape(H, GP))
    small = dict(
        a_re=g_ar.reshape(1, G, P), a_im=g_ai.reshape(1, G, P),
        log_dt=g_ldt.reshape(G, P).sum(axis=1).reshape(1, G),
        b_re=jnp.transpose(g_brt.reshape(H, G, P), (1, 2, 0))[None], b_im=jnp.transpose(g_bit.reshape(H, G, P), (1, 2, 0))[None],
        c_re=jnp.transpose(diag_blocks(gc_re), (1, 2, 0, 3)).reshape(1, G, H, P),
        c_im=-jnp.transpose(diag_blocks(gc_im), (1, 2, 0, 3)).reshape(1, G, H, P),
        d_skip=g_dskip.reshape(1, G, H), conv_b=g_conv_b, norm_ffn=g_norm_ffn, ffn_conv_b=g_ffn_conv_b,
        norm_final=g_norm_final.reshape(D), conv_w=g_conv_w[None], ffn_conv_w=g_ffn_conv_w[None])
    rep = ["a_re", "a_im", "log_dt", "b_re", "b_im", "c_re", "c_im", "d_skip", "conv_b", "norm_ffn", "ffn_conv_b", "norm_final"]
    order = rep + ["conv_w", "ffn_conv_w"]
    full_shapes = {n: args[n].shape for n in rep}
    full_shapes["conv_w"], full_shapes["ffn_conv_w"] = (1, 3, CW), (1, 3, FF)
    rep_pack = _pack([small[n] for n in rep], LANES)
    rep_rows = rep_pack.shape[0]
    gpack = jnp.concatenate([loss_part, rep_pack, _pack([small["conv_w"], small["ffn_conv_w"]])], axis=0)
    rep0 = loss_part.shape[0]
    rows = gpack.shape[0]
    du, gpack = lax.optimization_barrier((du, gpack))
    st_small, tok_small = _seq_start("grads_small", "gather", [gpack], ["row"], None)

    dproj = _concat_cols("dproj", [du, dv, dgb, dgc, dma, dmb], tm)
    gw_in = _mm("gw_in", xn, dproj, "tn", out_dtype=BF16, tk=L, dep=tok_small)
    dproj, gw_in = lax.optimization_barrier((dproj, gw_in))
    st_gin, tok_gin = _seq_start("grads_in", "exchange", [gw_in], ["col"], None)
    dxn = _mm("d_xn", dproj, W["w_in"], "nt", out_dtype=BF16)
    grad_x, g_norm_tok = _rms_bwd("rms_tok_bwd", dxn, x2, norm_tok, dh1, tm, False)

    res = {}

    def big_update(n):
        res[n] = [r[None] for r in _adamw("adamw_" + n, parts[n], args[n][0], args["m_" + n][0], args["v_" + n][0])]

    def after(xs, dep):
        return lax.optimization_barrier((list(xs), dep))[0]

    parts = dict(zip(ffnw, after([parts[n] for n in ffnw], grad_x)))
    for n in ffnw:
        big_update(n)
    parts.update(zip(mixw, after(_seq_wait("grads_mix", st_gmix, grad_x), [res[n][1] for n in ffnw])))
    for n in mixw:
        big_update(n)
    gall = after(_seq_wait("grads_small", st_small, None), [res[n][1] for n in mixw])[0].reshape(N_DEV, rows, LANES)
    gcw, gfcw = _unpack(gall[:, rep0 + rep_rows:], [full_shapes["conv_w"], full_shapes["ffn_conv_w"]], (N_DEV,))
    cws, fcws = CW // N_DEV, FF // N_DEV
    gcw = lax.dynamic_slice_in_dim(gcw[:, 0], me * cws, cws, axis=2)
    gfcw = lax.dynamic_slice_in_dim(gfcw[:, 0], me * fcws, fcws, axis=2)
    res["conv_w"] = [r[None] for r in _adamw("adamw_conv_w", gcw, conv_w[0], m_conv_w[0], v_conv_w[0])]
    res["ffn_conv_w"] = [r[None] for r in _adamw("adamw_ffn_conv_w", gfcw, ffn_conv_w[0], m_ffn_conv_w[0], v_ffn_conv_w[0])]
    rep_out = _adamw("adamw_small", gall[:, rep0:rep0 + rep_rows], _pack([args[n] for n in rep], LANES),
                     _pack([args["m_" + n] for n in rep], LANES), _pack([args["v_" + n] for n in rep], LANES))
    rep_out = [_unpack(r, [full_shapes[n] for n in rep]) for r in rep_out]
    for i, n in enumerate(rep):
        res[n] = [r[i] for r in rep_out]
    nt_pack = after([_pack([g_norm_tok])], [res[n][1] for n in ("a_re", "conv_w", "ffn_conv_w")])
    nt_all = _all_gather("gather_norm_tok_grad", nt_pack, ["row"])[0].reshape(N_DEV, -1, LANES)
    nt_out = _adamw("adamw_norm_tok", nt_all, _pack([norm_tok]), _pack([m_norm_tok]), _pack([v_norm_tok]))
    res["norm_tok"] = [_unpack(r, [norm_tok.shape])[0] for r in nt_out]
    parts["w_in"] = after(_seq_wait("grads_in", st_gin, None), nt_out[0])[0]
    big_update("w_in")

    loss = jnp.sum(gall[:, 0, 0])
    names = ["norm_tok", "w_in", "a_re", "a_im", "log_dt", "b_re", "b_im", "c_re", "c_im", "d_skip", "w_glu", "w_ssm_out",
             "conv_w", "conv_b", "w_conv_out", "w_o", "norm_ffn", "w_up", "ffn_conv_w", "ffn_conv_b", "w_down", "norm_final"]
    out = [loss, grad_x[None]]
    for slot in range(4):
        out += [res[n][slot] for n in names]
    return tuple(out)
```

```python
import functools

import jax
import jax.numpy as jnp
from jax import lax
from jax.experimental import pallas as pl
from jax.experimental.pallas import tpu as pltpu
from jax.experimental.pallas import tpu_sc as plsc

F32 = jnp.float32
BF16 = jnp.bfloat16
N_DEV = 8
LANES = 128
SLAB = 4
EPS = 1e-6
ADAM_LR = 0.001
ADAM_B1 = 0.9
ADAM_B2 = 0.999
ADAM_EPS = 1e-08
ADAM_WD = 0.01
ADAM_STEP = 10
VMEM_LIMIT = 56 * 1024 * 1024
MESH = pl.DeviceIdType.MESH


def _tile(n, pref, mult=LANES):
    best = None
    t = mult
    while t <= min(n, pref):
        if n % t == 0:
            best = t
        t += mult
    return best if best is not None else n


def _params(ndim):
    return pltpu.CompilerParams(dimension_semantics=("arbitrary",) * ndim, vmem_limit_bytes=VMEM_LIMIT)


def _sds(shape, dtype):
    return jax.ShapeDtypeStruct(tuple(shape), dtype)


def _mm(name, a, b, mode, *, out_dtype=F32, tm=1024, tn=1024, tk=2048, dims=None, a_spec=None, b_spec=None, dep=None):
    if dims is None:
        if mode == "nn":
            (M, K), N = a.shape, b.shape[1]
        elif mode == "nt":
            (M, K), N = a.shape, b.shape[0]
        else:
            (K, M), N = a.shape, b.shape[1]
    else:
        M, N, K = dims
    tm, tn, tk = _tile(M, tm), _tile(N, tn), _tile(K, tk)
    nk = K // tk
    if mode == "nn":
        dn = (((1,), (0,)), ((), ()))
        sa = pl.BlockSpec((tm, tk), lambda i, j, k: (i, k))
        sb = pl.BlockSpec((tk, tn), lambda i, j, k: (k, j))
    elif mode == "nt":
        dn = (((1,), (1,)), ((), ()))
        sa = pl.BlockSpec((tm, tk), lambda i, j, k: (i, k))
        sb = pl.BlockSpec((tn, tk), lambda i, j, k: (j, k))
    else:
        dn = (((0,), (0,)), ((), ()))
        sa = pl.BlockSpec((tk, tm), lambda i, j, k: (k, i))
        sb = pl.BlockSpec((tk, tn), lambda i, j, k: (k, j))
    sa = a_spec(tm, tn, tk) if a_spec is not None else sa
    sb = b_spec(tm, tn, tk) if b_spec is not None else sb
    use_acc = nk > 1 and out_dtype != F32

    deps = [] if dep is None else [dep]

    def body(a_ref, b_ref, *rest):
        o_ref, acc = rest[len(deps)], rest[len(deps) + 1:]
        k = pl.program_id(2)
        p = lax.dot_general(a_ref[...], b_ref[...], dn, preferred_element_type=F32)
        if nk == 1:
            o_ref[...] = p.astype(out_dtype)
        else:
            tgt = acc[0] if use_acc else o_ref

            @pl.when(k == 0)
            def _():
                tgt[...] = p

            @pl.when(k > 0)
            def _():
                tgt[...] += p

            if use_acc:
                @pl.when(k == nk - 1)
                def _():
                    o_ref[...] = acc[0][...].astype(out_dtype)

    return pl.pallas_call(
        body, name=name, grid=(M // tm, N // tn, nk),
        in_specs=[sa, sb] + [pl.BlockSpec(memory_space=pl.ANY)] * len(deps),
        out_specs=pl.BlockSpec((tm, tn), lambda i, j, k: (i, j)),
        out_shape=_sds((M, N), out_dtype),
        scratch_shapes=[pltpu.VMEM((tm, tn), F32)] if use_acc else [],
        compiler_params=_params(3),
    )(a, b, *deps)


def _rows(name, body, L, tm, ins, outs):
    return pl.pallas_call(
        body, name=name, grid=(L // tm,),
        in_specs=[s for _, s in ins], out_specs=[s for _, s in outs],
        out_shape=[o for o, _ in outs], compiler_params=_params(1),
    )(*[a for a, _ in ins])


def _rs(tm, w, cb=0):
    return pl.BlockSpec((tm, w), lambda i: (i, cb))


def _fs(shape):
    return pl.BlockSpec(tuple(shape), lambda i: (0,) * len(shape))


def _acc_rows(i, ref, part):
    @pl.when(i == 0)
    def _():
        ref[...] = part

    @pl.when(i > 0)
    def _():
        ref[...] += part


def _cast_bf16(name, w):
    R, C = w.shape
    tr = _tile(R, max(16, (1 << 20) // C), 16)

    def body(w_ref, o_ref):
        o_ref[...] = w_ref[...].astype(BF16)

    return _rows(name, body, R, tr, [(w, _rs(tr, C))], [(_sds((R, C), BF16), _rs(tr, C))])[0]


def _concat_cols(name, pieces, tm):
    L = pieces[0].shape[0]
    widths = [p.shape[1] for p in pieces]

    def body(*refs):
        o_ref, off = refs[-1], 0
        for p_ref, w in zip(refs[:-1], widths):
            o_ref[:, off:off + w] = p_ref[...]
            off += w

    return _rows(name, body, L, tm, [(p, _rs(tm, w)) for p, w in zip(pieces, widths)],
                 [(_sds((L, sum(widths)), pieces[0].dtype), _rs(tm, sum(widths)))])[0]


def _rms_fwd(name, x, g, tm):
    L, D = x.shape

    def body(x_ref, g_ref, o_ref):
        xv = x_ref[...]
        r = lax.rsqrt(jnp.mean(xv * xv, axis=-1, keepdims=True) + EPS)
        o_ref[...] = (xv * r * g_ref[...]).astype(BF16)

    return _rows(name, body, L, tm, [(x, _rs(tm, D)), (g, _fs((1, D)))], [(_sds((L, D), BF16), _rs(tm, D))])[0]


def _res_rms_fwd(name, x, o, g, tm):
    L, D = x.shape

    def body(x_ref, o_ref, g_ref, h_ref, hn_ref):
        h = x_ref[...] + o_ref[...]
        r = lax.rsqrt(jnp.mean(h * h, axis=-1, keepdims=True) + EPS)
        h_ref[...] = h
        hn_ref[...] = (h * r * g_ref[...]).astype(BF16)

    return _rows(name, body, L, tm, [(x, _rs(tm, D)), (o, _rs(tm, D)), (g, _fs((1, D)))],
                 [(_sds((L, D), F32), _rs(tm, D)), (_sds((L, D), BF16), _rs(tm, D))])


def _rms_bwd(name, dn, h, g, dres, tm, with_bf16):
    L, D = h.shape

    def body(dn_ref, h_ref, g_ref, dres_ref, dh_ref, *rest):
        i = pl.program_id(0)
        h = h_ref[...]
        r = lax.rsqrt(jnp.mean(h * h, axis=-1, keepdims=True) + EPS)
        xh = h * r
        d = dn_ref[...].astype(F32)
        dxh = d * g_ref[...]
        dh = dres_ref[...] + r * (dxh - xh * jnp.mean(dxh * xh, axis=-1, keepdims=True))
        dh_ref[...] = dh
        if with_bf16:
            rest[0][...] = dh.astype(BF16)
        _acc_rows(i, rest[-1], jnp.sum(d * xh, axis=0, keepdims=True))

    outs = [(_sds((L, D), F32), _rs(tm, D))]
    if with_bf16:
        outs.append((_sds((L, D), BF16), _rs(tm, D)))
    outs.append((_sds((1, D), F32), _fs((1, D))))
    return _rows(name, body, L, tm, [(dn, _rs(tm, D)), (h, _rs(tm, D)), (g, _fs((1, D))), (dres, _rs(tm, D))], outs)


def _final(name, h1, o2, g, tgt, tm):
    L, D = h1.shape

    def body(h1_ref, o2_ref, g_ref, t_ref, dh_ref, dhb_ref, dg_ref, loss_ref):
        i = pl.program_id(0)
        h = h1_ref[...] + o2_ref[...]
        r = lax.rsqrt(jnp.mean(h * h, axis=-1, keepdims=True) + EPS)
        xh = h * r
        gv = g_ref[...]
        e = xh * gv - t_ref[...]
        part = 0.5 * jnp.sum(jnp.mean(e * e, axis=-1, keepdims=True), axis=0, keepdims=True)
        dy = e / D
        dxh = dy * gv
        dh = r * (dxh - xh * jnp.mean(dxh * xh, axis=-1, keepdims=True))
        dh_ref[...] = dh
        dhb_ref[...] = dh.astype(BF16)
        _acc_rows(i, dg_ref, jnp.sum(dy * xh, axis=0, keepdims=True))
        _acc_rows(i, loss_ref, jnp.broadcast_to(part, (8, LANES)))

    return _rows(name, body, L, tm,
                 [(h1, _rs(tm, D)), (o2, _rs(tm, D)), (g, _fs((1, D))), (tgt, _rs(tm, D))],
                 [(_sds((L, D), F32), _rs(tm, D)), (_sds((L, D), BF16), _rs(tm, D)),
                  (_sds((1, D), F32), _fs((1, D))), (_sds((8, LANES), F32), _fs((8, LANES)))])


def _gelu_and_grad(x):
    c, k = 0.7978845608028654, 0.044715
    x2 = x * x
    t = jnp.tanh(c * x * (1.0 + k * x2))
    half = 0.5 * x
    return half * (1.0 + t), 0.5 * (1.0 + t) + half * (1.0 - t * t) * (c * (1.0 + 3.0 * k * x2))


def _glu_fn(y, g1):
    ya = jax.nn.gelu(y)
    return ya * jax.nn.sigmoid(g1)


def _glu_fwd(name, y, g1, tm):
    L, W = y.shape

    def body(y_ref, g_ref, o_ref):
        o_ref[...] = _glu_fn(y_ref[...], g_ref[...].astype(F32)).astype(BF16)

    return _rows(name, body, L, tm, [(y, _rs(tm, W)), (g1, _rs(tm, W))], [(_sds((L, W), BF16), _rs(tm, W))])[0]


def _glu_bwd(name, y, g1, dya2, tm):
    L, W = y.shape

    def body(y_ref, g_ref, d_ref, dy_ref, dg_ref):
        _, vjp = jax.vjp(_glu_fn, y_ref[...], g_ref[...].astype(F32))
        dy, dg = vjp(d_ref[...].astype(F32))
        dy_ref[...] = dy
        dg_ref[...] = dg.astype(BF16)

    return _rows(name, body, L, tm, [(y, _rs(tm, W)), (g1, _rs(tm, W)), (dya2, _rs(tm, W))],
                 [(_sds((L, W), F32), _rs(tm, W)), (_sds((L, W), BF16), _rs(tm, W))])


def _gelu_bwd(name, y, dy_direct, dya_g, proj, dskip, tm):
    L, W = y.shape

    def body(y_ref, dd_ref, dg_ref, u_ref, dyb_ref, dsk_ref):
        i = pl.program_id(0)
        dy = dd_ref[...] + dg_ref[...].astype(F32) * _gelu_and_grad(y_ref[...])[1]
        dyb_ref[...] = dy.astype(BF16)
        _acc_rows(i, dsk_ref, jnp.sum(dy * u_ref[...].astype(F32), axis=0, keepdims=True))

    del dskip
    return _rows(name, body, L, tm,
                 [(y, _rs(tm, W)), (dy_direct, _rs(tm, W)), (dya_g, _rs(tm, W)), (proj, _rs(tm, W, 0))],
                 [(_sds((L, W), BF16), _rs(tm, W)), (_sds((1, W), F32), _fs((1, W)))])


def _merge_fn(ma, mb, za, zb):
    return jax.nn.sigmoid(ma) * za + jax.nn.sigmoid(mb) * zb


def _merge_fwd(name, proj, cb_a, cb_b, za, zb, tm):
    L, D = za.shape

    def body(ma_ref, mb_ref, za_ref, zb_ref, o_ref):
        o_ref[...] = _merge_fn(ma_ref[...].astype(F32), mb_ref[...].astype(F32), za_ref[...].astype(F32),
                               zb_ref[...].astype(F32)).astype(BF16)

    return _rows(name, body, L, tm,
                 [(proj, _rs(tm, D, cb_a)), (proj, _rs(tm, D, cb_b)), (za, _rs(tm, D)), (zb, _rs(tm, D))],
                 [(_sds((L, D), BF16), _rs(tm, D))])[0]


def _merge_bwd(name, proj, cb_a, cb_b, za, zb, dmerged, tm):
    L, D = za.shape

    def body(ma_ref, mb_ref, za_ref, zb_ref, d_ref, dza_ref, dzb_ref, dma_ref, dmb_ref):
        _, vjp = jax.vjp(_merge_fn, ma_ref[...].astype(F32), mb_ref[...].astype(F32), za_ref[...].astype(F32),
                         zb_ref[...].astype(F32))
        dma, dmb, dza, dzb = vjp(d_ref[...].astype(F32))
        dza_ref[...] = dza.astype(BF16)
        dzb_ref[...] = dzb.astype(BF16)
        dma_ref[...] = dma.astype(BF16)
        dmb_ref[...] = dmb.astype(BF16)

    return _rows(name, body, L, tm,
                 [(proj, _rs(tm, D, cb_a)), (proj, _rs(tm, D, cb_b)), (za, _rs(tm, D)), (zb, _rs(tm, D)),
                  (dmerged, _rs(tm, D))],
                 [(_sds((L, D), BF16), _rs(tm, D)), (_sds((L, D), BF16), _rs(tm, D)),
                  (_sds((L, D), BF16), _rs(tm, D)), (_sds((L, D), BF16), _rs(tm, D))])


def _shift_down(x, k):
    row = lax.broadcasted_iota(jnp.int32, x.shape, 0)
    return jnp.where(row >= k, pltpu.roll(x, k, axis=0), 0.0)


def _shift_up(x, k):
    n = x.shape[0]
    row = lax.broadcasted_iota(jnp.int32, x.shape, 0)
    return jnp.where(row < n - k, pltpu.roll(x, n - k, axis=0), 0.0)


def _conv3(cv, w_ref, b_ref):
    return (w_ref[2:3, :] * cv + w_ref[1:2, :] * _shift_down(cv, 1) + w_ref[0:1, :] * _shift_down(cv, 2)
            + b_ref[...])


def _conv3_bwd(dcc, cv, w_ref):
    dcv = w_ref[2:3, :] * dcc + w_ref[1:2, :] * _shift_up(dcc, 1) + w_ref[0:1, :] * _shift_up(dcc, 2)
    dw = [jnp.sum(dcc * _shift_down(cv, 2), axis=0, keepdims=True),
          jnp.sum(dcc * _shift_down(cv, 1), axis=0, keepdims=True),
          jnp.sum(dcc * cv, axis=0, keepdims=True)]
    db = jnp.sum(dcc, axis=0, keepdims=True)
    return dcv, dw, db


def _store_rows(ref, rows):
    for r, val in enumerate(rows):
        ref[r:r + 1, :] = val


def _cols(name, body, ncb, ins, outs):
    return pl.pallas_call(
        body, name=name, grid=(ncb,),
        in_specs=[s for _, s in ins], out_specs=[s for _, s in outs],
        out_shape=[o for o, _ in outs], compiler_params=_params(1),
    )(*[a for a, _ in ins])


def _cb(L, w, off=0):
    return pl.BlockSpec((L, w), lambda j: (0, j + off))


def _convb_fwd(name, proj, cb_v, cb_gb, cb_gc, w, b):
    L = proj.shape[0]
    W = w.shape[1]
    c = LANES

    def body(v_ref, gb_ref, gc_ref, w_ref, b_ref, q_ref):
        cc = _conv3(gc_ref[...].astype(F32) * v_ref[...].astype(F32), w_ref, b_ref)
        q_ref[...] = (gb_ref[...].astype(F32) * cc).astype(BF16)

    return _cols(name, body, W // c,
                 [(proj, _cb(L, c, cb_v)), (proj, _cb(L, c, cb_gb)), (proj, _cb(L, c, cb_gc)),
                  (w, _cb(3, c)), (b, _cb(1, c))],
                 [(_sds((L, W), BF16), _cb(L, c))])[0]


def _convb_bwd(name, proj, cb_v, cb_gb, cb_gc, w, b, dq):
    L = proj.shape[0]
    W = w.shape[1]
    c = LANES

    def body(v_ref, gb_ref, gc_ref, w_ref, b_ref, dq_ref, dv_ref, dgb_ref, dgc_ref, dw_ref, db_ref):
        v, gc = v_ref[...].astype(F32), gc_ref[...].astype(F32)
        cv = gc * v
        cc = _conv3(cv, w_ref, b_ref)
        dq = dq_ref[...].astype(F32)
        dgb_ref[...] = (dq * cc).astype(BF16)
        dcv, dw, db = _conv3_bwd(dq * gb_ref[...].astype(F32), cv, w_ref)
        dv_ref[...] = (dcv * gc).astype(BF16)
        dgc_ref[...] = (dcv * v).astype(BF16)
        _store_rows(dw_ref, dw)
        db_ref[...] = db

    return _cols(name, body, W // c,
                 [(proj, _cb(L, c, cb_v)), (proj, _cb(L, c, cb_gb)), (proj, _cb(L, c, cb_gc)),
                  (w, _cb(3, c)), (b, _cb(1, c)), (dq, _cb(L, c))],
                 [(_sds((L, W), BF16), _cb(L, c)), (_sds((L, W), BF16), _cb(L, c)), (_sds((L, W), BF16), _cb(L, c)),
                  (_sds((3, W), F32), _cb(3, c)), (_sds((1, W), F32), _cb(1, c))])


HALO = 16


def _ffn_tiles(L, Fw):
    tr = _tile(L, 256, HALO)
    tc = _tile(Fw, 1408)
    return tr, tc, Fw // tc, L // tr, tr // HALO


def _ffn_fwd(name, hh, w, b):
    L = hh.shape[0]
    Fw = w.shape[1]
    tr, tc, ncb, nrt, rpt = _ffn_tiles(L, Fw)

    def body(a_ref, p_ref, h2_ref, w_ref, b_ref, f_ref):
        first = pl.program_id(1) == 0
        for c0 in range(0, tc, LANES):
            cs = slice(c0, c0 + LANES)
            prev = jnp.where(first, 0.0, p_ref[:, cs].astype(F32))
            x = jnp.concatenate([prev, a_ref[:, cs].astype(F32)], axis=0)
            n = x.shape[0]
            a = (w_ref[2:3, cs] * x + w_ref[1:2, cs] * pltpu.roll(x, 1, axis=0) + w_ref[0:1, cs] * pltpu.roll(x, 2, axis=0)
                 + b_ref[:, cs])[HALO:n]
            f_ref[:, cs] = (_gelu_and_grad(a)[0] * h2_ref[:, cs].astype(F32)).astype(BF16)

    main = pl.BlockSpec((tr, tc), lambda j, i: (i, j))
    return pl.pallas_call(
        body, name=name, grid=(ncb, nrt),
        in_specs=[main, pl.BlockSpec((HALO, tc), lambda j, i: (jnp.maximum(i * rpt - 1, 0), j)),
                  pl.BlockSpec((tr, tc), lambda j, i: (i, j + ncb)),
                  pl.BlockSpec((3, tc), lambda j, i: (0, j)), pl.BlockSpec((1, tc), lambda j, i: (0, j))],
        out_specs=main, out_shape=_sds((L, Fw), BF16), compiler_params=_params(2),
    )(hh, hh, hh, w, b)


def _ffn_bwd(name, hh, w, b, df):
    L = hh.shape[0]
    Fw = w.shape[1]
    tr, tc, ncb, nrt, rpt = _ffn_tiles(L, Fw)

    def body(a_ref, ap_ref, an_ref, h2_ref, h2n_ref, df_ref, dfn_ref, w_ref, b_ref, dhh_ref, dw_ref, db_ref):
        i = pl.program_id(1)
        first, last = i == 0, i == nrt - 1
        for c0 in range(0, tc, LANES):
            cs = slice(c0, c0 + LANES)
            zero = jnp.zeros((HALO, LANES), F32)
            h1 = jnp.concatenate([jnp.where(first, 0.0, ap_ref[:, cs].astype(F32)), a_ref[:, cs].astype(F32),
                                  an_ref[:, cs].astype(F32)], axis=0)
            h2 = jnp.concatenate([zero, h2_ref[:, cs].astype(F32), h2n_ref[:, cs].astype(F32)], axis=0)
            d = jnp.concatenate([zero, df_ref[:, cs].astype(F32), jnp.where(last, 0.0, dfn_ref[:, cs].astype(F32))], axis=0)
            n = h1.shape[0]
            s1, s2 = pltpu.roll(h1, 1, axis=0), pltpu.roll(h1, 2, axis=0)
            a = w_ref[2:3, cs] * h1 + w_ref[1:2, cs] * s1 + w_ref[0:1, cs] * s2 + b_ref[:, cs]
            ga, dga = _gelu_and_grad(a)
            da = d * h2 * dga
            dh1 = w_ref[2:3, cs] * da + w_ref[1:2, cs] * pltpu.roll(da, n - 1, axis=0) + w_ref[0:1, cs] * pltpu.roll(da, n - 2, axis=0)
            dhh_ref[0, :, cs] = dh1[HALO:HALO + tr].astype(BF16)
            dhh_ref[1, :, cs] = (d * ga)[HALO:HALO + tr].astype(BF16)
            dam = da[HALO:HALO + tr]
            rows = [jnp.sum(dam * s2[HALO:HALO + tr], axis=0, keepdims=True),
                    jnp.sum(dam * s1[HALO:HALO + tr], axis=0, keepdims=True),
                    jnp.sum(dam * h1[HALO:HALO + tr], axis=0, keepdims=True),
                    jnp.sum(dam, axis=0, keepdims=True)]

            @pl.when(first)
            def _():
                for r in range(3):
                    dw_ref[r:r + 1, cs] = rows[r]
                db_ref[:, cs] = rows[3]

            @pl.when(i > 0)
            def _():
                for r in range(3):
                    dw_ref[r:r + 1, cs] += rows[r]
                db_ref[:, cs] += rows[3]

    def spec(col_off, kind):
        if kind == "main":
            return pl.BlockSpec((tr, tc), lambda j, i: (i, j + col_off))
        if kind == "prev":
            return pl.BlockSpec((HALO, tc), lambda j, i: (jnp.maximum(i * rpt - 1, 0), j + col_off))
        return pl.BlockSpec((HALO, tc), lambda j, i: (jnp.minimum((i + 1) * rpt, nrt * rpt - 1), j + col_off))

    return pl.pallas_call(
        body, name=name, grid=(ncb, nrt),
        in_specs=[spec(0, "main"), spec(0, "prev"), spec(0, "next"), spec(ncb, "main"), spec(ncb, "next"),
                  spec(0, "main"), spec(0, "next"),
                  pl.BlockSpec((3, tc), lambda j, i: (0, j)), pl.BlockSpec((1, tc), lambda j, i: (0, j))],
        out_specs=[pl.BlockSpec((2, tr, tc), lambda j, i: (0, i, j)),
                   pl.BlockSpec((3, tc), lambda j, i: (0, j)), pl.BlockSpec((1, tc), lambda j, i: (0, j))],
        out_shape=[_sds((2, L, Fw), BF16), _sds((3, Fw), F32), _sds((1, Fw), F32)], compiler_params=_params(2),
    )(hh, hh, hh, hh, hh, df, df, w, b)


def _prep_fn(ar, ai, ldt, brt, bit):
    dt = jnp.exp(ldt)
    mag = jnp.exp(dt * ar)
    are = mag * jnp.cos(dt * ai)
    aim = mag * jnp.sin(dt * ai)
    nr = are - 1.0
    ni = aim
    den = ar * ar + ai * ai
    fr = (nr * ar + ni * ai) / den
    fi = (ni * ar - nr * ai) / den
    return are, aim, fr * brt - fi * bit, fr * bit + fi * brt


def _prep_fwd(name, ar, ai, ldt, brt, bit):
    def body(ar_ref, ai_ref, l_ref, br_ref, bi_ref, o1, o2, o3, o4):
        o1[...], o2[...], o3[...], o4[...] = _prep_fn(ar_ref[...], ai_ref[...], l_ref[...], br_ref[...], bi_ref[...])

    return pl.pallas_call(body, name=name,
                          out_shape=[_sds(ar.shape, F32), _sds(ar.shape, F32), _sds(brt.shape, F32), _sds(brt.shape, F32)],
                          )(ar, ai, ldt, brt, bit)


def _prep_bwd(name, ar, ai, ldt, brt, bit, g1, g2, g3, g4):
    def body(ar_ref, ai_ref, l_ref, br_ref, bi_ref, g1_ref, g2_ref, g3_ref, g4_ref, o1, o2, o3, o4, o5):
        _, vjp = jax.vjp(_prep_fn, ar_ref[...], ai_ref[...], l_ref[...], br_ref[...], bi_ref[...])
        o1[...], o2[...], o3[...], o4[...], o5[...] = vjp((g1_ref[...], g2_ref[...], g3_ref[...], g4_ref[...]))

    return pl.pallas_call(body, name=name,
                          out_shape=[_sds(ar.shape, F32)] * 3 + [_sds(brt.shape, F32)] * 2,
                          )(ar, ai, ldt, brt, bit, g1, g2, g3, g4)


def _ssm_in(name, src, m1, m2, tm, dep=None):
    L = src.shape[0]
    nb = m1.shape[0]

    deps = [] if dep is None else [dep]

    def body(s_ref, m1_ref, m2_ref, *rest):
        o1_ref, o2_ref = rest[len(deps):]
        u = s_ref[...].astype(BF16)
        r1 = jnp.dot(u, m1_ref[...], preferred_element_type=F32)
        r2 = jnp.dot(u, m2_ref[...], preferred_element_type=F32)
        for q in range(SLAB):
            o1_ref[q] = r1[:, q * LANES:(q + 1) * LANES].astype(BF16)
            o2_ref[q] = r2[:, q * LANES:(q + 1) * LANES].astype(BF16)

    ms = pl.BlockSpec((None, LANES, SLAB * LANES), lambda i, j: (j, 0, 0))
    os_ = pl.BlockSpec((SLAB, tm, LANES), lambda i, j: (j, i, 0))
    return pl.pallas_call(
        body, name=name, grid=(L // tm, nb),
        in_specs=[pl.BlockSpec((tm, LANES), lambda i, j: (i, j)), ms, ms] + [pl.BlockSpec(memory_space=pl.ANY)] * len(deps),
        out_specs=[os_, os_],
        out_shape=[_sds((SLAB * nb, L, LANES), BF16)] * 2, compiler_params=_params(2),
    )(src, m1, m2, *deps)


def _ssm_out(name, x1, x2, m1, m2, aux, dvec, tm, post=None):
    L = x1.shape[1]
    nb = m1.shape[0]

    def body(x1_ref, x2_ref, m1_ref, m2_ref, a_ref, d_ref, o_ref, *rest):
        a1 = jnp.concatenate([x1_ref[q] for q in range(SLAB)], axis=1).astype(BF16)
        a2 = jnp.concatenate([x2_ref[q] for q in range(SLAB)], axis=1).astype(BF16)
        y = (jnp.dot(a1, m1_ref[...], preferred_element_type=F32) + jnp.dot(a2, m2_ref[...], preferred_element_type=F32)
             + d_ref[...] * a_ref[...].astype(F32))
        o_ref[...] = y
        if post is not None:
            rest[0][...] = post(y).astype(BF16)

    xs = pl.BlockSpec((SLAB, tm, LANES), lambda i, j: (j, i, 0))
    ms = pl.BlockSpec((None, SLAB * LANES, LANES), lambda i, j: (j, 0, 0))
    cs = pl.BlockSpec((tm, LANES), lambda i, j: (i, j))
    W = nb * LANES
    outs, ospecs = [_sds((L, W), F32)], [cs]
    if post is not None:
        outs.append(_sds((L, W), BF16))
        ospecs.append(cs)
    return pl.pallas_call(
        body, name=name, grid=(L // tm, nb),
        in_specs=[xs, xs, ms, ms, cs, pl.BlockSpec((1, LANES), lambda i, j: (0, j))], out_specs=ospecs,
        out_shape=outs, compiler_params=_params(2),
    )(x1, x2, m1, m2, aux, dvec)


def _ssm_dw(name, src, x1, x2, tk):
    L = src.shape[0]
    nb = x1.shape[0] // SLAB
    dn = (((0,), (0,)), ((), ()))

    def body(s_ref, x1_ref, x2_ref, o1_ref, o2_ref):
        k = pl.program_id(1)
        s = s_ref[...].astype(BF16)
        a1 = jnp.concatenate([x1_ref[q] for q in range(SLAB)], axis=1).astype(BF16)
        a2 = jnp.concatenate([x2_ref[q] for q in range(SLAB)], axis=1).astype(BF16)
        _acc_rows(k, o1_ref, lax.dot_general(s, a1, dn, preferred_element_type=F32))
        _acc_rows(k, o2_ref, lax.dot_general(s, a2, dn, preferred_element_type=F32))

    xs = pl.BlockSpec((SLAB, tk, LANES), lambda j, k: (j, k, 0))
    os_ = pl.BlockSpec((None, LANES, SLAB * LANES), lambda j, k: (j, 0, 0))
    return pl.pallas_call(
        body, name=name, grid=(nb, L // tk),
        in_specs=[pl.BlockSpec((tk, LANES), lambda j, k: (k, j)), xs, xs], out_specs=[os_, os_],
        out_shape=[_sds((nb, LANES, SLAB * LANES), F32)] * 2, compiler_params=_params(2),
    )(src, x1, x2)


def _scan(name, b_re, b_im, a_re, a_im, xs=None):
    reverse = xs is not None
    ns, L, _ = b_re.shape
    ng = ns // 8
    tc = min(LANES, L)
    pitch = tc + 8
    nt = L // tc
    n_in = 4 if reverse else 2

    def body(*refs):
        ins = refs[:n_in]
        ar_ref, ai_ref = refs[n_in], refs[n_in + 1]
        o_re, o_im = refs[n_in + 2], refs[n_in + 3]
        k = n_in + 4
        if reverse:
            da_re, da_im = refs[k], refs[k + 1]
            k += 2
        stage = refs[k:k + n_in]
        out_re, out_im, st_re, st_im = refs[k + n_in:k + n_in + 4]
        acc = refs[k + n_in + 4:]
        i = pl.program_id(0)

        @pl.when(i == 0)
        def _():
            st_re[...] = jnp.zeros(st_re.shape, F32)
            st_im[...] = jnp.zeros(st_im.shape, F32)
            for r in acc:
                r[...] = jnp.zeros(r.shape, F32)

        for s in range(ns):
            for src, dst in zip(ins, stage):
                dst[pl.ds(s * pitch, tc), :] = src[s].astype(F32)

        a_r = [ar_ref[g] for g in range(ng)]
        a_i = [ai_ref[g] for g in range(ng)]

        def step(tt, carry):
            t = (tc - 1 - tt) if reverse else tt
            new = []
            for g in range(ng):
                rows = pl.ds(g * 8 * pitch + t, 8, stride=pitch)
                cr, ci = carry[2 * g], carry[2 * g + 1]
                br, bi = stage[0][rows, :], stage[1][rows, :]
                if reverse:
                    xr, xi = stage[2][rows, :], stage[3][rows, :]
                    acc[0][g] += xr * cr + xi * ci
                    acc[1][g] += xr * ci - xi * cr
                    nr = a_r[g] * cr + a_i[g] * ci + br
                    ni = a_r[g] * ci - a_i[g] * cr + bi
                else:
                    nr = a_r[g] * cr - a_i[g] * ci + br
                    ni = a_r[g] * ci + a_i[g] * cr + bi
                out_re[rows, :] = nr
                out_im[rows, :] = ni
                new += [nr, ni]
            return tuple(new)

        init = []
        for g in range(ng):
            init += [st_re[g], st_im[g]]
        fin = lax.fori_loop(0, tc, step, tuple(init), unroll=2)
        for g in range(ng):
            st_re[g] = fin[2 * g]
            st_im[g] = fin[2 * g + 1]
        for s in range(ns):
            o_re[s] = out_re[pl.ds(s * pitch, tc), :].astype(BF16)
            o_im[s] = out_im[pl.ds(s * pitch, tc), :].astype(BF16)
        if reverse:
            da_re[...] = acc[0][...]
            da_im[...] = acc[1][...]

    tmap = (lambda i: (0, nt - 1 - i, 0)) if reverse else (lambda i: (0, i, 0))
    bs = pl.BlockSpec((ns, tc, LANES), tmap)
    as_ = pl.BlockSpec((ng, 8, LANES), lambda i: (0, 0, 0))
    ins = [b_re, b_im] + (list(xs) if reverse else [])
    out_shape = [_sds((ns, L, LANES), BF16)] * 2 + ([_sds((ng, 8, LANES), F32)] * 2 if reverse else [])
    out_specs = [bs, bs] + ([as_, as_] if reverse else [])
    scratch = [pltpu.VMEM((ns * pitch, LANES), F32)] * (n_in + 2) + [pltpu.VMEM((ng, 8, LANES), F32)] * (4 if reverse else 2)
    return pl.pallas_call(
        body, name=name, grid=(nt,), in_specs=[bs] * n_in + [as_, as_], out_specs=out_specs,
        out_shape=out_shape, scratch_shapes=scratch, compiler_params=_params(1),
    )(*ins, a_re, a_im)


def _scan_steps(tc, pitch, ng, reverse, a_r, a_i, stage_b, stage_x, out_re, out_im, st_re, st_im, acc):
    def step(tt, carry):
        t = (tc - 1 - tt) if reverse else tt
        new = []
        for g in range(ng):
            rows = pl.ds(g * 8 * pitch + t, 8, stride=pitch)
            cr, ci = carry[2 * g], carry[2 * g + 1]
            br, bi = stage_b[0][rows, :], stage_b[1][rows, :]
            if reverse:
                xr, xi = stage_x[0][rows, :], stage_x[1][rows, :]
                acc[0][g] += xr * cr + xi * ci
                acc[1][g] += xr * ci - xi * cr
                nr = a_r[g] * cr + a_i[g] * ci + br
                ni = a_r[g] * ci - a_i[g] * cr + bi
            else:
                nr = a_r[g] * cr - a_i[g] * ci + br
                ni = a_r[g] * ci + a_i[g] * cr + bi
            out_re[rows, :] = nr
            out_im[rows, :] = ni
            new += [nr, ni]
        return tuple(new)

    init = []
    for g in range(ng):
        init += [st_re[g], st_im[g]]
    fin = lax.fori_loop(0, tc, step, tuple(init), unroll=2)
    for g in range(ng):
        st_re[g] = fin[2 * g]
        st_im[g] = fin[2 * g + 1]


def _s5_fwd(name, proj, bm_re, bm_im, cm_re, cm_im, dskip, a_re, a_im):
    L = proj.shape[0]
    nb = bm_re.shape[0]
    ns, W = SLAB * nb, nb * LANES
    ng = ns // 8
    tc = min(LANES, L)
    pitch = tc + 8
    wide = SLAB * LANES

    def body(u_ref, bre_ref, bim_ref, cre_ref, cim_ref, d_ref, ar_ref, ai_ref, xr_ref, xi_ref, y_ref, ya_ref,
             sb_re, sb_im, out_re, out_im, st_re, st_im):
        @pl.when(pl.program_id(0) == 0)
        def _():
            st_re[...] = jnp.zeros(st_re.shape, F32)
            st_im[...] = jnp.zeros(st_im.shape, F32)

        for j in range(nb):
            ub = u_ref[:, j * LANES:(j + 1) * LANES]
            r1 = jnp.dot(ub, bre_ref[j], preferred_element_type=F32)
            r2 = jnp.dot(ub, bim_ref[j], preferred_element_type=F32)
            for q in range(SLAB):
                sb_re[pl.ds((SLAB * j + q) * pitch, tc), :] = r1[:, q * LANES:(q + 1) * LANES]
                sb_im[pl.ds((SLAB * j + q) * pitch, tc), :] = r2[:, q * LANES:(q + 1) * LANES]
        a_r = [ar_ref[g] for g in range(ng)]
        a_i = [ai_ref[g] for g in range(ng)]
        _scan_steps(tc, pitch, ng, False, a_r, a_i, (sb_re, sb_im), None, out_re, out_im, st_re, st_im, None)
        for j in range(nb):
            x1 = [out_re[pl.ds((SLAB * j + q) * pitch, tc), :].astype(BF16) for q in range(SLAB)]
            x2 = [out_im[pl.ds((SLAB * j + q) * pitch, tc), :].astype(BF16) for q in range(SLAB)]
            for q in range(SLAB):
                xr_ref[SLAB * j + q] = x1[q]
                xi_ref[SLAB * j + q] = x2[q]
            cols = slice(j * LANES, (j + 1) * LANES)
            y = (jnp.dot(jnp.concatenate(x1, axis=1), cre_ref[j], preferred_element_type=F32)
                 + jnp.dot(jnp.concatenate(x2, axis=1), cim_ref[j], preferred_element_type=F32)
                 + d_ref[:, cols] * u_ref[:, cols].astype(F32))
            y_ref[:, cols] = y
            ya_ref[:, cols] = jax.nn.gelu(y).astype(BF16)

    full3 = lambda s: pl.BlockSpec(s, lambda i: (0, 0, 0))
    xs = pl.BlockSpec((ns, tc, LANES), lambda i: (0, i, 0))
    rows = pl.BlockSpec((tc, W), lambda i: (i, 0))
    return pl.pallas_call(
        body, name=name, grid=(L // tc,),
        in_specs=[rows, full3((nb, LANES, wide)), full3((nb, LANES, wide)), full3((nb, wide, LANES)),
                  full3((nb, wide, LANES)), pl.BlockSpec((1, W), lambda i: (0, 0)), full3((ng, 8, LANES)), full3((ng, 8, LANES))],
        out_specs=[xs, xs, rows, rows],
        out_shape=[_sds((ns, L, LANES), BF16)] * 2 + [_sds((L, W), F32), _sds((L, W), BF16)],
        scratch_shapes=[pltpu.VMEM((ns * pitch, LANES), F32)] * 4 + [pltpu.VMEM((ng, 8, LANES), F32)] * 2,
        compiler_params=_params(1),
    )(proj, bm_re, bm_im, cm_re, cm_im, dskip, a_re, a_im)


def _s5_bwd(name, dyb, proj, xs_re, xs_im, cmt_re, cmt_im, bmt_re, bmt_im, dskip, a_re, a_im):
    L = dyb.shape[0]
    nb = cmt_re.shape[0]
    ns, W = SLAB * nb, nb * LANES
    ng = ns // 8
    tc = min(LANES, L)
    pitch = tc + 8
    nt = L // tc
    wide = SLAB * LANES
    dn = (((0,), (0,)), ((), ()))

    def body(dy_ref, u_ref, xr_ref, xi_ref, cre_ref, cim_ref, bre_ref, bim_ref, d_ref, ar_ref, ai_ref,
             du_ref, gbr_ref, gbi_ref, gcr_ref, gci_ref, dar_ref, dai_ref,
             sd_re, sd_im, sx_re, sx_im, out_re, out_im, st_re, st_im, acc_re, acc_im):
        first = pl.program_id(0) == 0

        @pl.when(first)
        def _():
            for r in (st_re, st_im, acc_re, acc_im):
                r[...] = jnp.zeros(r.shape, F32)
            for r in (gbr_ref, gbi_ref, gcr_ref, gci_ref):
                r[...] = jnp.zeros(r.shape, F32)

        for j in range(nb):
            dyj = dy_ref[:, j * LANES:(j + 1) * LANES]
            r1 = jnp.dot(dyj, cre_ref[j], preferred_element_type=F32)
            r2 = jnp.dot(dyj, cim_ref[j], preferred_element_type=F32)
            for q in range(SLAB):
                s = SLAB * j + q
                sd_re[pl.ds(s * pitch, tc), :] = r1[:, q * LANES:(q + 1) * LANES]
                sd_im[pl.ds(s * pitch, tc), :] = r2[:, q * LANES:(q + 1) * LANES]
                sx_re[pl.ds(s * pitch, tc), :] = xr_ref[s].astype(F32)
                sx_im[pl.ds(s * pitch, tc), :] = xi_ref[s].astype(F32)
        a_r = [ar_ref[g] for g in range(ng)]
        a_i = [ai_ref[g] for g in range(ng)]
        _scan_steps(tc, pitch, ng, True, a_r, a_i, (sd_re, sd_im), (sx_re, sx_im), out_re, out_im, st_re, st_im,
                    (acc_re, acc_im))
        for j in range(nb):
            cols = slice(j * LANES, (j + 1) * LANES)
            l1 = jnp.concatenate([out_re[pl.ds((SLAB * j + q) * pitch, tc), :] for q in range(SLAB)], axis=1).astype(BF16)
            l2 = jnp.concatenate([out_im[pl.ds((SLAB * j + q) * pitch, tc), :] for q in range(SLAB)], axis=1).astype(BF16)
            dyj = dy_ref[:, cols]
            du = (jnp.dot(l1, bre_ref[j], preferred_element_type=F32) + jnp.dot(l2, bim_ref[j], preferred_element_type=F32)
                  + d_ref[:, cols] * dyj.astype(F32))
            du_ref[:, cols] = du.astype(BF16)
            uj = u_ref[:, cols]
            gbr_ref[j] += lax.dot_general(uj, l1, dn, preferred_element_type=F32)
            gbi_ref[j] += lax.dot_general(uj, l2, dn, preferred_element_type=F32)
            x1 = jnp.concatenate([xr_ref[SLAB * j + q] for q in range(SLAB)], axis=1)
            x2 = jnp.concatenate([xi_ref[SLAB * j + q] for q in range(SLAB)], axis=1)
            gcr_ref[j] += lax.dot_general(dyj, x1, dn, preferred_element_type=F32)
            gci_ref[j] += lax.dot_general(dyj, x2, dn, preferred_element_type=F32)
        dar_ref[...] = acc_re[...]
        dai_ref[...] = acc_im[...]

    full3 = lambda s: pl.BlockSpec(s, lambda i: (0, 0, 0))
    xs = pl.BlockSpec((ns, tc, LANES), lambda i: (0, nt - 1 - i, 0))
    rows = pl.BlockSpec((tc, W), lambda i: (nt - 1 - i, 0))
    mat_a, mat_b = full3((nb, LANES, wide)), full3((nb, wide, LANES))
    vec = full3((ng, 8, LANES))
    return pl.pallas_call(
        body, name=name, grid=(nt,),
        in_specs=[rows, rows, xs, xs, mat_a, mat_a, mat_b, mat_b, pl.BlockSpec((1, W), lambda i: (0, 0)), vec, vec],
        out_specs=[rows, mat_a, mat_a, mat_a, mat_a, vec, vec],
        out_shape=[_sds((L, W), BF16)] + [_sds((nb, LANES, wide), F32)] * 4 + [_sds((ng, 8, LANES), F32)] * 2,
        scratch_shapes=[pltpu.VMEM((ns * pitch, LANES), F32)] * 6 + [pltpu.VMEM((ng, 8, LANES), F32)] * 4,
        compiler_params=_params(1),
    )(dyb, proj, xs_re, xs_im, cmt_re, cmt_im, bmt_re, bmt_im, dskip, a_re, a_im)


def _peer(k):
    x, y, c = lax.axis_index("x"), lax.axis_index("y"), lax.axis_index("c")
    px = 1 - x if (k >> 2) & 1 else x
    py = 1 - y if (k >> 1) & 1 else y
    pc = 1 - c if k & 1 else c
    return (px, py, pc), 4 * px + 2 * py + pc


def _window(ref, kind, idx, n):
    if kind == "col":
        w = ref.shape[1] // n
        return ref.at[:, pl.ds(pl.multiple_of(idx * w, LANES), w)]
    r = ref.shape[0] // n
    return ref.at[pl.ds(pl.multiple_of(idx * r, 8), r), :]


def _all_gather(name, shards, kinds):
    n = len(shards)
    fulls = []
    for s, kind in zip(shards, kinds):
        fulls.append(_sds((s.shape[0], s.shape[1] * N_DEV) if kind == "col" else (s.shape[0] * N_DEV, s.shape[1]), s.dtype))

    def body(*refs):
        src, dst = refs[:n], refs[n:2 * n]
        send, recv, loc = refs[2 * n:]
        me = 4 * lax.axis_index("x") + 2 * lax.axis_index("y") + lax.axis_index("c")
        copies = []
        for a in range(n):
            own = pltpu.make_async_copy(src[a], _window(dst[a], kinds[a], me, N_DEV), loc.at[a])
            own.start()
            copies.append(own)
        sends = []
        for k in range(1, N_DEV):
            dev, _ = _peer(k)
            for a in range(n):
                cp = pltpu.make_async_remote_copy(
                    src_ref=src[a], dst_ref=_window(dst[a], kinds[a], me, N_DEV),
                    send_sem=send.at[a * N_DEV + k], recv_sem=recv.at[a * N_DEV + k],
                    device_id=dev, device_id_type=MESH)
                cp.start()
                sends.append(cp)
        for k in range(1, N_DEV):
            dev, pidx = _peer(k)
            for a in range(n):
                pltpu.make_async_remote_copy(
                    src_ref=src[a], dst_ref=_window(dst[a], kinds[a], pidx, N_DEV),
                    send_sem=send.at[a * N_DEV + k], recv_sem=recv.at[a * N_DEV + k],
                    device_id=dev, device_id_type=MESH).wait_recv()
        for cp in sends:
            cp.wait_send()
        for cp in copies:
            cp.wait()

    any_ = pl.BlockSpec(memory_space=pl.ANY)
    return pl.pallas_call(
        body, name=name, in_specs=[any_] * n, out_specs=[any_] * n, out_shape=fulls,
        scratch_shapes=[pltpu.SemaphoreType.DMA((n * N_DEV,)), pltpu.SemaphoreType.DMA((n * N_DEV,)),
                        pltpu.SemaphoreType.DMA((n,))],
        compiler_params=pltpu.CompilerParams(has_side_effects=True),
    )(*shards)


_HBM = pl.BlockSpec(memory_space=pltpu.HBM)
_SEM = pl.BlockSpec(memory_space=pltpu.SEMAPHORE)
_ANY = pl.BlockSpec(memory_space=pl.ANY)
_EFFECT = pltpu.SideEffectType.DATAFLOW_SIDE_EFFECTING


def _xfer_refs(mode, kinds, a, src, dst, me, pidx):
    if mode == "gather":
        return src[a], _window(dst[a], kinds[a], me, N_DEV), _window(dst[a], kinds[a], pidx, N_DEV)
    return _window(src[a], kinds[a], pidx, N_DEV), dst[a].at[me], dst[a].at[pidx]


def _xfer_out_shapes(mode, arrs, kinds):
    outs = []
    for s, kind in zip(arrs, kinds):
        if mode == "gather":
            outs.append((s.shape[0], s.shape[1] * N_DEV) if kind == "col" else (s.shape[0] * N_DEV, s.shape[1]))
        else:
            outs.append((N_DEV,) + ((s.shape[0], s.shape[1] // N_DEV) if kind == "col" else (s.shape[0] // N_DEV, s.shape[1])))
    return outs


def _xfer_start(name, mode, arrs, kinds, after):
    n = len(arrs)
    shapes = _xfer_out_shapes(mode, arrs, kinds)

    def body(*refs):
        src, dst = refs[:n], refs[n:2 * n]
        send, recv = refs[2 * n + 1], refs[2 * n + 2]
        token, loc = refs[2 * n + 3 + 2 * n], refs[2 * n + 4 + 2 * n]
        me = 4 * lax.axis_index("x") + 2 * lax.axis_index("y") + lax.axis_index("c")
        own = []
        for a in range(n):
            s, _, d = _xfer_refs(mode, kinds, a, src, dst, me, me)
            own.append(pltpu.make_async_copy(s, d, loc.at[a]))
            own[-1].start()
        for cp in own:
            cp.wait()
        for k in range(1, N_DEV):
            dev, pidx = _peer(k)
            for a in range(n):
                s, d, _ = _xfer_refs(mode, kinds, a, src, dst, me, pidx)
                pltpu.make_async_remote_copy(src_ref=s, dst_ref=d, send_sem=send.at[a * N_DEV + k],
                                             recv_sem=recv.at[a * N_DEV + k], device_id=dev, device_id_type=MESH).start()
        token[...] = jnp.zeros(token.shape, F32)

    lands = [pltpu.with_memory_space_constraint(lax.empty(shp, s.dtype), pltpu.HBM) for shp, s in zip(shapes, arrs)]
    srcs = [pltpu.with_memory_space_constraint(s, pltpu.HBM) for s in arrs]
    res = pl.pallas_call(
        body, name=name,
        in_specs=[_HBM] * (2 * n) + [_ANY],
        out_specs=[_SEM, _SEM] + [_HBM] * (2 * n) + [pl.BlockSpec(memory_space=pltpu.VMEM)],
        out_shape=[pltpu.SemaphoreType.DMA((n * N_DEV,)), pltpu.SemaphoreType.DMA((n * N_DEV,))]
        + [pltpu.HBM(s.shape, s.dtype) for s in arrs] + [pltpu.HBM(shp, s.dtype) for shp, s in zip(shapes, arrs)]
        + [_sds((8, LANES), F32)],
        input_output_aliases={i: 2 + i for i in range(2 * n)},
        scratch_shapes=[pltpu.SemaphoreType.DMA((n,))],
        compiler_params=pltpu.CompilerParams(has_side_effects=_EFFECT),
    )(*srcs, *lands, after)
    return dict(mode=mode, kinds=kinds, n=n, send=res[0], recv=res[1], srcs=res[2:2 + n], lands=res[2 + n:2 + 2 * n]), res[-1]


def _xfer_wait(name, st, after):
    n, mode, kinds = st["n"], st["mode"], st["kinds"]

    def body(*refs):
        src, dst = refs[:n], refs[n:2 * n]
        send, recv = refs[2 * n], refs[2 * n + 1]
        me = 4 * lax.axis_index("x") + 2 * lax.axis_index("y") + lax.axis_index("c")
        for k in range(1, N_DEV):
            dev, pidx = _peer(k)
            for a in range(n):
                s, d, land = _xfer_refs(mode, kinds, a, src, dst, me, pidx)
                cp = pltpu.make_async_remote_copy(src_ref=s, dst_ref=land, send_sem=send.at[a * N_DEV + k],
                                                  recv_sem=recv.at[a * N_DEV + k], device_id=dev, device_id_type=MESH)
                cp.wait_send()
                cp.wait_recv()

    res = pl.pallas_call(
        body, name=name,
        in_specs=[_HBM] * (2 * n) + [_SEM, _SEM, _ANY],
        out_specs=[_HBM] * (2 * n),
        out_shape=[pltpu.HBM(s.shape, s.dtype) for s in st["srcs"]] + [pltpu.HBM(s.shape, s.dtype) for s in st["lands"]],
        input_output_aliases={i: i for i in range(2 * n)},
        compiler_params=pltpu.CompilerParams(has_side_effects=_EFFECT),
    )(*st["srcs"], *st["lands"], st["send"], st["recv"], after)
    return list(res[n:])


def _sc_xfer(name, mode, arrs, kinds, collective_id):
    n = len(arrs)
    shapes = _xfer_out_shapes(mode, arrs, kinds)
    hbm = pltpu.MemorySpace.HBM
    src = [jax.new_ref(a, memory_space=hbm) for a in arrs]
    dst = [jax.empty_ref(_sds(shp, a.dtype), memory_space=hbm) for shp, a in zip(shapes, arrs)]

    @pl.kernel(mesh=plsc.ScalarSubcoreMesh(axis_name="seq", num_cores=1), name=name,
               scratch_types=(pltpu.SemaphoreType.DMA((n * N_DEV,)), pltpu.SemaphoreType.DMA((n * N_DEV,)),
                              pltpu.SemaphoreType.DMA((n,))),
               compiler_params=pltpu.CompilerParams(collective_id=collective_id))
    def launch(send, recv, loc):
        barrier = pltpu.get_barrier_semaphore()
        for k in range(1, N_DEV):
            pl.semaphore_signal(barrier, inc=1, device_id=_peer(k)[0], device_id_type=MESH)
        pl.semaphore_wait(barrier, N_DEV - 1)
        me = 4 * lax.axis_index("x") + 2 * lax.axis_index("y") + lax.axis_index("c")
        own, sends = [], []
        for a in range(n):
            s, _, d = _xfer_refs(mode, kinds, a, src, dst, me, me)
            own.append(pltpu.make_async_copy(s, d, loc.at[a]))
            own[-1].start()
        for k in range(1, N_DEV):
            dev, pidx = _peer(k)
            for a in range(n):
                s, d, _ = _xfer_refs(mode, kinds, a, src, dst, me, pidx)
                sends.append(pltpu.make_async_remote_copy(src_ref=s, dst_ref=d, send_sem=send.at[a * N_DEV + k],
                                                          recv_sem=recv.at[a * N_DEV + k], device_id=dev, device_id_type=MESH))
                sends[-1].start()
        for cp in own:
            cp.wait()
        for k in range(1, N_DEV):
            dev, pidx = _peer(k)
            for a in range(n):
                s, _, land = _xfer_refs(mode, kinds, a, src, dst, me, pidx)
                pltpu.make_async_remote_copy(src_ref=s, dst_ref=land, send_sem=send.at[a * N_DEV + k],
                                             recv_sem=recv.at[a * N_DEV + k], device_id=dev, device_id_type=MESH).wait_recv()
        for cp in sends:
            cp.wait_send()

    launch()
    return [d[...] for d in dst]


def _sc_gather(name, arrs, kinds, collective_id):
    n = len(arrs)
    pairs = 7
    shapes = _xfer_out_shapes("gather", arrs, kinds)
    hbm = pltpu.MemorySpace.HBM
    src = [jax.new_ref(a, memory_space=hbm) for a in arrs]
    dst = [jax.empty_ref(_sds(shp, a.dtype), memory_space=hbm) for shp, a in zip(shapes, arrs)]

    @pl.kernel(mesh=plsc.ScalarSubcoreMesh(axis_name="seq", num_cores=1), name=name,
               scratch_types=(pltpu.SemaphoreType.DMA((n * pairs,)), pltpu.SemaphoreType.DMA((n * pairs,)),
                              pltpu.SemaphoreType.DMA((n,))),
               compiler_params=pltpu.CompilerParams(collective_id=collective_id))
    def launch(send, recv, loc):
        x, y, c = lax.axis_index("x"), lax.axis_index("y"), lax.axis_index("c")
        me = 4 * x + 2 * y + c
        sib = (x, y, 1 - c)
        chips = []
        for fx, fy in ((1, 0), (0, 1), (1, 1)):
            px, py = (1 - x if fx else x), (1 - y if fy else y)
            chips.append(((px, py, c), 4 * px + 2 * py + c, 4 * px + 2 * py + (1 - c)))
        barrier = pltpu.get_barrier_semaphore()
        for dev in [sib] + [ch[0] for ch in chips]:
            pl.semaphore_signal(barrier, inc=1, device_id=dev, device_id_type=MESH)
        pl.semaphore_wait(barrier, 4)

        def win(a, idx):
            return _window(dst[a], kinds[a], idx, N_DEV)

        def rcopy(a, p, s, d, dev):
            return pltpu.make_async_remote_copy(src_ref=s, dst_ref=d, send_sem=send.at[a * pairs + p],
                                                recv_sem=recv.at[a * pairs + p], device_id=dev, device_id_type=MESH)

        own, sends = [], []
        for a in range(n):
            own.append(pltpu.make_async_copy(src[a], win(a, me), loc.at[a]))
            own[-1].start()
        for j, (dev, _, _) in enumerate(chips):
            for a in range(n):
                sends.append(rcopy(a, 1 + j, src[a], win(a, me), dev))
                sends[-1].start()
        for a in range(n):
            sends.append(rcopy(a, 0, src[a], win(a, me), sib))
            sends[-1].start()
        for j, (dev, idx, _) in enumerate(chips):
            for a in range(n):
                rcopy(a, 1 + j, src[a], win(a, idx), dev).wait_recv()
                sends.append(rcopy(a, 4 + j, win(a, idx), win(a, idx), sib))
                sends[-1].start()
        for cp in own:
            cp.wait()
        for a in range(n):
            rcopy(a, 0, src[a], win(a, 4 * x + 2 * y + (1 - c)), sib).wait_recv()
        for j, (_, _, sidx) in enumerate(chips):
            for a in range(n):
                rcopy(a, 4 + j, src[a], win(a, sidx), sib).wait_recv()
        for cp in sends:
            cp.wait_send()

    launch()
    return [d[...] for d in dst]


_SEQ_IDS = {"gather_in": 7, "gather_mix": 1, "gather_ffn": 2, "grads_down": 3, "grads_up": 8, "grads_mix": 4,
            "grads_small": 5, "grads_in": 6}


def _seq_start(name, mode, arrs, kinds, after):
    arrs = list(arrs)
    if after is not None:
        after, *arrs = lax.optimization_barrier((after, *arrs))
    if mode == "gather":
        return _sc_gather(name, arrs, kinds, _SEQ_IDS[name]), None
    return _sc_xfer(name, mode, arrs, kinds, _SEQ_IDS[name]), None


def _seq_wait(name, res, after):
    del name, after
    return list(res)


def _adamw(name, parts, w, m, v):
    P, R, C = parts.shape
    sub = 16 if parts.dtype == BF16 else 8
    tr = R if R * C <= (1 << 18) else _tile(R, max(sub, (1 << 18) // C), sub)

    def body(p_ref, w_ref, m_ref, v_ref, g_ref, d_ref, nm_ref, nv_ref):
        g = p_ref[0].astype(F32)
        for s in range(1, P):
            g = g + p_ref[s].astype(F32)
        m2 = ADAM_B1 * m_ref[...] + (1.0 - ADAM_B1) * g
        v2 = ADAM_B2 * v_ref[...] + (1.0 - ADAM_B2) * (g * g)
        m_hat = m2 / (1.0 - ADAM_B1 ** ADAM_STEP)
        v_hat = v2 / (1.0 - ADAM_B2 ** ADAM_STEP)
        g_ref[...] = g
        d_ref[...] = -ADAM_LR * (m_hat / (jnp.sqrt(v_hat) + ADAM_EPS) + ADAM_WD * w_ref[...])
        nm_ref[...] = m2
        nv_ref[...] = v2

    sp = pl.BlockSpec((tr, C), lambda i: (i, 0))
    return pl.pallas_call(
        body, name=name, grid=(R // tr,),
        in_specs=[pl.BlockSpec((P, tr, C), lambda i: (0, i, 0)), sp, sp, sp], out_specs=[sp] * 4,
        out_shape=[_sds((R, C), F32)] * 4, compiler_params=_params(1),
    )(parts, w, m, v)


def _pack(arrs, row_mult=8):
    pieces, total = [], 0
    for a in arrs:
        f = a.reshape(-1).astype(F32)
        pad = (-f.shape[0]) % (8 * LANES)
        pieces.append(jnp.pad(f, (0, pad)) if pad else f)
        total += f.shape[0] + pad
    tail = (-total) % (row_mult * LANES)
    if tail:
        pieces.append(jnp.zeros((tail,), F32))
    return jnp.concatenate(pieces).reshape(-1, LANES)


def _unpack(buf, shapes, lead=()):
    out, row = [], 0
    for shp in shapes:
        size = 1
        for d in shp:
            size *= d
        rows = -(-size // (8 * LANES)) * 8
        piece = buf[..., row:row + rows, :].reshape(lead + (rows * LANES,))[..., :size]
        out.append(piece.reshape(lead + tuple(shp)))
        row += rows
    return out


def kernel(x, norm_tok, w_in, a_re, a_im, log_dt, b_re, b_im, c_re, c_im, d_skip, w_glu, w_ssm_out, conv_w, conv_b, w_conv_out, w_o, norm_ffn, w_up, ffn_conv_w, ffn_conv_b, w_down, norm_final, loss_target, m_norm_tok, m_w_in, m_a_re, m_a_im, m_log_dt, m_b_re, m_b_im, m_c_re, m_c_im, m_d_skip, m_w_glu, m_w_ssm_out, m_conv_w, m_conv_b, m_w_conv_out, m_w_o, m_norm_ffn, m_w_up, m_ffn_conv_w, m_ffn_conv_b, m_w_down, m_norm_final, v_norm_tok, v_w_in, v_a_re, v_a_im, v_log_dt, v_b_re, v_b_im, v_c_re, v_c_im, v_d_skip, v_w_glu, v_w_ssm_out, v_conv_w, v_conv_b, v_w_conv_out, v_w_o, v_norm_ffn, v_w_up, v_ffn_conv_w, v_ffn_conv_b, v_w_down, v_norm_final):
    args = dict(locals())
    L, D = x.shape[1], x.shape[2]
    G, P, H = b_re.shape[1], b_re.shape[2], b_re.shape[3]
    SW = G * H
    CW = conv_b.shape[1]
    FF = ffn_conv_b.shape[1]
    GP = G * P
    nb = SW // LANES
    gpb = LANES // H
    me = 4 * lax.axis_index("x") + 2 * lax.axis_index("y") + lax.axis_index("c")
    tm = _tile(L, 256, 16)
    x2 = x[0]
    tgt = loss_target[0]

    big = [("w_in", "col"), ("w_glu", "row"), ("w_ssm_out", "col"), ("w_conv_out", "col"), ("w_o", "row"),
           ("w_up", "col"), ("w_down", "row")]
    shards = [_cast_bf16("cast_" + n, args[n][0]) for n, _ in big]
    small_in = _pack([conv_w[0], ffn_conv_w[0]])
    kind = dict(big)
    mixw, ffnw = ["w_glu", "w_ssm_out", "w_conv_out", "w_o"], ["w_up", "w_down"]
    shard = dict(zip([n for n, _ in big], shards))
    gathered, _ = _seq_start("gather_in", "gather", [shard["w_in"], small_in], ["col", "row"], None)
    W = {"w_in": gathered[0]}
    st_mix, tok_mix = _seq_start("gather_mix", "gather", [shard[n] for n in mixw], [kind[n] for n in mixw], None)
    st_ffn, tok_ffn = _seq_start("gather_ffn", "gather", [shard[n] for n in ffnw], [kind[n] for n in ffnw], None)
    cw_parts, fcw_parts = _unpack(gathered[-1].reshape(N_DEV, -1, LANES), [conv_w.shape[1:], ffn_conv_w.shape[1:]], (N_DEV,))
    conv_w_full = jnp.moveaxis(cw_parts, 0, 1).reshape(3, CW)
    ffn_conv_w_full = jnp.moveaxis(fcw_parts, 0, 1).reshape(3, FF)

    ar_row, ai_row = a_re.reshape(1, GP), a_im.reshape(1, GP)
    ldt_row = jnp.broadcast_to(log_dt.reshape(G, 1), (G, P)).reshape(1, GP)
    brt = jnp.transpose(b_re[0], (2, 0, 1)).reshape(H, GP)
    bit = jnp.transpose(b_im[0], (2, 0, 1)).reshape(H, GP)
    abar_re, abar_im, bbar_re, bbar_im = _prep_fwd("s5_prep", ar_row, ai_row, ldt_row, brt, bit)
    eye = jnp.eye(gpb, dtype=F32)

    def b_blocks(bt):
        return jnp.einsum("ab,hjbp->jahbp", eye, bt.reshape(H, nb, gpb, P)).reshape(nb, LANES, gpb * P)

    def c_blocks(c):
        return jnp.einsum("ab,jahp->jbpah", eye, c.reshape(nb, gpb, H, P)).reshape(nb, gpb * P, LANES)

    def diag_blocks(mat):
        return jnp.einsum("jahap->hjap", mat.reshape(nb, gpb, H, gpb, P))

    bm_re, bm_im = b_blocks(bbar_re), b_blocks(bbar_im)
    cm_re, cm_im = c_blocks(c_re[0]), -c_blocks(c_im[0])
    a3_re, a3_im = abar_re.reshape(-1, 8, LANES), abar_im.reshape(-1, 8, LANES)
    dskip_row = d_skip.reshape(1, SW)

    cbs = SW // LANES
    cb_v, cb_gb, cb_gc = cbs, cbs + CW // LANES, cbs + 2 * CW // LANES
    cb_ma = (SW + 3 * CW) // D
    xn = _rms_fwd("rms_tok", x2, norm_tok, tm)
    proj = _mm("proj", xn, W["w_in"], "nn", out_dtype=BF16)
    xs_re, xs_im, y, ya = _s5_fwd("s5_fwd", proj, bm_re.astype(BF16), bm_im.astype(BF16), cm_re.astype(BF16),
                                  cm_im.astype(BF16), dskip_row, a3_re, a3_im)
    W.update(zip(mixw, _seq_wait("gather_mix", st_mix, ya)))
    g1 = _mm("glu_gate", ya, W["w_glu"], "nn", out_dtype=BF16)
    ya2 = _glu_fwd("glu", y, g1, tm)
    za = _mm("ssm_out", ya2, W["w_ssm_out"], "nn", out_dtype=BF16)
    q = _convb_fwd("convb", proj, cb_v, cb_gb, cb_gc, conv_w_full, conv_b)
    zb = _mm("conv_out", q, W["w_conv_out"], "nn", out_dtype=BF16)
    merged = _merge_fwd("merge", proj, cb_ma, cb_ma + 1, za, zb, tm)
    o1 = _mm("mix_out", merged, W["w_o"], "nn")
    h1, hn = _res_rms_fwd("rms_ffn", x2, o1, norm_ffn, tm)
    W.update(zip(ffnw, _seq_wait("gather_ffn", st_ffn, hn)))
    hh = _mm("ffn_up", hn, W["w_up"], "nn", out_dtype=BF16)
    f = _ffn_fwd("ffn_act", hh, ffn_conv_w_full, ffn_conv_b)
    o2 = _mm("ffn_down", f, W["w_down"], "nn", tk=2816)
    dh2, dh2b, g_norm_final, loss_part = _final("final", h1, o2, norm_final.reshape(1, D), tgt, tm)

    df = _mm("d_ffn_act", dh2b, W["w_down"], "nt", tn=1408, out_dtype=BF16)
    gw_down = _mm("gw_down", f, dh2b, "tn", out_dtype=BF16, tm=1408, tn=512, tk=L)
    df, gw_down = lax.optimization_barrier((df, gw_down))
    st_gdown, _ = _seq_start("grads_down", "exchange", [gw_down], [kind["w_down"]], None)
    dhh, g_ffn_conv_w, g_ffn_conv_b = _ffn_bwd("ffn_act_bwd", hh, ffn_conv_w_full, ffn_conv_b, df)
    nhalf = lambda t: FF // t
    gw_up = _mm("gw_up", hn, dhh, "tn", out_dtype=BF16, tn=_tile(FF, 1024), tk=L, dims=(D, 2 * FF, L),
                b_spec=lambda a, b, c: pl.BlockSpec((None, c, b), lambda i, j, k: (j // nhalf(b), k, j % nhalf(b))))
    dhh, gw_up = lax.optimization_barrier((dhh, gw_up))
    st_gup, _ = _seq_start("grads_up", "exchange", [gw_up], [kind["w_up"]], None)
    dhn = _mm("d_ffn_in", dhh, W["w_up"], "nt", out_dtype=BF16, tk=_tile(FF, 2816), dims=(L, D, 2 * FF),
              a_spec=lambda a, b, c: pl.BlockSpec((None, a, c), lambda i, j, k: (k // nhalf(c), i, k % nhalf(c))))
    dh1, dh1b, g_norm_ffn = _rms_bwd("rms_ffn_bwd", dhn, h1, norm_ffn, dh2, tm, True)

    dmerged = _mm("d_merged", dh1b, W["w_o"], "nt", out_dtype=BF16)
    gw_o = _mm("gw_o", merged, dh1b, "tn", out_dtype=BF16, tk=L)
    dmerged, gw_o = lax.optimization_barrier((dmerged, gw_o))
    dza, dzb, dma, dmb = _merge_bwd("merge_bwd", proj, cb_ma, cb_ma + 1, za, zb, dmerged, tm)
    dq = _mm("d_q", dzb, W["w_conv_out"], "nt", out_dtype=BF16)
    gw_conv_out = _mm("gw_conv_out", q, dzb, "tn", out_dtype=BF16, tk=L)
    dq, gw_conv_out = lax.optimization_barrier((dq, gw_conv_out))
    dv, dgb, dgc, g_conv_w, g_conv_b = _convb_bwd("convb_bwd", proj, cb_v, cb_gb, cb_gc, conv_w_full, conv_b, dq)
    dya2 = _mm("d_ya2", dza, W["w_ssm_out"], "nt", out_dtype=BF16)
    gw_ssm_out = _mm("gw_ssm_out", ya2, dza, "tn", out_dtype=BF16, tk=L)
    dya2, gw_ssm_out = lax.optimization_barrier((dya2, gw_ssm_out))
    dy_direct, dg1 = _glu_bwd("glu_bwd", y, g1, dya2, tm)
    dya_g = _mm("d_ya_gate", dg1, W["w_glu"], "nt", out_dtype=BF16)
    gw_glu = _mm("gw_glu", ya, dg1, "tn", out_dtype=BF16, tk=L)
    dya_g, gw_glu = lax.optimization_barrier((dya_g, gw_glu))
    st_gmix, tok_gmix = _seq_start("grads_mix", "exchange", [gw_glu, gw_ssm_out, gw_conv_out, gw_o],
                                   [kind[n] for n in mixw], None)
    dyb, g_dskip = _gelu_bwd("gelu_bwd", y, dy_direct, dya_g, proj, dskip_row, tm)
    swap = lambda m: jnp.swapaxes(m, 1, 2).astype(BF16)
    du, gb_re, gb_im, gc_re, gc_im, dab_re, dab_im = _s5_bwd(
        "s5_bwd", dyb, proj, xs_re, xs_im, swap(cm_re), swap(cm_im), swap(bm_re), swap(bm_im), dskip_row, a3_re, a3_im)
    parts = {"w_down": st_gdown[0], "w_up": st_gup[0]}
    g_ar, g_ai, g_ldt, g_brt, g_bit = _prep_bwd(
        "s5_prep_bwd", ar_row, ai_row, ldt_row, brt, bit, dab_re.reshape(1, GP), dab_im.reshape(1, GP),
        diag_blocks(gb_re).reshape(H, GP), diag_blocks(gb_im).reshape(H, GP))
    small = dict(
        a_re=g_ar.reshape(1, G, P), a_im=g_ai.reshape(1, G, P),
        log_dt=g_ldt.reshape(G, P).sum(axis=1).reshape(1, G),
        b_re=jnp.transpose(g_brt.reshape(H, G, P), (1, 2, 0))[None], b_im=jnp.transpose(g_bit.reshape(H, G, P), (1, 2, 0))[None],
        c_re=jnp.transpose(diag_blocks(gc_re), (1, 2, 0, 3)).reshape(1, G, H, P),
        c_im=-jnp.transpose(diag_blocks(gc_im), (1, 2, 0, 3)).reshape(1, G, H, P),
        d_skip=g_dskip.reshape(1, G, H), conv_b=g_conv_b, norm_ffn=g_norm_ffn, ffn_conv_b=g_ffn_conv_b,
        norm_final=g_norm_final.reshape(D), conv_w=g_conv_w[None], ffn_conv_w=g_ffn_conv_w[None])
    rep = ["a_re", "a_im", "log_dt", "b_re", "b_im", "c_re", "c_im", "d_skip", "conv_b", "norm_ffn", "ffn_conv_b", "norm_final"]
    order = rep + ["conv_w", "ffn_conv_w"]
    full_shapes = {n: args[n].shape for n in rep}
    full_shapes["conv_w"], full_shapes["ffn_conv_w"] = (1, 3, CW), (1, 3, FF)
    rep_pack = _pack([small[n] for n in rep], LANES)
    rep_rows = rep_pack.shape[0]
    gpack = jnp.concatenate([loss_part, rep_pack, _pack([small["conv_w"], small["ffn_conv_w"]])], axis=0)
    rep0 = loss_part.shape[0]
    rows = gpack.shape[0]
    du, gpack = lax.optimization_barrier((du, gpack))
    st_small, tok_small = _seq_start("grads_small", "gather", [gpack], ["row"], None)

    dproj = _concat_cols("dproj", [du, dv, dgb, dgc, dma, dmb], tm)
    gw_in = _mm("gw_in", xn, dproj, "tn", out_dtype=BF16, tk=L, dep=tok_small)
    dproj, gw_in = lax.optimization_barrier((dproj, gw_in))
    st_gin, tok_gin = _seq_start("grads_in", "exchange", [gw_in], ["col"], None)
    dxn = _mm("d_xn", dproj, W["w_in"], "nt", out_dtype=BF16)
    grad_x, g_norm_tok = _rms_bwd("rms_tok_bwd", dxn, x2, norm_tok, dh1, tm, False)

    res = {}

    def big_update(n):
        res[n] = [r[None] for r in _adamw("adamw_" + n, parts[n], args[n][0], args["m_" + n][0], args["v_" + n][0])]

    def after(xs, dep):
        return lax.optimization_barrier((list(xs), dep))[0]

    parts["w_down"] = after([parts["w_down"]], grad_x)[0]
    big_update("w_down")
    parts["w_up"] = after([parts["w_up"]], res["w_down"][1])[0]
    big_update("w_up")
    parts.update(zip(mixw, after(_seq_wait("grads_mix", st_gmix, grad_x), [res[n][1] for n in ffnw])))
    for n in mixw:
        big_update(n)
    gall = after(_seq_wait("grads_small", st_small, None), [res[n][1] for n in mixw])[0].reshape(N_DEV, rows, LANES)
    gcw, gfcw = _unpack(gall[:, rep0 + rep_rows:], [full_shapes["conv_w"], full_shapes["ffn_conv_w"]], (N_DEV,))
    cws, fcws = CW // N_DEV, FF // N_DEV
    gcw = lax.dynamic_slice_in_dim(gcw[:, 0], me * cws, cws, axis=2)
    gfcw = lax.dynamic_slice_in_dim(gfcw[:, 0], me * fcws, fcws, axis=2)
    res["conv_w"] = [r[None] for r in _adamw("adamw_conv_w", gcw, conv_w[0], m_conv_w[0], v_conv_w[0])]
    res["ffn_conv_w"] = [r[None] for r in _adamw("adamw_ffn_conv_w", gfcw, ffn_conv_w[0], m_ffn_conv_w[0], v_ffn_conv_w[0])]
    rep_out = _adamw("adamw_small", gall[:, rep0:rep0 + rep_rows], _pack([args[n] for n in rep], LANES),
                     _pack([args["m_" + n] for n in rep], LANES), _pack([args["v_" + n] for n in rep], LANES))
    rep_out = [_unpack(r, [full_shapes[n] for n in rep]) for r in rep_out]
    for i, n in enumerate(rep):
        res[n] = [r[i] for r in rep_out]
    nt_pack = after([_pack([g_norm_tok])], [res[n][1] for n in ("a_re", "conv_w", "ffn_conv_w")])
    nt_all = _all_gather("gather_norm_tok_grad", nt_pack, ["row"])[0].reshape(N_DEV, -1, LANES)
    nt_out = _adamw("adamw_norm_tok", nt_all, _pack([norm_tok]), _pack([m_norm_tok]), _pack([v_norm_tok]))
    res["norm_tok"] = [_unpack(r, [norm_tok.shape])[0] for r in nt_out]
    parts["w_in"] = after(_seq_wait("grads_in", st_gin, None), nt_out[0])[0]
    big_update("w_in")

    loss = jnp.sum(gall[:, 0, 0])
    names = ["norm_tok", "w_in", "a_re", "a_im", "log_dt", "b_re", "b_im", "c_re", "c_im", "d_skip", "w_glu", "w_ssm_out",
             "conv_w", "conv_b", "w_conv_out", "w_o", "norm_ffn", "w_up", "ffn_conv_w", "ffn_conv_b", "w_down", "norm_final"]
    out = [loss, grad_x[None]]
    for slot in range(4):
        out += [res[n][slot] for n in names]
    return tuple(out)
```

```python
import jax
import jax.numpy as jnp
from jax import lax
from jax.experimental import pallas as pl
from jax.experimental.pallas import tpu as pltpu
from jax.experimental.pallas import tpu_sc as plsc

F32 = jnp.float32
BF16 = jnp.bfloat16
N_DEV = 8
LANES = 128
SLAB = 4
EPS = 1e-6
ADAM_LR = 0.001
ADAM_B1 = 0.9
ADAM_B2 = 0.999
ADAM_EPS = 1e-08
ADAM_WD = 0.01
ADAM_STEP = 10
VMEM_LIMIT = 56 * 1024 * 1024
MESH = pl.DeviceIdType.MESH


def _tile(n, pref, mult=LANES):
    best = None
    t = mult
    while t <= min(n, pref):
        if n % t == 0:
            best = t
        t += mult
    return best if best is not None else n


def _params(ndim):
    return pltpu.CompilerParams(dimension_semantics=("arbitrary",) * ndim, vmem_limit_bytes=VMEM_LIMIT)


def _sds(shape, dtype):
    return jax.ShapeDtypeStruct(tuple(shape), dtype)


def _mm(name, a, b, mode, *, out_dtype=F32, tm=1024, tn=1024, tk=2048, dims=None, a_spec=None, b_spec=None):
    if dims is None:
        if mode == "nn":
            (M, K), N = a.shape, b.shape[1]
        elif mode == "nt":
            (M, K), N = a.shape, b.shape[0]
        else:
            (K, M), N = a.shape, b.shape[1]
    else:
        M, N, K = dims
    tm, tn, tk = _tile(M, tm), _tile(N, tn), _tile(K, tk)
    nk = K // tk
    if mode == "nn":
        dn = (((1,), (0,)), ((), ()))
        sa = pl.BlockSpec((tm, tk), lambda i, j, k: (i, k))
        sb = pl.BlockSpec((tk, tn), lambda i, j, k: (k, j))
    elif mode == "nt":
        dn = (((1,), (1,)), ((), ()))
        sa = pl.BlockSpec((tm, tk), lambda i, j, k: (i, k))
        sb = pl.BlockSpec((tn, tk), lambda i, j, k: (j, k))
    else:
        dn = (((0,), (0,)), ((), ()))
        sa = pl.BlockSpec((tk, tm), lambda i, j, k: (k, i))
        sb = pl.BlockSpec((tk, tn), lambda i, j, k: (k, j))
    sa = a_spec(tm, tn, tk) if a_spec is not None else sa
    sb = b_spec(tm, tn, tk) if b_spec is not None else sb
    use_acc = nk > 1 and out_dtype != F32

    def body(a_ref, b_ref, o_ref, *acc):
        k = pl.program_id(2)
        p = lax.dot_general(a_ref[...], b_ref[...], dn, preferred_element_type=F32)
        if nk == 1:
            o_ref[...] = p.astype(out_dtype)
        else:
            tgt = acc[0] if use_acc else o_ref

            @pl.when(k == 0)
            def _():
                tgt[...] = p

            @pl.when(k > 0)
            def _():
                tgt[...] += p

            if use_acc:
                @pl.when(k == nk - 1)
                def _():
                    o_ref[...] = acc[0][...].astype(out_dtype)

    return pl.pallas_call(
        body, name=name, grid=(M // tm, N // tn, nk),
        in_specs=[sa, sb], out_specs=pl.BlockSpec((tm, tn), lambda i, j, k: (i, j)),
        out_shape=_sds((M, N), out_dtype),
        scratch_shapes=[pltpu.VMEM((tm, tn), F32)] if use_acc else [],
        compiler_params=_params(3),
    )(a, b)


def _rows(name, body, L, tm, ins, outs):
    return pl.pallas_call(
        body, name=name, grid=(L // tm,),
        in_specs=[s for _, s in ins], out_specs=[s for _, s in outs],
        out_shape=[o for o, _ in outs], compiler_params=_params(1),
    )(*[a for a, _ in ins])


def _rs(tm, w, cb=0):
    return pl.BlockSpec((tm, w), lambda i: (i, cb))


def _fs(shape):
    return pl.BlockSpec(tuple(shape), lambda i: (0,) * len(shape))


def _acc_rows(i, ref, part):
    @pl.when(i == 0)
    def _():
        ref[...] = part

    @pl.when(i > 0)
    def _():
        ref[...] += part


def _cast_bf16(name, w):
    R, C = w.shape
    tr = _tile(R, max(16, (1 << 20) // C), 16)

    def body(w_ref, o_ref):
        o_ref[...] = w_ref[...].astype(BF16)

    return _rows(name, body, R, tr, [(w, _rs(tr, C))], [(_sds((R, C), BF16), _rs(tr, C))])[0]


def _concat_cols(name, pieces, tm):
    L = pieces[0].shape[0]
    widths = [p.shape[1] for p in pieces]

    def body(*refs):
        o_ref, off = refs[-1], 0
        for p_ref, w in zip(refs[:-1], widths):
            o_ref[:, off:off + w] = p_ref[...]
            off += w

    return _rows(name, body, L, tm, [(p, _rs(tm, w)) for p, w in zip(pieces, widths)],
                 [(_sds((L, sum(widths)), pieces[0].dtype), _rs(tm, sum(widths)))])[0]


def _rms_fwd(name, x, g, tm):
    L, D = x.shape

    def body(x_ref, g_ref, o_ref):
        xv = x_ref[...]
        r = lax.rsqrt(jnp.mean(xv * xv, axis=-1, keepdims=True) + EPS)
        o_ref[...] = (xv * r * g_ref[...]).astype(BF16)

    return _rows(name, body, L, tm, [(x, _rs(tm, D)), (g, _fs((1, D)))], [(_sds((L, D), BF16), _rs(tm, D))])[0]


def _res_rms_fwd(name, x, o, g, tm):
    L, D = x.shape

    def body(x_ref, o_ref, g_ref, h_ref, hn_ref):
        h = x_ref[...] + o_ref[...]
        r = lax.rsqrt(jnp.mean(h * h, axis=-1, keepdims=True) + EPS)
        h_ref[...] = h
        hn_ref[...] = (h * r * g_ref[...]).astype(BF16)

    return _rows(name, body, L, tm, [(x, _rs(tm, D)), (o, _rs(tm, D)), (g, _fs((1, D)))],
                 [(_sds((L, D), F32), _rs(tm, D)), (_sds((L, D), BF16), _rs(tm, D))])


def _rms_bwd(name, dn, h, g, dres, tm, with_bf16):
    L, D = h.shape

    def body(dn_ref, h_ref, g_ref, dres_ref, dh_ref, *rest):
        i = pl.program_id(0)
        h = h_ref[...]
        r = lax.rsqrt(jnp.mean(h * h, axis=-1, keepdims=True) + EPS)
        xh = h * r
        d = dn_ref[...].astype(F32)
        dxh = d * g_ref[...]
        dh = dres_ref[...] + r * (dxh - xh * jnp.mean(dxh * xh, axis=-1, keepdims=True))
        dh_ref[...] = dh
        if with_bf16:
            rest[0][...] = dh.astype(BF16)
        _acc_rows(i, rest[-1], jnp.sum(d * xh, axis=0, keepdims=True))

    outs = [(_sds((L, D), F32), _rs(tm, D))]
    if with_bf16:
        outs.append((_sds((L, D), BF16), _rs(tm, D)))
    outs.append((_sds((1, D), F32), _fs((1, D))))
    return _rows(name, body, L, tm, [(dn, _rs(tm, D)), (h, _rs(tm, D)), (g, _fs((1, D))), (dres, _rs(tm, D))], outs)


def _final(name, h1, o2, g, tgt, tm):
    L, D = h1.shape

    def body(h1_ref, o2_ref, g_ref, t_ref, dh_ref, dhb_ref, dg_ref, loss_ref):
        i = pl.program_id(0)
        h = h1_ref[...] + o2_ref[...]
        r = lax.rsqrt(jnp.mean(h * h, axis=-1, keepdims=True) + EPS)
        xh = h * r
        gv = g_ref[...]
        e = xh * gv - t_ref[...]
        part = 0.5 * jnp.sum(jnp.mean(e * e, axis=-1, keepdims=True), axis=0, keepdims=True)
        dy = e / D
        dxh = dy * gv
        dh = r * (dxh - xh * jnp.mean(dxh * xh, axis=-1, keepdims=True))
        dh_ref[...] = dh
        dhb_ref[...] = dh.astype(BF16)
        _acc_rows(i, dg_ref, jnp.sum(dy * xh, axis=0, keepdims=True))
        _acc_rows(i, loss_ref, jnp.broadcast_to(part, (8, LANES)))

    return _rows(name, body, L, tm,
                 [(h1, _rs(tm, D)), (o2, _rs(tm, D)), (g, _fs((1, D))), (tgt, _rs(tm, D))],
                 [(_sds((L, D), F32), _rs(tm, D)), (_sds((L, D), BF16), _rs(tm, D)),
                  (_sds((1, D), F32), _fs((1, D))), (_sds((8, LANES), F32), _fs((8, LANES)))])


def _gelu_and_grad(x):
    c, k = 0.7978845608028654, 0.044715
    x2 = x * x
    t = jnp.tanh(c * x * (1.0 + k * x2))
    half = 0.5 * x
    return half * (1.0 + t), 0.5 * (1.0 + t) + half * (1.0 - t * t) * (c * (1.0 + 3.0 * k * x2))


def _glu_fn(y, g1):
    ya = jax.nn.gelu(y)
    return ya * jax.nn.sigmoid(g1)


def _glu_fwd(name, y, g1, tm):
    L, W = y.shape

    def body(y_ref, g_ref, o_ref):
        o_ref[...] = _glu_fn(y_ref[...], g_ref[...].astype(F32)).astype(BF16)

    return _rows(name, body, L, tm, [(y, _rs(tm, W)), (g1, _rs(tm, W))], [(_sds((L, W), BF16), _rs(tm, W))])[0]


def _glu_bwd(name, y, g1, dya2, tm):
    L, W = y.shape

    def body(y_ref, g_ref, d_ref, dy_ref, dg_ref):
        _, vjp = jax.vjp(_glu_fn, y_ref[...], g_ref[...].astype(F32))
        dy, dg = vjp(d_ref[...].astype(F32))
        dy_ref[...] = dy
        dg_ref[...] = dg.astype(BF16)

    return _rows(name, body, L, tm, [(y, _rs(tm, W)), (g1, _rs(tm, W)), (dya2, _rs(tm, W))],
                 [(_sds((L, W), F32), _rs(tm, W)), (_sds((L, W), BF16), _rs(tm, W))])


def _gelu_bwd(name, y, dy_direct, dya_g, proj, tm):
    L, W = y.shape

    def body(y_ref, dd_ref, dg_ref, u_ref, dyb_ref, dsk_ref):
        i = pl.program_id(0)
        dy = dd_ref[...] + dg_ref[...].astype(F32) * _gelu_and_grad(y_ref[...])[1]
        dyb_ref[...] = dy.astype(BF16)
        _acc_rows(i, dsk_ref, jnp.sum(dy * u_ref[...].astype(F32), axis=0, keepdims=True))

    return _rows(name, body, L, tm,
                 [(y, _rs(tm, W)), (dy_direct, _rs(tm, W)), (dya_g, _rs(tm, W)), (proj, _rs(tm, W, 0))],
                 [(_sds((L, W), BF16), _rs(tm, W)), (_sds((1, W), F32), _fs((1, W)))])


def _merge_fn(ma, mb, za, zb):
    return jax.nn.sigmoid(ma) * za + jax.nn.sigmoid(mb) * zb


def _merge_fwd(name, proj, cb_a, cb_b, za, zb, tm):
    L, D = za.shape

    def body(ma_ref, mb_ref, za_ref, zb_ref, o_ref):
        o_ref[...] = _merge_fn(ma_ref[...].astype(F32), mb_ref[...].astype(F32), za_ref[...].astype(F32),
                               zb_ref[...].astype(F32)).astype(BF16)

    return _rows(name, body, L, tm,
                 [(proj, _rs(tm, D, cb_a)), (proj, _rs(tm, D, cb_b)), (za, _rs(tm, D)), (zb, _rs(tm, D))],
                 [(_sds((L, D), BF16), _rs(tm, D))])[0]


def _merge_bwd(name, proj, cb_a, cb_b, za, zb, dmerged, tm):
    L, D = za.shape

    def body(ma_ref, mb_ref, za_ref, zb_ref, d_ref, dza_ref, dzb_ref, dma_ref, dmb_ref):
        _, vjp = jax.vjp(_merge_fn, ma_ref[...].astype(F32), mb_ref[...].astype(F32), za_ref[...].astype(F32),
                         zb_ref[...].astype(F32))
        dma, dmb, dza, dzb = vjp(d_ref[...].astype(F32))
        dza_ref[...] = dza.astype(BF16)
        dzb_ref[...] = dzb.astype(BF16)
        dma_ref[...] = dma.astype(BF16)
        dmb_ref[...] = dmb.astype(BF16)

    return _rows(name, body, L, tm,
                 [(proj, _rs(tm, D, cb_a)), (proj, _rs(tm, D, cb_b)), (za, _rs(tm, D)), (zb, _rs(tm, D)),
                  (dmerged, _rs(tm, D))],
                 [(_sds((L, D), BF16), _rs(tm, D)), (_sds((L, D), BF16), _rs(tm, D)),
                  (_sds((L, D), BF16), _rs(tm, D)), (_sds((L, D), BF16), _rs(tm, D))])


def _shift_down(x, k):
    row = lax.broadcasted_iota(jnp.int32, x.shape, 0)
    return jnp.where(row >= k, pltpu.roll(x, k, axis=0), 0.0)


def _shift_up(x, k):
    n = x.shape[0]
    row = lax.broadcasted_iota(jnp.int32, x.shape, 0)
    return jnp.where(row < n - k, pltpu.roll(x, n - k, axis=0), 0.0)


def _conv3(cv, w_ref, b_ref):
    return (w_ref[2:3, :] * cv + w_ref[1:2, :] * _shift_down(cv, 1) + w_ref[0:1, :] * _shift_down(cv, 2)
            + b_ref[...])


def _conv3_bwd(dcc, cv, w_ref):
    dcv = w_ref[2:3, :] * dcc + w_ref[1:2, :] * _shift_up(dcc, 1) + w_ref[0:1, :] * _shift_up(dcc, 2)
    dw = [jnp.sum(dcc * _shift_down(cv, 2), axis=0, keepdims=True),
          jnp.sum(dcc * _shift_down(cv, 1), axis=0, keepdims=True),
          jnp.sum(dcc * cv, axis=0, keepdims=True)]
    db = jnp.sum(dcc, axis=0, keepdims=True)
    return dcv, dw, db


def _store_rows(ref, rows):
    for r, val in enumerate(rows):
        ref[r:r + 1, :] = val


def _cols(name, body, ncb, ins, outs):
    return pl.pallas_call(
        body, name=name, grid=(ncb,),
        in_specs=[s for _, s in ins], out_specs=[s for _, s in outs],
        out_shape=[o for o, _ in outs], compiler_params=_params(1),
    )(*[a for a, _ in ins])


def _cb(L, w, off=0):
    return pl.BlockSpec((L, w), lambda j: (0, j + off))


def _convb_fwd(name, proj, cb_v, cb_gb, cb_gc, w, b):
    L = proj.shape[0]
    W = w.shape[1]
    c = LANES

    def body(v_ref, gb_ref, gc_ref, w_ref, b_ref, q_ref):
        cc = _conv3(gc_ref[...].astype(F32) * v_ref[...].astype(F32), w_ref, b_ref)
        q_ref[...] = (gb_ref[...].astype(F32) * cc).astype(BF16)

    return _cols(name, body, W // c,
                 [(proj, _cb(L, c, cb_v)), (proj, _cb(L, c, cb_gb)), (proj, _cb(L, c, cb_gc)),
                  (w, _cb(3, c)), (b, _cb(1, c))],
                 [(_sds((L, W), BF16), _cb(L, c))])[0]


def _convb_bwd(name, proj, cb_v, cb_gb, cb_gc, w, b, dq):
    L = proj.shape[0]
    W = w.shape[1]
    c = LANES

    def body(v_ref, gb_ref, gc_ref, w_ref, b_ref, dq_ref, dv_ref, dgb_ref, dgc_ref, dw_ref, db_ref):
        v, gc = v_ref[...].astype(F32), gc_ref[...].astype(F32)
        cv = gc * v
        cc = _conv3(cv, w_ref, b_ref)
        dq = dq_ref[...].astype(F32)
        dgb_ref[...] = (dq * cc).astype(BF16)
        dcv, dw, db = _conv3_bwd(dq * gb_ref[...].astype(F32), cv, w_ref)
        dv_ref[...] = (dcv * gc).astype(BF16)
        dgc_ref[...] = (dcv * v).astype(BF16)
        _store_rows(dw_ref, dw)
        db_ref[...] = db

    return _cols(name, body, W // c,
                 [(proj, _cb(L, c, cb_v)), (proj, _cb(L, c, cb_gb)), (proj, _cb(L, c, cb_gc)),
                  (w, _cb(3, c)), (b, _cb(1, c)), (dq, _cb(L, c))],
                 [(_sds((L, W), BF16), _cb(L, c)), (_sds((L, W), BF16), _cb(L, c)), (_sds((L, W), BF16), _cb(L, c)),
                  (_sds((3, W), F32), _cb(3, c)), (_sds((1, W), F32), _cb(1, c))])


HALO = 16


def _ffn_tiles(L, Fw):
    tr = _tile(L, 256, HALO)
    tc = _tile(Fw, 1408)
    return tr, tc, Fw // tc, L // tr, tr // HALO


def _ffn_fwd(name, hh, w, b):
    L = hh.shape[0]
    Fw = w.shape[1]
    tr, tc, ncb, nrt, rpt = _ffn_tiles(L, Fw)

    def body(a_ref, p_ref, h2_ref, w_ref, b_ref, f_ref):
        first = pl.program_id(1) == 0
        for c0 in range(0, tc, LANES):
            cs = slice(c0, c0 + LANES)
            prev = jnp.where(first, 0.0, p_ref[:, cs].astype(F32))
            x = jnp.concatenate([prev, a_ref[:, cs].astype(F32)], axis=0)
            n = x.shape[0]
            a = (w_ref[2:3, cs] * x + w_ref[1:2, cs] * pltpu.roll(x, 1, axis=0) + w_ref[0:1, cs] * pltpu.roll(x, 2, axis=0)
                 + b_ref[:, cs])[HALO:n]
            f_ref[:, cs] = (_gelu_and_grad(a)[0] * h2_ref[:, cs].astype(F32)).astype(BF16)

    main = pl.BlockSpec((tr, tc), lambda j, i: (i, j))
    return pl.pallas_call(
        body, name=name, grid=(ncb, nrt),
        in_specs=[main, pl.BlockSpec((HALO, tc), lambda j, i: (jnp.maximum(i * rpt - 1, 0), j)),
                  pl.BlockSpec((tr, tc), lambda j, i: (i, j + ncb)),
                  pl.BlockSpec((3, tc), lambda j, i: (0, j)), pl.BlockSpec((1, tc), lambda j, i: (0, j))],
        out_specs=main, out_shape=_sds((L, Fw), BF16), compiler_params=_params(2),
    )(hh, hh, hh, w, b)


def _ffn_bwd(name, hh, w, b, df):
    L = hh.shape[0]
    Fw = w.shape[1]
    tr, tc, ncb, nrt, rpt = _ffn_tiles(L, Fw)

    def body(a_ref, ap_ref, an_ref, h2_ref, h2n_ref, df_ref, dfn_ref, w_ref, b_ref, dhh_ref, dw_ref, db_ref):
        i = pl.program_id(1)
        first, last = i == 0, i == nrt - 1
        for c0 in range(0, tc, LANES):
            cs = slice(c0, c0 + LANES)
            zero = jnp.zeros((HALO, LANES), F32)
            h1 = jnp.concatenate([jnp.where(first, 0.0, ap_ref[:, cs].astype(F32)), a_ref[:, cs].astype(F32),
                                  an_ref[:, cs].astype(F32)], axis=0)
            h2 = jnp.concatenate([zero, h2_ref[:, cs].astype(F32), h2n_ref[:, cs].astype(F32)], axis=0)
            d = jnp.concatenate([zero, df_ref[:, cs].astype(F32), jnp.where(last, 0.0, dfn_ref[:, cs].astype(F32))], axis=0)
            n = h1.shape[0]
            s1, s2 = pltpu.roll(h1, 1, axis=0), pltpu.roll(h1, 2, axis=0)
            a = w_ref[2:3, cs] * h1 + w_ref[1:2, cs] * s1 + w_ref[0:1, cs] * s2 + b_ref[:, cs]
            ga, dga = _gelu_and_grad(a)
            da = d * h2 * dga
            dh1 = w_ref[2:3, cs] * da + w_ref[1:2, cs] * pltpu.roll(da, n - 1, axis=0) + w_ref[0:1, cs] * pltpu.roll(da, n - 2, axis=0)
            dhh_ref[0, :, cs] = dh1[HALO:HALO + tr].astype(BF16)
            dhh_ref[1, :, cs] = (d * ga)[HALO:HALO + tr].astype(BF16)
            dam = da[HALO:HALO + tr]
            rows = [jnp.sum(dam * s2[HALO:HALO + tr], axis=0, keepdims=True),
                    jnp.sum(dam * s1[HALO:HALO + tr], axis=0, keepdims=True),
                    jnp.sum(dam * h1[HALO:HALO + tr], axis=0, keepdims=True),
                    jnp.sum(dam, axis=0, keepdims=True)]

            @pl.when(first)
            def _():
                for r in range(3):
                    dw_ref[r:r + 1, cs] = rows[r]
                db_ref[:, cs] = rows[3]

            @pl.when(i > 0)
            def _():
                for r in range(3):
                    dw_ref[r:r + 1, cs] += rows[r]
                db_ref[:, cs] += rows[3]

    def spec(col_off, kind):
        if kind == "main":
            return pl.BlockSpec((tr, tc), lambda j, i: (i, j + col_off))
        if kind == "prev":
            return pl.BlockSpec((HALO, tc), lambda j, i: (jnp.maximum(i * rpt - 1, 0), j + col_off))
        return pl.BlockSpec((HALO, tc), lambda j, i: (jnp.minimum((i + 1) * rpt, nrt * rpt - 1), j + col_off))

    return pl.pallas_call(
        body, name=name, grid=(ncb, nrt),
        in_specs=[spec(0, "main"), spec(0, "prev"), spec(0, "next"), spec(ncb, "main"), spec(ncb, "next"),
                  spec(0, "main"), spec(0, "next"),
                  pl.BlockSpec((3, tc), lambda j, i: (0, j)), pl.BlockSpec((1, tc), lambda j, i: (0, j))],
        out_specs=[pl.BlockSpec((2, tr, tc), lambda j, i: (0, i, j)),
                   pl.BlockSpec((3, tc), lambda j, i: (0, j)), pl.BlockSpec((1, tc), lambda j, i: (0, j))],
        out_shape=[_sds((2, L, Fw), BF16), _sds((3, Fw), F32), _sds((1, Fw), F32)], compiler_params=_params(2),
    )(hh, hh, hh, hh, hh, df, df, w, b)


def _prep_fn(ar, ai, ldt, brt, bit):
    dt = jnp.exp(ldt)
    mag = jnp.exp(dt * ar)
    are = mag * jnp.cos(dt * ai)
    aim = mag * jnp.sin(dt * ai)
    nr = are - 1.0
    ni = aim
    den = ar * ar + ai * ai
    fr = (nr * ar + ni * ai) / den
    fi = (ni * ar - nr * ai) / den
    return are, aim, fr * brt - fi * bit, fr * bit + fi * brt


def _prep_fwd(name, ar, ai, ldt, brt, bit):
    def body(ar_ref, ai_ref, l_ref, br_ref, bi_ref, o1, o2, o3, o4):
        o1[...], o2[...], o3[...], o4[...] = _prep_fn(ar_ref[...], ai_ref[...], l_ref[...], br_ref[...], bi_ref[...])

    return pl.pallas_call(body, name=name,
                          out_shape=[_sds(ar.shape, F32), _sds(ar.shape, F32), _sds(brt.shape, F32), _sds(brt.shape, F32)],
                          )(ar, ai, ldt, brt, bit)


def _prep_bwd(name, ar, ai, ldt, brt, bit, g1, g2, g3, g4):
    def body(ar_ref, ai_ref, l_ref, br_ref, bi_ref, g1_ref, g2_ref, g3_ref, g4_ref, o1, o2, o3, o4, o5):
        _, vjp = jax.vjp(_prep_fn, ar_ref[...], ai_ref[...], l_ref[...], br_ref[...], bi_ref[...])
        o1[...], o2[...], o3[...], o4[...], o5[...] = vjp((g1_ref[...], g2_ref[...], g3_ref[...], g4_ref[...]))

    return pl.pallas_call(body, name=name,
                          out_shape=[_sds(ar.shape, F32)] * 3 + [_sds(brt.shape, F32)] * 2,
                          )(ar, ai, ldt, brt, bit, g1, g2, g3, g4)


def _scan_steps(tc, pitch, ng, reverse, a_r, a_i, stage_b, stage_x, out_re, out_im, st_re, st_im, acc):
    def step(tt, carry):
        t = (tc - 1 - tt) if reverse else tt
        new = []
        for g in range(ng):
            rows = pl.ds(g * 8 * pitch + t, 8, stride=pitch)
            cr, ci = carry[2 * g], carry[2 * g + 1]
            br, bi = stage_b[0][rows, :], stage_b[1][rows, :]
            if reverse:
                xr, xi = stage_x[0][rows, :], stage_x[1][rows, :]
                acc[0][g] += xr * cr + xi * ci
                acc[1][g] += xr * ci - xi * cr
                nr = a_r[g] * cr + a_i[g] * ci + br
                ni = a_r[g] * ci - a_i[g] * cr + bi
            else:
                nr = a_r[g] * cr - a_i[g] * ci + br
                ni = a_r[g] * ci + a_i[g] * cr + bi
            out_re[rows, :] = nr
            out_im[rows, :] = ni
            new += [nr, ni]
        return tuple(new)

    init = []
    for g in range(ng):
        init += [st_re[g], st_im[g]]
    fin = lax.fori_loop(0, tc, step, tuple(init), unroll=2)
    for g in range(ng):
        st_re[g] = fin[2 * g]
        st_im[g] = fin[2 * g + 1]


def _s5_fwd(name, proj, bm_re, bm_im, cm_re, cm_im, dskip, a_re, a_im):
    L = proj.shape[0]
    nb = bm_re.shape[0]
    ns, W = SLAB * nb, nb * LANES
    ng = ns // 8
    tc = min(2 * LANES, L)
    pitch = tc + 8
    wide = SLAB * LANES

    def body(u_ref, bre_ref, bim_ref, cre_ref, cim_ref, d_ref, ar_ref, ai_ref, xr_ref, xi_ref, y_ref, ya_ref,
             sb_re, sb_im, out_re, out_im, st_re, st_im):
        @pl.when(pl.program_id(0) == 0)
        def _():
            st_re[...] = jnp.zeros(st_re.shape, F32)
            st_im[...] = jnp.zeros(st_im.shape, F32)

        for j in range(nb):
            ub = u_ref[:, j * LANES:(j + 1) * LANES]
            r1 = jnp.dot(ub, bre_ref[j], preferred_element_type=F32)
            r2 = jnp.dot(ub, bim_ref[j], preferred_element_type=F32)
            for q in range(SLAB):
                sb_re[pl.ds((SLAB * j + q) * pitch, tc), :] = r1[:, q * LANES:(q + 1) * LANES]
                sb_im[pl.ds((SLAB * j + q) * pitch, tc), :] = r2[:, q * LANES:(q + 1) * LANES]
        a_r = [ar_ref[g] for g in range(ng)]
        a_i = [ai_ref[g] for g in range(ng)]
        _scan_steps(tc, pitch, ng, False, a_r, a_i, (sb_re, sb_im), None, out_re, out_im, st_re, st_im, None)
        for j in range(nb):
            x1 = [out_re[pl.ds((SLAB * j + q) * pitch, tc), :].astype(BF16) for q in range(SLAB)]
            x2 = [out_im[pl.ds((SLAB * j + q) * pitch, tc), :].astype(BF16) for q in range(SLAB)]
            for q in range(SLAB):
                xr_ref[SLAB * j + q] = x1[q]
                xi_ref[SLAB * j + q] = x2[q]
            cols = slice(j * LANES, (j + 1) * LANES)
            y = (jnp.dot(jnp.concatenate(x1, axis=1), cre_ref[j], preferred_element_type=F32)
                 + jnp.dot(jnp.concatenate(x2, axis=1), cim_ref[j], preferred_element_type=F32)
                 + d_ref[:, cols] * u_ref[:, cols].astype(F32))
            y_ref[:, cols] = y
            ya_ref[:, cols] = jax.nn.gelu(y).astype(BF16)

    full3 = lambda s: pl.BlockSpec(s, lambda i: (0, 0, 0))
    xs = pl.BlockSpec((ns, tc, LANES), lambda i: (0, i, 0))
    rows = pl.BlockSpec((tc, W), lambda i: (i, 0))
    return pl.pallas_call(
        body, name=name, grid=(L // tc,),
        in_specs=[rows, full3((nb, LANES, wide)), full3((nb, LANES, wide)), full3((nb, wide, LANES)),
                  full3((nb, wide, LANES)), pl.BlockSpec((1, W), lambda i: (0, 0)), full3((ng, 8, LANES)), full3((ng, 8, LANES))],
        out_specs=[xs, xs, rows, rows],
        out_shape=[_sds((ns, L, LANES), BF16)] * 2 + [_sds((L, W), F32), _sds((L, W), BF16)],
        scratch_shapes=[pltpu.VMEM((ns * pitch, LANES), F32)] * 4 + [pltpu.VMEM((ng, 8, LANES), F32)] * 2,
        compiler_params=_params(1),
    )(proj, bm_re, bm_im, cm_re, cm_im, dskip, a_re, a_im)


def _s5_bwd(name, dyb, proj, xs_re, xs_im, cmt_re, cmt_im, bmt_re, bmt_im, dskip, a_re, a_im):
    L = dyb.shape[0]
    nb = cmt_re.shape[0]
    ns, W = SLAB * nb, nb * LANES
    ng = ns // 8
    tc = min(LANES, L)
    pitch = tc + 8
    nt = L // tc
    wide = SLAB * LANES
    dn = (((0,), (0,)), ((), ()))

    def body(dy_ref, u_ref, xr_ref, xi_ref, cre_ref, cim_ref, bre_ref, bim_ref, d_ref, ar_ref, ai_ref,
             du_ref, gbr_ref, gbi_ref, gcr_ref, gci_ref, dar_ref, dai_ref,
             sd_re, sd_im, sx_re, sx_im, out_re, out_im, st_re, st_im, acc_re, acc_im):
        first = pl.program_id(0) == 0

        @pl.when(first)
        def _():
            for r in (st_re, st_im, acc_re, acc_im):
                r[...] = jnp.zeros(r.shape, F32)
            for r in (gbr_ref, gbi_ref, gcr_ref, gci_ref):
                r[...] = jnp.zeros(r.shape, F32)

        for j in range(nb):
            dyj = dy_ref[:, j * LANES:(j + 1) * LANES]
            r1 = jnp.dot(dyj, cre_ref[j], preferred_element_type=F32)
            r2 = jnp.dot(dyj, cim_ref[j], preferred_element_type=F32)
            for q in range(SLAB):
                s = SLAB * j + q
                sd_re[pl.ds(s * pitch, tc), :] = r1[:, q * LANES:(q + 1) * LANES]
                sd_im[pl.ds(s * pitch, tc), :] = r2[:, q * LANES:(q + 1) * LANES]
                sx_re[pl.ds(s * pitch, tc), :] = xr_ref[s].astype(F32)
                sx_im[pl.ds(s * pitch, tc), :] = xi_ref[s].astype(F32)
        a_r = [ar_ref[g] for g in range(ng)]
        a_i = [ai_ref[g] for g in range(ng)]
        _scan_steps(tc, pitch, ng, True, a_r, a_i, (sd_re, sd_im), (sx_re, sx_im), out_re, out_im, st_re, st_im,
                    (acc_re, acc_im))
        for j in range(nb):
            cols = slice(j * LANES, (j + 1) * LANES)
            l1 = jnp.concatenate([out_re[pl.ds((SLAB * j + q) * pitch, tc), :] for q in range(SLAB)], axis=1).astype(BF16)
            l2 = jnp.concatenate([out_im[pl.ds((SLAB * j + q) * pitch, tc), :] for q in range(SLAB)], axis=1).astype(BF16)
            dyj = dy_ref[:, cols]
            du = (jnp.dot(l1, bre_ref[j], preferred_element_type=F32) + jnp.dot(l2, bim_ref[j], preferred_element_type=F32)
                  + d_ref[:, cols] * dyj.astype(F32))
            du_ref[:, cols] = du.astype(BF16)
            uj = u_ref[:, cols]
            gbr_ref[j] += lax.dot_general(uj, l1, dn, preferred_element_type=F32)
            gbi_ref[j] += lax.dot_general(uj, l2, dn, preferred_element_type=F32)
            x1 = jnp.concatenate([xr_ref[SLAB * j + q] for q in range(SLAB)], axis=1)
            x2 = jnp.concatenate([xi_ref[SLAB * j + q] for q in range(SLAB)], axis=1)
            gcr_ref[j] += lax.dot_general(dyj, x1, dn, preferred_element_type=F32)
            gci_ref[j] += lax.dot_general(dyj, x2, dn, preferred_element_type=F32)
        dar_ref[...] = acc_re[...]
        dai_ref[...] = acc_im[...]

    full3 = lambda s: pl.BlockSpec(s, lambda i: (0, 0, 0))
    xs = pl.BlockSpec((ns, tc, LANES), lambda i: (0, nt - 1 - i, 0))
    rows = pl.BlockSpec((tc, W), lambda i: (nt - 1 - i, 0))
    mat_a, mat_b = full3((nb, LANES, wide)), full3((nb, wide, LANES))
    vec = full3((ng, 8, LANES))
    return pl.pallas_call(
        body, name=name, grid=(nt,),
        in_specs=[rows, rows, xs, xs, mat_a, mat_a, mat_b, mat_b, pl.BlockSpec((1, W), lambda i: (0, 0)), vec, vec],
        out_specs=[rows, mat_a, mat_a, mat_a, mat_a, vec, vec],
        out_shape=[_sds((L, W), BF16)] + [_sds((nb, LANES, wide), F32)] * 4 + [_sds((ng, 8, LANES), F32)] * 2,
        scratch_shapes=[pltpu.VMEM((ns * pitch, LANES), F32)] * 6 + [pltpu.VMEM((ng, 8, LANES), F32)] * 4,
        compiler_params=_params(1),
    )(dyb, proj, xs_re, xs_im, cmt_re, cmt_im, bmt_re, bmt_im, dskip, a_re, a_im)


def _peer(k):
    x, y, c = lax.axis_index("x"), lax.axis_index("y"), lax.axis_index("c")
    px = 1 - x if (k >> 2) & 1 else x
    py = 1 - y if (k >> 1) & 1 else y
    pc = 1 - c if k & 1 else c
    return (px, py, pc), 4 * px + 2 * py + pc


def _window(ref, kind, idx, n):
    if kind == "col":
        w = ref.shape[1] // n
        return ref.at[:, pl.ds(pl.multiple_of(idx * w, LANES), w)]
    r = ref.shape[0] // n
    return ref.at[pl.ds(pl.multiple_of(idx * r, 8), r), :]


def _all_gather(name, shards, kinds):
    n = len(shards)
    fulls = []
    for s, kind in zip(shards, kinds):
        fulls.append(_sds((s.shape[0], s.shape[1] * N_DEV) if kind == "col" else (s.shape[0] * N_DEV, s.shape[1]), s.dtype))

    def body(*refs):
        src, dst = refs[:n], refs[n:2 * n]
        send, recv, loc = refs[2 * n:]
        me = 4 * lax.axis_index("x") + 2 * lax.axis_index("y") + lax.axis_index("c")
        copies = []
        for a in range(n):
            own = pltpu.make_async_copy(src[a], _window(dst[a], kinds[a], me, N_DEV), loc.at[a])
            own.start()
            copies.append(own)
        sends = []
        for k in range(1, N_DEV):
            dev, _ = _peer(k)
            for a in range(n):
                cp = pltpu.make_async_remote_copy(
                    src_ref=src[a], dst_ref=_window(dst[a], kinds[a], me, N_DEV),
                    send_sem=send.at[a * N_DEV + k], recv_sem=recv.at[a * N_DEV + k],
                    device_id=dev, device_id_type=MESH)
                cp.start()
                sends.append(cp)
        for k in range(1, N_DEV):
            dev, pidx = _peer(k)
            for a in range(n):
                pltpu.make_async_remote_copy(
                    src_ref=src[a], dst_ref=_window(dst[a], kinds[a], pidx, N_DEV),
                    send_sem=send.at[a * N_DEV + k], recv_sem=recv.at[a * N_DEV + k],
                    device_id=dev, device_id_type=MESH).wait_recv()
        for cp in sends:
            cp.wait_send()
        for cp in copies:
            cp.wait()

    any_ = pl.BlockSpec(memory_space=pl.ANY)
    return pl.pallas_call(
        body, name=name, in_specs=[any_] * n, out_specs=[any_] * n, out_shape=fulls,
        scratch_shapes=[pltpu.SemaphoreType.DMA((n * N_DEV,)), pltpu.SemaphoreType.DMA((n * N_DEV,)),
                        pltpu.SemaphoreType.DMA((n,))],
        compiler_params=pltpu.CompilerParams(has_side_effects=True),
    )(*shards)


def _xfer_refs(mode, kinds, a, src, dst, me, pidx):
    if mode == "gather":
        return src[a], _window(dst[a], kinds[a], me, N_DEV), _window(dst[a], kinds[a], pidx, N_DEV)
    return _window(src[a], kinds[a], pidx, N_DEV), dst[a].at[me], dst[a].at[pidx]


def _xfer_out_shapes(mode, arrs, kinds):
    outs = []
    for s, kind in zip(arrs, kinds):
        if mode == "gather":
            outs.append((s.shape[0], s.shape[1] * N_DEV) if kind == "col" else (s.shape[0] * N_DEV, s.shape[1]))
        else:
            outs.append((N_DEV,) + ((s.shape[0], s.shape[1] // N_DEV) if kind == "col" else (s.shape[0] // N_DEV, s.shape[1])))
    return outs


def _sc_xfer(name, mode, arrs, kinds, collective_id):
    n = len(arrs)
    shapes = _xfer_out_shapes(mode, arrs, kinds)
    hbm = pltpu.MemorySpace.HBM
    src = [jax.new_ref(a, memory_space=hbm) for a in arrs]
    dst = [jax.empty_ref(_sds(shp, a.dtype), memory_space=hbm) for shp, a in zip(shapes, arrs)]

    @pl.kernel(mesh=plsc.ScalarSubcoreMesh(axis_name="seq", num_cores=1), name=name,
               scratch_types=(pltpu.SemaphoreType.DMA((n * N_DEV,)), pltpu.SemaphoreType.DMA((n * N_DEV,)),
                              pltpu.SemaphoreType.DMA((n,))),
               compiler_params=pltpu.CompilerParams(collective_id=collective_id))
    def launch(send, recv, loc):
        barrier = pltpu.get_barrier_semaphore()
        for k in range(1, N_DEV):
            pl.semaphore_signal(barrier, inc=1, device_id=_peer(k)[0], device_id_type=MESH)
        pl.semaphore_wait(barrier, N_DEV - 1)
        me = 4 * lax.axis_index("x") + 2 * lax.axis_index("y") + lax.axis_index("c")
        own, sends = [], []
        for a in range(n):
            s, _, d = _xfer_refs(mode, kinds, a, src, dst, me, me)
            own.append(pltpu.make_async_copy(s, d, loc.at[a]))
            own[-1].start()
        for k in range(1, N_DEV):
            dev, pidx = _peer(k)
            for a in range(n):
                s, d, _ = _xfer_refs(mode, kinds, a, src, dst, me, pidx)
                sends.append(pltpu.make_async_remote_copy(src_ref=s, dst_ref=d, send_sem=send.at[a * N_DEV + k],
                                                          recv_sem=recv.at[a * N_DEV + k], device_id=dev, device_id_type=MESH))
                sends[-1].start()
        for cp in own:
            cp.wait()
        for k in range(1, N_DEV):
            dev, pidx = _peer(k)
            for a in range(n):
                s, _, land = _xfer_refs(mode, kinds, a, src, dst, me, pidx)
                pltpu.make_async_remote_copy(src_ref=s, dst_ref=land, send_sem=send.at[a * N_DEV + k],
                                             recv_sem=recv.at[a * N_DEV + k], device_id=dev, device_id_type=MESH).wait_recv()
        for cp in sends:
            cp.wait_send()

    launch()
    return [d[...] for d in dst]


def _sc_gather(name, arrs, kinds, collective_id):
    n = len(arrs)
    pairs = 7
    shapes = _xfer_out_shapes("gather", arrs, kinds)
    hbm = pltpu.MemorySpace.HBM
    src = [jax.new_ref(a, memory_space=hbm) for a in arrs]
    dst = [jax.empty_ref(_sds(shp, a.dtype), memory_space=hbm) for shp, a in zip(shapes, arrs)]

    @pl.kernel(mesh=plsc.ScalarSubcoreMesh(axis_name="seq", num_cores=1), name=name,
               scratch_types=(pltpu.SemaphoreType.DMA((n * pairs,)), pltpu.SemaphoreType.DMA((n * pairs,)),
                              pltpu.SemaphoreType.DMA((n,))),
               compiler_params=pltpu.CompilerParams(collective_id=collective_id))
    def launch(send, recv, loc):
        x, y, c = lax.axis_index("x"), lax.axis_index("y"), lax.axis_index("c")
        me = 4 * x + 2 * y + c
        sib = (x, y, 1 - c)
        chips = []
        for fx, fy in ((1, 0), (0, 1), (1, 1)):
            px, py = (1 - x if fx else x), (1 - y if fy else y)
            chips.append(((px, py, c), 4 * px + 2 * py + c, 4 * px + 2 * py + (1 - c)))
        barrier = pltpu.get_barrier_semaphore()
        for dev in [sib] + [ch[0] for ch in chips]:
            pl.semaphore_signal(barrier, inc=1, device_id=dev, device_id_type=MESH)
        pl.semaphore_wait(barrier, 4)

        def win(a, idx):
            return _window(dst[a], kinds[a], idx, N_DEV)

        def rcopy(a, p, s, d, dev):
            return pltpu.make_async_remote_copy(src_ref=s, dst_ref=d, send_sem=send.at[a * pairs + p],
                                                recv_sem=recv.at[a * pairs + p], device_id=dev, device_id_type=MESH)

        own, sends = [], []
        for a in range(n):
            own.append(pltpu.make_async_copy(src[a], win(a, me), loc.at[a]))
            own[-1].start()
        for j, (dev, _, _) in enumerate(chips):
            for a in range(n):
                sends.append(rcopy(a, 1 + j, src[a], win(a, me), dev))
                sends[-1].start()
        for a in range(n):
            sends.append(rcopy(a, 0, src[a], win(a, me), sib))
            sends[-1].start()
        for j, (dev, idx, _) in enumerate(chips):
            for a in range(n):
                rcopy(a, 1 + j, src[a], win(a, idx), dev).wait_recv()
                sends.append(rcopy(a, 4 + j, win(a, idx), win(a, idx), sib))
                sends[-1].start()
        for cp in own:
            cp.wait()
        for a in range(n):
            rcopy(a, 0, src[a], win(a, 4 * x + 2 * y + (1 - c)), sib).wait_recv()
        for j, (_, _, sidx) in enumerate(chips):
            for a in range(n):
                rcopy(a, 4 + j, src[a], win(a, sidx), sib).wait_recv()
        for cp in sends:
            cp.wait_send()

    launch()
    return [d[...] for d in dst]


_SEQ_IDS = {"gather_in": 7, "gather_mix": 1, "gather_ffn": 2, "grads_down": 3, "grads_up": 8, "grads_mix": 4,
            "grads_small": 5, "grads_in": 6}


def _launch(name, mode, arrs, kinds):
    if mode == "gather":
        return _sc_gather(name, list(arrs), kinds, _SEQ_IDS[name])
    return _sc_xfer(name, mode, list(arrs), kinds, _SEQ_IDS[name])


def _adamw(name, parts, w, m, v):
    P, R, C = parts.shape
    sub = 16 if parts.dtype == BF16 else 8
    tr = R if R * C <= (1 << 18) else _tile(R, max(sub, (1 << 18) // C), sub)

    def body(p_ref, w_ref, m_ref, v_ref, g_ref, d_ref, nm_ref, nv_ref):
        g = p_ref[0].astype(F32)
        for s in range(1, P):
            g = g + p_ref[s].astype(F32)
        m2 = ADAM_B1 * m_ref[...] + (1.0 - ADAM_B1) * g
        v2 = ADAM_B2 * v_ref[...] + (1.0 - ADAM_B2) * (g * g)
        m_hat = m2 / (1.0 - ADAM_B1 ** ADAM_STEP)
        v_hat = v2 / (1.0 - ADAM_B2 ** ADAM_STEP)
        g_ref[...] = g
        d_ref[...] = -ADAM_LR * (m_hat / (jnp.sqrt(v_hat) + ADAM_EPS) + ADAM_WD * w_ref[...])
        nm_ref[...] = m2
        nv_ref[...] = v2

    sp = pl.BlockSpec((tr, C), lambda i: (i, 0))
    return pl.pallas_call(
        body, name=name, grid=(R // tr,),
        in_specs=[pl.BlockSpec((P, tr, C), lambda i: (0, i, 0)), sp, sp, sp], out_specs=[sp] * 4,
        out_shape=[_sds((R, C), F32)] * 4, compiler_params=_params(1),
    )(parts, w, m, v)


def _pack(arrs, row_mult=8):
    pieces, total = [], 0
    for a in arrs:
        f = a.reshape(-1).astype(F32)
        pad = (-f.shape[0]) % (8 * LANES)
        pieces.append(jnp.pad(f, (0, pad)) if pad else f)
        total += f.shape[0] + pad
    tail = (-total) % (row_mult * LANES)
    if tail:
        pieces.append(jnp.zeros((tail,), F32))
    return jnp.concatenate(pieces).reshape(-1, LANES)


def _unpack(buf, shapes, lead=()):
    out, row = [], 0
    for shp in shapes:
        size = 1
        for d in shp:
            size *= d
        rows = -(-size // (8 * LANES)) * 8
        piece = buf[..., row:row + rows, :].reshape(lead + (rows * LANES,))[..., :size]
        out.append(piece.reshape(lead + tuple(shp)))
        row += rows
    return out


def kernel(x, norm_tok, w_in, a_re, a_im, log_dt, b_re, b_im, c_re, c_im, d_skip, w_glu, w_ssm_out, conv_w, conv_b, w_conv_out, w_o, norm_ffn, w_up, ffn_conv_w, ffn_conv_b, w_down, norm_final, loss_target, m_norm_tok, m_w_in, m_a_re, m_a_im, m_log_dt, m_b_re, m_b_im, m_c_re, m_c_im, m_d_skip, m_w_glu, m_w_ssm_out, m_conv_w, m_conv_b, m_w_conv_out, m_w_o, m_norm_ffn, m_w_up, m_ffn_conv_w, m_ffn_conv_b, m_w_down, m_norm_final, v_norm_tok, v_w_in, v_a_re, v_a_im, v_log_dt, v_b_re, v_b_im, v_c_re, v_c_im, v_d_skip, v_w_glu, v_w_ssm_out, v_conv_w, v_conv_b, v_w_conv_out, v_w_o, v_norm_ffn, v_w_up, v_ffn_conv_w, v_ffn_conv_b, v_w_down, v_norm_final):
    args = dict(locals())
    L, D = x.shape[1], x.shape[2]
    G, P, H = b_re.shape[1], b_re.shape[2], b_re.shape[3]
    SW = G * H
    CW = conv_b.shape[1]
    FF = ffn_conv_b.shape[1]
    GP = G * P
    nb = SW // LANES
    gpb = LANES // H
    me = 4 * lax.axis_index("x") + 2 * lax.axis_index("y") + lax.axis_index("c")
    tm = _tile(L, 256, 16)
    x2 = x[0]
    tgt = loss_target[0]

    big = [("w_in", "col"), ("w_glu", "row"), ("w_ssm_out", "col"), ("w_conv_out", "col"), ("w_o", "row"),
           ("w_up", "col"), ("w_down", "row")]
    shards = [_cast_bf16("cast_" + n, args[n][0]) for n, _ in big]
    small_in = _pack([conv_w[0], ffn_conv_w[0]])
    kind = dict(big)
    mixw, ffnw = ["w_glu", "w_ssm_out", "w_conv_out", "w_o"], ["w_up", "w_down"]
    shard = dict(zip([n for n, _ in big], shards))
    gathered = _launch("gather_in", "gather", [shard["w_in"], small_in], ["col", "row"])
    W = {"w_in": gathered[0]}
    W.update(zip(mixw, _launch("gather_mix", "gather", [shard[n] for n in mixw], [kind[n] for n in mixw])))
    W.update(zip(ffnw, _launch("gather_ffn", "gather", [shard[n] for n in ffnw], [kind[n] for n in ffnw])))
    cw_parts, fcw_parts = _unpack(gathered[-1].reshape(N_DEV, -1, LANES), [conv_w.shape[1:], ffn_conv_w.shape[1:]], (N_DEV,))
    conv_w_full = jnp.moveaxis(cw_parts, 0, 1).reshape(3, CW)
    ffn_conv_w_full = jnp.moveaxis(fcw_parts, 0, 1).reshape(3, FF)

    ar_row, ai_row = a_re.reshape(1, GP), a_im.reshape(1, GP)
    ldt_row = jnp.broadcast_to(log_dt.reshape(G, 1), (G, P)).reshape(1, GP)
    brt = jnp.transpose(b_re[0], (2, 0, 1)).reshape(H, GP)
    bit = jnp.transpose(b_im[0], (2, 0, 1)).reshape(H, GP)
    abar_re, abar_im, bbar_re, bbar_im = _prep_fwd("s5_prep", ar_row, ai_row, ldt_row, brt, bit)
    eye = jnp.eye(gpb, dtype=F32)

    def b_blocks(bt):
        return jnp.einsum("ab,hjbp->jahbp", eye, bt.reshape(H, nb, gpb, P)).reshape(nb, LANES, gpb * P)

    def c_blocks(c):
        return jnp.einsum("ab,jahp->jbpah", eye, c.reshape(nb, gpb, H, P)).reshape(nb, gpb * P, LANES)

    def diag_blocks(mat):
        return jnp.einsum("jahap->hjap", mat.reshape(nb, gpb, H, gpb, P))

    bm_re, bm_im = b_blocks(bbar_re), b_blocks(bbar_im)
    cm_re, cm_im = c_blocks(c_re[0]), -c_blocks(c_im[0])
    a3_re, a3_im = abar_re.reshape(-1, 8, LANES), abar_im.reshape(-1, 8, LANES)
    dskip_row = d_skip.reshape(1, SW)

    cbs = SW // LANES
    cb_v, cb_gb, cb_gc = cbs, cbs + CW // LANES, cbs + 2 * CW // LANES
    cb_ma = (SW + 3 * CW) // D
    xn = _rms_fwd("rms_tok", x2, norm_tok, tm)
    proj = _mm("proj", xn, W["w_in"], "nn", out_dtype=BF16)
    xs_re, xs_im, y, ya = _s5_fwd("s5_fwd", proj, bm_re.astype(BF16), bm_im.astype(BF16), cm_re.astype(BF16),
                                  cm_im.astype(BF16), dskip_row, a3_re, a3_im)
    g1 = _mm("glu_gate", ya, W["w_glu"], "nn", out_dtype=BF16)
    ya2 = _glu_fwd("glu", y, g1, tm)
    za = _mm("ssm_out", ya2, W["w_ssm_out"], "nn", out_dtype=BF16)
    q = _convb_fwd("convb", proj, cb_v, cb_gb, cb_gc, conv_w_full, conv_b)
    zb = _mm("conv_out", q, W["w_conv_out"], "nn", out_dtype=BF16)
    merged = _merge_fwd("merge", proj, cb_ma, cb_ma + 1, za, zb, tm)
    o1 = _mm("mix_out", merged, W["w_o"], "nn")
    h1, hn = _res_rms_fwd("rms_ffn", x2, o1, norm_ffn, tm)
    hh = _mm("ffn_up", hn, W["w_up"], "nn", out_dtype=BF16)
    f = _ffn_fwd("ffn_act", hh, ffn_conv_w_full, ffn_conv_b)
    o2 = _mm("ffn_down", f, W["w_down"], "nn", tk=2816)
    dh2, dh2b, g_norm_final, loss_part = _final("final", h1, o2, norm_final.reshape(1, D), tgt, tm)

    df = _mm("d_ffn_act", dh2b, W["w_down"], "nt", tn=1408, out_dtype=BF16)
    gw_down = _mm("gw_down", f, dh2b, "tn", out_dtype=BF16, tm=1408, tn=512, tk=L)
    df, gw_down = lax.optimization_barrier((df, gw_down))
    parts = {"w_down": _launch("grads_down", "exchange", [gw_down], [kind["w_down"]])[0]}
    dhh, g_ffn_conv_w, g_ffn_conv_b = _ffn_bwd("ffn_act_bwd", hh, ffn_conv_w_full, ffn_conv_b, df)
    nhalf = lambda t: FF // t
    gw_up = _mm("gw_up", hn, dhh, "tn", out_dtype=BF16, tn=_tile(FF, 1024), tk=L, dims=(D, 2 * FF, L),
                b_spec=lambda a, b, c: pl.BlockSpec((None, c, b), lambda i, j, k: (j // nhalf(b), k, j % nhalf(b))))
    dhh, gw_up = lax.optimization_barrier((dhh, gw_up))
    parts["w_up"] = _launch("grads_up", "exchange", [gw_up], [kind["w_up"]])[0]
    dhn = _mm("d_ffn_in", dhh, W["w_up"], "nt", out_dtype=BF16, tk=_tile(FF, 2816), dims=(L, D, 2 * FF),
              a_spec=lambda a, b, c: pl.BlockSpec((None, a, c), lambda i, j, k: (k // nhalf(c), i, k % nhalf(c))))
    dh1, dh1b, g_norm_ffn = _rms_bwd("rms_ffn_bwd", dhn, h1, norm_ffn, dh2, tm, True)

    dmerged = _mm("d_merged", dh1b, W["w_o"], "nt", out_dtype=BF16)
    gw_o = _mm("gw_o", merged, dh1b, "tn", out_dtype=BF16, tk=L)
    dmerged, gw_o = lax.optimization_barrier((dmerged, gw_o))
    dza, dzb, dma, dmb = _merge_bwd("merge_bwd", proj, cb_ma, cb_ma + 1, za, zb, dmerged, tm)
    dq = _mm("d_q", dzb, W["w_conv_out"], "nt", out_dtype=BF16)
    gw_conv_out = _mm("gw_conv_out", q, dzb, "tn", out_dtype=BF16, tk=L)
    dq, gw_conv_out = lax.optimization_barrier((dq, gw_conv_out))
    dv, dgb, dgc, g_conv_w, g_conv_b = _convb_bwd("convb_bwd", proj, cb_v, cb_gb, cb_gc, conv_w_full, conv_b, dq)
    dya2 = _mm("d_ya2", dza, W["w_ssm_out"], "nt", out_dtype=BF16)
    gw_ssm_out = _mm("gw_ssm_out", ya2, dza, "tn", out_dtype=BF16, tk=L)
    dya2, gw_ssm_out = lax.optimization_barrier((dya2, gw_ssm_out))
    dy_direct, dg1 = _glu_bwd("glu_bwd", y, g1, dya2, tm)
    dya_g = _mm("d_ya_gate", dg1, W["w_glu"], "nt", out_dtype=BF16)
    gw_glu = _mm("gw_glu", ya, dg1, "tn", out_dtype=BF16, tk=L)
    dya_g, gw_glu = lax.optimization_barrier((dya_g, gw_glu))
    parts_mix = _launch("grads_mix", "exchange", [gw_glu, gw_ssm_out, gw_conv_out, gw_o], [kind[n] for n in mixw])
    dyb, g_dskip = _gelu_bwd("gelu_bwd", y, dy_direct, dya_g, proj, tm)
    swap = lambda m: jnp.swapaxes(m, 1, 2).astype(BF16)
    du, gb_re, gb_im, gc_re, gc_im, dab_re, dab_im = _s5_bwd(
        "s5_bwd", dyb, proj, xs_re, xs_im, swap(cm_re), swap(cm_im), swap(bm_re), swap(bm_im), dskip_row, a3_re, a3_im)
    g_ar, g_ai, g_ldt, g_brt, g_bit = _prep_bwd(
        "s5_prep_bwd", ar_row, ai_row, ldt_row, brt, bit, dab_re.reshape(1, GP), dab_im.reshape(1, GP),
        diag_blocks(gb_re).reshape(H, GP), diag_blocks(gb_im).reshape(H, GP))
    small = dict(
        a_re=g_ar.reshape(1, G, P), a_im=g_ai.reshape(1, G, P),
        log_dt=g_ldt.reshape(G, P).sum(axis=1).reshape(1, G),
        b_re=jnp.transpose(g_brt.reshape(H, G, P), (1, 2, 0))[None], b_im=jnp.transpose(g_bit.reshape(H, G, P), (1, 2, 0))[None],
        c_re=jnp.transpose(diag_blocks(gc_re), (1, 2, 0, 3)).reshape(1, G, H, P),
        c_im=-jnp.transpose(diag_blocks(gc_im), (1, 2, 0, 3)).reshape(1, G, H, P),
        d_skip=g_dskip.reshape(1, G, H), conv_b=g_conv_b, norm_ffn=g_norm_ffn, ffn_conv_b=g_ffn_conv_b,
        norm_final=g_norm_final.reshape(D), conv_w=g_conv_w[None], ffn_conv_w=g_ffn_conv_w[None])
    rep = ["a_re", "a_im", "log_dt", "b_re", "b_im", "c_re", "c_im", "d_skip", "conv_b", "norm_ffn", "ffn_conv_b", "norm_final"]
    order = rep + ["conv_w", "ffn_conv_w"]
    full_shapes = {n: args[n].shape for n in rep}
    full_shapes["conv_w"], full_shapes["ffn_conv_w"] = (1, 3, CW), (1, 3, FF)
    rep_pack = _pack([small[n] for n in rep], LANES)
    rep_rows = rep_pack.shape[0]
    gpack = jnp.concatenate([loss_part, rep_pack, _pack([small["conv_w"], small["ffn_conv_w"]])], axis=0)
    rep0 = loss_part.shape[0]
    rows = gpack.shape[0]
    du, gpack = lax.optimization_barrier((du, gpack))
    gall = _launch("grads_small", "gather", [gpack], ["row"])[0]

    dproj = _concat_cols("dproj", [du, dv, dgb, dgc, dma, dmb], tm)
    gw_in = _mm("gw_in", xn, dproj, "tn", out_dtype=BF16, tk=L)
    dproj, gw_in = lax.optimization_barrier((dproj, gw_in))
    parts_in = _launch("grads_in", "exchange", [gw_in], ["col"])
    dxn = _mm("d_xn", dproj, W["w_in"], "nt", out_dtype=BF16, tk=4096)
    grad_x, g_norm_tok = _rms_bwd("rms_tok_bwd", dxn, x2, norm_tok, dh1, tm, False)

    res = {}

    def big_update(n):
        res[n] = [r[None] for r in _adamw("adamw_" + n, parts[n], args[n][0], args["m_" + n][0], args["v_" + n][0])]

    def after(xs, dep):
        return lax.optimization_barrier((list(xs), dep))[0]

    parts["w_down"] = after([parts["w_down"]], grad_x)[0]
    big_update("w_down")
    parts["w_up"] = after([parts["w_up"]], res["w_down"][1])[0]
    big_update("w_up")
    parts.update(zip(mixw, after(parts_mix, [res[n][1] for n in ffnw])))
    for n in mixw:
        big_update(n)
    gall = after([gall], [res[n][1] for n in mixw])[0].reshape(N_DEV, rows, LANES)
    gcw, gfcw = _unpack(gall[:, rep0 + rep_rows:], [full_shapes["conv_w"], full_shapes["ffn_conv_w"]], (N_DEV,))
    cws, fcws = CW // N_DEV, FF // N_DEV
    gcw = lax.dynamic_slice_in_dim(gcw[:, 0], me * cws, cws, axis=2)
    gfcw = lax.dynamic_slice_in_dim(gfcw[:, 0], me * fcws, fcws, axis=2)
    res["conv_w"] = [r[None] for r in _adamw("adamw_conv_w", gcw, conv_w[0], m_conv_w[0], v_conv_w[0])]
    res["ffn_conv_w"] = [r[None] for r in _adamw("adamw_ffn_conv_w", gfcw, ffn_conv_w[0], m_ffn_conv_w[0], v_ffn_conv_w[0])]
    rep_out = _adamw("adamw_small", gall[:, rep0:rep0 + rep_rows], _pack([args[n] for n in rep], LANES),
                     _pack([args["m_" + n] for n in rep], LANES), _pack([args["v_" + n] for n in rep], LANES))
    rep_out = [_unpack(r, [full_shapes[n] for n in rep]) for r in rep_out]
    for i, n in enumerate(rep):
        res[n] = [r[i] for r in rep_out]
    nt_pack = after([_pack([g_norm_tok])], [res[n][1] for n in ("a_re", "conv_w", "ffn_conv_w")])
    nt_all = _all_gather("gather_norm_tok_grad", nt_pack, ["row"])[0].reshape(N_DEV, -1, LANES)
    nt_out = _adamw("adamw_norm_tok", nt_all, _pack([norm_tok]), _pack([m_norm_tok]), _pack([v_norm_tok]))
    res["norm_tok"] = [_unpack(r, [norm_tok.shape])[0] for r in nt_out]
    parts["w_in"] = after(parts_in, nt_out[0])[0]
    big_update("w_in")

    loss = jnp.sum(gall[:, 0, 0])
    names = ["norm_tok", "w_in", "a_re", "a_im", "log_dt", "b_re", "b_im", "c_re", "c_im", "d_skip", "w_glu", "w_ssm_out",
             "conv_w", "conv_b", "w_conv_out", "w_o", "norm_ffn", "w_up", "ffn_conv_w", "ffn_conv_b", "w_down", "norm_final"]
    out = [loss, grad_x[None]]
    for slot in range(4):
        out += [res[n][slot] for n in names]
    return tuple(out)
```

```python
import jax
import jax.numpy as jnp
from jax import lax
from jax.experimental import pallas as pl
from jax.experimental.pallas import tpu as pltpu
from jax.experimental.pallas import tpu_sc as plsc

F32 = jnp.float32
BF16 = jnp.bfloat16
N_DEV = 8
LANES = 128
SLAB = 4
EPS = 1e-6
ADAM_LR = 0.001
ADAM_B1 = 0.9
ADAM_B2 = 0.999
ADAM_EPS = 1e-08
ADAM_WD = 0.01
ADAM_STEP = 10
VMEM_LIMIT = 56 * 1024 * 1024
MESH = pl.DeviceIdType.MESH


def _tile(n, pref, mult=LANES):
    best = None
    t = mult
    while t <= min(n, pref):
        if n % t == 0:
            best = t
        t += mult
    return best if best is not None else n


def _params(ndim):
    return pltpu.CompilerParams(dimension_semantics=("arbitrary",) * ndim, vmem_limit_bytes=VMEM_LIMIT)


def _sds(shape, dtype):
    return jax.ShapeDtypeStruct(tuple(shape), dtype)


def _mm(name, a, b, mode, *, out_dtype=F32, tm=1024, tn=1024, tk=2048, dims=None, a_spec=None, b_spec=None):
    if dims is None:
        if mode == "nn":
            (M, K), N = a.shape, b.shape[1]
        elif mode == "nt":
            (M, K), N = a.shape, b.shape[0]
        else:
            (K, M), N = a.shape, b.shape[1]
    else:
        M, N, K = dims
    tm, tn, tk = _tile(M, tm), _tile(N, tn), _tile(K, tk)
    nk = K // tk
    if mode == "nn":
        dn = (((1,), (0,)), ((), ()))
        sa = pl.BlockSpec((tm, tk), lambda i, j, k: (i, k))
        sb = pl.BlockSpec((tk, tn), lambda i, j, k: (k, j))
    elif mode == "nt":
        dn = (((1,), (1,)), ((), ()))
        sa = pl.BlockSpec((tm, tk), lambda i, j, k: (i, k))
        sb = pl.BlockSpec((tn, tk), lambda i, j, k: (j, k))
    else:
        dn = (((0,), (0,)), ((), ()))
        sa = pl.BlockSpec((tk, tm), lambda i, j, k: (k, i))
        sb = pl.BlockSpec((tk, tn), lambda i, j, k: (k, j))
    sa = a_spec(tm, tn, tk) if a_spec is not None else sa
    sb = b_spec(tm, tn, tk) if b_spec is not None else sb
    use_acc = nk > 1 and out_dtype != F32

    def body(a_ref, b_ref, o_ref, *acc):
        k = pl.program_id(2)
        p = lax.dot_general(a_ref[...], b_ref[...], dn, preferred_element_type=F32)
        if nk == 1:
            o_ref[...] = p.astype(out_dtype)
        else:
            tgt = acc[0] if use_acc else o_ref

            @pl.when(k == 0)
            def _():
                tgt[...] = p

            @pl.when(k > 0)
            def _():
                tgt[...] += p

            if use_acc:
                @pl.when(k == nk - 1)
                def _():
                    o_ref[...] = acc[0][...].astype(out_dtype)

    return pl.pallas_call(
        body, name=name, grid=(M // tm, N // tn, nk),
        in_specs=[sa, sb], out_specs=pl.BlockSpec((tm, tn), lambda i, j, k: (i, j)),
        out_shape=_sds((M, N), out_dtype),
        scratch_shapes=[pltpu.VMEM((tm, tn), F32)] if use_acc else [],
        compiler_params=_params(3),
    )(a, b)


def _rows(name, body, L, tm, ins, outs):
    return pl.pallas_call(
        body, name=name, grid=(L // tm,),
        in_specs=[s for _, s in ins], out_specs=[s for _, s in outs],
        out_shape=[o for o, _ in outs], compiler_params=_params(1),
    )(*[a for a, _ in ins])


def _rs(tm, w, cb=0):
    return pl.BlockSpec((tm, w), lambda i: (i, cb))


def _fs(shape):
    return pl.BlockSpec(tuple(shape), lambda i: (0,) * len(shape))


def _acc_rows(i, ref, part):
    @pl.when(i == 0)
    def _():
        ref[...] = part

    @pl.when(i > 0)
    def _():
        ref[...] += part


def _cast_bf16(name, w):
    R, C = w.shape
    tr = _tile(R, max(16, (1 << 20) // C), 16)

    def body(w_ref, o_ref):
        o_ref[...] = w_ref[...].astype(BF16)

    return _rows(name, body, R, tr, [(w, _rs(tr, C))], [(_sds((R, C), BF16), _rs(tr, C))])[0]


def _concat_cols(name, pieces, tm):
    L = pieces[0].shape[0]
    widths = [p.shape[1] for p in pieces]

    def body(*refs):
        o_ref, off = refs[-1], 0
        for p_ref, w in zip(refs[:-1], widths):
            o_ref[:, off:off + w] = p_ref[...]
            off += w

    return _rows(name, body, L, tm, [(p, _rs(tm, w)) for p, w in zip(pieces, widths)],
                 [(_sds((L, sum(widths)), pieces[0].dtype), _rs(tm, sum(widths)))])[0]


def _rms_fwd(name, x, g, tm):
    L, D = x.shape

    def body(x_ref, g_ref, o_ref):
        xv = x_ref[...]
        r = lax.rsqrt(jnp.mean(xv * xv, axis=-1, keepdims=True) + EPS)
        o_ref[...] = (xv * r * g_ref[...]).astype(BF16)

    return _rows(name, body, L, tm, [(x, _rs(tm, D)), (g, _fs((1, D)))], [(_sds((L, D), BF16), _rs(tm, D))])[0]


def _res_rms_fwd(name, x, o, g, tm):
    L, D = x.shape

    def body(x_ref, o_ref, g_ref, h_ref, hn_ref):
        h = x_ref[...] + o_ref[...]
        r = lax.rsqrt(jnp.mean(h * h, axis=-1, keepdims=True) + EPS)
        h_ref[...] = h
        hn_ref[...] = (h * r * g_ref[...]).astype(BF16)

    return _rows(name, body, L, tm, [(x, _rs(tm, D)), (o, _rs(tm, D)), (g, _fs((1, D)))],
                 [(_sds((L, D), F32), _rs(tm, D)), (_sds((L, D), BF16), _rs(tm, D))])


def _rms_bwd(name, dn, h, g, dres, tm, with_bf16):
    L, D = h.shape

    def body(dn_ref, h_ref, g_ref, dres_ref, dh_ref, *rest):
        i = pl.program_id(0)
        h = h_ref[...]
        r = lax.rsqrt(jnp.mean(h * h, axis=-1, keepdims=True) + EPS)
        xh = h * r
        d = dn_ref[...].astype(F32)
        dxh = d * g_ref[...]
        dh = dres_ref[...] + r * (dxh - xh * jnp.mean(dxh * xh, axis=-1, keepdims=True))
        dh_ref[...] = dh
        if with_bf16:
            rest[0][...] = dh.astype(BF16)
        _acc_rows(i, rest[-1], jnp.sum(d * xh, axis=0, keepdims=True))

    outs = [(_sds((L, D), F32), _rs(tm, D))]
    if with_bf16:
        outs.append((_sds((L, D), BF16), _rs(tm, D)))
    outs.append((_sds((1, D), F32), _fs((1, D))))
    return _rows(name, body, L, tm, [(dn, _rs(tm, D)), (h, _rs(tm, D)), (g, _fs((1, D))), (dres, _rs(tm, D))], outs)


def _final(name, h1, o2, g, tgt, tm):
    L, D = h1.shape

    def body(h1_ref, o2_ref, g_ref, t_ref, dh_ref, dhb_ref, dg_ref, loss_ref):
        i = pl.program_id(0)
        h = h1_ref[...] + o2_ref[...]
        r = lax.rsqrt(jnp.mean(h * h, axis=-1, keepdims=True) + EPS)
        xh = h * r
        gv = g_ref[...]
        e = xh * gv - t_ref[...]
        part = 0.5 * jnp.sum(jnp.mean(e * e, axis=-1, keepdims=True), axis=0, keepdims=True)
        dy = e / D
        dxh = dy * gv
        dh = r * (dxh - xh * jnp.mean(dxh * xh, axis=-1, keepdims=True))
        dh_ref[...] = dh
        dhb_ref[...] = dh.astype(BF16)
        _acc_rows(i, dg_ref, jnp.sum(dy * xh, axis=0, keepdims=True))
        _acc_rows(i, loss_ref, jnp.broadcast_to(part, (8, LANES)))

    return _rows(name, body, L, tm,
                 [(h1, _rs(tm, D)), (o2, _rs(tm, D)), (g, _fs((1, D))), (tgt, _rs(tm, D))],
                 [(_sds((L, D), F32), _rs(tm, D)), (_sds((L, D), BF16), _rs(tm, D)),
                  (_sds((1, D), F32), _fs((1, D))), (_sds((8, LANES), F32), _fs((8, LANES)))])


def _gelu_and_grad(x):
    c, k = 0.7978845608028654, 0.044715
    x2 = x * x
    t = jnp.tanh(c * x * (1.0 + k * x2))
    half = 0.5 * x
    return half * (1.0 + t), 0.5 * (1.0 + t) + half * (1.0 - t * t) * (c * (1.0 + 3.0 * k * x2))


def _glu_fn(y, g1):
    ya = jax.nn.gelu(y)
    return ya * jax.nn.sigmoid(g1)


def _glu_fwd(name, y, g1, tm):
    L, W = y.shape

    def body(y_ref, g_ref, o_ref):
        o_ref[...] = _glu_fn(y_ref[...], g_ref[...].astype(F32)).astype(BF16)

    return _rows(name, body, L, tm, [(y, _rs(tm, W)), (g1, _rs(tm, W))], [(_sds((L, W), BF16), _rs(tm, W))])[0]


def _glu_bwd(name, y, g1, dya2, tm):
    L, W = y.shape

    def body(y_ref, g_ref, d_ref, dy_ref, dg_ref):
        _, vjp = jax.vjp(_glu_fn, y_ref[...], g_ref[...].astype(F32))
        dy, dg = vjp(d_ref[...].astype(F32))
        dy_ref[...] = dy
        dg_ref[...] = dg.astype(BF16)

    return _rows(name, body, L, tm, [(y, _rs(tm, W)), (g1, _rs(tm, W)), (dya2, _rs(tm, W))],
                 [(_sds((L, W), F32), _rs(tm, W)), (_sds((L, W), BF16), _rs(tm, W))])


def _gelu_bwd(name, y, dy_direct, dya_g, proj, tm):
    L, W = y.shape

    def body(y_ref, dd_ref, dg_ref, u_ref, dyb_ref, dsk_ref):
        i = pl.program_id(0)
        dy = dd_ref[...] + dg_ref[...].astype(F32) * _gelu_and_grad(y_ref[...])[1]
        dyb_ref[...] = dy.astype(BF16)
        _acc_rows(i, dsk_ref, jnp.sum(dy * u_ref[...].astype(F32), axis=0, keepdims=True))

    return _rows(name, body, L, tm,
                 [(y, _rs(tm, W)), (dy_direct, _rs(tm, W)), (dya_g, _rs(tm, W)), (proj, _rs(tm, W, 0))],
                 [(_sds((L, W), BF16), _rs(tm, W)), (_sds((1, W), F32), _fs((1, W)))])


def _merge_fn(ma, mb, za, zb):
    return jax.nn.sigmoid(ma) * za + jax.nn.sigmoid(mb) * zb


def _merge_fwd(name, proj, cb_a, cb_b, za, zb, tm):
    L, D = za.shape

    def body(ma_ref, mb_ref, za_ref, zb_ref, o_ref):
        o_ref[...] = _merge_fn(ma_ref[...].astype(F32), mb_ref[...].astype(F32), za_ref[...].astype(F32),
                               zb_ref[...].astype(F32)).astype(BF16)

    return _rows(name, body, L, tm,
                 [(proj, _rs(tm, D, cb_a)), (proj, _rs(tm, D, cb_b)), (za, _rs(tm, D)), (zb, _rs(tm, D))],
                 [(_sds((L, D), BF16), _rs(tm, D))])[0]


def _merge_bwd(name, proj, cb_a, cb_b, za, zb, dmerged, tm):
    L, D = za.shape

    def body(ma_ref, mb_ref, za_ref, zb_ref, d_ref, dza_ref, dzb_ref, dma_ref, dmb_ref):
        _, vjp = jax.vjp(_merge_fn, ma_ref[...].astype(F32), mb_ref[...].astype(F32), za_ref[...].astype(F32),
                         zb_ref[...].astype(F32))
        dma, dmb, dza, dzb = vjp(d_ref[...].astype(F32))
        dza_ref[...] = dza.astype(BF16)
        dzb_ref[...] = dzb.astype(BF16)
        dma_ref[...] = dma.astype(BF16)
        dmb_ref[...] = dmb.astype(BF16)

    return _rows(name, body, L, tm,
                 [(proj, _rs(tm, D, cb_a)), (proj, _rs(tm, D, cb_b)), (za, _rs(tm, D)), (zb, _rs(tm, D)),
                  (dmerged, _rs(tm, D))],
                 [(_sds((L, D), BF16), _rs(tm, D)), (_sds((L, D), BF16), _rs(tm, D)),
                  (_sds((L, D), BF16), _rs(tm, D)), (_sds((L, D), BF16), _rs(tm, D))])


def _shift_down(x, k):
    row = lax.broadcasted_iota(jnp.int32, x.shape, 0)
    return jnp.where(row >= k, pltpu.roll(x, k, axis=0), 0.0)


def _shift_up(x, k):
    n = x.shape[0]
    row = lax.broadcasted_iota(jnp.int32, x.shape, 0)
    return jnp.where(row < n - k, pltpu.roll(x, n - k, axis=0), 0.0)


def _conv3(cv, w_ref, b_ref):
    return (w_ref[2:3, :] * cv + w_ref[1:2, :] * _shift_down(cv, 1) + w_ref[0:1, :] * _shift_down(cv, 2)
            + b_ref[...])


def _conv3_bwd(dcc, cv, w_ref):
    dcv = w_ref[2:3, :] * dcc + w_ref[1:2, :] * _shift_up(dcc, 1) + w_ref[0:1, :] * _shift_up(dcc, 2)
    dw = [jnp.sum(dcc * _shift_down(cv, 2), axis=0, keepdims=True),
          jnp.sum(dcc * _shift_down(cv, 1), axis=0, keepdims=True),
          jnp.sum(dcc * cv, axis=0, keepdims=True)]
    db = jnp.sum(dcc, axis=0, keepdims=True)
    return dcv, dw, db


def _store_rows(ref, rows):
    for r, val in enumerate(rows):
        ref[r:r + 1, :] = val


def _cols(name, body, ncb, ins, outs):
    return pl.pallas_call(
        body, name=name, grid=(ncb,),
        in_specs=[s for _, s in ins], out_specs=[s for _, s in outs],
        out_shape=[o for o, _ in outs], compiler_params=_params(1),
    )(*[a for a, _ in ins])


def _cb(L, w, off=0):
    return pl.BlockSpec((L, w), lambda j: (0, j + off))


def _convb_fwd(name, proj, cb_v, cb_gb, cb_gc, w, b):
    L = proj.shape[0]
    W = w.shape[1]
    c = LANES

    def body(v_ref, gb_ref, gc_ref, w_ref, b_ref, q_ref):
        cc = _conv3(gc_ref[...].astype(F32) * v_ref[...].astype(F32), w_ref, b_ref)
        q_ref[...] = (gb_ref[...].astype(F32) * cc).astype(BF16)

    return _cols(name, body, W // c,
                 [(proj, _cb(L, c, cb_v)), (proj, _cb(L, c, cb_gb)), (proj, _cb(L, c, cb_gc)),
                  (w, _cb(3, c)), (b, _cb(1, c))],
                 [(_sds((L, W), BF16), _cb(L, c))])[0]


def _convb_bwd(name, proj, cb_v, cb_gb, cb_gc, w, b, dq):
    L = proj.shape[0]
    W = w.shape[1]
    c = LANES

    def body(v_ref, gb_ref, gc_ref, w_ref, b_ref, dq_ref, dv_ref, dgb_ref, dgc_ref, dw_ref, db_ref):
        v, gc = v_ref[...].astype(F32), gc_ref[...].astype(F32)
        cv = gc * v
        cc = _conv3(cv, w_ref, b_ref)
        dq = dq_ref[...].astype(F32)
        dgb_ref[...] = (dq * cc).astype(BF16)
        dcv, dw, db = _conv3_bwd(dq * gb_ref[...].astype(F32), cv, w_ref)
        dv_ref[...] = (dcv * gc).astype(BF16)
        dgc_ref[...] = (dcv * v).astype(BF16)
        _store_rows(dw_ref, dw)
        db_ref[...] = db

    return _cols(name, body, W // c,
                 [(proj, _cb(L, c, cb_v)), (proj, _cb(L, c, cb_gb)), (proj, _cb(L, c, cb_gc)),
                  (w, _cb(3, c)), (b, _cb(1, c)), (dq, _cb(L, c))],
                 [(_sds((L, W), BF16), _cb(L, c)), (_sds((L, W), BF16), _cb(L, c)), (_sds((L, W), BF16), _cb(L, c)),
                  (_sds((3, W), F32), _cb(3, c)), (_sds((1, W), F32), _cb(1, c))])


HALO = 16


def _ffn_up_act(name, hn, w_up, w, b):
    L, D = hn.shape
    Fw = w.shape[1]
    tm, tc = _tile(L, 1024, HALO), _tile(Fw, 512)
    ncb = Fw // tc

    def body(x_ref, wa_ref, wg_ref, w_ref, b_ref, hh_ref, f_ref, carry):
        i, j = pl.program_id(0), pl.program_id(1)
        a16 = jnp.dot(x_ref[...], wa_ref[...], preferred_element_type=F32).astype(BF16)
        g16 = jnp.dot(x_ref[...], wg_ref[...], preferred_element_type=F32).astype(BF16)
        hh_ref[0] = a16
        hh_ref[1] = g16
        for c0 in range(0, tc, LANES):
            cs = slice(c0, c0 + LANES)
            prev = jnp.where(i == 0, 0.0, carry[j, :, cs])
            x = jnp.concatenate([prev, a16[:, cs].astype(F32)], axis=0)
            n = x.shape[0]
            a = (w_ref[2:3, cs] * x + w_ref[1:2, cs] * pltpu.roll(x, 1, axis=0) + w_ref[0:1, cs] * pltpu.roll(x, 2, axis=0)
                 + b_ref[:, cs])[8:n]
            f_ref[:, cs] = (_gelu_and_grad(a)[0] * g16[:, cs].astype(F32)).astype(BF16)
            carry[j, :, cs] = x[n - 8:n]

    return pl.pallas_call(
        body, name=name, grid=(L // tm, ncb),
        in_specs=[pl.BlockSpec((tm, D), lambda i, j: (i, 0)), pl.BlockSpec((D, tc), lambda i, j: (0, j)),
                  pl.BlockSpec((D, tc), lambda i, j: (0, j + ncb)),
                  pl.BlockSpec((3, tc), lambda i, j: (0, j)), pl.BlockSpec((1, tc), lambda i, j: (0, j))],
        out_specs=[pl.BlockSpec((2, tm, tc), lambda i, j: (0, i, j)), pl.BlockSpec((tm, tc), lambda i, j: (i, j))],
        out_shape=[_sds((2, L, Fw), BF16), _sds((L, Fw), BF16)],
        scratch_shapes=[pltpu.VMEM((ncb, 8, tc), F32)], compiler_params=_params(2),
    )(hn, w_up, w_up, w, b)


def _ffn_down_bwd_act(name, dy, w_down, hh, w, b):
    L, D = dy.shape
    Fw = w.shape[1]
    tm, tc = _tile(L, 512, HALO), _tile(Fw, 512)
    ncb, nrt, rpt = Fw // tc, L // tm, tm // HALO
    dn = (((1,), (1,)), ((), ()))

    def body(dy_ref, wd_ref, a_ref, ap_ref, h2_ref, w_ref, b_ref, dhh_ref, dw_ref, db_ref, carry, acc):
        i, j = pl.program_id(0), pl.program_id(1)
        first_rows = i == nrt - 1
        d16 = lax.dot_general(dy_ref[...], wd_ref[...], dn, preferred_element_type=F32).astype(BF16)
        for c0 in range(0, tc, LANES):
            cs = slice(c0, c0 + LANES)
            h1 = jnp.concatenate([jnp.where(first_rows, 0.0, ap_ref[:, cs].astype(F32)), a_ref[:, cs].astype(F32)], axis=0)
            s1, s2 = pltpu.roll(h1, 1, axis=0), pltpu.roll(h1, 2, axis=0)
            n = h1.shape[0]
            a = (w_ref[2:3, cs] * h1 + w_ref[1:2, cs] * s1 + w_ref[0:1, cs] * s2 + b_ref[:, cs])[HALO:n]
            ga, dga = _gelu_and_grad(a)
            d = d16[:, cs].astype(F32)
            da = d * h2_ref[:, cs].astype(F32) * dga
            dae = jnp.concatenate([da, jnp.where(i == 0, 0.0, carry[j, :, cs])], axis=0)
            m = dae.shape[0]
            dh1 = w_ref[2:3, cs] * dae + w_ref[1:2, cs] * pltpu.roll(dae, m - 1, axis=0) + w_ref[0:1, cs] * pltpu.roll(dae, m - 2, axis=0)
            dhh_ref[0, :, cs] = dh1[0:tm].astype(BF16)
            dhh_ref[1, :, cs] = (d * ga).astype(BF16)
            carry[j, :, cs] = da[0:8]
            rows = [jnp.sum(da * s2[HALO:n], axis=0, keepdims=True), jnp.sum(da * s1[HALO:n], axis=0, keepdims=True),
                    jnp.sum(da * h1[HALO:n], axis=0, keepdims=True), jnp.sum(da, axis=0, keepdims=True)]
            for r in range(4):
                tot = jnp.where(i == 0, 0.0, acc[j, r:r + 1, cs]) + rows[r]
                acc[j, r:r + 1, cs] = tot
                if r < 3:
                    dw_ref[r:r + 1, cs] = tot
                else:
                    db_ref[:, cs] = tot

    rt = lambda i: nrt - 1 - i
    dhh, dw, db = pl.pallas_call(
        body, name=name, grid=(nrt, ncb),
        in_specs=[pl.BlockSpec((tm, D), lambda i, j: (rt(i), 0)), pl.BlockSpec((tc, D), lambda i, j: (j, 0)),
                  pl.BlockSpec((None, tm, tc), lambda i, j: (0, rt(i), j)),
                  pl.BlockSpec((None, HALO, tc), lambda i, j: (0, jnp.maximum(rt(i) * rpt - 1, 0), j)),
                  pl.BlockSpec((None, tm, tc), lambda i, j: (1, rt(i), j)),
                  pl.BlockSpec((3, tc), lambda i, j: (0, j)), pl.BlockSpec((1, tc), lambda i, j: (0, j))],
        out_specs=[pl.BlockSpec((2, tm, tc), lambda i, j: (0, rt(i), j)),
                   pl.BlockSpec((None, 3, tc), lambda i, j: (i, 0, j)), pl.BlockSpec((None, 1, tc), lambda i, j: (i, 0, j))],
        out_shape=[_sds((2, L, Fw), BF16), _sds((nrt, 3, Fw), F32), _sds((nrt, 1, Fw), F32)],
        scratch_shapes=[pltpu.VMEM((ncb, 8, tc), F32), pltpu.VMEM((ncb, 8, tc), F32)], compiler_params=_params(2),
    )(dy, w_down, hh, hh, hh, w, b)
    return dhh, dw[nrt - 1], db[nrt - 1]


def _prep_fn(ar, ai, ldt, brt, bit):
    dt = jnp.exp(ldt)
    mag = jnp.exp(dt * ar)
    are = mag * jnp.cos(dt * ai)
    aim = mag * jnp.sin(dt * ai)
    nr = are - 1.0
    ni = aim
    den = ar * ar + ai * ai
    fr = (nr * ar + ni * ai) / den
    fi = (ni * ar - nr * ai) / den
    return are, aim, fr * brt - fi * bit, fr * bit + fi * brt


def _prep_fwd(name, ar, ai, ldt, brt, bit):
    def body(ar_ref, ai_ref, l_ref, br_ref, bi_ref, o1, o2, o3, o4):
        o1[...], o2[...], o3[...], o4[...] = _prep_fn(ar_ref[...], ai_ref[...], l_ref[...], br_ref[...], bi_ref[...])

    return pl.pallas_call(body, name=name,
                          out_shape=[_sds(ar.shape, F32), _sds(ar.shape, F32), _sds(brt.shape, F32), _sds(brt.shape, F32)],
                          )(ar, ai, ldt, brt, bit)


def _prep_bwd(name, ar, ai, ldt, brt, bit, g1, g2, g3, g4):
    def body(ar_ref, ai_ref, l_ref, br_ref, bi_ref, g1_ref, g2_ref, g3_ref, g4_ref, o1, o2, o3, o4, o5):
        _, vjp = jax.vjp(_prep_fn, ar_ref[...], ai_ref[...], l_ref[...], br_ref[...], bi_ref[...])
        o1[...], o2[...], o3[...], o4[...], o5[...] = vjp((g1_ref[...], g2_ref[...], g3_ref[...], g4_ref[...]))

    return pl.pallas_call(body, name=name,
                          out_shape=[_sds(ar.shape, F32)] * 3 + [_sds(brt.shape, F32)] * 2,
                          )(ar, ai, ldt, brt, bit, g1, g2, g3, g4)


def _scan_steps(tc, pitch, ng, reverse, a_r, a_i, stage_b, stage_x, out_re, out_im, st_re, st_im, acc):
    def step(tt, carry):
        t = (tc - 1 - tt) if reverse else tt
        new, sums = [], []
        for g in range(ng):
            rows = pl.ds(g * 8 * pitch + t, 8, stride=pitch)
            cr, ci = carry[2 * g], carry[2 * g + 1]
            br, bi = stage_b[0][rows, :], stage_b[1][rows, :]
            if reverse:
                xr, xi = stage_x[0][rows, :], stage_x[1][rows, :]
                sums += [carry[2 * ng + 2 * g] + (xr * cr + xi * ci), carry[2 * ng + 2 * g + 1] + (xr * ci - xi * cr)]
                nr = a_r[g] * cr + a_i[g] * ci + br
                ni = a_r[g] * ci - a_i[g] * cr + bi
            else:
                nr = a_r[g] * cr - a_i[g] * ci + br
                ni = a_r[g] * ci + a_i[g] * cr + bi
            out_re[rows, :] = nr
            out_im[rows, :] = ni
            new += [nr, ni]
        return tuple(new + sums)

    init = []
    for g in range(ng):
        init += [st_re[g], st_im[g]]
    if reverse:
        for g in range(ng):
            init += [acc[0][g], acc[1][g]]
    fin = lax.fori_loop(0, tc, step, tuple(init), unroll=2)
    for g in range(ng):
        st_re[g] = fin[2 * g]
        st_im[g] = fin[2 * g + 1]
        if reverse:
            acc[0][g] = fin[2 * ng + 2 * g]
            acc[1][g] = fin[2 * ng + 2 * g + 1]


def _s5_fwd(name, proj, bm_re, bm_im, cm_re, cm_im, dskip, a_re, a_im):
    L = proj.shape[0]
    nb = bm_re.shape[0]
    ns, W = SLAB * nb, nb * LANES
    ng = ns // 8
    tc = min(2 * LANES, L)
    pitch = tc + 8
    wide = SLAB * LANES

    def body(u_ref, bre_ref, bim_ref, cre_ref, cim_ref, d_ref, ar_ref, ai_ref, xr_ref, xi_ref, y_ref, ya_ref,
             sb_re, sb_im, out_re, out_im, st_re, st_im):
        @pl.when(pl.program_id(0) == 0)
        def _():
            st_re[...] = jnp.zeros(st_re.shape, F32)
            st_im[...] = jnp.zeros(st_im.shape, F32)

        for j in range(nb):
            ub = u_ref[:, j * LANES:(j + 1) * LANES]
            r1 = jnp.dot(ub, bre_ref[j], preferred_element_type=F32)
            r2 = jnp.dot(ub, bim_ref[j], preferred_element_type=F32)
            for q in range(SLAB):
                sb_re[pl.ds((SLAB * j + q) * pitch, tc), :] = r1[:, q * LANES:(q + 1) * LANES]
                sb_im[pl.ds((SLAB * j + q) * pitch, tc), :] = r2[:, q * LANES:(q + 1) * LANES]
        a_r = [ar_ref[g] for g in range(ng)]
        a_i = [ai_ref[g] for g in range(ng)]
        _scan_steps(tc, pitch, ng, False, a_r, a_i, (sb_re, sb_im), None, out_re, out_im, st_re, st_im, None)
        for j in range(nb):
            x1 = [out_re[pl.ds((SLAB * j + q) * pitch, tc), :].astype(BF16) for q in range(SLAB)]
            x2 = [out_im[pl.ds((SLAB * j + q) * pitch, tc), :].astype(BF16) for q in range(SLAB)]
            for q in range(SLAB):
                xr_ref[SLAB * j + q] = x1[q]
                xi_ref[SLAB * j + q] = x2[q]
            cols = slice(j * LANES, (j + 1) * LANES)
            y = (jnp.dot(jnp.concatenate(x1, axis=1), cre_ref[j], preferred_element_type=F32)
                 + jnp.dot(jnp.concatenate(x2, axis=1), cim_ref[j], preferred_element_type=F32)
                 + d_ref[:, cols] * u_ref[:, cols].astype(F32))
            y_ref[:, cols] = y
            ya_ref[:, cols] = jax.nn.gelu(y).astype(BF16)

    full3 = lambda s: pl.BlockSpec(s, lambda i: (0, 0, 0))
    xs = pl.BlockSpec((ns, tc, LANES), lambda i: (0, i, 0))
    rows = pl.BlockSpec((tc, W), lambda i: (i, 0))
    return pl.pallas_call(
        body, name=name, grid=(L // tc,),
        in_specs=[rows, full3((nb, LANES, wide)), full3((nb, LANES, wide)), full3((nb, wide, LANES)),
                  full3((nb, wide, LANES)), pl.BlockSpec((1, W), lambda i: (0, 0)), full3((ng, 8, LANES)), full3((ng, 8, LANES))],
        out_specs=[xs, xs, rows, rows],
        out_shape=[_sds((ns, L, LANES), BF16)] * 2 + [_sds((L, W), F32), _sds((L, W), BF16)],
        scratch_shapes=[pltpu.VMEM((ns * pitch, LANES), F32)] * 4 + [pltpu.VMEM((ng, 8, LANES), F32)] * 2,
        compiler_params=_params(1),
    )(proj, bm_re, bm_im, cm_re, cm_im, dskip, a_re, a_im)


def _s5_bwd(name, dyb, proj, xs_re, xs_im, cmt_re, cmt_im, bmt_re, bmt_im, dskip, a_re, a_im):
    L = dyb.shape[0]
    nb = cmt_re.shape[0]
    ns, W = SLAB * nb, nb * LANES
    ng = ns // 8
    tc = min(LANES, L)
    pitch = tc + 8
    nt = L // tc
    wide = SLAB * LANES
    dn = (((0,), (0,)), ((), ()))

    def body(dy_ref, u_ref, xr_ref, xi_ref, cre_ref, cim_ref, bre_ref, bim_ref, d_ref, ar_ref, ai_ref,
             du_ref, gbr_ref, gbi_ref, gcr_ref, gci_ref, dar_ref, dai_ref,
             sd_re, sd_im, sx_re, sx_im, out_re, out_im, st_re, st_im, acc_re, acc_im):
        first = pl.program_id(0) == 0

        @pl.when(first)
        def _():
            for r in (st_re, st_im, acc_re, acc_im):
                r[...] = jnp.zeros(r.shape, F32)
            for r in (gbr_ref, gbi_ref, gcr_ref, gci_ref):
                r[...] = jnp.zeros(r.shape, F32)

        for j in range(nb):
            dyj = dy_ref[:, j * LANES:(j + 1) * LANES]
            r1 = jnp.dot(dyj, cre_ref[j], preferred_element_type=F32)
            r2 = jnp.dot(dyj, cim_ref[j], preferred_element_type=F32)
            for q in range(SLAB):
                s = SLAB * j + q
                sd_re[pl.ds(s * pitch, tc), :] = r1[:, q * LANES:(q + 1) * LANES]
                sd_im[pl.ds(s * pitch, tc), :] = r2[:, q * LANES:(q + 1) * LANES]
                sx_re[pl.ds(s * pitch, tc), :] = xr_ref[s].astype(F32)
                sx_im[pl.ds(s * pitch, tc), :] = xi_ref[s].astype(F32)
        a_r = [ar_ref[g] for g in range(ng)]
        a_i = [ai_ref[g] for g in range(ng)]
        _scan_steps(tc, pitch, ng, True, a_r, a_i, (sd_re, sd_im), (sx_re, sx_im), out_re, out_im, st_re, st_im,
                    (acc_re, acc_im))
        for j in range(nb):
            cols = slice(j * LANES, (j + 1) * LANES)
            l1 = jnp.concatenate([out_re[pl.ds((SLAB * j + q) * pitch, tc), :] for q in range(SLAB)], axis=1).astype(BF16)
            l2 = jnp.concatenate([out_im[pl.ds((SLAB * j + q) * pitch, tc), :] for q in range(SLAB)], axis=1).astype(BF16)
            dyj = dy_ref[:, cols]
            du = (jnp.dot(l1, bre_ref[j], preferred_element_type=F32) + jnp.dot(l2, bim_ref[j], preferred_element_type=F32)
                  + d_ref[:, cols] * dyj.astype(F32))
            du_ref[:, cols] = du.astype(BF16)
            uj = u_ref[:, cols]
            gbr_ref[j] += lax.dot_general(uj, l1, dn, preferred_element_type=F32)
            gbi_ref[j] += lax.dot_general(uj, l2, dn, preferred_element_type=F32)
            x1 = jnp.concatenate([xr_ref[SLAB * j + q] for q in range(SLAB)], axis=1)
            x2 = jnp.concatenate([xi_ref[SLAB * j + q] for q in range(SLAB)], axis=1)
            gcr_ref[j] += lax.dot_general(dyj, x1, dn, preferred_element_type=F32)
            gci_ref[j] += lax.dot_general(dyj, x2, dn, preferred_element_type=F32)
        dar_ref[...] = acc_re[...]
        dai_ref[...] = acc_im[...]

    full3 = lambda s: pl.BlockSpec(s, lambda i: (0, 0, 0))
    xs = pl.BlockSpec((ns, tc, LANES), lambda i: (0, nt - 1 - i, 0))
    rows = pl.BlockSpec((tc, W), lambda i: (nt - 1 - i, 0))
    mat_a, mat_b = full3((nb, LANES, wide)), full3((nb, wide, LANES))
    vec = full3((ng, 8, LANES))
    return pl.pallas_call(
        body, name=name, grid=(nt,),
        in_specs=[rows, rows, xs, xs, mat_a, mat_a, mat_b, mat_b, pl.BlockSpec((1, W), lambda i: (0, 0)), vec, vec],
        out_specs=[rows, mat_a, mat_a, mat_a, mat_a, vec, vec],
        out_shape=[_sds((L, W), BF16)] + [_sds((nb, LANES, wide), F32)] * 4 + [_sds((ng, 8, LANES), F32)] * 2,
        scratch_shapes=[pltpu.VMEM((ns * pitch, LANES), F32)] * 6 + [pltpu.VMEM((ng, 8, LANES), F32)] * 4,
        compiler_params=_params(1),
    )(dyb, proj, xs_re, xs_im, cmt_re, cmt_im, bmt_re, bmt_im, dskip, a_re, a_im)


def _peer(k):
    x, y, c = lax.axis_index("x"), lax.axis_index("y"), lax.axis_index("c")
    px = 1 - x if (k >> 2) & 1 else x
    py = 1 - y if (k >> 1) & 1 else y
    pc = 1 - c if k & 1 else c
    return (px, py, pc), 4 * px + 2 * py + pc


def _window(ref, kind, idx, n):
    if kind == "col":
        w = ref.shape[1] // n
        return ref.at[:, pl.ds(pl.multiple_of(idx * w, LANES), w)]
    r = ref.shape[0] // n
    return ref.at[pl.ds(pl.multiple_of(idx * r, 8), r), :]


def _all_gather(name, shards, kinds):
    n = len(shards)
    fulls = []
    for s, kind in zip(shards, kinds):
        fulls.append(_sds((s.shape[0], s.shape[1] * N_DEV) if kind == "col" else (s.shape[0] * N_DEV, s.shape[1]), s.dtype))

    def body(*refs):
        src, dst = refs[:n], refs[n:2 * n]
        send, recv, loc = refs[2 * n:]
        me = 4 * lax.axis_index("x") + 2 * lax.axis_index("y") + lax.axis_index("c")
        copies = []
        for a in range(n):
            own = pltpu.make_async_copy(src[a], _window(dst[a], kinds[a], me, N_DEV), loc.at[a])
            own.start()
            copies.append(own)
        sends = []
        for k in range(1, N_DEV):
            dev, _ = _peer(k)
            for a in range(n):
                cp = pltpu.make_async_remote_copy(
                    src_ref=src[a], dst_ref=_window(dst[a], kinds[a], me, N_DEV),
                    send_sem=send.at[a * N_DEV + k], recv_sem=recv.at[a * N_DEV + k],
                    device_id=dev, device_id_type=MESH)
                cp.start()
                sends.append(cp)
        for k in range(1, N_DEV):
            dev, pidx = _peer(k)
            for a in range(n):
                pltpu.make_async_remote_copy(
                    src_ref=src[a], dst_ref=_window(dst[a], kinds[a], pidx, N_DEV),
                    send_sem=send.at[a * N_DEV + k], recv_sem=recv.at[a * N_DEV + k],
                    device_id=dev, device_id_type=MESH).wait_recv()
        for cp in sends:
            cp.wait_send()
        for cp in copies:
            cp.wait()

    any_ = pl.BlockSpec(memory_space=pl.ANY)
    return pl.pallas_call(
        body, name=name, in_specs=[any_] * n, out_specs=[any_] * n, out_shape=fulls,
        scratch_shapes=[pltpu.SemaphoreType.DMA((n * N_DEV,)), pltpu.SemaphoreType.DMA((n * N_DEV,)),
                        pltpu.SemaphoreType.DMA((n,))],
        compiler_params=pltpu.CompilerParams(has_side_effects=True),
    )(*shards)


def _xfer_refs(mode, kinds, a, src, dst, me, pidx):
    if mode == "gather":
        return src[a], _window(dst[a], kinds[a], me, N_DEV), _window(dst[a], kinds[a], pidx, N_DEV)
    return _window(src[a], kinds[a], pidx, N_DEV), dst[a].at[me], dst[a].at[pidx]


def _xfer_out_shapes(mode, arrs, kinds):
    outs = []
    for s, kind in zip(arrs, kinds):
        if mode == "gather":
            outs.append((s.shape[0], s.shape[1] * N_DEV) if kind == "col" else (s.shape[0] * N_DEV, s.shape[1]))
        else:
            outs.append((N_DEV,) + ((s.shape[0], s.shape[1] // N_DEV) if kind == "col" else (s.shape[0] // N_DEV, s.shape[1])))
    return outs


def _sc_xfer(name, mode, arrs, kinds, collective_id):
    n = len(arrs)
    shapes = _xfer_out_shapes(mode, arrs, kinds)
    hbm = pltpu.MemorySpace.HBM
    src = [jax.new_ref(a, memory_space=hbm) for a in arrs]
    dst = [jax.empty_ref(_sds(shp, a.dtype), memory_space=hbm) for shp, a in zip(shapes, arrs)]

    @pl.kernel(mesh=plsc.ScalarSubcoreMesh(axis_name="seq", num_cores=1), name=name,
               scratch_types=(pltpu.SemaphoreType.DMA((n * N_DEV,)), pltpu.SemaphoreType.DMA((n * N_DEV,)),
                              pltpu.SemaphoreType.DMA((n,))),
               compiler_params=pltpu.CompilerParams(collective_id=collective_id))
    def launch(send, recv, loc):
        barrier = pltpu.get_barrier_semaphore()
        for k in range(1, N_DEV):
            pl.semaphore_signal(barrier, inc=1, device_id=_peer(k)[0], device_id_type=MESH)
        pl.semaphore_wait(barrier, N_DEV - 1)
        me = 4 * lax.axis_index("x") + 2 * lax.axis_index("y") + lax.axis_index("c")
        own, sends = [], []
        for a in range(n):
            s, _, d = _xfer_refs(mode, kinds, a, src, dst, me, me)
            own.append(pltpu.make_async_copy(s, d, loc.at[a]))
            own[-1].start()
        for k in range(1, N_DEV):
            dev, pidx = _peer(k)
            for a in range(n):
                s, d, _ = _xfer_refs(mode, kinds, a, src, dst, me, pidx)
                sends.append(pltpu.make_async_remote_copy(src_ref=s, dst_ref=d, send_sem=send.at[a * N_DEV + k],
                                                          recv_sem=recv.at[a * N_DEV + k], device_id=dev, device_id_type=MESH))
                sends[-1].start()
        for cp in own:
            cp.wait()
        for k in range(1, N_DEV):
            dev, pidx = _peer(k)
            for a in range(n):
                s, _, land = _xfer_refs(mode, kinds, a, src, dst, me, pidx)
                pltpu.make_async_remote_copy(src_ref=s, dst_ref=land, send_sem=send.at[a * N_DEV + k],
                                             recv_sem=recv.at[a * N_DEV + k], device_id=dev, device_id_type=MESH).wait_recv()
        for cp in sends:
            cp.wait_send()

    launch()
    return [d[...] for d in dst]


def _sc_gather(name, arrs, kinds, collective_id):
    n = len(arrs)
    pairs = 7
    shapes = _xfer_out_shapes("gather", arrs, kinds)
    hbm = pltpu.MemorySpace.HBM
    src = [jax.new_ref(a, memory_space=hbm) for a in arrs]
    dst = [jax.empty_ref(_sds(shp, a.dtype), memory_space=hbm) for shp, a in zip(shapes, arrs)]

    @pl.kernel(mesh=plsc.ScalarSubcoreMesh(axis_name="seq", num_cores=1), name=name,
               scratch_types=(pltpu.SemaphoreType.DMA((n * pairs,)), pltpu.SemaphoreType.DMA((n * pairs,)),
                              pltpu.SemaphoreType.DMA((n,))),
               compiler_params=pltpu.CompilerParams(collective_id=collective_id))
    def launch(send, recv, loc):
        x, y, c = lax.axis_index("x"), lax.axis_index("y"), lax.axis_index("c")
        me = 4 * x + 2 * y + c
        sib = (x, y, 1 - c)
        chips = []
        for fx, fy in ((1, 0), (0, 1), (1, 1)):
            px, py = (1 - x if fx else x), (1 - y if fy else y)
            chips.append(((px, py, c), 4 * px + 2 * py + c, 4 * px + 2 * py + (1 - c)))
        barrier = pltpu.get_barrier_semaphore()
        for dev in [sib] + [ch[0] for ch in chips]:
            pl.semaphore_signal(barrier, inc=1, device_id=dev, device_id_type=MESH)
        pl.semaphore_wait(barrier, 4)

        def win(a, idx):
            return _window(dst[a], kinds[a], idx, N_DEV)

        def rcopy(a, p, s, d, dev):
            return pltpu.make_async_remote_copy(src_ref=s, dst_ref=d, send_sem=send.at[a * pairs + p],
                                                recv_sem=recv.at[a * pairs + p], device_id=dev, device_id_type=MESH)

        own, sends = [], []
        for a in range(n):
            own.append(pltpu.make_async_copy(src[a], win(a, me), loc.at[a]))
            own[-1].start()
        for j, (dev, _, _) in enumerate(chips):
            for a in range(n):
                sends.append(rcopy(a, 1 + j, src[a], win(a, me), dev))
                sends[-1].start()
        for a in range(n):
            sends.append(rcopy(a, 0, src[a], win(a, me), sib))
            sends[-1].start()
        for j, (dev, idx, _) in enumerate(chips):
            for a in range(n):
                rcopy(a, 1 + j, src[a], win(a, idx), dev).wait_recv()
                sends.append(rcopy(a, 4 + j, win(a, idx), win(a, idx), sib))
                sends[-1].start()
        for cp in own:
            cp.wait()
        for a in range(n):
            rcopy(a, 0, src[a], win(a, 4 * x + 2 * y + (1 - c)), sib).wait_recv()
        for j, (_, _, sidx) in enumerate(chips):
            for a in range(n):
                rcopy(a, 4 + j, src[a], win(a, sidx), sib).wait_recv()
        for cp in sends:
            cp.wait_send()

    launch()
    return [d[...] for d in dst]


_SEQ_IDS = {"gather_in": 7, "gather_mix": 1, "gather_ffn": 2, "grads_down": 3, "grads_up": 8, "grads_mix": 4,
            "grads_small": 5, "grads_in": 6}


def _launch(name, mode, arrs, kinds):
    if mode == "gather":
        return _sc_gather(name, list(arrs), kinds, _SEQ_IDS[name])
    return _sc_xfer(name, mode, list(arrs), kinds, _SEQ_IDS[name])


def _adamw(name, parts, w, m, v):
    P, R, C = parts.shape
    sub = 16 if parts.dtype == BF16 else 8
    tr = R if R * C <= (1 << 18) else _tile(R, max(sub, (1 << 18) // C), sub)

    def body(p_ref, w_ref, m_ref, v_ref, g_ref, d_ref, nm_ref, nv_ref):
        g = p_ref[0].astype(F32)
        for s in range(1, P):
            g = g + p_ref[s].astype(F32)
        m2 = ADAM_B1 * m_ref[...] + (1.0 - ADAM_B1) * g
        v2 = ADAM_B2 * v_ref[...] + (1.0 - ADAM_B2) * (g * g)
        m_hat = m2 / (1.0 - ADAM_B1 ** ADAM_STEP)
        v_hat = v2 / (1.0 - ADAM_B2 ** ADAM_STEP)
        g_ref[...] = g
        d_ref[...] = -ADAM_LR * (m_hat / (jnp.sqrt(v_hat) + ADAM_EPS) + ADAM_WD * w_ref[...])
        nm_ref[...] = m2
        nv_ref[...] = v2

    sp = pl.BlockSpec((tr, C), lambda i: (i, 0))
    return pl.pallas_call(
        body, name=name, grid=(R // tr,),
        in_specs=[pl.BlockSpec((P, tr, C), lambda i: (0, i, 0)), sp, sp, sp], out_specs=[sp] * 4,
        out_shape=[_sds((R, C), F32)] * 4, compiler_params=_params(1),
    )(parts, w, m, v)


def _pack(arrs, row_mult=8):
    pieces, total = [], 0
    for a in arrs:
        f = a.reshape(-1).astype(F32)
        pad = (-f.shape[0]) % (8 * LANES)
        pieces.append(jnp.pad(f, (0, pad)) if pad else f)
        total += f.shape[0] + pad
    tail = (-total) % (row_mult * LANES)
    if tail:
        pieces.append(jnp.zeros((tail,), F32))
    return jnp.concatenate(pieces).reshape(-1, LANES)


def _unpack(buf, shapes, lead=()):
    out, row = [], 0
    for shp in shapes:
        size = 1
        for d in shp:
            size *= d
        rows = -(-size // (8 * LANES)) * 8
        piece = buf[..., row:row + rows, :].reshape(lead + (rows * LANES,))[..., :size]
        out.append(piece.reshape(lead + tuple(shp)))
        row += rows
    return out


def kernel(x, norm_tok, w_in, a_re, a_im, log_dt, b_re, b_im, c_re, c_im, d_skip, w_glu, w_ssm_out, conv_w, conv_b, w_conv_out, w_o, norm_ffn, w_up, ffn_conv_w, ffn_conv_b, w_down, norm_final, loss_target, m_norm_tok, m_w_in, m_a_re, m_a_im, m_log_dt, m_b_re, m_b_im, m_c_re, m_c_im, m_d_skip, m_w_glu, m_w_ssm_out, m_conv_w, m_conv_b, m_w_conv_out, m_w_o, m_norm_ffn, m_w_up, m_ffn_conv_w, m_ffn_conv_b, m_w_down, m_norm_final, v_norm_tok, v_w_in, v_a_re, v_a_im, v_log_dt, v_b_re, v_b_im, v_c_re, v_c_im, v_d_skip, v_w_glu, v_w_ssm_out, v_conv_w, v_conv_b, v_w_conv_out, v_w_o, v_norm_ffn, v_w_up, v_ffn_conv_w, v_ffn_conv_b, v_w_down, v_norm_final):
    args = dict(locals())
    L, D = x.shape[1], x.shape[2]
    G, P, H = b_re.shape[1], b_re.shape[2], b_re.shape[3]
    SW = G * H
    CW = conv_b.shape[1]
    FF = ffn_conv_b.shape[1]
    GP = G * P
    nb = SW // LANES
    gpb = LANES // H
    me = 4 * lax.axis_index("x") + 2 * lax.axis_index("y") + lax.axis_index("c")
    tm = _tile(L, 256, 16)
    x2 = x[0]
    tgt = loss_target[0]

    big = [("w_in", "col"), ("w_glu", "row"), ("w_ssm_out", "col"), ("w_conv_out", "col"), ("w_o", "row"),
           ("w_up", "col"), ("w_down", "row")]
    shards = [_cast_bf16("cast_" + n, args[n][0]) for n, _ in big]
    small_in = _pack([conv_w[0], ffn_conv_w[0]])
    kind = dict(big)
    mixw, ffnw = ["w_glu", "w_ssm_out", "w_conv_out", "w_o"], ["w_up", "w_down"]
    shard = dict(zip([n for n, _ in big], shards))
    gathered = _launch("gather_in", "gather", [shard["w_in"], small_in], ["col", "row"])
    W = {"w_in": gathered[0]}
    W.update(zip(mixw, _launch("gather_mix", "gather", [shard[n] for n in mixw], [kind[n] for n in mixw])))
    W.update(zip(ffnw, _launch("gather_ffn", "gather", [shard[n] for n in ffnw], [kind[n] for n in ffnw])))
    cw_parts, fcw_parts = _unpack(gathered[-1].reshape(N_DEV, -1, LANES), [conv_w.shape[1:], ffn_conv_w.shape[1:]], (N_DEV,))
    conv_w_full = jnp.moveaxis(cw_parts, 0, 1).reshape(3, CW)
    ffn_conv_w_full = jnp.moveaxis(fcw_parts, 0, 1).reshape(3, FF)

    ar_row, ai_row = a_re.reshape(1, GP), a_im.reshape(1, GP)
    ldt_row = jnp.broadcast_to(log_dt.reshape(G, 1), (G, P)).reshape(1, GP)
    brt = jnp.transpose(b_re[0], (2, 0, 1)).reshape(H, GP)
    bit = jnp.transpose(b_im[0], (2, 0, 1)).reshape(H, GP)
    abar_re, abar_im, bbar_re, bbar_im = _prep_fwd("s5_prep", ar_row, ai_row, ldt_row, brt, bit)
    eye = jnp.eye(gpb, dtype=F32)

    def b_blocks(bt):
        return jnp.einsum("ab,hjbp->jahbp", eye, bt.reshape(H, nb, gpb, P)).reshape(nb, LANES, gpb * P)

    def c_blocks(c):
        return jnp.einsum("ab,jahp->jbpah", eye, c.reshape(nb, gpb, H, P)).reshape(nb, gpb * P, LANES)

    def diag_blocks(mat):
        return jnp.einsum("jahap->hjap", mat.reshape(nb, gpb, H, gpb, P))

    bm_re, bm_im = b_blocks(bbar_re), b_blocks(bbar_im)
    cm_re, cm_im = c_blocks(c_re[0]), -c_blocks(c_im[0])
    a3_re, a3_im = abar_re.reshape(-1, 8, LANES), abar_im.reshape(-1, 8, LANES)
    dskip_row = d_skip.reshape(1, SW)

    cbs = SW // LANES
    cb_v, cb_gb, cb_gc = cbs, cbs + CW // LANES, cbs + 2 * CW // LANES
    cb_ma = (SW + 3 * CW) // D
    xn = _rms_fwd("rms_tok", x2, norm_tok, tm)
    proj = _mm("proj", xn, W["w_in"], "nn", out_dtype=BF16)
    xs_re, xs_im, y, ya = _s5_fwd("s5_fwd", proj, bm_re.astype(BF16), bm_im.astype(BF16), cm_re.astype(BF16),
                                  cm_im.astype(BF16), dskip_row, a3_re, a3_im)
    g1 = _mm("glu_gate", ya, W["w_glu"], "nn", out_dtype=BF16)
    ya2 = _glu_fwd("glu", y, g1, tm)
    za = _mm("ssm_out", ya2, W["w_ssm_out"], "nn", out_dtype=BF16)
    q = _convb_fwd("convb", proj, cb_v, cb_gb, cb_gc, conv_w_full, conv_b)
    zb = _mm("conv_out", q, W["w_conv_out"], "nn", out_dtype=BF16)
    merged = _merge_fwd("merge", proj, cb_ma, cb_ma + 1, za, zb, tm)
    o1 = _mm("mix_out", merged, W["w_o"], "nn")
    h1, hn = _res_rms_fwd("rms_ffn", x2, o1, norm_ffn, tm)
    hh, f = _ffn_up_act("ffn_up_act", hn, W["w_up"], ffn_conv_w_full, ffn_conv_b)
    o2 = _mm("ffn_down", f, W["w_down"], "nn", tk=2816)
    dh2, dh2b, g_norm_final, loss_part = _final("final", h1, o2, norm_final.reshape(1, D), tgt, tm)

    gw_down = _mm("gw_down", f, dh2b, "tn", out_dtype=BF16, tm=1408, tn=512, tk=L)
    dh2b, gw_down = lax.optimization_barrier((dh2b, gw_down))
    parts = {"w_down": _launch("grads_down", "exchange", [gw_down], [kind["w_down"]])[0]}
    dhh, g_ffn_conv_w, g_ffn_conv_b = _ffn_down_bwd_act("ffn_down_bwd_act", dh2b, W["w_down"], hh, ffn_conv_w_full,
                                                         ffn_conv_b)
    nhalf = lambda t: FF // t
    gw_up = _mm("gw_up", hn, dhh, "tn", out_dtype=BF16, tn=_tile(FF, 1024), tk=L, dims=(D, 2 * FF, L),
                b_spec=lambda a, b, c: pl.BlockSpec((None, c, b), lambda i, j, k: (j // nhalf(b), k, j % nhalf(b))))
    dhh, gw_up = lax.optimization_barrier((dhh, gw_up))
    parts["w_up"] = _launch("grads_up", "exchange", [gw_up], [kind["w_up"]])[0]
    dhn = _mm("d_ffn_in", dhh, W["w_up"], "nt", out_dtype=BF16, tk=_tile(FF, 2816), dims=(L, D, 2 * FF),
              a_spec=lambda a, b, c: pl.BlockSpec((None, a, c), lambda i, j, k: (k // nhalf(c), i, k % nhalf(c))))
    dh1, dh1b, g_norm_ffn = _rms_bwd("rms_ffn_bwd", dhn, h1, norm_ffn, dh2, tm, True)

    dmerged = _mm("d_merged", dh1b, W["w_o"], "nt", out_dtype=BF16)
    gw_o = _mm("gw_o", merged, dh1b, "tn", out_dtype=BF16, tk=L)
    dmerged, gw_o = lax.optimization_barrier((dmerged, gw_o))
    dza, dzb, dma, dmb = _merge_bwd("merge_bwd", proj, cb_ma, cb_ma + 1, za, zb, dmerged, tm)
    dq = _mm("d_q", dzb, W["w_conv_out"], "nt", out_dtype=BF16)
    gw_conv_out = _mm("gw_conv_out", q, dzb, "tn", out_dtype=BF16, tk=L)
    dq, gw_conv_out = lax.optimization_barrier((dq, gw_conv_out))
    dv, dgb, dgc, g_conv_w, g_conv_b = _convb_bwd("convb_bwd", proj, cb_v, cb_gb, cb_gc, conv_w_full, conv_b, dq)
    dya2 = _mm("d_ya2", dza, W["w_ssm_out"], "nt", out_dtype=BF16)
    gw_ssm_out = _mm("gw_ssm_out", ya2, dza, "tn", out_dtype=BF16, tk=L)
    dya2, gw_ssm_out = lax.optimization_barrier((dya2, gw_ssm_out))
    dy_direct, dg1 = _glu_bwd("glu_bwd", y, g1, dya2, tm)
    dya_g = _mm("d_ya_gate", dg1, W["w_glu"], "nt", out_dtype=BF16)
    gw_glu = _mm("gw_glu", ya, dg1, "tn", out_dtype=BF16, tk=L)
    dya_g, gw_glu = lax.optimization_barrier((dya_g, gw_glu))
    parts_mix = _launch("grads_mix", "exchange", [gw_glu, gw_ssm_out, gw_conv_out, gw_o], [kind[n] for n in mixw])
    dyb, g_dskip = _gelu_bwd("gelu_bwd", y, dy_direct, dya_g, proj, tm)
    swap = lambda m: jnp.swapaxes(m, 1, 2).astype(BF16)
    du, gb_re, gb_im, gc_re, gc_im, dab_re, dab_im = _s5_bwd(
        "s5_bwd", dyb, proj, xs_re, xs_im, swap(cm_re), swap(cm_im), swap(bm_re), swap(bm_im), dskip_row, a3_re, a3_im)
    g_ar, g_ai, g_ldt, g_brt, g_bit = _prep_bwd(
        "s5_prep_bwd", ar_row, ai_row, ldt_row, brt, bit, dab_re.reshape(1, GP), dab_im.reshape(1, GP),
        diag_blocks(gb_re).reshape(H, GP), diag_blocks(gb_im).reshape(H, GP))
    small = dict(
        a_re=g_ar.reshape(1, G, P), a_im=g_ai.reshape(1, G, P),
        log_dt=g_ldt.reshape(G, P).sum(axis=1).reshape(1, G),
        b_re=jnp.transpose(g_brt.reshape(H, G, P), (1, 2, 0))[None], b_im=jnp.transpose(g_bit.reshape(H, G, P), (1, 2, 0))[None],
        c_re=jnp.transpose(diag_blocks(gc_re), (1, 2, 0, 3)).reshape(1, G, H, P),
        c_im=-jnp.transpose(diag_blocks(gc_im), (1, 2, 0, 3)).reshape(1, G, H, P),
        d_skip=g_dskip.reshape(1, G, H), conv_b=g_conv_b, norm_ffn=g_norm_ffn, ffn_conv_b=g_ffn_conv_b,
        norm_final=g_norm_final.reshape(D), conv_w=g_conv_w[None], ffn_conv_w=g_ffn_conv_w[None])
    rep = ["a_re", "a_im", "log_dt", "b_re", "b_im", "c_re", "c_im", "d_skip", "conv_b", "norm_ffn", "ffn_conv_b", "norm_final"]
    order = rep + ["conv_w", "ffn_conv_w"]
    full_shapes = {n: args[n].shape for n in rep}
    full_shapes["conv_w"], full_shapes["ffn_conv_w"] = (1, 3, CW), (1, 3, FF)
    rep_pack = _pack([small[n] for n in rep], LANES)
    rep_rows = rep_pack.shape[0]
    gpack = jnp.concatenate([loss_part, rep_pack, _pack([small["conv_w"], small["ffn_conv_w"]])], axis=0)
    rep0 = loss_part.shape[0]
    rows = gpack.shape[0]
    du, gpack = lax.optimization_barrier((du, gpack))
    gall = _launch("grads_small", "gather", [gpack], ["row"])[0]

    dproj = _concat_cols("dproj", [du, dv, dgb, dgc, dma, dmb], tm)
    gw_in = _mm("gw_in", xn, dproj, "tn", out_dtype=BF16, tk=L)
    dproj, gw_in = lax.optimization_barrier((dproj, gw_in))
    parts_in = _launch("grads_in", "exchange", [gw_in], ["col"])
    dxn = _mm("d_xn", dproj, W["w_in"], "nt", out_dtype=BF16, tk=4096)
    grad_x, g_norm_tok = _rms_bwd("rms_tok_bwd", dxn, x2, norm_tok, dh1, tm, False)

    res = {}

    def big_update(n):
        res[n] = [r[None] for r in _adamw("adamw_" + n, parts[n], args[n][0], args["m_" + n][0], args["v_" + n][0])]

    def after(xs, dep):
        return lax.optimization_barrier((list(xs), dep))[0]

    parts["w_down"] = after([parts["w_down"]], grad_x)[0]
    big_update("w_down")
    parts["w_up"] = after([parts["w_up"]], res["w_down"][1])[0]
    big_update("w_up")
    parts.update(zip(mixw, after(parts_mix, [res[n][1] for n in ffnw])))
    for n in mixw:
        big_update(n)
    gall = after([gall], [res[n][1] for n in mixw])[0].reshape(N_DEV, rows, LANES)
    gcw, gfcw = _unpack(gall[:, rep0 + rep_rows:], [full_shapes["conv_w"], full_shapes["ffn_conv_w"]], (N_DEV,))
    cws, fcws = CW // N_DEV, FF // N_DEV
    gcw = lax.dynamic_slice_in_dim(gcw[:, 0], me * cws, cws, axis=2)
    gfcw = lax.dynamic_slice_in_dim(gfcw[:, 0], me * fcws, fcws, axis=2)
    res["conv_w"] = [r[None] for r in _adamw("adamw_conv_w", gcw, conv_w[0], m_conv_w[0], v_conv_w[0])]
    res["ffn_conv_w"] = [r[None] for r in _adamw("adamw_ffn_conv_w", gfcw, ffn_conv_w[0], m_ffn_conv_w[0], v_ffn_conv_w[0])]
    rep_out = _adamw("adamw_small", gall[:, rep0:rep0 + rep_rows], _pack([args[n] for n in rep], LANES),
                     _pack([args["m_" + n] for n in rep], LANES), _pack([args["v_" + n] for n in rep], LANES))
    rep_out = [_unpack(r, [full_shapes[n] for n in rep]) for r in rep_out]
    for i, n in enumerate(rep):
        res[n] = [r[i] for r in rep_out]
    nt_pack = after([_pack([g_norm_tok])], [res[n][1] for n in ("a_re", "conv_w", "ffn_conv_w")])
    nt_all = _all_gather("gather_norm_tok_grad", nt_pack, ["row"])[0].reshape(N_DEV, -1, LANES)
    nt_out = _adamw("adamw_norm_tok", nt_all, _pack([norm_tok]), _pack([m_norm_tok]), _pack([v_norm_tok]))
    res["norm_tok"] = [_unpack(r, [norm_tok.shape])[0] for r in nt_out]
    parts["w_in"] = after(parts_in, nt_out[0])[0]
    big_update("w_in")

    loss = jnp.sum(gall[:, 0, 0])
    names = ["norm_tok", "w_in", "a_re", "a_im", "log_dt", "b_re", "b_im", "c_re", "c_im", "d_skip", "w_glu", "w_ssm_out",
             "conv_w", "conv_b", "w_conv_out", "w_o", "norm_ffn", "w_up", "ffn_conv_w", "ffn_conv_b", "w_down", "norm_final"]
    out = [loss, grad_x[None]]
    for slot in range(4):
        out += [res[n][slot] for n in names]
    return tuple(out)
```

```python
import jax
import jax.numpy as jnp
from jax import lax
from jax.experimental import pallas as pl
from jax.experimental.pallas import tpu as pltpu
from jax.experimental.pallas import tpu_sc as plsc

F32 = jnp.float32
BF16 = jnp.bfloat16
N_DEV = 8
LANES = 128
SLAB = 4
EPS = 1e-6
ADAM_LR = 0.001
ADAM_B1 = 0.9
ADAM_B2 = 0.999
ADAM_EPS = 1e-08
ADAM_WD = 0.01
ADAM_STEP = 10
VMEM_LIMIT = 56 * 1024 * 1024
MESH = pl.DeviceIdType.MESH


def _tile(n, pref, mult=LANES):
    best = None
    t = mult
    while t <= min(n, pref):
        if n % t == 0:
            best = t
        t += mult
    return best if best is not None else n


def _params(ndim):
    return pltpu.CompilerParams(dimension_semantics=("arbitrary",) * ndim, vmem_limit_bytes=VMEM_LIMIT)


def _sds(shape, dtype):
    return jax.ShapeDtypeStruct(tuple(shape), dtype)


def _mm(name, a, b, mode, *, out_dtype=F32, tm=1024, tn=1024, tk=2048, dims=None, a_spec=None, b_spec=None):
    if dims is None:
        if mode == "nn":
            (M, K), N = a.shape, b.shape[1]
        elif mode == "nt":
            (M, K), N = a.shape, b.shape[0]
        else:
            (K, M), N = a.shape, b.shape[1]
    else:
        M, N, K = dims
    tm, tn, tk = _tile(M, tm), _tile(N, tn), _tile(K, tk)
    nk = K // tk
    if mode == "nn":
        dn = (((1,), (0,)), ((), ()))
        sa = pl.BlockSpec((tm, tk), lambda i, j, k: (i, k))
        sb = pl.BlockSpec((tk, tn), lambda i, j, k: (k, j))
    elif mode == "nt":
        dn = (((1,), (1,)), ((), ()))
        sa = pl.BlockSpec((tm, tk), lambda i, j, k: (i, k))
        sb = pl.BlockSpec((tn, tk), lambda i, j, k: (j, k))
    else:
        dn = (((0,), (0,)), ((), ()))
        sa = pl.BlockSpec((tk, tm), lambda i, j, k: (k, i))
        sb = pl.BlockSpec((tk, tn), lambda i, j, k: (k, j))
    sa = a_spec(tm, tn, tk) if a_spec is not None else sa
    sb = b_spec(tm, tn, tk) if b_spec is not None else sb
    use_acc = nk > 1 and out_dtype != F32

    def body(a_ref, b_ref, o_ref, *acc):
        k = pl.program_id(2)
        p = lax.dot_general(a_ref[...], b_ref[...], dn, preferred_element_type=F32)
        if nk == 1:
            o_ref[...] = p.astype(out_dtype)
        else:
            tgt = acc[0] if use_acc else o_ref

            @pl.when(k == 0)
            def _():
                tgt[...] = p

            @pl.when(k > 0)
            def _():
                tgt[...] += p

            if use_acc:
                @pl.when(k == nk - 1)
                def _():
                    o_ref[...] = acc[0][...].astype(out_dtype)

    return pl.pallas_call(
        body, name=name, grid=(M // tm, N // tn, nk),
        in_specs=[sa, sb], out_specs=pl.BlockSpec((tm, tn), lambda i, j, k: (i, j)),
        out_shape=_sds((M, N), out_dtype),
        scratch_shapes=[pltpu.VMEM((tm, tn), F32)] if use_acc else [],
        compiler_params=_params(3),
    )(a, b)


def _mm_rows(name, a, b, extra, outs, epilogue, *, tm, tk):
    M, K = a.shape
    N = b.shape[1]
    tm, tk = _tile(M, tm, 16), _tile(K, tk)
    nk = K // tk
    ne, no = len(extra), len(outs)

    def body(a_ref, b_ref, *rest):
        i, k = pl.program_id(0), pl.program_id(1)
        p = jnp.dot(a_ref[...], b_ref[...], preferred_element_type=F32)
        if nk == 1:
            epilogue(i, p, rest[:ne], rest[ne:ne + no])
            return
        acc = rest[ne + no]
        _acc_rows(k, acc, p)

        @pl.when(k == nk - 1)
        def _():
            epilogue(i, acc, rest[:ne], rest[ne:ne + no])

    return pl.pallas_call(
        body, name=name, grid=(M // tm, nk),
        in_specs=[pl.BlockSpec((tm, tk), lambda i, k: (i, k)), pl.BlockSpec((tk, N), lambda i, k: (k, 0))]
        + [s for _, s in extra],
        out_specs=[s for _, s in outs], out_shape=[o for o, _ in outs],
        scratch_shapes=[pltpu.VMEM((tm, N), F32)] if nk > 1 else [], compiler_params=_params(2),
    )(a, b, *[e for e, _ in extra])


def _rows(name, body, L, tm, ins, outs):
    return pl.pallas_call(
        body, name=name, grid=(L // tm,),
        in_specs=[s for _, s in ins], out_specs=[s for _, s in outs],
        out_shape=[o for o, _ in outs], compiler_params=_params(1),
    )(*[a for a, _ in ins])


def _rs(tm, w, cb=0):
    return pl.BlockSpec((tm, w), lambda i: (i, cb))


def _fs(shape):
    return pl.BlockSpec(tuple(shape), lambda i: (0,) * len(shape))


def _acc_rows(i, ref, part):
    @pl.when(i == 0)
    def _():
        ref[...] = part

    @pl.when(i > 0)
    def _():
        ref[...] += part


def _cast_bf16(name, w):
    R, C = w.shape
    tr = _tile(R, max(16, (1 << 20) // C), 16)

    def body(w_ref, o_ref):
        o_ref[...] = w_ref[...].astype(BF16)

    return _rows(name, body, R, tr, [(w, _rs(tr, C))], [(_sds((R, C), BF16), _rs(tr, C))])[0]


def _concat_cols(name, pieces, tm):
    L = pieces[0].shape[0]
    widths = [p.shape[1] for p in pieces]

    def body(*refs):
        o_ref, off = refs[-1], 0
        for p_ref, w in zip(refs[:-1], widths):
            o_ref[:, off:off + w] = p_ref[...]
            off += w

    return _rows(name, body, L, tm, [(p, _rs(tm, w)) for p, w in zip(pieces, widths)],
                 [(_sds((L, sum(widths)), pieces[0].dtype), _rs(tm, sum(widths)))])[0]


def _rms_fwd(name, x, g, tm):
    L, D = x.shape

    def body(x_ref, g_ref, o_ref):
        xv = x_ref[...]
        r = lax.rsqrt(jnp.mean(xv * xv, axis=-1, keepdims=True) + EPS)
        o_ref[...] = (xv * r * g_ref[...]).astype(BF16)

    return _rows(name, body, L, tm, [(x, _rs(tm, D)), (g, _fs((1, D)))], [(_sds((L, D), BF16), _rs(tm, D))])[0]


def _mm_res_rms(name, a, w, x, g):
    L, D = x.shape

    def epilogue(i, o, extra, outs):
        x_ref, g_ref = extra
        for r0 in range(0, tm, sub):
            rs = slice(r0, r0 + sub)
            h = x_ref[rs, :] + o[rs, :]
            r = lax.rsqrt(jnp.mean(h * h, axis=-1, keepdims=True) + EPS)
            outs[0][rs, :] = h
            outs[1][rs, :] = (h * r * g_ref[...]).astype(BF16)

    tm = _tile(L, 512, 16)
    sub = _tile(tm, LANES, 16)
    row = pl.BlockSpec((tm, D), lambda i, k: (i, 0))
    return _mm_rows(name, a, w, [(x, row), (g, pl.BlockSpec((1, D), lambda i, k: (0, 0)))],
                    [(_sds((L, D), F32), row), (_sds((L, D), BF16), row)], epilogue, tm=tm, tk=a.shape[1])


def _rms_bwd(name, dn, h, g, dres, tm, with_bf16):
    L, D = h.shape

    def body(dn_ref, h_ref, g_ref, dres_ref, dh_ref, *rest):
        i = pl.program_id(0)
        h = h_ref[...]
        r = lax.rsqrt(jnp.mean(h * h, axis=-1, keepdims=True) + EPS)
        xh = h * r
        d = dn_ref[...].astype(F32)
        dxh = d * g_ref[...]
        dh = dres_ref[...] + r * (dxh - xh * jnp.mean(dxh * xh, axis=-1, keepdims=True))
        dh_ref[...] = dh
        if with_bf16:
            rest[0][...] = dh.astype(BF16)
        _acc_rows(i, rest[-1], jnp.sum(d * xh, axis=0, keepdims=True))

    outs = [(_sds((L, D), F32), _rs(tm, D))]
    if with_bf16:
        outs.append((_sds((L, D), BF16), _rs(tm, D)))
    outs.append((_sds((1, D), F32), _fs((1, D))))
    return _rows(name, body, L, tm, [(dn, _rs(tm, D)), (h, _rs(tm, D)), (g, _fs((1, D))), (dres, _rs(tm, D))], outs)


def _mm_final(name, f, w, h1, g, tgt, tk):
    L, D = h1.shape

    def epilogue(i, o2, extra, outs):
        h1_ref, g_ref, t_ref = extra
        dh_ref, dhb_ref, dg_ref, loss_ref = outs
        gv = g_ref[...]
        dg, part = 0.0, 0.0
        for r0 in range(0, tm, sub):
            rs = slice(r0, r0 + sub)
            h = h1_ref[rs, :] + o2[rs, :]
            r = lax.rsqrt(jnp.mean(h * h, axis=-1, keepdims=True) + EPS)
            xh = h * r
            e = xh * gv - t_ref[rs, :]
            part = part + 0.5 * jnp.sum(jnp.mean(e * e, axis=-1, keepdims=True), axis=0, keepdims=True)
            dy = e / D
            dxh = dy * gv
            dh = r * (dxh - xh * jnp.mean(dxh * xh, axis=-1, keepdims=True))
            dh_ref[rs, :] = dh
            dhb_ref[rs, :] = dh.astype(BF16)
            dg = dg + jnp.sum(dy * xh, axis=0, keepdims=True)
        _acc_rows(i, dg_ref, dg)
        _acc_rows(i, loss_ref, jnp.broadcast_to(part, (8, LANES)))

    tm = _tile(L, 512, 16)
    sub = _tile(tm, LANES, 16)
    row = pl.BlockSpec((tm, D), lambda i, k: (i, 0))
    fixed = lambda shape: pl.BlockSpec(shape, lambda i, k: (0, 0))
    return _mm_rows(name, f, w, [(h1, row), (g, fixed((1, D))), (tgt, row)],
                    [(_sds((L, D), F32), row), (_sds((L, D), BF16), row), (_sds((1, D), F32), fixed((1, D))),
                     (_sds((8, LANES), F32), fixed((8, LANES)))], epilogue, tm=tm, tk=tk)


def _gelu_and_grad(x):
    c, k = 0.7978845608028654, 0.044715
    x2 = x * x
    t = jnp.tanh(c * x * (1.0 + k * x2))
    half = 0.5 * x
    return half * (1.0 + t), 0.5 * (1.0 + t) + half * (1.0 - t * t) * (c * (1.0 + 3.0 * k * x2))


def _glu_fn(y, g1):
    ya = jax.nn.gelu(y)
    return ya * jax.nn.sigmoid(g1)


def _glu_fwd(name, y, g1, tm):
    L, W = y.shape

    def body(y_ref, g_ref, o_ref):
        o_ref[...] = _glu_fn(y_ref[...], g_ref[...].astype(F32)).astype(BF16)

    return _rows(name, body, L, tm, [(y, _rs(tm, W)), (g1, _rs(tm, W))], [(_sds((L, W), BF16), _rs(tm, W))])[0]


def _glu_bwd(name, y, g1, dya2, tm):
    L, W = y.shape

    def body(y_ref, g_ref, d_ref, dy_ref, dg_ref):
        _, vjp = jax.vjp(_glu_fn, y_ref[...], g_ref[...].astype(F32))
        dy, dg = vjp(d_ref[...].astype(F32))
        dy_ref[...] = dy
        dg_ref[...] = dg.astype(BF16)

    return _rows(name, body, L, tm, [(y, _rs(tm, W)), (g1, _rs(tm, W)), (dya2, _rs(tm, W))],
                 [(_sds((L, W), F32), _rs(tm, W)), (_sds((L, W), BF16), _rs(tm, W))])


def _gelu_bwd(name, y, dy_direct, dya_g, proj, tm):
    L, W = y.shape

    def body(y_ref, dd_ref, dg_ref, u_ref, dyb_ref, dsk_ref):
        i = pl.program_id(0)
        dy = dd_ref[...] + dg_ref[...].astype(F32) * _gelu_and_grad(y_ref[...])[1]
        dyb_ref[...] = dy.astype(BF16)
        _acc_rows(i, dsk_ref, jnp.sum(dy * u_ref[...].astype(F32), axis=0, keepdims=True))

    return _rows(name, body, L, tm,
                 [(y, _rs(tm, W)), (dy_direct, _rs(tm, W)), (dya_g, _rs(tm, W)), (proj, _rs(tm, W, 0))],
                 [(_sds((L, W), BF16), _rs(tm, W)), (_sds((1, W), F32), _fs((1, W)))])


def _merge_fn(ma, mb, za, zb):
    return jax.nn.sigmoid(ma) * za + jax.nn.sigmoid(mb) * zb


def _merge_fwd(name, proj, cb_a, cb_b, za, zb, tm):
    L, D = za.shape

    def body(ma_ref, mb_ref, za_ref, zb_ref, o_ref):
        o_ref[...] = _merge_fn(ma_ref[...].astype(F32), mb_ref[...].astype(F32), za_ref[...].astype(F32),
                               zb_ref[...].astype(F32)).astype(BF16)

    return _rows(name, body, L, tm,
                 [(proj, _rs(tm, D, cb_a)), (proj, _rs(tm, D, cb_b)), (za, _rs(tm, D)), (zb, _rs(tm, D))],
                 [(_sds((L, D), BF16), _rs(tm, D))])[0]


def _merge_bwd(name, proj, cb_a, cb_b, za, zb, dmerged, tm):
    L, D = za.shape

    def body(ma_ref, mb_ref, za_ref, zb_ref, d_ref, dza_ref, dzb_ref, dma_ref, dmb_ref):
        _, vjp = jax.vjp(_merge_fn, ma_ref[...].astype(F32), mb_ref[...].astype(F32), za_ref[...].astype(F32),
                         zb_ref[...].astype(F32))
        dma, dmb, dza, dzb = vjp(d_ref[...].astype(F32))
        dza_ref[...] = dza.astype(BF16)
        dzb_ref[...] = dzb.astype(BF16)
        dma_ref[...] = dma.astype(BF16)
        dmb_ref[...] = dmb.astype(BF16)

    return _rows(name, body, L, tm,
                 [(proj, _rs(tm, D, cb_a)), (proj, _rs(tm, D, cb_b)), (za, _rs(tm, D)), (zb, _rs(tm, D)),
                  (dmerged, _rs(tm, D))],
                 [(_sds((L, D), BF16), _rs(tm, D)), (_sds((L, D), BF16), _rs(tm, D)),
                  (_sds((L, D), BF16), _rs(tm, D)), (_sds((L, D), BF16), _rs(tm, D))])


def _shift_down(x, k):
    row = lax.broadcasted_iota(jnp.int32, x.shape, 0)
    return jnp.where(row >= k, pltpu.roll(x, k, axis=0), 0.0)


def _shift_up(x, k):
    n = x.shape[0]
    row = lax.broadcasted_iota(jnp.int32, x.shape, 0)
    return jnp.where(row < n - k, pltpu.roll(x, n - k, axis=0), 0.0)


def _conv3(cv, w_ref, b_ref):
    return (w_ref[2:3, :] * cv + w_ref[1:2, :] * _shift_down(cv, 1) + w_ref[0:1, :] * _shift_down(cv, 2)
            + b_ref[...])


def _conv3_bwd(dcc, cv, w_ref):
    dcv = w_ref[2:3, :] * dcc + w_ref[1:2, :] * _shift_up(dcc, 1) + w_ref[0:1, :] * _shift_up(dcc, 2)
    dw = [jnp.sum(dcc * _shift_down(cv, 2), axis=0, keepdims=True),
          jnp.sum(dcc * _shift_down(cv, 1), axis=0, keepdims=True),
          jnp.sum(dcc * cv, axis=0, keepdims=True)]
    db = jnp.sum(dcc, axis=0, keepdims=True)
    return dcv, dw, db


def _store_rows(ref, rows):
    for r, val in enumerate(rows):
        ref[r:r + 1, :] = val


def _cols(name, body, ncb, ins, outs):
    return pl.pallas_call(
        body, name=name, grid=(ncb,),
        in_specs=[s for _, s in ins], out_specs=[s for _, s in outs],
        out_shape=[o for o, _ in outs], compiler_params=_params(1),
    )(*[a for a, _ in ins])


def _cb(L, w, off=0):
    return pl.BlockSpec((L, w), lambda j: (0, j + off))


def _convb_fwd(name, proj, cb_v, cb_gb, cb_gc, w, b):
    L = proj.shape[0]
    W = w.shape[1]
    c = LANES

    def body(v_ref, gb_ref, gc_ref, w_ref, b_ref, q_ref):
        cc = _conv3(gc_ref[...].astype(F32) * v_ref[...].astype(F32), w_ref, b_ref)
        q_ref[...] = (gb_ref[...].astype(F32) * cc).astype(BF16)

    return _cols(name, body, W // c,
                 [(proj, _cb(L, c, cb_v)), (proj, _cb(L, c, cb_gb)), (proj, _cb(L, c, cb_gc)),
                  (w, _cb(3, c)), (b, _cb(1, c))],
                 [(_sds((L, W), BF16), _cb(L, c))])[0]


def _convb_bwd(name, proj, cb_v, cb_gb, cb_gc, w, b, dq):
    L = proj.shape[0]
    W = w.shape[1]
    c = LANES

    def body(v_ref, gb_ref, gc_ref, w_ref, b_ref, dq_ref, dv_ref, dgb_ref, dgc_ref, dw_ref, db_ref):
        v, gc = v_ref[...].astype(F32), gc_ref[...].astype(F32)
        cv = gc * v
        cc = _conv3(cv, w_ref, b_ref)
        dq = dq_ref[...].astype(F32)
        dgb_ref[...] = (dq * cc).astype(BF16)
        dcv, dw, db = _conv3_bwd(dq * gb_ref[...].astype(F32), cv, w_ref)
        dv_ref[...] = (dcv * gc).astype(BF16)
        dgc_ref[...] = (dcv * v).astype(BF16)
        _store_rows(dw_ref, dw)
        db_ref[...] = db

    return _cols(name, body, W // c,
                 [(proj, _cb(L, c, cb_v)), (proj, _cb(L, c, cb_gb)), (proj, _cb(L, c, cb_gc)),
                  (w, _cb(3, c)), (b, _cb(1, c)), (dq, _cb(L, c))],
                 [(_sds((L, W), BF16), _cb(L, c)), (_sds((L, W), BF16), _cb(L, c)), (_sds((L, W), BF16), _cb(L, c)),
                  (_sds((3, W), F32), _cb(3, c)), (_sds((1, W), F32), _cb(1, c))])


HALO = 16


def _ffn_up_act(name, hn, w_up, w, b):
    L, D = hn.shape
    Fw = w.shape[1]
    tm, tc = _tile(L, 1024, HALO), _tile(Fw, 512)
    ncb = Fw // tc

    def body(x_ref, wa_ref, wg_ref, w_ref, b_ref, hh_ref, f_ref, carry):
        i, j = pl.program_id(0), pl.program_id(1)
        a16 = jnp.dot(x_ref[...], wa_ref[...], preferred_element_type=F32).astype(BF16)
        g16 = jnp.dot(x_ref[...], wg_ref[...], preferred_element_type=F32).astype(BF16)
        hh_ref[0] = a16
        hh_ref[1] = g16
        for c0 in range(0, tc, LANES):
            cs = slice(c0, c0 + LANES)
            prev = jnp.where(i == 0, 0.0, carry[j, :, cs])
            x = jnp.concatenate([prev, a16[:, cs].astype(F32)], axis=0)
            n = x.shape[0]
            a = (w_ref[2:3, cs] * x + w_ref[1:2, cs] * pltpu.roll(x, 1, axis=0) + w_ref[0:1, cs] * pltpu.roll(x, 2, axis=0)
                 + b_ref[:, cs])[8:n]
            f_ref[:, cs] = (_gelu_and_grad(a)[0] * g16[:, cs].astype(F32)).astype(BF16)
            carry[j, :, cs] = x[n - 8:n]

    return pl.pallas_call(
        body, name=name, grid=(L // tm, ncb),
        in_specs=[pl.BlockSpec((tm, D), lambda i, j: (i, 0)), pl.BlockSpec((D, tc), lambda i, j: (0, j)),
                  pl.BlockSpec((D, tc), lambda i, j: (0, j + ncb)),
                  pl.BlockSpec((3, tc), lambda i, j: (0, j)), pl.BlockSpec((1, tc), lambda i, j: (0, j))],
        out_specs=[pl.BlockSpec((2, tm, tc), lambda i, j: (0, i, j)), pl.BlockSpec((tm, tc), lambda i, j: (i, j))],
        out_shape=[_sds((2, L, Fw), BF16), _sds((L, Fw), BF16)],
        scratch_shapes=[pltpu.VMEM((ncb, 8, tc), F32)], compiler_params=_params(2),
    )(hn, w_up, w_up, w, b)


def _ffn_down_bwd_act(name, dy, w_down, hh, w, b):
    L, D = dy.shape
    Fw = w.shape[1]
    tm, tc = _tile(L, 512, HALO), _tile(Fw, 512)
    ncb, nrt, rpt = Fw // tc, L // tm, tm // HALO
    dn = (((1,), (1,)), ((), ()))

    def body(dy_ref, wd_ref, a_ref, ap_ref, h2_ref, w_ref, b_ref, dhh_ref, dw_ref, db_ref, carry, acc):
        i, j = pl.program_id(0), pl.program_id(1)
        first_rows = i == nrt - 1
        d16 = lax.dot_general(dy_ref[...], wd_ref[...], dn, preferred_element_type=F32).astype(BF16)
        for c0 in range(0, tc, LANES):
            cs = slice(c0, c0 + LANES)
            h1 = jnp.concatenate([jnp.where(first_rows, 0.0, ap_ref[:, cs].astype(F32)), a_ref[:, cs].astype(F32)], axis=0)
            s1, s2 = pltpu.roll(h1, 1, axis=0), pltpu.roll(h1, 2, axis=0)
            n = h1.shape[0]
            a = (w_ref[2:3, cs] * h1 + w_ref[1:2, cs] * s1 + w_ref[0:1, cs] * s2 + b_ref[:, cs])[HALO:n]
            ga, dga = _gelu_and_grad(a)
            d = d16[:, cs].astype(F32)
            da = d * h2_ref[:, cs].astype(F32) * dga
            dae = jnp.concatenate([da, jnp.where(i == 0, 0.0, carry[j, :, cs])], axis=0)
            m = dae.shape[0]
            dh1 = w_ref[2:3, cs] * dae + w_ref[1:2, cs] * pltpu.roll(dae, m - 1, axis=0) + w_ref[0:1, cs] * pltpu.roll(dae, m - 2, axis=0)
            dhh_ref[0, :, cs] = dh1[0:tm].astype(BF16)
            dhh_ref[1, :, cs] = (d * ga).astype(BF16)
            carry[j, :, cs] = da[0:8]
            rows = [jnp.sum(da * s2[HALO:n], axis=0, keepdims=True), jnp.sum(da * s1[HALO:n], axis=0, keepdims=True),
                    jnp.sum(da * h1[HALO:n], axis=0, keepdims=True), jnp.sum(da, axis=0, keepdims=True)]
            for r in range(4):
                tot = jnp.where(i == 0, 0.0, acc[j, r:r + 1, cs]) + rows[r]
                acc[j, r:r + 1, cs] = tot
                if r < 3:
                    dw_ref[r:r + 1, cs] = tot
                else:
                    db_ref[:, cs] = tot

    rt = lambda i: nrt - 1 - i
    dhh, dw, db = pl.pallas_call(
        body, name=name, grid=(nrt, ncb),
        in_specs=[pl.BlockSpec((tm, D), lambda i, j: (rt(i), 0)), pl.BlockSpec((tc, D), lambda i, j: (j, 0)),
                  pl.BlockSpec((None, tm, tc), lambda i, j: (0, rt(i), j)),
                  pl.BlockSpec((None, HALO, tc), lambda i, j: (0, jnp.maximum(rt(i) * rpt - 1, 0), j)),
                  pl.BlockSpec((None, tm, tc), lambda i, j: (1, rt(i), j)),
                  pl.BlockSpec((3, tc), lambda i, j: (0, j)), pl.BlockSpec((1, tc), lambda i, j: (0, j))],
        out_specs=[pl.BlockSpec((2, tm, tc), lambda i, j: (0, rt(i), j)),
                   pl.BlockSpec((None, 3, tc), lambda i, j: (i, 0, j)), pl.BlockSpec((None, 1, tc), lambda i, j: (i, 0, j))],
        out_shape=[_sds((2, L, Fw), BF16), _sds((nrt, 3, Fw), F32), _sds((nrt, 1, Fw), F32)],
        scratch_shapes=[pltpu.VMEM((ncb, 8, tc), F32), pltpu.VMEM((ncb, 8, tc), F32)], compiler_params=_params(2),
    )(dy, w_down, hh, hh, hh, w, b)
    return dhh, dw[nrt - 1], db[nrt - 1]


def _prep_fn(ar, ai, ldt, brt, bit):
    dt = jnp.exp(ldt)
    mag = jnp.exp(dt * ar)
    are = mag * jnp.cos(dt * ai)
    aim = mag * jnp.sin(dt * ai)
    nr = are - 1.0
    ni = aim
    den = ar * ar + ai * ai
    fr = (nr * ar + ni * ai) / den
    fi = (ni * ar - nr * ai) / den
    return are, aim, fr * brt - fi * bit, fr * bit + fi * brt


def _prep_fwd(name, ar, ai, ldt, brt, bit):
    def body(ar_ref, ai_ref, l_ref, br_ref, bi_ref, o1, o2, o3, o4):
        o1[...], o2[...], o3[...], o4[...] = _prep_fn(ar_ref[...], ai_ref[...], l_ref[...], br_ref[...], bi_ref[...])

    return pl.pallas_call(body, name=name,
                          out_shape=[_sds(ar.shape, F32), _sds(ar.shape, F32), _sds(brt.shape, F32), _sds(brt.shape, F32)],
                          )(ar, ai, ldt, brt, bit)


def _prep_bwd(name, ar, ai, ldt, brt, bit, g1, g2, g3, g4):
    def body(ar_ref, ai_ref, l_ref, br_ref, bi_ref, g1_ref, g2_ref, g3_ref, g4_ref, o1, o2, o3, o4, o5):
        _, vjp = jax.vjp(_prep_fn, ar_ref[...], ai_ref[...], l_ref[...], br_ref[...], bi_ref[...])
        o1[...], o2[...], o3[...], o4[...], o5[...] = vjp((g1_ref[...], g2_ref[...], g3_ref[...], g4_ref[...]))

    return pl.pallas_call(body, name=name,
                          out_shape=[_sds(ar.shape, F32)] * 3 + [_sds(brt.shape, F32)] * 2,
                          )(ar, ai, ldt, brt, bit, g1, g2, g3, g4)


def _scan_steps(tc, pitch, ng, reverse, a_r, a_i, stage_b, stage_x, out_re, out_im, st_re, st_im, acc):
    def step(tt, carry):
        t = (tc - 1 - tt) if reverse else tt
        new, sums = [], []
        for g in range(ng):
            rows = pl.ds(g * 8 * pitch + t, 8, stride=pitch)
            cr, ci = carry[2 * g], carry[2 * g + 1]
            br, bi = stage_b[0][rows, :], stage_b[1][rows, :]
            if reverse:
                xr, xi = stage_x[0][rows, :], stage_x[1][rows, :]
                sums += [carry[2 * ng + 2 * g] + (xr * cr + xi * ci), carry[2 * ng + 2 * g + 1] + (xr * ci - xi * cr)]
                nr = a_r[g] * cr + a_i[g] * ci + br
                ni = a_r[g] * ci - a_i[g] * cr + bi
            else:
                nr = a_r[g] * cr - a_i[g] * ci + br
                ni = a_r[g] * ci + a_i[g] * cr + bi
            out_re[rows, :] = nr
            out_im[rows, :] = ni
            new += [nr, ni]
        return tuple(new + sums)

    init = []
    for g in range(ng):
        init += [st_re[g], st_im[g]]
    if reverse:
        for g in range(ng):
            init += [acc[0][g], acc[1][g]]
    fin = lax.fori_loop(0, tc, step, tuple(init), unroll=2)
    for g in range(ng):
        st_re[g] = fin[2 * g]
        st_im[g] = fin[2 * g + 1]
        if reverse:
            acc[0][g] = fin[2 * ng + 2 * g]
            acc[1][g] = fin[2 * ng + 2 * g + 1]


def _s5_fwd(name, proj, bm_re, bm_im, cm_re, cm_im, dskip, a_re, a_im):
    L = proj.shape[0]
    nb = bm_re.shape[0]
    ns, W = SLAB * nb, nb * LANES
    ng = ns // 8
    tc = min(2 * LANES, L)
    pitch = tc + 8
    wide = SLAB * LANES

    def body(u_ref, bre_ref, bim_ref, cre_ref, cim_ref, d_ref, ar_ref, ai_ref, xr_ref, xi_ref, y_ref, ya_ref,
             sb_re, sb_im, out_re, out_im, st_re, st_im):
        @pl.when(pl.program_id(0) == 0)
        def _():
            st_re[...] = jnp.zeros(st_re.shape, F32)
            st_im[...] = jnp.zeros(st_im.shape, F32)

        for j in range(nb):
            ub = u_ref[:, j * LANES:(j + 1) * LANES]
            r1 = jnp.dot(ub, bre_ref[j], preferred_element_type=F32)
            r2 = jnp.dot(ub, bim_ref[j], preferred_element_type=F32)
            for q in range(SLAB):
                sb_re[pl.ds((SLAB * j + q) * pitch, tc), :] = r1[:, q * LANES:(q + 1) * LANES]
                sb_im[pl.ds((SLAB * j + q) * pitch, tc), :] = r2[:, q * LANES:(q + 1) * LANES]
        a_r = [ar_ref[g] for g in range(ng)]
        a_i = [ai_ref[g] for g in range(ng)]
        _scan_steps(tc, pitch, ng, False, a_r, a_i, (sb_re, sb_im), None, out_re, out_im, st_re, st_im, None)
        for j in range(nb):
            x1 = [out_re[pl.ds((SLAB * j + q) * pitch, tc), :].astype(BF16) for q in range(SLAB)]
            x2 = [out_im[pl.ds((SLAB * j + q) * pitch, tc), :].astype(BF16) for q in range(SLAB)]
            for q in range(SLAB):
                xr_ref[SLAB * j + q] = x1[q]
                xi_ref[SLAB * j + q] = x2[q]
            cols = slice(j * LANES, (j + 1) * LANES)
            y = (jnp.dot(jnp.concatenate(x1, axis=1), cre_ref[j], preferred_element_type=F32)
                 + jnp.dot(jnp.concatenate(x2, axis=1), cim_ref[j], preferred_element_type=F32)
                 + d_ref[:, cols] * u_ref[:, cols].astype(F32))
            y_ref[:, cols] = y
            ya_ref[:, cols] = jax.nn.gelu(y).astype(BF16)

    full3 = lambda s: pl.BlockSpec(s, lambda i: (0, 0, 0))
    xs = pl.BlockSpec((ns, tc, LANES), lambda i: (0, i, 0))
    rows = pl.BlockSpec((tc, W), lambda i: (i, 0))
    return pl.pallas_call(
        body, name=name, grid=(L // tc,),
        in_specs=[rows, full3((nb, LANES, wide)), full3((nb, LANES, wide)), full3((nb, wide, LANES)),
                  full3((nb, wide, LANES)), pl.BlockSpec((1, W), lambda i: (0, 0)), full3((ng, 8, LANES)), full3((ng, 8, LANES))],
        out_specs=[xs, xs, rows, rows],
        out_shape=[_sds((ns, L, LANES), BF16)] * 2 + [_sds((L, W), F32), _sds((L, W), BF16)],
        scratch_shapes=[pltpu.VMEM((ns * pitch, LANES), F32)] * 4 + [pltpu.VMEM((ng, 8, LANES), F32)] * 2,
        compiler_params=_params(1),
    )(proj, bm_re, bm_im, cm_re, cm_im, dskip, a_re, a_im)


def _s5_bwd(name, dyb, proj, xs_re, xs_im, cmt_re, cmt_im, bmt_re, bmt_im, dskip, a_re, a_im):
    L = dyb.shape[0]
    nb = cmt_re.shape[0]
    ns, W = SLAB * nb, nb * LANES
    ng = ns // 8
    tc = min(LANES, L)
    pitch = tc + 8
    nt = L // tc
    wide = SLAB * LANES
    dn = (((0,), (0,)), ((), ()))

    def body(dy_ref, u_ref, xr_ref, xi_ref, cre_ref, cim_ref, bre_ref, bim_ref, d_ref, ar_ref, ai_ref,
             du_ref, gbr_ref, gbi_ref, gcr_ref, gci_ref, dar_ref, dai_ref,
             sd_re, sd_im, sx_re, sx_im, out_re, out_im, st_re, st_im, acc_re, acc_im):
        first = pl.program_id(0) == 0

        @pl.when(first)
        def _():
            for r in (st_re, st_im, acc_re, acc_im):
                r[...] = jnp.zeros(r.shape, F32)
            for r in (gbr_ref, gbi_ref, gcr_ref, gci_ref):
                r[...] = jnp.zeros(r.shape, F32)

        for j in range(nb):
            dyj = dy_ref[:, j * LANES:(j + 1) * LANES]
            r1 = jnp.dot(dyj, cre_ref[j], preferred_element_type=F32)
            r2 = jnp.dot(dyj, cim_ref[j], preferred_element_type=F32)
            for q in range(SLAB):
                s = SLAB * j + q
                sd_re[pl.ds(s * pitch, tc), :] = r1[:, q * LANES:(q + 1) * LANES]
                sd_im[pl.ds(s * pitch, tc), :] = r2[:, q * LANES:(q + 1) * LANES]
                sx_re[pl.ds(s * pitch, tc), :] = xr_ref[s].astype(F32)
                sx_im[pl.ds(s * pitch, tc), :] = xi_ref[s].astype(F32)
        a_r = [ar_ref[g] for g in range(ng)]
        a_i = [ai_ref[g] for g in range(ng)]
        _scan_steps(tc, pitch, ng, True, a_r, a_i, (sd_re, sd_im), (sx_re, sx_im), out_re, out_im, st_re, st_im,
                    (acc_re, acc_im))
        for j in range(nb):
            cols = slice(j * LANES, (j + 1) * LANES)
            l1 = jnp.concatenate([out_re[pl.ds((SLAB * j + q) * pitch, tc), :] for q in range(SLAB)], axis=1).astype(BF16)
            l2 = jnp.concatenate([out_im[pl.ds((SLAB * j + q) * pitch, tc), :] for q in range(SLAB)], axis=1).astype(BF16)
            dyj = dy_ref[:, cols]
            du = (jnp.dot(l1, bre_ref[j], preferred_element_type=F32) + jnp.dot(l2, bim_ref[j], preferred_element_type=F32)
                  + d_ref[:, cols] * dyj.astype(F32))
            du_ref[:, cols] = du.astype(BF16)
            uj = u_ref[:, cols]
            gbr_ref[j] += lax.dot_general(uj, l1, dn, preferred_element_type=F32)
            gbi_ref[j] += lax.dot_general(uj, l2, dn, preferred_element_type=F32)
            x1 = jnp.concatenate([xr_ref[SLAB * j + q] for q in range(SLAB)], axis=1)
            x2 = jnp.concatenate([xi_ref[SLAB * j + q] for q in range(SLAB)], axis=1)
            gcr_ref[j] += lax.dot_general(dyj, x1, dn, preferred_element_type=F32)
            gci_ref[j] += lax.dot_general(dyj, x2, dn, preferred_element_type=F32)
        dar_ref[...] = acc_re[...]
        dai_ref[...] = acc_im[...]

    full3 = lambda s: pl.BlockSpec(s, lambda i: (0, 0, 0))
    xs = pl.BlockSpec((ns, tc, LANES), lambda i: (0, nt - 1 - i, 0))
    rows = pl.BlockSpec((tc, W), lambda i: (nt - 1 - i, 0))
    mat_a, mat_b = full3((nb, LANES, wide)), full3((nb, wide, LANES))
    vec = full3((ng, 8, LANES))
    return pl.pallas_call(
        body, name=name, grid=(nt,),
        in_specs=[rows, rows, xs, xs, mat_a, mat_a, mat_b, mat_b, pl.BlockSpec((1, W), lambda i: (0, 0)), vec, vec],
        out_specs=[rows, mat_a, mat_a, mat_a, mat_a, vec, vec],
        out_shape=[_sds((L, W), BF16)] + [_sds((nb, LANES, wide), F32)] * 4 + [_sds((ng, 8, LANES), F32)] * 2,
        scratch_shapes=[pltpu.VMEM((ns * pitch, LANES), F32)] * 6 + [pltpu.VMEM((ng, 8, LANES), F32)] * 4,
        compiler_params=_params(1),
    )(dyb, proj, xs_re, xs_im, cmt_re, cmt_im, bmt_re, bmt_im, dskip, a_re, a_im)


def _peer(k):
    x, y, c = lax.axis_index("x"), lax.axis_index("y"), lax.axis_index("c")
    px = 1 - x if (k >> 2) & 1 else x
    py = 1 - y if (k >> 1) & 1 else y
    pc = 1 - c if k & 1 else c
    return (px, py, pc), 4 * px + 2 * py + pc


def _window(ref, kind, idx, n):
    if kind == "col":
        w = ref.shape[1] // n
        return ref.at[:, pl.ds(pl.multiple_of(idx * w, LANES), w)]
    r = ref.shape[0] // n
    return ref.at[pl.ds(pl.multiple_of(idx * r, 8), r), :]


def _all_gather(name, shards, kinds):
    n = len(shards)
    fulls = []
    for s, kind in zip(shards, kinds):
        fulls.append(_sds((s.shape[0], s.shape[1] * N_DEV) if kind == "col" else (s.shape[0] * N_DEV, s.shape[1]), s.dtype))

    def body(*refs):
        src, dst = refs[:n], refs[n:2 * n]
        send, recv, loc = refs[2 * n:]
        me = 4 * lax.axis_index("x") + 2 * lax.axis_index("y") + lax.axis_index("c")
        copies = []
        for a in range(n):
            own = pltpu.make_async_copy(src[a], _window(dst[a], kinds[a], me, N_DEV), loc.at[a])
            own.start()
            copies.append(own)
        sends = []
        for k in range(1, N_DEV):
            dev, _ = _peer(k)
            for a in range(n):
                cp = pltpu.make_async_remote_copy(
                    src_ref=src[a], dst_ref=_window(dst[a], kinds[a], me, N_DEV),
                    send_sem=send.at[a * N_DEV + k], recv_sem=recv.at[a * N_DEV + k],
                    device_id=dev, device_id_type=MESH)
                cp.start()
                sends.append(cp)
        for k in range(1, N_DEV):
            dev, pidx = _peer(k)
            for a in range(n):
                pltpu.make_async_remote_copy(
                    src_ref=src[a], dst_ref=_window(dst[a], kinds[a], pidx, N_DEV),
                    send_sem=send.at[a * N_DEV + k], recv_sem=recv.at[a * N_DEV + k],
                    device_id=dev, device_id_type=MESH).wait_recv()
        for cp in sends:
            cp.wait_send()
        for cp in copies:
            cp.wait()

    any_ = pl.BlockSpec(memory_space=pl.ANY)
    return pl.pallas_call(
        body, name=name, in_specs=[any_] * n, out_specs=[any_] * n, out_shape=fulls,
        scratch_shapes=[pltpu.SemaphoreType.DMA((n * N_DEV,)), pltpu.SemaphoreType.DMA((n * N_DEV,)),
                        pltpu.SemaphoreType.DMA((n,))],
        compiler_params=pltpu.CompilerParams(has_side_effects=True),
    )(*shards)


def _xfer_refs(mode, kinds, a, src, dst, me, pidx):
    if mode == "gather":
        return src[a], _window(dst[a], kinds[a], me, N_DEV), _window(dst[a], kinds[a], pidx, N_DEV)
    return _window(src[a], kinds[a], pidx, N_DEV), dst[a].at[me], dst[a].at[pidx]


def _xfer_out_shapes(mode, arrs, kinds):
    outs = []
    for s, kind in zip(arrs, kinds):
        if mode == "gather":
            outs.append((s.shape[0], s.shape[1] * N_DEV) if kind == "col" else (s.shape[0] * N_DEV, s.shape[1]))
        else:
            outs.append((N_DEV,) + ((s.shape[0], s.shape[1] // N_DEV) if kind == "col" else (s.shape[0] // N_DEV, s.shape[1])))
    return outs


def _sc_xfer(name, mode, arrs, kinds, collective_id):
    n = len(arrs)
    shapes = _xfer_out_shapes(mode, arrs, kinds)
    hbm = pltpu.MemorySpace.HBM
    src = [jax.new_ref(a, memory_space=hbm) for a in arrs]
    dst = [jax.empty_ref(_sds(shp, a.dtype), memory_space=hbm) for shp, a in zip(shapes, arrs)]

    @pl.kernel(mesh=plsc.ScalarSubcoreMesh(axis_name="seq", num_cores=1), name=name,
               scratch_types=(pltpu.SemaphoreType.DMA((n * N_DEV,)), pltpu.SemaphoreType.DMA((n * N_DEV,)),
                              pltpu.SemaphoreType.DMA((n,))),
               compiler_params=pltpu.CompilerParams(collective_id=collective_id))
    def launch(send, recv, loc):
        barrier = pltpu.get_barrier_semaphore()
        for k in range(1, N_DEV):
            pl.semaphore_signal(barrier, inc=1, device_id=_peer(k)[0], device_id_type=MESH)
        pl.semaphore_wait(barrier, N_DEV - 1)
        me = 4 * lax.axis_index("x") + 2 * lax.axis_index("y") + lax.axis_index("c")
        own, sends = [], []
        for a in range(n):
            s, _, d = _xfer_refs(mode, kinds, a, src, dst, me, me)
            own.append(pltpu.make_async_copy(s, d, loc.at[a]))
            own[-1].start()
        for k in range(1, N_DEV):
            dev, pidx = _peer(k)
            for a in range(n):
                s, d, _ = _xfer_refs(mode, kinds, a, src, dst, me, pidx)
                sends.append(pltpu.make_async_remote_copy(src_ref=s, dst_ref=d, send_sem=send.at[a * N_DEV + k],
                                                          recv_sem=recv.at[a * N_DEV + k], device_id=dev, device_id_type=MESH))
                sends[-1].start()
        for cp in own:
            cp.wait()
        for k in range(1, N_DEV):
            dev, pidx = _peer(k)
            for a in range(n):
                s, _, land = _xfer_refs(mode, kinds, a, src, dst, me, pidx)
                pltpu.make_async_remote_copy(src_ref=s, dst_ref=land, send_sem=send.at[a * N_DEV + k],
                                             recv_sem=recv.at[a * N_DEV + k], device_id=dev, device_id_type=MESH).wait_recv()
        for cp in sends:
            cp.wait_send()

    launch()
    return [d[...] for d in dst]


def _sc_gather(name, arrs, kinds, collective_id):
    n = len(arrs)
    pairs = 7
    shapes = _xfer_out_shapes("gather", arrs, kinds)
    hbm = pltpu.MemorySpace.HBM
    src = [jax.new_ref(a, memory_space=hbm) for a in arrs]
    dst = [jax.empty_ref(_sds(shp, a.dtype), memory_space=hbm) for shp, a in zip(shapes, arrs)]

    @pl.kernel(mesh=plsc.ScalarSubcoreMesh(axis_name="seq", num_cores=1), name=name,
               scratch_types=(pltpu.SemaphoreType.DMA((n * pairs,)), pltpu.SemaphoreType.DMA((n * pairs,)),
                              pltpu.SemaphoreType.DMA((n,))),
               compiler_params=pltpu.CompilerParams(collective_id=collective_id))
    def launch(send, recv, loc):
        x, y, c = lax.axis_index("x"), lax.axis_index("y"), lax.axis_index("c")
        me = 4 * x + 2 * y + c
        sib = (x, y, 1 - c)
        chips = []
        for fx, fy in ((1, 0), (0, 1), (1, 1)):
            px, py = (1 - x if fx else x), (1 - y if fy else y)
            chips.append(((px, py, c), 4 * px + 2 * py + c, 4 * px + 2 * py + (1 - c)))
        barrier = pltpu.get_barrier_semaphore()
        for dev in [sib] + [ch[0] for ch in chips]:
            pl.semaphore_signal(barrier, inc=1, device_id=dev, device_id_type=MESH)
        pl.semaphore_wait(barrier, 4)

        def win(a, idx):
            return _window(dst[a], kinds[a], idx, N_DEV)

        def rcopy(a, p, s, d, dev):
            return pltpu.make_async_remote_copy(src_ref=s, dst_ref=d, send_sem=send.at[a * pairs + p],
                                                recv_sem=recv.at[a * pairs + p], device_id=dev, device_id_type=MESH)

        own, sends = [], []
        for a in range(n):
            own.append(pltpu.make_async_copy(src[a], win(a, me), loc.at[a]))
            own[-1].start()
        for j, (dev, _, _) in enumerate(chips):
            for a in range(n):
                sends.append(rcopy(a, 1 + j, src[a], win(a, me), dev))
                sends[-1].start()
        for a in range(n):
            sends.append(rcopy(a, 0, src[a], win(a, me), sib))
            sends[-1].start()
        for j, (dev, idx, _) in enumerate(chips):
            for a in range(n):
                rcopy(a, 1 + j, src[a], win(a, idx), dev).wait_recv()
                sends.append(rcopy(a, 4 + j, win(a, idx), win(a, idx), sib))
                sends[-1].start()
        for cp in own:
            cp.wait()
        for a in range(n):
            rcopy(a, 0, src[a], win(a, 4 * x + 2 * y + (1 - c)), sib).wait_recv()
        for j, (_, _, sidx) in enumerate(chips):
            for a in range(n):
                rcopy(a, 4 + j, src[a], win(a, sidx), sib).wait_recv()
        for cp in sends:
            cp.wait_send()

    launch()
    return [d[...] for d in dst]


_SEQ_IDS = {"gather_in": 7, "gather_mix": 1, "gather_ffn": 2, "grads_down": 3, "grads_up": 8, "grads_mix": 4,
            "grads_small": 5, "grads_in": 6}


def _launch(name, mode, arrs, kinds):
    if mode == "gather":
        return _sc_gather(name, list(arrs), kinds, _SEQ_IDS[name])
    return _sc_xfer(name, mode, list(arrs), kinds, _SEQ_IDS[name])


def _adamw(name, parts, w, m, v):
    P, R, C = parts.shape
    sub = 16 if parts.dtype == BF16 else 8
    tr = R if R * C <= (1 << 18) else _tile(R, max(sub, (1 << 18) // C), sub)

    def body(p_ref, w_ref, m_ref, v_ref, g_ref, d_ref, nm_ref, nv_ref):
        g = p_ref[0].astype(F32)
        for s in range(1, P):
            g = g + p_ref[s].astype(F32)
        m2 = ADAM_B1 * m_ref[...] + (1.0 - ADAM_B1) * g
        v2 = ADAM_B2 * v_ref[...] + (1.0 - ADAM_B2) * (g * g)
        m_hat = m2 / (1.0 - ADAM_B1 ** ADAM_STEP)
        v_hat = v2 / (1.0 - ADAM_B2 ** ADAM_STEP)
        g_ref[...] = g
        d_ref[...] = -ADAM_LR * (m_hat / (jnp.sqrt(v_hat) + ADAM_EPS) + ADAM_WD * w_ref[...])
        nm_ref[...] = m2
        nv_ref[...] = v2

    sp = pl.BlockSpec((tr, C), lambda i: (i, 0))
    return pl.pallas_call(
        body, name=name, grid=(R // tr,),
        in_specs=[pl.BlockSpec((P, tr, C), lambda i: (0, i, 0)), sp, sp, sp], out_specs=[sp] * 4,
        out_shape=[_sds((R, C), F32)] * 4, compiler_params=_params(1),
    )(parts, w, m, v)


def _pack(arrs, row_mult=8):
    pieces, total = [], 0
    for a in arrs:
        f = a.reshape(-1).astype(F32)
        pad = (-f.shape[0]) % (8 * LANES)
        pieces.append(jnp.pad(f, (0, pad)) if pad else f)
        total += f.shape[0] + pad
    tail = (-total) % (row_mult * LANES)
    if tail:
        pieces.append(jnp.zeros((tail,), F32))
    return jnp.concatenate(pieces).reshape(-1, LANES)


def _unpack(buf, shapes, lead=()):
    out, row = [], 0
    for shp in shapes:
        size = 1
        for d in shp:
            size *= d
        rows = -(-size // (8 * LANES)) * 8
        piece = buf[..., row:row + rows, :].reshape(lead + (rows * LANES,))[..., :size]
        out.append(piece.reshape(lead + tuple(shp)))
        row += rows
    return out


def kernel(x, norm_tok, w_in, a_re, a_im, log_dt, b_re, b_im, c_re, c_im, d_skip, w_glu, w_ssm_out, conv_w, conv_b, w_conv_out, w_o, norm_ffn, w_up, ffn_conv_w, ffn_conv_b, w_down, norm_final, loss_target, m_norm_tok, m_w_in, m_a_re, m_a_im, m_log_dt, m_b_re, m_b_im, m_c_re, m_c_im, m_d_skip, m_w_glu, m_w_ssm_out, m_conv_w, m_conv_b, m_w_conv_out, m_w_o, m_norm_ffn, m_w_up, m_ffn_conv_w, m_ffn_conv_b, m_w_down, m_norm_final, v_norm_tok, v_w_in, v_a_re, v_a_im, v_log_dt, v_b_re, v_b_im, v_c_re, v_c_im, v_d_skip, v_w_glu, v_w_ssm_out, v_conv_w, v_conv_b, v_w_conv_out, v_w_o, v_norm_ffn, v_w_up, v_ffn_conv_w, v_ffn_conv_b, v_w_down, v_norm_final):
    args = dict(locals())
    L, D = x.shape[1], x.shape[2]
    G, P, H = b_re.shape[1], b_re.shape[2], b_re.shape[3]
    SW = G * H
    CW = conv_b.shape[1]
    FF = ffn_conv_b.shape[1]
    GP = G * P
    nb = SW // LANES
    gpb = LANES // H
    me = 4 * lax.axis_index("x") + 2 * lax.axis_index("y") + lax.axis_index("c")
    tm = _tile(L, 256, 16)
    x2 = x[0]
    tgt = loss_target[0]

    big = [("w_in", "col"), ("w_glu", "row"), ("w_ssm_out", "col"), ("w_conv_out", "col"), ("w_o", "row"),
           ("w_up", "col"), ("w_down", "row")]
    shards = [_cast_bf16("cast_" + n, args[n][0]) for n, _ in big]
    small_in = _pack([conv_w[0], ffn_conv_w[0]])
    kind = dict(big)
    mixw, ffnw = ["w_glu", "w_ssm_out", "w_conv_out", "w_o"], ["w_up", "w_down"]
    shard = dict(zip([n for n, _ in big], shards))
    gathered = _launch("gather_in", "gather", [shard["w_in"], small_in], ["col", "row"])
    W = {"w_in": gathered[0]}
    W.update(zip(mixw, _launch("gather_mix", "gather", [shard[n] for n in mixw], [kind[n] for n in mixw])))
    W.update(zip(ffnw, _launch("gather_ffn", "gather", [shard[n] for n in ffnw], [kind[n] for n in ffnw])))
    cw_parts, fcw_parts = _unpack(gathered[-1].reshape(N_DEV, -1, LANES), [conv_w.shape[1:], ffn_conv_w.shape[1:]], (N_DEV,))
    conv_w_full = jnp.moveaxis(cw_parts, 0, 1).reshape(3, CW)
    ffn_conv_w_full = jnp.moveaxis(fcw_parts, 0, 1).reshape(3, FF)

    ar_row, ai_row = a_re.reshape(1, GP), a_im.reshape(1, GP)
    ldt_row = jnp.broadcast_to(log_dt.reshape(G, 1), (G, P)).reshape(1, GP)
    brt = jnp.transpose(b_re[0], (2, 0, 1)).reshape(H, GP)
    bit = jnp.transpose(b_im[0], (2, 0, 1)).reshape(H, GP)
    abar_re, abar_im, bbar_re, bbar_im = _prep_fwd("s5_prep", ar_row, ai_row, ldt_row, brt, bit)
    eye = jnp.eye(gpb, dtype=F32)

    def b_blocks(bt):
        return jnp.einsum("ab,hjbp->jahbp", eye, bt.reshape(H, nb, gpb, P)).reshape(nb, LANES, gpb * P)

    def c_blocks(c):
        return jnp.einsum("ab,jahp->jbpah", eye, c.reshape(nb, gpb, H, P)).reshape(nb, gpb * P, LANES)

    def diag_blocks(mat):
        return jnp.einsum("jahap->hjap", mat.reshape(nb, gpb, H, gpb, P))

    bm_re, bm_im = b_blocks(bbar_re), b_blocks(bbar_im)
    cm_re, cm_im = c_blocks(c_re[0]), -c_blocks(c_im[0])
    a3_re, a3_im = abar_re.reshape(-1, 8, LANES), abar_im.reshape(-1, 8, LANES)
    dskip_row = d_skip.reshape(1, SW)

    cbs = SW // LANES
    cb_v, cb_gb, cb_gc = cbs, cbs + CW // LANES, cbs + 2 * CW // LANES
    cb_ma = (SW + 3 * CW) // D
    xn = _rms_fwd("rms_tok", x2, norm_tok, tm)
    proj = _mm("proj", xn, W["w_in"], "nn", out_dtype=BF16)
    xs_re, xs_im, y, ya = _s5_fwd("s5_fwd", proj, bm_re.astype(BF16), bm_im.astype(BF16), cm_re.astype(BF16),
                                  cm_im.astype(BF16), dskip_row, a3_re, a3_im)
    g1 = _mm("glu_gate", ya, W["w_glu"], "nn", out_dtype=BF16)
    ya2 = _glu_fwd("glu", y, g1, tm)
    za = _mm("ssm_out", ya2, W["w_ssm_out"], "nn", out_dtype=BF16)
    q = _convb_fwd("convb", proj, cb_v, cb_gb, cb_gc, conv_w_full, conv_b)
    zb = _mm("conv_out", q, W["w_conv_out"], "nn", out_dtype=BF16)
    merged = _merge_fwd("merge", proj, cb_ma, cb_ma + 1, za, zb, tm)
    h1, hn = _mm_res_rms("mix_out_rms", merged, W["w_o"], x2, norm_ffn)
    hh, f = _ffn_up_act("ffn_up_act", hn, W["w_up"], ffn_conv_w_full, ffn_conv_b)
    dh2, dh2b, g_norm_final, loss_part = _mm_final("ffn_down_final", f, W["w_down"], h1, norm_final.reshape(1, D), tgt,
                                                   _tile(FF, 512))

    gw_down = _mm("gw_down", f, dh2b, "tn", out_dtype=BF16, tm=1408, tn=512, tk=L)
    dh2b, gw_down = lax.optimization_barrier((dh2b, gw_down))
    parts = {"w_down": _launch("grads_down", "exchange", [gw_down], [kind["w_down"]])[0]}
    dhh, g_ffn_conv_w, g_ffn_conv_b = _ffn_down_bwd_act("ffn_down_bwd_act", dh2b, W["w_down"], hh, ffn_conv_w_full,
                                                         ffn_conv_b)
    nhalf = lambda t: FF // t
    gw_up = _mm("gw_up", hn, dhh, "tn", out_dtype=BF16, tn=_tile(FF, 1024), tk=L, dims=(D, 2 * FF, L),
                b_spec=lambda a, b, c: pl.BlockSpec((None, c, b), lambda i, j, k: (j // nhalf(b), k, j % nhalf(b))))
    dhh, gw_up = lax.optimization_barrier((dhh, gw_up))
    parts["w_up"] = _launch("grads_up", "exchange", [gw_up], [kind["w_up"]])[0]
    dhn = _mm("d_ffn_in", dhh, W["w_up"], "nt", out_dtype=BF16, tk=_tile(FF, 2816), dims=(L, D, 2 * FF),
              a_spec=lambda a, b, c: pl.BlockSpec((None, a, c), lambda i, j, k: (k // nhalf(c), i, k % nhalf(c))))
    dh1, dh1b, g_norm_ffn = _rms_bwd("rms_ffn_bwd", dhn, h1, norm_ffn, dh2, tm, True)

    dmerged = _mm("d_merged", dh1b, W["w_o"], "nt", out_dtype=BF16)
    gw_o = _mm("gw_o", merged, dh1b, "tn", out_dtype=BF16, tk=L)
    dmerged, gw_o = lax.optimization_barrier((dmerged, gw_o))
    dza, dzb, dma, dmb = _merge_bwd("merge_bwd", proj, cb_ma, cb_ma + 1, za, zb, dmerged, tm)
    dq = _mm("d_q", dzb, W["w_conv_out"], "nt", out_dtype=BF16)
    gw_conv_out = _mm("gw_conv_out", q, dzb, "tn", out_dtype=BF16, tk=L)
    dq, gw_conv_out = lax.optimization_barrier((dq, gw_conv_out))
    dv, dgb, dgc, g_conv_w, g_conv_b = _convb_bwd("convb_bwd", proj, cb_v, cb_gb, cb_gc, conv_w_full, conv_b, dq)
    dya2 = _mm("d_ya2", dza, W["w_ssm_out"], "nt", out_dtype=BF16)
    gw_ssm_out = _mm("gw_ssm_out", ya2, dza, "tn", out_dtype=BF16, tk=L)
    dya2, gw_ssm_out = lax.optimization_barrier((dya2, gw_ssm_out))
    dy_direct, dg1 = _glu_bwd("glu_bwd", y, g1, dya2, tm)
    dya_g = _mm("d_ya_gate", dg1, W["w_glu"], "nt", out_dtype=BF16)
    gw_glu = _mm("gw_glu", ya, dg1, "tn", out_dtype=BF16, tk=L)
    dya_g, gw_glu = lax.optimization_barrier((dya_g, gw_glu))
    parts_mix = _launch("grads_mix", "exchange", [gw_glu, gw_ssm_out, gw_conv_out, gw_o], [kind[n] for n in mixw])
    dyb, g_dskip = _gelu_bwd("gelu_bwd", y, dy_direct, dya_g, proj, tm)
    swap = lambda m: jnp.swapaxes(m, 1, 2).astype(BF16)
    du, gb_re, gb_im, gc_re, gc_im, dab_re, dab_im = _s5_bwd(
        "s5_bwd", dyb, proj, xs_re, xs_im, swap(cm_re), swap(cm_im), swap(bm_re), swap(bm_im), dskip_row, a3_re, a3_im)
    g_ar, g_ai, g_ldt, g_brt, g_bit = _prep_bwd(
        "s5_prep_bwd", ar_row, ai_row, ldt_row, brt, bit, dab_re.reshape(1, GP), dab_im.reshape(1, GP),
        diag_blocks(gb_re).reshape(H, GP), diag_blocks(gb_im).reshape(H, GP))
    small = dict(
        a_re=g_ar.reshape(1, G, P), a_im=g_ai.reshape(1, G, P),
        log_dt=g_ldt.reshape(G, P).sum(axis=1).reshape(1, G),
        b_re=jnp.transpose(g_brt.reshape(H, G, P), (1, 2, 0))[None], b_im=jnp.transpose(g_bit.reshape(H, G, P), (1, 2, 0))[None],
        c_re=jnp.transpose(diag_blocks(gc_re), (1, 2, 0, 3)).reshape(1, G, H, P),
        c_im=-jnp.transpose(diag_blocks(gc_im), (1, 2, 0, 3)).reshape(1, G, H, P),
        d_skip=g_dskip.reshape(1, G, H), conv_b=g_conv_b, norm_ffn=g_norm_ffn, ffn_conv_b=g_ffn_conv_b,
        norm_final=g_norm_final.reshape(D), conv_w=g_conv_w[None], ffn_conv_w=g_ffn_conv_w[None])
    rep = ["a_re", "a_im", "log_dt", "b_re", "b_im", "c_re", "c_im", "d_skip", "conv_b", "norm_ffn", "ffn_conv_b", "norm_final"]
    order = rep + ["conv_w", "ffn_conv_w"]
    full_shapes = {n: args[n].shape for n in rep}
    full_shapes["conv_w"], full_shapes["ffn_conv_w"] = (1, 3, CW), (1, 3, FF)
    rep_pack = _pack([small[n] for n in rep], LANES)
    rep_rows = rep_pack.shape[0]
    gpack = jnp.concatenate([loss_part, rep_pack, _pack([small["conv_w"], small["ffn_conv_w"]])], axis=0)
    rep0 = loss_part.shape[0]
    rows = gpack.shape[0]
    du, gpack = lax.optimization_barrier((du, gpack))
    gall = _launch("grads_small", "gather", [gpack], ["row"])[0]

    dproj = _concat_cols("dproj", [du, dv, dgb, dgc, dma, dmb], tm)
    gw_in = _mm("gw_in", xn, dproj, "tn", out_dtype=BF16, tk=L)
    dproj, gw_in = lax.optimization_barrier((dproj, gw_in))
    parts_in = _launch("grads_in", "exchange", [gw_in], ["col"])
    dxn = _mm("d_xn", dproj, W["w_in"], "nt", out_dtype=BF16, tk=4096)
    grad_x, g_norm_tok = _rms_bwd("rms_tok_bwd", dxn, x2, norm_tok, dh1, tm, False)

    res = {}

    def big_update(n):
        res[n] = [r[None] for r in _adamw("adamw_" + n, parts[n], args[n][0], args["m_" + n][0], args["v_" + n][0])]

    def after(xs, dep):
        return lax.optimization_barrier((list(xs), dep))[0]

    parts["w_down"] = after([parts["w_down"]], grad_x)[0]
    big_update("w_down")
    parts["w_up"] = after([parts["w_up"]], res["w_down"][1])[0]
    big_update("w_up")
    parts.update(zip(mixw, after(parts_mix, [res[n][1] for n in ffnw])))
    for n in mixw:
        big_update(n)
    gall = after([gall], [res[n][1] for n in mixw])[0].reshape(N_DEV, rows, LANES)
    gcw, gfcw = _unpack(gall[:, rep0 + rep_rows:], [full_shapes["conv_w"], full_shapes["ffn_conv_w"]], (N_DEV,))
    cws, fcws = CW // N_DEV, FF // N_DEV
    gcw = lax.dynamic_slice_in_dim(gcw[:, 0], me * cws, cws, axis=2)
    gfcw = lax.dynamic_slice_in_dim(gfcw[:, 0], me * fcws, fcws, axis=2)
    res["conv_w"] = [r[None] for r in _adamw("adamw_conv_w", gcw, conv_w[0], m_conv_w[0], v_conv_w[0])]
    res["ffn_conv_w"] = [r[None] for r in _adamw("adamw_ffn_conv_w", gfcw, ffn_conv_w[0], m_ffn_conv_w[0], v_ffn_conv_w[0])]
    rep_out = _adamw("adamw_small", gall[:, rep0:rep0 + rep_rows], _pack([args[n] for n in rep], LANES),
                     _pack([args["m_" + n] for n in rep], LANES), _pack([args["v_" + n] for n in rep], LANES))
    rep_out = [_unpack(r, [full_shapes[n] for n in rep]) for r in rep_out]
    for i, n in enumerate(rep):
        res[n] = [r[i] for r in rep_out]
    nt_pack = after([_pack([g_norm_tok])], [res[n][1] for n in ("a_re", "conv_w", "ffn_conv_w")])
    nt_all = _all_gather("gather_norm_tok_grad", nt_pack, ["row"])[0].reshape(N_DEV, -1, LANES)
    nt_out = _adamw("adamw_norm_tok", nt_all, _pack([norm_tok]), _pack([m_norm_tok]), _pack([v_norm_tok]))
    res["norm_tok"] = [_unpack(r, [norm_tok.shape])[0] for r in nt_out]
    parts["w_in"] = after(parts_in, nt_out[0])[0]
    big_update("w_in")

    loss = jnp.sum(gall[:, 0, 0])
    names = ["norm_tok", "w_in", "a_re", "a_im", "log_dt", "b_re", "b_im", "c_re", "c_im", "d_skip", "w_glu", "w_ssm_out",
             "conv_w", "conv_b", "w_conv_out", "w_o", "norm_ffn", "w_up", "ffn_conv_w", "ffn_conv_b", "w_down", "norm_final"]
    out = [loss, grad_x[None]]
    for slot in range(4):
        out += [res[n][slot] for n in names]
    return tuple(out)
```

```python
import jax
import jax.numpy as jnp
from jax import lax
from jax.experimental import pallas as pl
from jax.experimental.pallas import tpu as pltpu
from jax.experimental.pallas import tpu_sc as plsc

F32 = jnp.float32
BF16 = jnp.bfloat16
N_DEV = 8
LANES = 128
SLAB = 4
EPS = 1e-6
ADAM_LR = 0.001
ADAM_B1 = 0.9
ADAM_B2 = 0.999
ADAM_EPS = 1e-08
ADAM_WD = 0.01
ADAM_STEP = 10
VMEM_LIMIT = 56 * 1024 * 1024
MESH = pl.DeviceIdType.MESH


def _tile(n, pref, mult=LANES):
    best = None
    t = mult
    while t <= min(n, pref):
        if n % t == 0:
            best = t
        t += mult
    return best if best is not None else n


def _params(ndim):
    return pltpu.CompilerParams(dimension_semantics=("arbitrary",) * ndim, vmem_limit_bytes=VMEM_LIMIT)


def _sds(shape, dtype):
    return jax.ShapeDtypeStruct(tuple(shape), dtype)


def _mm(name, a, b, mode, *, out_dtype=F32, tm=1024, tn=1024, tk=2048, dims=None, a_spec=None, b_spec=None):
    if dims is None:
        if mode == "nn":
            (M, K), N = a.shape, b.shape[1]
        elif mode == "nt":
            (M, K), N = a.shape, b.shape[0]
        else:
            (K, M), N = a.shape, b.shape[1]
    else:
        M, N, K = dims
    tm, tn, tk = _tile(M, tm), _tile(N, tn), _tile(K, tk)
    nk = K // tk
    if mode == "nn":
        dn = (((1,), (0,)), ((), ()))
        sa = pl.BlockSpec((tm, tk), lambda i, j, k: (i, k))
        sb = pl.BlockSpec((tk, tn), lambda i, j, k: (k, j))
    elif mode == "nt":
        dn = (((1,), (1,)), ((), ()))
        sa = pl.BlockSpec((tm, tk), lambda i, j, k: (i, k))
        sb = pl.BlockSpec((tn, tk), lambda i, j, k: (j, k))
    else:
        dn = (((0,), (0,)), ((), ()))
        sa = pl.BlockSpec((tk, tm), lambda i, j, k: (k, i))
        sb = pl.BlockSpec((tk, tn), lambda i, j, k: (k, j))
    sa = a_spec(tm, tn, tk) if a_spec is not None else sa
    sb = b_spec(tm, tn, tk) if b_spec is not None else sb
    use_acc = nk > 1 and out_dtype != F32

    def body(a_ref, b_ref, o_ref, *acc):
        k = pl.program_id(2)
        p = lax.dot_general(a_ref[...], b_ref[...], dn, preferred_element_type=F32)
        if nk == 1:
            o_ref[...] = p.astype(out_dtype)
        else:
            tgt = acc[0] if use_acc else o_ref

            @pl.when(k == 0)
            def _():
                tgt[...] = p

            @pl.when(k > 0)
            def _():
                tgt[...] += p

            if use_acc:
                @pl.when(k == nk - 1)
                def _():
                    o_ref[...] = acc[0][...].astype(out_dtype)

    return pl.pallas_call(
        body, name=name, grid=(M // tm, N // tn, nk),
        in_specs=[sa, sb], out_specs=pl.BlockSpec((tm, tn), lambda i, j, k: (i, j)),
        out_shape=_sds((M, N), out_dtype),
        scratch_shapes=[pltpu.VMEM((tm, tn), F32)] if use_acc else [],
        compiler_params=_params(3),
    )(a, b)


def _mm_rows(name, a, b, extra, outs, epilogue, *, tm, tk):
    M, K = a.shape
    N = b.shape[1]
    tm, tk = _tile(M, tm, 16), _tile(K, tk)
    nk = K // tk
    ne, no = len(extra), len(outs)

    def body(a_ref, b_ref, *rest):
        i, k = pl.program_id(0), pl.program_id(1)
        p = jnp.dot(a_ref[...], b_ref[...], preferred_element_type=F32)
        if nk == 1:
            epilogue(i, p, rest[:ne], rest[ne:ne + no])
            return
        acc = rest[ne + no]
        _acc_rows(k, acc, p)

        @pl.when(k == nk - 1)
        def _():
            epilogue(i, acc, rest[:ne], rest[ne:ne + no])

    return pl.pallas_call(
        body, name=name, grid=(M // tm, nk),
        in_specs=[pl.BlockSpec((tm, tk), lambda i, k: (i, k)), pl.BlockSpec((tk, N), lambda i, k: (k, 0))]
        + [s for _, s in extra],
        out_specs=[s for _, s in outs], out_shape=[o for o, _ in outs],
        scratch_shapes=[pltpu.VMEM((tm, N), F32)] if nk > 1 else [], compiler_params=_params(2),
    )(a, b, *[e for e, _ in extra])


def _rows(name, body, L, tm, ins, outs):
    return pl.pallas_call(
        body, name=name, grid=(L // tm,),
        in_specs=[s for _, s in ins], out_specs=[s for _, s in outs],
        out_shape=[o for o, _ in outs], compiler_params=_params(1),
    )(*[a for a, _ in ins])


def _rs(tm, w, cb=0):
    return pl.BlockSpec((tm, w), lambda i: (i, cb))


def _fs(shape):
    return pl.BlockSpec(tuple(shape), lambda i: (0,) * len(shape))


def _acc_rows(i, ref, part):
    @pl.when(i == 0)
    def _():
        ref[...] = part

    @pl.when(i > 0)
    def _():
        ref[...] += part


def _cast_bf16(name, w):
    R, C = w.shape
    tr = _tile(R, max(16, (1 << 20) // C), 16)

    def body(w_ref, o_ref):
        o_ref[...] = w_ref[...].astype(BF16)

    return _rows(name, body, R, tr, [(w, _rs(tr, C))], [(_sds((R, C), BF16), _rs(tr, C))])[0]


def _concat_cols(name, pieces, tm):
    L = pieces[0].shape[0]
    widths = [p.shape[1] for p in pieces]

    def body(*refs):
        o_ref, off = refs[-1], 0
        for p_ref, w in zip(refs[:-1], widths):
            o_ref[:, off:off + w] = p_ref[...]
            off += w

    return _rows(name, body, L, tm, [(p, _rs(tm, w)) for p, w in zip(pieces, widths)],
                 [(_sds((L, sum(widths)), pieces[0].dtype), _rs(tm, sum(widths)))])[0]


def _rms_fwd(name, x, g, tm):
    L, D = x.shape

    def body(x_ref, g_ref, o_ref):
        xv = x_ref[...]
        r = lax.rsqrt(jnp.mean(xv * xv, axis=-1, keepdims=True) + EPS)
        o_ref[...] = (xv * r * g_ref[...]).astype(BF16)

    return _rows(name, body, L, tm, [(x, _rs(tm, D)), (g, _fs((1, D)))], [(_sds((L, D), BF16), _rs(tm, D))])[0]


def _mm_res_rms(name, a, w, x, g):
    L, D = x.shape

    def epilogue(i, o, extra, outs):
        x_ref, g_ref = extra
        for r0 in range(0, tm, sub):
            rs = slice(r0, r0 + sub)
            h = x_ref[rs, :] + o[rs, :]
            r = lax.rsqrt(jnp.mean(h * h, axis=-1, keepdims=True) + EPS)
            outs[0][rs, :] = h
            outs[1][rs, :] = (h * r * g_ref[...]).astype(BF16)

    tm = _tile(L, 512, 16)
    sub = _tile(tm, LANES, 16)
    row = pl.BlockSpec((tm, D), lambda i, k: (i, 0))
    return _mm_rows(name, a, w, [(x, row), (g, pl.BlockSpec((1, D), lambda i, k: (0, 0)))],
                    [(_sds((L, D), F32), row), (_sds((L, D), BF16), row)], epilogue, tm=tm, tk=a.shape[1])


def _rms_bwd(name, dn, h, g, dres, tm, with_bf16):
    L, D = h.shape

    def body(dn_ref, h_ref, g_ref, dres_ref, dh_ref, *rest):
        i = pl.program_id(0)
        h = h_ref[...]
        r = lax.rsqrt(jnp.mean(h * h, axis=-1, keepdims=True) + EPS)
        xh = h * r
        d = dn_ref[...].astype(F32)
        dxh = d * g_ref[...]
        dh = dres_ref[...] + r * (dxh - xh * jnp.mean(dxh * xh, axis=-1, keepdims=True))
        dh_ref[...] = dh
        if with_bf16:
            rest[0][...] = dh.astype(BF16)
        _acc_rows(i, rest[-1], jnp.sum(d * xh, axis=0, keepdims=True))

    outs = [(_sds((L, D), F32), _rs(tm, D))]
    if with_bf16:
        outs.append((_sds((L, D), BF16), _rs(tm, D)))
    outs.append((_sds((1, D), F32), _fs((1, D))))
    return _rows(name, body, L, tm, [(dn, _rs(tm, D)), (h, _rs(tm, D)), (g, _fs((1, D))), (dres, _rs(tm, D))], outs)


def _final(name, h1, o2, g, tgt, tm):
    L, D = h1.shape

    def body(h1_ref, o2_ref, g_ref, t_ref, dh_ref, dhb_ref, dg_ref, loss_ref):
        i = pl.program_id(0)
        h = h1_ref[...] + o2_ref[...]
        r = lax.rsqrt(jnp.mean(h * h, axis=-1, keepdims=True) + EPS)
        xh = h * r
        gv = g_ref[...]
        e = xh * gv - t_ref[...]
        part = 0.5 * jnp.sum(jnp.mean(e * e, axis=-1, keepdims=True), axis=0, keepdims=True)
        dy = e / D
        dxh = dy * gv
        dh = r * (dxh - xh * jnp.mean(dxh * xh, axis=-1, keepdims=True))
        dh_ref[...] = dh
        dhb_ref[...] = dh.astype(BF16)
        _acc_rows(i, dg_ref, jnp.sum(dy * xh, axis=0, keepdims=True))
        _acc_rows(i, loss_ref, jnp.broadcast_to(part, (8, LANES)))

    return _rows(name, body, L, tm,
                 [(h1, _rs(tm, D)), (o2, _rs(tm, D)), (g, _fs((1, D))), (tgt, _rs(tm, D))],
                 [(_sds((L, D), F32), _rs(tm, D)), (_sds((L, D), BF16), _rs(tm, D)),
                  (_sds((1, D), F32), _fs((1, D))), (_sds((8, LANES), F32), _fs((8, LANES)))])


def _gelu_and_grad(x):
    c, k = 0.7978845608028654, 0.044715
    x2 = x * x
    t = jnp.tanh(c * x * (1.0 + k * x2))
    half = 0.5 * x
    return half * (1.0 + t), 0.5 * (1.0 + t) + half * (1.0 - t * t) * (c * (1.0 + 3.0 * k * x2))


def _glu_fn(y, g1):
    ya = jax.nn.gelu(y)
    return ya * jax.nn.sigmoid(g1)


def _glu_fwd(name, y, g1, tm):
    L, W = y.shape

    def body(y_ref, g_ref, o_ref):
        o_ref[...] = _glu_fn(y_ref[...], g_ref[...].astype(F32)).astype(BF16)

    return _rows(name, body, L, tm, [(y, _rs(tm, W)), (g1, _rs(tm, W))], [(_sds((L, W), BF16), _rs(tm, W))])[0]


def _glu_bwd(name, y, g1, dya2, tm):
    L, W = y.shape

    def body(y_ref, g_ref, d_ref, dy_ref, dg_ref):
        _, vjp = jax.vjp(_glu_fn, y_ref[...], g_ref[...].astype(F32))
        dy, dg = vjp(d_ref[...].astype(F32))
        dy_ref[...] = dy
        dg_ref[...] = dg.astype(BF16)

    return _rows(name, body, L, tm, [(y, _rs(tm, W)), (g1, _rs(tm, W)), (dya2, _rs(tm, W))],
                 [(_sds((L, W), F32), _rs(tm, W)), (_sds((L, W), BF16), _rs(tm, W))])


def _gelu_bwd(name, y, dy_direct, dya_g, proj, tm):
    L, W = y.shape

    def body(y_ref, dd_ref, dg_ref, u_ref, dyb_ref, dsk_ref):
        i = pl.program_id(0)
        dy = dd_ref[...] + dg_ref[...].astype(F32) * _gelu_and_grad(y_ref[...])[1]
        dyb_ref[...] = dy.astype(BF16)
        _acc_rows(i, dsk_ref, jnp.sum(dy * u_ref[...].astype(F32), axis=0, keepdims=True))

    return _rows(name, body, L, tm,
                 [(y, _rs(tm, W)), (dy_direct, _rs(tm, W)), (dya_g, _rs(tm, W)), (proj, _rs(tm, W, 0))],
                 [(_sds((L, W), BF16), _rs(tm, W)), (_sds((1, W), F32), _fs((1, W)))])


def _merge_fn(ma, mb, za, zb):
    return jax.nn.sigmoid(ma) * za + jax.nn.sigmoid(mb) * zb


def _merge_fwd(name, proj, cb_a, cb_b, za, zb, tm):
    L, D = za.shape

    def body(ma_ref, mb_ref, za_ref, zb_ref, o_ref):
        o_ref[...] = _merge_fn(ma_ref[...].astype(F32), mb_ref[...].astype(F32), za_ref[...].astype(F32),
                               zb_ref[...].astype(F32)).astype(BF16)

    return _rows(name, body, L, tm,
                 [(proj, _rs(tm, D, cb_a)), (proj, _rs(tm, D, cb_b)), (za, _rs(tm, D)), (zb, _rs(tm, D))],
                 [(_sds((L, D), BF16), _rs(tm, D))])[0]


def _merge_bwd(name, proj, cb_a, cb_b, za, zb, dmerged, tm):
    L, D = za.shape

    def body(ma_ref, mb_ref, za_ref, zb_ref, d_ref, dza_ref, dzb_ref, dma_ref, dmb_ref):
        _, vjp = jax.vjp(_merge_fn, ma_ref[...].astype(F32), mb_ref[...].astype(F32), za_ref[...].astype(F32),
                         zb_ref[...].astype(F32))
        dma, dmb, dza, dzb = vjp(d_ref[...].astype(F32))
        dza_ref[...] = dza.astype(BF16)
        dzb_ref[...] = dzb.astype(BF16)
        dma_ref[...] = dma.astype(BF16)
        dmb_ref[...] = dmb.astype(BF16)

    return _rows(name, body, L, tm,
                 [(proj, _rs(tm, D, cb_a)), (proj, _rs(tm, D, cb_b)), (za, _rs(tm, D)), (zb, _rs(tm, D)),
                  (dmerged, _rs(tm, D))],
                 [(_sds((L, D), BF16), _rs(tm, D)), (_sds((L, D), BF16), _rs(tm, D)),
                  (_sds((L, D), BF16), _rs(tm, D)), (_sds((L, D), BF16), _rs(tm, D))])


def _shift_down(x, k):
    row = lax.broadcasted_iota(jnp.int32, x.shape, 0)
    return jnp.where(row >= k, pltpu.roll(x, k, axis=0), 0.0)


def _shift_up(x, k):
    n = x.shape[0]
    row = lax.broadcasted_iota(jnp.int32, x.shape, 0)
    return jnp.where(row < n - k, pltpu.roll(x, n - k, axis=0), 0.0)


def _conv3(cv, w_ref, b_ref):
    return (w_ref[2:3, :] * cv + w_ref[1:2, :] * _shift_down(cv, 1) + w_ref[0:1, :] * _shift_down(cv, 2)
            + b_ref[...])


def _conv3_bwd(dcc, cv, w_ref):
    dcv = w_ref[2:3, :] * dcc + w_ref[1:2, :] * _shift_up(dcc, 1) + w_ref[0:1, :] * _shift_up(dcc, 2)
    dw = [jnp.sum(dcc * _shift_down(cv, 2), axis=0, keepdims=True),
          jnp.sum(dcc * _shift_down(cv, 1), axis=0, keepdims=True),
          jnp.sum(dcc * cv, axis=0, keepdims=True)]
    db = jnp.sum(dcc, axis=0, keepdims=True)
    return dcv, dw, db


def _store_rows(ref, rows):
    for r, val in enumerate(rows):
        ref[r:r + 1, :] = val


def _cols(name, body, ncb, ins, outs):
    return pl.pallas_call(
        body, name=name, grid=(ncb,),
        in_specs=[s for _, s in ins], out_specs=[s for _, s in outs],
        out_shape=[o for o, _ in outs], compiler_params=_params(1),
    )(*[a for a, _ in ins])


def _cb(L, w, off=0):
    return pl.BlockSpec((L, w), lambda j: (0, j + off))


def _convb_fwd(name, proj, cb_v, cb_gb, cb_gc, w, b):
    L = proj.shape[0]
    W = w.shape[1]
    c = LANES

    def body(v_ref, gb_ref, gc_ref, w_ref, b_ref, q_ref):
        cc = _conv3(gc_ref[...].astype(F32) * v_ref[...].astype(F32), w_ref, b_ref)
        q_ref[...] = (gb_ref[...].astype(F32) * cc).astype(BF16)

    return _cols(name, body, W // c,
                 [(proj, _cb(L, c, cb_v)), (proj, _cb(L, c, cb_gb)), (proj, _cb(L, c, cb_gc)),
                  (w, _cb(3, c)), (b, _cb(1, c))],
                 [(_sds((L, W), BF16), _cb(L, c))])[0]


def _convb_bwd(name, proj, cb_v, cb_gb, cb_gc, w, b, dq):
    L = proj.shape[0]
    W = w.shape[1]
    c = LANES

    def body(v_ref, gb_ref, gc_ref, w_ref, b_ref, dq_ref, dv_ref, dgb_ref, dgc_ref, dw_ref, db_ref):
        v, gc = v_ref[...].astype(F32), gc_ref[...].astype(F32)
        cv = gc * v
        cc = _conv3(cv, w_ref, b_ref)
        dq = dq_ref[...].astype(F32)
        dgb_ref[...] = (dq * cc).astype(BF16)
        dcv, dw, db = _conv3_bwd(dq * gb_ref[...].astype(F32), cv, w_ref)
        dv_ref[...] = (dcv * gc).astype(BF16)
        dgc_ref[...] = (dcv * v).astype(BF16)
        _store_rows(dw_ref, dw)
        db_ref[...] = db

    return _cols(name, body, W // c,
                 [(proj, _cb(L, c, cb_v)), (proj, _cb(L, c, cb_gb)), (proj, _cb(L, c, cb_gc)),
                  (w, _cb(3, c)), (b, _cb(1, c)), (dq, _cb(L, c))],
                 [(_sds((L, W), BF16), _cb(L, c)), (_sds((L, W), BF16), _cb(L, c)), (_sds((L, W), BF16), _cb(L, c)),
                  (_sds((3, W), F32), _cb(3, c)), (_sds((1, W), F32), _cb(1, c))])


HALO = 16


def _ffn_up_act(name, hn, w_up, w, b):
    L, D = hn.shape
    Fw = w.shape[1]
    tm, tc = _tile(L, 1024, HALO), _tile(Fw, 512)
    ncb = Fw // tc

    def body(x_ref, wa_ref, wg_ref, w_ref, b_ref, hh_ref, f_ref, carry):
        i, j = pl.program_id(0), pl.program_id(1)
        a16 = jnp.dot(x_ref[...], wa_ref[...], preferred_element_type=F32).astype(BF16)
        g16 = jnp.dot(x_ref[...], wg_ref[...], preferred_element_type=F32).astype(BF16)
        hh_ref[0] = a16
        hh_ref[1] = g16
        for c0 in range(0, tc, LANES):
            cs = slice(c0, c0 + LANES)
            prev = jnp.where(i == 0, 0.0, carry[j, :, cs])
            x = jnp.concatenate([prev, a16[:, cs].astype(F32)], axis=0)
            n = x.shape[0]
            a = (w_ref[2:3, cs] * x + w_ref[1:2, cs] * pltpu.roll(x, 1, axis=0) + w_ref[0:1, cs] * pltpu.roll(x, 2, axis=0)
                 + b_ref[:, cs])[8:n]
            f_ref[:, cs] = (_gelu_and_grad(a)[0] * g16[:, cs].astype(F32)).astype(BF16)
            carry[j, :, cs] = x[n - 8:n]

    return pl.pallas_call(
        body, name=name, grid=(L // tm, ncb),
        in_specs=[pl.BlockSpec((tm, D), lambda i, j: (i, 0)), pl.BlockSpec((D, tc), lambda i, j: (0, j)),
                  pl.BlockSpec((D, tc), lambda i, j: (0, j + ncb)),
                  pl.BlockSpec((3, tc), lambda i, j: (0, j)), pl.BlockSpec((1, tc), lambda i, j: (0, j))],
        out_specs=[pl.BlockSpec((2, tm, tc), lambda i, j: (0, i, j)), pl.BlockSpec((tm, tc), lambda i, j: (i, j))],
        out_shape=[_sds((2, L, Fw), BF16), _sds((L, Fw), BF16)],
        scratch_shapes=[pltpu.VMEM((ncb, 8, tc), F32)], compiler_params=_params(2),
    )(hn, w_up, w_up, w, b)


def _ffn_down_bwd_act(name, dy, w_down, hh, w, b):
    L, D = dy.shape
    Fw = w.shape[1]
    tm, tc = _tile(L, 512, HALO), _tile(Fw, 512)
    ncb, nrt, rpt = Fw // tc, L // tm, tm // HALO
    dn = (((1,), (1,)), ((), ()))

    def body(dy_ref, wd_ref, a_ref, ap_ref, h2_ref, w_ref, b_ref, dhh_ref, dw_ref, db_ref, carry, acc):
        i, j = pl.program_id(0), pl.program_id(1)
        first_rows = i == nrt - 1
        d16 = lax.dot_general(dy_ref[...], wd_ref[...], dn, preferred_element_type=F32).astype(BF16)
        for c0 in range(0, tc, LANES):
            cs = slice(c0, c0 + LANES)
            h1 = jnp.concatenate([jnp.where(first_rows, 0.0, ap_ref[:, cs].astype(F32)), a_ref[:, cs].astype(F32)], axis=0)
            s1, s2 = pltpu.roll(h1, 1, axis=0), pltpu.roll(h1, 2, axis=0)
            n = h1.shape[0]
            a = (w_ref[2:3, cs] * h1 + w_ref[1:2, cs] * s1 + w_ref[0:1, cs] * s2 + b_ref[:, cs])[HALO:n]
            ga, dga = _gelu_and_grad(a)
            d = d16[:, cs].astype(F32)
            da = d * h2_ref[:, cs].astype(F32) * dga
            dae = jnp.concatenate([da, jnp.where(i == 0, 0.0, carry[j, :, cs])], axis=0)
            m = dae.shape[0]
            dh1 = w_ref[2:3, cs] * dae + w_ref[1:2, cs] * pltpu.roll(dae, m - 1, axis=0) + w_ref[0:1, cs] * pltpu.roll(dae, m - 2, axis=0)
            dhh_ref[0, :, cs] = dh1[0:tm].astype(BF16)
            dhh_ref[1, :, cs] = (d * ga).astype(BF16)
            carry[j, :, cs] = da[0:8]
            rows = [jnp.sum(da * s2[HALO:n], axis=0, keepdims=True), jnp.sum(da * s1[HALO:n], axis=0, keepdims=True),
                    jnp.sum(da * h1[HALO:n], axis=0, keepdims=True), jnp.sum(da, axis=0, keepdims=True)]
            for r in range(4):
                tot = jnp.where(i == 0, 0.0, acc[j, r:r + 1, cs]) + rows[r]
                acc[j, r:r + 1, cs] = tot
                if r < 3:
                    dw_ref[r:r + 1, cs] = tot
                else:
                    db_ref[:, cs] = tot

    rt = lambda i: nrt - 1 - i
    dhh, dw, db = pl.pallas_call(
        body, name=name, grid=(nrt, ncb),
        in_specs=[pl.BlockSpec((tm, D), lambda i, j: (rt(i), 0)), pl.BlockSpec((tc, D), lambda i, j: (j, 0)),
                  pl.BlockSpec((None, tm, tc), lambda i, j: (0, rt(i), j)),
                  pl.BlockSpec((None, HALO, tc), lambda i, j: (0, jnp.maximum(rt(i) * rpt - 1, 0), j)),
                  pl.BlockSpec((None, tm, tc), lambda i, j: (1, rt(i), j)),
                  pl.BlockSpec((3, tc), lambda i, j: (0, j)), pl.BlockSpec((1, tc), lambda i, j: (0, j))],
        out_specs=[pl.BlockSpec((2, tm, tc), lambda i, j: (0, rt(i), j)),
                   pl.BlockSpec((None, 3, tc), lambda i, j: (i, 0, j)), pl.BlockSpec((None, 1, tc), lambda i, j: (i, 0, j))],
        out_shape=[_sds((2, L, Fw), BF16), _sds((nrt, 3, Fw), F32), _sds((nrt, 1, Fw), F32)],
        scratch_shapes=[pltpu.VMEM((ncb, 8, tc), F32), pltpu.VMEM((ncb, 8, tc), F32)], compiler_params=_params(2),
    )(dy, w_down, hh, hh, hh, w, b)
    return dhh, dw[nrt - 1], db[nrt - 1]


def _prep_fn(ar, ai, ldt, brt, bit):
    dt = jnp.exp(ldt)
    mag = jnp.exp(dt * ar)
    are = mag * jnp.cos(dt * ai)
    aim = mag * jnp.sin(dt * ai)
    nr = are - 1.0
    ni = aim
    den = ar * ar + ai * ai
    fr = (nr * ar + ni * ai) / den
    fi = (ni * ar - nr * ai) / den
    return are, aim, fr * brt - fi * bit, fr * bit + fi * brt


def _prep_fwd(name, ar, ai, ldt, brt, bit):
    def body(ar_ref, ai_ref, l_ref, br_ref, bi_ref, o1, o2, o3, o4):
        o1[...], o2[...], o3[...], o4[...] = _prep_fn(ar_ref[...], ai_ref[...], l_ref[...], br_ref[...], bi_ref[...])

    return pl.pallas_call(body, name=name,
                          out_shape=[_sds(ar.shape, F32), _sds(ar.shape, F32), _sds(brt.shape, F32), _sds(brt.shape, F32)],
                          )(ar, ai, ldt, brt, bit)


def _prep_bwd(name, ar, ai, ldt, brt, bit, g1, g2, g3, g4):
    def body(ar_ref, ai_ref, l_ref, br_ref, bi_ref, g1_ref, g2_ref, g3_ref, g4_ref, o1, o2, o3, o4, o5):
        _, vjp = jax.vjp(_prep_fn, ar_ref[...], ai_ref[...], l_ref[...], br_ref[...], bi_ref[...])
        o1[...], o2[...], o3[...], o4[...], o5[...] = vjp((g1_ref[...], g2_ref[...], g3_ref[...], g4_ref[...]))

    return pl.pallas_call(body, name=name,
                          out_shape=[_sds(ar.shape, F32)] * 3 + [_sds(brt.shape, F32)] * 2,
                          )(ar, ai, ldt, brt, bit, g1, g2, g3, g4)


def _scan_steps(tc, pitch, ng, reverse, a_r, a_i, stage_b, stage_x, out_re, out_im, st_re, st_im, acc):
    def step(tt, carry):
        t = (tc - 1 - tt) if reverse else tt
        new, sums = [], []
        for g in range(ng):
            rows = pl.ds(g * 8 * pitch + t, 8, stride=pitch)
            cr, ci = carry[2 * g], carry[2 * g + 1]
            br, bi = stage_b[0][rows, :], stage_b[1][rows, :]
            if reverse:
                xr, xi = stage_x[0][rows, :], stage_x[1][rows, :]
                sums += [carry[2 * ng + 2 * g] + (xr * cr + xi * ci), carry[2 * ng + 2 * g + 1] + (xr * ci - xi * cr)]
                nr = a_r[g] * cr + a_i[g] * ci + br
                ni = a_r[g] * ci - a_i[g] * cr + bi
            else:
                nr = a_r[g] * cr - a_i[g] * ci + br
                ni = a_r[g] * ci + a_i[g] * cr + bi
            out_re[rows, :] = nr
            out_im[rows, :] = ni
            new += [nr, ni]
        return tuple(new + sums)

    init = []
    for g in range(ng):
        init += [st_re[g], st_im[g]]
    if reverse:
        for g in range(ng):
            init += [acc[0][g], acc[1][g]]
    fin = lax.fori_loop(0, tc, step, tuple(init), unroll=2)
    for g in range(ng):
        st_re[g] = fin[2 * g]
        st_im[g] = fin[2 * g + 1]
        if reverse:
            acc[0][g] = fin[2 * ng + 2 * g]
            acc[1][g] = fin[2 * ng + 2 * g + 1]


def _s5_fwd(name, proj, bm_re, bm_im, cm_re, cm_im, dskip, a_re, a_im):
    L = proj.shape[0]
    nb = bm_re.shape[0]
    ns, W = SLAB * nb, nb * LANES
    ng = ns // 8
    tc = min(2 * LANES, L)
    pitch = tc + 8
    wide = SLAB * LANES

    def body(u_ref, bre_ref, bim_ref, cre_ref, cim_ref, d_ref, ar_ref, ai_ref, xr_ref, xi_ref, y_ref, ya_ref,
             sb_re, sb_im, out_re, out_im, st_re, st_im):
        @pl.when(pl.program_id(0) == 0)
        def _():
            st_re[...] = jnp.zeros(st_re.shape, F32)
            st_im[...] = jnp.zeros(st_im.shape, F32)

        for j in range(nb):
            ub = u_ref[:, j * LANES:(j + 1) * LANES]
            r1 = jnp.dot(ub, bre_ref[j], preferred_element_type=F32)
            r2 = jnp.dot(ub, bim_ref[j], preferred_element_type=F32)
            for q in range(SLAB):
                sb_re[pl.ds((SLAB * j + q) * pitch, tc), :] = r1[:, q * LANES:(q + 1) * LANES]
                sb_im[pl.ds((SLAB * j + q) * pitch, tc), :] = r2[:, q * LANES:(q + 1) * LANES]
        a_r = [ar_ref[g] for g in range(ng)]
        a_i = [ai_ref[g] for g in range(ng)]
        _scan_steps(tc, pitch, ng, False, a_r, a_i, (sb_re, sb_im), None, out_re, out_im, st_re, st_im, None)
        for j in range(nb):
            x1 = [out_re[pl.ds((SLAB * j + q) * pitch, tc), :].astype(BF16) for q in range(SLAB)]
            x2 = [out_im[pl.ds((SLAB * j + q) * pitch, tc), :].astype(BF16) for q in range(SLAB)]
            for q in range(SLAB):
                xr_ref[SLAB * j + q] = x1[q]
                xi_ref[SLAB * j + q] = x2[q]
            cols = slice(j * LANES, (j + 1) * LANES)
            y = (jnp.dot(jnp.concatenate(x1, axis=1), cre_ref[j], preferred_element_type=F32)
                 + jnp.dot(jnp.concatenate(x2, axis=1), cim_ref[j], preferred_element_type=F32)
                 + d_ref[:, cols] * u_ref[:, cols].astype(F32))
            y_ref[:, cols] = y
            ya_ref[:, cols] = jax.nn.gelu(y).astype(BF16)

    full3 = lambda s: pl.BlockSpec(s, lambda i: (0, 0, 0))
    xs = pl.BlockSpec((ns, tc, LANES), lambda i: (0, i, 0))
    rows = pl.BlockSpec((tc, W), lambda i: (i, 0))
    return pl.pallas_call(
        body, name=name, grid=(L // tc,),
        in_specs=[rows, full3((nb, LANES, wide)), full3((nb, LANES, wide)), full3((nb, wide, LANES)),
                  full3((nb, wide, LANES)), pl.BlockSpec((1, W), lambda i: (0, 0)), full3((ng, 8, LANES)), full3((ng, 8, LANES))],
        out_specs=[xs, xs, rows, rows],
        out_shape=[_sds((ns, L, LANES), BF16)] * 2 + [_sds((L, W), F32), _sds((L, W), BF16)],
        scratch_shapes=[pltpu.VMEM((ns * pitch, LANES), F32)] * 4 + [pltpu.VMEM((ng, 8, LANES), F32)] * 2,
        compiler_params=_params(1),
    )(proj, bm_re, bm_im, cm_re, cm_im, dskip, a_re, a_im)


def _s5_bwd(name, dyb, proj, xs_re, xs_im, cmt_re, cmt_im, bmt_re, bmt_im, dskip, a_re, a_im):
    L = dyb.shape[0]
    nb = cmt_re.shape[0]
    ns, W = SLAB * nb, nb * LANES
    ng = ns // 8
    tc = min(LANES, L)
    pitch = tc + 8
    nt = L // tc
    wide = SLAB * LANES
    dn = (((0,), (0,)), ((), ()))

    def body(dy_ref, u_ref, xr_ref, xi_ref, cre_ref, cim_ref, bre_ref, bim_ref, d_ref, ar_ref, ai_ref,
             du_ref, gbr_ref, gbi_ref, gcr_ref, gci_ref, dar_ref, dai_ref,
             sd_re, sd_im, sx_re, sx_im, out_re, out_im, st_re, st_im, acc_re, acc_im):
        first = pl.program_id(0) == 0

        @pl.when(first)
        def _():
            for r in (st_re, st_im, acc_re, acc_im):
                r[...] = jnp.zeros(r.shape, F32)
            for r in (gbr_ref, gbi_ref, gcr_ref, gci_ref):
                r[...] = jnp.zeros(r.shape, F32)

        for j in range(nb):
            dyj = dy_ref[:, j * LANES:(j + 1) * LANES]
            r1 = jnp.dot(dyj, cre_ref[j], preferred_element_type=F32)
            r2 = jnp.dot(dyj, cim_ref[j], preferred_element_type=F32)
            for q in range(SLAB):
                s = SLAB * j + q
                sd_re[pl.ds(s * pitch, tc), :] = r1[:, q * LANES:(q + 1) * LANES]
                sd_im[pl.ds(s * pitch, tc), :] = r2[:, q * LANES:(q + 1) * LANES]
                sx_re[pl.ds(s * pitch, tc), :] = xr_ref[s].astype(F32)
                sx_im[pl.ds(s * pitch, tc), :] = xi_ref[s].astype(F32)
        a_r = [ar_ref[g] for g in range(ng)]
        a_i = [ai_ref[g] for g in range(ng)]
        _scan_steps(tc, pitch, ng, True, a_r, a_i, (sd_re, sd_im), (sx_re, sx_im), out_re, out_im, st_re, st_im,
                    (acc_re, acc_im))
        for j in range(nb):
            cols = slice(j * LANES, (j + 1) * LANES)
            l1 = jnp.concatenate([out_re[pl.ds((SLAB * j + q) * pitch, tc), :] for q in range(SLAB)], axis=1).astype(BF16)
            l2 = jnp.concatenate([out_im[pl.ds((SLAB * j + q) * pitch, tc), :] for q in range(SLAB)], axis=1).astype(BF16)
            dyj = dy_ref[:, cols]
            du = (jnp.dot(l1, bre_ref[j], preferred_element_type=F32) + jnp.dot(l2, bim_ref[j], preferred_element_type=F32)
                  + d_ref[:, cols] * dyj.astype(F32))
            du_ref[:, cols] = du.astype(BF16)
            uj = u_ref[:, cols]
            gbr_ref[j] += lax.dot_general(uj, l1, dn, preferred_element_type=F32)
            gbi_ref[j] += lax.dot_general(uj, l2, dn, preferred_element_type=F32)
            x1 = jnp.concatenate([xr_ref[SLAB * j + q] for q in range(SLAB)], axis=1)
            x2 = jnp.concatenate([xi_ref[SLAB * j + q] for q in range(SLAB)], axis=1)
            gcr_ref[j] += lax.dot_general(dyj, x1, dn, preferred_element_type=F32)
            gci_ref[j] += lax.dot_general(dyj, x2, dn, preferred_element_type=F32)
        dar_ref[...] = acc_re[...]
        dai_ref[...] = acc_im[...]

    full3 = lambda s: pl.BlockSpec(s, lambda i: (0, 0, 0))
    xs = pl.BlockSpec((ns, tc, LANES), lambda i: (0, nt - 1 - i, 0))
    rows = pl.BlockSpec((tc, W), lambda i: (nt - 1 - i, 0))
    mat_a, mat_b = full3((nb, LANES, wide)), full3((nb, wide, LANES))
    vec = full3((ng, 8, LANES))
    return pl.pallas_call(
        body, name=name, grid=(nt,),
        in_specs=[rows, rows, xs, xs, mat_a, mat_a, mat_b, mat_b, pl.BlockSpec((1, W), lambda i: (0, 0)), vec, vec],
        out_specs=[rows, mat_a, mat_a, mat_a, mat_a, vec, vec],
        out_shape=[_sds((L, W), BF16)] + [_sds((nb, LANES, wide), F32)] * 4 + [_sds((ng, 8, LANES), F32)] * 2,
        scratch_shapes=[pltpu.VMEM((ns * pitch, LANES), F32)] * 6 + [pltpu.VMEM((ng, 8, LANES), F32)] * 4,
        compiler_params=_params(1),
    )(dyb, proj, xs_re, xs_im, cmt_re, cmt_im, bmt_re, bmt_im, dskip, a_re, a_im)


def _peer(k):
    x, y, c = lax.axis_index("x"), lax.axis_index("y"), lax.axis_index("c")
    px = 1 - x if (k >> 2) & 1 else x
    py = 1 - y if (k >> 1) & 1 else y
    pc = 1 - c if k & 1 else c
    return (px, py, pc), 4 * px + 2 * py + pc


def _window(ref, kind, idx, n):
    if kind == "col":
        w = ref.shape[1] // n
        return ref.at[:, pl.ds(pl.multiple_of(idx * w, LANES), w)]
    r = ref.shape[0] // n
    return ref.at[pl.ds(pl.multiple_of(idx * r, 8), r), :]


def _all_gather(name, shards, kinds):
    n = len(shards)
    fulls = []
    for s, kind in zip(shards, kinds):
        fulls.append(_sds((s.shape[0], s.shape[1] * N_DEV) if kind == "col" else (s.shape[0] * N_DEV, s.shape[1]), s.dtype))

    def body(*refs):
        src, dst = refs[:n], refs[n:2 * n]
        send, recv, loc = refs[2 * n:]
        me = 4 * lax.axis_index("x") + 2 * lax.axis_index("y") + lax.axis_index("c")
        copies = []
        for a in range(n):
            own = pltpu.make_async_copy(src[a], _window(dst[a], kinds[a], me, N_DEV), loc.at[a])
            own.start()
            copies.append(own)
        sends = []
        for k in range(1, N_DEV):
            dev, _ = _peer(k)
            for a in range(n):
                cp = pltpu.make_async_remote_copy(
                    src_ref=src[a], dst_ref=_window(dst[a], kinds[a], me, N_DEV),
                    send_sem=send.at[a * N_DEV + k], recv_sem=recv.at[a * N_DEV + k],
                    device_id=dev, device_id_type=MESH)
                cp.start()
                sends.append(cp)
        for k in range(1, N_DEV):
            dev, pidx = _peer(k)
            for a in range(n):
                pltpu.make_async_remote_copy(
                    src_ref=src[a], dst_ref=_window(dst[a], kinds[a], pidx, N_DEV),
                    send_sem=send.at[a * N_DEV + k], recv_sem=recv.at[a * N_DEV + k],
                    device_id=dev, device_id_type=MESH).wait_recv()
        for cp in sends:
            cp.wait_send()
        for cp in copies:
            cp.wait()

    any_ = pl.BlockSpec(memory_space=pl.ANY)
    return pl.pallas_call(
        body, name=name, in_specs=[any_] * n, out_specs=[any_] * n, out_shape=fulls,
        scratch_shapes=[pltpu.SemaphoreType.DMA((n * N_DEV,)), pltpu.SemaphoreType.DMA((n * N_DEV,)),
                        pltpu.SemaphoreType.DMA((n,))],
        compiler_params=pltpu.CompilerParams(has_side_effects=True),
    )(*shards)


def _xfer_refs(mode, kinds, a, src, dst, me, pidx):
    if mode == "gather":
        return src[a], _window(dst[a], kinds[a], me, N_DEV), _window(dst[a], kinds[a], pidx, N_DEV)
    return _window(src[a], kinds[a], pidx, N_DEV), dst[a].at[me], dst[a].at[pidx]


def _xfer_out_shapes(mode, arrs, kinds):
    outs = []
    for s, kind in zip(arrs, kinds):
        if mode == "gather":
            outs.append((s.shape[0], s.shape[1] * N_DEV) if kind == "col" else (s.shape[0] * N_DEV, s.shape[1]))
        else:
            outs.append((N_DEV,) + ((s.shape[0], s.shape[1] // N_DEV) if kind == "col" else (s.shape[0] // N_DEV, s.shape[1])))
    return outs


def _sc_xfer(name, mode, arrs, kinds, collective_id):
    n = len(arrs)
    shapes = _xfer_out_shapes(mode, arrs, kinds)
    hbm = pltpu.MemorySpace.HBM
    src = [jax.new_ref(a, memory_space=hbm) for a in arrs]
    dst = [jax.empty_ref(_sds(shp, a.dtype), memory_space=hbm) for shp, a in zip(shapes, arrs)]

    @pl.kernel(mesh=plsc.ScalarSubcoreMesh(axis_name="seq", num_cores=1), name=name,
               scratch_types=(pltpu.SemaphoreType.DMA((n * N_DEV,)), pltpu.SemaphoreType.DMA((n * N_DEV,)),
                              pltpu.SemaphoreType.DMA((n,))),
               compiler_params=pltpu.CompilerParams(collective_id=collective_id))
    def launch(send, recv, loc):
        barrier = pltpu.get_barrier_semaphore()
        for k in range(1, N_DEV):
            pl.semaphore_signal(barrier, inc=1, device_id=_peer(k)[0], device_id_type=MESH)
        pl.semaphore_wait(barrier, N_DEV - 1)
        me = 4 * lax.axis_index("x") + 2 * lax.axis_index("y") + lax.axis_index("c")
        own, sends = [], []
        for a in range(n):
            s, _, d = _xfer_refs(mode, kinds, a, src, dst, me, me)
            own.append(pltpu.make_async_copy(s, d, loc.at[a]))
            own[-1].start()
        for k in range(1, N_DEV):
            dev, pidx = _peer(k)
            for a in range(n):
                s, d, _ = _xfer_refs(mode, kinds, a, src, dst, me, pidx)
                sends.append(pltpu.make_async_remote_copy(src_ref=s, dst_ref=d, send_sem=send.at[a * N_DEV + k],
                                                          recv_sem=recv.at[a * N_DEV + k], device_id=dev, device_id_type=MESH))
                sends[-1].start()
        for cp in own:
            cp.wait()
        for k in range(1, N_DEV):
            dev, pidx = _peer(k)
            for a in range(n):
                s, _, land = _xfer_refs(mode, kinds, a, src, dst, me, pidx)
                pltpu.make_async_remote_copy(src_ref=s, dst_ref=land, send_sem=send.at[a * N_DEV + k],
                                             recv_sem=recv.at[a * N_DEV + k], device_id=dev, device_id_type=MESH).wait_recv()
        for cp in sends:
            cp.wait_send()

    launch()
    return [d[...] for d in dst]


def _sc_gather(name, arrs, kinds, collective_id):
    n = len(arrs)
    pairs = 7
    shapes = _xfer_out_shapes("gather", arrs, kinds)
    hbm = pltpu.MemorySpace.HBM
    src = [jax.new_ref(a, memory_space=hbm) for a in arrs]
    dst = [jax.empty_ref(_sds(shp, a.dtype), memory_space=hbm) for shp, a in zip(shapes, arrs)]

    @pl.kernel(mesh=plsc.ScalarSubcoreMesh(axis_name="seq", num_cores=1), name=name,
               scratch_types=(pltpu.SemaphoreType.DMA((n * pairs,)), pltpu.SemaphoreType.DMA((n * pairs,)),
                              pltpu.SemaphoreType.DMA((n,))),
               compiler_params=pltpu.CompilerParams(collective_id=collective_id))
    def launch(send, recv, loc):
        x, y, c = lax.axis_index("x"), lax.axis_index("y"), lax.axis_index("c")
        me = 4 * x + 2 * y + c
        sib = (x, y, 1 - c)
        chips = []
        for fx, fy in ((1, 0), (0, 1), (1, 1)):
            px, py = (1 - x if fx else x), (1 - y if fy else y)
            chips.append(((px, py, c), 4 * px + 2 * py + c, 4 * px + 2 * py + (1 - c)))
        barrier = pltpu.get_barrier_semaphore()
        for dev in [sib] + [ch[0] for ch in chips]:
            pl.semaphore_signal(barrier, inc=1, device_id=dev, device_id_type=MESH)
        pl.semaphore_wait(barrier, 4)

        def win(a, idx):
            return _window(dst[a], kinds[a], idx, N_DEV)

        def rcopy(a, p, s, d, dev):
            return pltpu.make_async_remote_copy(src_ref=s, dst_ref=d, send_sem=send.at[a * pairs + p],
                                                recv_sem=recv.at[a * pairs + p], device_id=dev, device_id_type=MESH)

        own, sends = [], []
        for a in range(n):
            own.append(pltpu.make_async_copy(src[a], win(a, me), loc.at[a]))
            own[-1].start()
        for j, (dev, _, _) in enumerate(chips):
            for a in range(n):
                sends.append(rcopy(a, 1 + j, src[a], win(a, me), dev))
                sends[-1].start()
        for a in range(n):
            sends.append(rcopy(a, 0, src[a], win(a, me), sib))
            sends[-1].start()
        for j, (dev, idx, _) in enumerate(chips):
            for a in range(n):
                rcopy(a, 1 + j, src[a], win(a, idx), dev).wait_recv()
                sends.append(rcopy(a, 4 + j, win(a, idx), win(a, idx), sib))
                sends[-1].start()
        for cp in own:
            cp.wait()
        for a in range(n):
            rcopy(a, 0, src[a], win(a, 4 * x + 2 * y + (1 - c)), sib).wait_recv()
        for j, (_, _, sidx) in enumerate(chips):
            for a in range(n):
                rcopy(a, 4 + j, src[a], win(a, sidx), sib).wait_recv()
        for cp in sends:
            cp.wait_send()

    launch()
    return [d[...] for d in dst]


_SEQ_IDS = {"gather_in": 7, "gather_mix": 1, "gather_ffn": 2, "grads_down": 3, "grads_up": 8, "grads_mix": 4,
            "grads_small": 5, "grads_in": 6}


def _launch(name, mode, arrs, kinds):
    if mode == "gather":
        return _sc_gather(name, list(arrs), kinds, _SEQ_IDS[name])
    return _sc_xfer(name, mode, list(arrs), kinds, _SEQ_IDS[name])


def _adamw(name, parts, w, m, v):
    P, R, C = parts.shape
    sub = 16 if parts.dtype == BF16 else 8
    tr = R if R * C <= (1 << 18) else _tile(R, max(sub, (1 << 18) // C), sub)

    def body(p_ref, w_ref, m_ref, v_ref, g_ref, d_ref, nm_ref, nv_ref):
        g = p_ref[0].astype(F32)
        for s in range(1, P):
            g = g + p_ref[s].astype(F32)
        m2 = ADAM_B1 * m_ref[...] + (1.0 - ADAM_B1) * g
        v2 = ADAM_B2 * v_ref[...] + (1.0 - ADAM_B2) * (g * g)
        m_hat = m2 / (1.0 - ADAM_B1 ** ADAM_STEP)
        v_hat = v2 / (1.0 - ADAM_B2 ** ADAM_STEP)
        g_ref[...] = g
        d_ref[...] = -ADAM_LR * (m_hat / (jnp.sqrt(v_hat) + ADAM_EPS) + ADAM_WD * w_ref[...])
        nm_ref[...] = m2
        nv_ref[...] = v2

    sp = pl.BlockSpec((tr, C), lambda i: (i, 0))
    return pl.pallas_call(
        body, name=name, grid=(R // tr,),
        in_specs=[pl.BlockSpec((P, tr, C), lambda i: (0, i, 0)), sp, sp, sp], out_specs=[sp] * 4,
        out_shape=[_sds((R, C), F32)] * 4, compiler_params=_params(1),
    )(parts, w, m, v)


def _pack(arrs, row_mult=8):
    pieces, total = [], 0
    for a in arrs:
        f = a.reshape(-1).astype(F32)
        pad = (-f.shape[0]) % (8 * LANES)
        pieces.append(jnp.pad(f, (0, pad)) if pad else f)
        total += f.shape[0] + pad
    tail = (-total) % (row_mult * LANES)
    if tail:
        pieces.append(jnp.zeros((tail,), F32))
    return jnp.concatenate(pieces).reshape(-1, LANES)


def _unpack(buf, shapes, lead=()):
    out, row = [], 0
    for shp in shapes:
        size = 1
        for d in shp:
            size *= d
        rows = -(-size // (8 * LANES)) * 8
        piece = buf[..., row:row + rows, :].reshape(lead + (rows * LANES,))[..., :size]
        out.append(piece.reshape(lead + tuple(shp)))
        row += rows
    return out


def kernel(x, norm_tok, w_in, a_re, a_im, log_dt, b_re, b_im, c_re, c_im, d_skip, w_glu, w_ssm_out, conv_w, conv_b, w_conv_out, w_o, norm_ffn, w_up, ffn_conv_w, ffn_conv_b, w_down, norm_final, loss_target, m_norm_tok, m_w_in, m_a_re, m_a_im, m_log_dt, m_b_re, m_b_im, m_c_re, m_c_im, m_d_skip, m_w_glu, m_w_ssm_out, m_conv_w, m_conv_b, m_w_conv_out, m_w_o, m_norm_ffn, m_w_up, m_ffn_conv_w, m_ffn_conv_b, m_w_down, m_norm_final, v_norm_tok, v_w_in, v_a_re, v_a_im, v_log_dt, v_b_re, v_b_im, v_c_re, v_c_im, v_d_skip, v_w_glu, v_w_ssm_out, v_conv_w, v_conv_b, v_w_conv_out, v_w_o, v_norm_ffn, v_w_up, v_ffn_conv_w, v_ffn_conv_b, v_w_down, v_norm_final):
    args = dict(locals())
    L, D = x.shape[1], x.shape[2]
    G, P, H = b_re.shape[1], b_re.shape[2], b_re.shape[3]
    SW = G * H
    CW = conv_b.shape[1]
    FF = ffn_conv_b.shape[1]
    GP = G * P
    nb = SW // LANES
    gpb = LANES // H
    me = 4 * lax.axis_index("x") + 2 * lax.axis_index("y") + lax.axis_index("c")
    tm = _tile(L, 256, 16)
    x2 = x[0]
    tgt = loss_target[0]

    big = [("w_in", "col"), ("w_glu", "row"), ("w_ssm_out", "col"), ("w_conv_out", "col"), ("w_o", "row"),
           ("w_up", "col"), ("w_down", "row")]
    shards = [_cast_bf16("cast_" + n, args[n][0]) for n, _ in big]
    small_in = _pack([conv_w[0], ffn_conv_w[0]])
    kind = dict(big)
    mixw, ffnw = ["w_glu", "w_ssm_out", "w_conv_out", "w_o"], ["w_up", "w_down"]
    shard = dict(zip([n for n, _ in big], shards))
    gathered = _launch("gather_in", "gather", [shard["w_in"], small_in], ["col", "row"])
    W = {"w_in": gathered[0]}
    W.update(zip(mixw, _launch("gather_mix", "gather", [shard[n] for n in mixw], [kind[n] for n in mixw])))
    W.update(zip(ffnw, _launch("gather_ffn", "gather", [shard[n] for n in ffnw], [kind[n] for n in ffnw])))
    cw_parts, fcw_parts = _unpack(gathered[-1].reshape(N_DEV, -1, LANES), [conv_w.shape[1:], ffn_conv_w.shape[1:]], (N_DEV,))
    conv_w_full = jnp.moveaxis(cw_parts, 0, 1).reshape(3, CW)
    ffn_conv_w_full = jnp.moveaxis(fcw_parts, 0, 1).reshape(3, FF)

    ar_row, ai_row = a_re.reshape(1, GP), a_im.reshape(1, GP)
    ldt_row = jnp.broadcast_to(log_dt.reshape(G, 1), (G, P)).reshape(1, GP)
    brt = jnp.transpose(b_re[0], (2, 0, 1)).reshape(H, GP)
    bit = jnp.transpose(b_im[0], (2, 0, 1)).reshape(H, GP)
    abar_re, abar_im, bbar_re, bbar_im = _prep_fwd("s5_prep", ar_row, ai_row, ldt_row, brt, bit)
    eye = jnp.eye(gpb, dtype=F32)

    def b_blocks(bt):
        return jnp.einsum("ab,hjbp->jahbp", eye, bt.reshape(H, nb, gpb, P)).reshape(nb, LANES, gpb * P)

    def c_blocks(c):
        return jnp.einsum("ab,jahp->jbpah", eye, c.reshape(nb, gpb, H, P)).reshape(nb, gpb * P, LANES)

    def diag_blocks(mat):
        return jnp.einsum("jahap->hjap", mat.reshape(nb, gpb, H, gpb, P))

    bm_re, bm_im = b_blocks(bbar_re), b_blocks(bbar_im)
    cm_re, cm_im = c_blocks(c_re[0]), -c_blocks(c_im[0])
    a3_re, a3_im = abar_re.reshape(-1, 8, LANES), abar_im.reshape(-1, 8, LANES)
    dskip_row = d_skip.reshape(1, SW)

    cbs = SW // LANES
    cb_v, cb_gb, cb_gc = cbs, cbs + CW // LANES, cbs + 2 * CW // LANES
    cb_ma = (SW + 3 * CW) // D
    xn = _rms_fwd("rms_tok", x2, norm_tok, tm)
    proj = _mm("proj", xn, W["w_in"], "nn", out_dtype=BF16)
    xs_re, xs_im, y, ya = _s5_fwd("s5_fwd", proj, bm_re.astype(BF16), bm_im.astype(BF16), cm_re.astype(BF16),
                                  cm_im.astype(BF16), dskip_row, a3_re, a3_im)
    g1 = _mm("glu_gate", ya, W["w_glu"], "nn", out_dtype=BF16)
    ya2 = _glu_fwd("glu", y, g1, tm)
    za = _mm("ssm_out", ya2, W["w_ssm_out"], "nn", out_dtype=BF16)
    q = _convb_fwd("convb", proj, cb_v, cb_gb, cb_gc, conv_w_full, conv_b)
    zb = _mm("conv_out", q, W["w_conv_out"], "nn", out_dtype=BF16)
    merged = _merge_fwd("merge", proj, cb_ma, cb_ma + 1, za, zb, tm)
    h1, hn = _mm_res_rms("mix_out_rms", merged, W["w_o"], x2, norm_ffn)
    hh, f = _ffn_up_act("ffn_up_act", hn, W["w_up"], ffn_conv_w_full, ffn_conv_b)
    o2 = _mm("ffn_down", f, W["w_down"], "nn", tk=2816)
    dh2, dh2b, g_norm_final, loss_part = _final("final", h1, o2, norm_final.reshape(1, D), tgt, tm)

    gw_down = _mm("gw_down", f, dh2b, "tn", out_dtype=BF16, tm=1408, tn=512, tk=L)
    dh2b, gw_down = lax.optimization_barrier((dh2b, gw_down))
    parts = {"w_down": _launch("grads_down", "exchange", [gw_down], [kind["w_down"]])[0]}
    dhh, g_ffn_conv_w, g_ffn_conv_b = _ffn_down_bwd_act("ffn_down_bwd_act", dh2b, W["w_down"], hh, ffn_conv_w_full,
                                                         ffn_conv_b)
    nhalf = lambda t: FF // t
    gw_up = _mm("gw_up", hn, dhh, "tn", out_dtype=BF16, tn=_tile(FF, 1024), tk=L, dims=(D, 2 * FF, L),
                b_spec=lambda a, b, c: pl.BlockSpec((None, c, b), lambda i, j, k: (j // nhalf(b), k, j % nhalf(b))))
    dhh, gw_up = lax.optimization_barrier((dhh, gw_up))
    parts["w_up"] = _launch("grads_up", "exchange", [gw_up], [kind["w_up"]])[0]
    dhn = _mm("d_ffn_in", dhh, W["w_up"], "nt", out_dtype=BF16, tk=_tile(FF, 2816), dims=(L, D, 2 * FF),
              a_spec=lambda a, b, c: pl.BlockSpec((None, a, c), lambda i, j, k: (k // nhalf(c), i, k % nhalf(c))))
    dh1, dh1b, g_norm_ffn = _rms_bwd("rms_ffn_bwd", dhn, h1, norm_ffn, dh2, tm, True)

    dmerged = _mm("d_merged", dh1b, W["w_o"], "nt", out_dtype=BF16)
    gw_o = _mm("gw_o", merged, dh1b, "tn", out_dtype=BF16, tk=L)
    dmerged, gw_o = lax.optimization_barrier((dmerged, gw_o))
    dza, dzb, dma, dmb = _merge_bwd("merge_bwd", proj, cb_ma, cb_ma + 1, za, zb, dmerged, tm)
    dq = _mm("d_q", dzb, W["w_conv_out"], "nt", out_dtype=BF16)
    gw_conv_out = _mm("gw_conv_out", q, dzb, "tn", out_dtype=BF16, tk=L)
    dq, gw_conv_out = lax.optimization_barrier((dq, gw_conv_out))
    dv, dgb, dgc, g_conv_w, g_conv_b = _convb_bwd("convb_bwd", proj, cb_v, cb_gb, cb_gc, conv_w_full, conv_b, dq)
    dya2 = _mm("d_ya2", dza, W["w_ssm_out"], "nt", out_dtype=BF16)
    gw_ssm_out = _mm("gw_ssm_out", ya2, dza, "tn", out_dtype=BF16, tk=L)
    dya2, gw_ssm_out = lax.optimization_barrier((dya2, gw_ssm_out))
    dy_direct, dg1 = _glu_bwd("glu_bwd", y, g1, dya2, tm)
    dya_g = _mm("d_ya_gate", dg1, W["w_glu"], "nt", out_dtype=BF16)
    gw_glu = _mm("gw_glu", ya, dg1, "tn", out_dtype=BF16, tk=L)
    dya_g, gw_glu = lax.optimization_barrier((dya_g, gw_glu))
    parts_mix = _launch("grads_mix", "exchange", [gw_glu, gw_ssm_out, gw_conv_out, gw_o], [kind[n] for n in mixw])
    dyb, g_dskip = _gelu_bwd("gelu_bwd", y, dy_direct, dya_g, proj, tm)
    swap = lambda m: jnp.swapaxes(m, 1, 2).astype(BF16)
    du, gb_re, gb_im, gc_re, gc_im, dab_re, dab_im = _s5_bwd(
        "s5_bwd", dyb, proj, xs_re, xs_im, swap(cm_re), swap(cm_im), swap(bm_re), swap(bm_im), dskip_row, a3_re, a3_im)
    g_ar, g_ai, g_ldt, g_brt, g_bit = _prep_bwd(
        "s5_prep_bwd", ar_row, ai_row, ldt_row, brt, bit, dab_re.reshape(1, GP), dab_im.reshape(1, GP),
        diag_blocks(gb_re).reshape(H, GP), diag_blocks(gb_im).reshape(H, GP))
    small = dict(
        a_re=g_ar.reshape(1, G, P), a_im=g_ai.reshape(1, G, P),
        log_dt=g_ldt.reshape(G, P).sum(axis=1).reshape(1, G),
        b_re=jnp.transpose(g_brt.reshape(H, G, P), (1, 2, 0))[None], b_im=jnp.transpose(g_bit.reshape(H, G, P), (1, 2, 0))[None],
        c_re=jnp.transpose(diag_blocks(gc_re), (1, 2, 0, 3)).reshape(1, G, H, P),
        c_im=-jnp.transpose(diag_blocks(gc_im), (1, 2, 0, 3)).reshape(1, G, H, P),
        d_skip=g_dskip.reshape(1, G, H), conv_b=g_conv_b, norm_ffn=g_norm_ffn, ffn_conv_b=g_ffn_conv_b,
        norm_final=g_norm_final.reshape(D), conv_w=g_conv_w[None], ffn_conv_w=g_ffn_conv_w[None])
    rep = ["a_re", "a_im", "log_dt", "b_re", "b_im", "c_re", "c_im", "d_skip", "conv_b", "norm_ffn", "ffn_conv_b", "norm_final"]
    order = rep + ["conv_w", "ffn_conv_w"]
    full_shapes = {n: args[n].shape for n in rep}
    full_shapes["conv_w"], full_shapes["ffn_conv_w"] = (1, 3, CW), (1, 3, FF)
    rep_pack = _pack([small[n] for n in rep], LANES)
    rep_rows = rep_pack.shape[0]
    gpack = jnp.concatenate([loss_part, rep_pack, _pack([small["conv_w"], small["ffn_conv_w"]])], axis=0)
    rep0 = loss_part.shape[0]
    rows = gpack.shape[0]
    du, gpack = lax.optimization_barrier((du, gpack))
    gall = _launch("grads_small", "gather", [gpack], ["row"])[0]

    dproj = _concat_cols("dproj", [du, dv, dgb, dgc, dma, dmb], tm)
    gw_in = _mm("gw_in", xn, dproj, "tn", out_dtype=BF16, tk=L)
    dproj, gw_in = lax.optimization_barrier((dproj, gw_in))
    parts_in = _launch("grads_in", "exchange", [gw_in], ["col"])
    dxn = _mm("d_xn", dproj, W["w_in"], "nt", out_dtype=BF16, tk=4096)
    grad_x, g_norm_tok = _rms_bwd("rms_tok_bwd", dxn, x2, norm_tok, dh1, tm, False)

    res = {}

    def big_update(n):
        res[n] = [r[None] for r in _adamw("adamw_" + n, parts[n], args[n][0], args["m_" + n][0], args["v_" + n][0])]

    def after(xs, dep):
        return lax.optimization_barrier((list(xs), dep))[0]

    parts["w_down"] = after([parts["w_down"]], grad_x)[0]
    big_update("w_down")
    parts["w_up"] = after([parts["w_up"]], res["w_down"][1])[0]
    big_update("w_up")
    parts.update(zip(mixw, after(parts_mix, [res[n][1] for n in ffnw])))
    for n in mixw:
        big_update(n)
    gall = after([gall], [res[n][1] for n in mixw])[0].reshape(N_DEV, rows, LANES)
    gcw, gfcw = _unpack(gall[:, rep0 + rep_rows:], [full_shapes["conv_w"], full_shapes["ffn_conv_w"]], (N_DEV,))
    cws, fcws = CW // N_DEV, FF // N_DEV
    gcw = lax.dynamic_slice_in_dim(gcw[:, 0], me * cws, cws, axis=2)
    gfcw = lax.dynamic_slice_in_dim(gfcw[:, 0], me * fcws, fcws, axis=2)
    res["conv_w"] = [r[None] for r in _adamw("adamw_conv_w", gcw, conv_w[0], m_conv_w[0], v_conv_w[0])]
    res["ffn_conv_w"] = [r[None] for r in _adamw("adamw_ffn_conv_w", gfcw, ffn_conv_w[0], m_ffn_conv_w[0], v_ffn_conv_w[0])]
    rep_out = _adamw("adamw_small", gall[:, rep0:rep0 + rep_rows], _pack([args[n] for n in rep], LANES),
                     _pack([args["m_" + n] for n in rep], LANES), _pack([args["v_" + n] for n in rep], LANES))
    rep_out = [_unpack(r, [full_shapes[n] for n in rep]) for r in rep_out]
    for i, n in enumerate(rep):
        res[n] = [r[i] for r in rep_out]
    nt_pack = after([_pack([g_norm_tok])], [res[n][1] for n in ("a_re", "conv_w", "ffn_conv_w")])
    nt_all = _all_gather("gather_norm_tok_grad", nt_pack, ["row"])[0].reshape(N_DEV, -1, LANES)
    nt_out = _adamw("adamw_norm_tok", nt_all, _pack([norm_tok]), _pack([m_norm_tok]), _pack([v_norm_tok]))
    res["norm_tok"] = [_unpack(r, [norm_tok.shape])[0] for r in nt_out]
    parts["w_in"] = after(parts_in, nt_out[0])[0]
    big_update("w_in")

    loss = jnp.sum(gall[:, 0, 0])
    names = ["norm_tok", "w_in", "a_re", "a_im", "log_dt", "b_re", "b_im", "c_re", "c_im", "d_skip", "w_glu", "w_ssm_out",
             "conv_w", "conv_b", "w_conv_out", "w_o", "norm_ffn", "w_up", "ffn_conv_w", "ffn_conv_b", "w_down", "norm_final"]
    out = [loss, grad_x[None]]
    for slot in range(4):
        out += [res[n][slot] for n in names]
    return tuple(out)
```

```python
import jax
import jax.numpy as jnp
from jax import lax
from jax.experimental import pallas as pl
from jax.experimental.pallas import tpu as pltpu
from jax.experimental.pallas import tpu_sc as plsc

F32 = jnp.float32
BF16 = jnp.bfloat16
N_DEV = 8
LANES = 128
SLAB = 4
EPS = 1e-6
ADAM_LR = 0.001
ADAM_B1 = 0.9
ADAM_B2 = 0.999
ADAM_EPS = 1e-08
ADAM_WD = 0.01
ADAM_STEP = 10
VMEM_LIMIT = 56 * 1024 * 1024
MESH = pl.DeviceIdType.MESH


def _tile(n, pref, mult=LANES):
    best = None
    t = mult
    while t <= min(n, pref):
        if n % t == 0:
            best = t
        t += mult
    return best if best is not None else n


def _params(ndim):
    return pltpu.CompilerParams(dimension_semantics=("arbitrary",) * ndim, vmem_limit_bytes=VMEM_LIMIT)


def _sds(shape, dtype):
    return jax.ShapeDtypeStruct(tuple(shape), dtype)


def _mm(name, a, b, mode, *, out_dtype=F32, tm=1024, tn=1024, tk=2048, dims=None, a_spec=None, b_spec=None):
    if dims is None:
        if mode == "nn":
            (M, K), N = a.shape, b.shape[1]
        elif mode == "nt":
            (M, K), N = a.shape, b.shape[0]
        else:
            (K, M), N = a.shape, b.shape[1]
    else:
        M, N, K = dims
    tm, tn, tk = _tile(M, tm), _tile(N, tn), _tile(K, tk)
    nk = K // tk
    if mode == "nn":
        dn = (((1,), (0,)), ((), ()))
        sa = pl.BlockSpec((tm, tk), lambda i, j, k: (i, k))
        sb = pl.BlockSpec((tk, tn), lambda i, j, k: (k, j))
    elif mode == "nt":
        dn = (((1,), (1,)), ((), ()))
        sa = pl.BlockSpec((tm, tk), lambda i, j, k: (i, k))
        sb = pl.BlockSpec((tn, tk), lambda i, j, k: (j, k))
    else:
        dn = (((0,), (0,)), ((), ()))
        sa = pl.BlockSpec((tk, tm), lambda i, j, k: (k, i))
        sb = pl.BlockSpec((tk, tn), lambda i, j, k: (k, j))
    sa = a_spec(tm, tn, tk) if a_spec is not None else sa
    sb = b_spec(tm, tn, tk) if b_spec is not None else sb
    use_acc = nk > 1 and out_dtype != F32

    def body(a_ref, b_ref, o_ref, *acc):
        k = pl.program_id(2)
        p = lax.dot_general(a_ref[...], b_ref[...], dn, preferred_element_type=F32)
        if nk == 1:
            o_ref[...] = p.astype(out_dtype)
        else:
            tgt = acc[0] if use_acc else o_ref

            @pl.when(k == 0)
            def _():
                tgt[...] = p

            @pl.when(k > 0)
            def _():
                tgt[...] += p

            if use_acc:
                @pl.when(k == nk - 1)
                def _():
                    o_ref[...] = acc[0][...].astype(out_dtype)

    return pl.pallas_call(
        body, name=name, grid=(M // tm, N // tn, nk),
        in_specs=[sa, sb], out_specs=pl.BlockSpec((tm, tn), lambda i, j, k: (i, j)),
        out_shape=_sds((M, N), out_dtype),
        scratch_shapes=[pltpu.VMEM((tm, tn), F32)] if use_acc else [],
        compiler_params=_params(3),
    )(a, b)


def _mm_rows(name, a, b, extra, outs, epilogue, *, tm, tk):
    M, K = a.shape
    N = b.shape[1]
    tm, tk = _tile(M, tm, 16), _tile(K, tk)
    nk = K // tk
    ne, no = len(extra), len(outs)

    def body(a_ref, b_ref, *rest):
        i, k = pl.program_id(0), pl.program_id(1)
        p = jnp.dot(a_ref[...], b_ref[...], preferred_element_type=F32)
        if nk == 1:
            epilogue(i, p, rest[:ne], rest[ne:ne + no])
            return
        acc = rest[ne + no]
        _acc_rows(k, acc, p)

        @pl.when(k == nk - 1)
        def _():
            epilogue(i, acc, rest[:ne], rest[ne:ne + no])

    return pl.pallas_call(
        body, name=name, grid=(M // tm, nk),
        in_specs=[pl.BlockSpec((tm, tk), lambda i, k: (i, k)), pl.BlockSpec((tk, N), lambda i, k: (k, 0))]
        + [s for _, s in extra],
        out_specs=[s for _, s in outs], out_shape=[o for o, _ in outs],
        scratch_shapes=[pltpu.VMEM((tm, N), F32)] if nk > 1 else [], compiler_params=_params(2),
    )(a, b, *[e for e, _ in extra])


def _rows(name, body, L, tm, ins, outs):
    return pl.pallas_call(
        body, name=name, grid=(L // tm,),
        in_specs=[s for _, s in ins], out_specs=[s for _, s in outs],
        out_shape=[o for o, _ in outs], compiler_params=_params(1),
    )(*[a for a, _ in ins])


def _rs(tm, w, cb=0):
    return pl.BlockSpec((tm, w), lambda i: (i, cb))


def _fs(shape):
    return pl.BlockSpec(tuple(shape), lambda i: (0,) * len(shape))


def _acc_rows(i, ref, part):
    @pl.when(i == 0)
    def _():
        ref[...] = part

    @pl.when(i > 0)
    def _():
        ref[...] += part


def _cast_bf16(name, w):
    R, C = w.shape
    tr = _tile(R, max(16, (1 << 20) // C), 16)

    def body(w_ref, o_ref):
        o_ref[...] = w_ref[...].astype(BF16)

    return _rows(name, body, R, tr, [(w, _rs(tr, C))], [(_sds((R, C), BF16), _rs(tr, C))])[0]


def _concat_cols(name, pieces, tm):
    L = pieces[0].shape[0]
    widths = [p.shape[1] for p in pieces]

    def body(*refs):
        o_ref, off = refs[-1], 0
        for p_ref, w in zip(refs[:-1], widths):
            o_ref[:, off:off + w] = p_ref[...]
            off += w

    return _rows(name, body, L, tm, [(p, _rs(tm, w)) for p, w in zip(pieces, widths)],
                 [(_sds((L, sum(widths)), pieces[0].dtype), _rs(tm, sum(widths)))])[0]


def _rms_fwd(name, x, g, tm):
    L, D = x.shape

    def body(x_ref, g_ref, o_ref):
        xv = x_ref[...]
        r = lax.rsqrt(jnp.mean(xv * xv, axis=-1, keepdims=True) + EPS)
        o_ref[...] = (xv * r * g_ref[...]).astype(BF16)

    return _rows(name, body, L, tm, [(x, _rs(tm, D)), (g, _fs((1, D)))], [(_sds((L, D), BF16), _rs(tm, D))])[0]


def _mm_res_rms(name, a, w, x, g):
    L, D = x.shape

    def epilogue(i, o, extra, outs):
        x_ref, g_ref = extra
        for r0 in range(0, tm, sub):
            rs = slice(r0, r0 + sub)
            h = x_ref[rs, :] + o[rs, :]
            r = lax.rsqrt(jnp.mean(h * h, axis=-1, keepdims=True) + EPS)
            outs[0][rs, :] = h
            outs[1][rs, :] = (h * r * g_ref[...]).astype(BF16)

    tm = _tile(L, 512, 16)
    sub = _tile(tm, LANES, 16)
    row = pl.BlockSpec((tm, D), lambda i, k: (i, 0))
    return _mm_rows(name, a, w, [(x, row), (g, pl.BlockSpec((1, D), lambda i, k: (0, 0)))],
                    [(_sds((L, D), F32), row), (_sds((L, D), BF16), row)], epilogue, tm=tm, tk=a.shape[1])


def _rms_bwd(name, dn, h, g, dres, tm, with_bf16):
    L, D = h.shape

    def body(dn_ref, h_ref, g_ref, dres_ref, dh_ref, *rest):
        i = pl.program_id(0)
        h = h_ref[...]
        r = lax.rsqrt(jnp.mean(h * h, axis=-1, keepdims=True) + EPS)
        xh = h * r
        d = dn_ref[...].astype(F32)
        dxh = d * g_ref[...]
        dh = dres_ref[...] + r * (dxh - xh * jnp.mean(dxh * xh, axis=-1, keepdims=True))
        dh_ref[...] = dh
        if with_bf16:
            rest[0][...] = dh.astype(BF16)
        _acc_rows(i, rest[-1], jnp.sum(d * xh, axis=0, keepdims=True))

    outs = [(_sds((L, D), F32), _rs(tm, D))]
    if with_bf16:
        outs.append((_sds((L, D), BF16), _rs(tm, D)))
    outs.append((_sds((1, D), F32), _fs((1, D))))
    return _rows(name, body, L, tm, [(dn, _rs(tm, D)), (h, _rs(tm, D)), (g, _fs((1, D))), (dres, _rs(tm, D))], outs)


def _final(name, h1, o2, g, tgt, tm):
    L, D = h1.shape

    def body(h1_ref, o2_ref, g_ref, t_ref, dh_ref, dhb_ref, dg_ref, loss_ref):
        i = pl.program_id(0)
        h = h1_ref[...] + o2_ref[...]
        r = lax.rsqrt(jnp.mean(h * h, axis=-1, keepdims=True) + EPS)
        xh = h * r
        gv = g_ref[...]
        e = xh * gv - t_ref[...]
        part = 0.5 * jnp.sum(jnp.mean(e * e, axis=-1, keepdims=True), axis=0, keepdims=True)
        dy = e / D
        dxh = dy * gv
        dh = r * (dxh - xh * jnp.mean(dxh * xh, axis=-1, keepdims=True))
        dh_ref[...] = dh
        dhb_ref[...] = dh.astype(BF16)
        _acc_rows(i, dg_ref, jnp.sum(dy * xh, axis=0, keepdims=True))
        _acc_rows(i, loss_ref, jnp.broadcast_to(part, (8, LANES)))

    return _rows(name, body, L, tm,
                 [(h1, _rs(tm, D)), (o2, _rs(tm, D)), (g, _fs((1, D))), (tgt, _rs(tm, D))],
                 [(_sds((L, D), F32), _rs(tm, D)), (_sds((L, D), BF16), _rs(tm, D)),
                  (_sds((1, D), F32), _fs((1, D))), (_sds((8, LANES), F32), _fs((8, LANES)))])


def _gelu_and_grad(x):
    c, k = 0.7978845608028654, 0.044715
    x2 = x * x
    t = jnp.tanh(c * x * (1.0 + k * x2))
    half = 0.5 * x
    return half * (1.0 + t), 0.5 * (1.0 + t) + half * (1.0 - t * t) * (c * (1.0 + 3.0 * k * x2))


def _glu_fn(y, g1):
    ya = jax.nn.gelu(y)
    return ya * jax.nn.sigmoid(g1)


def _mm_glu(name, ya, w, y):
    L, W = y.shape

    def epilogue(i, o, extra, outs):
        for r0 in range(0, tm, sub):
            rs = slice(r0, r0 + sub)
            g16 = o[rs, :].astype(BF16)
            outs[0][rs, :] = g16
            outs[1][rs, :] = _glu_fn(extra[0][rs, :], g16.astype(F32)).astype(BF16)

    tm = _tile(L, 512, 16)
    sub = _tile(tm, LANES, 16)
    row = pl.BlockSpec((tm, W), lambda i, k: (i, 0))
    return _mm_rows(name, ya, w, [(y, row)], [(_sds((L, W), BF16), row), (_sds((L, W), BF16), row)], epilogue,
                    tm=tm, tk=ya.shape[1])


def _glu_bwd(name, y, g1, dya2, tm):
    L, W = y.shape

    def body(y_ref, g_ref, d_ref, dy_ref, dg_ref):
        _, vjp = jax.vjp(_glu_fn, y_ref[...], g_ref[...].astype(F32))
        dy, dg = vjp(d_ref[...].astype(F32))
        dy_ref[...] = dy
        dg_ref[...] = dg.astype(BF16)

    return _rows(name, body, L, tm, [(y, _rs(tm, W)), (g1, _rs(tm, W)), (dya2, _rs(tm, W))],
                 [(_sds((L, W), F32), _rs(tm, W)), (_sds((L, W), BF16), _rs(tm, W))])


def _gelu_bwd(name, y, dy_direct, dya_g, proj, tm):
    L, W = y.shape

    def body(y_ref, dd_ref, dg_ref, u_ref, dyb_ref, dsk_ref):
        i = pl.program_id(0)
        dy = dd_ref[...] + dg_ref[...].astype(F32) * _gelu_and_grad(y_ref[...])[1]
        dyb_ref[...] = dy.astype(BF16)
        _acc_rows(i, dsk_ref, jnp.sum(dy * u_ref[...].astype(F32), axis=0, keepdims=True))

    return _rows(name, body, L, tm,
                 [(y, _rs(tm, W)), (dy_direct, _rs(tm, W)), (dya_g, _rs(tm, W)), (proj, _rs(tm, W, 0))],
                 [(_sds((L, W), BF16), _rs(tm, W)), (_sds((1, W), F32), _fs((1, W)))])


def _merge_fn(ma, mb, za, zb):
    return jax.nn.sigmoid(ma) * za + jax.nn.sigmoid(mb) * zb


def _mm_merge(name, q, w, proj, cb_a, cb_b, za):
    L, D = za.shape

    def epilogue(i, o, extra, outs):
        ma_ref, mb_ref, za_ref = extra
        for r0 in range(0, tm, sub):
            rs = slice(r0, r0 + sub)
            zb16 = o[rs, :].astype(BF16)
            outs[0][rs, :] = zb16
            outs[1][rs, :] = _merge_fn(ma_ref[rs, :].astype(F32), mb_ref[rs, :].astype(F32), za_ref[rs, :].astype(F32),
                                       zb16.astype(F32)).astype(BF16)

    tm = _tile(L, 512, 16)
    sub = _tile(tm, LANES, 16)
    row = pl.BlockSpec((tm, D), lambda i, k: (i, 0))
    col = lambda cb: pl.BlockSpec((tm, D), lambda i, k: (i, cb))
    return _mm_rows(name, q, w, [(proj, col(cb_a)), (proj, col(cb_b)), (za, row)],
                    [(_sds((L, D), BF16), row), (_sds((L, D), BF16), row)], epilogue, tm=tm, tk=q.shape[1])


def _merge_bwd(name, proj, cb_a, cb_b, za, zb, dmerged, tm):
    L, D = za.shape

    def body(ma_ref, mb_ref, za_ref, zb_ref, d_ref, dza_ref, dzb_ref, dma_ref, dmb_ref):
        _, vjp = jax.vjp(_merge_fn, ma_ref[...].astype(F32), mb_ref[...].astype(F32), za_ref[...].astype(F32),
                         zb_ref[...].astype(F32))
        dma, dmb, dza, dzb = vjp(d_ref[...].astype(F32))
        dza_ref[...] = dza.astype(BF16)
        dzb_ref[...] = dzb.astype(BF16)
        dma_ref[...] = dma.astype(BF16)
        dmb_ref[...] = dmb.astype(BF16)

    return _rows(name, body, L, tm,
                 [(proj, _rs(tm, D, cb_a)), (proj, _rs(tm, D, cb_b)), (za, _rs(tm, D)), (zb, _rs(tm, D)),
                  (dmerged, _rs(tm, D))],
                 [(_sds((L, D), BF16), _rs(tm, D)), (_sds((L, D), BF16), _rs(tm, D)),
                  (_sds((L, D), BF16), _rs(tm, D)), (_sds((L, D), BF16), _rs(tm, D))])


def _shift_down(x, k):
    row = lax.broadcasted_iota(jnp.int32, x.shape, 0)
    return jnp.where(row >= k, pltpu.roll(x, k, axis=0), 0.0)


def _shift_up(x, k):
    n = x.shape[0]
    row = lax.broadcasted_iota(jnp.int32, x.shape, 0)
    return jnp.where(row < n - k, pltpu.roll(x, n - k, axis=0), 0.0)


def _conv3(cv, w_ref, b_ref):
    return (w_ref[2:3, :] * cv + w_ref[1:2, :] * _shift_down(cv, 1) + w_ref[0:1, :] * _shift_down(cv, 2)
            + b_ref[...])


def _conv3_bwd(dcc, cv, w_ref):
    dcv = w_ref[2:3, :] * dcc + w_ref[1:2, :] * _shift_up(dcc, 1) + w_ref[0:1, :] * _shift_up(dcc, 2)
    dw = [jnp.sum(dcc * _shift_down(cv, 2), axis=0, keepdims=True),
          jnp.sum(dcc * _shift_down(cv, 1), axis=0, keepdims=True),
          jnp.sum(dcc * cv, axis=0, keepdims=True)]
    db = jnp.sum(dcc, axis=0, keepdims=True)
    return dcv, dw, db


def _store_rows(ref, rows):
    for r, val in enumerate(rows):
        ref[r:r + 1, :] = val


def _cols(name, body, ncb, ins, outs):
    return pl.pallas_call(
        body, name=name, grid=(ncb,),
        in_specs=[s for _, s in ins], out_specs=[s for _, s in outs],
        out_shape=[o for o, _ in outs], compiler_params=_params(1),
    )(*[a for a, _ in ins])


def _cb(L, w, off=0):
    return pl.BlockSpec((L, w), lambda j: (0, j + off))


def _convb_fwd(name, proj, cb_v, cb_gb, cb_gc, w, b):
    L = proj.shape[0]
    W = w.shape[1]
    c = LANES

    def body(v_ref, gb_ref, gc_ref, w_ref, b_ref, q_ref):
        cc = _conv3(gc_ref[...].astype(F32) * v_ref[...].astype(F32), w_ref, b_ref)
        q_ref[...] = (gb_ref[...].astype(F32) * cc).astype(BF16)

    return _cols(name, body, W // c,
                 [(proj, _cb(L, c, cb_v)), (proj, _cb(L, c, cb_gb)), (proj, _cb(L, c, cb_gc)),
                  (w, _cb(3, c)), (b, _cb(1, c))],
                 [(_sds((L, W), BF16), _cb(L, c))])[0]


def _convb_bwd(name, proj, cb_v, cb_gb, cb_gc, w, b, dq):
    L = proj.shape[0]
    W = w.shape[1]
    c = LANES

    def body(v_ref, gb_ref, gc_ref, w_ref, b_ref, dq_ref, dv_ref, dgb_ref, dgc_ref, dw_ref, db_ref):
        v, gc = v_ref[...].astype(F32), gc_ref[...].astype(F32)
        cv = gc * v
        cc = _conv3(cv, w_ref, b_ref)
        dq = dq_ref[...].astype(F32)
        dgb_ref[...] = (dq * cc).astype(BF16)
        dcv, dw, db = _conv3_bwd(dq * gb_ref[...].astype(F32), cv, w_ref)
        dv_ref[...] = (dcv * gc).astype(BF16)
        dgc_ref[...] = (dcv * v).astype(BF16)
        _store_rows(dw_ref, dw)
        db_ref[...] = db

    return _cols(name, body, W // c,
                 [(proj, _cb(L, c, cb_v)), (proj, _cb(L, c, cb_gb)), (proj, _cb(L, c, cb_gc)),
                  (w, _cb(3, c)), (b, _cb(1, c)), (dq, _cb(L, c))],
                 [(_sds((L, W), BF16), _cb(L, c)), (_sds((L, W), BF16), _cb(L, c)), (_sds((L, W), BF16), _cb(L, c)),
                  (_sds((3, W), F32), _cb(3, c)), (_sds((1, W), F32), _cb(1, c))])


HALO = 16


def _ffn_up_act(name, hn, w_up, w, b):
    L, D = hn.shape
    Fw = w.shape[1]
    tm, tc = _tile(L, 1024, HALO), _tile(Fw, 512)
    ncb = Fw // tc

    def body(x_ref, wa_ref, wg_ref, w_ref, b_ref, hh_ref, f_ref, carry):
        i, j = pl.program_id(0), pl.program_id(1)
        a16 = jnp.dot(x_ref[...], wa_ref[...], preferred_element_type=F32).astype(BF16)
        g16 = jnp.dot(x_ref[...], wg_ref[...], preferred_element_type=F32).astype(BF16)
        hh_ref[0] = a16
        hh_ref[1] = g16
        for c0 in range(0, tc, LANES):
            cs = slice(c0, c0 + LANES)
            prev = jnp.where(i == 0, 0.0, carry[j, :, cs])
            x = jnp.concatenate([prev, a16[:, cs].astype(F32)], axis=0)
            n = x.shape[0]
            a = (w_ref[2:3, cs] * x + w_ref[1:2, cs] * pltpu.roll(x, 1, axis=0) + w_ref[0:1, cs] * pltpu.roll(x, 2, axis=0)
                 + b_ref[:, cs])[8:n]
            f_ref[:, cs] = (_gelu_and_grad(a)[0] * g16[:, cs].astype(F32)).astype(BF16)
            carry[j, :, cs] = x[n - 8:n]

    return pl.pallas_call(
        body, name=name, grid=(L // tm, ncb),
        in_specs=[pl.BlockSpec((tm, D), lambda i, j: (i, 0)), pl.BlockSpec((D, tc), lambda i, j: (0, j)),
                  pl.BlockSpec((D, tc), lambda i, j: (0, j + ncb)),
                  pl.BlockSpec((3, tc), lambda i, j: (0, j)), pl.BlockSpec((1, tc), lambda i, j: (0, j))],
        out_specs=[pl.BlockSpec((2, tm, tc), lambda i, j: (0, i, j)), pl.BlockSpec((tm, tc), lambda i, j: (i, j))],
        out_shape=[_sds((2, L, Fw), BF16), _sds((L, Fw), BF16)],
        scratch_shapes=[pltpu.VMEM((ncb, 8, tc), F32)], compiler_params=_params(2),
    )(hn, w_up, w_up, w, b)


def _ffn_down_bwd_act(name, dy, w_down, hh, w, b):
    L, D = dy.shape
    Fw = w.shape[1]
    tm, tc = _tile(L, 512, HALO), _tile(Fw, 512)
    ncb, nrt, rpt = Fw // tc, L // tm, tm // HALO
    dn = (((1,), (1,)), ((), ()))

    def body(dy_ref, wd_ref, a_ref, ap_ref, h2_ref, w_ref, b_ref, dhh_ref, dw_ref, db_ref, carry, acc):
        i, j = pl.program_id(0), pl.program_id(1)
        first_rows = i == nrt - 1
        d16 = lax.dot_general(dy_ref[...], wd_ref[...], dn, preferred_element_type=F32).astype(BF16)
        for c0 in range(0, tc, LANES):
            cs = slice(c0, c0 + LANES)
            h1 = jnp.concatenate([jnp.where(first_rows, 0.0, ap_ref[:, cs].astype(F32)), a_ref[:, cs].astype(F32)], axis=0)
            s1, s2 = pltpu.roll(h1, 1, axis=0), pltpu.roll(h1, 2, axis=0)
            n = h1.shape[0]
            a = (w_ref[2:3, cs] * h1 + w_ref[1:2, cs] * s1 + w_ref[0:1, cs] * s2 + b_ref[:, cs])[HALO:n]
            ga, dga = _gelu_and_grad(a)
            d = d16[:, cs].astype(F32)
            da = d * h2_ref[:, cs].astype(F32) * dga
            dae = jnp.concatenate([da, jnp.where(i == 0, 0.0, carry[j, :, cs])], axis=0)
            m = dae.shape[0]
            dh1 = w_ref[2:3, cs] * dae + w_ref[1:2, cs] * pltpu.roll(dae, m - 1, axis=0) + w_ref[0:1, cs] * pltpu.roll(dae, m - 2, axis=0)
            dhh_ref[0, :, cs] = dh1[0:tm].astype(BF16)
            dhh_ref[1, :, cs] = (d * ga).astype(BF16)
            carry[j, :, cs] = da[0:8]
            rows = [jnp.sum(da * s2[HALO:n], axis=0, keepdims=True), jnp.sum(da * s1[HALO:n], axis=0, keepdims=True),
                    jnp.sum(da * h1[HALO:n], axis=0, keepdims=True), jnp.sum(da, axis=0, keepdims=True)]
            for r in range(4):
                tot = jnp.where(i == 0, 0.0, acc[j, r:r + 1, cs]) + rows[r]
                acc[j, r:r + 1, cs] = tot
                if r < 3:
                    dw_ref[r:r + 1, cs] = tot
                else:
                    db_ref[:, cs] = tot

    rt = lambda i: nrt - 1 - i
    dhh, dw, db = pl.pallas_call(
        body, name=name, grid=(nrt, ncb),
        in_specs=[pl.BlockSpec((tm, D), lambda i, j: (rt(i), 0)), pl.BlockSpec((tc, D), lambda i, j: (j, 0)),
                  pl.BlockSpec((None, tm, tc), lambda i, j: (0, rt(i), j)),
                  pl.BlockSpec((None, HALO, tc), lambda i, j: (0, jnp.maximum(rt(i) * rpt - 1, 0), j)),
                  pl.BlockSpec((None, tm, tc), lambda i, j: (1, rt(i), j)),
                  pl.BlockSpec((3, tc), lambda i, j: (0, j)), pl.BlockSpec((1, tc), lambda i, j: (0, j))],
        out_specs=[pl.BlockSpec((2, tm, tc), lambda i, j: (0, rt(i), j)),
                   pl.BlockSpec((None, 3, tc), lambda i, j: (i, 0, j)), pl.BlockSpec((None, 1, tc), lambda i, j: (i, 0, j))],
        out_shape=[_sds((2, L, Fw), BF16), _sds((nrt, 3, Fw), F32), _sds((nrt, 1, Fw), F32)],
        scratch_shapes=[pltpu.VMEM((ncb, 8, tc), F32), pltpu.VMEM((ncb, 8, tc), F32)], compiler_params=_params(2),
    )(dy, w_down, hh, hh, hh, w, b)
    return dhh, dw[nrt - 1], db[nrt - 1]


def _prep_fn(ar, ai, ldt, brt, bit):
    dt = jnp.exp(ldt)
    mag = jnp.exp(dt * ar)
    are = mag * jnp.cos(dt * ai)
    aim = mag * jnp.sin(dt * ai)
    nr = are - 1.0
    ni = aim
    den = ar * ar + ai * ai
    fr = (nr * ar + ni * ai) / den
    fi = (ni * ar - nr * ai) / den
    return are, aim, fr * brt - fi * bit, fr * bit + fi * brt


def _prep_fwd(name, ar, ai, ldt, brt, bit):
    def body(ar_ref, ai_ref, l_ref, br_ref, bi_ref, o1, o2, o3, o4):
        o1[...], o2[...], o3[...], o4[...] = _prep_fn(ar_ref[...], ai_ref[...], l_ref[...], br_ref[...], bi_ref[...])

    return pl.pallas_call(body, name=name,
                          out_shape=[_sds(ar.shape, F32), _sds(ar.shape, F32), _sds(brt.shape, F32), _sds(brt.shape, F32)],
                          )(ar, ai, ldt, brt, bit)


def _prep_bwd(name, ar, ai, ldt, brt, bit, g1, g2, g3, g4):
    def body(ar_ref, ai_ref, l_ref, br_ref, bi_ref, g1_ref, g2_ref, g3_ref, g4_ref, o1, o2, o3, o4, o5):
        _, vjp = jax.vjp(_prep_fn, ar_ref[...], ai_ref[...], l_ref[...], br_ref[...], bi_ref[...])
        o1[...], o2[...], o3[...], o4[...], o5[...] = vjp((g1_ref[...], g2_ref[...], g3_ref[...], g4_ref[...]))

    return pl.pallas_call(body, name=name,
                          out_shape=[_sds(ar.shape, F32)] * 3 + [_sds(brt.shape, F32)] * 2,
                          )(ar, ai, ldt, brt, bit, g1, g2, g3, g4)


def _scan_steps(tc, pitch, ng, reverse, a_r, a_i, stage_b, stage_x, out_re, out_im, st_re, st_im, acc):
    def step(tt, carry):
        t = (tc - 1 - tt) if reverse else tt
        new, sums = [], []
        for g in range(ng):
            rows = pl.ds(g * 8 * pitch + t, 8, stride=pitch)
            cr, ci = carry[2 * g], carry[2 * g + 1]
            br, bi = stage_b[0][rows, :], stage_b[1][rows, :]
            if reverse:
                xr, xi = stage_x[0][rows, :], stage_x[1][rows, :]
                sums += [carry[2 * ng + 2 * g] + (xr * cr + xi * ci), carry[2 * ng + 2 * g + 1] + (xr * ci - xi * cr)]
                nr = a_r[g] * cr + a_i[g] * ci + br
                ni = a_r[g] * ci - a_i[g] * cr + bi
            else:
                nr = a_r[g] * cr - a_i[g] * ci + br
                ni = a_r[g] * ci + a_i[g] * cr + bi
            out_re[rows, :] = nr
            out_im[rows, :] = ni
            new += [nr, ni]
        return tuple(new + sums)

    init = []
    for g in range(ng):
        init += [st_re[g], st_im[g]]
    if reverse:
        for g in range(ng):
            init += [acc[0][g], acc[1][g]]
    fin = lax.fori_loop(0, tc, step, tuple(init), unroll=2)
    for g in range(ng):
        st_re[g] = fin[2 * g]
        st_im[g] = fin[2 * g + 1]
        if reverse:
            acc[0][g] = fin[2 * ng + 2 * g]
            acc[1][g] = fin[2 * ng + 2 * g + 1]


def _s5_fwd(name, proj, bm_re, bm_im, cm_re, cm_im, dskip, a_re, a_im):
    L = proj.shape[0]
    nb = bm_re.shape[0]
    ns, W = SLAB * nb, nb * LANES
    ng = ns // 8
    tc = min(2 * LANES, L)
    pitch = tc + 8
    wide = SLAB * LANES

    def body(u_ref, bre_ref, bim_ref, cre_ref, cim_ref, d_ref, ar_ref, ai_ref, xr_ref, xi_ref, y_ref, ya_ref,
             sb_re, sb_im, out_re, out_im, st_re, st_im):
        @pl.when(pl.program_id(0) == 0)
        def _():
            st_re[...] = jnp.zeros(st_re.shape, F32)
            st_im[...] = jnp.zeros(st_im.shape, F32)

        for j in range(nb):
            ub = u_ref[:, j * LANES:(j + 1) * LANES]
            r1 = jnp.dot(ub, bre_ref[j], preferred_element_type=F32)
            r2 = jnp.dot(ub, bim_ref[j], preferred_element_type=F32)
            for q in range(SLAB):
                sb_re[pl.ds((SLAB * j + q) * pitch, tc), :] = r1[:, q * LANES:(q + 1) * LANES]
                sb_im[pl.ds((SLAB * j + q) * pitch, tc), :] = r2[:, q * LANES:(q + 1) * LANES]
        a_r = [ar_ref[g] for g in range(ng)]
        a_i = [ai_ref[g] for g in range(ng)]
        _scan_steps(tc, pitch, ng, False, a_r, a_i, (sb_re, sb_im), None, out_re, out_im, st_re, st_im, None)
        for j in range(nb):
            x1 = [out_re[pl.ds((SLAB * j + q) * pitch, tc), :].astype(BF16) for q in range(SLAB)]
            x2 = [out_im[pl.ds((SLAB * j + q) * pitch, tc), :].astype(BF16) for q in range(SLAB)]
            for q in range(SLAB):
                xr_ref[SLAB * j + q] = x1[q]
                xi_ref[SLAB * j + q] = x2[q]
            cols = slice(j * LANES, (j + 1) * LANES)
            y = (jnp.dot(jnp.concatenate(x1, axis=1), cre_ref[j], preferred_element_type=F32)
                 + jnp.dot(jnp.concatenate(x2, axis=1), cim_ref[j], preferred_element_type=F32)
                 + d_ref[:, cols] * u_ref[:, cols].astype(F32))
            y_ref[:, cols] = y
            ya_ref[:, cols] = jax.nn.gelu(y).astype(BF16)

    full3 = lambda s: pl.BlockSpec(s, lambda i: (0, 0, 0))
    xs = pl.BlockSpec((ns, tc, LANES), lambda i: (0, i, 0))
    rows = pl.BlockSpec((tc, W), lambda i: (i, 0))
    return pl.pallas_call(
        body, name=name, grid=(L // tc,),
        in_specs=[rows, full3((nb, LANES, wide)), full3((nb, LANES, wide)), full3((nb, wide, LANES)),
                  full3((nb, wide, LANES)), pl.BlockSpec((1, W), lambda i: (0, 0)), full3((ng, 8, LANES)), full3((ng, 8, LANES))],
        out_specs=[xs, xs, rows, rows],
        out_shape=[_sds((ns, L, LANES), BF16)] * 2 + [_sds((L, W), F32), _sds((L, W), BF16)],
        scratch_shapes=[pltpu.VMEM((ns * pitch, LANES), F32)] * 4 + [pltpu.VMEM((ng, 8, LANES), F32)] * 2,
        compiler_params=_params(1),
    )(proj, bm_re, bm_im, cm_re, cm_im, dskip, a_re, a_im)


def _s5_bwd(name, dyb, proj, xs_re, xs_im, cmt_re, cmt_im, bmt_re, bmt_im, dskip, a_re, a_im):
    L = dyb.shape[0]
    nb = cmt_re.shape[0]
    ns, W = SLAB * nb, nb * LANES
    ng = ns // 8
    tc = min(LANES, L)
    pitch = tc + 8
    nt = L // tc
    wide = SLAB * LANES
    dn = (((0,), (0,)), ((), ()))

    def body(dy_ref, u_ref, xr_ref, xi_ref, cre_ref, cim_ref, bre_ref, bim_ref, d_ref, ar_ref, ai_ref,
             du_ref, gbr_ref, gbi_ref, gcr_ref, gci_ref, dar_ref, dai_ref,
             sd_re, sd_im, sx_re, sx_im, out_re, out_im, st_re, st_im, acc_re, acc_im):
        first = pl.program_id(0) == 0

        @pl.when(first)
        def _():
            for r in (st_re, st_im, acc_re, acc_im):
                r[...] = jnp.zeros(r.shape, F32)
            for r in (gbr_ref, gbi_ref, gcr_ref, gci_ref):
                r[...] = jnp.zeros(r.shape, F32)

        for j in range(nb):
            dyj = dy_ref[:, j * LANES:(j + 1) * LANES]
            r1 = jnp.dot(dyj, cre_ref[j], preferred_element_type=F32)
            r2 = jnp.dot(dyj, cim_ref[j], preferred_element_type=F32)
            for q in range(SLAB):
                s = SLAB * j + q
                sd_re[pl.ds(s * pitch, tc), :] = r1[:, q * LANES:(q + 1) * LANES]
                sd_im[pl.ds(s * pitch, tc), :] = r2[:, q * LANES:(q + 1) * LANES]
                sx_re[pl.ds(s * pitch, tc), :] = xr_ref[s].astype(F32)
                sx_im[pl.ds(s * pitch, tc), :] = xi_ref[s].astype(F32)
        a_r = [ar_ref[g] for g in range(ng)]
        a_i = [ai_ref[g] for g in range(ng)]
        _scan_steps(tc, pitch, ng, True, a_r, a_i, (sd_re, sd_im), (sx_re, sx_im), out_re, out_im, st_re, st_im,
                    (acc_re, acc_im))
        for j in range(nb):
            cols = slice(j * LANES, (j + 1) * LANES)
            l1 = jnp.concatenate([out_re[pl.ds((SLAB * j + q) * pitch, tc), :] for q in range(SLAB)], axis=1).astype(BF16)
            l2 = jnp.concatenate([out_im[pl.ds((SLAB * j + q) * pitch, tc), :] for q in range(SLAB)], axis=1).astype(BF16)
            dyj = dy_ref[:, cols]
            du = (jnp.dot(l1, bre_ref[j], preferred_element_type=F32) + jnp.dot(l2, bim_ref[j], preferred_element_type=F32)
                  + d_ref[:, cols] * dyj.astype(F32))
            du_ref[:, cols] = du.astype(BF16)
            uj = u_ref[:, cols]
            gbr_ref[j] += lax.dot_general(uj, l1, dn, preferred_element_type=F32)
            gbi_ref[j] += lax.dot_general(uj, l2, dn, preferred_element_type=F32)
            x1 = jnp.concatenate([xr_ref[SLAB * j + q] for q in range(SLAB)], axis=1)
            x2 = jnp.concatenate([xi_ref[SLAB * j + q] for q in range(SLAB)], axis=1)
            gcr_ref[j] += lax.dot_general(dyj, x1, dn, preferred_element_type=F32)
            gci_ref[j] += lax.dot_general(dyj, x2, dn, preferred_element_type=F32)
        dar_ref[...] = acc_re[...]
        dai_ref[...] = acc_im[...]

    full3 = lambda s: pl.BlockSpec(s, lambda i: (0, 0, 0))
    xs = pl.BlockSpec((ns, tc, LANES), lambda i: (0, nt - 1 - i, 0))
    rows = pl.BlockSpec((tc, W), lambda i: (nt - 1 - i, 0))
    mat_a, mat_b = full3((nb, LANES, wide)), full3((nb, wide, LANES))
    vec = full3((ng, 8, LANES))
    return pl.pallas_call(
        body, name=name, grid=(nt,),
        in_specs=[rows, rows, xs, xs, mat_a, mat_a, mat_b, mat_b, pl.BlockSpec((1, W), lambda i: (0, 0)), vec, vec],
        out_specs=[rows, mat_a, mat_a, mat_a, mat_a, vec, vec],
        out_shape=[_sds((L, W), BF16)] + [_sds((nb, LANES, wide), F32)] * 4 + [_sds((ng, 8, LANES), F32)] * 2,
        scratch_shapes=[pltpu.VMEM((ns * pitch, LANES), F32)] * 6 + [pltpu.VMEM((ng, 8, LANES), F32)] * 4,
        compiler_params=_params(1),
    )(dyb, proj, xs_re, xs_im, cmt_re, cmt_im, bmt_re, bmt_im, dskip, a_re, a_im)


def _peer(k):
    x, y, c = lax.axis_index("x"), lax.axis_index("y"), lax.axis_index("c")
    px = 1 - x if (k >> 2) & 1 else x
    py = 1 - y if (k >> 1) & 1 else y
    pc = 1 - c if k & 1 else c
    return (px, py, pc), 4 * px + 2 * py + pc


def _window(ref, kind, idx, n):
    if kind == "col":
        w = ref.shape[1] // n
        return ref.at[:, pl.ds(pl.multiple_of(idx * w, LANES), w)]
    r = ref.shape[0] // n
    return ref.at[pl.ds(pl.multiple_of(idx * r, 8), r), :]


def _all_gather(name, shards, kinds):
    n = len(shards)
    fulls = []
    for s, kind in zip(shards, kinds):
        fulls.append(_sds((s.shape[0], s.shape[1] * N_DEV) if kind == "col" else (s.shape[0] * N_DEV, s.shape[1]), s.dtype))

    def body(*refs):
        src, dst = refs[:n], refs[n:2 * n]
        send, recv, loc = refs[2 * n:]
        me = 4 * lax.axis_index("x") + 2 * lax.axis_index("y") + lax.axis_index("c")
        copies = []
        for a in range(n):
            own = pltpu.make_async_copy(src[a], _window(dst[a], kinds[a], me, N_DEV), loc.at[a])
            own.start()
            copies.append(own)
        sends = []
        for k in range(1, N_DEV):
            dev, _ = _peer(k)
            for a in range(n):
                cp = pltpu.make_async_remote_copy(
                    src_ref=src[a], dst_ref=_window(dst[a], kinds[a], me, N_DEV),
                    send_sem=send.at[a * N_DEV + k], recv_sem=recv.at[a * N_DEV + k],
                    device_id=dev, device_id_type=MESH)
                cp.start()
                sends.append(cp)
        for k in range(1, N_DEV):
            dev, pidx = _peer(k)
            for a in range(n):
                pltpu.make_async_remote_copy(
                    src_ref=src[a], dst_ref=_window(dst[a], kinds[a], pidx, N_DEV),
                    send_sem=send.at[a * N_DEV + k], recv_sem=recv.at[a * N_DEV + k],
                    device_id=dev, device_id_type=MESH).wait_recv()
        for cp in sends:
            cp.wait_send()
        for cp in copies:
            cp.wait()

    any_ = pl.BlockSpec(memory_space=pl.ANY)
    return pl.pallas_call(
        body, name=name, in_specs=[any_] * n, out_specs=[any_] * n, out_shape=fulls,
        scratch_shapes=[pltpu.SemaphoreType.DMA((n * N_DEV,)), pltpu.SemaphoreType.DMA((n * N_DEV,)),
                        pltpu.SemaphoreType.DMA((n,))],
        compiler_params=pltpu.CompilerParams(has_side_effects=True),
    )(*shards)


def _xfer_refs(mode, kinds, a, src, dst, me, pidx):
    if mode == "gather":
        return src[a], _window(dst[a], kinds[a], me, N_DEV), _window(dst[a], kinds[a], pidx, N_DEV)
    return _window(src[a], kinds[a], pidx, N_DEV), dst[a].at[me], dst[a].at[pidx]


def _xfer_out_shapes(mode, arrs, kinds):
    outs = []
    for s, kind in zip(arrs, kinds):
        if mode == "gather":
            outs.append((s.shape[0], s.shape[1] * N_DEV) if kind == "col" else (s.shape[0] * N_DEV, s.shape[1]))
        else:
            outs.append((N_DEV,) + ((s.shape[0], s.shape[1] // N_DEV) if kind == "col" else (s.shape[0] // N_DEV, s.shape[1])))
    return outs


def _sc_xfer(name, mode, arrs, kinds, collective_id):
    n = len(arrs)
    shapes = _xfer_out_shapes(mode, arrs, kinds)
    hbm = pltpu.MemorySpace.HBM
    src = [jax.new_ref(a, memory_space=hbm) for a in arrs]
    dst = [jax.empty_ref(_sds(shp, a.dtype), memory_space=hbm) for shp, a in zip(shapes, arrs)]

    @pl.kernel(mesh=plsc.ScalarSubcoreMesh(axis_name="seq", num_cores=1), name=name,
               scratch_types=(pltpu.SemaphoreType.DMA((n * N_DEV,)), pltpu.SemaphoreType.DMA((n * N_DEV,)),
                              pltpu.SemaphoreType.DMA((n,))),
               compiler_params=pltpu.CompilerParams(collective_id=collective_id))
    def launch(send, recv, loc):
        barrier = pltpu.get_barrier_semaphore()
        for k in range(1, N_DEV):
            pl.semaphore_signal(barrier, inc=1, device_id=_peer(k)[0], device_id_type=MESH)
        pl.semaphore_wait(barrier, N_DEV - 1)
        me = 4 * lax.axis_index("x") + 2 * lax.axis_index("y") + lax.axis_index("c")
        own, sends = [], []
        for a in range(n):
            s, _, d = _xfer_refs(mode, kinds, a, src, dst, me, me)
            own.append(pltpu.make_async_copy(s, d, loc.at[a]))
            own[-1].start()
        for k in range(1, N_DEV):
            dev, pidx = _peer(k)
            for a in range(n):
                s, d, _ = _xfer_refs(mode, kinds, a, src, dst, me, pidx)
                sends.append(pltpu.make_async_remote_copy(src_ref=s, dst_ref=d, send_sem=send.at[a * N_DEV + k],
                                                          recv_sem=recv.at[a * N_DEV + k], device_id=dev, device_id_type=MESH))
                sends[-1].start()
        for cp in own:
            cp.wait()
        for k in range(1, N_DEV):
            dev, pidx = _peer(k)
            for a in range(n):
                s, _, land = _xfer_refs(mode, kinds, a, src, dst, me, pidx)
                pltpu.make_async_remote_copy(src_ref=s, dst_ref=land, send_sem=send.at[a * N_DEV + k],
                                             recv_sem=recv.at[a * N_DEV + k], device_id=dev, device_id_type=MESH).wait_recv()
        for cp in sends:
            cp.wait_send()

    launch()
    return [d[...] for d in dst]


def _sc_gather(name, arrs, kinds, collective_id):
    n = len(arrs)
    pairs = 7
    shapes = _xfer_out_shapes("gather", arrs, kinds)
    hbm = pltpu.MemorySpace.HBM
    src = [jax.new_ref(a, memory_space=hbm) for a in arrs]
    dst = [jax.empty_ref(_sds(shp, a.dtype), memory_space=hbm) for shp, a in zip(shapes, arrs)]

    @pl.kernel(mesh=plsc.ScalarSubcoreMesh(axis_name="seq", num_cores=1), name=name,
               scratch_types=(pltpu.SemaphoreType.DMA((n * pairs,)), pltpu.SemaphoreType.DMA((n * pairs,)),
                              pltpu.SemaphoreType.DMA((n,))),
               compiler_params=pltpu.CompilerParams(collective_id=collective_id))
    def launch(send, recv, loc):
        x, y, c = lax.axis_index("x"), lax.axis_index("y"), lax.axis_index("c")
        me = 4 * x + 2 * y + c
        sib = (x, y, 1 - c)
        chips = []
        for fx, fy in ((1, 0), (0, 1), (1, 1)):
            px, py = (1 - x if fx else x), (1 - y if fy else y)
            chips.append(((px, py, c), 4 * px + 2 * py + c, 4 * px + 2 * py + (1 - c)))
        barrier = pltpu.get_barrier_semaphore()
        for dev in [sib] + [ch[0] for ch in chips]:
            pl.semaphore_signal(barrier, inc=1, device_id=dev, device_id_type=MESH)
        pl.semaphore_wait(barrier, 4)

        def win(a, idx):
            return _window(dst[a], kinds[a], idx, N_DEV)

        def rcopy(a, p, s, d, dev):
            return pltpu.make_async_remote_copy(src_ref=s, dst_ref=d, send_sem=send.at[a * pairs + p],
                                                recv_sem=recv.at[a * pairs + p], device_id=dev, device_id_type=MESH)

        own, sends = [], []
        for a in range(n):
            own.append(pltpu.make_async_copy(src[a], win(a, me), loc.at[a]))
            own[-1].start()
        for j, (dev, _, _) in enumerate(chips):
            for a in range(n):
                sends.append(rcopy(a, 1 + j, src[a], win(a, me), dev))
                sends[-1].start()
        for a in range(n):
            sends.append(rcopy(a, 0, src[a], win(a, me), sib))
            sends[-1].start()
        for j, (dev, idx, _) in enumerate(chips):
            for a in range(n):
                rcopy(a, 1 + j, src[a], win(a, idx), dev).wait_recv()
                sends.append(rcopy(a, 4 + j, win(a, idx), win(a, idx), sib))
                sends[-1].start()
        for cp in own:
            cp.wait()
        for a in range(n):
            rcopy(a, 0, src[a], win(a, 4 * x + 2 * y + (1 - c)), sib).wait_recv()
        for j, (_, _, sidx) in enumerate(chips):
            for a in range(n):
                rcopy(a, 4 + j, src[a], win(a, sidx), sib).wait_recv()
        for cp in sends:
            cp.wait_send()

    launch()
    return [d[...] for d in dst]


_SEQ_IDS = {"gather_in": 7, "gather_mix": 1, "gather_up": 2, "gather_down": 9, "grads_down": 3, "grads_up": 8,
            "grads_mix": 4, "grads_small": 5, "grads_in": 6}


def _launch(name, mode, arrs, kinds):
    if mode == "gather":
        return _sc_gather(name, list(arrs), kinds, _SEQ_IDS[name])
    return _sc_xfer(name, mode, list(arrs), kinds, _SEQ_IDS[name])


def _adamw(name, parts, w, m, v):
    P, R, C = parts.shape
    sub = 16 if parts.dtype == BF16 else 8
    tr = R if R * C <= (1 << 18) else _tile(R, max(sub, (1 << 18) // C), sub)

    def body(p_ref, w_ref, m_ref, v_ref, g_ref, d_ref, nm_ref, nv_ref):
        g = p_ref[0].astype(F32)
        for s in range(1, P):
            g = g + p_ref[s].astype(F32)
        m2 = ADAM_B1 * m_ref[...] + (1.0 - ADAM_B1) * g
        v2 = ADAM_B2 * v_ref[...] + (1.0 - ADAM_B2) * (g * g)
        m_hat = m2 / (1.0 - ADAM_B1 ** ADAM_STEP)
        v_hat = v2 / (1.0 - ADAM_B2 ** ADAM_STEP)
        g_ref[...] = g
        d_ref[...] = -ADAM_LR * (m_hat / (jnp.sqrt(v_hat) + ADAM_EPS) + ADAM_WD * w_ref[...])
        nm_ref[...] = m2
        nv_ref[...] = v2

    sp = pl.BlockSpec((tr, C), lambda i: (i, 0))
    return pl.pallas_call(
        body, name=name, grid=(R // tr,),
        in_specs=[pl.BlockSpec((P, tr, C), lambda i: (0, i, 0)), sp, sp, sp], out_specs=[sp] * 4,
        out_shape=[_sds((R, C), F32)] * 4, compiler_params=_params(1),
    )(parts, w, m, v)


def _pack(arrs, row_mult=8):
    pieces, total = [], 0
    for a in arrs:
        f = a.reshape(-1).astype(F32)
        pad = (-f.shape[0]) % (8 * LANES)
        pieces.append(jnp.pad(f, (0, pad)) if pad else f)
        total += f.shape[0] + pad
    tail = (-total) % (row_mult * LANES)
    if tail:
        pieces.append(jnp.zeros((tail,), F32))
    return jnp.concatenate(pieces).reshape(-1, LANES)


def _unpack(buf, shapes, lead=()):
    out, row = [], 0
    for shp in shapes:
        size = 1
        for d in shp:
            size *= d
        rows = -(-size // (8 * LANES)) * 8
        piece = buf[..., row:row + rows, :].reshape(lead + (rows * LANES,))[..., :size]
        out.append(piece.reshape(lead + tuple(shp)))
        row += rows
    return out


def kernel(x, norm_tok, w_in, a_re, a_im, log_dt, b_re, b_im, c_re, c_im, d_skip, w_glu, w_ssm_out, conv_w, conv_b, w_conv_out, w_o, norm_ffn, w_up, ffn_conv_w, ffn_conv_b, w_down, norm_final, loss_target, m_norm_tok, m_w_in, m_a_re, m_a_im, m_log_dt, m_b_re, m_b_im, m_c_re, m_c_im, m_d_skip, m_w_glu, m_w_ssm_out, m_conv_w, m_conv_b, m_w_conv_out, m_w_o, m_norm_ffn, m_w_up, m_ffn_conv_w, m_ffn_conv_b, m_w_down, m_norm_final, v_norm_tok, v_w_in, v_a_re, v_a_im, v_log_dt, v_b_re, v_b_im, v_c_re, v_c_im, v_d_skip, v_w_glu, v_w_ssm_out, v_conv_w, v_conv_b, v_w_conv_out, v_w_o, v_norm_ffn, v_w_up, v_ffn_conv_w, v_ffn_conv_b, v_w_down, v_norm_final):
    args = dict(locals())
    L, D = x.shape[1], x.shape[2]
    G, P, H = b_re.shape[1], b_re.shape[2], b_re.shape[3]
    SW = G * H
    CW = conv_b.shape[1]
    FF = ffn_conv_b.shape[1]
    GP = G * P
    nb = SW // LANES
    gpb = LANES // H
    me = 4 * lax.axis_index("x") + 2 * lax.axis_index("y") + lax.axis_index("c")
    tm = _tile(L, 256, 16)
    x2 = x[0]
    tgt = loss_target[0]

    big = [("w_in", "col"), ("w_glu", "row"), ("w_ssm_out", "col"), ("w_conv_out", "col"), ("w_o", "row"),
           ("w_up", "col"), ("w_down", "row")]
    shards = [_cast_bf16("cast_" + n, args[n][0]) for n, _ in big]
    small_in = _pack([conv_w[0], ffn_conv_w[0]])
    kind = dict(big)
    mixw, ffnw = ["w_glu", "w_ssm_out", "w_conv_out", "w_o"], ["w_up", "w_down"]
    shard = dict(zip([n for n, _ in big], shards))
    gathered = _launch("gather_in", "gather", [shard["w_in"], small_in], ["col", "row"])
    W = {"w_in": gathered[0]}
    W.update(zip(mixw, _launch("gather_mix", "gather", [shard[n] for n in mixw], [kind[n] for n in mixw])))
    for n in ffnw:
        W[n] = _launch("gather_" + n[2:], "gather", [shard[n]], [kind[n]])[0]
    cw_parts, fcw_parts = _unpack(gathered[-1].reshape(N_DEV, -1, LANES), [conv_w.shape[1:], ffn_conv_w.shape[1:]], (N_DEV,))
    conv_w_full = jnp.moveaxis(cw_parts, 0, 1).reshape(3, CW)
    ffn_conv_w_full = jnp.moveaxis(fcw_parts, 0, 1).reshape(3, FF)

    ar_row, ai_row = a_re.reshape(1, GP), a_im.reshape(1, GP)
    ldt_row = jnp.broadcast_to(log_dt.reshape(G, 1), (G, P)).reshape(1, GP)
    brt = jnp.transpose(b_re[0], (2, 0, 1)).reshape(H, GP)
    bit = jnp.transpose(b_im[0], (2, 0, 1)).reshape(H, GP)
    abar_re, abar_im, bbar_re, bbar_im = _prep_fwd("s5_prep", ar_row, ai_row, ldt_row, brt, bit)
    eye = jnp.eye(gpb, dtype=F32)

    def b_blocks(bt):
        return jnp.einsum("ab,hjbp->jahbp", eye, bt.reshape(H, nb, gpb, P)).reshape(nb, LANES, gpb * P)

    def c_blocks(c):
        return jnp.einsum("ab,jahp->jbpah", eye, c.reshape(nb, gpb, H, P)).reshape(nb, gpb * P, LANES)

    def diag_blocks(mat):
        return jnp.einsum("jahap->hjap", mat.reshape(nb, gpb, H, gpb, P))

    bm_re, bm_im = b_blocks(bbar_re), b_blocks(bbar_im)
    cm_re, cm_im = c_blocks(c_re[0]), -c_blocks(c_im[0])
    a3_re, a3_im = abar_re.reshape(-1, 8, LANES), abar_im.reshape(-1, 8, LANES)
    dskip_row = d_skip.reshape(1, SW)

    cbs = SW // LANES
    cb_v, cb_gb, cb_gc = cbs, cbs + CW // LANES, cbs + 2 * CW // LANES
    cb_ma = (SW + 3 * CW) // D
    xn = _rms_fwd("rms_tok", x2, norm_tok, tm)
    proj = _mm("proj", xn, W["w_in"], "nn", out_dtype=BF16)
    xs_re, xs_im, y, ya = _s5_fwd("s5_fwd", proj, bm_re.astype(BF16), bm_im.astype(BF16), cm_re.astype(BF16),
                                  cm_im.astype(BF16), dskip_row, a3_re, a3_im)
    g1, ya2 = _mm_glu("glu_gate", ya, W["w_glu"], y)
    za = _mm("ssm_out", ya2, W["w_ssm_out"], "nn", out_dtype=BF16)
    q = _convb_fwd("convb", proj, cb_v, cb_gb, cb_gc, conv_w_full, conv_b)
    zb, merged = _mm_merge("conv_out_merge", q, W["w_conv_out"], proj, cb_ma, cb_ma + 1, za)
    h1, hn = _mm_res_rms("mix_out_rms", merged, W["w_o"], x2, norm_ffn)
    hh, f = _ffn_up_act("ffn_up_act", hn, W["w_up"], ffn_conv_w_full, ffn_conv_b)
    o2 = _mm("ffn_down", f, W["w_down"], "nn", tk=2816)
    dh2, dh2b, g_norm_final, loss_part = _final("final", h1, o2, norm_final.reshape(1, D), tgt, tm)

    gw_down = _mm("gw_down", f, dh2b, "tn", out_dtype=BF16, tm=1408, tn=512, tk=L)
    dh2b, gw_down = lax.optimization_barrier((dh2b, gw_down))
    parts = {"w_down": _launch("grads_down", "exchange", [gw_down], [kind["w_down"]])[0]}
    dhh, g_ffn_conv_w, g_ffn_conv_b = _ffn_down_bwd_act("ffn_down_bwd_act", dh2b, W["w_down"], hh, ffn_conv_w_full,
                                                         ffn_conv_b)
    nhalf = lambda t: FF // t
    gw_up = _mm("gw_up", hn, dhh, "tn", out_dtype=BF16, tn=_tile(FF, 1024), tk=L, dims=(D, 2 * FF, L),
                b_spec=lambda a, b, c: pl.BlockSpec((None, c, b), lambda i, j, k: (j // nhalf(b), k, j % nhalf(b))))
    dhh, gw_up = lax.optimization_barrier((dhh, gw_up))
    parts["w_up"] = _launch("grads_up", "exchange", [gw_up], [kind["w_up"]])[0]
    dhn = _mm("d_ffn_in", dhh, W["w_up"], "nt", out_dtype=BF16, tk=_tile(FF, 2816), dims=(L, D, 2 * FF),
              a_spec=lambda a, b, c: pl.BlockSpec((None, a, c), lambda i, j, k: (k // nhalf(c), i, k % nhalf(c))))
    dh1, dh1b, g_norm_ffn = _rms_bwd("rms_ffn_bwd", dhn, h1, norm_ffn, dh2, tm, True)

    dmerged = _mm("d_merged", dh1b, W["w_o"], "nt", out_dtype=BF16)
    gw_o = _mm("gw_o", merged, dh1b, "tn", out_dtype=BF16, tk=L)
    dmerged, gw_o = lax.optimization_barrier((dmerged, gw_o))
    dza, dzb, dma, dmb = _merge_bwd("merge_bwd", proj, cb_ma, cb_ma + 1, za, zb, dmerged, tm)
    dq = _mm("d_q", dzb, W["w_conv_out"], "nt", out_dtype=BF16)
    gw_conv_out = _mm("gw_conv_out", q, dzb, "tn", out_dtype=BF16, tk=L)
    dq, gw_conv_out = lax.optimization_barrier((dq, gw_conv_out))
    dv, dgb, dgc, g_conv_w, g_conv_b = _convb_bwd("convb_bwd", proj, cb_v, cb_gb, cb_gc, conv_w_full, conv_b, dq)
    dya2 = _mm("d_ya2", dza, W["w_ssm_out"], "nt", out_dtype=BF16)
    gw_ssm_out = _mm("gw_ssm_out", ya2, dza, "tn", out_dtype=BF16, tk=L)
    dya2, gw_ssm_out = lax.optimization_barrier((dya2, gw_ssm_out))
    dy_direct, dg1 = _glu_bwd("glu_bwd", y, g1, dya2, tm)
    dya_g = _mm("d_ya_gate", dg1, W["w_glu"], "nt", out_dtype=BF16)
    gw_glu = _mm("gw_glu", ya, dg1, "tn", out_dtype=BF16, tk=L)
    dya_g, gw_glu = lax.optimization_barrier((dya_g, gw_glu))
    parts_mix = _launch("grads_mix", "exchange", [gw_glu, gw_ssm_out, gw_conv_out, gw_o], [kind[n] for n in mixw])
    dyb, g_dskip = _gelu_bwd("gelu_bwd", y, dy_direct, dya_g, proj, tm)
    swap = lambda m: jnp.swapaxes(m, 1, 2).astype(BF16)
    du, gb_re, gb_im, gc_re, gc_im, dab_re, dab_im = _s5_bwd(
        "s5_bwd", dyb, proj, xs_re, xs_im, swap(cm_re), swap(cm_im), swap(bm_re), swap(bm_im), dskip_row, a3_re, a3_im)
    g_ar, g_ai, g_ldt, g_brt, g_bit = _prep_bwd(
        "s5_prep_bwd", ar_row, ai_row, ldt_row, brt, bit, dab_re.reshape(1, GP), dab_im.reshape(1, GP),
        diag_blocks(gb_re).reshape(H, GP), diag_blocks(gb_im).reshape(H, GP))
    small = dict(
        a_re=g_ar.reshape(1, G, P), a_im=g_ai.reshape(1, G, P),
        log_dt=g_ldt.reshape(G, P).sum(axis=1).reshape(1, G),
        b_re=jnp.transpose(g_brt.reshape(H, G, P), (1, 2, 0))[None], b_im=jnp.transpose(g_bit.reshape(H, G, P), (1, 2, 0))[None],
        c_re=jnp.transpose(diag_blocks(gc_re), (1, 2, 0, 3)).reshape(1, G, H, P),
        c_im=-jnp.transpose(diag_blocks(gc_im), (1, 2, 0, 3)).reshape(1, G, H, P),
        d_skip=g_dskip.reshape(1, G, H), conv_b=g_conv_b, norm_ffn=g_norm_ffn, ffn_conv_b=g_ffn_conv_b,
        norm_final=g_norm_final.reshape(D), conv_w=g_conv_w[None], ffn_conv_w=g_ffn_conv_w[None])
    rep = ["a_re", "a_im", "log_dt", "b_re", "b_im", "c_re", "c_im", "d_skip", "conv_b", "norm_ffn", "ffn_conv_b", "norm_final"]
    order = rep + ["conv_w", "ffn_conv_w"]
    full_shapes = {n: args[n].shape for n in rep}
    full_shapes["conv_w"], full_shapes["ffn_conv_w"] = (1, 3, CW), (1, 3, FF)
    rep_pack = _pack([small[n] for n in rep], LANES)
    rep_rows = rep_pack.shape[0]
    gpack = jnp.concatenate([loss_part, rep_pack, _pack([small["conv_w"], small["ffn_conv_w"]])], axis=0)
    rep0 = loss_part.shape[0]
    rows = gpack.shape[0]
    du, gpack = lax.optimization_barrier((du, gpack))
    gall = _launch("grads_small", "gather", [gpack], ["row"])[0]

    dproj = _concat_cols("dproj", [du, dv, dgb, dgc, dma, dmb], tm)
    gw_in = _mm("gw_in", xn, dproj, "tn", out_dtype=BF16, tk=L)
    dproj, gw_in = lax.optimization_barrier((dproj, gw_in))
    parts_in = _launch("grads_in", "exchange", [gw_in], ["col"])
    dxn = _mm("d_xn", dproj, W["w_in"], "nt", out_dtype=BF16, tk=4096)
    grad_x, g_norm_tok = _rms_bwd("rms_tok_bwd", dxn, x2, norm_tok, dh1, tm, False)

    res = {}

    def big_update(n):
        res[n] = [r[None] for r in _adamw("adamw_" + n, parts[n], args[n][0], args["m_" + n][0], args["v_" + n][0])]

    def after(xs, dep):
        return lax.optimization_barrier((list(xs), dep))[0]

    parts["w_down"] = after([parts["w_down"]], grad_x)[0]
    big_update("w_down")
    parts["w_up"] = after([parts["w_up"]], res["w_down"][1])[0]
    big_update("w_up")
    parts.update(zip(mixw, after(parts_mix, [res[n][1] for n in ffnw])))
    for n in mixw:
        big_update(n)
    gall = after([gall], [res[n][1] for n in mixw])[0].reshape(N_DEV, rows, LANES)
    gcw, gfcw = _unpack(gall[:, rep0 + rep_rows:], [full_shapes["conv_w"], full_shapes["ffn_conv_w"]], (N_DEV,))
    cws, fcws = CW // N_DEV, FF // N_DEV
    gcw = lax.dynamic_slice_in_dim(gcw[:, 0], me * cws, cws, axis=2)
    gfcw = lax.dynamic_slice_in_dim(gfcw[:, 0], me * fcws, fcws, axis=2)
    res["conv_w"] = [r[None] for r in _adamw("adamw_conv_w", gcw, conv_w[0], m_conv_w[0], v_conv_w[0])]
    res["ffn_conv_w"] = [r[None] for r in _adamw("adamw_ffn_conv_w", gfcw, ffn_conv_w[0], m_ffn_conv_w[0], v_ffn_conv_w[0])]
    rep_out = _adamw("adamw_small", gall[:, rep0:rep0 + rep_rows], _pack([args[n] for n in rep], LANES),
                     _pack([args["m_" + n] for n in rep], LANES), _pack([args["v_" + n] for n in rep], LANES))
    rep_out = [_unpack(r, [full_shapes[n] for n in rep]) for r in rep_out]
    for i, n in enumerate(rep):
        res[n] = [r[i] for r in rep_out]
    nt_pack = after([_pack([g_norm_tok])], [res[n][1] for n in ("a_re", "conv_w", "ffn_conv_w")])
    nt_all = _all_gather("gather_norm_tok_grad", nt_pack, ["row"])[0].reshape(N_DEV, -1, LANES)
    nt_out = _adamw("adamw_norm_tok", nt_all, _pack([norm_tok]), _pack([m_norm_tok]), _pack([v_norm_tok]))
    res["norm_tok"] = [_unpack(r, [norm_tok.shape])[0] for r in nt_out]
    parts["w_in"] = after(parts_in, nt_out[0])[0]
    big_update("w_in")

    loss = jnp.sum(gall[:, 0, 0])
    names = ["norm_tok", "w_in", "a_re", "a_im", "log_dt", "b_re", "b_im", "c_re", "c_im", "d_skip", "w_glu", "w_ssm_out",
             "conv_w", "conv_b", "w_conv_out", "w_o", "norm_ffn", "w_up", "ffn_conv_w", "ffn_conv_b", "w_down", "norm_final"]
    out = [loss, grad_x[None]]
    for slot in range(4):
        out += [res[n][slot] for n in names]
    return tuple(out)
```

```python
import jax
import jax.numpy as jnp
from jax import lax
from jax.experimental import pallas as pl
from jax.experimental.pallas import tpu as pltpu
from jax.experimental.pallas import tpu_sc as plsc

F32 = jnp.float32
BF16 = jnp.bfloat16
N_DEV = 8
LANES = 128
SLAB = 4
EPS = 1e-6
ADAM_LR = 0.001
ADAM_B1 = 0.9
ADAM_B2 = 0.999
ADAM_EPS = 1e-08
ADAM_WD = 0.01
ADAM_STEP = 10
VMEM_LIMIT = 56 * 1024 * 1024
MESH = pl.DeviceIdType.MESH


def _tile(n, pref, mult=LANES):
    best = None
    t = mult
    while t <= min(n, pref):
        if n % t == 0:
            best = t
        t += mult
    return best if best is not None else n


def _params(ndim):
    return pltpu.CompilerParams(dimension_semantics=("arbitrary",) * ndim, vmem_limit_bytes=VMEM_LIMIT)


def _sds(shape, dtype):
    return jax.ShapeDtypeStruct(tuple(shape), dtype)


def _mm(name, a, b, mode, *, out_dtype=F32, tm=1024, tn=1024, tk=2048, dims=None, a_spec=None, b_spec=None):
    if dims is None:
        if mode == "nn":
            (M, K), N = a.shape, b.shape[1]
        elif mode == "nt":
            (M, K), N = a.shape, b.shape[0]
        else:
            (K, M), N = a.shape, b.shape[1]
    else:
        M, N, K = dims
    tm, tn, tk = _tile(M, tm), _tile(N, tn), _tile(K, tk)
    nk = K // tk
    if mode == "nn":
        dn = (((1,), (0,)), ((), ()))
        sa = pl.BlockSpec((tm, tk), lambda i, j, k: (i, k))
        sb = pl.BlockSpec((tk, tn), lambda i, j, k: (k, j))
    elif mode == "nt":
        dn = (((1,), (1,)), ((), ()))
        sa = pl.BlockSpec((tm, tk), lambda i, j, k: (i, k))
        sb = pl.BlockSpec((tn, tk), lambda i, j, k: (j, k))
    else:
        dn = (((0,), (0,)), ((), ()))
        sa = pl.BlockSpec((tk, tm), lambda i, j, k: (k, i))
        sb = pl.BlockSpec((tk, tn), lambda i, j, k: (k, j))
    sa = a_spec(tm, tn, tk) if a_spec is not None else sa
    sb = b_spec(tm, tn, tk) if b_spec is not None else sb
    use_acc = nk > 1 and out_dtype != F32

    def body(a_ref, b_ref, o_ref, *acc):
        k = pl.program_id(2)
        p = lax.dot_general(a_ref[...], b_ref[...], dn, preferred_element_type=F32)
        if nk == 1:
            o_ref[...] = p.astype(out_dtype)
        else:
            tgt = acc[0] if use_acc else o_ref

            @pl.when(k == 0)
            def _():
                tgt[...] = p

            @pl.when(k > 0)
            def _():
                tgt[...] += p

            if use_acc:
                @pl.when(k == nk - 1)
                def _():
                    o_ref[...] = acc[0][...].astype(out_dtype)

    return pl.pallas_call(
        body, name=name, grid=(M // tm, N // tn, nk),
        in_specs=[sa, sb], out_specs=pl.BlockSpec((tm, tn), lambda i, j, k: (i, j)),
        out_shape=_sds((M, N), out_dtype),
        scratch_shapes=[pltpu.VMEM((tm, tn), F32)] if use_acc else [],
        compiler_params=_params(3),
    )(a, b)


def _mm_rows(name, a, b, extra, outs, epilogue, *, tm, tk, nt=False):
    M, K = a.shape
    N = b.shape[0] if nt else b.shape[1]
    tm, tk = _tile(M, tm, 16), _tile(K, tk)
    nk = K // tk
    ne, no = len(extra), len(outs)
    dn = (((1,), (1 if nt else 0,)), ((), ()))
    b_spec = pl.BlockSpec((N, tk), lambda i, k: (0, k)) if nt else pl.BlockSpec((tk, N), lambda i, k: (k, 0))

    def body(a_ref, b_ref, *rest):
        i, k = pl.program_id(0), pl.program_id(1)
        p = lax.dot_general(a_ref[...], b_ref[...], dn, preferred_element_type=F32)
        if nk == 1:
            epilogue(i, p, rest[:ne], rest[ne:ne + no])
            return
        acc = rest[ne + no]
        _acc_rows(k, acc, p)

        @pl.when(k == nk - 1)
        def _():
            epilogue(i, acc, rest[:ne], rest[ne:ne + no])

    return pl.pallas_call(
        body, name=name, grid=(M // tm, nk),
        in_specs=[pl.BlockSpec((tm, tk), lambda i, k: (i, k)), b_spec] + [s for _, s in extra],
        out_specs=[s for _, s in outs], out_shape=[o for o, _ in outs],
        scratch_shapes=[pltpu.VMEM((tm, N), F32)] if nk > 1 else [], compiler_params=_params(2),
    )(a, b, *[e for e, _ in extra])


def _rows(name, body, L, tm, ins, outs):
    return pl.pallas_call(
        body, name=name, grid=(L // tm,),
        in_specs=[s for _, s in ins], out_specs=[s for _, s in outs],
        out_shape=[o for o, _ in outs], compiler_params=_params(1),
    )(*[a for a, _ in ins])


def _rs(tm, w, cb=0):
    return pl.BlockSpec((tm, w), lambda i: (i, cb))


def _fs(shape):
    return pl.BlockSpec(tuple(shape), lambda i: (0,) * len(shape))


def _acc_rows(i, ref, part):
    @pl.when(i == 0)
    def _():
        ref[...] = part

    @pl.when(i > 0)
    def _():
        ref[...] += part


def _cast_bf16(name, w):
    R, C = w.shape
    tr = _tile(R, max(16, (1 << 20) // C), 16)

    def body(w_ref, o_ref):
        o_ref[...] = w_ref[...].astype(BF16)

    return _rows(name, body, R, tr, [(w, _rs(tr, C))], [(_sds((R, C), BF16), _rs(tr, C))])[0]


def _concat_cols(name, pieces, tm):
    L = pieces[0].shape[0]
    widths = [p.shape[1] for p in pieces]

    def body(*refs):
        o_ref, off = refs[-1], 0
        for p_ref, w in zip(refs[:-1], widths):
            o_ref[:, off:off + w] = p_ref[...]
            off += w

    return _rows(name, body, L, tm, [(p, _rs(tm, w)) for p, w in zip(pieces, widths)],
                 [(_sds((L, sum(widths)), pieces[0].dtype), _rs(tm, sum(widths)))])[0]


def _rms_fwd(name, x, g, tm):
    L, D = x.shape

    def body(x_ref, g_ref, o_ref):
        xv = x_ref[...]
        r = lax.rsqrt(jnp.mean(xv * xv, axis=-1, keepdims=True) + EPS)
        o_ref[...] = (xv * r * g_ref[...]).astype(BF16)

    return _rows(name, body, L, tm, [(x, _rs(tm, D)), (g, _fs((1, D)))], [(_sds((L, D), BF16), _rs(tm, D))])[0]


def _mm_res_rms(name, a, w, x, g):
    L, D = x.shape

    def epilogue(i, o, extra, outs):
        x_ref, g_ref = extra
        for r0 in range(0, tm, sub):
            rs = slice(r0, r0 + sub)
            h = x_ref[rs, :] + o[rs, :]
            r = lax.rsqrt(jnp.mean(h * h, axis=-1, keepdims=True) + EPS)
            outs[0][rs, :] = h
            outs[1][rs, :] = (h * r * g_ref[...]).astype(BF16)

    tm = _tile(L, 512, 16)
    sub = _tile(tm, LANES, 16)
    row = pl.BlockSpec((tm, D), lambda i, k: (i, 0))
    return _mm_rows(name, a, w, [(x, row), (g, pl.BlockSpec((1, D), lambda i, k: (0, 0)))],
                    [(_sds((L, D), F32), row), (_sds((L, D), BF16), row)], epilogue, tm=tm, tk=a.shape[1])


def _rms_bwd(name, dn, h, g, dres, tm, with_bf16):
    L, D = h.shape

    def body(dn_ref, h_ref, g_ref, dres_ref, dh_ref, *rest):
        i = pl.program_id(0)
        h = h_ref[...]
        r = lax.rsqrt(jnp.mean(h * h, axis=-1, keepdims=True) + EPS)
        xh = h * r
        d = dn_ref[...].astype(F32)
        dxh = d * g_ref[...]
        dh = dres_ref[...] + r * (dxh - xh * jnp.mean(dxh * xh, axis=-1, keepdims=True))
        dh_ref[...] = dh
        if with_bf16:
            rest[0][...] = dh.astype(BF16)
        _acc_rows(i, rest[-1], jnp.sum(d * xh, axis=0, keepdims=True))

    outs = [(_sds((L, D), F32), _rs(tm, D))]
    if with_bf16:
        outs.append((_sds((L, D), BF16), _rs(tm, D)))
    outs.append((_sds((1, D), F32), _fs((1, D))))
    return _rows(name, body, L, tm, [(dn, _rs(tm, D)), (h, _rs(tm, D)), (g, _fs((1, D))), (dres, _rs(tm, D))], outs)


def _final(name, h1, o2, g, tgt, tm):
    L, D = h1.shape

    def body(h1_ref, o2_ref, g_ref, t_ref, dh_ref, dhb_ref, dg_ref, loss_ref):
        i = pl.program_id(0)
        h = h1_ref[...] + o2_ref[...]
        r = lax.rsqrt(jnp.mean(h * h, axis=-1, keepdims=True) + EPS)
        xh = h * r
        gv = g_ref[...]
        e = xh * gv - t_ref[...]
        part = 0.5 * jnp.sum(jnp.mean(e * e, axis=-1, keepdims=True), axis=0, keepdims=True)
        dy = e / D
        dxh = dy * gv
        dh = r * (dxh - xh * jnp.mean(dxh * xh, axis=-1, keepdims=True))
        dh_ref[...] = dh
        dhb_ref[...] = dh.astype(BF16)
        _acc_rows(i, dg_ref, jnp.sum(dy * xh, axis=0, keepdims=True))
        _acc_rows(i, loss_ref, jnp.broadcast_to(part, (8, LANES)))

    return _rows(name, body, L, tm,
                 [(h1, _rs(tm, D)), (o2, _rs(tm, D)), (g, _fs((1, D))), (tgt, _rs(tm, D))],
                 [(_sds((L, D), F32), _rs(tm, D)), (_sds((L, D), BF16), _rs(tm, D)),
                  (_sds((1, D), F32), _fs((1, D))), (_sds((8, LANES), F32), _fs((8, LANES)))])


def _gelu_and_grad(x):
    c, k = 0.7978845608028654, 0.044715
    x2 = x * x
    t = jnp.tanh(c * x * (1.0 + k * x2))
    half = 0.5 * x
    return half * (1.0 + t), 0.5 * (1.0 + t) + half * (1.0 - t * t) * (c * (1.0 + 3.0 * k * x2))


def _glu_fn(y, g1):
    ya = jax.nn.gelu(y)
    return ya * jax.nn.sigmoid(g1)


def _mm_glu(name, ya, w, y):
    L, W = y.shape

    def epilogue(i, o, extra, outs):
        for r0 in range(0, tm, sub):
            rs = slice(r0, r0 + sub)
            g16 = o[rs, :].astype(BF16)
            outs[0][rs, :] = g16
            outs[1][rs, :] = _glu_fn(extra[0][rs, :], g16.astype(F32)).astype(BF16)

    tm = _tile(L, 512, 16)
    sub = _tile(tm, LANES, 16)
    row = pl.BlockSpec((tm, W), lambda i, k: (i, 0))
    return _mm_rows(name, ya, w, [(y, row)], [(_sds((L, W), BF16), row), (_sds((L, W), BF16), row)], epilogue,
                    tm=tm, tk=ya.shape[1])


def _glu_bwd(name, y, g1, dya2, tm):
    L, W = y.shape

    def body(y_ref, g_ref, d_ref, dy_ref, dg_ref):
        _, vjp = jax.vjp(_glu_fn, y_ref[...], g_ref[...].astype(F32))
        dy, dg = vjp(d_ref[...].astype(F32))
        dy_ref[...] = dy
        dg_ref[...] = dg.astype(BF16)

    return _rows(name, body, L, tm, [(y, _rs(tm, W)), (g1, _rs(tm, W)), (dya2, _rs(tm, W))],
                 [(_sds((L, W), F32), _rs(tm, W)), (_sds((L, W), BF16), _rs(tm, W))])


def _gelu_bwd(name, y, dy_direct, dya_g, proj, tm):
    L, W = y.shape

    def body(y_ref, dd_ref, dg_ref, u_ref, dyb_ref, dsk_ref):
        i = pl.program_id(0)
        dy = dd_ref[...] + dg_ref[...].astype(F32) * _gelu_and_grad(y_ref[...])[1]
        dyb_ref[...] = dy.astype(BF16)
        _acc_rows(i, dsk_ref, jnp.sum(dy * u_ref[...].astype(F32), axis=0, keepdims=True))

    return _rows(name, body, L, tm,
                 [(y, _rs(tm, W)), (dy_direct, _rs(tm, W)), (dya_g, _rs(tm, W)), (proj, _rs(tm, W, 0))],
                 [(_sds((L, W), BF16), _rs(tm, W)), (_sds((1, W), F32), _fs((1, W)))])


def _merge_fn(ma, mb, za, zb):
    return jax.nn.sigmoid(ma) * za + jax.nn.sigmoid(mb) * zb


def _mm_merge(name, q, w, proj, cb_a, cb_b, za):
    L, D = za.shape

    def epilogue(i, o, extra, outs):
        ma_ref, mb_ref, za_ref = extra
        for r0 in range(0, tm, sub):
            rs = slice(r0, r0 + sub)
            zb16 = o[rs, :].astype(BF16)
            outs[0][rs, :] = zb16
            outs[1][rs, :] = _merge_fn(ma_ref[rs, :].astype(F32), mb_ref[rs, :].astype(F32), za_ref[rs, :].astype(F32),
                                       zb16.astype(F32)).astype(BF16)

    tm = _tile(L, 512, 16)
    sub = _tile(tm, LANES, 16)
    row = pl.BlockSpec((tm, D), lambda i, k: (i, 0))
    col = lambda cb: pl.BlockSpec((tm, D), lambda i, k: (i, cb))
    return _mm_rows(name, q, w, [(proj, col(cb_a)), (proj, col(cb_b)), (za, row)],
                    [(_sds((L, D), BF16), row), (_sds((L, D), BF16), row)], epilogue, tm=tm, tk=q.shape[1])


def _mm_merge_bwd(name, dh, w, proj, cb_a, cb_b, za, zb):
    L, D = za.shape

    def epilogue(i, o, extra, outs):
        ma_ref, mb_ref, za_ref, zb_ref = extra
        for r0 in range(0, tm, sub):
            rs = slice(r0, r0 + sub)
            _, vjp = jax.vjp(_merge_fn, ma_ref[rs, :].astype(F32), mb_ref[rs, :].astype(F32), za_ref[rs, :].astype(F32),
                             zb_ref[rs, :].astype(F32))
            dma, dmb, dza, dzb = vjp(o[rs, :].astype(BF16).astype(F32))
            for ref, val in zip(outs, (dza, dzb, dma, dmb)):
                ref[rs, :] = val.astype(BF16)

    tm = _tile(L, 256, 16)
    sub = _tile(tm, LANES, 16)
    row = pl.BlockSpec((tm, D), lambda i, k: (i, 0))
    col = lambda cb: pl.BlockSpec((tm, D), lambda i, k: (i, cb))
    return _mm_rows(name, dh, w, [(proj, col(cb_a)), (proj, col(cb_b)), (za, row), (zb, row)],
                    [(_sds((L, D), BF16), row)] * 4, epilogue, tm=tm, tk=dh.shape[1], nt=True)


def _shift_down(x, k):
    row = lax.broadcasted_iota(jnp.int32, x.shape, 0)
    return jnp.where(row >= k, pltpu.roll(x, k, axis=0), 0.0)


def _shift_up(x, k):
    n = x.shape[0]
    row = lax.broadcasted_iota(jnp.int32, x.shape, 0)
    return jnp.where(row < n - k, pltpu.roll(x, n - k, axis=0), 0.0)


def _conv3(cv, w_ref, b_ref):
    return (w_ref[2:3, :] * cv + w_ref[1:2, :] * _shift_down(cv, 1) + w_ref[0:1, :] * _shift_down(cv, 2)
            + b_ref[...])


def _conv3_bwd(dcc, cv, w_ref):
    dcv = w_ref[2:3, :] * dcc + w_ref[1:2, :] * _shift_up(dcc, 1) + w_ref[0:1, :] * _shift_up(dcc, 2)
    dw = [jnp.sum(dcc * _shift_down(cv, 2), axis=0, keepdims=True),
          jnp.sum(dcc * _shift_down(cv, 1), axis=0, keepdims=True),
          jnp.sum(dcc * cv, axis=0, keepdims=True)]
    db = jnp.sum(dcc, axis=0, keepdims=True)
    return dcv, dw, db


def _store_rows(ref, rows):
    for r, val in enumerate(rows):
        ref[r:r + 1, :] = val


def _cols(name, body, ncb, ins, outs):
    return pl.pallas_call(
        body, name=name, grid=(ncb,),
        in_specs=[s for _, s in ins], out_specs=[s for _, s in outs],
        out_shape=[o for o, _ in outs], compiler_params=_params(1),
    )(*[a for a, _ in ins])


def _cb(L, w, off=0):
    return pl.BlockSpec((L, w), lambda j: (0, j + off))


def _convb_fwd(name, proj, cb_v, cb_gb, cb_gc, w, b):
    L = proj.shape[0]
    W = w.shape[1]
    c = LANES

    def body(v_ref, gb_ref, gc_ref, w_ref, b_ref, q_ref):
        cc = _conv3(gc_ref[...].astype(F32) * v_ref[...].astype(F32), w_ref, b_ref)
        q_ref[...] = (gb_ref[...].astype(F32) * cc).astype(BF16)

    return _cols(name, body, W // c,
                 [(proj, _cb(L, c, cb_v)), (proj, _cb(L, c, cb_gb)), (proj, _cb(L, c, cb_gc)),
                  (w, _cb(3, c)), (b, _cb(1, c))],
                 [(_sds((L, W), BF16), _cb(L, c))])[0]


def _convb_bwd(name, proj, cb_v, cb_gb, cb_gc, w, b, dq):
    L = proj.shape[0]
    W = w.shape[1]
    c = LANES

    def body(v_ref, gb_ref, gc_ref, w_ref, b_ref, dq_ref, dv_ref, dgb_ref, dgc_ref, dw_ref, db_ref):
        v, gc = v_ref[...].astype(F32), gc_ref[...].astype(F32)
        cv = gc * v
        cc = _conv3(cv, w_ref, b_ref)
        dq = dq_ref[...].astype(F32)
        dgb_ref[...] = (dq * cc).astype(BF16)
        dcv, dw, db = _conv3_bwd(dq * gb_ref[...].astype(F32), cv, w_ref)
        dv_ref[...] = (dcv * gc).astype(BF16)
        dgc_ref[...] = (dcv * v).astype(BF16)
        _store_rows(dw_ref, dw)
        db_ref[...] = db

    return _cols(name, body, W // c,
                 [(proj, _cb(L, c, cb_v)), (proj, _cb(L, c, cb_gb)), (proj, _cb(L, c, cb_gc)),
                  (w, _cb(3, c)), (b, _cb(1, c)), (dq, _cb(L, c))],
                 [(_sds((L, W), BF16), _cb(L, c)), (_sds((L, W), BF16), _cb(L, c)), (_sds((L, W), BF16), _cb(L, c)),
                  (_sds((3, W), F32), _cb(3, c)), (_sds((1, W), F32), _cb(1, c))])


HALO = 16


def _ffn_up_act(name, hn, w_up, w, b):
    L, D = hn.shape
    Fw = w.shape[1]
    tm, tc = _tile(L, 1024, HALO), _tile(Fw, 512)
    ncb = Fw // tc

    def body(x_ref, wa_ref, wg_ref, w_ref, b_ref, hh_ref, f_ref, carry):
        i, j = pl.program_id(0), pl.program_id(1)
        a16 = jnp.dot(x_ref[...], wa_ref[...], preferred_element_type=F32).astype(BF16)
        g16 = jnp.dot(x_ref[...], wg_ref[...], preferred_element_type=F32).astype(BF16)
        hh_ref[0] = a16
        hh_ref[1] = g16
        for c0 in range(0, tc, LANES):
            cs = slice(c0, c0 + LANES)
            prev = jnp.where(i == 0, 0.0, carry[j, :, cs])
            x = jnp.concatenate([prev, a16[:, cs].astype(F32)], axis=0)
            n = x.shape[0]
            a = (w_ref[2:3, cs] * x + w_ref[1:2, cs] * pltpu.roll(x, 1, axis=0) + w_ref[0:1, cs] * pltpu.roll(x, 2, axis=0)
                 + b_ref[:, cs])[8:n]
            f_ref[:, cs] = (_gelu_and_grad(a)[0] * g16[:, cs].astype(F32)).astype(BF16)
            carry[j, :, cs] = x[n - 8:n]

    return pl.pallas_call(
        body, name=name, grid=(L // tm, ncb),
        in_specs=[pl.BlockSpec((tm, D), lambda i, j: (i, 0)), pl.BlockSpec((D, tc), lambda i, j: (0, j)),
                  pl.BlockSpec((D, tc), lambda i, j: (0, j + ncb)),
                  pl.BlockSpec((3, tc), lambda i, j: (0, j)), pl.BlockSpec((1, tc), lambda i, j: (0, j))],
        out_specs=[pl.BlockSpec((2, tm, tc), lambda i, j: (0, i, j)), pl.BlockSpec((tm, tc), lambda i, j: (i, j))],
        out_shape=[_sds((2, L, Fw), BF16), _sds((L, Fw), BF16)],
        scratch_shapes=[pltpu.VMEM((ncb, 8, tc), F32)], compiler_params=_params(2),
    )(hn, w_up, w_up, w, b)


def _ffn_down_bwd_act(name, dy, w_down, hh, w, b):
    L, D = dy.shape
    Fw = w.shape[1]
    tm, tc = _tile(L, 512, HALO), _tile(Fw, 512)
    ncb, nrt, rpt = Fw // tc, L // tm, tm // HALO
    dn = (((1,), (1,)), ((), ()))

    def body(dy_ref, wd_ref, a_ref, ap_ref, h2_ref, w_ref, b_ref, dhh_ref, dw_ref, db_ref, carry, acc):
        i, j = pl.program_id(0), pl.program_id(1)
        first_rows = i == nrt - 1
        d16 = lax.dot_general(dy_ref[...], wd_ref[...], dn, preferred_element_type=F32).astype(BF16)
        for c0 in range(0, tc, LANES):
            cs = slice(c0, c0 + LANES)
            h1 = jnp.concatenate([jnp.where(first_rows, 0.0, ap_ref[:, cs].astype(F32)), a_ref[:, cs].astype(F32)], axis=0)
            s1, s2 = pltpu.roll(h1, 1, axis=0), pltpu.roll(h1, 2, axis=0)
            n = h1.shape[0]
            a = (w_ref[2:3, cs] * h1 + w_ref[1:2, cs] * s1 + w_ref[0:1, cs] * s2 + b_ref[:, cs])[HALO:n]
            ga, dga = _gelu_and_grad(a)
            d = d16[:, cs].astype(F32)
            da = d * h2_ref[:, cs].astype(F32) * dga
            dae = jnp.concatenate([da, jnp.where(i == 0, 0.0, carry[j, :, cs])], axis=0)
            m = dae.shape[0]
            dh1 = w_ref[2:3, cs] * dae + w_ref[1:2, cs] * pltpu.roll(dae, m - 1, axis=0) + w_ref[0:1, cs] * pltpu.roll(dae, m - 2, axis=0)
            dhh_ref[0, :, cs] = dh1[0:tm].astype(BF16)
            dhh_ref[1, :, cs] = (d * ga).astype(BF16)
            carry[j, :, cs] = da[0:8]
            rows = [jnp.sum(da * s2[HALO:n], axis=0, keepdims=True), jnp.sum(da * s1[HALO:n], axis=0, keepdims=True),
                    jnp.sum(da * h1[HALO:n], axis=0, keepdims=True), jnp.sum(da, axis=0, keepdims=True)]
            for r in range(4):
                tot = jnp.where(i == 0, 0.0, acc[j, r:r + 1, cs]) + rows[r]
                acc[j, r:r + 1, cs] = tot
                if r < 3:
                    dw_ref[r:r + 1, cs] = tot
                else:
                    db_ref[:, cs] = tot

    rt = lambda i: nrt - 1 - i
    dhh, dw, db = pl.pallas_call(
        body, name=name, grid=(nrt, ncb),
        in_specs=[pl.BlockSpec((tm, D), lambda i, j: (rt(i), 0)), pl.BlockSpec((tc, D), lambda i, j: (j, 0)),
                  pl.BlockSpec((None, tm, tc), lambda i, j: (0, rt(i), j)),
                  pl.BlockSpec((None, HALO, tc), lambda i, j: (0, jnp.maximum(rt(i) * rpt - 1, 0), j)),
                  pl.BlockSpec((None, tm, tc), lambda i, j: (1, rt(i), j)),
                  pl.BlockSpec((3, tc), lambda i, j: (0, j)), pl.BlockSpec((1, tc), lambda i, j: (0, j))],
        out_specs=[pl.BlockSpec((2, tm, tc), lambda i, j: (0, rt(i), j)),
                   pl.BlockSpec((None, 3, tc), lambda i, j: (i, 0, j)), pl.BlockSpec((None, 1, tc), lambda i, j: (i, 0, j))],
        out_shape=[_sds((2, L, Fw), BF16), _sds((nrt, 3, Fw), F32), _sds((nrt, 1, Fw), F32)],
        scratch_shapes=[pltpu.VMEM((ncb, 8, tc), F32), pltpu.VMEM((ncb, 8, tc), F32)], compiler_params=_params(2),
    )(dy, w_down, hh, hh, hh, w, b)
    return dhh, dw[nrt - 1], db[nrt - 1]


def _prep_fn(ar, ai, ldt, brt, bit):
    dt = jnp.exp(ldt)
    mag = jnp.exp(dt * ar)
    are = mag * jnp.cos(dt * ai)
    aim = mag * jnp.sin(dt * ai)
    nr = are - 1.0
    ni = aim
    den = ar * ar + ai * ai
    fr = (nr * ar + ni * ai) / den
    fi = (ni * ar - nr * ai) / den
    return are, aim, fr * brt - fi * bit, fr * bit + fi * brt


def _prep_fwd(name, ar, ai, ldt, brt, bit):
    def body(ar_ref, ai_ref, l_ref, br_ref, bi_ref, o1, o2, o3, o4):
        o1[...], o2[...], o3[...], o4[...] = _prep_fn(ar_ref[...], ai_ref[...], l_ref[...], br_ref[...], bi_ref[...])

    return pl.pallas_call(body, name=name,
                          out_shape=[_sds(ar.shape, F32), _sds(ar.shape, F32), _sds(brt.shape, F32), _sds(brt.shape, F32)],
                          )(ar, ai, ldt, brt, bit)


def _prep_bwd(name, ar, ai, ldt, brt, bit, g1, g2, g3, g4):
    def body(ar_ref, ai_ref, l_ref, br_ref, bi_ref, g1_ref, g2_ref, g3_ref, g4_ref, o1, o2, o3, o4, o5):
        _, vjp = jax.vjp(_prep_fn, ar_ref[...], ai_ref[...], l_ref[...], br_ref[...], bi_ref[...])
        o1[...], o2[...], o3[...], o4[...], o5[...] = vjp((g1_ref[...], g2_ref[...], g3_ref[...], g4_ref[...]))

    return pl.pallas_call(body, name=name,
                          out_shape=[_sds(ar.shape, F32)] * 3 + [_sds(brt.shape, F32)] * 2,
                          )(ar, ai, ldt, brt, bit, g1, g2, g3, g4)


def _scan_steps(tc, pitch, ng, reverse, a_r, a_i, stage_b, stage_x, out_re, out_im, st_re, st_im, acc):
    def step(tt, carry):
        t = (tc - 1 - tt) if reverse else tt
        new, sums = [], []
        for g in range(ng):
            rows = pl.ds(g * 8 * pitch + t, 8, stride=pitch)
            cr, ci = carry[2 * g], carry[2 * g + 1]
            br, bi = stage_b[0][rows, :], stage_b[1][rows, :]
            if reverse:
                xr, xi = stage_x[0][rows, :], stage_x[1][rows, :]
                sums += [carry[2 * ng + 2 * g] + (xr * cr + xi * ci), carry[2 * ng + 2 * g + 1] + (xr * ci - xi * cr)]
                nr = a_r[g] * cr + a_i[g] * ci + br
                ni = a_r[g] * ci - a_i[g] * cr + bi
            else:
                nr = a_r[g] * cr - a_i[g] * ci + br
                ni = a_r[g] * ci + a_i[g] * cr + bi
            out_re[rows, :] = nr
            out_im[rows, :] = ni
            new += [nr, ni]
        return tuple(new + sums)

    init = []
    for g in range(ng):
        init += [st_re[g], st_im[g]]
    if reverse:
        for g in range(ng):
            init += [acc[0][g], acc[1][g]]
    fin = lax.fori_loop(0, tc, step, tuple(init), unroll=2)
    for g in range(ng):
        st_re[g] = fin[2 * g]
        st_im[g] = fin[2 * g + 1]
        if reverse:
            acc[0][g] = fin[2 * ng + 2 * g]
            acc[1][g] = fin[2 * ng + 2 * g + 1]


def _s5_fwd(name, proj, bm_re, bm_im, cm_re, cm_im, dskip, a_re, a_im):
    L = proj.shape[0]
    nb = bm_re.shape[0]
    ns, W = SLAB * nb, nb * LANES
    ng = ns // 8
    tc = min(2 * LANES, L)
    pitch = tc + 8
    wide = SLAB * LANES

    def body(u_ref, bre_ref, bim_ref, cre_ref, cim_ref, d_ref, ar_ref, ai_ref, xr_ref, xi_ref, y_ref, ya_ref,
             sb_re, sb_im, out_re, out_im, st_re, st_im):
        @pl.when(pl.program_id(0) == 0)
        def _():
            st_re[...] = jnp.zeros(st_re.shape, F32)
            st_im[...] = jnp.zeros(st_im.shape, F32)

        for j in range(nb):
            ub = u_ref[:, j * LANES:(j + 1) * LANES]
            r1 = jnp.dot(ub, bre_ref[j], preferred_element_type=F32)
            r2 = jnp.dot(ub, bim_ref[j], preferred_element_type=F32)
            for q in range(SLAB):
                sb_re[pl.ds((SLAB * j + q) * pitch, tc), :] = r1[:, q * LANES:(q + 1) * LANES]
                sb_im[pl.ds((SLAB * j + q) * pitch, tc), :] = r2[:, q * LANES:(q + 1) * LANES]
        a_r = [ar_ref[g] for g in range(ng)]
        a_i = [ai_ref[g] for g in range(ng)]
        _scan_steps(tc, pitch, ng, False, a_r, a_i, (sb_re, sb_im), None, out_re, out_im, st_re, st_im, None)
        for j in range(nb):
            x1 = [out_re[pl.ds((SLAB * j + q) * pitch, tc), :].astype(BF16) for q in range(SLAB)]
            x2 = [out_im[pl.ds((SLAB * j + q) * pitch, tc), :].astype(BF16) for q in range(SLAB)]
            for q in range(SLAB):
                xr_ref[SLAB * j + q] = x1[q]
                xi_ref[SLAB * j + q] = x2[q]
            cols = slice(j * LANES, (j + 1) * LANES)
            y = (jnp.dot(jnp.concatenate(x1, axis=1), cre_ref[j], preferred_element_type=F32)
                 + jnp.dot(jnp.concatenate(x2, axis=1), cim_ref[j], preferred_element_type=F32)
                 + d_ref[:, cols] * u_ref[:, cols].astype(F32))
            y_ref[:, cols] = y
            ya_ref[:, cols] = jax.nn.gelu(y).astype(BF16)

    full3 = lambda s: pl.BlockSpec(s, lambda i: (0, 0, 0))
    xs = pl.BlockSpec((ns, tc, LANES), lambda i: (0, i, 0))
    rows = pl.BlockSpec((tc, W), lambda i: (i, 0))
    return pl.pallas_call(
        body, name=name, grid=(L // tc,),
        in_specs=[rows, full3((nb, LANES, wide)), full3((nb, LANES, wide)), full3((nb, wide, LANES)),
                  full3((nb, wide, LANES)), pl.BlockSpec((1, W), lambda i: (0, 0)), full3((ng, 8, LANES)), full3((ng, 8, LANES))],
        out_specs=[xs, xs, rows, rows],
        out_shape=[_sds((ns, L, LANES), BF16)] * 2 + [_sds((L, W), F32), _sds((L, W), BF16)],
        scratch_shapes=[pltpu.VMEM((ns * pitch, LANES), F32)] * 4 + [pltpu.VMEM((ng, 8, LANES), F32)] * 2,
        compiler_params=_params(1),
    )(proj, bm_re, bm_im, cm_re, cm_im, dskip, a_re, a_im)


def _s5_bwd(name, dyb, proj, xs_re, xs_im, cmt_re, cmt_im, bmt_re, bmt_im, dskip, a_re, a_im):
    L = dyb.shape[0]
    nb = cmt_re.shape[0]
    ns, W = SLAB * nb, nb * LANES
    ng = ns // 8
    tc = min(LANES, L)
    pitch = tc + 8
    nt = L // tc
    wide = SLAB * LANES
    dn = (((0,), (0,)), ((), ()))

    def body(dy_ref, u_ref, xr_ref, xi_ref, cre_ref, cim_ref, bre_ref, bim_ref, d_ref, ar_ref, ai_ref,
             du_ref, gbr_ref, gbi_ref, gcr_ref, gci_ref, dar_ref, dai_ref,
             sd_re, sd_im, sx_re, sx_im, out_re, out_im, st_re, st_im, acc_re, acc_im):
        first = pl.program_id(0) == 0

        @pl.when(first)
        def _():
            for r in (st_re, st_im, acc_re, acc_im):
                r[...] = jnp.zeros(r.shape, F32)
            for r in (gbr_ref, gbi_ref, gcr_ref, gci_ref):
                r[...] = jnp.zeros(r.shape, F32)

        for j in range(nb):
            dyj = dy_ref[:, j * LANES:(j + 1) * LANES]
            r1 = jnp.dot(dyj, cre_ref[j], preferred_element_type=F32)
            r2 = jnp.dot(dyj, cim_ref[j], preferred_element_type=F32)
            for q in range(SLAB):
                s = SLAB * j + q
                sd_re[pl.ds(s * pitch, tc), :] = r1[:, q * LANES:(q + 1) * LANES]
                sd_im[pl.ds(s * pitch, tc), :] = r2[:, q * LANES:(q + 1) * LANES]
                sx_re[pl.ds(s * pitch, tc), :] = xr_ref[s].astype(F32)
                sx_im[pl.ds(s * pitch, tc), :] = xi_ref[s].astype(F32)
        a_r = [ar_ref[g] for g in range(ng)]
        a_i = [ai_ref[g] for g in range(ng)]
        _scan_steps(tc, pitch, ng, True, a_r, a_i, (sd_re, sd_im), (sx_re, sx_im), out_re, out_im, st_re, st_im,
                    (acc_re, acc_im))
        for j in range(nb):
            cols = slice(j * LANES, (j + 1) * LANES)
            l1 = jnp.concatenate([out_re[pl.ds((SLAB * j + q) * pitch, tc), :] for q in range(SLAB)], axis=1).astype(BF16)
            l2 = jnp.concatenate([out_im[pl.ds((SLAB * j + q) * pitch, tc), :] for q in range(SLAB)], axis=1).astype(BF16)
            dyj = dy_ref[:, cols]
            du = (jnp.dot(l1, bre_ref[j], preferred_element_type=F32) + jnp.dot(l2, bim_ref[j], preferred_element_type=F32)
                  + d_ref[:, cols] * dyj.astype(F32))
            du_ref[:, cols] = du.astype(BF16)
            uj = u_ref[:, cols]
            gbr_ref[j] += lax.dot_general(uj, l1, dn, preferred_element_type=F32)
            gbi_ref[j] += lax.dot_general(uj, l2, dn, preferred_element_type=F32)
            x1 = jnp.concatenate([xr_ref[SLAB * j + q] for q in range(SLAB)], axis=1)
            x2 = jnp.concatenate([xi_ref[SLAB * j + q] for q in range(SLAB)], axis=1)
            gcr_ref[j] += lax.dot_general(dyj, x1, dn, preferred_element_type=F32)
            gci_ref[j] += lax.dot_general(dyj, x2, dn, preferred_element_type=F32)
        dar_ref[...] = acc_re[...]
        dai_ref[...] = acc_im[...]

    full3 = lambda s: pl.BlockSpec(s, lambda i: (0, 0, 0))
    xs = pl.BlockSpec((ns, tc, LANES), lambda i: (0, nt - 1 - i, 0))
    rows = pl.BlockSpec((tc, W), lambda i: (nt - 1 - i, 0))
    mat_a, mat_b = full3((nb, LANES, wide)), full3((nb, wide, LANES))
    vec = full3((ng, 8, LANES))
    return pl.pallas_call(
        body, name=name, grid=(nt,),
        in_specs=[rows, rows, xs, xs, mat_a, mat_a, mat_b, mat_b, pl.BlockSpec((1, W), lambda i: (0, 0)), vec, vec],
        out_specs=[rows, mat_a, mat_a, mat_a, mat_a, vec, vec],
        out_shape=[_sds((L, W), BF16)] + [_sds((nb, LANES, wide), F32)] * 4 + [_sds((ng, 8, LANES), F32)] * 2,
        scratch_shapes=[pltpu.VMEM((ns * pitch, LANES), F32)] * 6 + [pltpu.VMEM((ng, 8, LANES), F32)] * 4,
        compiler_params=_params(1),
    )(dyb, proj, xs_re, xs_im, cmt_re, cmt_im, bmt_re, bmt_im, dskip, a_re, a_im)


def _peer(k):
    x, y, c = lax.axis_index("x"), lax.axis_index("y"), lax.axis_index("c")
    px = 1 - x if (k >> 2) & 1 else x
    py = 1 - y if (k >> 1) & 1 else y
    pc = 1 - c if k & 1 else c
    return (px, py, pc), 4 * px + 2 * py + pc


def _window(ref, kind, idx, n):
    if kind == "col":
        w = ref.shape[1] // n
        return ref.at[:, pl.ds(pl.multiple_of(idx * w, LANES), w)]
    r = ref.shape[0] // n
    return ref.at[pl.ds(pl.multiple_of(idx * r, 8), r), :]


def _all_gather(name, shards, kinds):
    n = len(shards)
    fulls = []
    for s, kind in zip(shards, kinds):
        fulls.append(_sds((s.shape[0], s.shape[1] * N_DEV) if kind == "col" else (s.shape[0] * N_DEV, s.shape[1]), s.dtype))

    def body(*refs):
        src, dst = refs[:n], refs[n:2 * n]
        send, recv, loc = refs[2 * n:]
        me = 4 * lax.axis_index("x") + 2 * lax.axis_index("y") + lax.axis_index("c")
        copies = []
        for a in range(n):
            own = pltpu.make_async_copy(src[a], _window(dst[a], kinds[a], me, N_DEV), loc.at[a])
            own.start()
            copies.append(own)
        sends = []
        for k in range(1, N_DEV):
            dev, _ = _peer(k)
            for a in range(n):
                cp = pltpu.make_async_remote_copy(
                    src_ref=src[a], dst_ref=_window(dst[a], kinds[a], me, N_DEV),
                    send_sem=send.at[a * N_DEV + k], recv_sem=recv.at[a * N_DEV + k],
                    device_id=dev, device_id_type=MESH)
                cp.start()
                sends.append(cp)
        for k in range(1, N_DEV):
            dev, pidx = _peer(k)
            for a in range(n):
                pltpu.make_async_remote_copy(
                    src_ref=src[a], dst_ref=_window(dst[a], kinds[a], pidx, N_DEV),
                    send_sem=send.at[a * N_DEV + k], recv_sem=recv.at[a * N_DEV + k],
                    device_id=dev, device_id_type=MESH).wait_recv()
        for cp in sends:
            cp.wait_send()
        for cp in copies:
            cp.wait()

    any_ = pl.BlockSpec(memory_space=pl.ANY)
    return pl.pallas_call(
        body, name=name, in_specs=[any_] * n, out_specs=[any_] * n, out_shape=fulls,
        scratch_shapes=[pltpu.SemaphoreType.DMA((n * N_DEV,)), pltpu.SemaphoreType.DMA((n * N_DEV,)),
                        pltpu.SemaphoreType.DMA((n,))],
        compiler_params=pltpu.CompilerParams(has_side_effects=True),
    )(*shards)


def _xfer_refs(mode, kinds, a, src, dst, me, pidx):
    if mode == "gather":
        return src[a], _window(dst[a], kinds[a], me, N_DEV), _window(dst[a], kinds[a], pidx, N_DEV)
    return _window(src[a], kinds[a], pidx, N_DEV), dst[a].at[me], dst[a].at[pidx]


def _xfer_out_shapes(mode, arrs, kinds):
    outs = []
    for s, kind in zip(arrs, kinds):
        if mode == "gather":
            outs.append((s.shape[0], s.shape[1] * N_DEV) if kind == "col" else (s.shape[0] * N_DEV, s.shape[1]))
        else:
            outs.append((N_DEV,) + ((s.shape[0], s.shape[1] // N_DEV) if kind == "col" else (s.shape[0] // N_DEV, s.shape[1])))
    return outs


def _sc_xfer(name, mode, arrs, kinds, collective_id):
    n = len(arrs)
    shapes = _xfer_out_shapes(mode, arrs, kinds)
    hbm = pltpu.MemorySpace.HBM
    src = [jax.new_ref(a, memory_space=hbm) for a in arrs]
    dst = [jax.empty_ref(_sds(shp, a.dtype), memory_space=hbm) for shp, a in zip(shapes, arrs)]

    @pl.kernel(mesh=plsc.ScalarSubcoreMesh(axis_name="seq", num_cores=1), name=name,
               scratch_types=(pltpu.SemaphoreType.DMA((n * N_DEV,)), pltpu.SemaphoreType.DMA((n * N_DEV,)),
                              pltpu.SemaphoreType.DMA((n,))),
               compiler_params=pltpu.CompilerParams(collective_id=collective_id))
    def launch(send, recv, loc):
        barrier = pltpu.get_barrier_semaphore()
        for k in range(1, N_DEV):
            pl.semaphore_signal(barrier, inc=1, device_id=_peer(k)[0], device_id_type=MESH)
        pl.semaphore_wait(barrier, N_DEV - 1)
        me = 4 * lax.axis_index("x") + 2 * lax.axis_index("y") + lax.axis_index("c")
        own, sends = [], []
        for a in range(n):
            s, _, d = _xfer_refs(mode, kinds, a, src, dst, me, me)
            own.append(pltpu.make_async_copy(s, d, loc.at[a]))
            own[-1].start()
        for k in range(1, N_DEV):
            dev, pidx = _peer(k)
            for a in range(n):
                s, d, _ = _xfer_refs(mode, kinds, a, src, dst, me, pidx)
                sends.append(pltpu.make_async_remote_copy(src_ref=s, dst_ref=d, send_sem=send.at[a * N_DEV + k],
                                                          recv_sem=recv.at[a * N_DEV + k], device_id=dev, device_id_type=MESH))
                sends[-1].start()
        for cp in own:
            cp.wait()
        for k in range(1, N_DEV):
            dev, pidx = _peer(k)
            for a in range(n):
                s, _, land = _xfer_refs(mode, kinds, a, src, dst, me, pidx)
                pltpu.make_async_remote_copy(src_ref=s, dst_ref=land, send_sem=send.at[a * N_DEV + k],
                                             recv_sem=recv.at[a * N_DEV + k], device_id=dev, device_id_type=MESH).wait_recv()
        for cp in sends:
            cp.wait_send()

    launch()
    return [d[...] for d in dst]


def _sc_gather(name, arrs, kinds, collective_id):
    n = len(arrs)
    pairs = 7
    shapes = _xfer_out_shapes("gather", arrs, kinds)
    hbm = pltpu.MemorySpace.HBM
    src = [jax.new_ref(a, memory_space=hbm) for a in arrs]
    dst = [jax.empty_ref(_sds(shp, a.dtype), memory_space=hbm) for shp, a in zip(shapes, arrs)]

    @pl.kernel(mesh=plsc.ScalarSubcoreMesh(axis_name="seq", num_cores=1), name=name,
               scratch_types=(pltpu.SemaphoreType.DMA((n * pairs,)), pltpu.SemaphoreType.DMA((n * pairs,)),
                              pltpu.SemaphoreType.DMA((n,))),
               compiler_params=pltpu.CompilerParams(collective_id=collective_id))
    def launch(send, recv, loc):
        x, y, c = lax.axis_index("x"), lax.axis_index("y"), lax.axis_index("c")
        me = 4 * x + 2 * y + c
        sib = (x, y, 1 - c)
        chips = []
        for fx, fy in ((1, 0), (0, 1), (1, 1)):
            px, py = (1 - x if fx else x), (1 - y if fy else y)
            chips.append(((px, py, c), 4 * px + 2 * py + c, 4 * px + 2 * py + (1 - c)))
        barrier = pltpu.get_barrier_semaphore()
        for dev in [sib] + [ch[0] for ch in chips]:
            pl.semaphore_signal(barrier, inc=1, device_id=dev, device_id_type=MESH)
        pl.semaphore_wait(barrier, 4)

        def win(a, idx):
            return _window(dst[a], kinds[a], idx, N_DEV)

        def rcopy(a, p, s, d, dev):
            return pltpu.make_async_remote_copy(src_ref=s, dst_ref=d, send_sem=send.at[a * pairs + p],
                                                recv_sem=recv.at[a * pairs + p], device_id=dev, device_id_type=MESH)

        own, sends = [], []
        for a in range(n):
            own.append(pltpu.make_async_copy(src[a], win(a, me), loc.at[a]))
            own[-1].start()
        for j, (dev, _, _) in enumerate(chips):
            for a in range(n):
                sends.append(rcopy(a, 1 + j, src[a], win(a, me), dev))
                sends[-1].start()
        for a in range(n):
            sends.append(rcopy(a, 0, src[a], win(a, me), sib))
            sends[-1].start()
        for j, (dev, idx, _) in enumerate(chips):
            for a in range(n):
                rcopy(a, 1 + j, src[a], win(a, idx), dev).wait_recv()
                sends.append(rcopy(a, 4 + j, win(a, idx), win(a, idx), sib))
                sends[-1].start()
        for cp in own:
            cp.wait()
        for a in range(n):
            rcopy(a, 0, src[a], win(a, 4 * x + 2 * y + (1 - c)), sib).wait_recv()
        for j, (_, _, sidx) in enumerate(chips):
            for a in range(n):
                rcopy(a, 4 + j, src[a], win(a, sidx), sib).wait_recv()
        for cp in sends:
            cp.wait_send()

    launch()
    return [d[...] for d in dst]


_SEQ_IDS = {"gather_in": 7, "gather_mix": 1, "gather_up": 2, "gather_down": 9, "grads_down": 3, "grads_up": 8,
            "grads_mix": 4, "grads_small": 5, "grads_in": 6}


def _launch(name, mode, arrs, kinds):
    if mode == "gather":
        return _sc_gather(name, list(arrs), kinds, _SEQ_IDS[name])
    return _sc_xfer(name, mode, list(arrs), kinds, _SEQ_IDS[name])


def _adamw(name, parts, w, m, v):
    P, R, C = parts.shape
    sub = 16 if parts.dtype == BF16 else 8
    tr = R if R * C <= (1 << 18) else _tile(R, max(sub, (1 << 18) // C), sub)

    def body(p_ref, w_ref, m_ref, v_ref, g_ref, d_ref, nm_ref, nv_ref):
        g = p_ref[0].astype(F32)
        for s in range(1, P):
            g = g + p_ref[s].astype(F32)
        m2 = ADAM_B1 * m_ref[...] + (1.0 - ADAM_B1) * g
        v2 = ADAM_B2 * v_ref[...] + (1.0 - ADAM_B2) * (g * g)
        m_hat = m2 / (1.0 - ADAM_B1 ** ADAM_STEP)
        v_hat = v2 / (1.0 - ADAM_B2 ** ADAM_STEP)
        g_ref[...] = g
        d_ref[...] = -ADAM_LR * (m_hat / (jnp.sqrt(v_hat) + ADAM_EPS) + ADAM_WD * w_ref[...])
        nm_ref[...] = m2
        nv_ref[...] = v2

    sp = pl.BlockSpec((tr, C), lambda i: (i, 0))
    return pl.pallas_call(
        body, name=name, grid=(R // tr,),
        in_specs=[pl.BlockSpec((P, tr, C), lambda i: (0, i, 0)), sp, sp, sp], out_specs=[sp] * 4,
        out_shape=[_sds((R, C), F32)] * 4, compiler_params=_params(1),
    )(parts, w, m, v)


def _pack(arrs, row_mult=8):
    pieces, total = [], 0
    for a in arrs:
        f = a.reshape(-1).astype(F32)
        pad = (-f.shape[0]) % (8 * LANES)
        pieces.append(jnp.pad(f, (0, pad)) if pad else f)
        total += f.shape[0] + pad
    tail = (-total) % (row_mult * LANES)
    if tail:
        pieces.append(jnp.zeros((tail,), F32))
    return jnp.concatenate(pieces).reshape(-1, LANES)


def _unpack(buf, shapes, lead=()):
    out, row = [], 0
    for shp in shapes:
        size = 1
        for d in shp:
            size *= d
        rows = -(-size // (8 * LANES)) * 8
        piece = buf[..., row:row + rows, :].reshape(lead + (rows * LANES,))[..., :size]
        out.append(piece.reshape(lead + tuple(shp)))
        row += rows
    return out


def kernel(x, norm_tok, w_in, a_re, a_im, log_dt, b_re, b_im, c_re, c_im, d_skip, w_glu, w_ssm_out, conv_w, conv_b, w_conv_out, w_o, norm_ffn, w_up, ffn_conv_w, ffn_conv_b, w_down, norm_final, loss_target, m_norm_tok, m_w_in, m_a_re, m_a_im, m_log_dt, m_b_re, m_b_im, m_c_re, m_c_im, m_d_skip, m_w_glu, m_w_ssm_out, m_conv_w, m_conv_b, m_w_conv_out, m_w_o, m_norm_ffn, m_w_up, m_ffn_conv_w, m_ffn_conv_b, m_w_down, m_norm_final, v_norm_tok, v_w_in, v_a_re, v_a_im, v_log_dt, v_b_re, v_b_im, v_c_re, v_c_im, v_d_skip, v_w_glu, v_w_ssm_out, v_conv_w, v_conv_b, v_w_conv_out, v_w_o, v_norm_ffn, v_w_up, v_ffn_conv_w, v_ffn_conv_b, v_w_down, v_norm_final):
    args = dict(locals())
    L, D = x.shape[1], x.shape[2]
    G, P, H = b_re.shape[1], b_re.shape[2], b_re.shape[3]
    SW = G * H
    CW = conv_b.shape[1]
    FF = ffn_conv_b.shape[1]
    GP = G * P
    nb = SW // LANES
    gpb = LANES // H
    me = 4 * lax.axis_index("x") + 2 * lax.axis_index("y") + lax.axis_index("c")
    tm = _tile(L, 256, 16)
    x2 = x[0]
    tgt = loss_target[0]

    big = [("w_in", "col"), ("w_glu", "row"), ("w_ssm_out", "col"), ("w_conv_out", "col"), ("w_o", "row"),
           ("w_up", "col"), ("w_down", "row")]
    shards = [_cast_bf16("cast_" + n, args[n][0]) for n, _ in big]
    small_in = _pack([conv_w[0], ffn_conv_w[0]])
    kind = dict(big)
    mixw, ffnw = ["w_glu", "w_ssm_out", "w_conv_out", "w_o"], ["w_up", "w_down"]
    shard = dict(zip([n for n, _ in big], shards))
    gathered = _launch("gather_in", "gather", [shard["w_in"], small_in], ["col", "row"])
    W = {"w_in": gathered[0]}
    W.update(zip(mixw, _launch("gather_mix", "gather", [shard[n] for n in mixw], [kind[n] for n in mixw])))
    for n in ffnw:
        W[n] = _launch("gather_" + n[2:], "gather", [shard[n]], [kind[n]])[0]
    cw_parts, fcw_parts = _unpack(gathered[-1].reshape(N_DEV, -1, LANES), [conv_w.shape[1:], ffn_conv_w.shape[1:]], (N_DEV,))
    conv_w_full = jnp.moveaxis(cw_parts, 0, 1).reshape(3, CW)
    ffn_conv_w_full = jnp.moveaxis(fcw_parts, 0, 1).reshape(3, FF)

    ar_row, ai_row = a_re.reshape(1, GP), a_im.reshape(1, GP)
    ldt_row = jnp.broadcast_to(log_dt.reshape(G, 1), (G, P)).reshape(1, GP)
    brt = jnp.transpose(b_re[0], (2, 0, 1)).reshape(H, GP)
    bit = jnp.transpose(b_im[0], (2, 0, 1)).reshape(H, GP)
    abar_re, abar_im, bbar_re, bbar_im = _prep_fwd("s5_prep", ar_row, ai_row, ldt_row, brt, bit)
    eye = jnp.eye(gpb, dtype=F32)

    def b_blocks(bt):
        return jnp.einsum("ab,hjbp->jahbp", eye, bt.reshape(H, nb, gpb, P)).reshape(nb, LANES, gpb * P)

    def c_blocks(c):
        return jnp.einsum("ab,jahp->jbpah", eye, c.reshape(nb, gpb, H, P)).reshape(nb, gpb * P, LANES)

    def diag_blocks(mat):
        return jnp.einsum("jahap->hjap", mat.reshape(nb, gpb, H, gpb, P))

    bm_re, bm_im = b_blocks(bbar_re), b_blocks(bbar_im)
    cm_re, cm_im = c_blocks(c_re[0]), -c_blocks(c_im[0])
    a3_re, a3_im = abar_re.reshape(-1, 8, LANES), abar_im.reshape(-1, 8, LANES)
    dskip_row = d_skip.reshape(1, SW)

    cbs = SW // LANES
    cb_v, cb_gb, cb_gc = cbs, cbs + CW // LANES, cbs + 2 * CW // LANES
    cb_ma = (SW + 3 * CW) // D
    xn = _rms_fwd("rms_tok", x2, norm_tok, tm)
    proj = _mm("proj", xn, W["w_in"], "nn", out_dtype=BF16)
    xs_re, xs_im, y, ya = _s5_fwd("s5_fwd", proj, bm_re.astype(BF16), bm_im.astype(BF16), cm_re.astype(BF16),
                                  cm_im.astype(BF16), dskip_row, a3_re, a3_im)
    g1, ya2 = _mm_glu("glu_gate", ya, W["w_glu"], y)
    za = _mm("ssm_out", ya2, W["w_ssm_out"], "nn", out_dtype=BF16)
    q = _convb_fwd("convb", proj, cb_v, cb_gb, cb_gc, conv_w_full, conv_b)
    zb, merged = _mm_merge("conv_out_merge", q, W["w_conv_out"], proj, cb_ma, cb_ma + 1, za)
    h1, hn = _mm_res_rms("mix_out_rms", merged, W["w_o"], x2, norm_ffn)
    hh, f = _ffn_up_act("ffn_up_act", hn, W["w_up"], ffn_conv_w_full, ffn_conv_b)
    o2 = _mm("ffn_down", f, W["w_down"], "nn", tk=2816)
    dh2, dh2b, g_norm_final, loss_part = _final("final", h1, o2, norm_final.reshape(1, D), tgt, tm)

    gw_down = _mm("gw_down", f, dh2b, "tn", out_dtype=BF16, tm=1408, tn=512, tk=L)
    dh2b, gw_down = lax.optimization_barrier((dh2b, gw_down))
    parts = {"w_down": _launch("grads_down", "exchange", [gw_down], [kind["w_down"]])[0]}
    dhh, g_ffn_conv_w, g_ffn_conv_b = _ffn_down_bwd_act("ffn_down_bwd_act", dh2b, W["w_down"], hh, ffn_conv_w_full,
                                                         ffn_conv_b)
    nhalf = lambda t: FF // t
    gw_up = _mm("gw_up", hn, dhh, "tn", out_dtype=BF16, tn=_tile(FF, 1024), tk=L, dims=(D, 2 * FF, L),
                b_spec=lambda a, b, c: pl.BlockSpec((None, c, b), lambda i, j, k: (j // nhalf(b), k, j % nhalf(b))))
    dhh, gw_up = lax.optimization_barrier((dhh, gw_up))
    parts["w_up"] = _launch("grads_up", "exchange", [gw_up], [kind["w_up"]])[0]
    dhn = _mm("d_ffn_in", dhh, W["w_up"], "nt", out_dtype=BF16, tk=_tile(FF, 2816), dims=(L, D, 2 * FF),
              a_spec=lambda a, b, c: pl.BlockSpec((None, a, c), lambda i, j, k: (k // nhalf(c), i, k % nhalf(c))))
    dh1, dh1b, g_norm_ffn = _rms_bwd("rms_ffn_bwd", dhn, h1, norm_ffn, dh2, tm, True)

    gw_o = _mm("gw_o", merged, dh1b, "tn", out_dtype=BF16, tk=L)
    dh1b, gw_o = lax.optimization_barrier((dh1b, gw_o))
    dza, dzb, dma, dmb = _mm_merge_bwd("d_merged_bwd", dh1b, W["w_o"], proj, cb_ma, cb_ma + 1, za, zb)
    dq = _mm("d_q", dzb, W["w_conv_out"], "nt", out_dtype=BF16)
    gw_conv_out = _mm("gw_conv_out", q, dzb, "tn", out_dtype=BF16, tk=L)
    dq, gw_conv_out = lax.optimization_barrier((dq, gw_conv_out))
    dv, dgb, dgc, g_conv_w, g_conv_b = _convb_bwd("convb_bwd", proj, cb_v, cb_gb, cb_gc, conv_w_full, conv_b, dq)
    dya2 = _mm("d_ya2", dza, W["w_ssm_out"], "nt", out_dtype=BF16)
    gw_ssm_out = _mm("gw_ssm_out", ya2, dza, "tn", out_dtype=BF16, tk=L)
    dya2, gw_ssm_out = lax.optimization_barrier((dya2, gw_ssm_out))
    dy_direct, dg1 = _glu_bwd("glu_bwd", y, g1, dya2, tm)
    dya_g = _mm("d_ya_gate", dg1, W["w_glu"], "nt", out_dtype=BF16)
    gw_glu = _mm("gw_glu", ya, dg1, "tn", out_dtype=BF16, tk=L)
    dya_g, gw_glu = lax.optimization_barrier((dya_g, gw_glu))
    parts_mix = _launch("grads_mix", "exchange", [gw_glu, gw_ssm_out, gw_conv_out, gw_o], [kind[n] for n in mixw])
    dyb, g_dskip = _gelu_bwd("gelu_bwd", y, dy_direct, dya_g, proj, tm)
    swap = lambda m: jnp.swapaxes(m, 1, 2).astype(BF16)
    du, gb_re, gb_im, gc_re, gc_im, dab_re, dab_im = _s5_bwd(
        "s5_bwd", dyb, proj, xs_re, xs_im, swap(cm_re), swap(cm_im), swap(bm_re), swap(bm_im), dskip_row, a3_re, a3_im)
    g_ar, g_ai, g_ldt, g_brt, g_bit = _prep_bwd(
        "s5_prep_bwd", ar_row, ai_row, ldt_row, brt, bit, dab_re.reshape(1, GP), dab_im.reshape(1, GP),
        diag_blocks(gb_re).reshape(H, GP), diag_blocks(gb_im).reshape(H, GP))
    small = dict(
        a_re=g_ar.reshape(1, G, P), a_im=g_ai.reshape(1, G, P),
        log_dt=g_ldt.reshape(G, P).sum(axis=1).reshape(1, G),
        b_re=jnp.transpose(g_brt.reshape(H, G, P), (1, 2, 0))[None], b_im=jnp.transpose(g_bit.reshape(H, G, P), (1, 2, 0))[None],
        c_re=jnp.transpose(diag_blocks(gc_re), (1, 2, 0, 3)).reshape(1, G, H, P),
        c_im=-jnp.transpose(diag_blocks(gc_im), (1, 2, 0, 3)).reshape(1, G, H, P),
        d_skip=g_dskip.reshape(1, G, H), conv_b=g_conv_b, norm_ffn=g_norm_ffn, ffn_conv_b=g_ffn_conv_b,
        norm_final=g_norm_final.reshape(D), conv_w=g_conv_w[None], ffn_conv_w=g_ffn_conv_w[None])
    rep = ["a_re", "a_im", "log_dt", "b_re", "b_im", "c_re", "c_im", "d_skip", "conv_b", "norm_ffn", "ffn_conv_b", "norm_final"]
    order = rep + ["conv_w", "ffn_conv_w"]
    full_shapes = {n: args[n].shape for n in rep}
    full_shapes["conv_w"], full_shapes["ffn_conv_w"] = (1, 3, CW), (1, 3, FF)
    rep_pack = _pack([small[n] for n in rep], LANES)
    rep_rows = rep_pack.shape[0]
    gpack = jnp.concatenate([loss_part, rep_pack, _pack([small["conv_w"], small["ffn_conv_w"]])], axis=0)
    rep0 = loss_part.shape[0]
    rows = gpack.shape[0]
    du, gpack = lax.optimization_barrier((du, gpack))
    gall = _launch("grads_small", "gather", [gpack], ["row"])[0]

    dproj = _concat_cols("dproj", [du, dv, dgb, dgc, dma, dmb], tm)
    gw_in = _mm("gw_in", xn, dproj, "tn", out_dtype=BF16, tk=L)
    dproj, gw_in = lax.optimization_barrier((dproj, gw_in))
    parts_in = _launch("grads_in", "exchange", [gw_in], ["col"])
    dxn = _mm("d_xn", dproj, W["w_in"], "nt", out_dtype=BF16, tk=4096)
    grad_x, g_norm_tok = _rms_bwd("rms_tok_bwd", dxn, x2, norm_tok, dh1, tm, False)

    res = {}

    def big_update(n):
        res[n] = [r[None] for r in _adamw("adamw_" + n, parts[n], args[n][0], args["m_" + n][0], args["v_" + n][0])]

    def after(xs, dep):
        return lax.optimization_barrier((list(xs), dep))[0]

    parts["w_down"] = after([parts["w_down"]], grad_x)[0]
    big_update("w_down")
    parts["w_up"] = after([parts["w_up"]], res["w_down"][1])[0]
    big_update("w_up")
    parts.update(zip(mixw, after(parts_mix, [res[n][1] for n in ffnw])))
    for n in mixw:
        big_update(n)
    gall = after([gall], [res[n][1] for n in mixw])[0].reshape(N_DEV, rows, LANES)
    gcw, gfcw = _unpack(gall[:, rep0 + rep_rows:], [full_shapes["conv_w"], full_shapes["ffn_conv_w"]], (N_DEV,))
    cws, fcws = CW // N_DEV, FF // N_DEV
    gcw = lax.dynamic_slice_in_dim(gcw[:, 0], me * cws, cws, axis=2)
    gfcw = lax.dynamic_slice_in_dim(gfcw[:, 0], me * fcws, fcws, axis=2)
    res["conv_w"] = [r[None] for r in _adamw("adamw_conv_w", gcw, conv_w[0], m_conv_w[0], v_conv_w[0])]
    res["ffn_conv_w"] = [r[None] for r in _adamw("adamw_ffn_conv_w", gfcw, ffn_conv_w[0], m_ffn_conv_w[0], v_ffn_conv_w[0])]
    rep_out = _adamw("adamw_small", gall[:, rep0:rep0 + rep_rows], _pack([args[n] for n in rep], LANES),
                     _pack([args["m_" + n] for n in rep], LANES), _pack([args["v_" + n] for n in rep], LANES))
    rep_out = [_unpack(r, [full_shapes[n] for n in rep]) for r in rep_out]
    for i, n in enumerate(rep):
        res[n] = [r[i] for r in rep_out]
    nt_pack = after([_pack([g_norm_tok])], [res[n][1] for n in ("a_re", "conv_w", "ffn_conv_w")])
    nt_all = _all_gather("gather_norm_tok_grad", nt_pack, ["row"])[0].reshape(N_DEV, -1, LANES)
    nt_out = _adamw("adamw_norm_tok", nt_all, _pack([norm_tok]), _pack([m_norm_tok]), _pack([v_norm_tok]))
    res["norm_tok"] = [_unpack(r, [norm_tok.shape])[0] for r in nt_out]
    parts["w_in"] = after(parts_in, nt_out[0])[0]
    big_update("w_in")

    loss = jnp.sum(gall[:, 0, 0])
    names = ["norm_tok", "w_in", "a_re", "a_im", "log_dt", "b_re", "b_im", "c_re", "c_im", "d_skip", "w_glu", "w_ssm_out",
             "conv_w", "conv_b", "w_conv_out", "w_o", "norm_ffn", "w_up", "ffn_conv_w", "ffn_conv_b", "w_down", "norm_final"]
    out = [loss, grad_x[None]]
    for slot in range(4):
        out += [res[n][slot] for n in names]
    return tuple(out)
```

```python
import jax
import jax.numpy as jnp
from jax import lax
from jax.experimental import pallas as pl
from jax.experimental.pallas import tpu as pltpu
from jax.experimental.pallas import tpu_sc as plsc

F32 = jnp.float32
BF16 = jnp.bfloat16
N_DEV = 8
LANES = 128
SLAB = 4
EPS = 1e-6
ADAM_LR = 0.001
ADAM_B1 = 0.9
ADAM_B2 = 0.999
ADAM_EPS = 1e-08
ADAM_WD = 0.01
ADAM_STEP = 10
VMEM_LIMIT = 56 * 1024 * 1024
MESH = pl.DeviceIdType.MESH


def _tile(n, pref, mult=LANES):
    best = None
    t = mult
    while t <= min(n, pref):
        if n % t == 0:
            best = t
        t += mult
    return best if best is not None else n


def _params(ndim):
    return pltpu.CompilerParams(dimension_semantics=("arbitrary",) * ndim, vmem_limit_bytes=VMEM_LIMIT)


def _sds(shape, dtype):
    return jax.ShapeDtypeStruct(tuple(shape), dtype)


def _mm(name, a, b, mode, *, out_dtype=F32, tm=1024, tn=1024, tk=2048, dims=None, a_spec=None, b_spec=None):
    if dims is None:
        if mode == "nn":
            (M, K), N = a.shape, b.shape[1]
        elif mode == "nt":
            (M, K), N = a.shape, b.shape[0]
        else:
            (K, M), N = a.shape, b.shape[1]
    else:
        M, N, K = dims
    tm, tn, tk = _tile(M, tm), _tile(N, tn), _tile(K, tk)
    nk = K // tk
    if mode == "nn":
        dn = (((1,), (0,)), ((), ()))
        sa = pl.BlockSpec((tm, tk), lambda i, j, k: (i, k))
        sb = pl.BlockSpec((tk, tn), lambda i, j, k: (k, j))
    elif mode == "nt":
        dn = (((1,), (1,)), ((), ()))
        sa = pl.BlockSpec((tm, tk), lambda i, j, k: (i, k))
        sb = pl.BlockSpec((tn, tk), lambda i, j, k: (j, k))
    else:
        dn = (((0,), (0,)), ((), ()))
        sa = pl.BlockSpec((tk, tm), lambda i, j, k: (k, i))
        sb = pl.BlockSpec((tk, tn), lambda i, j, k: (k, j))
    sa = a_spec(tm, tn, tk) if a_spec is not None else sa
    sb = b_spec(tm, tn, tk) if b_spec is not None else sb
    use_acc = nk > 1 and out_dtype != F32

    def body(a_ref, b_ref, o_ref, *acc):
        k = pl.program_id(2)
        p = lax.dot_general(a_ref[...], b_ref[...], dn, preferred_element_type=F32)
        if nk == 1:
            o_ref[...] = p.astype(out_dtype)
        else:
            tgt = acc[0] if use_acc else o_ref

            @pl.when(k == 0)
            def _():
                tgt[...] = p

            @pl.when(k > 0)
            def _():
                tgt[...] += p

            if use_acc:
                @pl.when(k == nk - 1)
                def _():
                    o_ref[...] = acc[0][...].astype(out_dtype)

    return pl.pallas_call(
        body, name=name, grid=(M // tm, N // tn, nk),
        in_specs=[sa, sb], out_specs=pl.BlockSpec((tm, tn), lambda i, j, k: (i, j)),
        out_shape=_sds((M, N), out_dtype),
        scratch_shapes=[pltpu.VMEM((tm, tn), F32)] if use_acc else [],
        compiler_params=_params(3),
    )(a, b)


def _mm_rows(name, a, b, extra, outs, epilogue, *, tm, tk, nt=False):
    M, K = a.shape
    N = b.shape[0] if nt else b.shape[1]
    tm, tk = _tile(M, tm, 16), _tile(K, tk)
    nk = K // tk
    ne, no = len(extra), len(outs)
    dn = (((1,), (1 if nt else 0,)), ((), ()))
    b_spec = pl.BlockSpec((N, tk), lambda i, k: (0, k)) if nt else pl.BlockSpec((tk, N), lambda i, k: (k, 0))

    def body(a_ref, b_ref, *rest):
        i, k = pl.program_id(0), pl.program_id(1)
        p = lax.dot_general(a_ref[...], b_ref[...], dn, preferred_element_type=F32)
        if nk == 1:
            epilogue(i, p, rest[:ne], rest[ne:ne + no])
            return
        acc = rest[ne + no]
        _acc_rows(k, acc, p)

        @pl.when(k == nk - 1)
        def _():
            epilogue(i, acc, rest[:ne], rest[ne:ne + no])

    return pl.pallas_call(
        body, name=name, grid=(M // tm, nk),
        in_specs=[pl.BlockSpec((tm, tk), lambda i, k: (i, k)), b_spec] + [s for _, s in extra],
        out_specs=[s for _, s in outs], out_shape=[o for o, _ in outs],
        scratch_shapes=[pltpu.VMEM((tm, N), F32)] if nk > 1 else [], compiler_params=_params(2),
    )(a, b, *[e for e, _ in extra])


def _rows(name, body, L, tm, ins, outs):
    return pl.pallas_call(
        body, name=name, grid=(L // tm,),
        in_specs=[s for _, s in ins], out_specs=[s for _, s in outs],
        out_shape=[o for o, _ in outs], compiler_params=_params(1),
    )(*[a for a, _ in ins])


def _rs(tm, w, cb=0):
    return pl.BlockSpec((tm, w), lambda i: (i, cb))


def _fs(shape):
    return pl.BlockSpec(tuple(shape), lambda i: (0,) * len(shape))


def _acc_rows(i, ref, part):
    @pl.when(i == 0)
    def _():
        ref[...] = part

    @pl.when(i > 0)
    def _():
        ref[...] += part


def _cast_bf16(name, w):
    R, C = w.shape
    tr = _tile(R, max(16, (1 << 20) // C), 16)

    def body(w_ref, o_ref):
        o_ref[...] = w_ref[...].astype(BF16)

    return _rows(name, body, R, tr, [(w, _rs(tr, C))], [(_sds((R, C), BF16), _rs(tr, C))])[0]


def _concat_cols(name, pieces, tm):
    L = pieces[0].shape[0]
    widths = [p.shape[1] for p in pieces]

    def body(*refs):
        o_ref, off = refs[-1], 0
        for p_ref, w in zip(refs[:-1], widths):
            o_ref[:, off:off + w] = p_ref[...]
            off += w

    return _rows(name, body, L, tm, [(p, _rs(tm, w)) for p, w in zip(pieces, widths)],
                 [(_sds((L, sum(widths)), pieces[0].dtype), _rs(tm, sum(widths)))])[0]


def _rms_fwd(name, x, g, tm):
    L, D = x.shape

    def body(x_ref, g_ref, o_ref):
        xv = x_ref[...]
        r = lax.rsqrt(jnp.mean(xv * xv, axis=-1, keepdims=True) + EPS)
        o_ref[...] = (xv * r * g_ref[...]).astype(BF16)

    return _rows(name, body, L, tm, [(x, _rs(tm, D)), (g, _fs((1, D)))], [(_sds((L, D), BF16), _rs(tm, D))])[0]


def _mm_res_rms(name, a, w, x, g):
    L, D = x.shape

    def epilogue(i, o, extra, outs):
        x_ref, g_ref = extra
        for r0 in range(0, tm, sub):
            rs = slice(r0, r0 + sub)
            h = x_ref[rs, :] + o[rs, :]
            r = lax.rsqrt(jnp.mean(h * h, axis=-1, keepdims=True) + EPS)
            outs[0][rs, :] = h
            outs[1][rs, :] = (h * r * g_ref[...]).astype(BF16)

    tm = _tile(L, 512, 16)
    sub = _tile(tm, LANES, 16)
    row = pl.BlockSpec((tm, D), lambda i, k: (i, 0))
    return _mm_rows(name, a, w, [(x, row), (g, pl.BlockSpec((1, D), lambda i, k: (0, 0)))],
                    [(_sds((L, D), F32), row), (_sds((L, D), BF16), row)], epilogue, tm=tm, tk=a.shape[1])


def _rms_bwd(name, dn, h, g, dres, tm, with_bf16):
    L, D = h.shape

    def body(dn_ref, h_ref, g_ref, dres_ref, dh_ref, *rest):
        i = pl.program_id(0)
        h = h_ref[...]
        r = lax.rsqrt(jnp.mean(h * h, axis=-1, keepdims=True) + EPS)
        xh = h * r
        d = dn_ref[...].astype(F32)
        dxh = d * g_ref[...]
        dh = dres_ref[...] + r * (dxh - xh * jnp.mean(dxh * xh, axis=-1, keepdims=True))
        dh_ref[...] = dh
        if with_bf16:
            rest[0][...] = dh.astype(BF16)
        _acc_rows(i, rest[-1], jnp.sum(d * xh, axis=0, keepdims=True))

    outs = [(_sds((L, D), F32), _rs(tm, D))]
    if with_bf16:
        outs.append((_sds((L, D), BF16), _rs(tm, D)))
    outs.append((_sds((1, D), F32), _fs((1, D))))
    return _rows(name, body, L, tm, [(dn, _rs(tm, D)), (h, _rs(tm, D)), (g, _fs((1, D))), (dres, _rs(tm, D))], outs)


def _final(name, h1, o2, g, tgt, tm):
    L, D = h1.shape

    def body(h1_ref, o2_ref, g_ref, t_ref, dh_ref, dhb_ref, dg_ref, loss_ref):
        i = pl.program_id(0)
        h = h1_ref[...] + o2_ref[...]
        r = lax.rsqrt(jnp.mean(h * h, axis=-1, keepdims=True) + EPS)
        xh = h * r
        gv = g_ref[...]
        e = xh * gv - t_ref[...]
        part = 0.5 * jnp.sum(jnp.mean(e * e, axis=-1, keepdims=True), axis=0, keepdims=True)
        dy = e / D
        dxh = dy * gv
        dh = r * (dxh - xh * jnp.mean(dxh * xh, axis=-1, keepdims=True))
        dh_ref[...] = dh
        dhb_ref[...] = dh.astype(BF16)
        _acc_rows(i, dg_ref, jnp.sum(dy * xh, axis=0, keepdims=True))
        _acc_rows(i, loss_ref, jnp.broadcast_to(part, (8, LANES)))

    return _rows(name, body, L, tm,
                 [(h1, _rs(tm, D)), (o2, _rs(tm, D)), (g, _fs((1, D))), (tgt, _rs(tm, D))],
                 [(_sds((L, D), F32), _rs(tm, D)), (_sds((L, D), BF16), _rs(tm, D)),
                  (_sds((1, D), F32), _fs((1, D))), (_sds((8, LANES), F32), _fs((8, LANES)))])


def _gelu_and_grad(x):
    c, k = 0.7978845608028654, 0.044715
    x2 = x * x
    t = jnp.tanh(x * (c + (c * k) * x2))
    ht = 0.5 * t
    s = 0.5 + ht
    return x * s, s + x * ((0.5 - ht * t) * (c + (3.0 * c * k) * x2))


def _glu_fn(y, g1):
    ya = jax.nn.gelu(y)
    return ya * jax.nn.sigmoid(g1)


def _mm_glu(name, ya, w, y):
    L, W = y.shape

    def epilogue(i, o, extra, outs):
        for r0 in range(0, tm, sub):
            rs = slice(r0, r0 + sub)
            g16 = o[rs, :].astype(BF16)
            outs[0][rs, :] = g16
            outs[1][rs, :] = _glu_fn(extra[0][rs, :], g16.astype(F32)).astype(BF16)

    tm = _tile(L, 512, 16)
    sub = _tile(tm, LANES, 16)
    row = pl.BlockSpec((tm, W), lambda i, k: (i, 0))
    return _mm_rows(name, ya, w, [(y, row)], [(_sds((L, W), BF16), row), (_sds((L, W), BF16), row)], epilogue,
                    tm=tm, tk=ya.shape[1])


def _glu_bwd(name, y, g1, dya2, tm):
    L, W = y.shape

    def body(y_ref, g_ref, d_ref, dy_ref, dg_ref):
        _, vjp = jax.vjp(_glu_fn, y_ref[...], g_ref[...].astype(F32))
        dy, dg = vjp(d_ref[...].astype(F32))
        dy_ref[...] = dy
        dg_ref[...] = dg.astype(BF16)

    return _rows(name, body, L, tm, [(y, _rs(tm, W)), (g1, _rs(tm, W)), (dya2, _rs(tm, W))],
                 [(_sds((L, W), F32), _rs(tm, W)), (_sds((L, W), BF16), _rs(tm, W))])


def _gelu_bwd(name, y, dy_direct, dya_g, proj, tm):
    L, W = y.shape

    def body(y_ref, dd_ref, dg_ref, u_ref, dyb_ref, dsk_ref):
        i = pl.program_id(0)
        dy = dd_ref[...] + dg_ref[...].astype(F32) * _gelu_and_grad(y_ref[...])[1]
        dyb_ref[...] = dy.astype(BF16)
        _acc_rows(i, dsk_ref, jnp.sum(dy * u_ref[...].astype(F32), axis=0, keepdims=True))

    return _rows(name, body, L, tm,
                 [(y, _rs(tm, W)), (dy_direct, _rs(tm, W)), (dya_g, _rs(tm, W)), (proj, _rs(tm, W, 0))],
                 [(_sds((L, W), BF16), _rs(tm, W)), (_sds((1, W), F32), _fs((1, W)))])


def _merge_fn(ma, mb, za, zb):
    return jax.nn.sigmoid(ma) * za + jax.nn.sigmoid(mb) * zb


def _mm_merge(name, q, w, proj, cb_a, cb_b, za):
    L, D = za.shape

    def epilogue(i, o, extra, outs):
        ma_ref, mb_ref, za_ref = extra
        for r0 in range(0, tm, sub):
            rs = slice(r0, r0 + sub)
            zb16 = o[rs, :].astype(BF16)
            outs[0][rs, :] = zb16
            outs[1][rs, :] = _merge_fn(ma_ref[rs, :].astype(F32), mb_ref[rs, :].astype(F32), za_ref[rs, :].astype(F32),
                                       zb16.astype(F32)).astype(BF16)

    tm = _tile(L, 512, 16)
    sub = _tile(tm, LANES, 16)
    row = pl.BlockSpec((tm, D), lambda i, k: (i, 0))
    col = lambda cb: pl.BlockSpec((tm, D), lambda i, k: (i, cb))
    return _mm_rows(name, q, w, [(proj, col(cb_a)), (proj, col(cb_b)), (za, row)],
                    [(_sds((L, D), BF16), row), (_sds((L, D), BF16), row)], epilogue, tm=tm, tk=q.shape[1])


def _mm_merge_bwd(name, dh, w, proj, cb_a, cb_b, za, zb):
    L, D = za.shape

    def epilogue(i, o, extra, outs):
        ma_ref, mb_ref, za_ref, zb_ref = extra
        for r0 in range(0, tm, sub):
            rs = slice(r0, r0 + sub)
            _, vjp = jax.vjp(_merge_fn, ma_ref[rs, :].astype(F32), mb_ref[rs, :].astype(F32), za_ref[rs, :].astype(F32),
                             zb_ref[rs, :].astype(F32))
            dma, dmb, dza, dzb = vjp(o[rs, :].astype(BF16).astype(F32))
            for ref, val in zip(outs, (dza, dzb, dma, dmb)):
                ref[rs, :] = val.astype(BF16)

    tm = _tile(L, 256, 16)
    sub = _tile(tm, LANES, 16)
    row = pl.BlockSpec((tm, D), lambda i, k: (i, 0))
    col = lambda cb: pl.BlockSpec((tm, D), lambda i, k: (i, cb))
    return _mm_rows(name, dh, w, [(proj, col(cb_a)), (proj, col(cb_b)), (za, row), (zb, row)],
                    [(_sds((L, D), BF16), row)] * 4, epilogue, tm=tm, tk=dh.shape[1], nt=True)


def _shift_down(x, k):
    row = lax.broadcasted_iota(jnp.int32, x.shape, 0)
    return jnp.where(row >= k, pltpu.roll(x, k, axis=0), 0.0)


def _shift_up(x, k):
    n = x.shape[0]
    row = lax.broadcasted_iota(jnp.int32, x.shape, 0)
    return jnp.where(row < n - k, pltpu.roll(x, n - k, axis=0), 0.0)


def _conv3(cv, w_ref, b_ref):
    return (w_ref[2:3, :] * cv + w_ref[1:2, :] * _shift_down(cv, 1) + w_ref[0:1, :] * _shift_down(cv, 2)
            + b_ref[...])


def _conv3_bwd(dcc, cv, w_ref):
    dcv = w_ref[2:3, :] * dcc + w_ref[1:2, :] * _shift_up(dcc, 1) + w_ref[0:1, :] * _shift_up(dcc, 2)
    dw = [jnp.sum(dcc * _shift_down(cv, 2), axis=0, keepdims=True),
          jnp.sum(dcc * _shift_down(cv, 1), axis=0, keepdims=True),
          jnp.sum(dcc * cv, axis=0, keepdims=True)]
    db = jnp.sum(dcc, axis=0, keepdims=True)
    return dcv, dw, db


def _store_rows(ref, rows):
    for r, val in enumerate(rows):
        ref[r:r + 1, :] = val


def _cols(name, body, ncb, ins, outs):
    return pl.pallas_call(
        body, name=name, grid=(ncb,),
        in_specs=[s for _, s in ins], out_specs=[s for _, s in outs],
        out_shape=[o for o, _ in outs], compiler_params=_params(1),
    )(*[a for a, _ in ins])


def _cb(L, w, off=0):
    return pl.BlockSpec((L, w), lambda j: (0, j + off))


def _convb_fwd(name, proj, cb_v, cb_gb, cb_gc, w, b):
    L = proj.shape[0]
    W = w.shape[1]
    c = LANES

    def body(v_ref, gb_ref, gc_ref, w_ref, b_ref, q_ref):
        cc = _conv3(gc_ref[...].astype(F32) * v_ref[...].astype(F32), w_ref, b_ref)
        q_ref[...] = (gb_ref[...].astype(F32) * cc).astype(BF16)

    return _cols(name, body, W // c,
                 [(proj, _cb(L, c, cb_v)), (proj, _cb(L, c, cb_gb)), (proj, _cb(L, c, cb_gc)),
                  (w, _cb(3, c)), (b, _cb(1, c))],
                 [(_sds((L, W), BF16), _cb(L, c))])[0]


def _convb_bwd(name, proj, cb_v, cb_gb, cb_gc, w, b, dq):
    L = proj.shape[0]
    W = w.shape[1]
    c = LANES

    def body(v_ref, gb_ref, gc_ref, w_ref, b_ref, dq_ref, dv_ref, dgb_ref, dgc_ref, dw_ref, db_ref):
        v, gc = v_ref[...].astype(F32), gc_ref[...].astype(F32)
        cv = gc * v
        cc = _conv3(cv, w_ref, b_ref)
        dq = dq_ref[...].astype(F32)
        dgb_ref[...] = (dq * cc).astype(BF16)
        dcv, dw, db = _conv3_bwd(dq * gb_ref[...].astype(F32), cv, w_ref)
        dv_ref[...] = (dcv * gc).astype(BF16)
        dgc_ref[...] = (dcv * v).astype(BF16)
        _store_rows(dw_ref, dw)
        db_ref[...] = db

    return _cols(name, body, W // c,
                 [(proj, _cb(L, c, cb_v)), (proj, _cb(L, c, cb_gb)), (proj, _cb(L, c, cb_gc)),
                  (w, _cb(3, c)), (b, _cb(1, c)), (dq, _cb(L, c))],
                 [(_sds((L, W), BF16), _cb(L, c)), (_sds((L, W), BF16), _cb(L, c)), (_sds((L, W), BF16), _cb(L, c)),
                  (_sds((3, W), F32), _cb(3, c)), (_sds((1, W), F32), _cb(1, c))])


HALO = 16


def _ffn_up_act(name, hn, w_up, w, b):
    L, D = hn.shape
    Fw = w.shape[1]
    tm, tc = _tile(L, 1024, HALO), _tile(Fw, 512)
    ncb = Fw // tc

    def body(x_ref, wa_ref, wg_ref, w_ref, b_ref, hh_ref, f_ref, carry):
        i, j = pl.program_id(0), pl.program_id(1)
        a16 = jnp.dot(x_ref[...], wa_ref[...], preferred_element_type=F32).astype(BF16)
        g16 = jnp.dot(x_ref[...], wg_ref[...], preferred_element_type=F32).astype(BF16)
        hh_ref[0] = a16
        hh_ref[1] = g16
        for c0 in range(0, tc, LANES):
            cs = slice(c0, c0 + LANES)
            prev = jnp.where(i == 0, 0.0, carry[j, :, cs])
            x = jnp.concatenate([prev, a16[:, cs].astype(F32)], axis=0)
            n = x.shape[0]
            a = (w_ref[2:3, cs] * x + w_ref[1:2, cs] * pltpu.roll(x, 1, axis=0) + w_ref[0:1, cs] * pltpu.roll(x, 2, axis=0)
                 + b_ref[:, cs])[8:n]
            f_ref[:, cs] = (_gelu_and_grad(a)[0] * g16[:, cs].astype(F32)).astype(BF16)
            carry[j, :, cs] = x[n - 8:n]

    return pl.pallas_call(
        body, name=name, grid=(L // tm, ncb),
        in_specs=[pl.BlockSpec((tm, D), lambda i, j: (i, 0)), pl.BlockSpec((D, tc), lambda i, j: (0, j)),
                  pl.BlockSpec((D, tc), lambda i, j: (0, j + ncb)),
                  pl.BlockSpec((3, tc), lambda i, j: (0, j)), pl.BlockSpec((1, tc), lambda i, j: (0, j))],
        out_specs=[pl.BlockSpec((2, tm, tc), lambda i, j: (0, i, j)), pl.BlockSpec((tm, tc), lambda i, j: (i, j))],
        out_shape=[_sds((2, L, Fw), BF16), _sds((L, Fw), BF16)],
        scratch_shapes=[pltpu.VMEM((ncb, 8, tc), F32)], compiler_params=_params(2),
    )(hn, w_up, w_up, w, b)


def _ffn_down_bwd_act(name, dy, w_down, hh, w, b):
    L, D = dy.shape
    Fw = w.shape[1]
    tm, tc = _tile(L, 512, HALO), _tile(Fw, 512)
    ncb, nrt, rpt = Fw // tc, L // tm, tm // HALO
    dn = (((1,), (1,)), ((), ()))

    def body(dy_ref, wd_ref, a_ref, ap_ref, h2_ref, w_ref, b_ref, dhh_ref, dw_ref, db_ref, carry, acc):
        i, j = pl.program_id(0), pl.program_id(1)
        first_rows = i == nrt - 1
        d16 = lax.dot_general(dy_ref[...], wd_ref[...], dn, preferred_element_type=F32).astype(BF16)
        for c0 in range(0, tc, LANES):
            cs = slice(c0, c0 + LANES)
            h1 = jnp.concatenate([jnp.where(first_rows, 0.0, ap_ref[:, cs].astype(F32)), a_ref[:, cs].astype(F32)], axis=0)
            s1, s2 = pltpu.roll(h1, 1, axis=0), pltpu.roll(h1, 2, axis=0)
            n = h1.shape[0]
            a = (w_ref[2:3, cs] * h1 + w_ref[1:2, cs] * s1 + w_ref[0:1, cs] * s2 + b_ref[:, cs])[HALO:n]
            ga, dga = _gelu_and_grad(a)
            d = d16[:, cs].astype(F32)
            da = d * h2_ref[:, cs].astype(F32) * dga
            dae = jnp.concatenate([da, jnp.where(i == 0, 0.0, carry[j, :, cs])], axis=0)
            m = dae.shape[0]
            dh1 = w_ref[2:3, cs] * dae + w_ref[1:2, cs] * pltpu.roll(dae, m - 1, axis=0) + w_ref[0:1, cs] * pltpu.roll(dae, m - 2, axis=0)
            dhh_ref[0, :, cs] = dh1[0:tm].astype(BF16)
            dhh_ref[1, :, cs] = (d * ga).astype(BF16)
            carry[j, :, cs] = da[0:8]
            rows = [jnp.sum(da * s2[HALO:n], axis=0, keepdims=True), jnp.sum(da * s1[HALO:n], axis=0, keepdims=True),
                    jnp.sum(da * h1[HALO:n], axis=0, keepdims=True), jnp.sum(da, axis=0, keepdims=True)]
            for r in range(4):
                tot = jnp.where(i == 0, 0.0, acc[j, r:r + 1, cs]) + rows[r]
                acc[j, r:r + 1, cs] = tot
                if r < 3:
                    dw_ref[r:r + 1, cs] = tot
                else:
                    db_ref[:, cs] = tot

    rt = lambda i: nrt - 1 - i
    dhh, dw, db = pl.pallas_call(
        body, name=name, grid=(nrt, ncb),
        in_specs=[pl.BlockSpec((tm, D), lambda i, j: (rt(i), 0)), pl.BlockSpec((tc, D), lambda i, j: (j, 0)),
                  pl.BlockSpec((None, tm, tc), lambda i, j: (0, rt(i), j)),
                  pl.BlockSpec((None, HALO, tc), lambda i, j: (0, jnp.maximum(rt(i) * rpt - 1, 0), j)),
                  pl.BlockSpec((None, tm, tc), lambda i, j: (1, rt(i), j)),
                  pl.BlockSpec((3, tc), lambda i, j: (0, j)), pl.BlockSpec((1, tc), lambda i, j: (0, j))],
        out_specs=[pl.BlockSpec((2, tm, tc), lambda i, j: (0, rt(i), j)),
                   pl.BlockSpec((None, 3, tc), lambda i, j: (i, 0, j)), pl.BlockSpec((None, 1, tc), lambda i, j: (i, 0, j))],
        out_shape=[_sds((2, L, Fw), BF16), _sds((nrt, 3, Fw), F32), _sds((nrt, 1, Fw), F32)],
        scratch_shapes=[pltpu.VMEM((ncb, 8, tc), F32), pltpu.VMEM((ncb, 8, tc), F32)], compiler_params=_params(2),
    )(dy, w_down, hh, hh, hh, w, b)
    return dhh, dw[nrt - 1], db[nrt - 1]


def _prep_fn(ar, ai, ldt, brt, bit):
    dt = jnp.exp(ldt)
    mag = jnp.exp(dt * ar)
    are = mag * jnp.cos(dt * ai)
    aim = mag * jnp.sin(dt * ai)
    nr = are - 1.0
    ni = aim
    den = ar * ar + ai * ai
    fr = (nr * ar + ni * ai) / den
    fi = (ni * ar - nr * ai) / den
    return are, aim, fr * brt - fi * bit, fr * bit + fi * brt


def _prep_fwd(name, ar, ai, ldt, brt, bit):
    def body(ar_ref, ai_ref, l_ref, br_ref, bi_ref, o1, o2, o3, o4):
        o1[...], o2[...], o3[...], o4[...] = _prep_fn(ar_ref[...], ai_ref[...], l_ref[...], br_ref[...], bi_ref[...])

    return pl.pallas_call(body, name=name,
                          out_shape=[_sds(ar.shape, F32), _sds(ar.shape, F32), _sds(brt.shape, F32), _sds(brt.shape, F32)],
                          )(ar, ai, ldt, brt, bit)


def _prep_bwd(name, ar, ai, ldt, brt, bit, g1, g2, g3, g4):
    def body(ar_ref, ai_ref, l_ref, br_ref, bi_ref, g1_ref, g2_ref, g3_ref, g4_ref, o1, o2, o3, o4, o5):
        _, vjp = jax.vjp(_prep_fn, ar_ref[...], ai_ref[...], l_ref[...], br_ref[...], bi_ref[...])
        o1[...], o2[...], o3[...], o4[...], o5[...] = vjp((g1_ref[...], g2_ref[...], g3_ref[...], g4_ref[...]))

    return pl.pallas_call(body, name=name,
                          out_shape=[_sds(ar.shape, F32)] * 3 + [_sds(brt.shape, F32)] * 2,
                          )(ar, ai, ldt, brt, bit, g1, g2, g3, g4)


def _scan_steps(tc, pitch, ng, reverse, a_r, a_i, stage_b, stage_x, out_re, out_im, st_re, st_im, acc):
    def step(tt, carry):
        t = (tc - 1 - tt) if reverse else tt
        new, sums = [], []
        for g in range(ng):
            rows = pl.ds(g * 8 * pitch + t, 8, stride=pitch)
            cr, ci = carry[2 * g], carry[2 * g + 1]
            br, bi = stage_b[0][rows, :], stage_b[1][rows, :]
            if reverse:
                xr, xi = stage_x[0][rows, :], stage_x[1][rows, :]
                sums += [carry[2 * ng + 2 * g] + (xr * cr + xi * ci), carry[2 * ng + 2 * g + 1] + (xr * ci - xi * cr)]
                nr = a_r[g] * cr + a_i[g] * ci + br
                ni = a_r[g] * ci - a_i[g] * cr + bi
            else:
                nr = a_r[g] * cr - a_i[g] * ci + br
                ni = a_r[g] * ci + a_i[g] * cr + bi
            out_re[rows, :] = nr
            out_im[rows, :] = ni
            new += [nr, ni]
        return tuple(new + sums)

    init = []
    for g in range(ng):
        init += [st_re[g], st_im[g]]
    if reverse:
        for g in range(ng):
            init += [acc[0][g], acc[1][g]]
    fin = lax.fori_loop(0, tc, step, tuple(init), unroll=2)
    for g in range(ng):
        st_re[g] = fin[2 * g]
        st_im[g] = fin[2 * g + 1]
        if reverse:
            acc[0][g] = fin[2 * ng + 2 * g]
            acc[1][g] = fin[2 * ng + 2 * g + 1]


def _s5_fwd(name, proj, bm_re, bm_im, cm_re, cm_im, dskip, a_re, a_im):
    L = proj.shape[0]
    nb = bm_re.shape[0]
    ns, W = SLAB * nb, nb * LANES
    ng = ns // 8
    tc = min(2 * LANES, L)
    pitch = tc + 8
    wide = SLAB * LANES

    def body(u_ref, bre_ref, bim_ref, cre_ref, cim_ref, d_ref, ar_ref, ai_ref, xr_ref, xi_ref, y_ref, ya_ref,
             sb_re, sb_im, out_re, out_im, st_re, st_im):
        @pl.when(pl.program_id(0) == 0)
        def _():
            st_re[...] = jnp.zeros(st_re.shape, F32)
            st_im[...] = jnp.zeros(st_im.shape, F32)

        for j in range(nb):
            ub = u_ref[:, j * LANES:(j + 1) * LANES]
            r1 = jnp.dot(ub, bre_ref[j], preferred_element_type=F32)
            r2 = jnp.dot(ub, bim_ref[j], preferred_element_type=F32)
            for q in range(SLAB):
                sb_re[pl.ds((SLAB * j + q) * pitch, tc), :] = r1[:, q * LANES:(q + 1) * LANES]
                sb_im[pl.ds((SLAB * j + q) * pitch, tc), :] = r2[:, q * LANES:(q + 1) * LANES]
        a_r = [ar_ref[g] for g in range(ng)]
        a_i = [ai_ref[g] for g in range(ng)]
        _scan_steps(tc, pitch, ng, False, a_r, a_i, (sb_re, sb_im), None, out_re, out_im, st_re, st_im, None)
        for j in range(nb):
            x1 = [out_re[pl.ds((SLAB * j + q) * pitch, tc), :].astype(BF16) for q in range(SLAB)]
            x2 = [out_im[pl.ds((SLAB * j + q) * pitch, tc), :].astype(BF16) for q in range(SLAB)]
            for q in range(SLAB):
                xr_ref[SLAB * j + q] = x1[q]
                xi_ref[SLAB * j + q] = x2[q]
            cols = slice(j * LANES, (j + 1) * LANES)
            y = (jnp.dot(jnp.concatenate(x1, axis=1), cre_ref[j], preferred_element_type=F32)
                 + jnp.dot(jnp.concatenate(x2, axis=1), cim_ref[j], preferred_element_type=F32)
                 + d_ref[:, cols] * u_ref[:, cols].astype(F32))
            y_ref[:, cols] = y
            ya_ref[:, cols] = jax.nn.gelu(y).astype(BF16)

    full3 = lambda s: pl.BlockSpec(s, lambda i: (0, 0, 0))
    xs = pl.BlockSpec((ns, tc, LANES), lambda i: (0, i, 0))
    rows = pl.BlockSpec((tc, W), lambda i: (i, 0))
    return pl.pallas_call(
        body, name=name, grid=(L // tc,),
        in_specs=[rows, full3((nb, LANES, wide)), full3((nb, LANES, wide)), full3((nb, wide, LANES)),
                  full3((nb, wide, LANES)), pl.BlockSpec((1, W), lambda i: (0, 0)), full3((ng, 8, LANES)), full3((ng, 8, LANES))],
        out_specs=[xs, xs, rows, rows],
        out_shape=[_sds((ns, L, LANES), BF16)] * 2 + [_sds((L, W), F32), _sds((L, W), BF16)],
        scratch_shapes=[pltpu.VMEM((ns * pitch, LANES), F32)] * 4 + [pltpu.VMEM((ng, 8, LANES), F32)] * 2,
        compiler_params=_params(1),
    )(proj, bm_re, bm_im, cm_re, cm_im, dskip, a_re, a_im)


def _s5_bwd(name, dyb, proj, xs_re, xs_im, cmt_re, cmt_im, bmt_re, bmt_im, dskip, a_re, a_im):
    L = dyb.shape[0]
    nb = cmt_re.shape[0]
    ns, W = SLAB * nb, nb * LANES
    ng = ns // 8
    tc = min(LANES, L)
    pitch = tc + 8
    nt = L // tc
    wide = SLAB * LANES
    dn = (((0,), (0,)), ((), ()))

    def body(dy_ref, u_ref, xr_ref, xi_ref, cre_ref, cim_ref, bre_ref, bim_ref, d_ref, ar_ref, ai_ref,
             du_ref, gbr_ref, gbi_ref, gcr_ref, gci_ref, dar_ref, dai_ref,
             sd_re, sd_im, sx_re, sx_im, out_re, out_im, st_re, st_im, acc_re, acc_im):
        first = pl.program_id(0) == 0

        @pl.when(first)
        def _():
            for r in (st_re, st_im, acc_re, acc_im):
                r[...] = jnp.zeros(r.shape, F32)
            for r in (gbr_ref, gbi_ref, gcr_ref, gci_ref):
                r[...] = jnp.zeros(r.shape, F32)

        for j in range(nb):
            dyj = dy_ref[:, j * LANES:(j + 1) * LANES]
            r1 = jnp.dot(dyj, cre_ref[j], preferred_element_type=F32)
            r2 = jnp.dot(dyj, cim_ref[j], preferred_element_type=F32)
            for q in range(SLAB):
                s = SLAB * j + q
                sd_re[pl.ds(s * pitch, tc), :] = r1[:, q * LANES:(q + 1) * LANES]
                sd_im[pl.ds(s * pitch, tc), :] = r2[:, q * LANES:(q + 1) * LANES]
                sx_re[pl.ds(s * pitch, tc), :] = xr_ref[s].astype(F32)
                sx_im[pl.ds(s * pitch, tc), :] = xi_ref[s].astype(F32)
        a_r = [ar_ref[g] for g in range(ng)]
        a_i = [ai_ref[g] for g in range(ng)]
        _scan_steps(tc, pitch, ng, True, a_r, a_i, (sd_re, sd_im), (sx_re, sx_im), out_re, out_im, st_re, st_im,
                    (acc_re, acc_im))
        for j in range(nb):
            cols = slice(j * LANES, (j + 1) * LANES)
            l1 = jnp.concatenate([out_re[pl.ds((SLAB * j + q) * pitch, tc), :] for q in range(SLAB)], axis=1).astype(BF16)
            l2 = jnp.concatenate([out_im[pl.ds((SLAB * j + q) * pitch, tc), :] for q in range(SLAB)], axis=1).astype(BF16)
            dyj = dy_ref[:, cols]
            du = (jnp.dot(l1, bre_ref[j], preferred_element_type=F32) + jnp.dot(l2, bim_ref[j], preferred_element_type=F32)
                  + d_ref[:, cols] * dyj.astype(F32))
            du_ref[:, cols] = du.astype(BF16)
            uj = u_ref[:, cols]
            gbr_ref[j] += lax.dot_general(uj, l1, dn, preferred_element_type=F32)
            gbi_ref[j] += lax.dot_general(uj, l2, dn, preferred_element_type=F32)
            x1 = jnp.concatenate([xr_ref[SLAB * j + q] for q in range(SLAB)], axis=1)
            x2 = jnp.concatenate([xi_ref[SLAB * j + q] for q in range(SLAB)], axis=1)
            gcr_ref[j] += lax.dot_general(dyj, x1, dn, preferred_element_type=F32)
            gci_ref[j] += lax.dot_general(dyj, x2, dn, preferred_element_type=F32)
        dar_ref[...] = acc_re[...]
        dai_ref[...] = acc_im[...]

    full3 = lambda s: pl.BlockSpec(s, lambda i: (0, 0, 0))
    xs = pl.BlockSpec((ns, tc, LANES), lambda i: (0, nt - 1 - i, 0))
    rows = pl.BlockSpec((tc, W), lambda i: (nt - 1 - i, 0))
    mat_a, mat_b = full3((nb, LANES, wide)), full3((nb, wide, LANES))
    vec = full3((ng, 8, LANES))
    return pl.pallas_call(
        body, name=name, grid=(nt,),
        in_specs=[rows, rows, xs, xs, mat_a, mat_a, mat_b, mat_b, pl.BlockSpec((1, W), lambda i: (0, 0)), vec, vec],
        out_specs=[rows, mat_a, mat_a, mat_a, mat_a, vec, vec],
        out_shape=[_sds((L, W), BF16)] + [_sds((nb, LANES, wide), F32)] * 4 + [_sds((ng, 8, LANES), F32)] * 2,
        scratch_shapes=[pltpu.VMEM((ns * pitch, LANES), F32)] * 6 + [pltpu.VMEM((ng, 8, LANES), F32)] * 4,
        compiler_params=_params(1),
    )(dyb, proj, xs_re, xs_im, cmt_re, cmt_im, bmt_re, bmt_im, dskip, a_re, a_im)


def _peer(k):
    x, y, c = lax.axis_index("x"), lax.axis_index("y"), lax.axis_index("c")
    px = 1 - x if (k >> 2) & 1 else x
    py = 1 - y if (k >> 1) & 1 else y
    pc = 1 - c if k & 1 else c
    return (px, py, pc), 4 * px + 2 * py + pc


def _window(ref, kind, idx, n):
    if kind == "col":
        w = ref.shape[1] // n
        return ref.at[:, pl.ds(pl.multiple_of(idx * w, LANES), w)]
    r = ref.shape[0] // n
    return ref.at[pl.ds(pl.multiple_of(idx * r, 8), r), :]


def _all_gather(name, shards, kinds):
    n = len(shards)
    fulls = []
    for s, kind in zip(shards, kinds):
        fulls.append(_sds((s.shape[0], s.shape[1] * N_DEV) if kind == "col" else (s.shape[0] * N_DEV, s.shape[1]), s.dtype))

    def body(*refs):
        src, dst = refs[:n], refs[n:2 * n]
        send, recv, loc = refs[2 * n:]
        me = 4 * lax.axis_index("x") + 2 * lax.axis_index("y") + lax.axis_index("c")
        copies = []
        for a in range(n):
            own = pltpu.make_async_copy(src[a], _window(dst[a], kinds[a], me, N_DEV), loc.at[a])
            own.start()
            copies.append(own)
        sends = []
        for k in range(1, N_DEV):
            dev, _ = _peer(k)
            for a in range(n):
                cp = pltpu.make_async_remote_copy(
                    src_ref=src[a], dst_ref=_window(dst[a], kinds[a], me, N_DEV),
                    send_sem=send.at[a * N_DEV + k], recv_sem=recv.at[a * N_DEV + k],
                    device_id=dev, device_id_type=MESH)
                cp.start()
                sends.append(cp)
        for k in range(1, N_DEV):
            dev, pidx = _peer(k)
            for a in range(n):
                pltpu.make_async_remote_copy(
                    src_ref=src[a], dst_ref=_window(dst[a], kinds[a], pidx, N_DEV),
                    send_sem=send.at[a * N_DEV + k], recv_sem=recv.at[a * N_DEV + k],
                    device_id=dev, device_id_type=MESH).wait_recv()
        for cp in sends:
            cp.wait_send()
        for cp in copies:
            cp.wait()

    any_ = pl.BlockSpec(memory_space=pl.ANY)
    return pl.pallas_call(
        body, name=name, in_specs=[any_] * n, out_specs=[any_] * n, out_shape=fulls,
        scratch_shapes=[pltpu.SemaphoreType.DMA((n * N_DEV,)), pltpu.SemaphoreType.DMA((n * N_DEV,)),
                        pltpu.SemaphoreType.DMA((n,))],
        compiler_params=pltpu.CompilerParams(has_side_effects=True),
    )(*shards)


def _xfer_refs(mode, kinds, a, src, dst, me, pidx):
    if mode == "gather":
        return src[a], _window(dst[a], kinds[a], me, N_DEV), _window(dst[a], kinds[a], pidx, N_DEV)
    return _window(src[a], kinds[a], pidx, N_DEV), dst[a].at[me], dst[a].at[pidx]


def _xfer_out_shapes(mode, arrs, kinds):
    outs = []
    for s, kind in zip(arrs, kinds):
        if mode == "gather":
            outs.append((s.shape[0], s.shape[1] * N_DEV) if kind == "col" else (s.shape[0] * N_DEV, s.shape[1]))
        else:
            outs.append((N_DEV,) + ((s.shape[0], s.shape[1] // N_DEV) if kind == "col" else (s.shape[0] // N_DEV, s.shape[1])))
    return outs


def _sc_xfer(name, mode, arrs, kinds, collective_id):
    n = len(arrs)
    shapes = _xfer_out_shapes(mode, arrs, kinds)
    hbm = pltpu.MemorySpace.HBM
    src = [jax.new_ref(a, memory_space=hbm) for a in arrs]
    dst = [jax.empty_ref(_sds(shp, a.dtype), memory_space=hbm) for shp, a in zip(shapes, arrs)]

    @pl.kernel(mesh=plsc.ScalarSubcoreMesh(axis_name="seq", num_cores=1), name=name,
               scratch_types=(pltpu.SemaphoreType.DMA((n * N_DEV,)), pltpu.SemaphoreType.DMA((n * N_DEV,)),
                              pltpu.SemaphoreType.DMA((n,))),
               compiler_params=pltpu.CompilerParams(collective_id=collective_id))
    def launch(send, recv, loc):
        barrier = pltpu.get_barrier_semaphore()
        for k in range(1, N_DEV):
            pl.semaphore_signal(barrier, inc=1, device_id=_peer(k)[0], device_id_type=MESH)
        pl.semaphore_wait(barrier, N_DEV - 1)
        me = 4 * lax.axis_index("x") + 2 * lax.axis_index("y") + lax.axis_index("c")
        own, sends = [], []
        for a in range(n):
            s, _, d = _xfer_refs(mode, kinds, a, src, dst, me, me)
            own.append(pltpu.make_async_copy(s, d, loc.at[a]))
            own[-1].start()
        for k in range(1, N_DEV):
            dev, pidx = _peer(k)
            for a in range(n):
                s, d, _ = _xfer_refs(mode, kinds, a, src, dst, me, pidx)
                sends.append(pltpu.make_async_remote_copy(src_ref=s, dst_ref=d, send_sem=send.at[a * N_DEV + k],
                                                          recv_sem=recv.at[a * N_DEV + k], device_id=dev, device_id_type=MESH))
                sends[-1].start()
        for cp in own:
            cp.wait()
        for k in range(1, N_DEV):
            dev, pidx = _peer(k)
            for a in range(n):
                s, _, land = _xfer_refs(mode, kinds, a, src, dst, me, pidx)
                pltpu.make_async_remote_copy(src_ref=s, dst_ref=land, send_sem=send.at[a * N_DEV + k],
                                             recv_sem=recv.at[a * N_DEV + k], device_id=dev, device_id_type=MESH).wait_recv()
        for cp in sends:
            cp.wait_send()

    launch()
    return [d[...] for d in dst]


def _sc_gather(name, arrs, kinds, collective_id):
    n = len(arrs)
    pairs = 7
    shapes = _xfer_out_shapes("gather", arrs, kinds)
    hbm = pltpu.MemorySpace.HBM
    src = [jax.new_ref(a, memory_space=hbm) for a in arrs]
    dst = [jax.empty_ref(_sds(shp, a.dtype), memory_space=hbm) for shp, a in zip(shapes, arrs)]

    @pl.kernel(mesh=plsc.ScalarSubcoreMesh(axis_name="seq", num_cores=1), name=name,
               scratch_types=(pltpu.SemaphoreType.DMA((n * pairs,)), pltpu.SemaphoreType.DMA((n * pairs,)),
                              pltpu.SemaphoreType.DMA((n,))),
               compiler_params=pltpu.CompilerParams(collective_id=collective_id))
    def launch(send, recv, loc):
        x, y, c = lax.axis_index("x"), lax.axis_index("y"), lax.axis_index("c")
        me = 4 * x + 2 * y + c
        sib = (x, y, 1 - c)
        chips = []
        for fx, fy in ((1, 0), (0, 1), (1, 1)):
            px, py = (1 - x if fx else x), (1 - y if fy else y)
            chips.append(((px, py, c), 4 * px + 2 * py + c, 4 * px + 2 * py + (1 - c)))
        barrier = pltpu.get_barrier_semaphore()
        for dev in [sib] + [ch[0] for ch in chips]:
            pl.semaphore_signal(barrier, inc=1, device_id=dev, device_id_type=MESH)
        pl.semaphore_wait(barrier, 4)

        def win(a, idx):
            return _window(dst[a], kinds[a], idx, N_DEV)

        def rcopy(a, p, s, d, dev):
            return pltpu.make_async_remote_copy(src_ref=s, dst_ref=d, send_sem=send.at[a * pairs + p],
                                                recv_sem=recv.at[a * pairs + p], device_id=dev, device_id_type=MESH)

        own, sends = [], []
        for a in range(n):
            own.append(pltpu.make_async_copy(src[a], win(a, me), loc.at[a]))
            own[-1].start()
        for j, (dev, _, _) in enumerate(chips):
            for a in range(n):
                sends.append(rcopy(a, 1 + j, src[a], win(a, me), dev))
                sends[-1].start()
        for a in range(n):
            sends.append(rcopy(a, 0, src[a], win(a, me), sib))
            sends[-1].start()
        for j, (dev, idx, _) in enumerate(chips):
            for a in range(n):
                rcopy(a, 1 + j, src[a], win(a, idx), dev).wait_recv()
                sends.append(rcopy(a, 4 + j, win(a, idx), win(a, idx), sib))
                sends[-1].start()
        for cp in own:
            cp.wait()
        for a in range(n):
            rcopy(a, 0, src[a], win(a, 4 * x + 2 * y + (1 - c)), sib).wait_recv()
        for j, (_, _, sidx) in enumerate(chips):
            for a in range(n):
                rcopy(a, 4 + j, src[a], win(a, sidx), sib).wait_recv()
        for cp in sends:
            cp.wait_send()

    launch()
    return [d[...] for d in dst]


_SEQ_IDS = {"gather_in": 7, "gather_mix": 1, "gather_up": 2, "gather_down": 9, "grads_down": 3, "grads_up": 8,
            "grads_mix": 4, "grads_small": 5, "grads_in": 6}


def _launch(name, mode, arrs, kinds):
    if mode == "gather":
        return _sc_gather(name, list(arrs), kinds, _SEQ_IDS[name])
    return _sc_xfer(name, mode, list(arrs), kinds, _SEQ_IDS[name])


def _adamw(name, parts, w, m, v):
    P, R, C = parts.shape
    sub = 16 if parts.dtype == BF16 else 8
    tr = R if R * C <= (1 << 18) else _tile(R, max(sub, (1 << 18) // C), sub)

    def body(p_ref, w_ref, m_ref, v_ref, g_ref, d_ref, nm_ref, nv_ref):
        g = p_ref[0].astype(F32)
        for s in range(1, P):
            g = g + p_ref[s].astype(F32)
        m2 = ADAM_B1 * m_ref[...] + (1.0 - ADAM_B1) * g
        v2 = ADAM_B2 * v_ref[...] + (1.0 - ADAM_B2) * (g * g)
        m_hat = m2 / (1.0 - ADAM_B1 ** ADAM_STEP)
        v_hat = v2 / (1.0 - ADAM_B2 ** ADAM_STEP)
        g_ref[...] = g
        d_ref[...] = -ADAM_LR * (m_hat / (jnp.sqrt(v_hat) + ADAM_EPS) + ADAM_WD * w_ref[...])
        nm_ref[...] = m2
        nv_ref[...] = v2

    sp = pl.BlockSpec((tr, C), lambda i: (i, 0))
    return pl.pallas_call(
        body, name=name, grid=(R // tr,),
        in_specs=[pl.BlockSpec((P, tr, C), lambda i: (0, i, 0)), sp, sp, sp], out_specs=[sp] * 4,
        out_shape=[_sds((R, C), F32)] * 4, compiler_params=_params(1),
    )(parts, w, m, v)


def _pack(arrs, row_mult=8):
    pieces, total = [], 0
    for a in arrs:
        f = a.reshape(-1).astype(F32)
        pad = (-f.shape[0]) % (8 * LANES)
        pieces.append(jnp.pad(f, (0, pad)) if pad else f)
        total += f.shape[0] + pad
    tail = (-total) % (row_mult * LANES)
    if tail:
        pieces.append(jnp.zeros((tail,), F32))
    return jnp.concatenate(pieces).reshape(-1, LANES)


def _unpack(buf, shapes, lead=()):
    out, row = [], 0
    for shp in shapes:
        size = 1
        for d in shp:
            size *= d
        rows = -(-size // (8 * LANES)) * 8
        piece = buf[..., row:row + rows, :].reshape(lead + (rows * LANES,))[..., :size]
        out.append(piece.reshape(lead + tuple(shp)))
        row += rows
    return out


def kernel(x, norm_tok, w_in, a_re, a_im, log_dt, b_re, b_im, c_re, c_im, d_skip, w_glu, w_ssm_out, conv_w, conv_b, w_conv_out, w_o, norm_ffn, w_up, ffn_conv_w, ffn_conv_b, w_down, norm_final, loss_target, m_norm_tok, m_w_in, m_a_re, m_a_im, m_log_dt, m_b_re, m_b_im, m_c_re, m_c_im, m_d_skip, m_w_glu, m_w_ssm_out, m_conv_w, m_conv_b, m_w_conv_out, m_w_o, m_norm_ffn, m_w_up, m_ffn_conv_w, m_ffn_conv_b, m_w_down, m_norm_final, v_norm_tok, v_w_in, v_a_re, v_a_im, v_log_dt, v_b_re, v_b_im, v_c_re, v_c_im, v_d_skip, v_w_glu, v_w_ssm_out, v_conv_w, v_conv_b, v_w_conv_out, v_w_o, v_norm_ffn, v_w_up, v_ffn_conv_w, v_ffn_conv_b, v_w_down, v_norm_final):
    args = dict(locals())
    L, D = x.shape[1], x.shape[2]
    G, P, H = b_re.shape[1], b_re.shape[2], b_re.shape[3]
    SW = G * H
    CW = conv_b.shape[1]
    FF = ffn_conv_b.shape[1]
    GP = G * P
    nb = SW // LANES
    gpb = LANES // H
    me = 4 * lax.axis_index("x") + 2 * lax.axis_index("y") + lax.axis_index("c")
    tm = _tile(L, 256, 16)
    x2 = x[0]
    tgt = loss_target[0]

    big = [("w_in", "col"), ("w_glu", "row"), ("w_ssm_out", "col"), ("w_conv_out", "col"), ("w_o", "row"),
           ("w_up", "col"), ("w_down", "row")]
    shards = [_cast_bf16("cast_" + n, args[n][0]) for n, _ in big]
    small_in = _pack([conv_w[0], ffn_conv_w[0]])
    kind = dict(big)
    mixw, ffnw = ["w_glu", "w_ssm_out", "w_conv_out", "w_o"], ["w_up", "w_down"]
    shard = dict(zip([n for n, _ in big], shards))
    gathered = _launch("gather_in", "gather", [shard["w_in"], small_in], ["col", "row"])
    W = {"w_in": gathered[0]}
    W.update(zip(mixw, _launch("gather_mix", "gather", [shard[n] for n in mixw], [kind[n] for n in mixw])))
    for n in ffnw:
        W[n] = _launch("gather_" + n[2:], "gather", [shard[n]], [kind[n]])[0]
    cw_parts, fcw_parts = _unpack(gathered[-1].reshape(N_DEV, -1, LANES), [conv_w.shape[1:], ffn_conv_w.shape[1:]], (N_DEV,))
    conv_w_full = jnp.moveaxis(cw_parts, 0, 1).reshape(3, CW)
    ffn_conv_w_full = jnp.moveaxis(fcw_parts, 0, 1).reshape(3, FF)

    ar_row, ai_row = a_re.reshape(1, GP), a_im.reshape(1, GP)
    ldt_row = jnp.broadcast_to(log_dt.reshape(G, 1), (G, P)).reshape(1, GP)
    brt = jnp.transpose(b_re[0], (2, 0, 1)).reshape(H, GP)
    bit = jnp.transpose(b_im[0], (2, 0, 1)).reshape(H, GP)
    abar_re, abar_im, bbar_re, bbar_im = _prep_fwd("s5_prep", ar_row, ai_row, ldt_row, brt, bit)
    eye = jnp.eye(gpb, dtype=F32)

    def b_blocks(bt):
        return jnp.einsum("ab,hjbp->jahbp", eye, bt.reshape(H, nb, gpb, P)).reshape(nb, LANES, gpb * P)

    def c_blocks(c):
        return jnp.einsum("ab,jahp->jbpah", eye, c.reshape(nb, gpb, H, P)).reshape(nb, gpb * P, LANES)

    def diag_blocks(mat):
        return jnp.einsum("jahap->hjap", mat.reshape(nb, gpb, H, gpb, P))

    bm_re, bm_im = b_blocks(bbar_re), b_blocks(bbar_im)
    cm_re, cm_im = c_blocks(c_re[0]), -c_blocks(c_im[0])
    a3_re, a3_im = abar_re.reshape(-1, 8, LANES), abar_im.reshape(-1, 8, LANES)
    dskip_row = d_skip.reshape(1, SW)

    cbs = SW // LANES
    cb_v, cb_gb, cb_gc = cbs, cbs + CW // LANES, cbs + 2 * CW // LANES
    cb_ma = (SW + 3 * CW) // D
    xn = _rms_fwd("rms_tok", x2, norm_tok, tm)
    proj = _mm("proj", xn, W["w_in"], "nn", out_dtype=BF16)
    xs_re, xs_im, y, ya = _s5_fwd("s5_fwd", proj, bm_re.astype(BF16), bm_im.astype(BF16), cm_re.astype(BF16),
                                  cm_im.astype(BF16), dskip_row, a3_re, a3_im)
    g1, ya2 = _mm_glu("glu_gate", ya, W["w_glu"], y)
    za = _mm("ssm_out", ya2, W["w_ssm_out"], "nn", out_dtype=BF16)
    q = _convb_fwd("convb", proj, cb_v, cb_gb, cb_gc, conv_w_full, conv_b)
    zb, merged = _mm_merge("conv_out_merge", q, W["w_conv_out"], proj, cb_ma, cb_ma + 1, za)
    h1, hn = _mm_res_rms("mix_out_rms", merged, W["w_o"], x2, norm_ffn)
    hh, f = _ffn_up_act("ffn_up_act", hn, W["w_up"], ffn_conv_w_full, ffn_conv_b)
    o2 = _mm("ffn_down", f, W["w_down"], "nn", tk=2816)
    dh2, dh2b, g_norm_final, loss_part = _final("final", h1, o2, norm_final.reshape(1, D), tgt, tm)

    gw_down = _mm("gw_down", f, dh2b, "tn", out_dtype=BF16, tm=1408, tn=512, tk=L)
    dh2b, gw_down = lax.optimization_barrier((dh2b, gw_down))
    parts = {"w_down": _launch("grads_down", "exchange", [gw_down], [kind["w_down"]])[0]}
    dhh, g_ffn_conv_w, g_ffn_conv_b = _ffn_down_bwd_act("ffn_down_bwd_act", dh2b, W["w_down"], hh, ffn_conv_w_full,
                                                         ffn_conv_b)
    nhalf = lambda t: FF // t
    gw_up = _mm("gw_up", hn, dhh, "tn", out_dtype=BF16, tn=_tile(FF, 1024), tk=L, dims=(D, 2 * FF, L),
                b_spec=lambda a, b, c: pl.BlockSpec((None, c, b), lambda i, j, k: (j // nhalf(b), k, j % nhalf(b))))
    dhh, gw_up = lax.optimization_barrier((dhh, gw_up))
    parts["w_up"] = _launch("grads_up", "exchange", [gw_up], [kind["w_up"]])[0]
    dhn = _mm("d_ffn_in", dhh, W["w_up"], "nt", out_dtype=BF16, tk=_tile(FF, 2816), dims=(L, D, 2 * FF),
              a_spec=lambda a, b, c: pl.BlockSpec((None, a, c), lambda i, j, k: (k // nhalf(c), i, k % nhalf(c))))
    dh1, dh1b, g_norm_ffn = _rms_bwd("rms_ffn_bwd", dhn, h1, norm_ffn, dh2, tm, True)

    gw_o = _mm("gw_o", merged, dh1b, "tn", out_dtype=BF16, tk=L)
    dh1b, gw_o = lax.optimization_barrier((dh1b, gw_o))
    dza, dzb, dma, dmb = _mm_merge_bwd("d_merged_bwd", dh1b, W["w_o"], proj, cb_ma, cb_ma + 1, za, zb)
    dq = _mm("d_q", dzb, W["w_conv_out"], "nt", out_dtype=BF16)
    gw_conv_out = _mm("gw_conv_out", q, dzb, "tn", out_dtype=BF16, tk=L)
    dq, gw_conv_out = lax.optimization_barrier((dq, gw_conv_out))
    dv, dgb, dgc, g_conv_w, g_conv_b = _convb_bwd("convb_bwd", proj, cb_v, cb_gb, cb_gc, conv_w_full, conv_b, dq)
    dya2 = _mm("d_ya2", dza, W["w_ssm_out"], "nt", out_dtype=BF16)
    gw_ssm_out = _mm("gw_ssm_out", ya2, dza, "tn", out_dtype=BF16, tk=L)
    dya2, gw_ssm_out = lax.optimization_barrier((dya2, gw_ssm_out))
    dy_direct, dg1 = _glu_bwd("glu_bwd", y, g1, dya2, tm)
    dya_g = _mm("d_ya_gate", dg1, W["w_glu"], "nt", out_dtype=BF16)
    gw_glu = _mm("gw_glu", ya, dg1, "tn", out_dtype=BF16, tk=L)
    dya_g, gw_glu = lax.optimization_barrier((dya_g, gw_glu))
    parts_mix = _launch("grads_mix", "exchange", [gw_glu, gw_ssm_out, gw_conv_out, gw_o], [kind[n] for n in mixw])
    dyb, g_dskip = _gelu_bwd("gelu_bwd", y, dy_direct, dya_g, proj, tm)
    swap = lambda m: jnp.swapaxes(m, 1, 2).astype(BF16)
    du, gb_re, gb_im, gc_re, gc_im, dab_re, dab_im = _s5_bwd(
        "s5_bwd", dyb, proj, xs_re, xs_im, swap(cm_re), swap(cm_im), swap(bm_re), swap(bm_im), dskip_row, a3_re, a3_im)
    g_ar, g_ai, g_ldt, g_brt, g_bit = _prep_bwd(
        "s5_prep_bwd", ar_row, ai_row, ldt_row, brt, bit, dab_re.reshape(1, GP), dab_im.reshape(1, GP),
        diag_blocks(gb_re).reshape(H, GP), diag_blocks(gb_im).reshape(H, GP))
    small = dict(
        a_re=g_ar.reshape(1, G, P), a_im=g_ai.reshape(1, G, P),
        log_dt=g_ldt.reshape(G, P).sum(axis=1).reshape(1, G),
        b_re=jnp.transpose(g_brt.reshape(H, G, P), (1, 2, 0))[None], b_im=jnp.transpose(g_bit.reshape(H, G, P), (1, 2, 0))[None],
        c_re=jnp.transpose(diag_blocks(gc_re), (1, 2, 0, 3)).reshape(1, G, H, P),
        c_im=-jnp.transpose(diag_blocks(gc_im), (1, 2, 0, 3)).reshape(1, G, H, P),
        d_skip=g_dskip.reshape(1, G, H), conv_b=g_conv_b, norm_ffn=g_norm_ffn, ffn_conv_b=g_ffn_conv_b,
        norm_final=g_norm_final.reshape(D), conv_w=g_conv_w[None], ffn_conv_w=g_ffn_conv_w[None])
    rep = ["a_re", "a_im", "log_dt", "b_re", "b_im", "c_re", "c_im", "d_skip", "conv_b", "norm_ffn", "ffn_conv_b", "norm_final"]
    order = rep + ["conv_w", "ffn_conv_w"]
    full_shapes = {n: args[n].shape for n in rep}
    full_shapes["conv_w"], full_shapes["ffn_conv_w"] = (1, 3, CW), (1, 3, FF)
    rep_pack = _pack([small[n] for n in rep], LANES)
    rep_rows = rep_pack.shape[0]
    gpack = jnp.concatenate([loss_part, rep_pack, _pack([small["conv_w"], small["ffn_conv_w"]])], axis=0)
    rep0 = loss_part.shape[0]
    rows = gpack.shape[0]
    du, gpack = lax.optimization_barrier((du, gpack))
    gall = _launch("grads_small", "gather", [gpack], ["row"])[0]

    dproj = _concat_cols("dproj", [du, dv, dgb, dgc, dma, dmb], tm)
    gw_in = _mm("gw_in", xn, dproj, "tn", out_dtype=BF16, tk=L)
    dproj, gw_in = lax.optimization_barrier((dproj, gw_in))
    parts_in = _launch("grads_in", "exchange", [gw_in], ["col"])
    dxn = _mm("d_xn", dproj, W["w_in"], "nt", out_dtype=BF16, tk=4096)
    grad_x, g_norm_tok = _rms_bwd("rms_tok_bwd", dxn, x2, norm_tok, dh1, tm, False)

    res = {}

    def big_update(n):
        res[n] = [r[None] for r in _adamw("adamw_" + n, parts[n], args[n][0], args["m_" + n][0], args["v_" + n][0])]

    def after(xs, dep):
        return lax.optimization_barrier((list(xs), dep))[0]

    parts["w_down"] = after([parts["w_down"]], grad_x)[0]
    big_update("w_down")
    parts["w_up"] = after([parts["w_up"]], res["w_down"][1])[0]
    big_update("w_up")
    parts.update(zip(mixw, after(parts_mix, [res[n][1] for n in ffnw])))
    for n in mixw:
        big_update(n)
    gall = after([gall], [res[n][1] for n in mixw])[0].reshape(N_DEV, rows, LANES)
    gcw, gfcw = _unpack(gall[:, rep0 + rep_rows:], [full_shapes["conv_w"], full_shapes["ffn_conv_w"]], (N_DEV,))
    cws, fcws = CW // N_DEV, FF // N_DEV
    gcw = lax.dynamic_slice_in_dim(gcw[:, 0], me * cws, cws, axis=2)
    gfcw = lax.dynamic_slice_in_dim(gfcw[:, 0], me * fcws, fcws, axis=2)
    res["conv_w"] = [r[None] for r in _adamw("adamw_conv_w", gcw, conv_w[0], m_conv_w[0], v_conv_w[0])]
    res["ffn_conv_w"] = [r[None] for r in _adamw("adamw_ffn_conv_w", gfcw, ffn_conv_w[0], m_ffn_conv_w[0], v_ffn_conv_w[0])]
    rep_out = _adamw("adamw_small", gall[:, rep0:rep0 + rep_rows], _pack([args[n] for n in rep], LANES),
                     _pack([args["m_" + n] for n in rep], LANES), _pack([args["v_" + n] for n in rep], LANES))
    rep_out = [_unpack(r, [full_shapes[n] for n in rep]) for r in rep_out]
    for i, n in enumerate(rep):
        res[n] = [r[i] for r in rep_out]
    nt_pack = after([_pack([g_norm_tok])], [res[n][1] for n in ("a_re", "conv_w", "ffn_conv_w")])
    nt_all = _all_gather("gather_norm_tok_grad", nt_pack, ["row"])[0].reshape(N_DEV, -1, LANES)
    nt_out = _adamw("adamw_norm_tok", nt_all, _pack([norm_tok]), _pack([m_norm_tok]), _pack([v_norm_tok]))
    res["norm_tok"] = [_unpack(r, [norm_tok.shape])[0] for r in nt_out]
    parts["w_in"] = after(parts_in, nt_out[0])[0]
    big_update("w_in")

    loss = jnp.sum(gall[:, 0, 0])
    names = ["norm_tok", "w_in", "a_re", "a_im", "log_dt", "b_re", "b_im", "c_re", "c_im", "d_skip", "w_glu", "w_ssm_out",
             "conv_w", "conv_b", "w_conv_out", "w_o", "norm_ffn", "w_up", "ffn_conv_w", "ffn_conv_b", "w_down", "norm_final"]
    out = [loss, grad_x[None]]
    for slot in range(4):
        out += [res[n][slot] for n in names]
    return tuple(out)
```
